```python
import math
import jax, jax.numpy as jnp
from jax import lax
import numpy as np

D_MODEL = 2048
BATCH = 8
SEQ = 8192
DEPTH = 2

CHUNK = 64
Q_BLOCK = 2 * CHUNK

N_A = (DEPTH + 1) // 2
N_B = DEPTH - N_A

D_RNN = 5 * D_MODEL // 4
LRU_BLOCKS = 10
LRU_BW = D_RNN // LRU_BLOCKS
CONV_W = 4
LRU_C = 8.0

N_HEADS = 16
HEAD_DIM = 128
D_ATTN = N_HEADS * HEAD_DIM

EPS = 1e-6

kernel_name = "hawk_stickbreak_yoco_trunk"


def _rmsnorm(x, g):
    xf = x.astype(jnp.float32)
    y = xf * lax.rsqrt(jnp.mean(xf * xf, axis=-1, keepdims=True) + EPS)
    return (y * g.astype(jnp.float32)).astype(x.dtype)


def _causal_depthwise_conv(xb, w, b):
    kernel = w[:, None, :].astype(xb.dtype)
    y = lax.conv_general_dilated(
        xb, kernel, window_strides=(1,), padding=[(CONV_W - 1, 0)],
        dimension_numbers=("NWC", "WIO", "NWC"), feature_group_count=xb.shape[-1])
    return y + b.astype(xb.dtype)


def _block_diag(xb, w, b):
    bsz, s, _ = xb.shape
    xr = xb.reshape(bsz, s, LRU_BLOCKS, LRU_BW)
    y = jnp.einsum("bsnc,ncd->bsnd", xr, w.astype(xb.dtype)).reshape(bsz, s, D_RNN)
    return y + b.astype(xb.dtype)


def _rg_lru(xb, w_r, b_r, w_i, b_i, lam):
    r = jax.nn.sigmoid(_block_diag(xb, w_r, b_r).astype(jnp.float32))
    i = jax.nn.sigmoid(_block_diag(xb, w_i, b_i).astype(jnp.float32))
    log_a = -LRU_C * r * jax.nn.softplus(-lam.astype(jnp.float32))
    a = jnp.exp(log_a)
    mult = jnp.sqrt(jnp.maximum(-jnp.expm1(2.0 * log_a), 0.0))
    u = mult * (i * xb.astype(jnp.float32))

    def combine(left, right):
        a1, b1 = left
        a2, b2 = right
        return a1 * a2, a2 * b1 + b2

    _, h = lax.associative_scan(combine, (a, u), axis=1)
    return h.astype(xb.dtype)


def _stick_breaking_attention(q, k, v):
    bsz, s, h, dh = q.shape
    n_blocks = s // Q_BLOCK
    scale = 1.0 / math.sqrt(dh)
    qb = q.reshape(bsz, n_blocks, Q_BLOCK, h, dh).transpose(1, 0, 2, 3, 4)
    starts = jnp.arange(n_blocks, dtype=jnp.int32) * Q_BLOCK
    key_pos = jnp.arange(s, dtype=jnp.int32)

    def one_block(args):
        start, qblk = args
        z = jnp.einsum("bqhd,bkhd->bhqk", qblk, k).astype(jnp.float32) * scale
        q_pos = start + jnp.arange(Q_BLOCK, dtype=jnp.int32)
        causal = key_pos[None, :] < q_pos[:, None]
        log_keep = jnp.where(causal, jax.nn.log_sigmoid(-z), 0.0)
        later = lax.cumsum(log_keep, axis=3, reverse=True) - log_keep
        weights = jnp.where(causal, jnp.exp(jax.nn.log_sigmoid(z) + later), 0.0)
        return jnp.einsum("bhqk,bkhd->bqhd", weights.astype(v.dtype), v)

    out = lax.map(one_block, (starts, qb))
    return out.transpose(1, 0, 2, 3, 4).reshape(bsz, s, h, dh)


def _fwd_setup_inputs(seed: int = 0) -> dict:
    key = jax.random.key(seed)
    ks = jax.random.split(key, 20)

    def nrm(k, shape, scale):
        return jax.random.normal(k, shape, jnp.float32) * scale

    x = nrm(ks[0], (BATCH, SEQ, D_MODEL), 1.0)
    a_norm = 1.0 + nrm(ks[1], (N_A, D_MODEL), 0.01)
    a_w_in = nrm(ks[2], (N_A, D_MODEL, 2 * D_RNN), D_MODEL ** -0.5)
    a_conv_w = nrm(ks[3], (N_A, CONV_W, D_RNN), CONV_W ** -0.5)
    a_conv_b = nrm(ks[4], (N_A, D_RNN), 0.01)
    a_w_r = nrm(ks[5], (N_A, LRU_BLOCKS, LRU_BW, LRU_BW), LRU_BW ** -0.5)
    a_b_r = nrm(ks[6], (N_A, D_RNN), 0.01)
    a_w_i = nrm(ks[7], (N_A, LRU_BLOCKS, LRU_BW, LRU_BW), LRU_BW ** -0.5)
    a_b_i = nrm(ks[8], (N_A, D_RNN), 0.01)
    u = jax.random.uniform(ks[9], (N_A, D_RNN), jnp.float32, minval=0.9, maxval=0.999)
    a_root = u ** (1.0 / LRU_C)
    a_lambda = jnp.log(a_root) - jnp.log1p(-a_root)
    a_w_out = nrm(ks[10], (N_A, D_RNN, D_MODEL), D_RNN ** -0.5)
    kv_norm = 1.0 + nrm(ks[11], (D_MODEL,), 0.01)
    w_kv = nrm(ks[12], (D_MODEL, 2 * D_ATTN), D_MODEL ** -0.5)
    b_norm = 1.0 + nrm(ks[13], (N_B, D_MODEL), 0.01)
    b_w_in = nrm(ks[14], (N_B, D_MODEL, 2 * D_ATTN), D_MODEL ** -0.5)
    b_w_out = nrm(ks[15], (N_B, D_ATTN, D_MODEL), D_ATTN ** -0.5)
    final_norm = 1.0 + nrm(ks[16], (D_MODEL,), 0.01)
    return {
        "x": x, "a_norm": a_norm, "a_w_in": a_w_in, "a_conv_w": a_conv_w,
        "a_conv_b": a_conv_b, "a_w_r": a_w_r, "a_b_r": a_b_r, "a_w_i": a_w_i,
        "a_b_i": a_b_i, "a_lambda": a_lambda, "a_w_out": a_w_out,
        "kv_norm": kv_norm, "w_kv": w_kv, "b_norm": b_norm, "b_w_in": b_w_in,
        "b_w_out": b_w_out, "final_norm": final_norm,
    }


def _fwd_reference(x, a_norm, a_w_in, a_conv_w, a_conv_b, a_w_r, a_b_r, a_w_i, a_b_i,
              a_lambda, a_w_out, kv_norm, w_kv, b_norm, b_w_in, b_w_out, final_norm):
    bsz, s, _ = x.shape
    k = v = None
    for layer in range(DEPTH):
        if layer < N_A:
            h = _rmsnorm(x, a_norm[layer])
            proj = h @ a_w_in[layer].astype(x.dtype)
            xb, gate = proj[..., :D_RNN], proj[..., D_RNN:]
            xb = _causal_depthwise_conv(xb, a_conv_w[layer], a_conv_b[layer])
            y = _rg_lru(xb, a_w_r[layer], a_b_r[layer], a_w_i[layer], a_b_i[layer],
                        a_lambda[layer])
            x = x + (y * jax.nn.silu(gate)) @ a_w_out[layer].astype(x.dtype)
            if layer == N_A - 1:
                kv = _rmsnorm(x, kv_norm) @ w_kv.astype(x.dtype)
                k = kv[..., :D_ATTN].reshape(bsz, s, N_HEADS, HEAD_DIM)
                v = kv[..., D_ATTN:].reshape(bsz, s, N_HEADS, HEAD_DIM)
        else:
            lb = layer - N_A
            h = _rmsnorm(x, b_norm[lb])
            proj = h @ b_w_in[lb].astype(x.dtype)
            q = proj[..., :D_ATTN].reshape(bsz, s, N_HEADS, HEAD_DIM)
            gate = proj[..., D_ATTN:]
            o = _stick_breaking_attention(q, k, v).reshape(bsz, s, D_ATTN)
            x = x + (o * jax.nn.silu(gate)) @ b_w_out[lb].astype(x.dtype)
    return _rmsnorm(x, final_norm)


import jax as _jax
import jax.numpy as _jnp

TWIN_FORMAT = 'train_step'
FWD_PARAMS = ['x', 'a_norm', 'a_w_in', 'a_conv_w', 'a_conv_b', 'a_w_r', 'a_b_r', 'a_w_i', 'a_b_i', 'a_lambda', 'a_w_out', 'kv_norm', 'w_kv', 'b_norm', 'b_w_in', 'b_w_out', 'final_norm']
TWIN_WEIGHTS = ['a_norm', 'a_w_in', 'a_conv_w', 'a_conv_b', 'a_w_r', 'a_b_r', 'a_w_i', 'a_b_i', 'a_lambda', 'a_w_out', 'kv_norm', 'w_kv', 'b_norm', 'b_w_in', 'b_w_out', 'final_norm']
TWIN_DIFF_INPUT = 'x'
TWIN_INPUTS = ['x', 'a_norm', 'a_w_in', 'a_conv_w', 'a_conv_b', 'a_w_r', 'a_b_r', 'a_w_i', 'a_b_i', 'a_lambda', 'a_w_out', 'kv_norm', 'w_kv', 'b_norm', 'b_w_in', 'b_w_out', 'final_norm', 'loss_target', 'm_a_norm', 'm_a_w_in', 'm_a_conv_w', 'm_a_conv_b', 'm_a_w_r', 'm_a_b_r', 'm_a_w_i', 'm_a_b_i', 'm_a_lambda', 'm_a_w_out', 'm_kv_norm', 'm_w_kv', 'm_b_norm', 'm_b_w_in', 'm_b_w_out', 'm_final_norm', 'v_a_norm', 'v_a_w_in', 'v_a_conv_w', 'v_a_conv_b', 'v_a_w_r', 'v_a_b_r', 'v_a_w_i', 'v_a_b_i', 'v_a_lambda', 'v_a_w_out', 'v_kv_norm', 'v_w_kv', 'v_b_norm', 'v_b_w_in', 'v_b_w_out', 'v_final_norm']
TWIN_OUTPUTS = ['loss', 'grad_x', 'grad_a_norm', 'grad_a_w_in', 'grad_a_conv_w', 'grad_a_conv_b', 'grad_a_w_r', 'grad_a_b_r', 'grad_a_w_i', 'grad_a_b_i', 'grad_a_lambda', 'grad_a_w_out', 'grad_kv_norm', 'grad_w_kv', 'grad_b_norm', 'grad_b_w_in', 'grad_b_w_out', 'grad_final_norm', 'delta_a_norm', 'delta_a_w_in', 'delta_a_conv_w', 'delta_a_conv_b', 'delta_a_w_r', 'delta_a_b_r', 'delta_a_w_i', 'delta_a_b_i', 'delta_a_lambda', 'delta_a_w_out', 'delta_kv_norm', 'delta_w_kv', 'delta_b_norm', 'delta_b_w_in', 'delta_b_w_out', 'delta_final_norm', 'new_m_a_norm', 'new_m_a_w_in', 'new_m_a_conv_w', 'new_m_a_conv_b', 'new_m_a_w_r', 'new_m_a_b_r', 'new_m_a_w_i', 'new_m_a_b_i', 'new_m_a_lambda', 'new_m_a_w_out', 'new_m_kv_norm', 'new_m_w_kv', 'new_m_b_norm', 'new_m_b_w_in', 'new_m_b_w_out', 'new_m_final_norm', 'new_v_a_norm', 'new_v_a_w_in', 'new_v_a_conv_w', 'new_v_a_conv_b', 'new_v_a_w_r', 'new_v_a_b_r', 'new_v_a_w_i', 'new_v_a_b_i', 'new_v_a_lambda', 'new_v_a_w_out', 'new_v_kv_norm', 'new_v_w_kv', 'new_v_b_norm', 'new_v_b_w_in', 'new_v_b_w_out', 'new_v_final_norm']
TWIN_LEAF_KINDS = {'loss': 'loss', 'grad_x': 'grad_x', 'grad_a_norm': 'grad_w', 'grad_a_w_in': 'grad_w', 'grad_a_conv_w': 'grad_w', 'grad_a_conv_b': 'grad_w', 'grad_a_w_r': 'grad_w', 'grad_a_b_r': 'grad_w', 'grad_a_w_i': 'grad_w', 'grad_a_b_i': 'grad_w', 'grad_a_lambda': 'grad_w', 'grad_a_w_out': 'grad_w', 'grad_kv_norm': 'grad_w', 'grad_w_kv': 'grad_w', 'grad_b_norm': 'grad_w', 'grad_b_w_in': 'grad_w', 'grad_b_w_out': 'grad_w', 'grad_final_norm': 'grad_w', 'delta_a_norm': 'delta_w', 'delta_a_w_in': 'delta_w', 'delta_a_conv_w': 'delta_w', 'delta_a_conv_b': 'delta_w', 'delta_a_w_r': 'delta_w', 'delta_a_b_r': 'delta_w', 'delta_a_w_i': 'delta_w', 'delta_a_b_i': 'delta_w', 'delta_a_lambda': 'delta_w', 'delta_a_w_out': 'delta_w', 'delta_kv_norm': 'delta_w', 'delta_w_kv': 'delta_w', 'delta_b_norm': 'delta_w', 'delta_b_w_in': 'delta_w', 'delta_b_w_out': 'delta_w', 'delta_final_norm': 'delta_w', 'new_m_a_norm': 'new_m', 'new_m_a_w_in': 'new_m', 'new_m_a_conv_w': 'new_m', 'new_m_a_conv_b': 'new_m', 'new_m_a_w_r': 'new_m', 'new_m_a_b_r': 'new_m', 'new_m_a_w_i': 'new_m', 'new_m_a_b_i': 'new_m', 'new_m_a_lambda': 'new_m', 'new_m_a_w_out': 'new_m', 'new_m_kv_norm': 'new_m', 'new_m_w_kv': 'new_m', 'new_m_b_norm': 'new_m', 'new_m_b_w_in': 'new_m', 'new_m_b_w_out': 'new_m', 'new_m_final_norm': 'new_m', 'new_v_a_norm': 'new_v', 'new_v_a_w_in': 'new_v', 'new_v_a_conv_w': 'new_v', 'new_v_a_conv_b': 'new_v', 'new_v_a_w_r': 'new_v', 'new_v_a_b_r': 'new_v', 'new_v_a_w_i': 'new_v', 'new_v_a_b_i': 'new_v', 'new_v_a_lambda': 'new_v', 'new_v_a_w_out': 'new_v', 'new_v_kv_norm': 'new_v', 'new_v_w_kv': 'new_v', 'new_v_b_norm': 'new_v', 'new_v_b_w_in': 'new_v', 'new_v_b_w_out': 'new_v', 'new_v_final_norm': 'new_v'}


def _forward(args):
    return _fwd_reference(*[args[k] for k in FWD_PARAMS])


def _output_shape():
    def fwd():
        inp = _fwd_setup_inputs(0)
        return _fwd_reference(*[inp[k] for k in FWD_PARAMS])
    out = _jax.eval_shape(fwd)
    return out.shape, out.dtype

N_MICROBATCH = 1
ADAM_LR = 0.001
ADAM_B1 = 0.9
ADAM_B2 = 0.999
ADAM_EPS = 1e-08
ADAM_WD = 0.01
ADAM_STEP = 10
PER_EXAMPLE_BATCH_AXIS = {'x': 0, 'loss_target': 0}
SHARED_INPUTS = []
_WEIGHT_DTYPES = {'a_norm': _jnp.float32, 'a_w_in': _jnp.float32, 'a_conv_w': _jnp.float32, 'a_conv_b': _jnp.float32, 'a_w_r': _jnp.float32, 'a_b_r': _jnp.float32, 'a_w_i': _jnp.float32, 'a_b_i': _jnp.float32, 'a_lambda': _jnp.float32, 'a_w_out': _jnp.float32, 'kv_norm': _jnp.float32, 'w_kv': _jnp.float32, 'b_norm': _jnp.float32, 'b_w_in': _jnp.float32, 'b_w_out': _jnp.float32, 'final_norm': _jnp.float32}
MOMENT_SCALE = {'a_norm': 7.002370e-02, 'a_w_in': 4.425868e-02, 'a_conv_w': 4.553270e-02, 'a_conv_b': 5.876591e-01, 'a_w_r': 1.266902e-02, 'a_b_r': 1.129968e-02, 'a_w_i': 2.233223e-02, 'a_b_i': 1.605023e-02, 'a_lambda': 2.312346e-02, 'a_w_out': 4.918008e-02, 'kv_norm': 4.808862e-02, 'w_kv': 3.349724e-02, 'b_norm': 4.957209e-02, 'b_w_in': 3.446468e-02, 'b_w_out': 4.315539e-02, 'final_norm': 3.197366e+01}


def _to_microbatches(a, axis):
    t = _jnp.moveaxis(a, axis, 0)
    t = t.reshape((N_MICROBATCH, t.shape[0] // N_MICROBATCH) + t.shape[1:])
    return _jnp.moveaxis(t, 1, axis + 1)


def setup_inputs(seed: int = 0) -> dict:
    inp = _fwd_setup_inputs(seed)
    key = _jax.random.fold_in(_jax.random.key(seed), 7919)
    shape, _ = _output_shape()
    out = dict(inp)
    out["loss_target"] = _jax.random.normal(_jax.random.fold_in(key, 0), shape, _jnp.float32)
    for i, name in enumerate(TWIN_WEIGHTS):
        w = inp[name].astype(_jnp.float32)
        if MOMENT_SCALE is None:
            s = _jnp.sqrt(_jnp.mean(_jnp.square(w)) + 1e-30)
        else:
            s = MOMENT_SCALE[name]
        km, kv = _jax.random.split(_jax.random.fold_in(key, i + 1))
        out[name] = w
        out["m_" + name] = s * _jax.random.normal(km, w.shape, _jnp.float32)
        out["v_" + name] = (s * s) * _jax.random.uniform(kv, w.shape, _jnp.float32, 0.5, 1.5)
    if N_MICROBATCH > 1:
        for name, axis in PER_EXAMPLE_BATCH_AXIS.items():
            out[name] = _to_microbatches(out[name], axis)
    return {'x': out['x'], 'a_norm': out['a_norm'], 'a_w_in': out['a_w_in'], 'a_conv_w': out['a_conv_w'], 'a_conv_b': out['a_conv_b'], 'a_w_r': out['a_w_r'], 'a_b_r': out['a_b_r'], 'a_w_i': out['a_w_i'], 'a_b_i': out['a_b_i'], 'a_lambda': out['a_lambda'], 'a_w_out': out['a_w_out'], 'kv_norm': out['kv_norm'], 'w_kv': out['w_kv'], 'b_norm': out['b_norm'], 'b_w_in': out['b_w_in'], 'b_w_out': out['b_w_out'], 'final_norm': out['final_norm'], 'loss_target': out['loss_target'], 'm_a_norm': out['m_a_norm'], 'm_a_w_in': out['m_a_w_in'], 'm_a_conv_w': out['m_a_conv_w'], 'm_a_conv_b': out['m_a_conv_b'], 'm_a_w_r': out['m_a_w_r'], 'm_a_b_r': out['m_a_b_r'], 'm_a_w_i': out['m_a_w_i'], 'm_a_b_i': out['m_a_b_i'], 'm_a_lambda': out['m_a_lambda'], 'm_a_w_out': out['m_a_w_out'], 'm_kv_norm': out['m_kv_norm'], 'm_w_kv': out['m_w_kv'], 'm_b_norm': out['m_b_norm'], 'm_b_w_in': out['m_b_w_in'], 'm_b_w_out': out['m_b_w_out'], 'm_final_norm': out['m_final_norm'], 'v_a_norm': out['v_a_norm'], 'v_a_w_in': out['v_a_w_in'], 'v_a_conv_w': out['v_a_conv_w'], 'v_a_conv_b': out['v_a_conv_b'], 'v_a_w_r': out['v_a_w_r'], 'v_a_b_r': out['v_a_b_r'], 'v_a_w_i': out['v_a_w_i'], 'v_a_b_i': out['v_a_b_i'], 'v_a_lambda': out['v_a_lambda'], 'v_a_w_out': out['v_a_w_out'], 'v_kv_norm': out['v_kv_norm'], 'v_w_kv': out['v_w_kv'], 'v_b_norm': out['v_b_norm'], 'v_b_w_in': out['v_b_w_in'], 'v_b_w_out': out['v_b_w_out'], 'v_final_norm': out['v_final_norm']}


def _loss(weights, diff, rest, loss_target):
    with _jax.named_scope("forward"):
        args = {**rest, TWIN_DIFF_INPUT: diff, **{k: w.astype(_WEIGHT_DTYPES[k]) for k, w in weights.items()}}
        y = _forward(args)
    with _jax.named_scope("loss_head"):
        err = _jnp.square(y.astype(_jnp.float32) - loss_target)
        return 0.5 * _jnp.sum(_jnp.mean(err, axis=-1)) if err.ndim else 0.5 * err


def _adamw(w, g, m, v):
    m = ADAM_B1 * m + (1.0 - ADAM_B1) * g
    v = ADAM_B2 * v + (1.0 - ADAM_B2) * _jnp.square(g)
    m_hat = m / (1.0 - ADAM_B1 ** ADAM_STEP)
    v_hat = v / (1.0 - ADAM_B2 ** ADAM_STEP)
    delta = -ADAM_LR * (m_hat / (_jnp.sqrt(v_hat) + ADAM_EPS) + ADAM_WD * w)
    return delta, m, v


def reference(x, a_norm, a_w_in, a_conv_w, a_conv_b, a_w_r, a_b_r, a_w_i, a_b_i, a_lambda, a_w_out, kv_norm, w_kv, b_norm, b_w_in, b_w_out, final_norm, loss_target, m_a_norm, m_a_w_in, m_a_conv_w, m_a_conv_b, m_a_w_r, m_a_b_r, m_a_w_i, m_a_b_i, m_a_lambda, m_a_w_out, m_kv_norm, m_w_kv, m_b_norm, m_b_w_in, m_b_w_out, m_final_norm, v_a_norm, v_a_w_in, v_a_conv_w, v_a_conv_b, v_a_w_r, v_a_b_r, v_a_w_i, v_a_b_i, v_a_lambda, v_a_w_out, v_kv_norm, v_w_kv, v_b_norm, v_b_w_in, v_b_w_out, v_final_norm):
    given = dict(x=x, a_norm=a_norm, a_w_in=a_w_in, a_conv_w=a_conv_w, a_conv_b=a_conv_b, a_w_r=a_w_r, a_b_r=a_b_r, a_w_i=a_w_i, a_b_i=a_b_i, a_lambda=a_lambda, a_w_out=a_w_out, kv_norm=kv_norm, w_kv=w_kv, b_norm=b_norm, b_w_in=b_w_in, b_w_out=b_w_out, final_norm=final_norm, loss_target=loss_target, m_a_norm=m_a_norm, m_a_w_in=m_a_w_in, m_a_conv_w=m_a_conv_w, m_a_conv_b=m_a_conv_b, m_a_w_r=m_a_w_r, m_a_b_r=m_a_b_r, m_a_w_i=m_a_w_i, m_a_b_i=m_a_b_i, m_a_lambda=m_a_lambda, m_a_w_out=m_a_w_out, m_kv_norm=m_kv_norm, m_w_kv=m_w_kv, m_b_norm=m_b_norm, m_b_w_in=m_b_w_in, m_b_w_out=m_b_w_out, m_final_norm=m_final_norm, v_a_norm=v_a_norm, v_a_w_in=v_a_w_in, v_a_conv_w=v_a_conv_w, v_a_conv_b=v_a_conv_b, v_a_w_r=v_a_w_r, v_a_b_r=v_a_b_r, v_a_w_i=v_a_w_i, v_a_b_i=v_a_b_i, v_a_lambda=v_a_lambda, v_a_w_out=v_a_w_out, v_kv_norm=v_kv_norm, v_w_kv=v_w_kv, v_b_norm=v_b_norm, v_b_w_in=v_b_w_in, v_b_w_out=v_b_w_out, v_final_norm=v_final_norm)
    weights = {n: given[n] for n in TWIN_WEIGHTS}
    shared = {n: given[n] for n in SHARED_INPUTS}
    per_example = {n: given[n] for n in ['x']}
    grad_fn = _jax.value_and_grad(_loss, argnums=(0, 1))

    def one_microbatch(ex, loss_target):
        ex = dict(ex)
        diff = ex.pop(TWIN_DIFF_INPUT)
        return grad_fn(weights, diff, {**shared, **ex}, loss_target)

    if N_MICROBATCH == 1:
        loss, (grad_w, grad_x) = one_microbatch(per_example, given["loss_target"])
    else:
        def body(carry, xs):
            loss_sum, grad_sum = carry
            l_k, (gw_k, gx_k) = one_microbatch(xs[0], xs[1])
            with _jax.named_scope("update"):
                return (loss_sum + l_k, _jax.tree.map(_jnp.add, grad_sum, gw_k)), gx_k

        init = (_jnp.zeros((), _jnp.float32), _jax.tree.map(_jnp.zeros_like, weights))
        (loss, grad_w), grad_x = _jax.lax.scan(body, init, (per_example, given["loss_target"]))
    with _jax.named_scope("update"):
        delta_w, new_m, new_v = {}, {}, {}
        for n in TWIN_WEIGHTS:
            delta_w[n], new_m[n], new_v[n] = _adamw(weights[n], grad_w[n], given["m_" + n], given["v_" + n])
    return (loss, grad_x, *[grad_w[n] for n in TWIN_WEIGHTS], *[delta_w[n] for n in TWIN_WEIGHTS],
            *[new_m[n] for n in TWIN_WEIGHTS], *[new_v[n] for n in TWIN_WEIGHTS])
```

```python
import functools
import math

import jax
import jax.numpy as jnp
from jax import lax
from jax.experimental import pallas as pl
from jax.experimental.pallas import tpu as pltpu

F32 = jnp.float32
BF16 = jnp.bfloat16
MESH = pl.DeviceIdType.MESH

EPS = 1e-6
LRU_C = 8.0
CONV_W = 4
HEAD_DIM = 128
ADAM_LR = 0.001
ADAM_B1 = 0.9
ADAM_B2 = 0.999
ADAM_EPS = 1e-08
ADAM_WD = 0.01
ADAM_STEP = 10

N_DEV = 8
LANES = 128
SUBLANES = 8
VMEM_LIMIT = 56 * 1024 * 1024

ATT_BLOCK = 256
SCAN_BLOCK = 256
ROW_BLOCK = 256
ANY = pl.BlockSpec(memory_space=pl.ANY)


def _pcall(body, **kw):
    return pl.pallas_call(body, **kw)


def _params(*sem):
    return pltpu.CompilerParams(dimension_semantics=sem, vmem_limit_bytes=VMEM_LIMIT)


def _pick(n, cap):
    if n <= cap:
        return n
    best = None
    for t in range(LANES, cap + 1, LANES):
        if n % t == 0:
            best = t
    assert best is not None, (n, cap)
    return best


def _sigmoid(x):
    return 1.0 / (1.0 + jnp.exp(-x))


def _dot(a, b, ca, cb):
    return lax.dot_general(a, b, (((ca,), (cb,)), ((), ())), preferred_element_type=F32)


def _mm_nn(a, b3, *, name, out_dtype, s_off=0, s_cnt=None, res=None):
    T, K = a.shape
    S, K2, n = b3.shape
    assert K == K2
    s_cnt = S if s_cnt is None else s_cnt
    tm = min(T, 512)
    tn = _pick(n, 1024)
    per = n // tn
    has_res = res is not None

    def body(a_ref, b_ref, *rest):
        o_ref = rest[-1]
        acc = jnp.dot(a_ref[...].astype(BF16), b_ref[...], preferred_element_type=F32)
        if has_res:
            acc = acc + rest[0][...]
        o_ref[...] = acc.astype(out_dtype)

    in_specs = [pl.BlockSpec((tm, K), lambda i, j: (i, 0)),
                pl.BlockSpec((None, K, tn), lambda i, j: (s_off + j // per, 0, j % per))]
    args = [a, b3]
    if has_res:
        in_specs.append(pl.BlockSpec((tm, tn), lambda i, j: (i, j)))
        args.append(res)
    return _pcall(
        body, name=name, grid=(T // tm, s_cnt * per), in_specs=in_specs,
        out_specs=pl.BlockSpec((tm, tn), lambda i, j: (i, j)),
        out_shape=jax.ShapeDtypeStruct((T, s_cnt * n), out_dtype),
        compiler_params=_params("parallel", "parallel"))(*args)


def _mm_nt(a, b, *, name, out_dtype=F32):
    T, K = a.shape
    tm = min(T, 512)
    if b.ndim == 2:
        N, K2 = b.shape
        tk = _pick(K, 2048)
        nk = K // tk
        tn = _pick(N, 1024)
        b_spec = pl.BlockSpec((tn, tk), lambda i, j, k: (j, k))
    else:
        S, N, tk = b.shape
        K2 = S * tk
        nk = S
        tn = _pick(N, 1024)
        b_spec = pl.BlockSpec((None, tn, tk), lambda i, j, k: (k, j, 0))
    assert K == K2

    def body(a_ref, b_ref, o_ref, acc_ref):
        k = pl.program_id(2)
        p = _dot(a_ref[...].astype(BF16), b_ref[...], 1, 1)

        @pl.when(k == 0)
        def _():
            acc_ref[...] = p

        @pl.when(k > 0)
        def _():
            acc_ref[...] += p

        @pl.when(k == nk - 1)
        def _():
            o_ref[...] = acc_ref[...].astype(out_dtype)

    return _pcall(
        body, name=name, grid=(T // tm, N // tn, nk),
        in_specs=[pl.BlockSpec((tm, tk), lambda i, j, k: (i, k)), b_spec],
        out_specs=pl.BlockSpec((tm, tn), lambda i, j, k: (i, j)),
        out_shape=jax.ShapeDtypeStruct((T, N), out_dtype),
        scratch_shapes=[pltpu.VMEM((tm, tn), F32)],
        compiler_params=_params("parallel", "parallel", "arbitrary"))(a, b)


def _mm_tn(a, b, *, name, shards=1):
    T, Ko = a.shape
    T2, N = b.shape
    assert T == T2
    n = N // shards
    tt = min(T, 512)
    tko = _pick(Ko, 1024)
    tn = _pick(n, 1024)
    per = n // tn

    def body(a_ref, b_ref, o_ref):
        t = pl.program_id(2)
        p = _dot(a_ref[...].astype(BF16), b_ref[...].astype(BF16), 0, 0)

        @pl.when(t == 0)
        def _():
            o_ref[...] = p

        @pl.when(t > 0)
        def _():
            o_ref[...] += p

    if shards == 1:
        out_spec = pl.BlockSpec((tko, tn), lambda i, j, t: (i, j))
        out_shape = jax.ShapeDtypeStruct((Ko, N), F32)
    else:
        out_spec = pl.BlockSpec((None, tko, tn), lambda i, j, t: (j // per, i, j % per))
        out_shape = jax.ShapeDtypeStruct((shards, Ko, n), F32)
    return _pcall(
        body, name=name, grid=(Ko // tko, N // tn, T // tt),
        in_specs=[pl.BlockSpec((tt, tko), lambda i, j, t: (t, i)),
                  pl.BlockSpec((tt, tn), lambda i, j, t: (t, j))],
        out_specs=out_spec, out_shape=out_shape,
        compiler_params=_params("parallel", "parallel", "arbitrary"))(a, b)


def _rms_fwd(x, gains, *, name):
    T, D = x.shape
    tm = min(T, ROW_BLOCK)
    n = len(gains)

    def body(x_ref, *refs):
        xv = x_ref[...]
        xh = xv * lax.rsqrt(jnp.mean(xv * xv, axis=-1, keepdims=True) + EPS)
        for g_ref, o_ref in zip(refs[:n], refs[n:]):
            o_ref[...] = (xh * g_ref[...]).astype(BF16)

    row = pl.BlockSpec((tm, D), lambda i: (i, 0))
    vec = pl.BlockSpec((1, D), lambda i: (0, 0))
    return _pcall(
        body, name=name, grid=(T // tm,), in_specs=[row] + [vec] * n, out_specs=[row] * n,
        out_shape=[jax.ShapeDtypeStruct((T, D), BF16)] * n,
        compiler_params=_params("parallel"))(x, *gains)


def _rms_bwd(x, dres, dhs, gains, *, name):
    T, D = x.shape
    tm = min(T, ROW_BLOCK)
    n = len(gains)

    def body(x_ref, dres_ref, *refs):
        dh_refs, g_refs = refs[:n], refs[n:2 * n]
        dx_ref, dg_refs = refs[2 * n], refs[2 * n + 1:]
        i = pl.program_id(0)
        xv = x_ref[...]
        r = lax.rsqrt(jnp.mean(xv * xv, axis=-1, keepdims=True) + EPS)
        xh = xv * r
        dxh = jnp.zeros_like(xv)
        for dh_ref, g_ref, dg_ref in zip(dh_refs, g_refs, dg_refs):
            dh = dh_ref[...]
            part = jnp.sum(dh * xh, axis=0, keepdims=True)

            @pl.when(i == 0)
            def _():
                dg_ref[...] = part

            @pl.when(i > 0)
            def _():
                dg_ref[...] += part

            dxh = dxh + dh * g_ref[...]
        dx_ref[...] = dres_ref[...] + r * (dxh - xh * jnp.mean(dxh * xh, axis=-1, keepdims=True))

    row = pl.BlockSpec((tm, D), lambda i: (i, 0))
    vec = pl.BlockSpec((1, D), lambda i: (0, 0))
    return _pcall(
        body, name=name, grid=(T // tm,), in_specs=[row, row] + [row] * n + [vec] * n,
        out_specs=[row] + [vec] * n,
        out_shape=[jax.ShapeDtypeStruct((T, D), F32)] + [jax.ShapeDtypeStruct((1, D), F32)] * n,
        compiler_params=_params("arbitrary"))(x, dres, *dhs, *gains)


def _final_loss(x, target, gain, *, name):
    T, D = x.shape
    tm = min(T, ROW_BLOCK)

    def body(x_ref, t_ref, g_ref, dx_ref, dg_ref, loss_ref):
        i = pl.program_id(0)
        xv = x_ref[...]
        g = g_ref[...]
        r = lax.rsqrt(jnp.mean(xv * xv, axis=-1, keepdims=True) + EPS)
        xh = xv * r
        err = xh * g - t_ref[...]
        part_loss = 0.5 * jnp.sum(jnp.mean(err * err, axis=-1, keepdims=True), axis=0, keepdims=True)
        dy = err * (1.0 / D)
        part_g = jnp.sum(dy * xh, axis=0, keepdims=True)

        @pl.when(i == 0)
        def _():
            dg_ref[...] = part_g
            loss_ref[...] = jnp.broadcast_to(part_loss, loss_ref.shape)

        @pl.when(i > 0)
        def _():
            dg_ref[...] += part_g
            loss_ref[...] += jnp.broadcast_to(part_loss, loss_ref.shape)

        dxh = dy * g
        dx_ref[...] = r * (dxh - xh * jnp.mean(dxh * xh, axis=-1, keepdims=True))

    row = pl.BlockSpec((tm, D), lambda i: (i, 0))
    vec = pl.BlockSpec((1, D), lambda i: (0, 0))
    return _pcall(
        body, name=name, grid=(T // tm,), in_specs=[row, row, vec],
        out_specs=[row, vec, pl.BlockSpec((1, LANES), lambda i: (0, 0))],
        out_shape=[jax.ShapeDtypeStruct((T, D), F32), jax.ShapeDtypeStruct((1, D), F32),
                   jax.ShapeDtypeStruct((1, LANES), F32)],
        compiler_params=_params("arbitrary"))(x, target, gain)


def _shift_down(x, prev_tail, j, row):
    tb = x.shape[0]
    prev = jnp.tile(prev_tail, (tb // SUBLANES, 1))
    return jnp.where(row >= j, pltpu.roll(x, j, 0), pltpu.roll(prev, j, 0))


def _shift_up(x, next_head, j, row):
    tb = x.shape[0]
    nxt = jnp.tile(next_head, (tb // SUBLANES, 1))
    return jnp.where(row < tb - j, pltpu.roll(x, tb - j, 0), pltpu.roll(nxt, tb - j, 0))


def _lru_gates(xb, wr, wi, br, bi, lam):
    xbb = xb.astype(BF16)
    r = _sigmoid(jnp.dot(xbb, wr, preferred_element_type=F32) + br)
    i = _sigmoid(jnp.dot(xbb, wi, preferred_element_type=F32) + bi)
    sp = jnp.maximum(-lam, 0.0) + jnp.log1p(jnp.exp(-jnp.abs(lam)))
    log_a = (-LRU_C) * r * sp
    a = jnp.exp(log_a)
    a2 = a * a
    mult = jnp.sqrt(jnp.maximum(-jnp.tanh(log_a) * (1.0 + a2), 0.0))
    return xbb, r, i, sp, a, a2, mult


def _acore_fwd(proj, conv_w, conv_b, w_r, w_i, b_r, b_i, lam, *, name):
    T, C2 = proj.shape
    C = C2 // 2
    nb, bw, _ = w_r.shape
    tb = min(T, SCAN_BLOCK)

    def body(xp_ref, gate_ref, cw_ref, cb_ref, wr_ref, wi_ref, br_ref, bi_ref, lam_ref,
             xb_ref, h_ref, yg_ref, tail_ref, hlast_ref):
        t = pl.program_id(1)

        @pl.when(t == 0)
        def _():
            tail_ref[...] = jnp.zeros_like(tail_ref)
            hlast_ref[...] = jnp.zeros_like(hlast_ref)

        row = lax.broadcasted_iota(jnp.int32, (tb, bw), 0)
        xp = xp_ref[...]
        tail = tail_ref[...]
        xb = cb_ref[...] + cw_ref[CONV_W - 1:CONV_W, :] * xp
        for j in range(1, CONV_W):
            xb = xb + cw_ref[CONV_W - 1 - j:CONV_W - j, :] * _shift_down(xp, tail, j, row)
        tail_ref[...] = xp[tb - SUBLANES:, :]
        xb_ref[...] = xb

        _, r, i, sp, a, a2, mult = _lru_gates(xb, wr_ref[...], wi_ref[...], br_ref[...], bi_ref[...],
                                              lam_ref[...])
        ca, cb = a, mult * (i * xb)
        s = 1
        while s < tb:
            m = row >= s
            cb = jnp.where(m, ca * pltpu.roll(cb, s, 0) + cb, cb)
            ca = jnp.where(m, ca * pltpu.roll(ca, s, 0), ca)
            s *= 2
        h = cb + ca * hlast_ref[SUBLANES - 1:SUBLANES, :]
        hlast_ref[...] = h[tb - SUBLANES:, :]
        h_ref[...] = h
        gate = gate_ref[...]
        yg_ref[...] = (h * (gate * _sigmoid(gate))).astype(BF16)

    blk = lambda off: pl.BlockSpec((tb, bw), lambda n, t: (t, off + n))
    vec = pl.BlockSpec((1, bw), lambda n, t: (0, n))
    wspec = pl.BlockSpec((None, bw, bw), lambda n, t: (n, 0, 0))
    return _pcall(
        body, name=name, grid=(nb, T // tb),
        in_specs=[blk(0), blk(nb), pl.BlockSpec((CONV_W, bw), lambda n, t: (0, n)), vec, wspec, wspec,
                  vec, vec, vec],
        out_specs=[blk(0), blk(0), blk(0)],
        out_shape=[jax.ShapeDtypeStruct((T, C), F32), jax.ShapeDtypeStruct((T, C), F32),
                   jax.ShapeDtypeStruct((T, C), BF16)],
        scratch_shapes=[pltpu.VMEM((SUBLANES, bw), F32), pltpu.VMEM((SUBLANES, bw), F32)],
        compiler_params=_params("parallel", "arbitrary"))(
            proj, proj, conv_w, conv_b, w_r, w_i, b_r, b_i, lam)


def _acore_bwd(dyg, proj, xb_all, h_all, conv_w, w_r, w_i, b_r, b_i, lam, *, name):
    T, C2 = proj.shape
    C = C2 // 2
    nb, bw, _ = w_r.shape
    tb = min(T, SCAN_BLOCK)
    nt = T // tb
    per8 = tb // SUBLANES

    def body(dyg_ref, xp_ref, gate_ref, xb_ref, h_ref, xp_prev_ref, h_prev_ref, cw_ref,
             wr_ref, wi_ref, br_ref, bi_ref, lam_ref,
             dxp_ref, dgate_ref, dcw_ref, dcb_ref, dbr_ref, dbi_ref, dlam_ref, dwr_ref, dwi_ref,
             gh_next_ref, a_next_ref, dxb_next_ref):
        step = pl.program_id(1)
        first_block = step == nt - 1

        @pl.when(step == 0)
        def _():
            gh_next_ref[...] = jnp.zeros_like(gh_next_ref)
            a_next_ref[...] = jnp.zeros_like(a_next_ref)
            dxb_next_ref[...] = jnp.zeros_like(dxb_next_ref)

        row = lax.broadcasted_iota(jnp.int32, (tb, bw), 0)
        keep = jnp.where(first_block, 0.0, 1.0)
        h_prev = h_prev_ref[...] * keep
        xp_prev = xp_prev_ref[...] * keep
        xp, gate, xb, h, dyg_v = xp_ref[...], gate_ref[...], xb_ref[...], h_ref[...], dyg_ref[...]
        lam_v = lam_ref[...]
        wr, wi = wr_ref[...], wi_ref[...]

        sg = _sigmoid(gate)
        dh = dyg_v * (gate * sg)
        dgate_ref[...] = (dyg_v * h * (sg * (1.0 + gate * (1.0 - sg)))).astype(BF16)

        xbb, r, i, sp, a, a2, mult = _lru_gates(xb, wr, wi, br_ref[...], bi_ref[...], lam_v)

        cg = dh
        cc = _shift_up(a, a_next_ref[...], 1, row)
        s = 1
        while s < tb:
            m = row < tb - s
            cg = jnp.where(m, cc * pltpu.roll(cg, tb - s, 0) + cg, cg)
            cc = jnp.where(m, cc * pltpu.roll(cc, tb - s, 0), cc)
            s *= 2
        gh = cg + cc * gh_next_ref[0:1, :]
        gh_next_ref[...] = gh[0:SUBLANES, :]
        a_next_ref[...] = a[0:SUBLANES, :]

        da = gh * _shift_down(h, h_prev, 1, row)
        dmult = gh * (i * xb)
        di = gh * mult * xb
        dxb = gh * mult * i
        dla = da * a - dmult * jnp.where(mult > 0.0, a2 / mult, 0.0)
        dr = dla * ((-LRU_C) * sp)
        dsp = jnp.sum(dla * ((-LRU_C) * r), axis=0, keepdims=True)
        dlam_part = dsp * (-_sigmoid(-lam_v))
        dpr = dr * r * (1.0 - r)
        dpi = di * i * (1.0 - i)
        dbr_part = jnp.sum(dpr, axis=0, keepdims=True)
        dbi_part = jnp.sum(dpi, axis=0, keepdims=True)
        dprb, dpib = dpr.astype(BF16), dpi.astype(BF16)
        dwr_part = _dot(xbb, dprb, 0, 0)
        dwi_part = _dot(xbb, dpib, 0, 0)
        dxb = dxb + _dot(dprb, wr, 1, 1) + _dot(dpib, wi, 1, 1)

        dxb_next = dxb_next_ref[...]
        dxp = cw_ref[CONV_W - 1:CONV_W, :] * dxb
        for j in range(1, CONV_W):
            dxp = dxp + cw_ref[CONV_W - 1 - j:CONV_W - j, :] * _shift_up(dxb, dxb_next, j, row)
        dxb_next_ref[...] = dxb[0:SUBLANES, :]
        dxp_ref[...] = dxp.astype(BF16)
        dcb_part = jnp.sum(dxb, axis=0, keepdims=True)
        dcw_rows = []
        for k in range(CONV_W):
            j = CONV_W - 1 - k
            sh = xp if j == 0 else _shift_down(xp, xp_prev, j, row)
            dcw_rows.append(jnp.sum(dxb * sh, axis=0, keepdims=True))

        @pl.when(step == 0)
        def _():
            for k in range(CONV_W):
                dcw_ref[k:k + 1, :] = dcw_rows[k]
            dcb_ref[...] = dcb_part
            dbr_ref[...] = dbr_part
            dbi_ref[...] = dbi_part
            dlam_ref[...] = dlam_part
            dwr_ref[...] = dwr_part
            dwi_ref[...] = dwi_part

        @pl.when(step > 0)
        def _():
            for k in range(CONV_W):
                dcw_ref[k:k + 1, :] += dcw_rows[k]
            dcb_ref[...] += dcb_part
            dbr_ref[...] += dbr_part
            dbi_ref[...] += dbi_part
            dlam_ref[...] += dlam_part
            dwr_ref[...] += dwr_part
            dwi_ref[...] += dwi_part

    rev = lambda s: nt - 1 - s
    blk = lambda off: pl.BlockSpec((tb, bw), lambda n, s: (rev(s), off + n))
    prev8 = lambda off: pl.BlockSpec(
        (SUBLANES, bw), lambda n, s: (jnp.maximum(rev(s) * per8 - 1, 0), off + n))
    vec = pl.BlockSpec((1, bw), lambda n, s: (0, n))
    wspec = pl.BlockSpec((None, bw, bw), lambda n, s: (n, 0, 0))
    cwspec = pl.BlockSpec((CONV_W, bw), lambda n, s: (0, n))
    vshape = jax.ShapeDtypeStruct((1, C), F32)
    wshape = jax.ShapeDtypeStruct((nb, bw, bw), F32)
    return _pcall(
        body, name=name, grid=(nb, nt),
        in_specs=[blk(0), blk(0), blk(nb), blk(0), blk(0), prev8(0), prev8(0), cwspec,
                  wspec, wspec, vec, vec, vec],
        out_specs=[blk(0), blk(0), cwspec, vec, vec, vec, vec, wspec, wspec],
        out_shape=[jax.ShapeDtypeStruct((T, C), BF16), jax.ShapeDtypeStruct((T, C), BF16),
                   jax.ShapeDtypeStruct((CONV_W, C), F32), vshape, vshape, vshape, vshape,
                   wshape, wshape],
        scratch_shapes=[pltpu.VMEM((SUBLANES, bw), F32)] * 3,
        compiler_params=_params("parallel", "arbitrary"))(
            dyg, proj, proj, xb_all, h_all, proj, h_all, conv_w, w_r, w_i, b_r, b_i, lam)


def _split_cumsum(lk, tri):
    hi = lk.astype(BF16)
    lo = (lk - hi.astype(F32)).astype(BF16)
    return (jnp.dot(hi, tri, preferred_element_type=F32) + jnp.dot(lo, tri, preferred_element_type=F32))


def _log_sigmoids(z):
    t = jnp.log1p(jnp.exp(-jnp.abs(z)))
    return jnp.minimum(z, 0.0) - t, -jnp.maximum(z, 0.0) - t


def _attn_fwd(q, kv, gate, *, name):
    T, HD = q.shape
    H = HD // HEAD_DIM
    bq = min(T, ATT_BLOCK)
    scale = 1.0 / math.sqrt(HEAD_DIM)

    def body(q_ref, k_ref, v_ref, g_ref, o_ref, og_ref, lt_ref):
        i = pl.program_id(1)
        qv = q_ref[...]
        rows = lax.broadcasted_iota(jnp.int32, (bq, bq), 0)
        cols = lax.broadcasted_iota(jnp.int32, (bq, bq), 1)
        tri = (rows > cols).astype(BF16)
        diag = cols < rows

        def block(j, acc, c, mask):
            ks = pl.multiple_of(j * bq, bq)
            kb = k_ref[pl.ds(ks, bq), :]
            vb = v_ref[pl.ds(ks, bq), :]
            z = _dot(qv, kb, 1, 1) * scale
            ls, lk = _log_sigmoids(z)
            if mask:
                lk = jnp.where(diag, lk, 0.0)
            later = _split_cumsum(lk, tri) + c
            w = jnp.exp(ls + later)
            if mask:
                w = jnp.where(diag, w, 0.0)
            acc = acc + jnp.dot(w.astype(BF16), vb, preferred_element_type=F32)
            return acc, c + jnp.sum(lk, axis=1, keepdims=True)

        acc, c = block(i, jnp.zeros((bq, HEAD_DIM), F32), jnp.zeros((bq, 1), F32), True)
        acc, c = lax.fori_loop(0, i, lambda jj, s: block(i - 1 - jj, s[0], s[1], False), (acc, c))
        o_ref[...] = acc
        g = g_ref[...]
        og_ref[...] = (acc * (g * _sigmoid(g))).astype(BF16)
        lt_ref[...] = jnp.broadcast_to(c, (bq, HEAD_DIM))

    qspec = pl.BlockSpec((bq, HEAD_DIM), lambda h, i: (i, h))
    return _pcall(
        body, name=name, grid=(H, T // bq),
        in_specs=[qspec, pl.BlockSpec((T, HEAD_DIM), lambda h, i: (0, h)),
                  pl.BlockSpec((T, HEAD_DIM), lambda h, i: (0, H + h)), qspec],
        out_specs=[qspec, qspec, qspec],
        out_shape=[jax.ShapeDtypeStruct((T, HD), F32), jax.ShapeDtypeStruct((T, HD), BF16),
                   jax.ShapeDtypeStruct((T, HD), F32)],
        compiler_params=_params("parallel", "arbitrary"))(q, kv, kv, gate)


def _attn_bwd(q, kv, gate, o, ltot, dog, *, name):
    T, HD = q.shape
    H = HD // HEAD_DIM
    bq = min(T, ATT_BLOCK)
    nq = T // bq
    scale = 1.0 / math.sqrt(HEAD_DIM)

    def body(q_ref, k_ref, v_ref, g_ref, o_ref, lt_ref, dog_ref,
             dq_ref, dg_ref, dk_ref, dv_ref, dk_acc, dv_acc):
        i = pl.program_id(1)

        @pl.when(i == 0)
        def _():
            dk_acc[...] = jnp.zeros_like(dk_acc)
            dv_acc[...] = jnp.zeros_like(dv_acc)

        qv = q_ref[...]
        g, ov, dogv = g_ref[...], o_ref[...], dog_ref[...]
        sg = _sigmoid(g)
        do = dogv * (g * sg)
        dg_ref[...] = (dogv * ov * (sg * (1.0 + g * (1.0 - sg)))).astype(BF16)
        dob = do.astype(BF16)
        ltot_v = lt_ref[:, 0:1]
        rows = lax.broadcasted_iota(jnp.int32, (bq, bq), 0)
        cols = lax.broadcasted_iota(jnp.int32, (bq, bq), 1)
        tri_incl = (rows <= cols).astype(BF16)
        tri_excl = (rows < cols).astype(BF16)
        diag = cols < rows

        def block(j, dq, p_lk, p_g, mask):
            ks = pl.multiple_of(j * bq, bq)
            kb = k_ref[pl.ds(ks, bq), :]
            vb = v_ref[pl.ds(ks, bq), :]
            z = _dot(qv, kb, 1, 1) * scale
            ls, lk = _log_sigmoids(z)
            if mask:
                lk = jnp.where(diag, lk, 0.0)
            later = (ltot_v - p_lk) - _split_cumsum(lk, tri_incl)
            w = jnp.exp(ls + later)
            if mask:
                w = jnp.where(diag, w, 0.0)
            gm = _dot(dob, vb, 1, 1) * w
            before = jnp.dot(gm.astype(BF16), tri_excl, preferred_element_type=F32) + p_g
            sig = jnp.exp(ls)
            dz = gm * (1.0 - sig) - before * sig
            if mask:
                dz = jnp.where(diag, dz, 0.0)
            dzb = (dz * scale).astype(BF16)
            dq = dq + jnp.dot(dzb, kb, preferred_element_type=F32)
            dk_acc[pl.ds(ks, bq), :] += _dot(dzb, qv, 0, 0)
            dv_acc[pl.ds(ks, bq), :] += _dot(w.astype(BF16), dob, 0, 0)
            return (dq, p_lk + jnp.sum(lk, axis=1, keepdims=True),
                    p_g + jnp.sum(gm, axis=1, keepdims=True))

        zero = jnp.zeros((bq, 1), F32)
        state = lax.fori_loop(0, i, lambda j, s: block(j, s[0], s[1], s[2], False),
                              (jnp.zeros((bq, HEAD_DIM), F32), zero, zero))
        dq, _, _ = block(i, state[0], state[1], state[2], True)
        dq_ref[...] = dq.astype(BF16)

        @pl.when(i == nq - 1)
        def _():
            dk_ref[...] = dk_acc[...].astype(BF16)
            dv_ref[...] = dv_acc[...].astype(BF16)

    qspec = pl.BlockSpec((bq, HEAD_DIM), lambda h, i: (i, h))
    kspec = pl.BlockSpec((T, HEAD_DIM), lambda h, i: (0, h))
    return _pcall(
        body, name=name, grid=(H, nq),
        in_specs=[qspec, kspec, pl.BlockSpec((T, HEAD_DIM), lambda h, i: (0, H + h)),
                  qspec, qspec, qspec, qspec],
        out_specs=[qspec, qspec, kspec, kspec],
        out_shape=[jax.ShapeDtypeStruct((T, HD), BF16)] * 4,
        scratch_shapes=[pltpu.VMEM((T, HEAD_DIM), F32)] * 2,
        compiler_params=_params("parallel", "arbitrary"))(q, kv, kv, gate, o, ltot, dog)


def _position():
    return lax.axis_index("x"), lax.axis_index("y"), lax.axis_index("c")


def _chip_of(k, x, y):
    return (1 - x if k & 1 else x), (1 - y if k & 2 else y)


def _weights_gather(shards):
    n = len(shards)

    def body(*refs):
        ins, outs = refs[:n], refs[n:2 * n]
        send_sems, recv_sems, local_sems = refs[2 * n:]
        x, y, c = _position()
        sibling = (x, y, 1 - c)
        chips = [_chip_of(k, x, y) for k in (1, 2, 3)]

        def copy(a, k, block, to, src=None):
            slot = outs[a].at[4 * block[0] + 2 * block[1] + block[2]]
            return pltpu.make_async_remote_copy(
                src_ref=slot if src is None else src, dst_ref=slot,
                send_sem=send_sems.at[a, k], recv_sem=recv_sems.at[a, k],
                device_id=to, device_id_type=MESH)

        mine = [pltpu.make_async_copy(ins[a], outs[a].at[4 * x + 2 * y + c], local_sems.at[a])
                for a in range(n)]
        for cp in mine:
            cp.start()
        first = []
        for a in range(n):
            first.append(copy(a, 0, (x, y, c), sibling, src=ins[a]))
            first += [copy(a, 1 + j, (x, y, c), (*chip, c), src=ins[a]) for j, chip in enumerate(chips)]
        for cp in first:
            cp.start()
        passed = []
        for j, chip in enumerate(chips):
            for a in range(n):
                copy(a, 1 + j, (*chip, c), (x, y, c)).wait_recv()
                fwd = copy(a, 4 + j, (*chip, c), sibling)
                fwd.start()
                passed.append(fwd)
        for a in range(n):
            copy(a, 0, sibling, (x, y, c)).wait_recv()
            for j, chip in enumerate(chips):
                copy(a, 4 + j, (*chip, 1 - c), (x, y, c)).wait_recv()
        for cp in first + passed:
            cp.wait_send()
        for cp in mine:
            cp.wait()

    return _pcall(
        body, name="weights_gather", in_specs=[ANY] * n, out_specs=[ANY] * n,
        out_shape=[jax.ShapeDtypeStruct((N_DEV,) + s.shape, s.dtype) for s in shards],
        scratch_shapes=[pltpu.SemaphoreType.DMA((n, 7)), pltpu.SemaphoreType.DMA((n, 7)),
                        pltpu.SemaphoreType.DMA((n,))])(*shards)


def _grads_to_sibling(grads, small):
    n = len(grads)

    def body(*refs):
        g_refs, small_ref = refs[:n], refs[n]
        kept, got, small_all = refs[n + 1:2 * n + 1], refs[2 * n + 1:3 * n + 1], refs[3 * n + 1]
        send_sems, recv_sems, local_sems, small_send, small_recv, small_local = refs[3 * n + 2:]
        x, y, c = _position()
        me = 4 * x + 2 * y + c
        remote, local = [], []
        for a in range(n):
            for k in range(4):
                cx, cy = _chip_of(k, x, y)
                remote.append(pltpu.make_async_remote_copy(
                    src_ref=g_refs[a].at[4 * cx + 2 * cy + (1 - c)], dst_ref=got[a].at[k],
                    send_sem=send_sems.at[a, k], recv_sem=recv_sems.at[a, k],
                    device_id=(x, y, 1 - c), device_id_type=MESH))
                local.append(pltpu.make_async_copy(
                    g_refs[a].at[4 * cx + 2 * cy + c], kept[a].at[k], local_sems.at[a, k]))
        peers = []
        for m in range(1, N_DEV):
            px, py, pc = x ^ (m >> 2), y ^ ((m >> 1) & 1), c ^ (m & 1)
            peers.append(pltpu.make_async_remote_copy(
                src_ref=small_ref, dst_ref=small_all.at[me],
                send_sem=small_send.at[m - 1], recv_sem=small_recv.at[m - 1],
                device_id=(px, py, pc), device_id_type=MESH))
        own = pltpu.make_async_copy(small_ref, small_all.at[me], small_local)
        for cp in peers + remote + local + [own]:
            cp.start()
        for cp in peers + remote:
            cp.wait_send()
        for m in range(1, N_DEV):
            px, py, pc = x ^ (m >> 2), y ^ ((m >> 1) & 1), c ^ (m & 1)
            pltpu.make_async_remote_copy(
                src_ref=small_ref, dst_ref=small_all.at[4 * px + 2 * py + pc],
                send_sem=small_send.at[m - 1], recv_sem=small_recv.at[m - 1],
                device_id=(px, py, pc), device_id_type=MESH).wait_recv()
        for cp in remote:
            cp.wait_recv()
        for cp in local + [own]:
            cp.wait()

    part = [jax.ShapeDtypeStruct((4,) + g.shape[1:], g.dtype) for g in grads]
    return _pcall(
        body, name="grads_to_sibling", in_specs=[ANY] * (n + 1), out_specs=[ANY] * (2 * n + 1),
        out_shape=part + part + [jax.ShapeDtypeStruct((N_DEV,) + small.shape, small.dtype)],
        scratch_shapes=[pltpu.SemaphoreType.DMA((n, 4)), pltpu.SemaphoreType.DMA((n, 4)),
                        pltpu.SemaphoreType.DMA((n, 4)), pltpu.SemaphoreType.DMA((7,)),
                        pltpu.SemaphoreType.DMA((7,)), pltpu.SemaphoreType.DMA])(*grads, small)


def _grads_to_chips(parts):
    n = len(parts)

    def body(*refs):
        p_refs, outs = refs[:n], refs[n:2 * n]
        send_sems, recv_sems = refs[2 * n:]
        x, y, c = _position()
        copies = []
        for a in range(n):
            for k in range(3):
                copies.append(pltpu.make_async_remote_copy(
                    src_ref=p_refs[a].at[k], dst_ref=outs[a].at[k],
                    send_sem=send_sems.at[a, k], recv_sem=recv_sems.at[a, k],
                    device_id=(*_chip_of(k + 1, x, y), c), device_id_type=MESH))
        for cp in copies:
            cp.start()
        for cp in copies:
            cp.wait_send()
        for cp in copies:
            cp.wait_recv()

    return _pcall(
        body, name="grads_to_chips", in_specs=[ANY] * n, out_specs=[ANY] * n,
        out_shape=[jax.ShapeDtypeStruct(p.shape, p.dtype) for p in parts],
        scratch_shapes=[pltpu.SemaphoreType.DMA((n, 3)), pltpu.SemaphoreType.DMA((n, 3))])(*parts)


def _pair_sum(kept, got, *, name):
    _, R, C = kept.shape
    tr = _pick8(R, max(SUBLANES, (1 << 19) // C))

    def body(a_ref, b_ref, own_ref, oth_ref):
        own_ref[...] = a_ref[0] + b_ref[0]
        for k in range(3):
            oth_ref[k] = a_ref[k + 1] + b_ref[k + 1]

    spec4 = pl.BlockSpec((4, tr, C), lambda i: (0, i, 0))
    return _pcall(
        body, name=name, grid=(R // tr,), in_specs=[spec4, spec4],
        out_specs=[pl.BlockSpec((tr, C), lambda i: (i, 0)), pl.BlockSpec((3, tr, C), lambda i: (0, i, 0))],
        out_shape=[jax.ShapeDtypeStruct((R, C), F32), jax.ShapeDtypeStruct((3, R, C), F32)],
        compiler_params=_params("parallel"))(kept, got)


def _pick8(n, cap):
    if n <= cap:
        return n
    best = None
    for t in range(SUBLANES, cap + 1, SUBLANES):
        if n % t == 0:
            best = t
    assert best is not None, (n, cap)
    return best


def _adamw(w, m, v, parts, *, name):
    R, C = w.shape
    tr = _pick8(R, max(SUBLANES, (1 << 18) // C))
    c1 = 1.0 - ADAM_B1 ** ADAM_STEP
    c2 = 1.0 - ADAM_B2 ** ADAM_STEP
    np_ = len(parts)

    def body(w_ref, m_ref, v_ref, *refs):
        p_refs = refs[:np_]
        g_ref, d_ref, nm_ref, nv_ref = refs[np_:]
        g = None
        for p_ref in p_refs:
            terms = [p_ref[...]] if len(p_ref.shape) == 2 else [p_ref[k] for k in range(p_ref.shape[0])]
            for t in terms:
                g = t if g is None else g + t
        mn = ADAM_B1 * m_ref[...] + (1.0 - ADAM_B1) * g
        vn = ADAM_B2 * v_ref[...] + (1.0 - ADAM_B2) * (g * g)
        d_ref[...] = -ADAM_LR * ((mn / c1) / (jnp.sqrt(vn / c2) + ADAM_EPS) + ADAM_WD * w_ref[...])
        g_ref[...] = g
        nm_ref[...] = mn
        nv_ref[...] = vn

    spec = pl.BlockSpec((tr, C), lambda i: (i, 0))
    pspecs = [spec if p.ndim == 2 else pl.BlockSpec((p.shape[0], tr, C), lambda i: (0, i, 0))
              for p in parts]
    return _pcall(
        body, name=name, grid=(R // tr,), in_specs=[spec] * 3 + pspecs, out_specs=[spec] * 4,
        out_shape=[jax.ShapeDtypeStruct((R, C), F32)] * 4,
        compiler_params=_params("parallel"))(w, m, v, *parts)


def _rows(a):
    return a.reshape(-1, LANES)


def _forward_backward(xs, target, a_norm, g_a_w_in, conv_w, conv_b, g_w_r, g_w_i, b_r, b_i, lam,
                      g_a_w_out, kv_norm, g_w_kv, b_norm, g_b_w_in, g_b_w_out, final_norm):
    (h_a,) = _rms_fwd(xs, [a_norm], name="a_norm_fwd")
    proj_a = _mm_nn(h_a, g_a_w_in, name="a_in_proj", out_dtype=F32)
    xb, h_rec, yg = _acore_fwd(proj_a, conv_w, conv_b, g_w_r, g_w_i, b_r, b_i, lam, name="a_core_fwd")
    x1 = _mm_nn(yg, g_a_w_out[None], name="a_out_proj", out_dtype=F32, res=xs)
    hk, hb = _rms_fwd(x1, [kv_norm, b_norm], name="kv_b_norm_fwd")
    kv = _mm_nn(hk, g_w_kv, name="kv_proj", out_dtype=BF16)
    half = N_DEV // 2
    q = _mm_nn(hb, g_b_w_in, name="q_proj", out_dtype=BF16, s_off=0, s_cnt=half)
    gate_b = _mm_nn(hb, g_b_w_in, name="b_gate_proj", out_dtype=F32, s_off=half, s_cnt=half)
    o, og, ltot = _attn_fwd(q, kv, gate_b, name="attn_fwd")
    x2 = _mm_nn(og, g_b_w_out[None], name="b_out_proj", out_dtype=F32, res=x1)
    dx2, d_final_norm, loss_part = _final_loss(x2, target, final_norm, name="final_norm_loss")

    dog = _mm_nt(dx2, g_b_w_out, name="b_out_proj_bwd")
    dw_b_out = _mm_tn(og, dx2, name="b_out_proj_wgrad")
    dq, dgate_b, dk, dv = _attn_bwd(q, kv, gate_b, o, ltot, dog, name="attn_bwd")
    dproj_b = jnp.concatenate([dq, dgate_b], axis=1)
    dkv = jnp.concatenate([dk, dv], axis=1)
    dhb = _mm_nt(dproj_b, g_b_w_in, name="b_in_proj_bwd")
    dw_b_in = _mm_tn(hb, dproj_b, name="b_in_proj_wgrad", shards=N_DEV)
    dhk = _mm_nt(dkv, g_w_kv, name="kv_proj_bwd")
    dw_kv = _mm_tn(hk, dkv, name="kv_proj_wgrad", shards=N_DEV)
    dx1, d_b_norm, d_kv_norm = _rms_bwd(x1, dx2, [dhb, dhk], [b_norm, kv_norm], name="kv_b_norm_bwd")
    dyg = _mm_nt(dx1, g_a_w_out, name="a_out_proj_bwd")
    dw_a_out = _mm_tn(yg, dx1, name="a_out_proj_wgrad")
    (dxp, dgate_a, d_conv_w, d_conv_b, d_b_r, d_b_i, d_lambda, dw_r, dw_i) = _acore_bwd(
        dyg, proj_a, xb, h_rec, conv_w, g_w_r, g_w_i, b_r, b_i, lam, name="a_core_bwd")
    dproj_a = jnp.concatenate([dxp, dgate_a], axis=1)
    dh_a = _mm_nt(dproj_a, g_a_w_in, name="a_in_proj_bwd")
    dw_a_in = _mm_tn(h_a, dproj_a, name="a_in_proj_wgrad", shards=N_DEV)
    grad_x, d_a_norm = _rms_bwd(xs, dx1, [dh_a], [a_norm], name="a_norm_bwd")
    return (loss_part, grad_x, dw_a_in, dw_a_out, dw_kv, dw_b_in, dw_b_out, dw_r, dw_i, d_a_norm,
            d_conv_w, d_conv_b, d_b_r, d_b_i, d_lambda, d_kv_norm, d_b_norm, d_final_norm)


def kernel(x, a_norm, a_w_in, a_conv_w, a_conv_b, a_w_r, a_b_r, a_w_i, a_b_i, a_lambda, a_w_out, kv_norm, w_kv, b_norm, b_w_in, b_w_out, final_norm, loss_target, m_a_norm, m_a_w_in, m_a_conv_w, m_a_conv_b, m_a_w_r, m_a_b_r, m_a_w_i, m_a_b_i, m_a_lambda, m_a_w_out, m_kv_norm, m_w_kv, m_b_norm, m_b_w_in, m_b_w_out, m_final_norm, v_a_norm, v_a_w_in, v_a_conv_w, v_a_conv_b, v_a_w_r, v_a_b_r, v_a_w_i, v_a_b_i, v_a_lambda, v_a_w_out, v_kv_norm, v_w_kv, v_b_norm, v_b_w_in, v_b_w_out, v_final_norm):
    T, D = x.shape[1], x.shape[2]
    nb, bw = a_w_r.shape[1], a_w_r.shape[3]
    C = nb * bw
    me = 4 * lax.axis_index("x") + 2 * lax.axis_index("y") + lax.axis_index("c")
    xs = x[0]
    target = loss_target[0]

    big = [a_w_in[0], a_w_out[0], w_kv, b_w_in[0], b_w_out[0], a_w_r[0], a_w_i[0]]
    sizes = [w.size // LANES for w in big]
    packed = jnp.concatenate([_rows(w.astype(BF16)) for w in big], axis=0)
    small_f32 = jnp.concatenate([_rows(a_conv_w[0]), _rows(b_norm[0])], axis=0)
    pad = (-small_f32.shape[0]) % SUBLANES
    small_f32 = jnp.pad(small_f32, ((0, pad), (0, 0)))
    packed_all, small_all = _weights_gather([packed, small_f32])

    offs = [0]
    for s in sizes:
        offs.append(offs[-1] + s)
    pieces = [packed_all[:, offs[i]:offs[i + 1], :] for i in range(len(big))]
    g_a_w_in = pieces[0].reshape(N_DEV, D, a_w_in.shape[2])
    g_a_w_out = pieces[1].reshape(C, D)
    g_w_kv = pieces[2].reshape(N_DEV, D, w_kv.shape[1])
    g_b_w_in = pieces[3].reshape(N_DEV, D, b_w_in.shape[2])
    g_b_w_out = pieces[4].reshape(b_w_out.shape[1] * N_DEV, D)
    rows_r = a_w_r.shape[2]
    g_w_r = pieces[5].reshape(N_DEV, nb, rows_r, bw).transpose(1, 0, 2, 3).reshape(nb, bw, bw)
    g_w_i = pieces[6].reshape(N_DEV, nb, rows_r, bw).transpose(1, 0, 2, 3).reshape(nb, bw, bw)
    cw_rows = a_conv_w.shape[1] * a_conv_w.shape[2] // LANES
    conv_w_full = small_all[:, :cw_rows, :].reshape(N_DEV, CONV_W, a_conv_w.shape[2])
    conv_w_full = conv_w_full.transpose(1, 0, 2).reshape(CONV_W, C)
    bn_rows = b_norm.shape[1] // LANES
    b_norm_full = small_all[:, cw_rows:cw_rows + bn_rows, :].reshape(1, D)
    kv_norm2, final_norm2 = kv_norm.reshape(1, D), final_norm.reshape(1, D)

    (loss_part, grad_x, dw_a_in, dw_a_out, dw_kv, dw_b_in, dw_b_out, dw_r, dw_i, d_a_norm, d_conv_w,
     d_conv_b, d_b_r, d_b_i, d_lambda, d_kv_norm, d_b_norm, d_final_norm) = _forward_backward(
         xs, target, a_norm, g_a_w_in, conv_w_full, a_conv_b, g_w_r, g_w_i, a_b_r, a_b_i, a_lambda,
         g_a_w_out, kv_norm2, g_w_kv, b_norm_full, g_b_w_in, g_b_w_out, final_norm2)

    def lru_shards(dw):
        return dw.reshape(nb, N_DEV, rows_r, bw).transpose(1, 0, 2, 3).reshape(N_DEV, nb * rows_r, bw)

    full = [dw_a_in, dw_a_out.reshape(N_DEV, a_w_out.shape[1], D), dw_kv, dw_b_in,
            dw_b_out.reshape(N_DEV, b_w_out.shape[1], D), lru_shards(dw_r), lru_shards(dw_i)]
    small_parts = [d_a_norm, d_conv_w, d_conv_b, d_b_r, d_b_i, d_lambda, d_kv_norm, d_b_norm, d_final_norm]
    small_sizes = [p.size // LANES for p in small_parts]
    small = jnp.concatenate([_rows(p) for p in small_parts], axis=0)
    outs = _grads_to_sibling(full, small)
    kept, got, small_everyone = outs[:len(full)], outs[len(full):2 * len(full)], outs[-1]
    sums = [_pair_sum(k_, g_, name=f"pair_sum_{i}") for i, (k_, g_) in enumerate(zip(kept, got))]
    others = _grads_to_chips([s[1] for s in sums])

    def shard2d(w):
        return w.reshape(-1, w.shape[-1])

    names_big = [(a_w_in, m_a_w_in, v_a_w_in), (a_w_out, m_a_w_out, v_a_w_out), (w_kv, m_w_kv, v_w_kv),
                 (b_w_in, m_b_w_in, v_b_w_in), (b_w_out, m_b_w_out, v_b_w_out),
                 (a_w_r, m_a_w_r, v_a_w_r), (a_w_i, m_a_w_i, v_a_w_i)]
    upd_big = []
    for i, (w, m, v) in enumerate(names_big):
        res = _adamw(shard2d(w), shard2d(m), shard2d(v), [sums[i][0], others[i]], name=f"adamw_{i}")
        upd_big.append([r.reshape(w.shape) for r in res])

    soffs = [0]
    for s in small_sizes:
        soffs.append(soffs[-1] + s)

    def small_piece(i):
        return small_everyone[:, soffs[i]:soffs[i + 1], :]

    cw_cols = a_conv_w.shape[2]
    conv_piece = small_piece(1).reshape(N_DEV, CONV_W, C)
    conv_piece = lax.dynamic_slice_in_dim(conv_piece, me * cw_cols, cw_cols, axis=2)
    conv_piece = conv_piece.reshape(N_DEV, CONV_W * cw_cols // LANES, LANES)
    bn_piece = lax.dynamic_slice_in_dim(small_piece(7), me * bn_rows, bn_rows, axis=1)
    small_g = jnp.concatenate([small_piece(0), conv_piece, small_piece(2), small_piece(3), small_piece(4),
                               small_piece(5), small_piece(6), bn_piece, small_piece(8)], axis=1)
    small_w = [(a_norm, m_a_norm, v_a_norm), (a_conv_w, m_a_conv_w, v_a_conv_w),
               (a_conv_b, m_a_conv_b, v_a_conv_b), (a_b_r, m_a_b_r, v_a_b_r), (a_b_i, m_a_b_i, v_a_b_i),
               (a_lambda, m_a_lambda, v_a_lambda), (kv_norm, m_kv_norm, v_kv_norm),
               (b_norm, m_b_norm, v_b_norm), (final_norm, m_final_norm, v_final_norm)]
    pack = lambda idx: jnp.concatenate([_rows(t[idx]) for t in small_w], axis=0)
    res_small = _adamw(pack(0), pack(1), pack(2), [small_g], name="adamw_small")
    woffs = [0]
    for t in small_w:
        woffs.append(woffs[-1] + t[0].size // LANES)
    upd_small = [[r[woffs[i]:woffs[i + 1]].reshape(small_w[i][0].shape) for r in res_small]
                 for i in range(len(small_w))]

    order = [("s", 0), ("b", 0), ("s", 1), ("s", 2), ("b", 5), ("s", 3), ("b", 6), ("s", 4), ("s", 5),
             ("b", 1), ("s", 6), ("b", 2), ("s", 7), ("b", 3), ("b", 4), ("s", 8)]
    per_weight = [(upd_big if kind == "b" else upd_small)[i] for kind, i in order]
    loss = lax.psum(loss_part[0, 0], ("x", "y", "c"))
    result = [loss, grad_x[None]]
    for field in range(4):
        result += [u[field] for u in per_weight]
    return tuple(result)
```

```python
import functools
import math

import jax
import jax.numpy as jnp
from jax import lax
from jax.experimental import pallas as pl
from jax.experimental.pallas import tpu as pltpu

F32 = jnp.float32
BF16 = jnp.bfloat16
MESH = pl.DeviceIdType.MESH

EPS = 1e-6
LRU_C = 8.0
CONV_W = 4
HEAD_DIM = 128
ADAM_LR = 0.001
ADAM_B1 = 0.9
ADAM_B2 = 0.999
ADAM_EPS = 1e-08
ADAM_WD = 0.01
ADAM_STEP = 10

N_DEV = 8
LANES = 128
SUBLANES = 8
VMEM_LIMIT = 56 * 1024 * 1024

ATT_KEY_BLOCK = 256
ATT_QUERY_BLOCK = 512
SCAN_BLOCK = 256
ROW_BLOCK = 256
ANY = pl.BlockSpec(memory_space=pl.ANY)


def _pcall(body, **kw):
    return pl.pallas_call(body, **kw)


def _params(*sem):
    return pltpu.CompilerParams(dimension_semantics=sem, vmem_limit_bytes=VMEM_LIMIT)


def _pick(n, cap):
    if n <= cap:
        return n
    best = None
    for t in range(LANES, cap + 1, LANES):
        if n % t == 0:
            best = t
    assert best is not None, (n, cap)
    return best


def _sigmoid(x):
    return 1.0 / (1.0 + jnp.exp(-x))


def _dot(a, b, ca, cb):
    return lax.dot_general(a, b, (((ca,), (cb,)), ((), ())), preferred_element_type=F32)


def _mm_nn(a, b3, *, name, out_dtype, s_off=0, s_cnt=None, res=None):
    T, K = a.shape
    S, K2, n = b3.shape
    assert K == K2
    s_cnt = S if s_cnt is None else s_cnt
    tm = min(T, 512)
    tn = _pick(n, 1024)
    per = n // tn
    has_res = res is not None

    def body(a_ref, b_ref, *rest):
        o_ref = rest[-1]
        acc = jnp.dot(a_ref[...].astype(BF16), b_ref[...], preferred_element_type=F32)
        if has_res:
            acc = acc + rest[0][...]
        o_ref[...] = acc.astype(out_dtype)

    in_specs = [pl.BlockSpec((tm, K), lambda i, j: (i, 0)),
                pl.BlockSpec((None, K, tn), lambda i, j: (s_off + j // per, 0, j % per))]
    args = [a, b3]
    if has_res:
        in_specs.append(pl.BlockSpec((tm, tn), lambda i, j: (i, j)))
        args.append(res)
    return _pcall(
        body, name=name, grid=(T // tm, s_cnt * per), in_specs=in_specs,
        out_specs=pl.BlockSpec((tm, tn), lambda i, j: (i, j)),
        out_shape=jax.ShapeDtypeStruct((T, s_cnt * n), out_dtype),
        compiler_params=_params("parallel", "parallel"))(*args)


def _mm_nt(a, b, *, name, out_dtype=F32):
    T, K = a.shape
    tm = min(T, 512)
    if b.ndim == 2:
        N, K2 = b.shape
        tk = _pick(K, 2048)
        nk = K // tk
        tn = _pick(N, 1024)
        b_spec = pl.BlockSpec((tn, tk), lambda i, j, k: (j, k))
    else:
        S, N, tk = b.shape
        K2 = S * tk
        nk = S
        tn = _pick(N, 1024)
        b_spec = pl.BlockSpec((None, tn, tk), lambda i, j, k: (k, j, 0))
    assert K == K2

    def body(a_ref, b_ref, o_ref, acc_ref):
        k = pl.program_id(2)
        p = _dot(a_ref[...].astype(BF16), b_ref[...], 1, 1)

        @pl.when(k == 0)
        def _():
            acc_ref[...] = p

        @pl.when(k > 0)
        def _():
            acc_ref[...] += p

        @pl.when(k == nk - 1)
        def _():
            o_ref[...] = acc_ref[...].astype(out_dtype)

    return _pcall(
        body, name=name, grid=(T // tm, N // tn, nk),
        in_specs=[pl.BlockSpec((tm, tk), lambda i, j, k: (i, k)), b_spec],
        out_specs=pl.BlockSpec((tm, tn), lambda i, j, k: (i, j)),
        out_shape=jax.ShapeDtypeStruct((T, N), out_dtype),
        scratch_shapes=[pltpu.VMEM((tm, tn), F32)],
        compiler_params=_params("parallel", "parallel", "arbitrary"))(a, b)


def _mm_tn(a, b, *, name, shards=1):
    T, Ko = a.shape
    T2, N = b.shape
    assert T == T2
    n = N // shards
    tt = min(T, 512)
    tko = _pick(Ko, 1024)
    tn = _pick(n, 1024)
    per = n // tn

    def body(a_ref, b_ref, o_ref):
        t = pl.program_id(2)
        p = _dot(a_ref[...].astype(BF16), b_ref[...].astype(BF16), 0, 0)

        @pl.when(t == 0)
        def _():
            o_ref[...] = p

        @pl.when(t > 0)
        def _():
            o_ref[...] += p

    if shards == 1:
        out_spec = pl.BlockSpec((tko, tn), lambda i, j, t: (i, j))
        out_shape = jax.ShapeDtypeStruct((Ko, N), F32)
    else:
        out_spec = pl.BlockSpec((None, tko, tn), lambda i, j, t: (j // per, i, j % per))
        out_shape = jax.ShapeDtypeStruct((shards, Ko, n), F32)
    return _pcall(
        body, name=name, grid=(Ko // tko, N // tn, T // tt),
        in_specs=[pl.BlockSpec((tt, tko), lambda i, j, t: (t, i)),
                  pl.BlockSpec((tt, tn), lambda i, j, t: (t, j))],
        out_specs=out_spec, out_shape=out_shape,
        compiler_params=_params("parallel", "parallel", "arbitrary"))(a, b)


def _rms_fwd(x, gains, *, name):
    T, D = x.shape
    tm = min(T, ROW_BLOCK)
    n = len(gains)

    def body(x_ref, *refs):
        xv = x_ref[...]
        xh = xv * lax.rsqrt(jnp.mean(xv * xv, axis=-1, keepdims=True) + EPS)
        for g_ref, o_ref in zip(refs[:n], refs[n:]):
            o_ref[...] = (xh * g_ref[...]).astype(BF16)

    row = pl.BlockSpec((tm, D), lambda i: (i, 0))
    vec = pl.BlockSpec((1, D), lambda i: (0, 0))
    return _pcall(
        body, name=name, grid=(T // tm,), in_specs=[row] + [vec] * n, out_specs=[row] * n,
        out_shape=[jax.ShapeDtypeStruct((T, D), BF16)] * n,
        compiler_params=_params("parallel"))(x, *gains)


def _rms_bwd(x, dres, dhs, gains, *, name):
    T, D = x.shape
    tm = min(T, ROW_BLOCK)
    n = len(gains)

    def body(x_ref, dres_ref, *refs):
        dh_refs, g_refs = refs[:n], refs[n:2 * n]
        dx_ref, dg_refs = refs[2 * n], refs[2 * n + 1:]
        i = pl.program_id(0)
        xv = x_ref[...]
        r = lax.rsqrt(jnp.mean(xv * xv, axis=-1, keepdims=True) + EPS)
        xh = xv * r
        dxh = jnp.zeros_like(xv)
        for dh_ref, g_ref, dg_ref in zip(dh_refs, g_refs, dg_refs):
            dh = dh_ref[...]
            part = jnp.sum(dh * xh, axis=0, keepdims=True)

            @pl.when(i == 0)
            def _():
                dg_ref[...] = part

            @pl.when(i > 0)
            def _():
                dg_ref[...] += part

            dxh = dxh + dh * g_ref[...]
        dx_ref[...] = dres_ref[...] + r * (dxh - xh * jnp.mean(dxh * xh, axis=-1, keepdims=True))

    row = pl.BlockSpec((tm, D), lambda i: (i, 0))
    vec = pl.BlockSpec((1, D), lambda i: (0, 0))
    return _pcall(
        body, name=name, grid=(T // tm,), in_specs=[row, row] + [row] * n + [vec] * n,
        out_specs=[row] + [vec] * n,
        out_shape=[jax.ShapeDtypeStruct((T, D), F32)] + [jax.ShapeDtypeStruct((1, D), F32)] * n,
        compiler_params=_params("arbitrary"))(x, dres, *dhs, *gains)


def _final_loss(x, target, gain, *, name):
    T, D = x.shape
    tm = min(T, ROW_BLOCK)

    def body(x_ref, t_ref, g_ref, dx_ref, dg_ref, loss_ref):
        i = pl.program_id(0)
        xv = x_ref[...]
        g = g_ref[...]
        r = lax.rsqrt(jnp.mean(xv * xv, axis=-1, keepdims=True) + EPS)
        xh = xv * r
        err = xh * g - t_ref[...]
        part_loss = 0.5 * jnp.sum(jnp.mean(err * err, axis=-1, keepdims=True), axis=0, keepdims=True)
        dy = err * (1.0 / D)
        part_g = jnp.sum(dy * xh, axis=0, keepdims=True)

        @pl.when(i == 0)
        def _():
            dg_ref[...] = part_g
            loss_ref[...] = jnp.broadcast_to(part_loss, loss_ref.shape)

        @pl.when(i > 0)
        def _():
            dg_ref[...] += part_g
            loss_ref[...] += jnp.broadcast_to(part_loss, loss_ref.shape)

        dxh = dy * g
        dx_ref[...] = r * (dxh - xh * jnp.mean(dxh * xh, axis=-1, keepdims=True))

    row = pl.BlockSpec((tm, D), lambda i: (i, 0))
    vec = pl.BlockSpec((1, D), lambda i: (0, 0))
    return _pcall(
        body, name=name, grid=(T // tm,), in_specs=[row, row, vec],
        out_specs=[row, vec, pl.BlockSpec((1, LANES), lambda i: (0, 0))],
        out_shape=[jax.ShapeDtypeStruct((T, D), F32), jax.ShapeDtypeStruct((1, D), F32),
                   jax.ShapeDtypeStruct((1, LANES), F32)],
        compiler_params=_params("arbitrary"))(x, target, gain)


def _shift_down(x, prev_tail, j, row):
    tb = x.shape[0]
    prev = jnp.tile(prev_tail, (tb // SUBLANES, 1))
    return jnp.where(row >= j, pltpu.roll(x, j, 0), pltpu.roll(prev, j, 0))


def _shift_up(x, next_head, j, row):
    tb = x.shape[0]
    nxt = jnp.tile(next_head, (tb // SUBLANES, 1))
    return jnp.where(row < tb - j, pltpu.roll(x, tb - j, 0), pltpu.roll(nxt, tb - j, 0))


def _lru_gates(xb, wr, wi, br, bi, lam):
    xbb = xb.astype(BF16)
    r = _sigmoid(jnp.dot(xbb, wr, preferred_element_type=F32) + br)
    i = _sigmoid(jnp.dot(xbb, wi, preferred_element_type=F32) + bi)
    sp = jnp.maximum(-lam, 0.0) + jnp.log1p(jnp.exp(-jnp.abs(lam)))
    log_a = (-LRU_C) * r * sp
    a = jnp.exp(log_a)
    a2 = a * a
    mult = jnp.sqrt(jnp.maximum(-jnp.tanh(log_a) * (1.0 + a2), 0.0))
    return xbb, r, i, sp, a, a2, mult


def _acore_fwd(proj, conv_w, conv_b, w_r, w_i, b_r, b_i, lam, *, name):
    T, C2 = proj.shape
    C = C2 // 2
    nb, bw, _ = w_r.shape
    tb = min(T, SCAN_BLOCK)

    def body(xp_ref, gate_ref, cw_ref, cb_ref, wr_ref, wi_ref, br_ref, bi_ref, lam_ref,
             xb_ref, h_ref, yg_ref, tail_ref, hlast_ref):
        t = pl.program_id(1)

        @pl.when(t == 0)
        def _():
            tail_ref[...] = jnp.zeros_like(tail_ref)
            hlast_ref[...] = jnp.zeros_like(hlast_ref)

        row = lax.broadcasted_iota(jnp.int32, (tb, bw), 0)
        xp = xp_ref[...]
        tail = tail_ref[...]
        xb = cb_ref[...] + cw_ref[CONV_W - 1:CONV_W, :] * xp
        for j in range(1, CONV_W):
            xb = xb + cw_ref[CONV_W - 1 - j:CONV_W - j, :] * _shift_down(xp, tail, j, row)
        tail_ref[...] = xp[tb - SUBLANES:, :]
        xb_ref[...] = xb

        _, r, i, sp, a, a2, mult = _lru_gates(xb, wr_ref[...], wi_ref[...], br_ref[...], bi_ref[...],
                                              lam_ref[...])
        ca, cb = a, mult * (i * xb)
        s = 1
        while s < tb:
            m = row >= s
            cb = jnp.where(m, ca * pltpu.roll(cb, s, 0) + cb, cb)
            ca = jnp.where(m, ca * pltpu.roll(ca, s, 0), ca)
            s *= 2
        h = cb + ca * hlast_ref[SUBLANES - 1:SUBLANES, :]
        hlast_ref[...] = h[tb - SUBLANES:, :]
        h_ref[...] = h
        gate = gate_ref[...]
        yg_ref[...] = (h * (gate * _sigmoid(gate))).astype(BF16)

    blk = lambda off: pl.BlockSpec((tb, bw), lambda n, t: (t, off + n))
    vec = pl.BlockSpec((1, bw), lambda n, t: (0, n))
    wspec = pl.BlockSpec((None, bw, bw), lambda n, t: (n, 0, 0))
    return _pcall(
        body, name=name, grid=(nb, T // tb),
        in_specs=[blk(0), blk(nb), pl.BlockSpec((CONV_W, bw), lambda n, t: (0, n)), vec, wspec, wspec,
                  vec, vec, vec],
        out_specs=[blk(0), blk(0), blk(0)],
        out_shape=[jax.ShapeDtypeStruct((T, C), F32), jax.ShapeDtypeStruct((T, C), F32),
                   jax.ShapeDtypeStruct((T, C), BF16)],
        scratch_shapes=[pltpu.VMEM((SUBLANES, bw), F32), pltpu.VMEM((SUBLANES, bw), F32)],
        compiler_params=_params("parallel", "arbitrary"))(
            proj, proj, conv_w, conv_b, w_r, w_i, b_r, b_i, lam)


def _acore_bwd(dyg, proj, xb_all, h_all, conv_w, w_r, w_i, b_r, b_i, lam, *, name):
    T, C2 = proj.shape
    C = C2 // 2
    nb, bw, _ = w_r.shape
    tb = min(T, SCAN_BLOCK)
    nt = T // tb
    per8 = tb // SUBLANES

    def body(dyg_ref, xp_ref, gate_ref, xb_ref, h_ref, xp_prev_ref, h_prev_ref, cw_ref,
             wr_ref, wi_ref, br_ref, bi_ref, lam_ref,
             dxp_ref, dgate_ref, dcw_ref, dcb_ref, dbr_ref, dbi_ref, dlam_ref, dwr_ref, dwi_ref,
             gh_next_ref, a_next_ref, dxb_next_ref):
        step = pl.program_id(1)
        first_block = step == nt - 1

        @pl.when(step == 0)
        def _():
            gh_next_ref[...] = jnp.zeros_like(gh_next_ref)
            a_next_ref[...] = jnp.zeros_like(a_next_ref)
            dxb_next_ref[...] = jnp.zeros_like(dxb_next_ref)

        row = lax.broadcasted_iota(jnp.int32, (tb, bw), 0)
        keep = jnp.where(first_block, 0.0, 1.0)
        h_prev = h_prev_ref[...] * keep
        xp_prev = xp_prev_ref[...] * keep
        xp, gate, xb, h, dyg_v = xp_ref[...], gate_ref[...], xb_ref[...], h_ref[...], dyg_ref[...]
        lam_v = lam_ref[...]
        wr, wi = wr_ref[...], wi_ref[...]

        sg = _sigmoid(gate)
        dh = dyg_v * (gate * sg)
        dgate_ref[...] = (dyg_v * h * (sg * (1.0 + gate * (1.0 - sg)))).astype(BF16)

        xbb, r, i, sp, a, a2, mult = _lru_gates(xb, wr, wi, br_ref[...], bi_ref[...], lam_v)

        cg = dh
        cc = _shift_up(a, a_next_ref[...], 1, row)
        s = 1
        while s < tb:
            m = row < tb - s
            cg = jnp.where(m, cc * pltpu.roll(cg, tb - s, 0) + cg, cg)
            cc = jnp.where(m, cc * pltpu.roll(cc, tb - s, 0), cc)
            s *= 2
        gh = cg + cc * gh_next_ref[0:1, :]
        gh_next_ref[...] = gh[0:SUBLANES, :]
        a_next_ref[...] = a[0:SUBLANES, :]

        da = gh * _shift_down(h, h_prev, 1, row)
        dmult = gh * (i * xb)
        di = gh * mult * xb
        dxb = gh * mult * i
        dla = da * a - dmult * jnp.where(mult > 0.0, a2 / mult, 0.0)
        dr = dla * ((-LRU_C) * sp)
        dsp = jnp.sum(dla * ((-LRU_C) * r), axis=0, keepdims=True)
        dlam_part = dsp * (-_sigmoid(-lam_v))
        dpr = dr * r * (1.0 - r)
        dpi = di * i * (1.0 - i)
        dbr_part = jnp.sum(dpr, axis=0, keepdims=True)
        dbi_part = jnp.sum(dpi, axis=0, keepdims=True)
        dprb, dpib = dpr.astype(BF16), dpi.astype(BF16)
        dwr_part = _dot(xbb, dprb, 0, 0)
        dwi_part = _dot(xbb, dpib, 0, 0)
        dxb = dxb + _dot(dprb, wr, 1, 1) + _dot(dpib, wi, 1, 1)

        dxb_next = dxb_next_ref[...]
        dxp = cw_ref[CONV_W - 1:CONV_W, :] * dxb
        for j in range(1, CONV_W):
            dxp = dxp + cw_ref[CONV_W - 1 - j:CONV_W - j, :] * _shift_up(dxb, dxb_next, j, row)
        dxb_next_ref[...] = dxb[0:SUBLANES, :]
        dxp_ref[...] = dxp.astype(BF16)
        dcb_part = jnp.sum(dxb, axis=0, keepdims=True)
        dcw_rows = []
        for k in range(CONV_W):
            j = CONV_W - 1 - k
            sh = xp if j == 0 else _shift_down(xp, xp_prev, j, row)
            dcw_rows.append(jnp.sum(dxb * sh, axis=0, keepdims=True))

        @pl.when(step == 0)
        def _():
            for k in range(CONV_W):
                dcw_ref[k:k + 1, :] = dcw_rows[k]
            dcb_ref[...] = dcb_part
            dbr_ref[...] = dbr_part
            dbi_ref[...] = dbi_part
            dlam_ref[...] = dlam_part
            dwr_ref[...] = dwr_part
            dwi_ref[...] = dwi_part

        @pl.when(step > 0)
        def _():
            for k in range(CONV_W):
                dcw_ref[k:k + 1, :] += dcw_rows[k]
            dcb_ref[...] += dcb_part
            dbr_ref[...] += dbr_part
            dbi_ref[...] += dbi_part
            dlam_ref[...] += dlam_part
            dwr_ref[...] += dwr_part
            dwi_ref[...] += dwi_part

    rev = lambda s: nt - 1 - s
    blk = lambda off: pl.BlockSpec((tb, bw), lambda n, s: (rev(s), off + n))
    prev8 = lambda off: pl.BlockSpec(
        (SUBLANES, bw), lambda n, s: (jnp.maximum(rev(s) * per8 - 1, 0), off + n))
    vec = pl.BlockSpec((1, bw), lambda n, s: (0, n))
    wspec = pl.BlockSpec((None, bw, bw), lambda n, s: (n, 0, 0))
    cwspec = pl.BlockSpec((CONV_W, bw), lambda n, s: (0, n))
    vshape = jax.ShapeDtypeStruct((1, C), F32)
    wshape = jax.ShapeDtypeStruct((nb, bw, bw), F32)
    return _pcall(
        body, name=name, grid=(nb, nt),
        in_specs=[blk(0), blk(0), blk(nb), blk(0), blk(0), prev8(0), prev8(0), cwspec,
                  wspec, wspec, vec, vec, vec],
        out_specs=[blk(0), blk(0), cwspec, vec, vec, vec, vec, wspec, wspec],
        out_shape=[jax.ShapeDtypeStruct((T, C), BF16), jax.ShapeDtypeStruct((T, C), BF16),
                   jax.ShapeDtypeStruct((CONV_W, C), F32), vshape, vshape, vshape, vshape,
                   wshape, wshape],
        scratch_shapes=[pltpu.VMEM((SUBLANES, bw), F32)] * 3,
        compiler_params=_params("parallel", "arbitrary"))(
            dyg, proj, proj, xb_all, h_all, proj, h_all, conv_w, w_r, w_i, b_r, b_i, lam)


def _split_cumsum(lk, tri):
    hi = lk.astype(BF16)
    lo = (lk - hi.astype(F32)).astype(BF16)
    return (jnp.dot(hi, tri, preferred_element_type=F32) + jnp.dot(lo, tri, preferred_element_type=F32))


def _log_sigmoids(z):
    t = jnp.log(1.0 + jnp.exp(-jnp.abs(z)))
    return jnp.minimum(z, 0.0) - t, -jnp.maximum(z, 0.0) - t


def _attn_blocks(T):
    bk = min(T, ATT_KEY_BLOCK)
    bq = min(T, ATT_QUERY_BLOCK)
    return bq, bk, bq // bk


def _attn_fwd(q, kv, gate, *, name):
    T, HD = q.shape
    H = HD // HEAD_DIM
    bq, bk, per = _attn_blocks(T)
    scale = 1.0 / math.sqrt(HEAD_DIM)

    def body(q_ref, k_ref, v_ref, g_ref, o_ref, og_ref, lt_ref):
        i = pl.program_id(1)
        qv = q_ref[...]
        tr = lax.broadcasted_iota(jnp.int32, (bk, bk), 0)
        tc = lax.broadcasted_iota(jnp.int32, (bk, bk), 1)
        tri = (tr > tc).astype(BF16)
        ahead = (lax.broadcasted_iota(jnp.int32, (bq, bk), 0)
                 - lax.broadcasted_iota(jnp.int32, (bq, bk), 1))

        def block(j, acc, c, mask):
            ks = pl.multiple_of(j * bk, bk)
            kb = k_ref[pl.ds(ks, bk), :]
            vb = v_ref[pl.ds(ks, bk), :]
            z = _dot(qv, kb, 1, 1) * scale
            ls, lk = _log_sigmoids(z)
            if mask:
                causal = ahead > ks - i * bq
                lk = jnp.where(causal, lk, 0.0)
            later = _split_cumsum(lk, tri) + c
            w = jnp.exp(ls + later)
            if mask:
                w = jnp.where(causal, w, 0.0)
            acc = acc + jnp.dot(w.astype(BF16), vb, preferred_element_type=F32)
            return acc, c + jnp.sum(lk, axis=1, keepdims=True)

        state = (jnp.zeros((bq, HEAD_DIM), F32), jnp.zeros((bq, 1), F32))
        for d in range(per):
            state = block(i * per + per - 1 - d, state[0], state[1], True)
        acc, c = lax.fori_loop(0, i * per, lambda jj, s: block(i * per - 1 - jj, s[0], s[1], False), state)
        o_ref[...] = acc
        g = g_ref[...]
        og_ref[...] = (acc * (g * _sigmoid(g))).astype(BF16)
        lt_ref[...] = jnp.broadcast_to(c, (bq, HEAD_DIM))

    qspec = pl.BlockSpec((bq, HEAD_DIM), lambda h, i: (i, h))
    return _pcall(
        body, name=name, grid=(H, T // bq),
        in_specs=[qspec, pl.BlockSpec((T, HEAD_DIM), lambda h, i: (0, h)),
                  pl.BlockSpec((T, HEAD_DIM), lambda h, i: (0, H + h)), qspec],
        out_specs=[qspec, qspec, qspec],
        out_shape=[jax.ShapeDtypeStruct((T, HD), F32), jax.ShapeDtypeStruct((T, HD), BF16),
                   jax.ShapeDtypeStruct((T, HD), F32)],
        compiler_params=_params("parallel", "arbitrary"))(q, kv, kv, gate)


def _attn_bwd(q, kv, gate, o, ltot, dog, *, name):
    T, HD = q.shape
    H = HD // HEAD_DIM
    bq, bk, per = _attn_blocks(T)
    nq = T // bq
    scale = 1.0 / math.sqrt(HEAD_DIM)

    def body(q_ref, k_ref, v_ref, g_ref, o_ref, lt_ref, dog_ref,
             dq_ref, dg_ref, dk_ref, dv_ref, dk_acc, dv_acc):
        i = pl.program_id(1)

        @pl.when(i == 0)
        def _():
            dk_acc[...] = jnp.zeros_like(dk_acc)
            dv_acc[...] = jnp.zeros_like(dv_acc)

        qv = q_ref[...]
        g, ov, dogv = g_ref[...], o_ref[...], dog_ref[...]
        sg = _sigmoid(g)
        do = dogv * (g * sg)
        dg_ref[...] = (dogv * ov * (sg * (1.0 + g * (1.0 - sg)))).astype(BF16)
        dob = do.astype(BF16)
        ltot_v = lt_ref[:, 0:1]
        tr = lax.broadcasted_iota(jnp.int32, (bk, bk), 0)
        tc = lax.broadcasted_iota(jnp.int32, (bk, bk), 1)
        tri_incl = (tr <= tc).astype(BF16)
        tri_excl = (tr < tc).astype(BF16)
        ahead = (lax.broadcasted_iota(jnp.int32, (bq, bk), 0)
                 - lax.broadcasted_iota(jnp.int32, (bq, bk), 1))

        def block(j, dq, p_lk, p_g, mask):
            ks = pl.multiple_of(j * bk, bk)
            kb = k_ref[pl.ds(ks, bk), :]
            vb = v_ref[pl.ds(ks, bk), :]
            z = _dot(qv, kb, 1, 1) * scale
            ls, lk = _log_sigmoids(z)
            if mask:
                causal = ahead > ks - i * bq
                lk = jnp.where(causal, lk, 0.0)
            later = (ltot_v - p_lk) - _split_cumsum(lk, tri_incl)
            w = jnp.exp(ls + later)
            if mask:
                w = jnp.where(causal, w, 0.0)
            gm = _dot(dob, vb, 1, 1) * w
            before = jnp.dot(gm.astype(BF16), tri_excl, preferred_element_type=F32) + p_g
            dz = gm - jnp.exp(ls) * (gm + before)
            if mask:
                dz = jnp.where(causal, dz, 0.0)
            dzb = (dz * scale).astype(BF16)
            dq = dq + jnp.dot(dzb, kb, preferred_element_type=F32)
            dk_acc[pl.ds(ks, bk), :] += _dot(dzb, qv, 0, 0)
            dv_acc[pl.ds(ks, bk), :] += _dot(w.astype(BF16), dob, 0, 0)
            return (dq, p_lk + jnp.sum(lk, axis=1, keepdims=True),
                    p_g + jnp.sum(gm, axis=1, keepdims=True))

        zero = jnp.zeros((bq, 1), F32)
        state = lax.fori_loop(0, i * per, lambda j, s: block(j, s[0], s[1], s[2], False),
                              (jnp.zeros((bq, HEAD_DIM), F32), zero, zero))
        for d in range(per):
            state = block(i * per + d, state[0], state[1], state[2], True)
        dq_ref[...] = state[0].astype(BF16)

        @pl.when(i == nq - 1)
        def _():
            dk_ref[...] = dk_acc[...].astype(BF16)
            dv_ref[...] = dv_acc[...].astype(BF16)

    qspec = pl.BlockSpec((bq, HEAD_DIM), lambda h, i: (i, h))
    kspec = pl.BlockSpec((T, HEAD_DIM), lambda h, i: (0, h))
    return _pcall(
        body, name=name, grid=(H, nq),
        in_specs=[qspec, kspec, pl.BlockSpec((T, HEAD_DIM), lambda h, i: (0, H + h)),
                  qspec, qspec, qspec, qspec],
        out_specs=[qspec, qspec, kspec, kspec],
        out_shape=[jax.ShapeDtypeStruct((T, HD), BF16)] * 4,
        scratch_shapes=[pltpu.VMEM((T, HEAD_DIM), F32)] * 2,
        compiler_params=_params("parallel", "arbitrary"))(q, kv, kv, gate, o, ltot, dog)


def _position():
    return lax.axis_index("x"), lax.axis_index("y"), lax.axis_index("c")


def _chip_of(k, x, y):
    return (1 - x if k & 1 else x), (1 - y if k & 2 else y)


def _weights_gather(shards):
    n = len(shards)

    def body(*refs):
        ins, outs = refs[:n], refs[n:2 * n]
        send_sems, recv_sems, local_sems = refs[2 * n:]
        x, y, c = _position()
        sibling = (x, y, 1 - c)
        chips = [_chip_of(k, x, y) for k in (1, 2, 3)]

        def copy(a, k, block, to, src=None):
            slot = outs[a].at[4 * block[0] + 2 * block[1] + block[2]]
            return pltpu.make_async_remote_copy(
                src_ref=slot if src is None else src, dst_ref=slot,
                send_sem=send_sems.at[a, k], recv_sem=recv_sems.at[a, k],
                device_id=to, device_id_type=MESH)

        mine = [pltpu.make_async_copy(ins[a], outs[a].at[4 * x + 2 * y + c], local_sems.at[a])
                for a in range(n)]
        for cp in mine:
            cp.start()
        first = []
        for a in range(n):
            first.append(copy(a, 0, (x, y, c), sibling, src=ins[a]))
            first += [copy(a, 1 + j, (x, y, c), (*chip, c), src=ins[a]) for j, chip in enumerate(chips)]
        for cp in first:
            cp.start()
        passed = []
        for j, chip in enumerate(chips):
            for a in range(n):
                copy(a, 1 + j, (*chip, c), (x, y, c)).wait_recv()
                fwd = copy(a, 4 + j, (*chip, c), sibling)
                fwd.start()
                passed.append(fwd)
        for a in range(n):
            copy(a, 0, sibling, (x, y, c)).wait_recv()
            for j, chip in enumerate(chips):
                copy(a, 4 + j, (*chip, 1 - c), (x, y, c)).wait_recv()
        for cp in first + passed:
            cp.wait_send()
        for cp in mine:
            cp.wait()

    return _pcall(
        body, name="weights_gather", in_specs=[ANY] * n, out_specs=[ANY] * n,
        out_shape=[jax.ShapeDtypeStruct((N_DEV,) + s.shape, s.dtype) for s in shards],
        scratch_shapes=[pltpu.SemaphoreType.DMA((n, 7)), pltpu.SemaphoreType.DMA((n, 7)),
                        pltpu.SemaphoreType.DMA((n,))])(*shards)


def _grads_to_sibling(grads, small):
    n = len(grads)

    def body(*refs):
        g_refs, small_ref = refs[:n], refs[n]
        got, small_all = refs[n + 1:2 * n + 1], refs[2 * n + 1]
        send_sems, recv_sems, small_send, small_recv, small_local = refs[2 * n + 2:]
        x, y, c = _position()
        me = 4 * x + 2 * y + c
        remote = []
        for a in range(n):
            for chip in range(4):
                remote.append(pltpu.make_async_remote_copy(
                    src_ref=g_refs[a].at[2 * chip + (1 - c)], dst_ref=got[a].at[chip],
                    send_sem=send_sems.at[a, chip], recv_sem=recv_sems.at[a, chip],
                    device_id=(x, y, 1 - c), device_id_type=MESH))
        peers = []
        for m in range(1, N_DEV):
            px, py, pc = x ^ (m >> 2), y ^ ((m >> 1) & 1), c ^ (m & 1)
            peers.append(pltpu.make_async_remote_copy(
                src_ref=small_ref, dst_ref=small_all.at[me],
                send_sem=small_send.at[m - 1], recv_sem=small_recv.at[m - 1],
                device_id=(px, py, pc), device_id_type=MESH))
        own = pltpu.make_async_copy(small_ref, small_all.at[me], small_local)
        for cp in peers + remote + [own]:
            cp.start()
        for cp in peers + remote:
            cp.wait_send()
        for m in range(1, N_DEV):
            px, py, pc = x ^ (m >> 2), y ^ ((m >> 1) & 1), c ^ (m & 1)
            pltpu.make_async_remote_copy(
                src_ref=small_ref, dst_ref=small_all.at[4 * px + 2 * py + pc],
                send_sem=small_send.at[m - 1], recv_sem=small_recv.at[m - 1],
                device_id=(px, py, pc), device_id_type=MESH).wait_recv()
        for cp in remote:
            cp.wait_recv()
        own.wait()

    part = [jax.ShapeDtypeStruct((4,) + g.shape[1:], g.dtype) for g in grads]
    return _pcall(
        body, name="grads_to_sibling", in_specs=[ANY] * (n + 1), out_specs=[ANY] * (n + 1),
        out_shape=part + [jax.ShapeDtypeStruct((N_DEV,) + small.shape, small.dtype)],
        scratch_shapes=[pltpu.SemaphoreType.DMA((n, 4)), pltpu.SemaphoreType.DMA((n, 4)),
                        pltpu.SemaphoreType.DMA((7,)), pltpu.SemaphoreType.DMA((7,)),
                        pltpu.SemaphoreType.DMA])(*grads, small)


def _grads_to_chips(parts):
    n = len(parts)

    def body(*refs):
        p_refs, outs = refs[:n], refs[n:2 * n]
        send_sems, recv_sems = refs[2 * n:]
        x, y, c = _position()
        copies = []
        for a in range(n):
            for k in range(3):
                cx, cy = _chip_of(k + 1, x, y)
                copies.append(pltpu.make_async_remote_copy(
                    src_ref=p_refs[a].at[2 * cx + cy], dst_ref=outs[a].at[k],
                    send_sem=send_sems.at[a, k], recv_sem=recv_sems.at[a, k],
                    device_id=(cx, cy, c), device_id_type=MESH))
        for cp in copies:
            cp.start()
        for cp in copies:
            cp.wait_send()
        for cp in copies:
            cp.wait_recv()

    return _pcall(
        body, name="grads_to_chips", in_specs=[ANY] * n, out_specs=[ANY] * n,
        out_shape=[jax.ShapeDtypeStruct((3,) + p.shape[1:], p.dtype) for p in parts],
        scratch_shapes=[pltpu.SemaphoreType.DMA((n, 3)), pltpu.SemaphoreType.DMA((n, 3))])(*parts)


def _pair_sum(grad, got, *, name):
    _, R, C = got.shape
    tr = _pick8(R, max(SUBLANES, (1 << 17) // C))

    def body(g_ref, b_ref, o_ref):
        north = lax.axis_index("c") == 1
        for chip in range(4):
            o_ref[chip] = jnp.where(north, g_ref[chip, 1], g_ref[chip, 0]) + b_ref[chip]

    return _pcall(
        body, name=name, grid=(R // tr,),
        in_specs=[pl.BlockSpec((4, 2, tr, C), lambda i: (0, 0, i, 0)),
                  pl.BlockSpec((4, tr, C), lambda i: (0, i, 0))],
        out_specs=pl.BlockSpec((4, tr, C), lambda i: (0, i, 0)),
        out_shape=jax.ShapeDtypeStruct((4, R, C), F32),
        compiler_params=_params("parallel"))(grad.reshape(4, 2, R, C), got)


def _pick8(n, cap):
    if n <= cap:
        return n
    best = None
    for t in range(SUBLANES, cap + 1, SUBLANES):
        if n % t == 0:
            best = t
    assert best is not None, (n, cap)
    return best


def _adamw(w, m, v, parts, *, name, chip_sums=None):
    R, C = w.shape
    tr = _pick8(R, max(SUBLANES, (1 << 17) // C))
    c1 = 1.0 - ADAM_B1 ** ADAM_STEP
    c2 = 1.0 - ADAM_B2 ** ADAM_STEP
    parts = list(parts) if chip_sums is None else [chip_sums] + list(parts)
    np_ = len(parts)

    def body(w_ref, m_ref, v_ref, *refs):
        p_refs = refs[:np_]
        g_ref, d_ref, nm_ref, nv_ref = refs[np_:]
        g = None
        if chip_sums is not None:
            s_ref, p_refs = p_refs[0], p_refs[1:]
            x1, y1 = lax.axis_index("x") == 1, lax.axis_index("y") == 1
            g = jnp.where(x1, jnp.where(y1, s_ref[3], s_ref[2]), jnp.where(y1, s_ref[1], s_ref[0]))
        for p_ref in p_refs:
            for t in [p_ref[k] for k in range(p_ref.shape[0])]:
                g = t if g is None else g + t
        mn = ADAM_B1 * m_ref[...] + (1.0 - ADAM_B1) * g
        vn = ADAM_B2 * v_ref[...] + (1.0 - ADAM_B2) * (g * g)
        d_ref[...] = -ADAM_LR * ((mn / c1) / (jnp.sqrt(vn / c2) + ADAM_EPS) + ADAM_WD * w_ref[...])
        g_ref[...] = g
        nm_ref[...] = mn
        nv_ref[...] = vn

    spec = pl.BlockSpec((tr, C), lambda i: (i, 0))
    pspecs = [pl.BlockSpec((p.shape[0], tr, C), lambda i: (0, i, 0)) for p in parts]
    return _pcall(
        body, name=name, grid=(R // tr,), in_specs=[spec] * 3 + pspecs, out_specs=[spec] * 4,
        out_shape=[jax.ShapeDtypeStruct((R, C), F32)] * 4,
        compiler_params=_params("parallel"))(w, m, v, *parts)


def _rows(a):
    return a.reshape(-1, LANES)


def _forward_backward(xs, target, a_norm, g_a_w_in, conv_w, conv_b, g_w_r, g_w_i, b_r, b_i, lam,
                      g_a_w_out, kv_norm, g_w_kv, b_norm, g_b_w_in, g_b_w_out, final_norm):
    (h_a,) = _rms_fwd(xs, [a_norm], name="a_norm_fwd")
    proj_a = _mm_nn(h_a, g_a_w_in, name="a_in_proj", out_dtype=F32)
    xb, h_rec, yg = _acore_fwd(proj_a, conv_w, conv_b, g_w_r, g_w_i, b_r, b_i, lam, name="a_core_fwd")
    x1 = _mm_nn(yg, g_a_w_out[None], name="a_out_proj", out_dtype=F32, res=xs)
    hk, hb = _rms_fwd(x1, [kv_norm, b_norm], name="kv_b_norm_fwd")
    kv = _mm_nn(hk, g_w_kv, name="kv_proj", out_dtype=BF16)
    half = N_DEV // 2
    q = _mm_nn(hb, g_b_w_in, name="q_proj", out_dtype=BF16, s_off=0, s_cnt=half)
    gate_b = _mm_nn(hb, g_b_w_in, name="b_gate_proj", out_dtype=F32, s_off=half, s_cnt=half)
    o, og, ltot = _attn_fwd(q, kv, gate_b, name="attn_fwd")
    x2 = _mm_nn(og, g_b_w_out[None], name="b_out_proj", out_dtype=F32, res=x1)
    dx2, d_final_norm, loss_part = _final_loss(x2, target, final_norm, name="final_norm_loss")

    dog = _mm_nt(dx2, g_b_w_out, name="b_out_proj_bwd")
    dw_b_out = _mm_tn(og, dx2, name="b_out_proj_wgrad")
    dq, dgate_b, dk, dv = _attn_bwd(q, kv, gate_b, o, ltot, dog, name="attn_bwd")
    dproj_b = jnp.concatenate([dq, dgate_b], axis=1)
    dkv = jnp.concatenate([dk, dv], axis=1)
    dhb = _mm_nt(dproj_b, g_b_w_in, name="b_in_proj_bwd")
    dw_b_in = _mm_tn(hb, dproj_b, name="b_in_proj_wgrad", shards=N_DEV)
    dhk = _mm_nt(dkv, g_w_kv, name="kv_proj_bwd")
    dw_kv = _mm_tn(hk, dkv, name="kv_proj_wgrad", shards=N_DEV)
    dx1, d_b_norm, d_kv_norm = _rms_bwd(x1, dx2, [dhb, dhk], [b_norm, kv_norm], name="kv_b_norm_bwd")
    dyg = _mm_nt(dx1, g_a_w_out, name="a_out_proj_bwd")
    dw_a_out = _mm_tn(yg, dx1, name="a_out_proj_wgrad")
    (dxp, dgate_a, d_conv_w, d_conv_b, d_b_r, d_b_i, d_lambda, dw_r, dw_i) = _acore_bwd(
        dyg, proj_a, xb, h_rec, conv_w, g_w_r, g_w_i, b_r, b_i, lam, name="a_core_bwd")
    dproj_a = jnp.concatenate([dxp, dgate_a], axis=1)
    dh_a = _mm_nt(dproj_a, g_a_w_in, name="a_in_proj_bwd")
    dw_a_in = _mm_tn(h_a, dproj_a, name="a_in_proj_wgrad", shards=N_DEV)
    grad_x, d_a_norm = _rms_bwd(xs, dx1, [dh_a], [a_norm], name="a_norm_bwd")
    return (loss_part, grad_x, dw_a_in, dw_a_out, dw_kv, dw_b_in, dw_b_out, dw_r, dw_i, d_a_norm,
            d_conv_w, d_conv_b, d_b_r, d_b_i, d_lambda, d_kv_norm, d_b_norm, d_final_norm)


def kernel(x, a_norm, a_w_in, a_conv_w, a_conv_b, a_w_r, a_b_r, a_w_i, a_b_i, a_lambda, a_w_out, kv_norm, w_kv, b_norm, b_w_in, b_w_out, final_norm, loss_target, m_a_norm, m_a_w_in, m_a_conv_w, m_a_conv_b, m_a_w_r, m_a_b_r, m_a_w_i, m_a_b_i, m_a_lambda, m_a_w_out, m_kv_norm, m_w_kv, m_b_norm, m_b_w_in, m_b_w_out, m_final_norm, v_a_norm, v_a_w_in, v_a_conv_w, v_a_conv_b, v_a_w_r, v_a_b_r, v_a_w_i, v_a_b_i, v_a_lambda, v_a_w_out, v_kv_norm, v_w_kv, v_b_norm, v_b_w_in, v_b_w_out, v_final_norm):
    T, D = x.shape[1], x.shape[2]
    nb, bw = a_w_r.shape[1], a_w_r.shape[3]
    C = nb * bw
    me = 4 * lax.axis_index("x") + 2 * lax.axis_index("y") + lax.axis_index("c")
    xs = x[0]
    target = loss_target[0]

    big = [a_w_in[0], a_w_out[0], w_kv, b_w_in[0], b_w_out[0], a_w_r[0], a_w_i[0]]
    sizes = [w.size // LANES for w in big]
    packed = jnp.concatenate([_rows(w.astype(BF16)) for w in big], axis=0)
    small_f32 = jnp.concatenate([_rows(a_conv_w[0]), _rows(b_norm[0])], axis=0)
    pad = (-small_f32.shape[0]) % SUBLANES
    small_f32 = jnp.pad(small_f32, ((0, pad), (0, 0)))
    packed_all, small_all = _weights_gather([packed, small_f32])

    offs = [0]
    for s in sizes:
        offs.append(offs[-1] + s)
    pieces = [packed_all[:, offs[i]:offs[i + 1], :] for i in range(len(big))]
    g_a_w_in = pieces[0].reshape(N_DEV, D, a_w_in.shape[2])
    g_a_w_out = pieces[1].reshape(C, D)
    g_w_kv = pieces[2].reshape(N_DEV, D, w_kv.shape[1])
    g_b_w_in = pieces[3].reshape(N_DEV, D, b_w_in.shape[2])
    g_b_w_out = pieces[4].reshape(b_w_out.shape[1] * N_DEV, D)
    rows_r = a_w_r.shape[2]
    g_w_r = pieces[5].reshape(N_DEV, nb, rows_r, bw).transpose(1, 0, 2, 3).reshape(nb, bw, bw)
    g_w_i = pieces[6].reshape(N_DEV, nb, rows_r, bw).transpose(1, 0, 2, 3).reshape(nb, bw, bw)
    cw_rows = a_conv_w.shape[1] * a_conv_w.shape[2] // LANES
    conv_w_full = small_all[:, :cw_rows, :].reshape(N_DEV, CONV_W, a_conv_w.shape[2])
    conv_w_full = conv_w_full.transpose(1, 0, 2).reshape(CONV_W, C)
    bn_rows = b_norm.shape[1] // LANES
    b_norm_full = small_all[:, cw_rows:cw_rows + bn_rows, :].reshape(1, D)
    kv_norm2, final_norm2 = kv_norm.reshape(1, D), final_norm.reshape(1, D)

    (loss_part, grad_x, dw_a_in, dw_a_out, dw_kv, dw_b_in, dw_b_out, dw_r, dw_i, d_a_norm, d_conv_w,
     d_conv_b, d_b_r, d_b_i, d_lambda, d_kv_norm, d_b_norm, d_final_norm) = _forward_backward(
         xs, target, a_norm, g_a_w_in, conv_w_full, a_conv_b, g_w_r, g_w_i, a_b_r, a_b_i, a_lambda,
         g_a_w_out, kv_norm2, g_w_kv, b_norm_full, g_b_w_in, g_b_w_out, final_norm2)

    def lru_shards(dw):
        return dw.reshape(nb, N_DEV, rows_r, bw).transpose(1, 0, 2, 3).reshape(N_DEV, nb * rows_r, bw)

    full = [dw_a_in, dw_a_out.reshape(N_DEV, a_w_out.shape[1], D), dw_kv, dw_b_in,
            dw_b_out.reshape(N_DEV, b_w_out.shape[1], D), lru_shards(dw_r), lru_shards(dw_i)]
    small_parts = [d_a_norm, d_conv_w, d_conv_b, d_b_r, d_b_i, d_lambda, d_kv_norm, d_b_norm, d_final_norm]
    small_sizes = [p.size // LANES for p in small_parts]
    small = jnp.concatenate([_rows(p) for p in small_parts], axis=0)
    outs = _grads_to_sibling(full, small)
    got, small_everyone = outs[:len(full)], outs[-1]
    sums = [_pair_sum(f_, g_, name=f"pair_sum_{i}") for i, (f_, g_) in enumerate(zip(full, got))]
    others = _grads_to_chips(sums)

    def shard2d(w):
        return w.reshape(-1, w.shape[-1])

    names_big = [(a_w_in, m_a_w_in, v_a_w_in), (a_w_out, m_a_w_out, v_a_w_out), (w_kv, m_w_kv, v_w_kv),
                 (b_w_in, m_b_w_in, v_b_w_in), (b_w_out, m_b_w_out, v_b_w_out),
                 (a_w_r, m_a_w_r, v_a_w_r), (a_w_i, m_a_w_i, v_a_w_i)]
    upd_big = []
    for i, (w, m, v) in enumerate(names_big):
        res = _adamw(shard2d(w), shard2d(m), shard2d(v), [others[i]], chip_sums=sums[i], name=f"adamw_{i}")
        upd_big.append([r.reshape(w.shape) for r in res])

    soffs = [0]
    for s in small_sizes:
        soffs.append(soffs[-1] + s)

    def small_piece(i):
        return small_everyone[:, soffs[i]:soffs[i + 1], :]

    cw_cols = a_conv_w.shape[2]
    conv_piece = small_piece(1).reshape(N_DEV, CONV_W, C)
    conv_piece = lax.dynamic_slice_in_dim(conv_piece, me * cw_cols, cw_cols, axis=2)
    conv_piece = conv_piece.reshape(N_DEV, CONV_W * cw_cols // LANES, LANES)
    bn_piece = lax.dynamic_slice_in_dim(small_piece(7), me * bn_rows, bn_rows, axis=1)
    small_g = jnp.concatenate([small_piece(0), conv_piece, small_piece(2), small_piece(3), small_piece(4),
                               small_piece(5), small_piece(6), bn_piece, small_piece(8)], axis=1)
    small_w = [(a_norm, m_a_norm, v_a_norm), (a_conv_w, m_a_conv_w, v_a_conv_w),
               (a_conv_b, m_a_conv_b, v_a_conv_b), (a_b_r, m_a_b_r, v_a_b_r), (a_b_i, m_a_b_i, v_a_b_i),
               (a_lambda, m_a_lambda, v_a_lambda), (kv_norm, m_kv_norm, v_kv_norm),
               (b_norm, m_b_norm, v_b_norm), (final_norm, m_final_norm, v_final_norm)]
    pack = lambda idx: jnp.concatenate([_rows(t[idx]) for t in small_w], axis=0)
    res_small = _adamw(pack(0), pack(1), pack(2), [small_g], name="adamw_small")
    woffs = [0]
    for t in small_w:
        woffs.append(woffs[-1] + t[0].size // LANES)
    upd_small = [[r[woffs[i]:woffs[i + 1]].reshape(small_w[i][0].shape) for r in res_small]
                 for i in range(len(small_w))]

    order = [("s", 0), ("b", 0), ("s", 1), ("s", 2), ("b", 5), ("s", 3), ("b", 6), ("s", 4), ("s", 5),
             ("b", 1), ("s", 6), ("b", 2), ("s", 7), ("b", 3), ("b", 4), ("s", 8)]
    per_weight = [(upd_big if kind == "b" else upd_small)[i] for kind, i in order]
    loss = lax.psum(loss_part[0, 0], ("x", "y", "c"))
    result = [loss, grad_x[None]]
    for field in range(4):
        result += [u[field] for u in per_weight]
    return tuple(result)
```

```python
import functools
import math

import jax
import jax.numpy as jnp
from jax import lax
from jax.experimental import pallas as pl
from jax.experimental.pallas import tpu as pltpu

F32 = jnp.float32
BF16 = jnp.bfloat16
MESH = pl.DeviceIdType.MESH

EPS = 1e-6
LOG2E = 1.4426950408889634
LRU_C = 8.0
CONV_W = 4
HEAD_DIM = 128
ADAM_LR = 0.001
ADAM_B1 = 0.9
ADAM_B2 = 0.999
ADAM_EPS = 1e-08
ADAM_WD = 0.01
ADAM_STEP = 10

N_DEV = 8
LANES = 128
SUBLANES = 8
VMEM_LIMIT = 56 * 1024 * 1024

ATT_KEY_BLOCK = 256
ATT_QUERY_BLOCK = 512
SCAN_BLOCK = 256
ROW_BLOCK = 256
ANY = pl.BlockSpec(memory_space=pl.ANY)


def _pcall(body, **kw):
    return pl.pallas_call(body, **kw)


def _params(*sem):
    return pltpu.CompilerParams(dimension_semantics=sem, vmem_limit_bytes=VMEM_LIMIT)


def _pick(n, cap):
    if n <= cap:
        return n
    best = None
    for t in range(LANES, cap + 1, LANES):
        if n % t == 0:
            best = t
    assert best is not None, (n, cap)
    return best


def _sigmoid(x):
    return 1.0 / (1.0 + jnp.exp(-x))


def _dot(a, b, ca, cb):
    return lax.dot_general(a, b, (((ca,), (cb,)), ((), ())), preferred_element_type=F32)


def _mm_nn(a, b3, *, name, out_dtype, s_off=0, s_cnt=None, res=None):
    T, K = a.shape
    S, K2, n = b3.shape
    assert K == K2
    s_cnt = S if s_cnt is None else s_cnt
    tm = min(T, 512)
    tn = _pick(n, 1024)
    per = n // tn
    has_res = res is not None

    def body(a_ref, b_ref, *rest):
        o_ref = rest[-1]
        acc = jnp.dot(a_ref[...].astype(BF16), b_ref[...], preferred_element_type=F32)
        if has_res:
            acc = acc + rest[0][...]
        o_ref[...] = acc.astype(out_dtype)

    in_specs = [pl.BlockSpec((tm, K), lambda i, j: (i, 0)),
                pl.BlockSpec((None, K, tn), lambda i, j: (s_off + j // per, 0, j % per))]
    args = [a, b3]
    if has_res:
        in_specs.append(pl.BlockSpec((tm, tn), lambda i, j: (i, j)))
        args.append(res)
    return _pcall(
        body, name=name, grid=(T // tm, s_cnt * per), in_specs=in_specs,
        out_specs=pl.BlockSpec((tm, tn), lambda i, j: (i, j)),
        out_shape=jax.ShapeDtypeStruct((T, s_cnt * n), out_dtype),
        compiler_params=_params("parallel", "parallel"))(*args)


def _mm_nt(a, b, *, name, out_dtype=F32):
    T, K = a.shape
    tm = min(T, 512)
    if b.ndim == 2:
        N, K2 = b.shape
        tk = _pick(K, 2048)
        nk = K // tk
        tn = _pick(N, 1024)
        b_spec = pl.BlockSpec((tn, tk), lambda i, j, k: (j, k))
    else:
        S, N, tk = b.shape
        K2 = S * tk
        nk = S
        tn = _pick(N, 1024)
        b_spec = pl.BlockSpec((None, tn, tk), lambda i, j, k: (k, j, 0))
    assert K == K2

    def body(a_ref, b_ref, o_ref, acc_ref):
        k = pl.program_id(2)
        p = _dot(a_ref[...].astype(BF16), b_ref[...], 1, 1)

        @pl.when(k == 0)
        def _():
            acc_ref[...] = p

        @pl.when(k > 0)
        def _():
            acc_ref[...] += p

        @pl.when(k == nk - 1)
        def _():
            o_ref[...] = acc_ref[...].astype(out_dtype)

    return _pcall(
        body, name=name, grid=(T // tm, N // tn, nk),
        in_specs=[pl.BlockSpec((tm, tk), lambda i, j, k: (i, k)), b_spec],
        out_specs=pl.BlockSpec((tm, tn), lambda i, j, k: (i, j)),
        out_shape=jax.ShapeDtypeStruct((T, N), out_dtype),
        scratch_shapes=[pltpu.VMEM((tm, tn), F32)],
        compiler_params=_params("parallel", "parallel", "arbitrary"))(a, b)


def _mm_tn(a, b, *, name, shards=1):
    T, Ko = a.shape
    T2, N = b.shape
    assert T == T2
    n = N // shards
    tt = min(T, 512)
    tko = _pick(Ko, 1024)
    tn = _pick(n, 1024)
    per = n // tn

    def body(a_ref, b_ref, o_ref):
        t = pl.program_id(2)
        p = _dot(a_ref[...].astype(BF16), b_ref[...].astype(BF16), 0, 0)

        @pl.when(t == 0)
        def _():
            o_ref[...] = p

        @pl.when(t > 0)
        def _():
            o_ref[...] += p

    if shards == 1:
        out_spec = pl.BlockSpec((tko, tn), lambda i, j, t: (i, j))
        out_shape = jax.ShapeDtypeStruct((Ko, N), F32)
    else:
        out_spec = pl.BlockSpec((None, tko, tn), lambda i, j, t: (j // per, i, j % per))
        out_shape = jax.ShapeDtypeStruct((shards, Ko, n), F32)
    return _pcall(
        body, name=name, grid=(Ko // tko, N // tn, T // tt),
        in_specs=[pl.BlockSpec((tt, tko), lambda i, j, t: (t, i)),
                  pl.BlockSpec((tt, tn), lambda i, j, t: (t, j))],
        out_specs=out_spec, out_shape=out_shape,
        compiler_params=_params("parallel", "parallel", "arbitrary"))(a, b)


def _rms_fwd(x, gains, *, name):
    T, D = x.shape
    tm = min(T, ROW_BLOCK)
    n = len(gains)

    def body(x_ref, *refs):
        xv = x_ref[...]
        xh = xv * lax.rsqrt(jnp.mean(xv * xv, axis=-1, keepdims=True) + EPS)
        for g_ref, o_ref in zip(refs[:n], refs[n:]):
            o_ref[...] = (xh * g_ref[...]).astype(BF16)

    row = pl.BlockSpec((tm, D), lambda i: (i, 0))
    vec = pl.BlockSpec((1, D), lambda i: (0, 0))
    return _pcall(
        body, name=name, grid=(T // tm,), in_specs=[row] + [vec] * n, out_specs=[row] * n,
        out_shape=[jax.ShapeDtypeStruct((T, D), BF16)] * n,
        compiler_params=_params("parallel"))(x, *gains)


def _rms_bwd(x, dres, dhs, gains, *, name):
    T, D = x.shape
    tm = min(T, ROW_BLOCK)
    n = len(gains)

    def body(x_ref, dres_ref, *refs):
        dh_refs, g_refs = refs[:n], refs[n:2 * n]
        dx_ref, dg_refs = refs[2 * n], refs[2 * n + 1:]
        i = pl.program_id(0)
        xv = x_ref[...]
        r = lax.rsqrt(jnp.mean(xv * xv, axis=-1, keepdims=True) + EPS)
        xh = xv * r
        dxh = jnp.zeros_like(xv)
        for dh_ref, g_ref, dg_ref in zip(dh_refs, g_refs, dg_refs):
            dh = dh_ref[...]
            part = jnp.sum(dh * xh, axis=0, keepdims=True)

            @pl.when(i == 0)
            def _():
                dg_ref[...] = part

            @pl.when(i > 0)
            def _():
                dg_ref[...] += part

            dxh = dxh + dh * g_ref[...]
        dx_ref[...] = dres_ref[...] + r * (dxh - xh * jnp.mean(dxh * xh, axis=-1, keepdims=True))

    row = pl.BlockSpec((tm, D), lambda i: (i, 0))
    vec = pl.BlockSpec((1, D), lambda i: (0, 0))
    return _pcall(
        body, name=name, grid=(T // tm,), in_specs=[row, row] + [row] * n + [vec] * n,
        out_specs=[row] + [vec] * n,
        out_shape=[jax.ShapeDtypeStruct((T, D), F32)] + [jax.ShapeDtypeStruct((1, D), F32)] * n,
        compiler_params=_params("arbitrary"))(x, dres, *dhs, *gains)


def _final_loss(x, target, gain, *, name):
    T, D = x.shape
    tm = min(T, ROW_BLOCK)

    def body(x_ref, t_ref, g_ref, dx_ref, dg_ref, loss_ref):
        i = pl.program_id(0)
        xv = x_ref[...]
        g = g_ref[...]
        r = lax.rsqrt(jnp.mean(xv * xv, axis=-1, keepdims=True) + EPS)
        xh = xv * r
        err = xh * g - t_ref[...]
        part_loss = 0.5 * jnp.sum(jnp.mean(err * err, axis=-1, keepdims=True), axis=0, keepdims=True)
        dy = err * (1.0 / D)
        part_g = jnp.sum(dy * xh, axis=0, keepdims=True)

        @pl.when(i == 0)
        def _():
            dg_ref[...] = part_g
            loss_ref[...] = jnp.broadcast_to(part_loss, loss_ref.shape)

        @pl.when(i > 0)
        def _():
            dg_ref[...] += part_g
            loss_ref[...] += jnp.broadcast_to(part_loss, loss_ref.shape)

        dxh = dy * g
        dx_ref[...] = r * (dxh - xh * jnp.mean(dxh * xh, axis=-1, keepdims=True))

    row = pl.BlockSpec((tm, D), lambda i: (i, 0))
    vec = pl.BlockSpec((1, D), lambda i: (0, 0))
    return _pcall(
        body, name=name, grid=(T // tm,), in_specs=[row, row, vec],
        out_specs=[row, vec, pl.BlockSpec((1, LANES), lambda i: (0, 0))],
        out_shape=[jax.ShapeDtypeStruct((T, D), F32), jax.ShapeDtypeStruct((1, D), F32),
                   jax.ShapeDtypeStruct((1, LANES), F32)],
        compiler_params=_params("arbitrary"))(x, target, gain)


def _shift_down(x, prev_tail, j, row):
    tb = x.shape[0]
    prev = jnp.tile(prev_tail, (tb // SUBLANES, 1))
    return jnp.where(row >= j, pltpu.roll(x, j, 0), pltpu.roll(prev, j, 0))


def _shift_up(x, next_head, j, row):
    tb = x.shape[0]
    nxt = jnp.tile(next_head, (tb // SUBLANES, 1))
    return jnp.where(row < tb - j, pltpu.roll(x, tb - j, 0), pltpu.roll(nxt, tb - j, 0))


def _lru_gates(xb, wr, wi, br, bi, lam):
    xbb = xb.astype(BF16)
    r = _sigmoid(jnp.dot(xbb, wr, preferred_element_type=F32) + br)
    i = _sigmoid(jnp.dot(xbb, wi, preferred_element_type=F32) + bi)
    sp = jnp.maximum(-lam, 0.0) + jnp.log1p(jnp.exp(-jnp.abs(lam)))
    log_a = (-LRU_C) * r * sp
    a = jnp.exp(log_a)
    a2 = a * a
    mult = jnp.sqrt(jnp.maximum(-jnp.tanh(log_a) * (1.0 + a2), 0.0))
    return xbb, r, i, sp, a, a2, mult


def _acore_fwd(proj, conv_w, conv_b, w_r, w_i, b_r, b_i, lam, *, name):
    T, C2 = proj.shape
    C = C2 // 2
    nb, bw, _ = w_r.shape
    tb = min(T, SCAN_BLOCK)

    def body(xp_ref, gate_ref, cw_ref, cb_ref, wr_ref, wi_ref, br_ref, bi_ref, lam_ref,
             xb_ref, h_ref, yg_ref, tail_ref, hlast_ref):
        t = pl.program_id(1)

        @pl.when(t == 0)
        def _():
            tail_ref[...] = jnp.zeros_like(tail_ref)
            hlast_ref[...] = jnp.zeros_like(hlast_ref)

        row = lax.broadcasted_iota(jnp.int32, (tb, bw), 0)
        xp = xp_ref[...]
        tail = tail_ref[...]
        xb = cb_ref[...] + cw_ref[CONV_W - 1:CONV_W, :] * xp
        for j in range(1, CONV_W):
            xb = xb + cw_ref[CONV_W - 1 - j:CONV_W - j, :] * _shift_down(xp, tail, j, row)
        tail_ref[...] = xp[tb - SUBLANES:, :]
        xb_ref[...] = xb

        _, r, i, sp, a, a2, mult = _lru_gates(xb, wr_ref[...], wi_ref[...], br_ref[...], bi_ref[...],
                                              lam_ref[...])
        ca, cb = a, mult * (i * xb)
        s = 1
        while s < tb:
            m = row >= s
            cb = jnp.where(m, ca * pltpu.roll(cb, s, 0) + cb, cb)
            ca = jnp.where(m, ca * pltpu.roll(ca, s, 0), ca)
            s *= 2
        h = cb + ca * hlast_ref[SUBLANES - 1:SUBLANES, :]
        hlast_ref[...] = h[tb - SUBLANES:, :]
        h_ref[...] = h
        gate = gate_ref[...]
        yg_ref[...] = (h * (gate * _sigmoid(gate))).astype(BF16)

    blk = lambda off: pl.BlockSpec((tb, bw), lambda n, t: (t, off + n))
    vec = pl.BlockSpec((1, bw), lambda n, t: (0, n))
    wspec = pl.BlockSpec((None, bw, bw), lambda n, t: (n, 0, 0))
    return _pcall(
        body, name=name, grid=(nb, T // tb),
        in_specs=[blk(0), blk(nb), pl.BlockSpec((CONV_W, bw), lambda n, t: (0, n)), vec, wspec, wspec,
                  vec, vec, vec],
        out_specs=[blk(0), blk(0), blk(0)],
        out_shape=[jax.ShapeDtypeStruct((T, C), F32), jax.ShapeDtypeStruct((T, C), F32),
                   jax.ShapeDtypeStruct((T, C), BF16)],
        scratch_shapes=[pltpu.VMEM((SUBLANES, bw), F32), pltpu.VMEM((SUBLANES, bw), F32)],
        compiler_params=_params("parallel", "arbitrary"))(
            proj, proj, conv_w, conv_b, w_r, w_i, b_r, b_i, lam)


def _acore_bwd(dyg, proj, xb_all, h_all, conv_w, w_r, w_i, b_r, b_i, lam, *, name):
    T, C2 = proj.shape
    C = C2 // 2
    nb, bw, _ = w_r.shape
    tb = min(T, SCAN_BLOCK)
    nt = T // tb
    per8 = tb // SUBLANES

    def body(dyg_ref, xp_ref, gate_ref, xb_ref, h_ref, xp_prev_ref, h_prev_ref, cw_ref,
             wr_ref, wi_ref, br_ref, bi_ref, lam_ref,
             dxp_ref, dgate_ref, dcw_ref, dcb_ref, dbr_ref, dbi_ref, dlam_ref, dwr_ref, dwi_ref,
             gh_next_ref, a_next_ref, dxb_next_ref):
        step = pl.program_id(1)
        first_block = step == nt - 1

        @pl.when(step == 0)
        def _():
            gh_next_ref[...] = jnp.zeros_like(gh_next_ref)
            a_next_ref[...] = jnp.zeros_like(a_next_ref)
            dxb_next_ref[...] = jnp.zeros_like(dxb_next_ref)

        row = lax.broadcasted_iota(jnp.int32, (tb, bw), 0)
        keep = jnp.where(first_block, 0.0, 1.0)
        h_prev = h_prev_ref[...] * keep
        xp_prev = xp_prev_ref[...] * keep
        xp, gate, xb, h, dyg_v = xp_ref[...], gate_ref[...], xb_ref[...], h_ref[...], dyg_ref[...]
        lam_v = lam_ref[...]
        wr, wi = wr_ref[...], wi_ref[...]

        sg = _sigmoid(gate)
        dh = dyg_v * (gate * sg)
        dgate_ref[...] = (dyg_v * h * (sg * (1.0 + gate * (1.0 - sg)))).astype(BF16)

        xbb, r, i, sp, a, a2, mult = _lru_gates(xb, wr, wi, br_ref[...], bi_ref[...], lam_v)

        cg = dh
        cc = _shift_up(a, a_next_ref[...], 1, row)
        s = 1
        while s < tb:
            m = row < tb - s
            cg = jnp.where(m, cc * pltpu.roll(cg, tb - s, 0) + cg, cg)
            cc = jnp.where(m, cc * pltpu.roll(cc, tb - s, 0), cc)
            s *= 2
        gh = cg + cc * gh_next_ref[0:1, :]
        gh_next_ref[...] = gh[0:SUBLANES, :]
        a_next_ref[...] = a[0:SUBLANES, :]

        da = gh * _shift_down(h, h_prev, 1, row)
        dmult = gh * (i * xb)
        di = gh * mult * xb
        dxb = gh * mult * i
        dla = da * a - dmult * jnp.where(mult > 0.0, a2 / mult, 0.0)
        dr = dla * ((-LRU_C) * sp)
        dsp = jnp.sum(dla * ((-LRU_C) * r), axis=0, keepdims=True)
        dlam_part = dsp * (-_sigmoid(-lam_v))
        dpr = dr * r * (1.0 - r)
        dpi = di * i * (1.0 - i)
        dbr_part = jnp.sum(dpr, axis=0, keepdims=True)
        dbi_part = jnp.sum(dpi, axis=0, keepdims=True)
        dprb, dpib = dpr.astype(BF16), dpi.astype(BF16)
        dwr_part = _dot(xbb, dprb, 0, 0)
        dwi_part = _dot(xbb, dpib, 0, 0)
        dxb = dxb + _dot(dprb, wr, 1, 1) + _dot(dpib, wi, 1, 1)

        dxb_next = dxb_next_ref[...]
        dxp = cw_ref[CONV_W - 1:CONV_W, :] * dxb
        for j in range(1, CONV_W):
            dxp = dxp + cw_ref[CONV_W - 1 - j:CONV_W - j, :] * _shift_up(dxb, dxb_next, j, row)
        dxb_next_ref[...] = dxb[0:SUBLANES, :]
        dxp_ref[...] = dxp.astype(BF16)
        dcb_part = jnp.sum(dxb, axis=0, keepdims=True)
        dcw_rows = []
        for k in range(CONV_W):
            j = CONV_W - 1 - k
            sh = xp if j == 0 else _shift_down(xp, xp_prev, j, row)
            dcw_rows.append(jnp.sum(dxb * sh, axis=0, keepdims=True))

        @pl.when(step == 0)
        def _():
            for k in range(CONV_W):
                dcw_ref[k:k + 1, :] = dcw_rows[k]
            dcb_ref[...] = dcb_part
            dbr_ref[...] = dbr_part
            dbi_ref[...] = dbi_part
            dlam_ref[...] = dlam_part
            dwr_ref[...] = dwr_part
            dwi_ref[...] = dwi_part

        @pl.when(step > 0)
        def _():
            for k in range(CONV_W):
                dcw_ref[k:k + 1, :] += dcw_rows[k]
            dcb_ref[...] += dcb_part
            dbr_ref[...] += dbr_part
            dbi_ref[...] += dbi_part
            dlam_ref[...] += dlam_part
            dwr_ref[...] += dwr_part
            dwi_ref[...] += dwi_part

    rev = lambda s: nt - 1 - s
    blk = lambda off: pl.BlockSpec((tb, bw), lambda n, s: (rev(s), off + n))
    prev8 = lambda off: pl.BlockSpec(
        (SUBLANES, bw), lambda n, s: (jnp.maximum(rev(s) * per8 - 1, 0), off + n))
    vec = pl.BlockSpec((1, bw), lambda n, s: (0, n))
    wspec = pl.BlockSpec((None, bw, bw), lambda n, s: (n, 0, 0))
    cwspec = pl.BlockSpec((CONV_W, bw), lambda n, s: (0, n))
    vshape = jax.ShapeDtypeStruct((1, C), F32)
    wshape = jax.ShapeDtypeStruct((nb, bw, bw), F32)
    return _pcall(
        body, name=name, grid=(nb, nt),
        in_specs=[blk(0), blk(0), blk(nb), blk(0), blk(0), prev8(0), prev8(0), cwspec,
                  wspec, wspec, vec, vec, vec],
        out_specs=[blk(0), blk(0), cwspec, vec, vec, vec, vec, wspec, wspec],
        out_shape=[jax.ShapeDtypeStruct((T, C), BF16), jax.ShapeDtypeStruct((T, C), BF16),
                   jax.ShapeDtypeStruct((CONV_W, C), F32), vshape, vshape, vshape, vshape,
                   wshape, wshape],
        scratch_shapes=[pltpu.VMEM((SUBLANES, bw), F32)] * 3,
        compiler_params=_params("parallel", "arbitrary"))(
            dyg, proj, proj, xb_all, h_all, proj, h_all, conv_w, w_r, w_i, b_r, b_i, lam)


def _split_cumsum(lk, tri):
    hi = lk.astype(BF16)
    lo = (lk - hi.astype(F32)).astype(BF16)
    return (jnp.dot(hi, tri, preferred_element_type=F32) + jnp.dot(lo, tri, preferred_element_type=F32))


def _log2_sigmoids(y):
    t = jnp.log(1.0 + jnp.exp2(-jnp.abs(y))) * LOG2E
    ls = jnp.minimum(y, 0.0) - t
    return ls, ls - y


def _attn_blocks(T):
    bk = min(T, ATT_KEY_BLOCK)
    bq = min(T, ATT_QUERY_BLOCK)
    return bq, bk, bq // bk


def _attn_fwd(q, kv, gate, *, name):
    T, HD = q.shape
    H = HD // HEAD_DIM
    bq, bk, per = _attn_blocks(T)
    scale = 1.0 / math.sqrt(HEAD_DIM)

    def body(q_ref, k_ref, v_ref, g_ref, o_ref, og_ref, lt_ref):
        i = pl.program_id(1)
        qv = q_ref[...]
        tr = lax.broadcasted_iota(jnp.int32, (bk, bk), 0)
        tc = lax.broadcasted_iota(jnp.int32, (bk, bk), 1)
        tri = (tr > tc).astype(BF16)
        ahead = (lax.broadcasted_iota(jnp.int32, (bq, bk), 0)
                 - lax.broadcasted_iota(jnp.int32, (bq, bk), 1))

        def group(top, acc, c, mask):
            starts = [pl.multiple_of((top - d) * bk, bk) for d in range(per)]
            zs = [_dot(qv, k_ref[pl.ds(ks, bk), :], 1, 1) for ks in starts]
            lss, sums, css, causals = [], [], [], []
            for ks, z in zip(starts, zs):
                ls, lk = _log2_sigmoids(z * (scale * LOG2E))
                if mask:
                    causals.append(ahead > ks - i * bq)
                    lk = jnp.where(causals[-1], lk, 0.0)
                lss.append(ls)
                sums.append(jnp.sum(lk, axis=1, keepdims=True))
                css.append(_split_cumsum(lk, tri))
            for d, ks in enumerate(starts):
                w = jnp.exp2(lss[d] + (css[d] + c))
                if mask:
                    w = jnp.where(causals[d], w, 0.0)
                acc = acc + jnp.dot(w.astype(BF16), v_ref[pl.ds(ks, bk), :], preferred_element_type=F32)
                c = c + sums[d]
            return acc, c

        state = group(i * per + per - 1, jnp.zeros((bq, HEAD_DIM), F32), jnp.zeros((bq, 1), F32), True)
        acc, c = lax.fori_loop(0, i, lambda gg, s: group((i - gg) * per - 1, s[0], s[1], False), state)
        o_ref[...] = acc
        g = g_ref[...]
        og_ref[...] = (acc * (g * _sigmoid(g))).astype(BF16)
        lt_ref[...] = jnp.broadcast_to(c, (bq, HEAD_DIM))

    qspec = pl.BlockSpec((bq, HEAD_DIM), lambda h, i: (i, h))
    return _pcall(
        body, name=name, grid=(H, T // bq),
        in_specs=[qspec, pl.BlockSpec((T, HEAD_DIM), lambda h, i: (0, h)),
                  pl.BlockSpec((T, HEAD_DIM), lambda h, i: (0, H + h)), qspec],
        out_specs=[qspec, qspec, qspec],
        out_shape=[jax.ShapeDtypeStruct((T, HD), F32), jax.ShapeDtypeStruct((T, HD), BF16),
                   jax.ShapeDtypeStruct((T, HD), F32)],
        compiler_params=_params("parallel", "arbitrary"))(q, kv, kv, gate)


def _attn_bwd(q, kv, gate, o, ltot, dog, *, name):
    T, HD = q.shape
    H = HD // HEAD_DIM
    bq, bk, per = _attn_blocks(T)
    nq = T // bq
    scale = 1.0 / math.sqrt(HEAD_DIM)

    def body(q_ref, k_ref, v_ref, g_ref, o_ref, lt_ref, dog_ref,
             dq_ref, dg_ref, dk_ref, dv_ref, dk_acc, dv_acc):
        i = pl.program_id(1)

        @pl.when(i == 0)
        def _():
            dk_acc[...] = jnp.zeros_like(dk_acc)
            dv_acc[...] = jnp.zeros_like(dv_acc)

        qv = q_ref[...]
        g, ov, dogv = g_ref[...], o_ref[...], dog_ref[...]
        sg = _sigmoid(g)
        do = dogv * (g * sg)
        dg_ref[...] = (dogv * ov * (sg * (1.0 + g * (1.0 - sg)))).astype(BF16)
        dob = do.astype(BF16)
        ltot_v = lt_ref[:, 0:1]
        tr = lax.broadcasted_iota(jnp.int32, (bk, bk), 0)
        tc = lax.broadcasted_iota(jnp.int32, (bk, bk), 1)
        tri_incl = (tr <= tc).astype(BF16)
        tri_excl = (tr < tc).astype(BF16)
        ahead = (lax.broadcasted_iota(jnp.int32, (bq, bk), 0)
                 - lax.broadcasted_iota(jnp.int32, (bq, bk), 1))

        def group(first, dq, p_lk, p_g, mask):
            starts = [pl.multiple_of((first + d) * bk, bk) for d in range(per)]
            zs = [_dot(qv, k_ref[pl.ds(ks, bk), :], 1, 1) for ks in starts]
            dws = [_dot(dob, v_ref[pl.ds(ks, bk), :], 1, 1) for ks in starts]
            lss, css, causals = [], [], []
            for ks, z in zip(starts, zs):
                ls, lk = _log2_sigmoids(z * (scale * LOG2E))
                if mask:
                    causals.append(ahead > ks - i * bq)
                    lk = jnp.where(causals[-1], lk, 0.0)
                lss.append(ls)
                css.append((p_lk, _split_cumsum(lk, tri_incl)))
                p_lk = p_lk + jnp.sum(lk, axis=1, keepdims=True)
            gms, wbs, befores = [], [], []
            for d in range(per):
                w = jnp.exp2(lss[d] + ((ltot_v - css[d][0]) - css[d][1]))
                if mask:
                    w = jnp.where(causals[d], w, 0.0)
                gm = dws[d] * w
                gms.append(gm)
                wbs.append(w.astype(BF16))
                befores.append(jnp.dot(gm.astype(BF16), tri_excl, preferred_element_type=F32) + p_g)
                p_g = p_g + jnp.sum(gm, axis=1, keepdims=True)
            for d, ks in enumerate(starts):
                dz = gms[d] - jnp.exp2(lss[d]) * (gms[d] + befores[d])
                if mask:
                    dz = jnp.where(causals[d], dz, 0.0)
                dzb = (dz * scale).astype(BF16)
                dq = dq + jnp.dot(dzb, k_ref[pl.ds(ks, bk), :], preferred_element_type=F32)
                dk_acc[pl.ds(ks, bk), :] += _dot(dzb, qv, 0, 0)
                dv_acc[pl.ds(ks, bk), :] += _dot(wbs[d], dob, 0, 0)
            return dq, p_lk, p_g

        zero = jnp.zeros((bq, 1), F32)
        state = lax.fori_loop(0, i, lambda gg, s: group(gg * per, s[0], s[1], s[2], False),
                              (jnp.zeros((bq, HEAD_DIM), F32), zero, zero))
        state = group(i * per, state[0], state[1], state[2], True)
        dq_ref[...] = state[0].astype(BF16)

        @pl.when(i == nq - 1)
        def _():
            dk_ref[...] = dk_acc[...].astype(BF16)
            dv_ref[...] = dv_acc[...].astype(BF16)

    qspec = pl.BlockSpec((bq, HEAD_DIM), lambda h, i: (i, h))
    kspec = pl.BlockSpec((T, HEAD_DIM), lambda h, i: (0, h))
    return _pcall(
        body, name=name, grid=(H, nq),
        in_specs=[qspec, kspec, pl.BlockSpec((T, HEAD_DIM), lambda h, i: (0, H + h)),
                  qspec, qspec, qspec, qspec],
        out_specs=[qspec, qspec, kspec, kspec],
        out_shape=[jax.ShapeDtypeStruct((T, HD), BF16)] * 4,
        scratch_shapes=[pltpu.VMEM((T, HEAD_DIM), F32)] * 2,
        compiler_params=_params("parallel", "arbitrary"))(q, kv, kv, gate, o, ltot, dog)


def _position():
    return lax.axis_index("x"), lax.axis_index("y"), lax.axis_index("c")


def _chip_of(k, x, y):
    return (1 - x if k & 1 else x), (1 - y if k & 2 else y)


def _weights_gather(shards):
    n = len(shards)

    def body(*refs):
        ins, outs = refs[:n], refs[n:2 * n]
        send_sems, recv_sems, local_sems = refs[2 * n:]
        x, y, c = _position()
        sibling = (x, y, 1 - c)
        chips = [_chip_of(k, x, y) for k in (1, 2, 3)]

        def copy(a, k, block, to, src=None):
            slot = outs[a].at[4 * block[0] + 2 * block[1] + block[2]]
            return pltpu.make_async_remote_copy(
                src_ref=slot if src is None else src, dst_ref=slot,
                send_sem=send_sems.at[a, k], recv_sem=recv_sems.at[a, k],
                device_id=to, device_id_type=MESH)

        mine = [pltpu.make_async_copy(ins[a], outs[a].at[4 * x + 2 * y + c], local_sems.at[a])
                for a in range(n)]
        for cp in mine:
            cp.start()
        first = []
        for a in range(n):
            first.append(copy(a, 0, (x, y, c), sibling, src=ins[a]))
            first += [copy(a, 1 + j, (x, y, c), (*chip, c), src=ins[a]) for j, chip in enumerate(chips)]
        for cp in first:
            cp.start()
        passed = []
        for j, chip in enumerate(chips):
            for a in range(n):
                copy(a, 1 + j, (*chip, c), (x, y, c)).wait_recv()
                fwd = copy(a, 4 + j, (*chip, c), sibling)
                fwd.start()
                passed.append(fwd)
        for a in range(n):
            copy(a, 0, sibling, (x, y, c)).wait_recv()
            for j, chip in enumerate(chips):
                copy(a, 4 + j, (*chip, 1 - c), (x, y, c)).wait_recv()
        for cp in first + passed:
            cp.wait_send()
        for cp in mine:
            cp.wait()

    return _pcall(
        body, name="weights_gather", in_specs=[ANY] * n, out_specs=[ANY] * n,
        out_shape=[jax.ShapeDtypeStruct((N_DEV,) + s.shape, s.dtype) for s in shards],
        scratch_shapes=[pltpu.SemaphoreType.DMA((n, 7)), pltpu.SemaphoreType.DMA((n, 7)),
                        pltpu.SemaphoreType.DMA((n,))])(*shards)


def _grads_to_sibling(grads, small):
    n = len(grads)

    def body(*refs):
        g_refs, small_ref = refs[:n], refs[n]
        got, small_all = refs[n + 1:2 * n + 1], refs[2 * n + 1]
        send_sems, recv_sems, small_send, small_recv, small_local = refs[2 * n + 2:]
        x, y, c = _position()
        me = 4 * x + 2 * y + c
        remote = []
        for a in range(n):
            for chip in range(4):
                remote.append(pltpu.make_async_remote_copy(
                    src_ref=g_refs[a].at[2 * chip + (1 - c)], dst_ref=got[a].at[chip],
                    send_sem=send_sems.at[a, chip], recv_sem=recv_sems.at[a, chip],
                    device_id=(x, y, 1 - c), device_id_type=MESH))
        peers = []
        for m in range(1, N_DEV):
            px, py, pc = x ^ (m >> 2), y ^ ((m >> 1) & 1), c ^ (m & 1)
            peers.append(pltpu.make_async_remote_copy(
                src_ref=small_ref, dst_ref=small_all.at[me],
                send_sem=small_send.at[m - 1], recv_sem=small_recv.at[m - 1],
                device_id=(px, py, pc), device_id_type=MESH))
        own = pltpu.make_async_copy(small_ref, small_all.at[me], small_local)
        for cp in peers + remote + [own]:
            cp.start()
        for cp in peers + remote:
            cp.wait_send()
        for m in range(1, N_DEV):
            px, py, pc = x ^ (m >> 2), y ^ ((m >> 1) & 1), c ^ (m & 1)
            pltpu.make_async_remote_copy(
                src_ref=small_ref, dst_ref=small_all.at[4 * px + 2 * py + pc],
                send_sem=small_send.at[m - 1], recv_sem=small_recv.at[m - 1],
                device_id=(px, py, pc), device_id_type=MESH).wait_recv()
        for cp in remote:
            cp.wait_recv()
        own.wait()

    part = [jax.ShapeDtypeStruct((4,) + g.shape[1:], g.dtype) for g in grads]
    return _pcall(
        body, name="grads_to_sibling", in_specs=[ANY] * (n + 1), out_specs=[ANY] * (n + 1),
        out_shape=part + [jax.ShapeDtypeStruct((N_DEV,) + small.shape, small.dtype)],
        scratch_shapes=[pltpu.SemaphoreType.DMA((n, 4)), pltpu.SemaphoreType.DMA((n, 4)),
                        pltpu.SemaphoreType.DMA((7,)), pltpu.SemaphoreType.DMA((7,)),
                        pltpu.SemaphoreType.DMA])(*grads, small)


def _grads_to_chips(parts):
    n = len(parts)

    def body(*refs):
        p_refs, outs = refs[:n], refs[n:2 * n]
        send_sems, recv_sems = refs[2 * n:]
        x, y, c = _position()
        copies = []
        for a in range(n):
            for k in range(3):
                cx, cy = _chip_of(k + 1, x, y)
                copies.append(pltpu.make_async_remote_copy(
                    src_ref=p_refs[a].at[2 * cx + cy], dst_ref=outs[a].at[k],
                    send_sem=send_sems.at[a, k], recv_sem=recv_sems.at[a, k],
                    device_id=(cx, cy, c), device_id_type=MESH))
        for cp in copies:
            cp.start()
        for cp in copies:
            cp.wait_send()
        for cp in copies:
            cp.wait_recv()

    return _pcall(
        body, name="grads_to_chips", in_specs=[ANY] * n, out_specs=[ANY] * n,
        out_shape=[jax.ShapeDtypeStruct((3,) + p.shape[1:], p.dtype) for p in parts],
        scratch_shapes=[pltpu.SemaphoreType.DMA((n, 3)), pltpu.SemaphoreType.DMA((n, 3))])(*parts)


def _pair_sum(grad, got, *, name):
    _, R, C = got.shape
    tr = _pick8(R, max(SUBLANES, (1 << 17) // C))

    def body(g_ref, b_ref, o_ref):
        north = lax.axis_index("c") == 1
        for chip in range(4):
            o_ref[chip] = jnp.where(north, g_ref[chip, 1], g_ref[chip, 0]) + b_ref[chip]

    return _pcall(
        body, name=name, grid=(R // tr,),
        in_specs=[pl.BlockSpec((4, 2, tr, C), lambda i: (0, 0, i, 0)),
                  pl.BlockSpec((4, tr, C), lambda i: (0, i, 0))],
        out_specs=pl.BlockSpec((4, tr, C), lambda i: (0, i, 0)),
        out_shape=jax.ShapeDtypeStruct((4, R, C), F32),
        compiler_params=_params("parallel"))(grad.reshape(4, 2, R, C), got)


def _pick8(n, cap):
    if n <= cap:
        return n
    best = None
    for t in range(SUBLANES, cap + 1, SUBLANES):
        if n % t == 0:
            best = t
    assert best is not None, (n, cap)
    return best


def _adamw(w, m, v, parts, *, name, chip_sums=None):
    R, C = w.shape
    tr = _pick8(R, max(SUBLANES, (1 << 17) // C))
    c1 = 1.0 - ADAM_B1 ** ADAM_STEP
    c2 = 1.0 - ADAM_B2 ** ADAM_STEP
    parts = list(parts) if chip_sums is None else [chip_sums] + list(parts)
    np_ = len(parts)

    def body(w_ref, m_ref, v_ref, *refs):
        p_refs = refs[:np_]
        g_ref, d_ref, nm_ref, nv_ref = refs[np_:]
        g = None
        if chip_sums is not None:
            s_ref, p_refs = p_refs[0], p_refs[1:]
            x1, y1 = lax.axis_index("x") == 1, lax.axis_index("y") == 1
            g = jnp.where(x1, jnp.where(y1, s_ref[3], s_ref[2]), jnp.where(y1, s_ref[1], s_ref[0]))
        for p_ref in p_refs:
            for t in [p_ref[k] for k in range(p_ref.shape[0])]:
                g = t if g is None else g + t
        mn = ADAM_B1 * m_ref[...] + (1.0 - ADAM_B1) * g
        vn = ADAM_B2 * v_ref[...] + (1.0 - ADAM_B2) * (g * g)
        d_ref[...] = -ADAM_LR * ((mn / c1) / (jnp.sqrt(vn / c2) + ADAM_EPS) + ADAM_WD * w_ref[...])
        g_ref[...] = g
        nm_ref[...] = mn
        nv_ref[...] = vn

    spec = pl.BlockSpec((tr, C), lambda i: (i, 0))
    pspecs = [pl.BlockSpec((p.shape[0], tr, C), lambda i: (0, i, 0)) for p in parts]
    return _pcall(
        body, name=name, grid=(R // tr,), in_specs=[spec] * 3 + pspecs, out_specs=[spec] * 4,
        out_shape=[jax.ShapeDtypeStruct((R, C), F32)] * 4,
        compiler_params=_params("parallel"))(w, m, v, *parts)


def _rows(a):
    return a.reshape(-1, LANES)


def _forward_backward(xs, target, a_norm, g_a_w_in, conv_w, conv_b, g_w_r, g_w_i, b_r, b_i, lam,
                      g_a_w_out, kv_norm, g_w_kv, b_norm, g_b_w_in, g_b_w_out, final_norm):
    (h_a,) = _rms_fwd(xs, [a_norm], name="a_norm_fwd")
    proj_a = _mm_nn(h_a, g_a_w_in, name="a_in_proj", out_dtype=F32)
    xb, h_rec, yg = _acore_fwd(proj_a, conv_w, conv_b, g_w_r, g_w_i, b_r, b_i, lam, name="a_core_fwd")
    x1 = _mm_nn(yg, g_a_w_out[None], name="a_out_proj", out_dtype=F32, res=xs)
    hk, hb = _rms_fwd(x1, [kv_norm, b_norm], name="kv_b_norm_fwd")
    kv = _mm_nn(hk, g_w_kv, name="kv_proj", out_dtype=BF16)
    half = N_DEV // 2
    q = _mm_nn(hb, g_b_w_in, name="q_proj", out_dtype=BF16, s_off=0, s_cnt=half)
    gate_b = _mm_nn(hb, g_b_w_in, name="b_gate_proj", out_dtype=F32, s_off=half, s_cnt=half)
    o, og, ltot = _attn_fwd(q, kv, gate_b, name="attn_fwd")
    x2 = _mm_nn(og, g_b_w_out[None], name="b_out_proj", out_dtype=F32, res=x1)
    dx2, d_final_norm, loss_part = _final_loss(x2, target, final_norm, name="final_norm_loss")

    dog = _mm_nt(dx2, g_b_w_out, name="b_out_proj_bwd")
    dw_b_out = _mm_tn(og, dx2, name="b_out_proj_wgrad")
    dq, dgate_b, dk, dv = _attn_bwd(q, kv, gate_b, o, ltot, dog, name="attn_bwd")
    dproj_b = jnp.concatenate([dq, dgate_b], axis=1)
    dkv = jnp.concatenate([dk, dv], axis=1)
    dhb = _mm_nt(dproj_b, g_b_w_in, name="b_in_proj_bwd")
    dw_b_in = _mm_tn(hb, dproj_b, name="b_in_proj_wgrad", shards=N_DEV)
    dhk = _mm_nt(dkv, g_w_kv, name="kv_proj_bwd")
    dw_kv = _mm_tn(hk, dkv, name="kv_proj_wgrad", shards=N_DEV)
    dx1, d_b_norm, d_kv_norm = _rms_bwd(x1, dx2, [dhb, dhk], [b_norm, kv_norm], name="kv_b_norm_bwd")
    dyg = _mm_nt(dx1, g_a_w_out, name="a_out_proj_bwd")
    dw_a_out = _mm_tn(yg, dx1, name="a_out_proj_wgrad")
    (dxp, dgate_a, d_conv_w, d_conv_b, d_b_r, d_b_i, d_lambda, dw_r, dw_i) = _acore_bwd(
        dyg, proj_a, xb, h_rec, conv_w, g_w_r, g_w_i, b_r, b_i, lam, name="a_core_bwd")
    dproj_a = jnp.concatenate([dxp, dgate_a], axis=1)
    dh_a = _mm_nt(dproj_a, g_a_w_in, name="a_in_proj_bwd")
    dw_a_in = _mm_tn(h_a, dproj_a, name="a_in_proj_wgrad", shards=N_DEV)
    grad_x, d_a_norm = _rms_bwd(xs, dx1, [dh_a], [a_norm], name="a_norm_bwd")
    return (loss_part, grad_x, dw_a_in, dw_a_out, dw_kv, dw_b_in, dw_b_out, dw_r, dw_i, d_a_norm,
            d_conv_w, d_conv_b, d_b_r, d_b_i, d_lambda, d_kv_norm, d_b_norm, d_final_norm)


def kernel(x, a_norm, a_w_in, a_conv_w, a_conv_b, a_w_r, a_b_r, a_w_i, a_b_i, a_lambda, a_w_out, kv_norm, w_kv, b_norm, b_w_in, b_w_out, final_norm, loss_target, m_a_norm, m_a_w_in, m_a_conv_w, m_a_conv_b, m_a_w_r, m_a_b_r, m_a_w_i, m_a_b_i, m_a_lambda, m_a_w_out, m_kv_norm, m_w_kv, m_b_norm, m_b_w_in, m_b_w_out, m_final_norm, v_a_norm, v_a_w_in, v_a_conv_w, v_a_conv_b, v_a_w_r, v_a_b_r, v_a_w_i, v_a_b_i, v_a_lambda, v_a_w_out, v_kv_norm, v_w_kv, v_b_norm, v_b_w_in, v_b_w_out, v_final_norm):
    T, D = x.shape[1], x.shape[2]
    nb, bw = a_w_r.shape[1], a_w_r.shape[3]
    C = nb * bw
    me = 4 * lax.axis_index("x") + 2 * lax.axis_index("y") + lax.axis_index("c")
    xs = x[0]
    target = loss_target[0]

    big = [a_w_in[0], a_w_out[0], w_kv, b_w_in[0], b_w_out[0], a_w_r[0], a_w_i[0]]
    sizes = [w.size // LANES for w in big]
    packed = jnp.concatenate([_rows(w.astype(BF16)) for w in big], axis=0)
    small_f32 = jnp.concatenate([_rows(a_conv_w[0]), _rows(b_norm[0])], axis=0)
    pad = (-small_f32.shape[0]) % SUBLANES
    small_f32 = jnp.pad(small_f32, ((0, pad), (0, 0)))
    packed_all, small_all = _weights_gather([packed, small_f32])

    offs = [0]
    for s in sizes:
        offs.append(offs[-1] + s)
    pieces = [packed_all[:, offs[i]:offs[i + 1], :] for i in range(len(big))]
    g_a_w_in = pieces[0].reshape(N_DEV, D, a_w_in.shape[2])
    g_a_w_out = pieces[1].reshape(C, D)
    g_w_kv = pieces[2].reshape(N_DEV, D, w_kv.shape[1])
    g_b_w_in = pieces[3].reshape(N_DEV, D, b_w_in.shape[2])
    g_b_w_out = pieces[4].reshape(b_w_out.shape[1] * N_DEV, D)
    rows_r = a_w_r.shape[2]
    g_w_r = pieces[5].reshape(N_DEV, nb, rows_r, bw).transpose(1, 0, 2, 3).reshape(nb, bw, bw)
    g_w_i = pieces[6].reshape(N_DEV, nb, rows_r, bw).transpose(1, 0, 2, 3).reshape(nb, bw, bw)
    cw_rows = a_conv_w.shape[1] * a_conv_w.shape[2] // LANES
    conv_w_full = small_all[:, :cw_rows, :].reshape(N_DEV, CONV_W, a_conv_w.shape[2])
    conv_w_full = conv_w_full.transpose(1, 0, 2).reshape(CONV_W, C)
    bn_rows = b_norm.shape[1] // LANES
    b_norm_full = small_all[:, cw_rows:cw_rows + bn_rows, :].reshape(1, D)
    kv_norm2, final_norm2 = kv_norm.reshape(1, D), final_norm.reshape(1, D)

    (loss_part, grad_x, dw_a_in, dw_a_out, dw_kv, dw_b_in, dw_b_out, dw_r, dw_i, d_a_norm, d_conv_w,
     d_conv_b, d_b_r, d_b_i, d_lambda, d_kv_norm, d_b_norm, d_final_norm) = _forward_backward(
         xs, target, a_norm, g_a_w_in, conv_w_full, a_conv_b, g_w_r, g_w_i, a_b_r, a_b_i, a_lambda,
         g_a_w_out, kv_norm2, g_w_kv, b_norm_full, g_b_w_in, g_b_w_out, final_norm2)

    def lru_shards(dw):
        return dw.reshape(nb, N_DEV, rows_r, bw).transpose(1, 0, 2, 3).reshape(N_DEV, nb * rows_r, bw)

    full = [dw_a_in, dw_a_out.reshape(N_DEV, a_w_out.shape[1], D), dw_kv, dw_b_in,
            dw_b_out.reshape(N_DEV, b_w_out.shape[1], D), lru_shards(dw_r), lru_shards(dw_i)]
    small_parts = [d_a_norm, d_conv_w, d_conv_b, d_b_r, d_b_i, d_lambda, d_kv_norm, d_b_norm, d_final_norm]
    small_sizes = [p.size // LANES for p in small_parts]
    small = jnp.concatenate([_rows(p) for p in small_parts], axis=0)
    outs = _grads_to_sibling(full, small)
    got, small_everyone = outs[:len(full)], outs[-1]
    sums = [_pair_sum(f_, g_, name=f"pair_sum_{i}") for i, (f_, g_) in enumerate(zip(full, got))]
    others = _grads_to_chips(sums)

    def shard2d(w):
        return w.reshape(-1, w.shape[-1])

    names_big = [(a_w_in, m_a_w_in, v_a_w_in), (a_w_out, m_a_w_out, v_a_w_out), (w_kv, m_w_kv, v_w_kv),
                 (b_w_in, m_b_w_in, v_b_w_in), (b_w_out, m_b_w_out, v_b_w_out),
                 (a_w_r, m_a_w_r, v_a_w_r), (a_w_i, m_a_w_i, v_a_w_i)]
    upd_big = []
    for i, (w, m, v) in enumerate(names_big):
        res = _adamw(shard2d(w), shard2d(m), shard2d(v), [others[i]], chip_sums=sums[i], name=f"adamw_{i}")
        upd_big.append([r.reshape(w.shape) for r in res])

    soffs = [0]
    for s in small_sizes:
        soffs.append(soffs[-1] + s)

    def small_piece(i):
        return small_everyone[:, soffs[i]:soffs[i + 1], :]

    cw_cols = a_conv_w.shape[2]
    conv_piece = small_piece(1).reshape(N_DEV, CONV_W, C)
    conv_piece = lax.dynamic_slice_in_dim(conv_piece, me * cw_cols, cw_cols, axis=2)
    conv_piece = conv_piece.reshape(N_DEV, CONV_W * cw_cols // LANES, LANES)
    bn_piece = lax.dynamic_slice_in_dim(small_piece(7), me * bn_rows, bn_rows, axis=1)
    small_g = jnp.concatenate([small_piece(0), conv_piece, small_piece(2), small_piece(3), small_piece(4),
                               small_piece(5), small_piece(6), bn_piece, small_piece(8)], axis=1)
    small_w = [(a_norm, m_a_norm, v_a_norm), (a_conv_w, m_a_conv_w, v_a_conv_w),
               (a_conv_b, m_a_conv_b, v_a_conv_b), (a_b_r, m_a_b_r, v_a_b_r), (a_b_i, m_a_b_i, v_a_b_i),
               (a_lambda, m_a_lambda, v_a_lambda), (kv_norm, m_kv_norm, v_kv_norm),
               (b_norm, m_b_norm, v_b_norm), (final_norm, m_final_norm, v_final_norm)]
    pack = lambda idx: jnp.concatenate([_rows(t[idx]) for t in small_w], axis=0)
    res_small = _adamw(pack(0), pack(1), pack(2), [small_g], name="adamw_small")
    woffs = [0]
    for t in small_w:
        woffs.append(woffs[-1] + t[0].size // LANES)
    upd_small = [[r[woffs[i]:woffs[i + 1]].reshape(small_w[i][0].shape) for r in res_small]
                 for i in range(len(small_w))]

    order = [("s", 0), ("b", 0), ("s", 1), ("s", 2), ("b", 5), ("s", 3), ("b", 6), ("s", 4), ("s", 5),
             ("b", 1), ("s", 6), ("b", 2), ("s", 7), ("b", 3), ("b", 4), ("s", 8)]
    per_weight = [(upd_big if kind == "b" else upd_small)[i] for kind, i in order]
    loss = lax.psum(loss_part[0, 0], ("x", "y", "c"))
    result = [loss, grad_x[None]]
    for field in range(4):
        result += [u[field] for u in per_weight]
    return tuple(result)
```

```python
import functools
import math

import jax
import jax.numpy as jnp
from jax import lax
from jax.experimental import pallas as pl
from jax.experimental.pallas import tpu as pltpu

F32 = jnp.float32
BF16 = jnp.bfloat16
MESH = pl.DeviceIdType.MESH

EPS = 1e-6
LOG2E = 1.4426950408889634
LRU_C = 8.0
CONV_W = 4
HEAD_DIM = 128
ADAM_LR = 0.001
ADAM_B1 = 0.9
ADAM_B2 = 0.999
ADAM_EPS = 1e-08
ADAM_WD = 0.01
ADAM_STEP = 10

N_DEV = 8
LANES = 128
SUBLANES = 8
VMEM_LIMIT = 56 * 1024 * 1024

ATT_KEY_BLOCK = 256
ATT_QUERY_BLOCK = 512
SCAN_BLOCK = 256
ROW_BLOCK = 256
ANY = pl.BlockSpec(memory_space=pl.ANY)


def _pcall(body, **kw):
    return pl.pallas_call(body, **kw)


def _params(*sem):
    return pltpu.CompilerParams(dimension_semantics=sem, vmem_limit_bytes=VMEM_LIMIT)


def _pick(n, cap):
    if n <= cap:
        return n
    best = None
    for t in range(LANES, cap + 1, LANES):
        if n % t == 0:
            best = t
    assert best is not None, (n, cap)
    return best


def _sigmoid(x):
    return 1.0 / (1.0 + jnp.exp(-x))


def _dot(a, b, ca, cb):
    return lax.dot_general(a, b, (((ca,), (cb,)), ((), ())), preferred_element_type=F32)


def _mm_nn(a, b3, *, name, out_dtype, s_off=0, s_cnt=None, res=None):
    T, K = a.shape
    S, K2, n = b3.shape
    assert K == K2
    s_cnt = S if s_cnt is None else s_cnt
    tm = min(T, 512)
    tn = _pick(n, 1024)
    per = n // tn
    has_res = res is not None

    def body(a_ref, b_ref, *rest):
        o_ref = rest[-1]
        acc = jnp.dot(a_ref[...].astype(BF16), b_ref[...], preferred_element_type=F32)
        if has_res:
            acc = acc + rest[0][...]
        o_ref[...] = acc.astype(out_dtype)

    in_specs = [pl.BlockSpec((tm, K), lambda i, j: (i, 0)),
                pl.BlockSpec((None, K, tn), lambda i, j: (s_off + j // per, 0, j % per))]
    args = [a, b3]
    if has_res:
        in_specs.append(pl.BlockSpec((tm, tn), lambda i, j: (i, j)))
        args.append(res)
    return _pcall(
        body, name=name, grid=(T // tm, s_cnt * per), in_specs=in_specs,
        out_specs=pl.BlockSpec((tm, tn), lambda i, j: (i, j)),
        out_shape=jax.ShapeDtypeStruct((T, s_cnt * n), out_dtype),
        compiler_params=_params("parallel", "parallel"))(*args)


def _mm_nt(a, b, *, name, out_dtype=F32):
    T, K = a.shape
    tm = min(T, 512)
    if b.ndim == 2:
        N, K2 = b.shape
        tk = _pick(K, 2048)
        nk = K // tk
        tn = _pick(N, 1024)
        b_spec = pl.BlockSpec((tn, tk), lambda i, j, k: (j, k))
    else:
        S, N, tk = b.shape
        K2 = S * tk
        nk = S
        tn = _pick(N, 1024)
        b_spec = pl.BlockSpec((None, tn, tk), lambda i, j, k: (k, j, 0))
    assert K == K2

    def body(a_ref, b_ref, o_ref, acc_ref):
        k = pl.program_id(2)
        p = _dot(a_ref[...].astype(BF16), b_ref[...], 1, 1)

        @pl.when(k == 0)
        def _():
            acc_ref[...] = p

        @pl.when(k > 0)
        def _():
            acc_ref[...] += p

        @pl.when(k == nk - 1)
        def _():
            o_ref[...] = acc_ref[...].astype(out_dtype)

    return _pcall(
        body, name=name, grid=(T // tm, N // tn, nk),
        in_specs=[pl.BlockSpec((tm, tk), lambda i, j, k: (i, k)), b_spec],
        out_specs=pl.BlockSpec((tm, tn), lambda i, j, k: (i, j)),
        out_shape=jax.ShapeDtypeStruct((T, N), out_dtype),
        scratch_shapes=[pltpu.VMEM((tm, tn), F32)],
        compiler_params=_params("parallel", "parallel", "arbitrary"))(a, b)


def _mm_tn(a, b, *, name, shards=1):
    T, Ko = a.shape
    T2, N = b.shape
    assert T == T2
    n = N // shards
    tt = min(T, 512)
    tko = _pick(Ko, 1024)
    tn = _pick(n, 1024)
    per = n // tn

    def body(a_ref, b_ref, o_ref):
        t = pl.program_id(2)
        p = _dot(a_ref[...].astype(BF16), b_ref[...].astype(BF16), 0, 0)

        @pl.when(t == 0)
        def _():
            o_ref[...] = p

        @pl.when(t > 0)
        def _():
            o_ref[...] += p

    if shards == 1:
        out_spec = pl.BlockSpec((tko, tn), lambda i, j, t: (i, j))
        out_shape = jax.ShapeDtypeStruct((Ko, N), F32)
    else:
        out_spec = pl.BlockSpec((None, tko, tn), lambda i, j, t: (j // per, i, j % per))
        out_shape = jax.ShapeDtypeStruct((shards, Ko, n), F32)
    return _pcall(
        body, name=name, grid=(Ko // tko, N // tn, T // tt),
        in_specs=[pl.BlockSpec((tt, tko), lambda i, j, t: (t, i)),
                  pl.BlockSpec((tt, tn), lambda i, j, t: (t, j))],
        out_specs=out_spec, out_shape=out_shape,
        compiler_params=_params("parallel", "parallel", "arbitrary"))(a, b)


def _rms_fwd(x, gains, *, name):
    T, D = x.shape
    tm = min(T, ROW_BLOCK)
    n = len(gains)

    def body(x_ref, *refs):
        xv = x_ref[...]
        xh = xv * lax.rsqrt(jnp.mean(xv * xv, axis=-1, keepdims=True) + EPS)
        for g_ref, o_ref in zip(refs[:n], refs[n:]):
            o_ref[...] = (xh * g_ref[...]).astype(BF16)

    row = pl.BlockSpec((tm, D), lambda i: (i, 0))
    vec = pl.BlockSpec((1, D), lambda i: (0, 0))
    return _pcall(
        body, name=name, grid=(T // tm,), in_specs=[row] + [vec] * n, out_specs=[row] * n,
        out_shape=[jax.ShapeDtypeStruct((T, D), BF16)] * n,
        compiler_params=_params("parallel"))(x, *gains)


def _rms_bwd(x, dres, dhs, gains, *, name):
    T, D = x.shape
    tm = min(T, ROW_BLOCK)
    n = len(gains)

    def body(x_ref, dres_ref, *refs):
        dh_refs, g_refs = refs[:n], refs[n:2 * n]
        dx_ref, dg_refs = refs[2 * n], refs[2 * n + 1:]
        i = pl.program_id(0)
        xv = x_ref[...]
        r = lax.rsqrt(jnp.mean(xv * xv, axis=-1, keepdims=True) + EPS)
        xh = xv * r
        dxh = jnp.zeros_like(xv)
        for dh_ref, g_ref, dg_ref in zip(dh_refs, g_refs, dg_refs):
            dh = dh_ref[...]
            part = jnp.sum(dh * xh, axis=0, keepdims=True)

            @pl.when(i == 0)
            def _():
                dg_ref[...] = part

            @pl.when(i > 0)
            def _():
                dg_ref[...] += part

            dxh = dxh + dh * g_ref[...]
        dx_ref[...] = dres_ref[...] + r * (dxh - xh * jnp.mean(dxh * xh, axis=-1, keepdims=True))

    row = pl.BlockSpec((tm, D), lambda i: (i, 0))
    vec = pl.BlockSpec((1, D), lambda i: (0, 0))
    return _pcall(
        body, name=name, grid=(T // tm,), in_specs=[row, row] + [row] * n + [vec] * n,
        out_specs=[row] + [vec] * n,
        out_shape=[jax.ShapeDtypeStruct((T, D), F32)] + [jax.ShapeDtypeStruct((1, D), F32)] * n,
        compiler_params=_params("arbitrary"))(x, dres, *dhs, *gains)


def _final_loss(x, target, gain, *, name):
    T, D = x.shape
    tm = min(T, ROW_BLOCK)

    def body(x_ref, t_ref, g_ref, dx_ref, dg_ref, loss_ref):
        i = pl.program_id(0)
        xv = x_ref[...]
        g = g_ref[...]
        r = lax.rsqrt(jnp.mean(xv * xv, axis=-1, keepdims=True) + EPS)
        xh = xv * r
        err = xh * g - t_ref[...]
        part_loss = 0.5 * jnp.sum(jnp.mean(err * err, axis=-1, keepdims=True), axis=0, keepdims=True)
        dy = err * (1.0 / D)
        part_g = jnp.sum(dy * xh, axis=0, keepdims=True)

        @pl.when(i == 0)
        def _():
            dg_ref[...] = part_g
            loss_ref[...] = jnp.broadcast_to(part_loss, loss_ref.shape)

        @pl.when(i > 0)
        def _():
            dg_ref[...] += part_g
            loss_ref[...] += jnp.broadcast_to(part_loss, loss_ref.shape)

        dxh = dy * g
        dx_ref[...] = r * (dxh - xh * jnp.mean(dxh * xh, axis=-1, keepdims=True))

    row = pl.BlockSpec((tm, D), lambda i: (i, 0))
    vec = pl.BlockSpec((1, D), lambda i: (0, 0))
    return _pcall(
        body, name=name, grid=(T // tm,), in_specs=[row, row, vec],
        out_specs=[row, vec, pl.BlockSpec((1, LANES), lambda i: (0, 0))],
        out_shape=[jax.ShapeDtypeStruct((T, D), F32), jax.ShapeDtypeStruct((1, D), F32),
                   jax.ShapeDtypeStruct((1, LANES), F32)],
        compiler_params=_params("arbitrary"))(x, target, gain)


def _shift_down(x, prev_tail, j, row):
    tb = x.shape[0]
    prev = jnp.tile(prev_tail, (tb // SUBLANES, 1))
    return jnp.where(row >= j, pltpu.roll(x, j, 0), pltpu.roll(prev, j, 0))


def _shift_up(x, next_head, j, row):
    tb = x.shape[0]
    nxt = jnp.tile(next_head, (tb // SUBLANES, 1))
    return jnp.where(row < tb - j, pltpu.roll(x, tb - j, 0), pltpu.roll(nxt, tb - j, 0))


def _lru_gates(xb, wr, wi, br, bi, lam):
    xbb = xb.astype(BF16)
    r = _sigmoid(jnp.dot(xbb, wr, preferred_element_type=F32) + br)
    i = _sigmoid(jnp.dot(xbb, wi, preferred_element_type=F32) + bi)
    sp = jnp.maximum(-lam, 0.0) + jnp.log1p(jnp.exp(-jnp.abs(lam)))
    log_a = (-LRU_C) * r * sp
    a = jnp.exp(log_a)
    a2 = a * a
    mult = jnp.sqrt(jnp.maximum(-jnp.tanh(log_a) * (1.0 + a2), 0.0))
    return xbb, r, i, sp, a, a2, mult


def _acore_fwd(proj, conv_w, conv_b, w_r, w_i, b_r, b_i, lam, *, name):
    T, C2 = proj.shape
    C = C2 // 2
    nb, bw, _ = w_r.shape
    tb = min(T, SCAN_BLOCK)

    def body(xp_ref, gate_ref, cw_ref, cb_ref, wr_ref, wi_ref, br_ref, bi_ref, lam_ref,
             xb_ref, h_ref, yg_ref, tail_ref, hlast_ref):
        t = pl.program_id(1)

        @pl.when(t == 0)
        def _():
            tail_ref[...] = jnp.zeros_like(tail_ref)
            hlast_ref[...] = jnp.zeros_like(hlast_ref)

        row = lax.broadcasted_iota(jnp.int32, (tb, bw), 0)
        xp = xp_ref[...]
        tail = tail_ref[...]
        xb = cb_ref[...] + cw_ref[CONV_W - 1:CONV_W, :] * xp
        for j in range(1, CONV_W):
            xb = xb + cw_ref[CONV_W - 1 - j:CONV_W - j, :] * _shift_down(xp, tail, j, row)
        tail_ref[...] = xp[tb - SUBLANES:, :]
        xb_ref[...] = xb

        _, r, i, sp, a, a2, mult = _lru_gates(xb, wr_ref[...], wi_ref[...], br_ref[...], bi_ref[...],
                                              lam_ref[...])
        ca, cb = a, mult * (i * xb)
        s = 1
        while s < tb:
            m = row >= s
            cb = jnp.where(m, ca * pltpu.roll(cb, s, 0) + cb, cb)
            ca = jnp.where(m, ca * pltpu.roll(ca, s, 0), ca)
            s *= 2
        h = cb + ca * hlast_ref[SUBLANES - 1:SUBLANES, :]
        hlast_ref[...] = h[tb - SUBLANES:, :]
        h_ref[...] = h
        gate = gate_ref[...]
        yg_ref[...] = (h * (gate * _sigmoid(gate))).astype(BF16)

    blk = lambda off: pl.BlockSpec((tb, bw), lambda n, t: (t, off + n))
    vec = pl.BlockSpec((1, bw), lambda n, t: (0, n))
    wspec = pl.BlockSpec((None, bw, bw), lambda n, t: (n, 0, 0))
    return _pcall(
        body, name=name, grid=(nb, T // tb),
        in_specs=[blk(0), blk(nb), pl.BlockSpec((CONV_W, bw), lambda n, t: (0, n)), vec, wspec, wspec,
                  vec, vec, vec],
        out_specs=[blk(0), blk(0), blk(0)],
        out_shape=[jax.ShapeDtypeStruct((T, C), F32), jax.ShapeDtypeStruct((T, C), F32),
                   jax.ShapeDtypeStruct((T, C), BF16)],
        scratch_shapes=[pltpu.VMEM((SUBLANES, bw), F32), pltpu.VMEM((SUBLANES, bw), F32)],
        compiler_params=_params("parallel", "arbitrary"))(
            proj, proj, conv_w, conv_b, w_r, w_i, b_r, b_i, lam)


def _acore_bwd(dyg, proj, xb_all, h_all, conv_w, w_r, w_i, b_r, b_i, lam, *, name):
    T, C2 = proj.shape
    C = C2 // 2
    nb, bw, _ = w_r.shape
    tb = min(T, SCAN_BLOCK)
    nt = T // tb
    per8 = tb // SUBLANES

    def body(dyg_ref, xp_ref, gate_ref, xb_ref, h_ref, xp_prev_ref, h_prev_ref, cw_ref,
             wr_ref, wi_ref, br_ref, bi_ref, lam_ref,
             dxp_ref, dgate_ref, dcw_ref, dcb_ref, dbr_ref, dbi_ref, dlam_ref, dwr_ref, dwi_ref,
             gh_next_ref, a_next_ref, dxb_next_ref):
        step = pl.program_id(1)
        first_block = step == nt - 1

        @pl.when(step == 0)
        def _():
            gh_next_ref[...] = jnp.zeros_like(gh_next_ref)
            a_next_ref[...] = jnp.zeros_like(a_next_ref)
            dxb_next_ref[...] = jnp.zeros_like(dxb_next_ref)

        row = lax.broadcasted_iota(jnp.int32, (tb, bw), 0)
        keep = jnp.where(first_block, 0.0, 1.0)
        h_prev = h_prev_ref[...] * keep
        xp_prev = xp_prev_ref[...] * keep
        xp, gate, xb, h, dyg_v = xp_ref[...], gate_ref[...], xb_ref[...], h_ref[...], dyg_ref[...]
        lam_v = lam_ref[...]
        wr, wi = wr_ref[...], wi_ref[...]

        sg = _sigmoid(gate)
        dh = dyg_v * (gate * sg)
        dgate_ref[...] = (dyg_v * h * (sg * (1.0 + gate * (1.0 - sg)))).astype(BF16)

        xbb, r, i, sp, a, a2, mult = _lru_gates(xb, wr, wi, br_ref[...], bi_ref[...], lam_v)

        cg = dh
        cc = _shift_up(a, a_next_ref[...], 1, row)
        s = 1
        while s < tb:
            m = row < tb - s
            cg = jnp.where(m, cc * pltpu.roll(cg, tb - s, 0) + cg, cg)
            cc = jnp.where(m, cc * pltpu.roll(cc, tb - s, 0), cc)
            s *= 2
        gh = cg + cc * gh_next_ref[0:1, :]
        gh_next_ref[...] = gh[0:SUBLANES, :]
        a_next_ref[...] = a[0:SUBLANES, :]

        da = gh * _shift_down(h, h_prev, 1, row)
        dmult = gh * (i * xb)
        di = gh * mult * xb
        dxb = gh * mult * i
        dla = da * a - dmult * jnp.where(mult > 0.0, a2 / mult, 0.0)
        dr = dla * ((-LRU_C) * sp)
        dsp = jnp.sum(dla * ((-LRU_C) * r), axis=0, keepdims=True)
        dlam_part = dsp * (-_sigmoid(-lam_v))
        dpr = dr * r * (1.0 - r)
        dpi = di * i * (1.0 - i)
        dbr_part = jnp.sum(dpr, axis=0, keepdims=True)
        dbi_part = jnp.sum(dpi, axis=0, keepdims=True)
        dprb, dpib = dpr.astype(BF16), dpi.astype(BF16)
        dwr_part = _dot(xbb, dprb, 0, 0)
        dwi_part = _dot(xbb, dpib, 0, 0)
        dxb = dxb + _dot(dprb, wr, 1, 1) + _dot(dpib, wi, 1, 1)

        dxb_next = dxb_next_ref[...]
        dxp = cw_ref[CONV_W - 1:CONV_W, :] * dxb
        for j in range(1, CONV_W):
            dxp = dxp + cw_ref[CONV_W - 1 - j:CONV_W - j, :] * _shift_up(dxb, dxb_next, j, row)
        dxb_next_ref[...] = dxb[0:SUBLANES, :]
        dxp_ref[...] = dxp.astype(BF16)
        dcb_part = jnp.sum(dxb, axis=0, keepdims=True)
        dcw_rows = []
        for k in range(CONV_W):
            j = CONV_W - 1 - k
            sh = xp if j == 0 else _shift_down(xp, xp_prev, j, row)
            dcw_rows.append(jnp.sum(dxb * sh, axis=0, keepdims=True))

        @pl.when(step == 0)
        def _():
            for k in range(CONV_W):
                dcw_ref[k:k + 1, :] = dcw_rows[k]
            dcb_ref[...] = dcb_part
            dbr_ref[...] = dbr_part
            dbi_ref[...] = dbi_part
            dlam_ref[...] = dlam_part
            dwr_ref[...] = dwr_part
            dwi_ref[...] = dwi_part

        @pl.when(step > 0)
        def _():
            for k in range(CONV_W):
                dcw_ref[k:k + 1, :] += dcw_rows[k]
            dcb_ref[...] += dcb_part
            dbr_ref[...] += dbr_part
            dbi_ref[...] += dbi_part
            dlam_ref[...] += dlam_part
            dwr_ref[...] += dwr_part
            dwi_ref[...] += dwi_part

    rev = lambda s: nt - 1 - s
    blk = lambda off: pl.BlockSpec((tb, bw), lambda n, s: (rev(s), off + n))
    prev8 = lambda off: pl.BlockSpec(
        (SUBLANES, bw), lambda n, s: (jnp.maximum(rev(s) * per8 - 1, 0), off + n))
    vec = pl.BlockSpec((1, bw), lambda n, s: (0, n))
    wspec = pl.BlockSpec((None, bw, bw), lambda n, s: (n, 0, 0))
    cwspec = pl.BlockSpec((CONV_W, bw), lambda n, s: (0, n))
    vshape = jax.ShapeDtypeStruct((1, C), F32)
    wshape = jax.ShapeDtypeStruct((nb, bw, bw), F32)
    return _pcall(
        body, name=name, grid=(nb, nt),
        in_specs=[blk(0), blk(0), blk(nb), blk(0), blk(0), prev8(0), prev8(0), cwspec,
                  wspec, wspec, vec, vec, vec],
        out_specs=[blk(0), blk(0), cwspec, vec, vec, vec, vec, wspec, wspec],
        out_shape=[jax.ShapeDtypeStruct((T, C), BF16), jax.ShapeDtypeStruct((T, C), BF16),
                   jax.ShapeDtypeStruct((CONV_W, C), F32), vshape, vshape, vshape, vshape,
                   wshape, wshape],
        scratch_shapes=[pltpu.VMEM((SUBLANES, bw), F32)] * 3,
        compiler_params=_params("parallel", "arbitrary"))(
            dyg, proj, proj, xb_all, h_all, proj, h_all, conv_w, w_r, w_i, b_r, b_i, lam)


def _later_sum(lk, tri):
    return jnp.dot(lk.astype(BF16), tri, preferred_element_type=F32)


def _log2_sigmoids(y):
    t = jnp.log(1.0 + jnp.exp2(-jnp.abs(y))) * LOG2E
    ls = jnp.minimum(y, 0.0) - t
    return ls, ls - y


def _attn_blocks(T):
    bk = min(T, ATT_KEY_BLOCK)
    bq = min(T, ATT_QUERY_BLOCK)
    return bq, bk, bq // bk


def _attn_fwd(q, kv, gate, *, name):
    T, HD = q.shape
    H = HD // HEAD_DIM
    bq, bk, per = _attn_blocks(T)
    scale = 1.0 / math.sqrt(HEAD_DIM)

    def body(q_ref, k_ref, v_ref, g_ref, o_ref, og_ref, lt_ref):
        i = pl.program_id(1)
        qv = q_ref[...]
        tr = lax.broadcasted_iota(jnp.int32, (bk, bk), 0)
        tc = lax.broadcasted_iota(jnp.int32, (bk, bk), 1)
        tri = (tr > tc).astype(BF16)
        ahead = (lax.broadcasted_iota(jnp.int32, (bq, bk), 0)
                 - lax.broadcasted_iota(jnp.int32, (bq, bk), 1))

        def group(top, acc, c, mask):
            starts = [pl.multiple_of((top - d) * bk, bk) for d in range(per)]
            zs = [_dot(qv, k_ref[pl.ds(ks, bk), :], 1, 1) for ks in starts]
            lss, sums, css, causals = [], [], [], []
            for ks, z in zip(starts, zs):
                ls, lk = _log2_sigmoids(z * (scale * LOG2E))
                if mask:
                    causals.append(ahead > ks - i * bq)
                    lk = jnp.where(causals[-1], lk, 0.0)
                lss.append(ls)
                sums.append(jnp.sum(lk, axis=1, keepdims=True))
                css.append(_later_sum(lk, tri))
            for d, ks in enumerate(starts):
                w = jnp.exp2(lss[d] + (css[d] + c))
                if mask:
                    w = jnp.where(causals[d], w, 0.0)
                acc = acc + jnp.dot(w.astype(BF16), v_ref[pl.ds(ks, bk), :], preferred_element_type=F32)
                c = c + sums[d]
            return acc, c

        state = group(i * per + per - 1, jnp.zeros((bq, HEAD_DIM), F32), jnp.zeros((bq, 1), F32), True)
        acc, c = lax.fori_loop(0, i, lambda gg, s: group((i - gg) * per - 1, s[0], s[1], False), state)
        o_ref[...] = acc
        g = g_ref[...]
        og_ref[...] = (acc * (g * _sigmoid(g))).astype(BF16)
        lt_ref[...] = jnp.broadcast_to(c, (bq, HEAD_DIM))

    qspec = pl.BlockSpec((bq, HEAD_DIM), lambda h, i: (i, h))
    return _pcall(
        body, name=name, grid=(H, T // bq),
        in_specs=[qspec, pl.BlockSpec((T, HEAD_DIM), lambda h, i: (0, h)),
                  pl.BlockSpec((T, HEAD_DIM), lambda h, i: (0, H + h)), qspec],
        out_specs=[qspec, qspec, qspec],
        out_shape=[jax.ShapeDtypeStruct((T, HD), F32), jax.ShapeDtypeStruct((T, HD), BF16),
                   jax.ShapeDtypeStruct((T, HD), F32)],
        compiler_params=_params("parallel", "arbitrary"))(q, kv, kv, gate)


def _attn_bwd(q, kv, gate, o, ltot, dog, *, name):
    T, HD = q.shape
    H = HD // HEAD_DIM
    bq, bk, per = _attn_blocks(T)
    nq = T // bq
    scale = 1.0 / math.sqrt(HEAD_DIM)

    def body(q_ref, k_ref, v_ref, g_ref, o_ref, lt_ref, dog_ref,
             dq_ref, dg_ref, dk_ref, dv_ref, dk_acc, dv_acc):
        i = pl.program_id(1)

        @pl.when(i == 0)
        def _():
            dk_acc[...] = jnp.zeros_like(dk_acc)
            dv_acc[...] = jnp.zeros_like(dv_acc)

        qv = q_ref[...]
        g, ov, dogv = g_ref[...], o_ref[...], dog_ref[...]
        sg = _sigmoid(g)
        do = dogv * (g * sg)
        dg_ref[...] = (dogv * ov * (sg * (1.0 + g * (1.0 - sg)))).astype(BF16)
        dob = do.astype(BF16)
        ltot_v = lt_ref[:, 0:1]
        tr = lax.broadcasted_iota(jnp.int32, (bk, bk), 0)
        tc = lax.broadcasted_iota(jnp.int32, (bk, bk), 1)
        tri_later = (tr > tc).astype(BF16)
        tri_excl = (tr < tc).astype(BF16)
        ahead = (lax.broadcasted_iota(jnp.int32, (bq, bk), 0)
                 - lax.broadcasted_iota(jnp.int32, (bq, bk), 1))

        def group(first, dq, p_lk, p_g, mask):
            starts = [pl.multiple_of((first + d) * bk, bk) for d in range(per)]
            zs = [_dot(qv, k_ref[pl.ds(ks, bk), :], 1, 1) for ks in starts]
            dws = [_dot(dob, v_ref[pl.ds(ks, bk), :], 1, 1) for ks in starts]
            lss, css, causals = [], [], []
            for ks, z in zip(starts, zs):
                ls, lk = _log2_sigmoids(z * (scale * LOG2E))
                if mask:
                    causals.append(ahead > ks - i * bq)
                    lk = jnp.where(causals[-1], lk, 0.0)
                lss.append(ls)
                p_lk = p_lk + jnp.sum(lk, axis=1, keepdims=True)
                css.append((ltot_v - p_lk) + _later_sum(lk, tri_later))
            gms, wbs, befores = [], [], []
            for d in range(per):
                w = jnp.exp2(lss[d] + css[d])
                if mask:
                    w = jnp.where(causals[d], w, 0.0)
                gm = dws[d] * w
                gms.append(gm)
                wbs.append(w.astype(BF16))
                befores.append(jnp.dot(gm.astype(BF16), tri_excl, preferred_element_type=F32) + p_g)
                p_g = p_g + jnp.sum(gm, axis=1, keepdims=True)
            for d, ks in enumerate(starts):
                dz = gms[d] - jnp.exp2(lss[d]) * (gms[d] + befores[d])
                if mask:
                    dz = jnp.where(causals[d], dz, 0.0)
                dzb = (dz * scale).astype(BF16)
                dq = dq + jnp.dot(dzb, k_ref[pl.ds(ks, bk), :], preferred_element_type=F32)
                dk_acc[pl.ds(ks, bk), :] += _dot(dzb, qv, 0, 0)
                dv_acc[pl.ds(ks, bk), :] += _dot(wbs[d], dob, 0, 0)
            return dq, p_lk, p_g

        zero = jnp.zeros((bq, 1), F32)
        state = lax.fori_loop(0, i, lambda gg, s: group(gg * per, s[0], s[1], s[2], False),
                              (jnp.zeros((bq, HEAD_DIM), F32), zero, zero))
        state = group(i * per, state[0], state[1], state[2], True)
        dq_ref[...] = state[0].astype(BF16)

        @pl.when(i == nq - 1)
        def _():
            dk_ref[...] = dk_acc[...].astype(BF16)
            dv_ref[...] = dv_acc[...].astype(BF16)

    qspec = pl.BlockSpec((bq, HEAD_DIM), lambda h, i: (i, h))
    kspec = pl.BlockSpec((T, HEAD_DIM), lambda h, i: (0, h))
    return _pcall(
        body, name=name, grid=(H, nq),
        in_specs=[qspec, kspec, pl.BlockSpec((T, HEAD_DIM), lambda h, i: (0, H + h)),
                  qspec, qspec, qspec, qspec],
        out_specs=[qspec, qspec, kspec, kspec],
        out_shape=[jax.ShapeDtypeStruct((T, HD), BF16)] * 4,
        scratch_shapes=[pltpu.VMEM((T, HEAD_DIM), F32)] * 2,
        compiler_params=_params("parallel", "arbitrary"))(q, kv, kv, gate, o, ltot, dog)


def _position():
    return lax.axis_index("x"), lax.axis_index("y"), lax.axis_index("c")


def _chip_of(k, x, y):
    return (1 - x if k & 1 else x), (1 - y if k & 2 else y)


def _weights_gather(shards):
    n = len(shards)

    def body(*refs):
        ins, outs = refs[:n], refs[n:2 * n]
        send_sems, recv_sems, local_sems = refs[2 * n:]
        x, y, c = _position()
        sibling = (x, y, 1 - c)
        chips = [_chip_of(k, x, y) for k in (1, 2, 3)]

        def copy(a, k, block, to, src=None):
            slot = outs[a].at[4 * block[0] + 2 * block[1] + block[2]]
            return pltpu.make_async_remote_copy(
                src_ref=slot if src is None else src, dst_ref=slot,
                send_sem=send_sems.at[a, k], recv_sem=recv_sems.at[a, k],
                device_id=to, device_id_type=MESH)

        mine = [pltpu.make_async_copy(ins[a], outs[a].at[4 * x + 2 * y + c], local_sems.at[a])
                for a in range(n)]
        for cp in mine:
            cp.start()
        first = []
        for a in range(n):
            first.append(copy(a, 0, (x, y, c), sibling, src=ins[a]))
            first += [copy(a, 1 + j, (x, y, c), (*chip, c), src=ins[a]) for j, chip in enumerate(chips)]
        for cp in first:
            cp.start()
        passed = []
        for j, chip in enumerate(chips):
            for a in range(n):
                copy(a, 1 + j, (*chip, c), (x, y, c)).wait_recv()
                fwd = copy(a, 4 + j, (*chip, c), sibling)
                fwd.start()
                passed.append(fwd)
        for a in range(n):
            copy(a, 0, sibling, (x, y, c)).wait_recv()
            for j, chip in enumerate(chips):
                copy(a, 4 + j, (*chip, 1 - c), (x, y, c)).wait_recv()
        for cp in first + passed:
            cp.wait_send()
        for cp in mine:
            cp.wait()

    return _pcall(
        body, name="weights_gather", in_specs=[ANY] * n, out_specs=[ANY] * n,
        out_shape=[jax.ShapeDtypeStruct((N_DEV,) + s.shape, s.dtype) for s in shards],
        scratch_shapes=[pltpu.SemaphoreType.DMA((n, 7)), pltpu.SemaphoreType.DMA((n, 7)),
                        pltpu.SemaphoreType.DMA((n,))])(*shards)


def _grads_to_sibling(grads, small):
    n = len(grads)

    def body(*refs):
        g_refs, small_ref = refs[:n], refs[n]
        got, small_all = refs[n + 1:2 * n + 1], refs[2 * n + 1]
        send_sems, recv_sems, small_send, small_recv, small_local = refs[2 * n + 2:]
        x, y, c = _position()
        me = 4 * x + 2 * y + c
        remote = []
        for a in range(n):
            for chip in range(4):
                remote.append(pltpu.make_async_remote_copy(
                    src_ref=g_refs[a].at[2 * chip + (1 - c)], dst_ref=got[a].at[chip],
                    send_sem=send_sems.at[a, chip], recv_sem=recv_sems.at[a, chip],
                    device_id=(x, y, 1 - c), device_id_type=MESH))
        peers = []
        for m in range(1, N_DEV):
            px, py, pc = x ^ (m >> 2), y ^ ((m >> 1) & 1), c ^ (m & 1)
            peers.append(pltpu.make_async_remote_copy(
                src_ref=small_ref, dst_ref=small_all.at[me],
                send_sem=small_send.at[m - 1], recv_sem=small_recv.at[m - 1],
                device_id=(px, py, pc), device_id_type=MESH))
        own = pltpu.make_async_copy(small_ref, small_all.at[me], small_local)
        for cp in peers + remote + [own]:
            cp.start()
        for cp in peers + remote:
            cp.wait_send()
        for m in range(1, N_DEV):
            px, py, pc = x ^ (m >> 2), y ^ ((m >> 1) & 1), c ^ (m & 1)
            pltpu.make_async_remote_copy(
                src_ref=small_ref, dst_ref=small_all.at[4 * px + 2 * py + pc],
                send_sem=small_send.at[m - 1], recv_sem=small_recv.at[m - 1],
                device_id=(px, py, pc), device_id_type=MESH).wait_recv()
        for cp in remote:
            cp.wait_recv()
        own.wait()

    part = [jax.ShapeDtypeStruct((4,) + g.shape[1:], g.dtype) for g in grads]
    return _pcall(
        body, name="grads_to_sibling", in_specs=[ANY] * (n + 1), out_specs=[ANY] * (n + 1),
        out_shape=part + [jax.ShapeDtypeStruct((N_DEV,) + small.shape, small.dtype)],
        scratch_shapes=[pltpu.SemaphoreType.DMA((n, 4)), pltpu.SemaphoreType.DMA((n, 4)),
                        pltpu.SemaphoreType.DMA((7,)), pltpu.SemaphoreType.DMA((7,)),
                        pltpu.SemaphoreType.DMA])(*grads, small)


def _grads_to_chips(parts):
    n = len(parts)

    def body(*refs):
        p_refs, outs = refs[:n], refs[n:2 * n]
        send_sems, recv_sems = refs[2 * n:]
        x, y, c = _position()
        copies = []
        for a in range(n):
            for k in range(3):
                cx, cy = _chip_of(k + 1, x, y)
                copies.append(pltpu.make_async_remote_copy(
                    src_ref=p_refs[a].at[2 * cx + cy], dst_ref=outs[a].at[k],
                    send_sem=send_sems.at[a, k], recv_sem=recv_sems.at[a, k],
                    device_id=(cx, cy, c), device_id_type=MESH))
        for cp in copies:
            cp.start()
        for cp in copies:
            cp.wait_send()
        for cp in copies:
            cp.wait_recv()

    return _pcall(
        body, name="grads_to_chips", in_specs=[ANY] * n, out_specs=[ANY] * n,
        out_shape=[jax.ShapeDtypeStruct((3,) + p.shape[1:], p.dtype) for p in parts],
        scratch_shapes=[pltpu.SemaphoreType.DMA((n, 3)), pltpu.SemaphoreType.DMA((n, 3))])(*parts)


def _pair_sum(grad, got, *, name):
    _, R, C = got.shape
    tr = _pick8(R, max(2 * SUBLANES, (1 << 17) // C))

    def body(g_ref, b_ref, o_ref, ob_ref):
        north = lax.axis_index("c") == 1
        for chip in range(4):
            s = jnp.where(north, g_ref[chip, 1], g_ref[chip, 0]) + b_ref[chip]
            o_ref[chip] = s
            ob_ref[chip] = s.astype(BF16)

    spec = pl.BlockSpec((4, tr, C), lambda i: (0, i, 0))
    return _pcall(
        body, name=name, grid=(R // tr,),
        in_specs=[pl.BlockSpec((4, 2, tr, C), lambda i: (0, 0, i, 0)), spec],
        out_specs=[spec, spec],
        out_shape=[jax.ShapeDtypeStruct((4, R, C), F32), jax.ShapeDtypeStruct((4, R, C), BF16)],
        compiler_params=_params("parallel"))(grad.reshape(4, 2, R, C), got)


def _pick8(n, cap):
    if n <= cap:
        return n
    best = None
    for t in range(SUBLANES, cap + 1, SUBLANES):
        if n % t == 0:
            best = t
    assert best is not None, (n, cap)
    return best


def _adamw(w, m, v, parts, *, name, chip_sums=None):
    R, C = w.shape
    tr = _pick8(R, max(SUBLANES, (1 << 17) // C))
    c1 = 1.0 - ADAM_B1 ** ADAM_STEP
    c2 = 1.0 - ADAM_B2 ** ADAM_STEP
    parts = list(parts) if chip_sums is None else [chip_sums] + list(parts)
    np_ = len(parts)

    def body(w_ref, m_ref, v_ref, *refs):
        p_refs = refs[:np_]
        g_ref, d_ref, nm_ref, nv_ref = refs[np_:]
        g = None
        if chip_sums is not None:
            s_ref, p_refs = p_refs[0], p_refs[1:]
            x1, y1 = lax.axis_index("x") == 1, lax.axis_index("y") == 1
            g = jnp.where(x1, jnp.where(y1, s_ref[3], s_ref[2]), jnp.where(y1, s_ref[1], s_ref[0]))
        for p_ref in p_refs:
            for t in [p_ref[k].astype(F32) for k in range(p_ref.shape[0])]:
                g = t if g is None else g + t
        mn = ADAM_B1 * m_ref[...] + (1.0 - ADAM_B1) * g
        vn = ADAM_B2 * v_ref[...] + (1.0 - ADAM_B2) * (g * g)
        d_ref[...] = -ADAM_LR * ((mn / c1) / (jnp.sqrt(vn / c2) + ADAM_EPS) + ADAM_WD * w_ref[...])
        g_ref[...] = g
        nm_ref[...] = mn
        nv_ref[...] = vn

    spec = pl.BlockSpec((tr, C), lambda i: (i, 0))
    pspecs = [pl.BlockSpec((p.shape[0], tr, C), lambda i: (0, i, 0)) for p in parts]
    return _pcall(
        body, name=name, grid=(R // tr,), in_specs=[spec] * 3 + pspecs, out_specs=[spec] * 4,
        out_shape=[jax.ShapeDtypeStruct((R, C), F32)] * 4,
        compiler_params=_params("parallel"))(w, m, v, *parts)


def _rows(a):
    return a.reshape(-1, LANES)


def _forward_backward(xs, target, a_norm, g_a_w_in, conv_w, conv_b, g_w_r, g_w_i, b_r, b_i, lam,
                      g_a_w_out, kv_norm, g_w_kv, b_norm, g_b_w_in, g_b_w_out, final_norm):
    (h_a,) = _rms_fwd(xs, [a_norm], name="a_norm_fwd")
    proj_a = _mm_nn(h_a, g_a_w_in, name="a_in_proj", out_dtype=F32)
    xb, h_rec, yg = _acore_fwd(proj_a, conv_w, conv_b, g_w_r, g_w_i, b_r, b_i, lam, name="a_core_fwd")
    x1 = _mm_nn(yg, g_a_w_out[None], name="a_out_proj", out_dtype=F32, res=xs)
    hk, hb = _rms_fwd(x1, [kv_norm, b_norm], name="kv_b_norm_fwd")
    kv = _mm_nn(hk, g_w_kv, name="kv_proj", out_dtype=BF16)
    half = N_DEV // 2
    q = _mm_nn(hb, g_b_w_in, name="q_proj", out_dtype=BF16, s_off=0, s_cnt=half)
    gate_b = _mm_nn(hb, g_b_w_in, name="b_gate_proj", out_dtype=F32, s_off=half, s_cnt=half)
    o, og, ltot = _attn_fwd(q, kv, gate_b, name="attn_fwd")
    x2 = _mm_nn(og, g_b_w_out[None], name="b_out_proj", out_dtype=F32, res=x1)
    dx2, d_final_norm, loss_part = _final_loss(x2, target, final_norm, name="final_norm_loss")

    dog = _mm_nt(dx2, g_b_w_out, name="b_out_proj_bwd")
    dw_b_out = _mm_tn(og, dx2, name="b_out_proj_wgrad")
    dq, dgate_b, dk, dv = _attn_bwd(q, kv, gate_b, o, ltot, dog, name="attn_bwd")
    dproj_b = jnp.concatenate([dq, dgate_b], axis=1)
    dkv = jnp.concatenate([dk, dv], axis=1)
    dhb = _mm_nt(dproj_b, g_b_w_in, name="b_in_proj_bwd")
    dw_b_in = _mm_tn(hb, dproj_b, name="b_in_proj_wgrad", shards=N_DEV)
    dhk = _mm_nt(dkv, g_w_kv, name="kv_proj_bwd")
    dw_kv = _mm_tn(hk, dkv, name="kv_proj_wgrad", shards=N_DEV)
    dx1, d_b_norm, d_kv_norm = _rms_bwd(x1, dx2, [dhb, dhk], [b_norm, kv_norm], name="kv_b_norm_bwd")
    dyg = _mm_nt(dx1, g_a_w_out, name="a_out_proj_bwd")
    dw_a_out = _mm_tn(yg, dx1, name="a_out_proj_wgrad")
    (dxp, dgate_a, d_conv_w, d_conv_b, d_b_r, d_b_i, d_lambda, dw_r, dw_i) = _acore_bwd(
        dyg, proj_a, xb, h_rec, conv_w, g_w_r, g_w_i, b_r, b_i, lam, name="a_core_bwd")
    dproj_a = jnp.concatenate([dxp, dgate_a], axis=1)
    dh_a = _mm_nt(dproj_a, g_a_w_in, name="a_in_proj_bwd")
    dw_a_in = _mm_tn(h_a, dproj_a, name="a_in_proj_wgrad", shards=N_DEV)
    grad_x, d_a_norm = _rms_bwd(xs, dx1, [dh_a], [a_norm], name="a_norm_bwd")
    return (loss_part, grad_x, dw_a_in, dw_a_out, dw_kv, dw_b_in, dw_b_out, dw_r, dw_i, d_a_norm,
            d_conv_w, d_conv_b, d_b_r, d_b_i, d_lambda, d_kv_norm, d_b_norm, d_final_norm)


def kernel(x, a_norm, a_w_in, a_conv_w, a_conv_b, a_w_r, a_b_r, a_w_i, a_b_i, a_lambda, a_w_out, kv_norm, w_kv, b_norm, b_w_in, b_w_out, final_norm, loss_target, m_a_norm, m_a_w_in, m_a_conv_w, m_a_conv_b, m_a_w_r, m_a_b_r, m_a_w_i, m_a_b_i, m_a_lambda, m_a_w_out, m_kv_norm, m_w_kv, m_b_norm, m_b_w_in, m_b_w_out, m_final_norm, v_a_norm, v_a_w_in, v_a_conv_w, v_a_conv_b, v_a_w_r, v_a_b_r, v_a_w_i, v_a_b_i, v_a_lambda, v_a_w_out, v_kv_norm, v_w_kv, v_b_norm, v_b_w_in, v_b_w_out, v_final_norm):
    T, D = x.shape[1], x.shape[2]
    nb, bw = a_w_r.shape[1], a_w_r.shape[3]
    C = nb * bw
    me = 4 * lax.axis_index("x") + 2 * lax.axis_index("y") + lax.axis_index("c")
    xs = x[0]
    target = loss_target[0]

    big = [a_w_in[0], a_w_out[0], w_kv, b_w_in[0], b_w_out[0], a_w_r[0], a_w_i[0]]
    sizes = [w.size // LANES for w in big]
    packed = jnp.concatenate([_rows(w.astype(BF16)) for w in big], axis=0)
    small_f32 = jnp.concatenate([_rows(a_conv_w[0]), _rows(b_norm[0])], axis=0)
    pad = (-small_f32.shape[0]) % SUBLANES
    small_f32 = jnp.pad(small_f32, ((0, pad), (0, 0)))
    packed_all, small_all = _weights_gather([packed, small_f32])

    offs = [0]
    for s in sizes:
        offs.append(offs[-1] + s)
    pieces = [packed_all[:, offs[i]:offs[i + 1], :] for i in range(len(big))]
    g_a_w_in = pieces[0].reshape(N_DEV, D, a_w_in.shape[2])
    g_a_w_out = pieces[1].reshape(C, D)
    g_w_kv = pieces[2].reshape(N_DEV, D, w_kv.shape[1])
    g_b_w_in = pieces[3].reshape(N_DEV, D, b_w_in.shape[2])
    g_b_w_out = pieces[4].reshape(b_w_out.shape[1] * N_DEV, D)
    rows_r = a_w_r.shape[2]
    g_w_r = pieces[5].reshape(N_DEV, nb, rows_r, bw).transpose(1, 0, 2, 3).reshape(nb, bw, bw)
    g_w_i = pieces[6].reshape(N_DEV, nb, rows_r, bw).transpose(1, 0, 2, 3).reshape(nb, bw, bw)
    cw_rows = a_conv_w.shape[1] * a_conv_w.shape[2] // LANES
    conv_w_full = small_all[:, :cw_rows, :].reshape(N_DEV, CONV_W, a_conv_w.shape[2])
    conv_w_full = conv_w_full.transpose(1, 0, 2).reshape(CONV_W, C)
    bn_rows = b_norm.shape[1] // LANES
    b_norm_full = small_all[:, cw_rows:cw_rows + bn_rows, :].reshape(1, D)
    kv_norm2, final_norm2 = kv_norm.reshape(1, D), final_norm.reshape(1, D)

    (loss_part, grad_x, dw_a_in, dw_a_out, dw_kv, dw_b_in, dw_b_out, dw_r, dw_i, d_a_norm, d_conv_w,
     d_conv_b, d_b_r, d_b_i, d_lambda, d_kv_norm, d_b_norm, d_final_norm) = _forward_backward(
         xs, target, a_norm, g_a_w_in, conv_w_full, a_conv_b, g_w_r, g_w_i, a_b_r, a_b_i, a_lambda,
         g_a_w_out, kv_norm2, g_w_kv, b_norm_full, g_b_w_in, g_b_w_out, final_norm2)

    def lru_shards(dw):
        return dw.reshape(nb, N_DEV, rows_r, bw).transpose(1, 0, 2, 3).reshape(N_DEV, nb * rows_r, bw)

    full = [dw_a_in, dw_a_out.reshape(N_DEV, a_w_out.shape[1], D), dw_kv, dw_b_in,
            dw_b_out.reshape(N_DEV, b_w_out.shape[1], D), lru_shards(dw_r), lru_shards(dw_i)]
    small_parts = [d_a_norm, d_conv_w, d_conv_b, d_b_r, d_b_i, d_lambda, d_kv_norm, d_b_norm, d_final_norm]
    small_sizes = [p.size // LANES for p in small_parts]
    small = jnp.concatenate([_rows(p) for p in small_parts], axis=0)
    outs = _grads_to_sibling(full, small)
    got, small_everyone = outs[:len(full)], outs[-1]
    sums = [_pair_sum(f_, g_, name=f"pair_sum_{i}") for i, (f_, g_) in enumerate(zip(full, got))]
    others = _grads_to_chips([s[1] for s in sums])

    def shard2d(w):
        return w.reshape(-1, w.shape[-1])

    names_big = [(a_w_in, m_a_w_in, v_a_w_in), (a_w_out, m_a_w_out, v_a_w_out), (w_kv, m_w_kv, v_w_kv),
                 (b_w_in, m_b_w_in, v_b_w_in), (b_w_out, m_b_w_out, v_b_w_out),
                 (a_w_r, m_a_w_r, v_a_w_r), (a_w_i, m_a_w_i, v_a_w_i)]
    upd_big = []
    for i, (w, m, v) in enumerate(names_big):
        res = _adamw(shard2d(w), shard2d(m), shard2d(v), [others[i]], chip_sums=sums[i][0], name=f"adamw_{i}")
        upd_big.append([r.reshape(w.shape) for r in res])

    soffs = [0]
    for s in small_sizes:
        soffs.append(soffs[-1] + s)

    def small_piece(i):
        return small_everyone[:, soffs[i]:soffs[i + 1], :]

    cw_cols = a_conv_w.shape[2]
    conv_piece = small_piece(1).reshape(N_DEV, CONV_W, C)
    conv_piece = lax.dynamic_slice_in_dim(conv_piece, me * cw_cols, cw_cols, axis=2)
    conv_piece = conv_piece.reshape(N_DEV, CONV_W * cw_cols // LANES, LANES)
    bn_piece = lax.dynamic_slice_in_dim(small_piece(7), me * bn_rows, bn_rows, axis=1)
    small_g = jnp.concatenate([small_piece(0), conv_piece, small_piece(2), small_piece(3), small_piece(4),
                               small_piece(5), small_piece(6), bn_piece, small_piece(8)], axis=1)
    small_w = [(a_norm, m_a_norm, v_a_norm), (a_conv_w, m_a_conv_w, v_a_conv_w),
               (a_conv_b, m_a_conv_b, v_a_conv_b), (a_b_r, m_a_b_r, v_a_b_r), (a_b_i, m_a_b_i, v_a_b_i),
               (a_lambda, m_a_lambda, v_a_lambda), (kv_norm, m_kv_norm, v_kv_norm),
               (b_norm, m_b_norm, v_b_norm), (final_norm, m_final_norm, v_final_norm)]
    pack = lambda idx: jnp.concatenate([_rows(t[idx]) for t in small_w], axis=0)
    res_small = _adamw(pack(0), pack(1), pack(2), [small_g], name="adamw_small")
    woffs = [0]
    for t in small_w:
        woffs.append(woffs[-1] + t[0].size // LANES)
    upd_small = [[r[woffs[i]:woffs[i + 1]].reshape(small_w[i][0].shape) for r in res_small]
                 for i in range(len(small_w))]

    order = [("s", 0), ("b", 0), ("s", 1), ("s", 2), ("b", 5), ("s", 3), ("b", 6), ("s", 4), ("s", 5),
             ("b", 1), ("s", 6), ("b", 2), ("s", 7), ("b", 3), ("b", 4), ("s", 8)]
    per_weight = [(upd_big if kind == "b" else upd_small)[i] for kind, i in order]
    loss = lax.psum(loss_part[0, 0], ("x", "y", "c"))
    result = [loss, grad_x[None]]
    for field in range(4):
        result += [u[field] for u in per_weight]
    return tuple(result)
```

```python
import functools
import math

import jax
import jax.numpy as jnp
from jax import lax
from jax.experimental import pallas as pl
from jax.experimental.pallas import tpu as pltpu

F32 = jnp.float32
BF16 = jnp.bfloat16
MESH = pl.DeviceIdType.MESH

EPS = 1e-6
LOG2E = 1.4426950408889634
LRU_C = 8.0
CONV_W = 4
HEAD_DIM = 128
ADAM_LR = 0.001
ADAM_B1 = 0.9
ADAM_B2 = 0.999
ADAM_EPS = 1e-08
ADAM_WD = 0.01
ADAM_STEP = 10

N_DEV = 8
LANES = 128
SUBLANES = 8
VMEM_LIMIT = 56 * 1024 * 1024

ATT_KEY_BLOCK = 256
ATT_QUERY_BLOCK = 512
SCAN_BLOCK = 256
ROW_BLOCK = 256
MM_TOKEN_BLOCK = 512
MM_WEIGHT_TILE = 1280
MM_CONTRACT_TOKENS = 2048
ANY = pl.BlockSpec(memory_space=pl.ANY)


def _pcall(body, **kw):
    return pl.pallas_call(body, **kw)


def _params(*sem):
    return pltpu.CompilerParams(dimension_semantics=sem, vmem_limit_bytes=VMEM_LIMIT)


def _pick(n, cap):
    if n <= cap:
        return n
    best = None
    for t in range(LANES, cap + 1, LANES):
        if n % t == 0:
            best = t
    assert best is not None, (n, cap)
    return best


def _sigmoid(x):
    return 1.0 / (1.0 + jnp.exp(-x))


def _dot(a, b, ca, cb):
    return lax.dot_general(a, b, (((ca,), (cb,)), ((), ())), preferred_element_type=F32)


def _mm_nn(a, w, *, name, out_dtype, col_off=0, cols=None, res=None):
    T, K = a.shape
    K2, N = w.shape
    assert K == K2
    cols = N if cols is None else cols
    tm = min(T, MM_TOKEN_BLOCK)
    tn = _pick(cols, MM_WEIGHT_TILE)
    assert col_off % tn == 0
    off = col_off // tn
    has_res = res is not None

    def body(a_ref, b_ref, *rest):
        o_ref = rest[-1]
        acc = jnp.dot(a_ref[...].astype(BF16), b_ref[...], preferred_element_type=F32)
        if has_res:
            acc = acc + rest[0][...]
        o_ref[...] = acc.astype(out_dtype)

    in_specs = [pl.BlockSpec((tm, K), lambda j, i: (i, 0)),
                pl.BlockSpec((K, tn), lambda j, i: (0, off + j))]
    args = [a, w]
    if has_res:
        in_specs.append(pl.BlockSpec((tm, tn), lambda j, i: (i, j)))
        args.append(res)
    return _pcall(
        body, name=name, grid=(cols // tn, T // tm), in_specs=in_specs,
        out_specs=pl.BlockSpec((tm, tn), lambda j, i: (i, j)),
        out_shape=jax.ShapeDtypeStruct((T, cols), out_dtype),
        compiler_params=_params("parallel", "parallel"))(*args)


def _mm_nt(a, w, *, name, out_dtype=F32):
    T, K = a.shape
    N, K2 = w.shape
    assert K == K2
    tm = min(T, MM_TOKEN_BLOCK)
    tn = _pick(N, MM_WEIGHT_TILE)

    def body(a_ref, b_ref, o_ref):
        o_ref[...] = _dot(a_ref[...].astype(BF16), b_ref[...], 1, 1).astype(out_dtype)

    return _pcall(
        body, name=name, grid=(N // tn, T // tm),
        in_specs=[pl.BlockSpec((tm, K), lambda j, i: (i, 0)), pl.BlockSpec((tn, K), lambda j, i: (j, 0))],
        out_specs=pl.BlockSpec((tm, tn), lambda j, i: (i, j)),
        out_shape=jax.ShapeDtypeStruct((T, N), out_dtype),
        compiler_params=_params("parallel", "parallel"))(a, w)


def _mm_tn(a, b, *, name, shards=1):
    T, Ko = a.shape
    T2, N = b.shape
    assert T == T2
    n = N // shards
    tt = min(T, MM_CONTRACT_TOKENS)
    tko = _pick(Ko, 1024)
    tn = _pick(n, 1024)
    per = n // tn

    def body(a_ref, b_ref, o_ref):
        t = pl.program_id(2)
        p = _dot(a_ref[...].astype(BF16), b_ref[...].astype(BF16), 0, 0)

        @pl.when(t == 0)
        def _():
            o_ref[...] = p

        @pl.when(t > 0)
        def _():
            o_ref[...] += p

    if shards == 1:
        out_spec = pl.BlockSpec((tko, tn), lambda i, j, t: (i, j))
        out_shape = jax.ShapeDtypeStruct((Ko, N), F32)
    else:
        out_spec = pl.BlockSpec((None, tko, tn), lambda i, j, t: (j // per, i, j % per))
        out_shape = jax.ShapeDtypeStruct((shards, Ko, n), F32)
    return _pcall(
        body, name=name, grid=(Ko // tko, N // tn, T // tt),
        in_specs=[pl.BlockSpec((tt, tko), lambda i, j, t: (t, i)),
                  pl.BlockSpec((tt, tn), lambda i, j, t: (t, j))],
        out_specs=out_spec, out_shape=out_shape,
        compiler_params=_params("parallel", "parallel", "arbitrary"))(a, b)


def _rms_fwd(x, gains, *, name):
    T, D = x.shape
    tm = min(T, ROW_BLOCK)
    n = len(gains)

    def body(x_ref, *refs):
        xv = x_ref[...]
        xh = xv * lax.rsqrt(jnp.mean(xv * xv, axis=-1, keepdims=True) + EPS)
        for g_ref, o_ref in zip(refs[:n], refs[n:]):
            o_ref[...] = (xh * g_ref[...]).astype(BF16)

    row = pl.BlockSpec((tm, D), lambda i: (i, 0))
    vec = pl.BlockSpec((1, D), lambda i: (0, 0))
    return _pcall(
        body, name=name, grid=(T // tm,), in_specs=[row] + [vec] * n, out_specs=[row] * n,
        out_shape=[jax.ShapeDtypeStruct((T, D), BF16)] * n,
        compiler_params=_params("parallel"))(x, *gains)


def _rms_bwd(x, dres, dhs, gains, *, name):
    T, D = x.shape
    tm = min(T, ROW_BLOCK)
    n = len(gains)

    def body(x_ref, dres_ref, *refs):
        dh_refs, g_refs = refs[:n], refs[n:2 * n]
        dx_ref, dg_refs = refs[2 * n], refs[2 * n + 1:]
        i = pl.program_id(0)
        xv = x_ref[...]
        r = lax.rsqrt(jnp.mean(xv * xv, axis=-1, keepdims=True) + EPS)
        xh = xv * r
        dxh = jnp.zeros_like(xv)
        for dh_ref, g_ref, dg_ref in zip(dh_refs, g_refs, dg_refs):
            dh = dh_ref[...]
            part = jnp.sum(dh * xh, axis=0, keepdims=True)

            @pl.when(i == 0)
            def _():
                dg_ref[...] = part

            @pl.when(i > 0)
            def _():
                dg_ref[...] += part

            dxh = dxh + dh * g_ref[...]
        dx_ref[...] = dres_ref[...] + r * (dxh - xh * jnp.mean(dxh * xh, axis=-1, keepdims=True))

    row = pl.BlockSpec((tm, D), lambda i: (i, 0))
    vec = pl.BlockSpec((1, D), lambda i: (0, 0))
    return _pcall(
        body, name=name, grid=(T // tm,), in_specs=[row, row] + [row] * n + [vec] * n,
        out_specs=[row] + [vec] * n,
        out_shape=[jax.ShapeDtypeStruct((T, D), F32)] + [jax.ShapeDtypeStruct((1, D), F32)] * n,
        compiler_params=_params("arbitrary"))(x, dres, *dhs, *gains)


def _final_loss(x, target, gain, *, name):
    T, D = x.shape
    tm = min(T, ROW_BLOCK)

    def body(x_ref, t_ref, g_ref, dx_ref, dg_ref, loss_ref):
        i = pl.program_id(0)
        xv = x_ref[...]
        g = g_ref[...]
        r = lax.rsqrt(jnp.mean(xv * xv, axis=-1, keepdims=True) + EPS)
        xh = xv * r
        err = xh * g - t_ref[...]
        part_loss = 0.5 * jnp.sum(jnp.mean(err * err, axis=-1, keepdims=True), axis=0, keepdims=True)
        dy = err * (1.0 / D)
        part_g = jnp.sum(dy * xh, axis=0, keepdims=True)

        @pl.when(i == 0)
        def _():
            dg_ref[...] = part_g
            loss_ref[...] = jnp.broadcast_to(part_loss, loss_ref.shape)

        @pl.when(i > 0)
        def _():
            dg_ref[...] += part_g
            loss_ref[...] += jnp.broadcast_to(part_loss, loss_ref.shape)

        dxh = dy * g
        dx_ref[...] = r * (dxh - xh * jnp.mean(dxh * xh, axis=-1, keepdims=True))

    row = pl.BlockSpec((tm, D), lambda i: (i, 0))
    vec = pl.BlockSpec((1, D), lambda i: (0, 0))
    return _pcall(
        body, name=name, grid=(T // tm,), in_specs=[row, row, vec],
        out_specs=[row, vec, pl.BlockSpec((1, LANES), lambda i: (0, 0))],
        out_shape=[jax.ShapeDtypeStruct((T, D), F32), jax.ShapeDtypeStruct((1, D), F32),
                   jax.ShapeDtypeStruct((1, LANES), F32)],
        compiler_params=_params("arbitrary"))(x, target, gain)


def _shift_down(x, prev_tail, j, row):
    tb = x.shape[0]
    prev = jnp.tile(prev_tail, (tb // SUBLANES, 1))
    return jnp.where(row >= j, pltpu.roll(x, j, 0), pltpu.roll(prev, j, 0))


def _shift_up(x, next_head, j, row):
    tb = x.shape[0]
    nxt = jnp.tile(next_head, (tb // SUBLANES, 1))
    return jnp.where(row < tb - j, pltpu.roll(x, tb - j, 0), pltpu.roll(nxt, tb - j, 0))


def _lru_gates(xb, wr, wi, br, bi, lam):
    xbb = xb.astype(BF16)
    r = _sigmoid(jnp.dot(xbb, wr, preferred_element_type=F32) + br)
    i = _sigmoid(jnp.dot(xbb, wi, preferred_element_type=F32) + bi)
    sp = jnp.maximum(-lam, 0.0) + jnp.log1p(jnp.exp(-jnp.abs(lam)))
    log_a = (-LRU_C) * r * sp
    a = jnp.exp(log_a)
    a2 = a * a
    mult = jnp.sqrt(jnp.maximum(-jnp.tanh(log_a) * (1.0 + a2), 0.0))
    return xbb, r, i, sp, a, a2, mult


def _acore_fwd(proj, conv_w, conv_b, w_r, w_i, b_r, b_i, lam, *, name):
    T, C2 = proj.shape
    C = C2 // 2
    nb, bw, _ = w_r.shape
    tb = min(T, SCAN_BLOCK)

    def body(xp_ref, gate_ref, cw_ref, cb_ref, wr_ref, wi_ref, br_ref, bi_ref, lam_ref,
             xb_ref, h_ref, yg_ref, tail_ref, hlast_ref):
        t = pl.program_id(1)

        @pl.when(t == 0)
        def _():
            tail_ref[...] = jnp.zeros_like(tail_ref)
            hlast_ref[...] = jnp.zeros_like(hlast_ref)

        row = lax.broadcasted_iota(jnp.int32, (tb, bw), 0)
        xp = xp_ref[...]
        tail = tail_ref[...]
        xb = cb_ref[...] + cw_ref[CONV_W - 1:CONV_W, :] * xp
        for j in range(1, CONV_W):
            xb = xb + cw_ref[CONV_W - 1 - j:CONV_W - j, :] * _shift_down(xp, tail, j, row)
        tail_ref[...] = xp[tb - SUBLANES:, :]
        xb_ref[...] = xb

        _, r, i, sp, a, a2, mult = _lru_gates(xb, wr_ref[...], wi_ref[...], br_ref[...], bi_ref[...],
                                              lam_ref[...])
        ca, cb = a, mult * (i * xb)
        s = 1
        while s < tb:
            m = row >= s
            cb = jnp.where(m, ca * pltpu.roll(cb, s, 0) + cb, cb)
            ca = jnp.where(m, ca * pltpu.roll(ca, s, 0), ca)
            s *= 2
        h = cb + ca * hlast_ref[SUBLANES - 1:SUBLANES, :]
        hlast_ref[...] = h[tb - SUBLANES:, :]
        h_ref[...] = h
        gate = gate_ref[...]
        yg_ref[...] = (h * (gate * _sigmoid(gate))).astype(BF16)

    blk = lambda off: pl.BlockSpec((tb, bw), lambda n, t: (t, off + n))
    vec = pl.BlockSpec((1, bw), lambda n, t: (0, n))
    wspec = pl.BlockSpec((None, bw, bw), lambda n, t: (n, 0, 0))
    return _pcall(
        body, name=name, grid=(nb, T // tb),
        in_specs=[blk(0), blk(nb), pl.BlockSpec((CONV_W, bw), lambda n, t: (0, n)), vec, wspec, wspec,
                  vec, vec, vec],
        out_specs=[blk(0), blk(0), blk(0)],
        out_shape=[jax.ShapeDtypeStruct((T, C), F32), jax.ShapeDtypeStruct((T, C), F32),
                   jax.ShapeDtypeStruct((T, C), BF16)],
        scratch_shapes=[pltpu.VMEM((SUBLANES, bw), F32), pltpu.VMEM((SUBLANES, bw), F32)],
        compiler_params=_params("parallel", "arbitrary"))(
            proj, proj, conv_w, conv_b, w_r, w_i, b_r, b_i, lam)


def _acore_bwd(dyg, proj, xb_all, h_all, conv_w, w_r, w_i, b_r, b_i, lam, *, name):
    T, C2 = proj.shape
    C = C2 // 2
    nb, bw, _ = w_r.shape
    tb = min(T, SCAN_BLOCK)
    nt = T // tb
    per8 = tb // SUBLANES

    def body(dyg_ref, xp_ref, gate_ref, xb_ref, h_ref, xp_prev_ref, h_prev_ref, cw_ref,
             wr_ref, wi_ref, br_ref, bi_ref, lam_ref,
             dxp_ref, dgate_ref, dcw_ref, dcb_ref, dbr_ref, dbi_ref, dlam_ref, dwr_ref, dwi_ref,
             gh_next_ref, a_next_ref, dxb_next_ref):
        step = pl.program_id(1)
        first_block = step == nt - 1

        @pl.when(step == 0)
        def _():
            gh_next_ref[...] = jnp.zeros_like(gh_next_ref)
            a_next_ref[...] = jnp.zeros_like(a_next_ref)
            dxb_next_ref[...] = jnp.zeros_like(dxb_next_ref)

        row = lax.broadcasted_iota(jnp.int32, (tb, bw), 0)
        keep = jnp.where(first_block, 0.0, 1.0)
        h_prev = h_prev_ref[...] * keep
        xp_prev = xp_prev_ref[...] * keep
        xp, gate, xb, h, dyg_v = xp_ref[...], gate_ref[...], xb_ref[...], h_ref[...], dyg_ref[...]
        lam_v = lam_ref[...]
        wr, wi = wr_ref[...], wi_ref[...]

        sg = _sigmoid(gate)
        dh = dyg_v * (gate * sg)
        dgate_ref[...] = (dyg_v * h * (sg * (1.0 + gate * (1.0 - sg)))).astype(BF16)

        xbb, r, i, sp, a, a2, mult = _lru_gates(xb, wr, wi, br_ref[...], bi_ref[...], lam_v)

        cg = dh
        cc = _shift_up(a, a_next_ref[...], 1, row)
        s = 1
        while s < tb:
            m = row < tb - s
            cg = jnp.where(m, cc * pltpu.roll(cg, tb - s, 0) + cg, cg)
            cc = jnp.where(m, cc * pltpu.roll(cc, tb - s, 0), cc)
            s *= 2
        gh = cg + cc * gh_next_ref[0:1, :]
        gh_next_ref[...] = gh[0:SUBLANES, :]
        a_next_ref[...] = a[0:SUBLANES, :]

        da = gh * _shift_down(h, h_prev, 1, row)
        dmult = gh * (i * xb)
        di = gh * mult * xb
        dxb = gh * mult * i
        dla = da * a - dmult * jnp.where(mult > 0.0, a2 / mult, 0.0)
        dr = dla * ((-LRU_C) * sp)
        dsp = jnp.sum(dla * ((-LRU_C) * r), axis=0, keepdims=True)
        dlam_part = dsp * (-_sigmoid(-lam_v))
        dpr = dr * r * (1.0 - r)
        dpi = di * i * (1.0 - i)
        dbr_part = jnp.sum(dpr, axis=0, keepdims=True)
        dbi_part = jnp.sum(dpi, axis=0, keepdims=True)
        dprb, dpib = dpr.astype(BF16), dpi.astype(BF16)
        dwr_part = _dot(xbb, dprb, 0, 0)
        dwi_part = _dot(xbb, dpib, 0, 0)
        dxb = dxb + _dot(dprb, wr, 1, 1) + _dot(dpib, wi, 1, 1)

        dxb_next = dxb_next_ref[...]
        dxp = cw_ref[CONV_W - 1:CONV_W, :] * dxb
        for j in range(1, CONV_W):
            dxp = dxp + cw_ref[CONV_W - 1 - j:CONV_W - j, :] * _shift_up(dxb, dxb_next, j, row)
        dxb_next_ref[...] = dxb[0:SUBLANES, :]
        dxp_ref[...] = dxp.astype(BF16)
        dcb_part = jnp.sum(dxb, axis=0, keepdims=True)
        dcw_rows = []
        for k in range(CONV_W):
            j = CONV_W - 1 - k
            sh = xp if j == 0 else _shift_down(xp, xp_prev, j, row)
            dcw_rows.append(jnp.sum(dxb * sh, axis=0, keepdims=True))

        @pl.when(step == 0)
        def _():
            for k in range(CONV_W):
                dcw_ref[k:k + 1, :] = dcw_rows[k]
            dcb_ref[...] = dcb_part
            dbr_ref[...] = dbr_part
            dbi_ref[...] = dbi_part
            dlam_ref[...] = dlam_part
            dwr_ref[...] = dwr_part
            dwi_ref[...] = dwi_part

        @pl.when(step > 0)
        def _():
            for k in range(CONV_W):
                dcw_ref[k:k + 1, :] += dcw_rows[k]
            dcb_ref[...] += dcb_part
            dbr_ref[...] += dbr_part
            dbi_ref[...] += dbi_part
            dlam_ref[...] += dlam_part
            dwr_ref[...] += dwr_part
            dwi_ref[...] += dwi_part

    rev = lambda s: nt - 1 - s
    blk = lambda off: pl.BlockSpec((tb, bw), lambda n, s: (rev(s), off + n))
    prev8 = lambda off: pl.BlockSpec(
        (SUBLANES, bw), lambda n, s: (jnp.maximum(rev(s) * per8 - 1, 0), off + n))
    vec = pl.BlockSpec((1, bw), lambda n, s: (0, n))
    wspec = pl.BlockSpec((None, bw, bw), lambda n, s: (n, 0, 0))
    cwspec = pl.BlockSpec((CONV_W, bw), lambda n, s: (0, n))
    vshape = jax.ShapeDtypeStruct((1, C), F32)
    wshape = jax.ShapeDtypeStruct((nb, bw, bw), F32)
    return _pcall(
        body, name=name, grid=(nb, nt),
        in_specs=[blk(0), blk(0), blk(nb), blk(0), blk(0), prev8(0), prev8(0), cwspec,
                  wspec, wspec, vec, vec, vec],
        out_specs=[blk(0), blk(0), cwspec, vec, vec, vec, vec, wspec, wspec],
        out_shape=[jax.ShapeDtypeStruct((T, C), BF16), jax.ShapeDtypeStruct((T, C), BF16),
                   jax.ShapeDtypeStruct((CONV_W, C), F32), vshape, vshape, vshape, vshape,
                   wshape, wshape],
        scratch_shapes=[pltpu.VMEM((SUBLANES, bw), F32)] * 3,
        compiler_params=_params("parallel", "arbitrary"))(
            dyg, proj, proj, xb_all, h_all, proj, h_all, conv_w, w_r, w_i, b_r, b_i, lam)


def _later_sum(lk, tri):
    return jnp.dot(lk.astype(BF16), tri, preferred_element_type=F32)


def _log2_sigmoids(y):
    t = jnp.log(1.0 + jnp.exp2(-jnp.abs(y))) * LOG2E
    ls = jnp.minimum(y, 0.0) - t
    return ls, ls - y


def _attn_blocks(T):
    bk = min(T, ATT_KEY_BLOCK)
    bq = min(T, ATT_QUERY_BLOCK)
    return bq, bk, bq // bk


def _attn_fwd(q, kv, gate, *, name):
    T, HD = q.shape
    H = HD // HEAD_DIM
    bq, bk, per = _attn_blocks(T)
    scale = 1.0 / math.sqrt(HEAD_DIM)

    def body(q_ref, k_ref, v_ref, g_ref, o_ref, og_ref, lt_ref):
        i = pl.program_id(1)
        qv = q_ref[...]
        tr = lax.broadcasted_iota(jnp.int32, (bk, bk), 0)
        tc = lax.broadcasted_iota(jnp.int32, (bk, bk), 1)
        tri = (tr > tc).astype(BF16)
        ahead = (lax.broadcasted_iota(jnp.int32, (bq, bk), 0)
                 - lax.broadcasted_iota(jnp.int32, (bq, bk), 1))

        def group(top, acc, c, mask):
            starts = [pl.multiple_of((top - d) * bk, bk) for d in range(per)]
            zs = [_dot(qv, k_ref[pl.ds(ks, bk), :], 1, 1) for ks in starts]
            lss, sums, css, causals = [], [], [], []
            for ks, z in zip(starts, zs):
                ls, lk = _log2_sigmoids(z * (scale * LOG2E))
                if mask:
                    causals.append(ahead > ks - i * bq)
                    lk = jnp.where(causals[-1], lk, 0.0)
                lss.append(ls)
                sums.append(jnp.sum(lk, axis=1, keepdims=True))
                css.append(_later_sum(lk, tri))
            for d, ks in enumerate(starts):
                w = jnp.exp2(lss[d] + (css[d] + c))
                if mask:
                    w = jnp.where(causals[d], w, 0.0)
                acc = acc + jnp.dot(w.astype(BF16), v_ref[pl.ds(ks, bk), :], preferred_element_type=F32)
                c = c + sums[d]
            return acc, c

        state = group(i * per + per - 1, jnp.zeros((bq, HEAD_DIM), F32), jnp.zeros((bq, 1), F32), True)
        acc, c = lax.fori_loop(0, i, lambda gg, s: group((i - gg) * per - 1, s[0], s[1], False), state)
        o_ref[...] = acc
        g = g_ref[...]
        og_ref[...] = (acc * (g * _sigmoid(g))).astype(BF16)
        lt_ref[...] = jnp.broadcast_to(c, (bq, HEAD_DIM))

    qspec = pl.BlockSpec((bq, HEAD_DIM), lambda h, i: (i, h))
    return _pcall(
        body, name=name, grid=(H, T // bq),
        in_specs=[qspec, pl.BlockSpec((T, HEAD_DIM), lambda h, i: (0, h)),
                  pl.BlockSpec((T, HEAD_DIM), lambda h, i: (0, H + h)), qspec],
        out_specs=[qspec, qspec, qspec],
        out_shape=[jax.ShapeDtypeStruct((T, HD), F32), jax.ShapeDtypeStruct((T, HD), BF16),
                   jax.ShapeDtypeStruct((T, HD), F32)],
        compiler_params=_params("parallel", "arbitrary"))(q, kv, kv, gate)


def _attn_bwd(q, kv, gate, o, ltot, dog, *, name):
    T, HD = q.shape
    H = HD // HEAD_DIM
    bq, bk, per = _attn_blocks(T)
    nq = T // bq
    scale = 1.0 / math.sqrt(HEAD_DIM)

    def body(q_ref, k_ref, v_ref, g_ref, o_ref, lt_ref, dog_ref,
             dq_ref, dg_ref, dk_ref, dv_ref, dk_acc, dv_acc):
        i = pl.program_id(1)

        @pl.when(i == 0)
        def _():
            dk_acc[...] = jnp.zeros_like(dk_acc)
            dv_acc[...] = jnp.zeros_like(dv_acc)

        qv = q_ref[...]
        g, ov, dogv = g_ref[...], o_ref[...], dog_ref[...]
        sg = _sigmoid(g)
        do = dogv * (g * sg)
        dg_ref[...] = (dogv * ov * (sg * (1.0 + g * (1.0 - sg)))).astype(BF16)
        dob = do.astype(BF16)
        ltot_v = lt_ref[:, 0:1]
        tr = lax.broadcasted_iota(jnp.int32, (bk, bk), 0)
        tc = lax.broadcasted_iota(jnp.int32, (bk, bk), 1)
        tri_later = (tr > tc).astype(BF16)
        tri_excl = (tr < tc).astype(BF16)
        ahead = (lax.broadcasted_iota(jnp.int32, (bq, bk), 0)
                 - lax.broadcasted_iota(jnp.int32, (bq, bk), 1))

        def group(first, dq, p_lk, p_g, mask):
            starts = [pl.multiple_of((first + d) * bk, bk) for d in range(per)]
            zs = [_dot(qv, k_ref[pl.ds(ks, bk), :], 1, 1) for ks in starts]
            dws = [_dot(dob, v_ref[pl.ds(ks, bk), :], 1, 1) for ks in starts]
            lss, css, causals = [], [], []
            for ks, z in zip(starts, zs):
                ls, lk = _log2_sigmoids(z * (scale * LOG2E))
                if mask:
                    causals.append(ahead > ks - i * bq)
                    lk = jnp.where(causals[-1], lk, 0.0)
                lss.append(ls)
                p_lk = p_lk + jnp.sum(lk, axis=1, keepdims=True)
                css.append((ltot_v - p_lk) + _later_sum(lk, tri_later))
            gms, wbs, befores = [], [], []
            for d in range(per):
                w = jnp.exp2(lss[d] + css[d])
                if mask:
                    w = jnp.where(causals[d], w, 0.0)
                gm = dws[d] * w
                gms.append(gm)
                wbs.append(w.astype(BF16))
                befores.append(jnp.dot(gm.astype(BF16), tri_excl, preferred_element_type=F32) + p_g)
                p_g = p_g + jnp.sum(gm, axis=1, keepdims=True)
            for d, ks in enumerate(starts):
                dz = gms[d] - jnp.exp2(lss[d]) * (gms[d] + befores[d])
                if mask:
                    dz = jnp.where(causals[d], dz, 0.0)
                dzb = (dz * scale).astype(BF16)
                dq = dq + jnp.dot(dzb, k_ref[pl.ds(ks, bk), :], preferred_element_type=F32)
                dk_acc[pl.ds(ks, bk), :] += _dot(dzb, qv, 0, 0)
                dv_acc[pl.ds(ks, bk), :] += _dot(wbs[d], dob, 0, 0)
            return dq, p_lk, p_g

        zero = jnp.zeros((bq, 1), F32)
        state = lax.fori_loop(0, i, lambda gg, s: group(gg * per, s[0], s[1], s[2], False),
                              (jnp.zeros((bq, HEAD_DIM), F32), zero, zero))
        state = group(i * per, state[0], state[1], state[2], True)
        dq_ref[...] = state[0].astype(BF16)

        @pl.when(i == nq - 1)
        def _():
            dk_ref[...] = dk_acc[...].astype(BF16)
            dv_ref[...] = dv_acc[...].astype(BF16)

    qspec = pl.BlockSpec((bq, HEAD_DIM), lambda h, i: (i, h))
    kspec = pl.BlockSpec((T, HEAD_DIM), lambda h, i: (0, h))
    return _pcall(
        body, name=name, grid=(H, nq),
        in_specs=[qspec, kspec, pl.BlockSpec((T, HEAD_DIM), lambda h, i: (0, H + h)),
                  qspec, qspec, qspec, qspec],
        out_specs=[qspec, qspec, kspec, kspec],
        out_shape=[jax.ShapeDtypeStruct((T, HD), BF16)] * 4,
        scratch_shapes=[pltpu.VMEM((T, HEAD_DIM), F32)] * 2,
        compiler_params=_params("parallel", "arbitrary"))(q, kv, kv, gate, o, ltot, dog)


def _position():
    return lax.axis_index("x"), lax.axis_index("y"), lax.axis_index("c")


def _chip_of(k, x, y):
    return (1 - x if k & 1 else x), (1 - y if k & 2 else y)


def _weights_gather(shards):
    n = len(shards)

    def body(*refs):
        ins, outs = refs[:n], refs[n:2 * n]
        send_sems, recv_sems, local_sems = refs[2 * n:]
        x, y, c = _position()
        sibling = (x, y, 1 - c)
        chips = [_chip_of(k, x, y) for k in (1, 2, 3)]

        def copy(a, k, block, to, src=None):
            slot = outs[a].at[4 * block[0] + 2 * block[1] + block[2]]
            return pltpu.make_async_remote_copy(
                src_ref=slot if src is None else src, dst_ref=slot,
                send_sem=send_sems.at[a, k], recv_sem=recv_sems.at[a, k],
                device_id=to, device_id_type=MESH)

        mine = [pltpu.make_async_copy(ins[a], outs[a].at[4 * x + 2 * y + c], local_sems.at[a])
                for a in range(n)]
        for cp in mine:
            cp.start()
        first = []
        for a in range(n):
            first.append(copy(a, 0, (x, y, c), sibling, src=ins[a]))
            first += [copy(a, 1 + j, (x, y, c), (*chip, c), src=ins[a]) for j, chip in enumerate(chips)]
        for cp in first:
            cp.start()
        passed = []
        for j, chip in enumerate(chips):
            for a in range(n):
                copy(a, 1 + j, (*chip, c), (x, y, c)).wait_recv()
                fwd = copy(a, 4 + j, (*chip, c), sibling)
                fwd.start()
                passed.append(fwd)
        for a in range(n):
            copy(a, 0, sibling, (x, y, c)).wait_recv()
            for j, chip in enumerate(chips):
                copy(a, 4 + j, (*chip, 1 - c), (x, y, c)).wait_recv()
        for cp in first + passed:
            cp.wait_send()
        for cp in mine:
            cp.wait()

    return _pcall(
        body, name="weights_gather", in_specs=[ANY] * n, out_specs=[ANY] * n,
        out_shape=[jax.ShapeDtypeStruct((N_DEV,) + s.shape, s.dtype) for s in shards],
        scratch_shapes=[pltpu.SemaphoreType.DMA((n, 7)), pltpu.SemaphoreType.DMA((n, 7)),
                        pltpu.SemaphoreType.DMA((n,))])(*shards)


def _grads_to_sibling(grads, small):
    n = len(grads)

    def body(*refs):
        g_refs, small_ref = refs[:n], refs[n]
        got, small_all = refs[n + 1:2 * n + 1], refs[2 * n + 1]
        send_sems, recv_sems, small_send, small_recv, small_local = refs[2 * n + 2:]
        x, y, c = _position()
        me = 4 * x + 2 * y + c
        remote = []
        for a in range(n):
            for chip in range(4):
                remote.append(pltpu.make_async_remote_copy(
                    src_ref=g_refs[a].at[2 * chip + (1 - c)], dst_ref=got[a].at[chip],
                    send_sem=send_sems.at[a, chip], recv_sem=recv_sems.at[a, chip],
                    device_id=(x, y, 1 - c), device_id_type=MESH))
        peers = []
        for m in range(1, N_DEV):
            px, py, pc = x ^ (m >> 2), y ^ ((m >> 1) & 1), c ^ (m & 1)
            peers.append(pltpu.make_async_remote_copy(
                src_ref=small_ref, dst_ref=small_all.at[me],
                send_sem=small_send.at[m - 1], recv_sem=small_recv.at[m - 1],
                device_id=(px, py, pc), device_id_type=MESH))
        own = pltpu.make_async_copy(small_ref, small_all.at[me], small_local)
        for cp in peers + remote + [own]:
            cp.start()
        for cp in peers + remote:
            cp.wait_send()
        for m in range(1, N_DEV):
            px, py, pc = x ^ (m >> 2), y ^ ((m >> 1) & 1), c ^ (m & 1)
            pltpu.make_async_remote_copy(
                src_ref=small_ref, dst_ref=small_all.at[4 * px + 2 * py + pc],
                send_sem=small_send.at[m - 1], recv_sem=small_recv.at[m - 1],
                device_id=(px, py, pc), device_id_type=MESH).wait_recv()
        for cp in remote:
            cp.wait_recv()
        own.wait()

    part = [jax.ShapeDtypeStruct((4,) + g.shape[1:], g.dtype) for g in grads]
    return _pcall(
        body, name="grads_to_sibling", in_specs=[ANY] * (n + 1), out_specs=[ANY] * (n + 1),
        out_shape=part + [jax.ShapeDtypeStruct((N_DEV,) + small.shape, small.dtype)],
        scratch_shapes=[pltpu.SemaphoreType.DMA((n, 4)), pltpu.SemaphoreType.DMA((n, 4)),
                        pltpu.SemaphoreType.DMA((7,)), pltpu.SemaphoreType.DMA((7,)),
                        pltpu.SemaphoreType.DMA])(*grads, small)


def _grads_to_chips(parts):
    n = len(parts)

    def body(*refs):
        p_refs, outs = refs[:n], refs[n:2 * n]
        send_sems, recv_sems = refs[2 * n:]
        x, y, c = _position()
        copies = []
        for a in range(n):
            for k in range(3):
                cx, cy = _chip_of(k + 1, x, y)
                copies.append(pltpu.make_async_remote_copy(
                    src_ref=p_refs[a].at[2 * cx + cy], dst_ref=outs[a].at[k],
                    send_sem=send_sems.at[a, k], recv_sem=recv_sems.at[a, k],
                    device_id=(cx, cy, c), device_id_type=MESH))
        for cp in copies:
            cp.start()
        for cp in copies:
            cp.wait_send()
        for cp in copies:
            cp.wait_recv()

    return _pcall(
        body, name="grads_to_chips", in_specs=[ANY] * n, out_specs=[ANY] * n,
        out_shape=[jax.ShapeDtypeStruct((3,) + p.shape[1:], p.dtype) for p in parts],
        scratch_shapes=[pltpu.SemaphoreType.DMA((n, 3)), pltpu.SemaphoreType.DMA((n, 3))])(*parts)


def _pair_sum(grad, got, *, name):
    _, R, C = got.shape
    tr = _pick8(R, max(2 * SUBLANES, (1 << 17) // C))

    def body(g_ref, b_ref, o_ref, ob_ref):
        north = lax.axis_index("c") == 1
        for chip in range(4):
            s = jnp.where(north, g_ref[chip, 1], g_ref[chip, 0]) + b_ref[chip]
            o_ref[chip] = s
            ob_ref[chip] = s.astype(BF16)

    spec = pl.BlockSpec((4, tr, C), lambda i: (0, i, 0))
    return _pcall(
        body, name=name, grid=(R // tr,),
        in_specs=[pl.BlockSpec((4, 2, tr, C), lambda i: (0, 0, i, 0)), spec],
        out_specs=[spec, spec],
        out_shape=[jax.ShapeDtypeStruct((4, R, C), F32), jax.ShapeDtypeStruct((4, R, C), BF16)],
        compiler_params=_params("parallel"))(grad.reshape(4, 2, R, C), got)


def _pick8(n, cap):
    if n <= cap:
        return n
    best = None
    for t in range(SUBLANES, cap + 1, SUBLANES):
        if n % t == 0:
            best = t
    assert best is not None, (n, cap)
    return best


def _adamw(w, m, v, parts, *, name, chip_sums=None):
    R, C = w.shape
    tr = _pick8(R, max(SUBLANES, (1 << 17) // C))
    c1 = 1.0 - ADAM_B1 ** ADAM_STEP
    c2 = 1.0 - ADAM_B2 ** ADAM_STEP
    parts = list(parts) if chip_sums is None else [chip_sums] + list(parts)
    np_ = len(parts)

    def body(w_ref, m_ref, v_ref, *refs):
        p_refs = refs[:np_]
        g_ref, d_ref, nm_ref, nv_ref = refs[np_:]
        g = None
        if chip_sums is not None:
            s_ref, p_refs = p_refs[0], p_refs[1:]
            x1, y1 = lax.axis_index("x") == 1, lax.axis_index("y") == 1
            g = jnp.where(x1, jnp.where(y1, s_ref[3], s_ref[2]), jnp.where(y1, s_ref[1], s_ref[0]))
        for p_ref in p_refs:
            for t in [p_ref[k].astype(F32) for k in range(p_ref.shape[0])]:
                g = t if g is None else g + t
        mn = ADAM_B1 * m_ref[...] + (1.0 - ADAM_B1) * g
        vn = ADAM_B2 * v_ref[...] + (1.0 - ADAM_B2) * (g * g)
        d_ref[...] = -ADAM_LR * ((mn / c1) / (jnp.sqrt(vn / c2) + ADAM_EPS) + ADAM_WD * w_ref[...])
        g_ref[...] = g
        nm_ref[...] = mn
        nv_ref[...] = vn

    spec = pl.BlockSpec((tr, C), lambda i: (i, 0))
    pspecs = [pl.BlockSpec((p.shape[0], tr, C), lambda i: (0, i, 0)) for p in parts]
    return _pcall(
        body, name=name, grid=(R // tr,), in_specs=[spec] * 3 + pspecs, out_specs=[spec] * 4,
        out_shape=[jax.ShapeDtypeStruct((R, C), F32)] * 4,
        compiler_params=_params("parallel"))(w, m, v, *parts)


def _rows(a):
    return a.reshape(-1, LANES)


def _whole_from_columns(shards, *, name):
    S, K, n = shards.shape
    tk = _pick8(K, 1024)

    def body(s_ref, o_ref):
        o_ref[...] = s_ref[...]

    return _pcall(
        body, name=name, grid=(K // tk, S),
        in_specs=[pl.BlockSpec((None, tk, n), lambda i, s: (s, i, 0))],
        out_specs=pl.BlockSpec((tk, n), lambda i, s: (i, s)),
        out_shape=jax.ShapeDtypeStruct((K, S * n), shards.dtype),
        compiler_params=_params("parallel", "parallel"))(shards)


def _forward_backward(xs, target, a_norm, g_a_w_in, conv_w, conv_b, g_w_r, g_w_i, b_r, b_i, lam,
                      g_a_w_out, kv_norm, g_w_kv, b_norm, g_b_w_in, g_b_w_out, final_norm):
    (h_a,) = _rms_fwd(xs, [a_norm], name="a_norm_fwd")
    proj_a = _mm_nn(h_a, g_a_w_in, name="a_in_proj", out_dtype=F32)
    xb, h_rec, yg = _acore_fwd(proj_a, conv_w, conv_b, g_w_r, g_w_i, b_r, b_i, lam, name="a_core_fwd")
    x1 = _mm_nn(yg, g_a_w_out, name="a_out_proj", out_dtype=F32, res=xs)
    hk, hb = _rms_fwd(x1, [kv_norm, b_norm], name="kv_b_norm_fwd")
    kv = _mm_nn(hk, g_w_kv, name="kv_proj", out_dtype=BF16)
    hd = g_b_w_in.shape[1] // 2
    q = _mm_nn(hb, g_b_w_in, name="q_proj", out_dtype=BF16, col_off=0, cols=hd)
    gate_b = _mm_nn(hb, g_b_w_in, name="b_gate_proj", out_dtype=F32, col_off=hd, cols=hd)
    o, og, ltot = _attn_fwd(q, kv, gate_b, name="attn_fwd")
    x2 = _mm_nn(og, g_b_w_out, name="b_out_proj", out_dtype=F32, res=x1)
    dx2, d_final_norm, loss_part = _final_loss(x2, target, final_norm, name="final_norm_loss")

    dog = _mm_nt(dx2, g_b_w_out, name="b_out_proj_bwd")
    dw_b_out = _mm_tn(og, dx2, name="b_out_proj_wgrad")
    dq, dgate_b, dk, dv = _attn_bwd(q, kv, gate_b, o, ltot, dog, name="attn_bwd")
    dproj_b = jnp.concatenate([dq, dgate_b], axis=1)
    dkv = jnp.concatenate([dk, dv], axis=1)
    dhb = _mm_nt(dproj_b, g_b_w_in, name="b_in_proj_bwd")
    dw_b_in = _mm_tn(hb, dproj_b, name="b_in_proj_wgrad", shards=N_DEV)
    dhk = _mm_nt(dkv, g_w_kv, name="kv_proj_bwd")
    dw_kv = _mm_tn(hk, dkv, name="kv_proj_wgrad", shards=N_DEV)
    dx1, d_b_norm, d_kv_norm = _rms_bwd(x1, dx2, [dhb, dhk], [b_norm, kv_norm], name="kv_b_norm_bwd")
    dyg = _mm_nt(dx1, g_a_w_out, name="a_out_proj_bwd")
    dw_a_out = _mm_tn(yg, dx1, name="a_out_proj_wgrad")
    (dxp, dgate_a, d_conv_w, d_conv_b, d_b_r, d_b_i, d_lambda, dw_r, dw_i) = _acore_bwd(
        dyg, proj_a, xb, h_rec, conv_w, g_w_r, g_w_i, b_r, b_i, lam, name="a_core_bwd")
    dproj_a = jnp.concatenate([dxp, dgate_a], axis=1)
    dh_a = _mm_nt(dproj_a, g_a_w_in, name="a_in_proj_bwd")
    dw_a_in = _mm_tn(h_a, dproj_a, name="a_in_proj_wgrad", shards=N_DEV)
    grad_x, d_a_norm = _rms_bwd(xs, dx1, [dh_a], [a_norm], name="a_norm_bwd")
    return (loss_part, grad_x, dw_a_in, dw_a_out, dw_kv, dw_b_in, dw_b_out, dw_r, dw_i, d_a_norm,
            d_conv_w, d_conv_b, d_b_r, d_b_i, d_lambda, d_kv_norm, d_b_norm, d_final_norm)


def kernel(x, a_norm, a_w_in, a_conv_w, a_conv_b, a_w_r, a_b_r, a_w_i, a_b_i, a_lambda, a_w_out, kv_norm, w_kv, b_norm, b_w_in, b_w_out, final_norm, loss_target, m_a_norm, m_a_w_in, m_a_conv_w, m_a_conv_b, m_a_w_r, m_a_b_r, m_a_w_i, m_a_b_i, m_a_lambda, m_a_w_out, m_kv_norm, m_w_kv, m_b_norm, m_b_w_in, m_b_w_out, m_final_norm, v_a_norm, v_a_w_in, v_a_conv_w, v_a_conv_b, v_a_w_r, v_a_b_r, v_a_w_i, v_a_b_i, v_a_lambda, v_a_w_out, v_kv_norm, v_w_kv, v_b_norm, v_b_w_in, v_b_w_out, v_final_norm):
    T, D = x.shape[1], x.shape[2]
    nb, bw = a_w_r.shape[1], a_w_r.shape[3]
    C = nb * bw
    me = 4 * lax.axis_index("x") + 2 * lax.axis_index("y") + lax.axis_index("c")
    xs = x[0]
    target = loss_target[0]

    big = [a_w_in[0], a_w_out[0], w_kv, b_w_in[0], b_w_out[0], a_w_r[0], a_w_i[0]]
    sizes = [w.size // LANES for w in big]
    packed = jnp.concatenate([_rows(w.astype(BF16)) for w in big], axis=0)
    small_f32 = jnp.concatenate([_rows(a_conv_w[0]), _rows(b_norm[0])], axis=0)
    pad = (-small_f32.shape[0]) % SUBLANES
    small_f32 = jnp.pad(small_f32, ((0, pad), (0, 0)))
    packed_all, small_all = _weights_gather([packed, small_f32])

    offs = [0]
    for s in sizes:
        offs.append(offs[-1] + s)
    pieces = [packed_all[:, offs[i]:offs[i + 1], :] for i in range(len(big))]
    g_a_w_in = _whole_from_columns(pieces[0].reshape(N_DEV, D, a_w_in.shape[2]), name="a_w_in_whole")
    g_a_w_out = pieces[1].reshape(C, D)
    g_w_kv = _whole_from_columns(pieces[2].reshape(N_DEV, D, w_kv.shape[1]), name="w_kv_whole")
    g_b_w_in = _whole_from_columns(pieces[3].reshape(N_DEV, D, b_w_in.shape[2]), name="b_w_in_whole")
    g_b_w_out = pieces[4].reshape(b_w_out.shape[1] * N_DEV, D)
    rows_r = a_w_r.shape[2]
    g_w_r = pieces[5].reshape(N_DEV, nb, rows_r, bw).transpose(1, 0, 2, 3).reshape(nb, bw, bw)
    g_w_i = pieces[6].reshape(N_DEV, nb, rows_r, bw).transpose(1, 0, 2, 3).reshape(nb, bw, bw)
    cw_rows = a_conv_w.shape[1] * a_conv_w.shape[2] // LANES
    conv_w_full = small_all[:, :cw_rows, :].reshape(N_DEV, CONV_W, a_conv_w.shape[2])
    conv_w_full = conv_w_full.transpose(1, 0, 2).reshape(CONV_W, C)
    bn_rows = b_norm.shape[1] // LANES
    b_norm_full = small_all[:, cw_rows:cw_rows + bn_rows, :].reshape(1, D)
    kv_norm2, final_norm2 = kv_norm.reshape(1, D), final_norm.reshape(1, D)

    (loss_part, grad_x, dw_a_in, dw_a_out, dw_kv, dw_b_in, dw_b_out, dw_r, dw_i, d_a_norm, d_conv_w,
     d_conv_b, d_b_r, d_b_i, d_lambda, d_kv_norm, d_b_norm, d_final_norm) = _forward_backward(
         xs, target, a_norm, g_a_w_in, conv_w_full, a_conv_b, g_w_r, g_w_i, a_b_r, a_b_i, a_lambda,
         g_a_w_out, kv_norm2, g_w_kv, b_norm_full, g_b_w_in, g_b_w_out, final_norm2)

    def lru_shards(dw):
        return dw.reshape(nb, N_DEV, rows_r, bw).transpose(1, 0, 2, 3).reshape(N_DEV, nb * rows_r, bw)

    full = [dw_a_in, dw_a_out.reshape(N_DEV, a_w_out.shape[1], D), dw_kv, dw_b_in,
            dw_b_out.reshape(N_DEV, b_w_out.shape[1], D), lru_shards(dw_r), lru_shards(dw_i)]
    small_parts = [d_a_norm, d_conv_w, d_conv_b, d_b_r, d_b_i, d_lambda, d_kv_norm, d_b_norm, d_final_norm]
    small_sizes = [p.size // LANES for p in small_parts]
    small = jnp.concatenate([_rows(p) for p in small_parts], axis=0)
    outs = _grads_to_sibling(full, small)
    got, small_everyone = outs[:len(full)], outs[-1]
    sums = [_pair_sum(f_, g_, name=f"pair_sum_{i}") for i, (f_, g_) in enumerate(zip(full, got))]
    others = _grads_to_chips([s[1] for s in sums])

    def shard2d(w):
        return w.reshape(-1, w.shape[-1])

    names_big = [(a_w_in, m_a_w_in, v_a_w_in), (a_w_out, m_a_w_out, v_a_w_out), (w_kv, m_w_kv, v_w_kv),
                 (b_w_in, m_b_w_in, v_b_w_in), (b_w_out, m_b_w_out, v_b_w_out),
                 (a_w_r, m_a_w_r, v_a_w_r), (a_w_i, m_a_w_i, v_a_w_i)]
    upd_big = []
    for i, (w, m, v) in enumerate(names_big):
        res = _adamw(shard2d(w), shard2d(m), shard2d(v), [others[i]], chip_sums=sums[i][0], name=f"adamw_{i}")
        upd_big.append([r.reshape(w.shape) for r in res])

    soffs = [0]
    for s in small_sizes:
        soffs.append(soffs[-1] + s)

    def small_piece(i):
        return small_everyone[:, soffs[i]:soffs[i + 1], :]

    cw_cols = a_conv_w.shape[2]
    conv_piece = small_piece(1).reshape(N_DEV, CONV_W, C)
    conv_piece = lax.dynamic_slice_in_dim(conv_piece, me * cw_cols, cw_cols, axis=2)
    conv_piece = conv_piece.reshape(N_DEV, CONV_W * cw_cols // LANES, LANES)
    bn_piece = lax.dynamic_slice_in_dim(small_piece(7), me * bn_rows, bn_rows, axis=1)
    small_g = jnp.concatenate([small_piece(0), conv_piece, small_piece(2), small_piece(3), small_piece(4),
                               small_piece(5), small_piece(6), bn_piece, small_piece(8)], axis=1)
    small_w = [(a_norm, m_a_norm, v_a_norm), (a_conv_w, m_a_conv_w, v_a_conv_w),
               (a_conv_b, m_a_conv_b, v_a_conv_b), (a_b_r, m_a_b_r, v_a_b_r), (a_b_i, m_a_b_i, v_a_b_i),
               (a_lambda, m_a_lambda, v_a_lambda), (kv_norm, m_kv_norm, v_kv_norm),
               (b_norm, m_b_norm, v_b_norm), (final_norm, m_final_norm, v_final_norm)]
    pack = lambda idx: jnp.concatenate([_rows(t[idx]) for t in small_w], axis=0)
    res_small = _adamw(pack(0), pack(1), pack(2), [small_g], name="adamw_small")
    woffs = [0]
    for t in small_w:
        woffs.append(woffs[-1] + t[0].size // LANES)
    upd_small = [[r[woffs[i]:woffs[i + 1]].reshape(small_w[i][0].shape) for r in res_small]
                 for i in range(len(small_w))]

    order = [("s", 0), ("b", 0), ("s", 1), ("s", 2), ("b", 5), ("s", 3), ("b", 6), ("s", 4), ("s", 5),
             ("b", 1), ("s", 6), ("b", 2), ("s", 7), ("b", 3), ("b", 4), ("s", 8)]
    per_weight = [(upd_big if kind == "b" else upd_small)[i] for kind, i in order]
    loss = lax.psum(loss_part[0, 0], ("x", "y", "c"))
    result = [loss, grad_x[None]]
    for field in range(4):
        result += [u[field] for u in per_weight]
    return tuple(result)
```

```python
import functools
import math

import jax
import jax.numpy as jnp
from jax import lax
from jax.experimental import pallas as pl
from jax.experimental.pallas import tpu as pltpu

F32 = jnp.float32
BF16 = jnp.bfloat16
MESH = pl.DeviceIdType.MESH

EPS = 1e-6
LOG2E = 1.4426950408889634
LRU_C = 8.0
CONV_W = 4
HEAD_DIM = 128
ADAM_LR = 0.001
ADAM_B1 = 0.9
ADAM_B2 = 0.999
ADAM_EPS = 1e-08
ADAM_WD = 0.01
ADAM_STEP = 10

N_DEV = 8
LANES = 128
SUBLANES = 8
VMEM_LIMIT = 56 * 1024 * 1024

ATT_KEY_BLOCK = 256
ATT_QUERY_BLOCK = 512
SCAN_BLOCK = 256
ROW_BLOCK = 256
MM_TOKEN_BLOCK = 512
MM_WEIGHT_TILE = 1280
MM_CONTRACT_TOKENS = 2048
ANY = pl.BlockSpec(memory_space=pl.ANY)


def _pcall(body, **kw):
    return pl.pallas_call(body, **kw)


def _params(*sem):
    return pltpu.CompilerParams(dimension_semantics=sem, vmem_limit_bytes=VMEM_LIMIT)


def _pick(n, cap):
    if n <= cap:
        return n
    best = None
    for t in range(LANES, cap + 1, LANES):
        if n % t == 0:
            best = t
    assert best is not None, (n, cap)
    return best


def _sigmoid(x):
    return 1.0 / (1.0 + jnp.exp(-x))


def _dot(a, b, ca, cb):
    return lax.dot_general(a, b, (((ca,), (cb,)), ((), ())), preferred_element_type=F32)


def _mm_nn(a, w, *, name, out_dtype, col_off=0, cols=None, res=None):
    T, K = a.shape
    K2, N = w.shape
    assert K == K2
    cols = N if cols is None else cols
    tm = min(T, MM_TOKEN_BLOCK)
    tn = _pick(cols, MM_WEIGHT_TILE)
    assert col_off % tn == 0
    off = col_off // tn
    has_res = res is not None

    def body(a_ref, b_ref, *rest):
        o_ref = rest[-1]
        acc = jnp.dot(a_ref[...].astype(BF16), b_ref[...], preferred_element_type=F32)
        if has_res:
            acc = acc + rest[0][...]
        o_ref[...] = acc.astype(out_dtype)

    in_specs = [pl.BlockSpec((tm, K), lambda j, i: (i, 0)),
                pl.BlockSpec((K, tn), lambda j, i: (0, off + j))]
    args = [a, w]
    if has_res:
        in_specs.append(pl.BlockSpec((tm, tn), lambda j, i: (i, j)))
        args.append(res)
    return _pcall(
        body, name=name, grid=(cols // tn, T // tm), in_specs=in_specs,
        out_specs=pl.BlockSpec((tm, tn), lambda j, i: (i, j)),
        out_shape=jax.ShapeDtypeStruct((T, cols), out_dtype),
        compiler_params=_params("parallel", "parallel"))(*args)


def _mm_nt(a, w, *, name, out_dtype=F32):
    T, K = a.shape
    N, K2 = w.shape
    assert K == K2
    tm = min(T, MM_TOKEN_BLOCK)
    tn = _pick(N, MM_WEIGHT_TILE)

    def body(a_ref, b_ref, o_ref):
        o_ref[...] = _dot(a_ref[...].astype(BF16), b_ref[...], 1, 1).astype(out_dtype)

    return _pcall(
        body, name=name, grid=(N // tn, T // tm),
        in_specs=[pl.BlockSpec((tm, K), lambda j, i: (i, 0)), pl.BlockSpec((tn, K), lambda j, i: (j, 0))],
        out_specs=pl.BlockSpec((tm, tn), lambda j, i: (i, j)),
        out_shape=jax.ShapeDtypeStruct((T, N), out_dtype),
        compiler_params=_params("parallel", "parallel"))(a, w)


def _mm_tn(a, b, *, name, shards=1):
    T, Ko = a.shape
    T2, N = b.shape
    assert T == T2
    n = N // shards
    tt = min(T, MM_CONTRACT_TOKENS)
    tko = _pick(Ko, 1024)
    tn = _pick(n, 1024)
    per = n // tn

    def body(a_ref, b_ref, o_ref):
        t = pl.program_id(2)
        p = _dot(a_ref[...].astype(BF16), b_ref[...].astype(BF16), 0, 0)

        @pl.when(t == 0)
        def _():
            o_ref[...] = p

        @pl.when(t > 0)
        def _():
            o_ref[...] += p

    if shards == 1:
        out_spec = pl.BlockSpec((tko, tn), lambda i, j, t: (i, j))
        out_shape = jax.ShapeDtypeStruct((Ko, N), F32)
    else:
        out_spec = pl.BlockSpec((None, tko, tn), lambda i, j, t: (j // per, i, j % per))
        out_shape = jax.ShapeDtypeStruct((shards, Ko, n), F32)
    return _pcall(
        body, name=name, grid=(Ko // tko, N // tn, T // tt),
        in_specs=[pl.BlockSpec((tt, tko), lambda i, j, t: (t, i)),
                  pl.BlockSpec((tt, tn), lambda i, j, t: (t, j))],
        out_specs=out_spec, out_shape=out_shape,
        compiler_params=_params("parallel", "parallel", "arbitrary"))(a, b)


def _rms_fwd(x, gains, *, name):
    T, D = x.shape
    tm = min(T, ROW_BLOCK)
    n = len(gains)

    def body(x_ref, *refs):
        xv = x_ref[...]
        xh = xv * lax.rsqrt(jnp.mean(xv * xv, axis=-1, keepdims=True) + EPS)
        for g_ref, o_ref in zip(refs[:n], refs[n:]):
            o_ref[...] = (xh * g_ref[...]).astype(BF16)

    row = pl.BlockSpec((tm, D), lambda i: (i, 0))
    vec = pl.BlockSpec((1, D), lambda i: (0, 0))
    return _pcall(
        body, name=name, grid=(T // tm,), in_specs=[row] + [vec] * n, out_specs=[row] * n,
        out_shape=[jax.ShapeDtypeStruct((T, D), BF16)] * n,
        compiler_params=_params("parallel"))(x, *gains)


def _rms_bwd(x, dres, dhs, gains, *, name):
    T, D = x.shape
    tm = min(T, ROW_BLOCK)
    n = len(gains)

    def body(x_ref, dres_ref, *refs):
        dh_refs, g_refs = refs[:n], refs[n:2 * n]
        dx_ref, dg_refs = refs[2 * n], refs[2 * n + 1:]
        i = pl.program_id(0)
        xv = x_ref[...]
        r = lax.rsqrt(jnp.mean(xv * xv, axis=-1, keepdims=True) + EPS)
        xh = xv * r
        dxh = jnp.zeros_like(xv)
        for dh_ref, g_ref, dg_ref in zip(dh_refs, g_refs, dg_refs):
            dh = dh_ref[...]
            part = jnp.sum(dh * xh, axis=0, keepdims=True)

            @pl.when(i == 0)
            def _():
                dg_ref[...] = part

            @pl.when(i > 0)
            def _():
                dg_ref[...] += part

            dxh = dxh + dh * g_ref[...]
        dx_ref[...] = dres_ref[...] + r * (dxh - xh * jnp.mean(dxh * xh, axis=-1, keepdims=True))

    row = pl.BlockSpec((tm, D), lambda i: (i, 0))
    vec = pl.BlockSpec((1, D), lambda i: (0, 0))
    return _pcall(
        body, name=name, grid=(T // tm,), in_specs=[row, row] + [row] * n + [vec] * n,
        out_specs=[row] + [vec] * n,
        out_shape=[jax.ShapeDtypeStruct((T, D), F32)] + [jax.ShapeDtypeStruct((1, D), F32)] * n,
        compiler_params=_params("arbitrary"))(x, dres, *dhs, *gains)


def _final_loss(x, target, gain, *, name):
    T, D = x.shape
    tm = min(T, ROW_BLOCK)

    def body(x_ref, t_ref, g_ref, dx_ref, dg_ref, loss_ref):
        i = pl.program_id(0)
        xv = x_ref[...]
        g = g_ref[...]
        r = lax.rsqrt(jnp.mean(xv * xv, axis=-1, keepdims=True) + EPS)
        xh = xv * r
        err = xh * g - t_ref[...]
        part_loss = 0.5 * jnp.sum(jnp.mean(err * err, axis=-1, keepdims=True), axis=0, keepdims=True)
        dy = err * (1.0 / D)
        part_g = jnp.sum(dy * xh, axis=0, keepdims=True)

        @pl.when(i == 0)
        def _():
            dg_ref[...] = part_g
            loss_ref[...] = jnp.broadcast_to(part_loss, loss_ref.shape)

        @pl.when(i > 0)
        def _():
            dg_ref[...] += part_g
            loss_ref[...] += jnp.broadcast_to(part_loss, loss_ref.shape)

        dxh = dy * g
        dx_ref[...] = r * (dxh - xh * jnp.mean(dxh * xh, axis=-1, keepdims=True))

    row = pl.BlockSpec((tm, D), lambda i: (i, 0))
    vec = pl.BlockSpec((1, D), lambda i: (0, 0))
    return _pcall(
        body, name=name, grid=(T // tm,), in_specs=[row, row, vec],
        out_specs=[row, vec, pl.BlockSpec((1, LANES), lambda i: (0, 0))],
        out_shape=[jax.ShapeDtypeStruct((T, D), F32), jax.ShapeDtypeStruct((1, D), F32),
                   jax.ShapeDtypeStruct((1, LANES), F32)],
        compiler_params=_params("arbitrary"))(x, target, gain)


def _shift_down(x, prev_tail, j, row):
    tb = x.shape[0]
    prev = jnp.tile(prev_tail, (tb // SUBLANES, 1))
    return jnp.where(row >= j, pltpu.roll(x, j, 0), pltpu.roll(prev, j, 0))


def _shift_up(x, next_head, j, row):
    tb = x.shape[0]
    nxt = jnp.tile(next_head, (tb // SUBLANES, 1))
    return jnp.where(row < tb - j, pltpu.roll(x, tb - j, 0), pltpu.roll(nxt, tb - j, 0))


def _lru_gates(xb, wr, wi, br, bi, lam):
    xbb = xb.astype(BF16)
    r = _sigmoid(jnp.dot(xbb, wr, preferred_element_type=F32) + br)
    i = _sigmoid(jnp.dot(xbb, wi, preferred_element_type=F32) + bi)
    sp = jnp.maximum(-lam, 0.0) + jnp.log1p(jnp.exp(-jnp.abs(lam)))
    log_a = (-LRU_C) * r * sp
    a = jnp.exp(log_a)
    a2 = a * a
    mult = jnp.sqrt(jnp.maximum(-jnp.tanh(log_a) * (1.0 + a2), 0.0))
    return xbb, r, i, sp, a, a2, mult


def _acore_fwd(proj, conv_w, conv_b, w_r, w_i, b_r, b_i, lam, *, name):
    T, C2 = proj.shape
    C = C2 // 2
    nb, bw, _ = w_r.shape
    tb = min(T, SCAN_BLOCK)

    def body(xp_ref, gate_ref, cw_ref, cb_ref, wr_ref, wi_ref, br_ref, bi_ref, lam_ref,
             xb_ref, h_ref, yg_ref, tail_ref, hlast_ref):
        t = pl.program_id(1)

        @pl.when(t == 0)
        def _():
            tail_ref[...] = jnp.zeros_like(tail_ref)
            hlast_ref[...] = jnp.zeros_like(hlast_ref)

        row = lax.broadcasted_iota(jnp.int32, (tb, bw), 0)
        xp = xp_ref[...]
        tail = tail_ref[...]
        xb = cb_ref[...] + cw_ref[CONV_W - 1:CONV_W, :] * xp
        for j in range(1, CONV_W):
            xb = xb + cw_ref[CONV_W - 1 - j:CONV_W - j, :] * _shift_down(xp, tail, j, row)
        tail_ref[...] = xp[tb - SUBLANES:, :]
        xb_ref[...] = xb

        _, r, i, sp, a, a2, mult = _lru_gates(xb, wr_ref[...], wi_ref[...], br_ref[...], bi_ref[...],
                                              lam_ref[...])
        ca, cb = a, mult * (i * xb)
        s = 1
        while s < tb:
            m = row >= s
            cb = jnp.where(m, ca * pltpu.roll(cb, s, 0) + cb, cb)
            ca = jnp.where(m, ca * pltpu.roll(ca, s, 0), ca)
            s *= 2
        h = cb + ca * hlast_ref[SUBLANES - 1:SUBLANES, :]
        hlast_ref[...] = h[tb - SUBLANES:, :]
        h_ref[...] = h
        gate = gate_ref[...]
        yg_ref[...] = (h * (gate * _sigmoid(gate))).astype(BF16)

    blk = lambda off: pl.BlockSpec((tb, bw), lambda n, t: (t, off + n))
    vec = pl.BlockSpec((1, bw), lambda n, t: (0, n))
    wspec = pl.BlockSpec((None, bw, bw), lambda n, t: (n, 0, 0))
    return _pcall(
        body, name=name, grid=(nb, T // tb),
        in_specs=[blk(0), blk(nb), pl.BlockSpec((CONV_W, bw), lambda n, t: (0, n)), vec, wspec, wspec,
                  vec, vec, vec],
        out_specs=[blk(0), blk(0), blk(0)],
        out_shape=[jax.ShapeDtypeStruct((T, C), F32), jax.ShapeDtypeStruct((T, C), F32),
                   jax.ShapeDtypeStruct((T, C), BF16)],
        scratch_shapes=[pltpu.VMEM((SUBLANES, bw), F32), pltpu.VMEM((SUBLANES, bw), F32)],
        compiler_params=_params("parallel", "arbitrary"))(
            proj, proj, conv_w, conv_b, w_r, w_i, b_r, b_i, lam)


def _acore_bwd(dyg, proj, xb_all, h_all, conv_w, w_r, w_i, b_r, b_i, lam, *, name):
    T, C2 = proj.shape
    C = C2 // 2
    nb, bw, _ = w_r.shape
    tb = min(T, SCAN_BLOCK)
    nt = T // tb
    per8 = tb // SUBLANES

    def body(dyg_ref, xp_ref, gate_ref, xb_ref, h_ref, xp_prev_ref, h_prev_ref, cw_ref,
             wr_ref, wi_ref, br_ref, bi_ref, lam_ref,
             dxp_ref, dgate_ref, dcw_ref, dcb_ref, dbr_ref, dbi_ref, dlam_ref, dwr_ref, dwi_ref,
             gh_next_ref, a_next_ref, dxb_next_ref):
        step = pl.program_id(1)
        first_block = step == nt - 1

        @pl.when(step == 0)
        def _():
            gh_next_ref[...] = jnp.zeros_like(gh_next_ref)
            a_next_ref[...] = jnp.zeros_like(a_next_ref)
            dxb_next_ref[...] = jnp.zeros_like(dxb_next_ref)

        row = lax.broadcasted_iota(jnp.int32, (tb, bw), 0)
        keep = jnp.where(first_block, 0.0, 1.0)
        h_prev = h_prev_ref[...] * keep
        xp_prev = xp_prev_ref[...] * keep
        xp, gate, xb, h, dyg_v = xp_ref[...], gate_ref[...], xb_ref[...], h_ref[...], dyg_ref[...]
        lam_v = lam_ref[...]
        wr, wi = wr_ref[...], wi_ref[...]

        sg = _sigmoid(gate)
        dh = dyg_v * (gate * sg)
        dgate_ref[...] = (dyg_v * h * (sg * (1.0 + gate * (1.0 - sg)))).astype(BF16)

        xbb, r, i, sp, a, a2, mult = _lru_gates(xb, wr, wi, br_ref[...], bi_ref[...], lam_v)

        cg = dh
        cc = _shift_up(a, a_next_ref[...], 1, row)
        s = 1
        while s < tb:
            m = row < tb - s
            cg = jnp.where(m, cc * pltpu.roll(cg, tb - s, 0) + cg, cg)
            cc = jnp.where(m, cc * pltpu.roll(cc, tb - s, 0), cc)
            s *= 2
        gh = cg + cc * gh_next_ref[0:1, :]
        gh_next_ref[...] = gh[0:SUBLANES, :]
        a_next_ref[...] = a[0:SUBLANES, :]

        da = gh * _shift_down(h, h_prev, 1, row)
        dmult = gh * (i * xb)
        di = gh * mult * xb
        dxb = gh * mult * i
        dla = da * a - dmult * jnp.where(mult > 0.0, a2 / mult, 0.0)
        dr = dla * ((-LRU_C) * sp)
        dsp = jnp.sum(dla * ((-LRU_C) * r), axis=0, keepdims=True)
        dlam_part = dsp * (-_sigmoid(-lam_v))
        dpr = dr * r * (1.0 - r)
        dpi = di * i * (1.0 - i)
        dbr_part = jnp.sum(dpr, axis=0, keepdims=True)
        dbi_part = jnp.sum(dpi, axis=0, keepdims=True)
        dprb, dpib = dpr.astype(BF16), dpi.astype(BF16)
        dwr_part = _dot(xbb, dprb, 0, 0)
        dwi_part = _dot(xbb, dpib, 0, 0)
        dxb = dxb + _dot(dprb, wr, 1, 1) + _dot(dpib, wi, 1, 1)

        dxb_next = dxb_next_ref[...]
        dxp = cw_ref[CONV_W - 1:CONV_W, :] * dxb
        for j in range(1, CONV_W):
            dxp = dxp + cw_ref[CONV_W - 1 - j:CONV_W - j, :] * _shift_up(dxb, dxb_next, j, row)
        dxb_next_ref[...] = dxb[0:SUBLANES, :]
        dxp_ref[...] = dxp.astype(BF16)
        dcb_part = jnp.sum(dxb, axis=0, keepdims=True)
        dcw_rows = []
        for k in range(CONV_W):
            j = CONV_W - 1 - k
            sh = xp if j == 0 else _shift_down(xp, xp_prev, j, row)
            dcw_rows.append(jnp.sum(dxb * sh, axis=0, keepdims=True))

        @pl.when(step == 0)
        def _():
            for k in range(CONV_W):
                dcw_ref[k:k + 1, :] = dcw_rows[k]
            dcb_ref[...] = dcb_part
            dbr_ref[...] = dbr_part
            dbi_ref[...] = dbi_part
            dlam_ref[...] = dlam_part
            dwr_ref[...] = dwr_part
            dwi_ref[...] = dwi_part

        @pl.when(step > 0)
        def _():
            for k in range(CONV_W):
                dcw_ref[k:k + 1, :] += dcw_rows[k]
            dcb_ref[...] += dcb_part
            dbr_ref[...] += dbr_part
            dbi_ref[...] += dbi_part
            dlam_ref[...] += dlam_part
            dwr_ref[...] += dwr_part
            dwi_ref[...] += dwi_part

    rev = lambda s: nt - 1 - s
    blk = lambda off: pl.BlockSpec((tb, bw), lambda n, s: (rev(s), off + n))
    prev8 = lambda off: pl.BlockSpec(
        (SUBLANES, bw), lambda n, s: (jnp.maximum(rev(s) * per8 - 1, 0), off + n))
    vec = pl.BlockSpec((1, bw), lambda n, s: (0, n))
    wspec = pl.BlockSpec((None, bw, bw), lambda n, s: (n, 0, 0))
    cwspec = pl.BlockSpec((CONV_W, bw), lambda n, s: (0, n))
    vshape = jax.ShapeDtypeStruct((1, C), F32)
    wshape = jax.ShapeDtypeStruct((nb, bw, bw), F32)
    return _pcall(
        body, name=name, grid=(nb, nt),
        in_specs=[blk(0), blk(0), blk(nb), blk(0), blk(0), prev8(0), prev8(0), cwspec,
                  wspec, wspec, vec, vec, vec],
        out_specs=[blk(0), blk(0), cwspec, vec, vec, vec, vec, wspec, wspec],
        out_shape=[jax.ShapeDtypeStruct((T, C), BF16), jax.ShapeDtypeStruct((T, C), BF16),
                   jax.ShapeDtypeStruct((CONV_W, C), F32), vshape, vshape, vshape, vshape,
                   wshape, wshape],
        scratch_shapes=[pltpu.VMEM((SUBLANES, bw), F32)] * 3,
        compiler_params=_params("parallel", "arbitrary"))(
            dyg, proj, proj, xb_all, h_all, proj, h_all, conv_w, w_r, w_i, b_r, b_i, lam)


def _later_sum(lk, tri):
    return jnp.dot(lk.astype(BF16), tri, preferred_element_type=F32)


def _log2_sigmoids(y):
    t = jnp.log(1.0 + jnp.exp2(-jnp.abs(y))) * LOG2E
    ls = jnp.minimum(y, 0.0) - t
    return ls, ls - y


def _attn_blocks(T):
    bk = min(T, ATT_KEY_BLOCK)
    bq = min(T, ATT_QUERY_BLOCK)
    return bq, bk, bq // bk


def _attn_fwd(q, kv, gate, *, name):
    T, HD = q.shape
    H = HD // HEAD_DIM
    bq, bk, per = _attn_blocks(T)
    scale = 1.0 / math.sqrt(HEAD_DIM)

    def body(q_ref, k_ref, v_ref, g_ref, o_ref, og_ref, lt_ref, w_ref):
        i = pl.program_id(1)
        qv = q_ref[...]
        tr = lax.broadcasted_iota(jnp.int32, (bk, bk), 0)
        tc = lax.broadcasted_iota(jnp.int32, (bk, bk), 1)
        tri = (tr > tc).astype(BF16)
        ahead = (lax.broadcasted_iota(jnp.int32, (bq, bk), 0)
                 - lax.broadcasted_iota(jnp.int32, (bq, bk), 1))

        def starts_of(top):
            return [pl.multiple_of((top - d) * bk, bk) for d in range(per)]

        def scores(top):
            return [_dot(qv, k_ref[pl.ds(ks, bk), :], 1, 1) for ks in starts_of(top)]

        def weights(top, zs, c, mask):
            lss, sums, css, causals = [], [], [], []
            for ks, z in zip(starts_of(top), zs):
                ls, lk = _log2_sigmoids(z * (scale * LOG2E))
                if mask:
                    causals.append(ahead > ks - i * bq)
                    lk = jnp.where(causals[-1], lk, 0.0)
                lss.append(ls)
                sums.append(jnp.sum(lk, axis=1, keepdims=True))
                css.append(_later_sum(lk, tri))
            for d in range(per):
                w = jnp.exp2(lss[d] + (css[d] + c))
                if mask:
                    w = jnp.where(causals[d], w, 0.0)
                w_ref[d] = w.astype(BF16)
                c = c + sums[d]
            return c

        def values(top, acc):
            for d, ks in enumerate(starts_of(top)):
                acc = acc + jnp.dot(w_ref[d], v_ref[pl.ds(ks, bk), :], preferred_element_type=F32)
            return acc

        def step(gg, state):
            acc, c = state
            top = (i - gg) * per + per - 1
            zs = scores(top)
            acc = values(top + per, acc)
            return acc, weights(top, zs, c, False)

        diag_top = i * per + per - 1
        c = weights(diag_top, scores(diag_top), jnp.zeros((bq, 1), F32), True)
        acc, c = lax.fori_loop(1, i + 1, step, (jnp.zeros((bq, HEAD_DIM), F32), c))
        acc = values(per - 1, acc)
        o_ref[...] = acc
        g = g_ref[...]
        og_ref[...] = (acc * (g * _sigmoid(g))).astype(BF16)
        lt_ref[...] = jnp.broadcast_to(c, (bq, HEAD_DIM))

    qspec = pl.BlockSpec((bq, HEAD_DIM), lambda h, i: (i, h))
    return _pcall(
        body, name=name, grid=(H, T // bq),
        in_specs=[qspec, pl.BlockSpec((T, HEAD_DIM), lambda h, i: (0, h)),
                  pl.BlockSpec((T, HEAD_DIM), lambda h, i: (0, H + h)), qspec],
        out_specs=[qspec, qspec, qspec],
        out_shape=[jax.ShapeDtypeStruct((T, HD), F32), jax.ShapeDtypeStruct((T, HD), BF16),
                   jax.ShapeDtypeStruct((T, HD), F32)],
        scratch_shapes=[pltpu.VMEM((per, bq, bk), BF16)],
        compiler_params=_params("parallel", "arbitrary"))(q, kv, kv, gate)


def _attn_bwd(q, kv, gate, o, ltot, dog, *, name):
    T, HD = q.shape
    H = HD // HEAD_DIM
    bq, bk, per = _attn_blocks(T)
    nq = T // bq
    scale = 1.0 / math.sqrt(HEAD_DIM)

    def body(q_ref, k_ref, v_ref, g_ref, o_ref, lt_ref, dog_ref,
             dq_ref, dg_ref, dk_ref, dv_ref, dk_acc, dv_acc, dz_ref, w_ref):
        i = pl.program_id(1)

        @pl.when(i == 0)
        def _():
            dk_acc[...] = jnp.zeros_like(dk_acc)
            dv_acc[...] = jnp.zeros_like(dv_acc)

        qv = q_ref[...]
        g, ov, dogv = g_ref[...], o_ref[...], dog_ref[...]
        sg = _sigmoid(g)
        do = dogv * (g * sg)
        dg_ref[...] = (dogv * ov * (sg * (1.0 + g * (1.0 - sg)))).astype(BF16)
        dob = do.astype(BF16)
        ltot_v = lt_ref[:, 0:1]
        tr = lax.broadcasted_iota(jnp.int32, (bk, bk), 0)
        tc = lax.broadcasted_iota(jnp.int32, (bk, bk), 1)
        tri_later = (tr > tc).astype(BF16)
        tri_excl = (tr < tc).astype(BF16)
        ahead = (lax.broadcasted_iota(jnp.int32, (bq, bk), 0)
                 - lax.broadcasted_iota(jnp.int32, (bq, bk), 1))

        def starts_of(first):
            return [pl.multiple_of((first + d) * bk, bk) for d in range(per)]

        def scores(first):
            return ([_dot(qv, k_ref[pl.ds(ks, bk), :], 1, 1) for ks in starts_of(first)],
                    [_dot(dob, v_ref[pl.ds(ks, bk), :], 1, 1) for ks in starts_of(first)])

        def front(first, zs, dws, p_lk, p_g, mask):
            lss, css, causals = [], [], []
            for ks, z in zip(starts_of(first), zs):
                ls, lk = _log2_sigmoids(z * (scale * LOG2E))
                if mask:
                    causals.append(ahead > ks - i * bq)
                    lk = jnp.where(causals[-1], lk, 0.0)
                lss.append(ls)
                p_lk = p_lk + jnp.sum(lk, axis=1, keepdims=True)
                css.append((ltot_v - p_lk) + _later_sum(lk, tri_later))
            gms, befores = [], []
            for d in range(per):
                w = jnp.exp2(lss[d] + css[d])
                if mask:
                    w = jnp.where(causals[d], w, 0.0)
                gm = dws[d] * w
                gms.append(gm)
                w_ref[d] = w.astype(BF16)
                befores.append(jnp.dot(gm.astype(BF16), tri_excl, preferred_element_type=F32) + p_g)
                p_g = p_g + jnp.sum(gm, axis=1, keepdims=True)
            for d in range(per):
                dz = gms[d] - jnp.exp2(lss[d]) * (gms[d] + befores[d])
                if mask:
                    dz = jnp.where(causals[d], dz, 0.0)
                dz_ref[d] = (dz * scale).astype(BF16)
            return p_lk, p_g

        def back(first, dq):
            for d, ks in enumerate(starts_of(first)):
                dzb = dz_ref[d]
                dq = dq + jnp.dot(dzb, k_ref[pl.ds(ks, bk), :], preferred_element_type=F32)
                dk_acc[pl.ds(ks, bk), :] += _dot(dzb, qv, 0, 0)
                dv_acc[pl.ds(ks, bk), :] += _dot(w_ref[d], dob, 0, 0)
            return dq

        def step(mask):
            def trip(g, state):
                dq, p_lk, p_g = state
                zs, dws = scores(g * per)
                dq = back((g - 1) * per, dq)
                return (dq,) + front(g * per, zs, dws, p_lk, p_g, mask)
            return trip

        zero = jnp.zeros((bq, 1), F32)
        state = (jnp.zeros((bq, HEAD_DIM), F32),) + front(0, *scores(0), zero, zero, True)
        state = lax.fori_loop(1, i, step(False), state)
        state = lax.fori_loop(jnp.maximum(i, 1), i + 1, step(True), state)
        dq_ref[...] = back(i * per, state[0]).astype(BF16)

        @pl.when(i == nq - 1)
        def _():
            dk_ref[...] = dk_acc[...].astype(BF16)
            dv_ref[...] = dv_acc[...].astype(BF16)

    qspec = pl.BlockSpec((bq, HEAD_DIM), lambda h, i: (i, h))
    kspec = pl.BlockSpec((T, HEAD_DIM), lambda h, i: (0, h))
    return _pcall(
        body, name=name, grid=(H, nq),
        in_specs=[qspec, kspec, pl.BlockSpec((T, HEAD_DIM), lambda h, i: (0, H + h)),
                  qspec, qspec, qspec, qspec],
        out_specs=[qspec, qspec, kspec, kspec],
        out_shape=[jax.ShapeDtypeStruct((T, HD), BF16)] * 4,
        scratch_shapes=[pltpu.VMEM((T, HEAD_DIM), F32)] * 2 + [pltpu.VMEM((per, bq, bk), BF16)] * 2,
        compiler_params=_params("parallel", "arbitrary"))(q, kv, kv, gate, o, ltot, dog)


def _position():
    return lax.axis_index("x"), lax.axis_index("y"), lax.axis_index("c")


def _chip_of(k, x, y):
    return (1 - x if k & 1 else x), (1 - y if k & 2 else y)


def _weights_gather(shards):
    n = len(shards)

    def body(*refs):
        ins, outs = refs[:n], refs[n:2 * n]
        send_sems, recv_sems, local_sems = refs[2 * n:]
        x, y, c = _position()
        sibling = (x, y, 1 - c)
        chips = [_chip_of(k, x, y) for k in (1, 2, 3)]

        def copy(a, k, block, to, src=None):
            slot = outs[a].at[4 * block[0] + 2 * block[1] + block[2]]
            return pltpu.make_async_remote_copy(
                src_ref=slot if src is None else src, dst_ref=slot,
                send_sem=send_sems.at[a, k], recv_sem=recv_sems.at[a, k],
                device_id=to, device_id_type=MESH)

        mine = [pltpu.make_async_copy(ins[a], outs[a].at[4 * x + 2 * y + c], local_sems.at[a])
                for a in range(n)]
        for cp in mine:
            cp.start()
        first = []
        for a in range(n):
            first.append(copy(a, 0, (x, y, c), sibling, src=ins[a]))
            first += [copy(a, 1 + j, (x, y, c), (*chip, c), src=ins[a]) for j, chip in enumerate(chips)]
        for cp in first:
            cp.start()
        passed = []
        for j, chip in enumerate(chips):
            for a in range(n):
                copy(a, 1 + j, (*chip, c), (x, y, c)).wait_recv()
                fwd = copy(a, 4 + j, (*chip, c), sibling)
                fwd.start()
                passed.append(fwd)
        for a in range(n):
            copy(a, 0, sibling, (x, y, c)).wait_recv()
            for j, chip in enumerate(chips):
                copy(a, 4 + j, (*chip, 1 - c), (x, y, c)).wait_recv()
        for cp in first + passed:
            cp.wait_send()
        for cp in mine:
            cp.wait()

    return _pcall(
        body, name="weights_gather", in_specs=[ANY] * n, out_specs=[ANY] * n,
        out_shape=[jax.ShapeDtypeStruct((N_DEV,) + s.shape, s.dtype) for s in shards],
        scratch_shapes=[pltpu.SemaphoreType.DMA((n, 7)), pltpu.SemaphoreType.DMA((n, 7)),
                        pltpu.SemaphoreType.DMA((n,))])(*shards)


def _grads_to_sibling(grads, small):
    n = len(grads)

    def body(*refs):
        g_refs, small_ref = refs[:n], refs[n]
        got, small_all = refs[n + 1:2 * n + 1], refs[2 * n + 1]
        send_sems, recv_sems, small_send, small_recv, small_local = refs[2 * n + 2:]
        x, y, c = _position()
        me = 4 * x + 2 * y + c
        remote = []
        for a in range(n):
            for chip in range(4):
                remote.append(pltpu.make_async_remote_copy(
                    src_ref=g_refs[a].at[2 * chip + (1 - c)], dst_ref=got[a].at[chip],
                    send_sem=send_sems.at[a, chip], recv_sem=recv_sems.at[a, chip],
                    device_id=(x, y, 1 - c), device_id_type=MESH))
        peers = []
        for m in range(1, N_DEV):
            px, py, pc = x ^ (m >> 2), y ^ ((m >> 1) & 1), c ^ (m & 1)
            peers.append(pltpu.make_async_remote_copy(
                src_ref=small_ref, dst_ref=small_all.at[me],
                send_sem=small_send.at[m - 1], recv_sem=small_recv.at[m - 1],
                device_id=(px, py, pc), device_id_type=MESH))
        own = pltpu.make_async_copy(small_ref, small_all.at[me], small_local)
        for cp in peers + remote + [own]:
            cp.start()
        for cp in peers + remote:
            cp.wait_send()
        for m in range(1, N_DEV):
            px, py, pc = x ^ (m >> 2), y ^ ((m >> 1) & 1), c ^ (m & 1)
            pltpu.make_async_remote_copy(
                src_ref=small_ref, dst_ref=small_all.at[4 * px + 2 * py + pc],
                send_sem=small_send.at[m - 1], recv_sem=small_recv.at[m - 1],
                device_id=(px, py, pc), device_id_type=MESH).wait_recv()
        for cp in remote:
            cp.wait_recv()
        own.wait()

    part = [jax.ShapeDtypeStruct((4,) + g.shape[1:], g.dtype) for g in grads]
    return _pcall(
        body, name="grads_to_sibling", in_specs=[ANY] * (n + 1), out_specs=[ANY] * (n + 1),
        out_shape=part + [jax.ShapeDtypeStruct((N_DEV,) + small.shape, small.dtype)],
        scratch_shapes=[pltpu.SemaphoreType.DMA((n, 4)), pltpu.SemaphoreType.DMA((n, 4)),
                        pltpu.SemaphoreType.DMA((7,)), pltpu.SemaphoreType.DMA((7,)),
                        pltpu.SemaphoreType.DMA])(*grads, small)


def _grads_to_chips(parts):
    n = len(parts)

    def body(*refs):
        p_refs, outs = refs[:n], refs[n:2 * n]
        send_sems, recv_sems = refs[2 * n:]
        x, y, c = _position()
        copies = []
        for a in range(n):
            for k in range(3):
                cx, cy = _chip_of(k + 1, x, y)
                copies.append(pltpu.make_async_remote_copy(
                    src_ref=p_refs[a].at[2 * cx + cy], dst_ref=outs[a].at[k],
                    send_sem=send_sems.at[a, k], recv_sem=recv_sems.at[a, k],
                    device_id=(cx, cy, c), device_id_type=MESH))
        for cp in copies:
            cp.start()
        for cp in copies:
            cp.wait_send()
        for cp in copies:
            cp.wait_recv()

    return _pcall(
        body, name="grads_to_chips", in_specs=[ANY] * n, out_specs=[ANY] * n,
        out_shape=[jax.ShapeDtypeStruct((3,) + p.shape[1:], p.dtype) for p in parts],
        scratch_shapes=[pltpu.SemaphoreType.DMA((n, 3)), pltpu.SemaphoreType.DMA((n, 3))])(*parts)


def _pair_sum(grad, got, *, name):
    _, R, C = got.shape
    tr = _pick8(R, max(2 * SUBLANES, (1 << 17) // C))

    def body(g_ref, b_ref, o_ref, ob_ref):
        north = lax.axis_index("c") == 1
        for chip in range(4):
            s = jnp.where(north, g_ref[chip, 1], g_ref[chip, 0]) + b_ref[chip]
            o_ref[chip] = s
            ob_ref[chip] = s.astype(BF16)

    spec = pl.BlockSpec((4, tr, C), lambda i: (0, i, 0))
    return _pcall(
        body, name=name, grid=(R // tr,),
        in_specs=[pl.BlockSpec((4, 2, tr, C), lambda i: (0, 0, i, 0)), spec],
        out_specs=[spec, spec],
        out_shape=[jax.ShapeDtypeStruct((4, R, C), F32), jax.ShapeDtypeStruct((4, R, C), BF16)],
        compiler_params=_params("parallel"))(grad.reshape(4, 2, R, C), got)


def _pick8(n, cap):
    if n <= cap:
        return n
    best = None
    for t in range(SUBLANES, cap + 1, SUBLANES):
        if n % t == 0:
            best = t
    assert best is not None, (n, cap)
    return best


def _adamw(w, m, v, parts, *, name, chip_sums=None):
    R, C = w.shape
    tr = _pick8(R, max(SUBLANES, (1 << 17) // C))
    c1 = 1.0 - ADAM_B1 ** ADAM_STEP
    c2 = 1.0 - ADAM_B2 ** ADAM_STEP
    parts = list(parts) if chip_sums is None else [chip_sums] + list(parts)
    np_ = len(parts)

    def body(w_ref, m_ref, v_ref, *refs):
        p_refs = refs[:np_]
        g_ref, d_ref, nm_ref, nv_ref = refs[np_:]
        g = None
        if chip_sums is not None:
            s_ref, p_refs = p_refs[0], p_refs[1:]
            x1, y1 = lax.axis_index("x") == 1, lax.axis_index("y") == 1
            g = jnp.where(x1, jnp.where(y1, s_ref[3], s_ref[2]), jnp.where(y1, s_ref[1], s_ref[0]))
        for p_ref in p_refs:
            for t in [p_ref[k].astype(F32) for k in range(p_ref.shape[0])]:
                g = t if g is None else g + t
        mn = ADAM_B1 * m_ref[...] + (1.0 - ADAM_B1) * g
        vn = ADAM_B2 * v_ref[...] + (1.0 - ADAM_B2) * (g * g)
        d_ref[...] = -ADAM_LR * ((mn / c1) / (jnp.sqrt(vn / c2) + ADAM_EPS) + ADAM_WD * w_ref[...])
        g_ref[...] = g
        nm_ref[...] = mn
        nv_ref[...] = vn

    spec = pl.BlockSpec((tr, C), lambda i: (i, 0))
    pspecs = [pl.BlockSpec((p.shape[0], tr, C), lambda i: (0, i, 0)) for p in parts]
    return _pcall(
        body, name=name, grid=(R // tr,), in_specs=[spec] * 3 + pspecs, out_specs=[spec] * 4,
        out_shape=[jax.ShapeDtypeStruct((R, C), F32)] * 4,
        compiler_params=_params("parallel"))(w, m, v, *parts)


def _rows(a):
    return a.reshape(-1, LANES)


def _whole_from_columns(shards, *, name):
    S, K, n = shards.shape
    tk = _pick8(K, 1024)

    def body(s_ref, o_ref):
        o_ref[...] = s_ref[...]

    return _pcall(
        body, name=name, grid=(K // tk, S),
        in_specs=[pl.BlockSpec((None, tk, n), lambda i, s: (s, i, 0))],
        out_specs=pl.BlockSpec((tk, n), lambda i, s: (i, s)),
        out_shape=jax.ShapeDtypeStruct((K, S * n), shards.dtype),
        compiler_params=_params("parallel", "parallel"))(shards)


def _forward_backward(xs, target, a_norm, g_a_w_in, conv_w, conv_b, g_w_r, g_w_i, b_r, b_i, lam,
                      g_a_w_out, kv_norm, g_w_kv, b_norm, g_b_w_in, g_b_w_out, final_norm):
    (h_a,) = _rms_fwd(xs, [a_norm], name="a_norm_fwd")
    proj_a = _mm_nn(h_a, g_a_w_in, name="a_in_proj", out_dtype=F32)
    xb, h_rec, yg = _acore_fwd(proj_a, conv_w, conv_b, g_w_r, g_w_i, b_r, b_i, lam, name="a_core_fwd")
    x1 = _mm_nn(yg, g_a_w_out, name="a_out_proj", out_dtype=F32, res=xs)
    hk, hb = _rms_fwd(x1, [kv_norm, b_norm], name="kv_b_norm_fwd")
    kv = _mm_nn(hk, g_w_kv, name="kv_proj", out_dtype=BF16)
    hd = g_b_w_in.shape[1] // 2
    q = _mm_nn(hb, g_b_w_in, name="q_proj", out_dtype=BF16, col_off=0, cols=hd)
    gate_b = _mm_nn(hb, g_b_w_in, name="b_gate_proj", out_dtype=F32, col_off=hd, cols=hd)
    o, og, ltot = _attn_fwd(q, kv, gate_b, name="attn_fwd")
    x2 = _mm_nn(og, g_b_w_out, name="b_out_proj", out_dtype=F32, res=x1)
    dx2, d_final_norm, loss_part = _final_loss(x2, target, final_norm, name="final_norm_loss")

    dog = _mm_nt(dx2, g_b_w_out, name="b_out_proj_bwd")
    dw_b_out = _mm_tn(og, dx2, name="b_out_proj_wgrad")
    dq, dgate_b, dk, dv = _attn_bwd(q, kv, gate_b, o, ltot, dog, name="attn_bwd")
    dproj_b = jnp.concatenate([dq, dgate_b], axis=1)
    dkv = jnp.concatenate([dk, dv], axis=1)
    dhb = _mm_nt(dproj_b, g_b_w_in, name="b_in_proj_bwd")
    dw_b_in = _mm_tn(hb, dproj_b, name="b_in_proj_wgrad", shards=N_DEV)
    dhk = _mm_nt(dkv, g_w_kv, name="kv_proj_bwd")
    dw_kv = _mm_tn(hk, dkv, name="kv_proj_wgrad", shards=N_DEV)
    dx1, d_b_norm, d_kv_norm = _rms_bwd(x1, dx2, [dhb, dhk], [b_norm, kv_norm], name="kv_b_norm_bwd")
    dyg = _mm_nt(dx1, g_a_w_out, name="a_out_proj_bwd")
    dw_a_out = _mm_tn(yg, dx1, name="a_out_proj_wgrad")
    (dxp, dgate_a, d_conv_w, d_conv_b, d_b_r, d_b_i, d_lambda, dw_r, dw_i) = _acore_bwd(
        dyg, proj_a, xb, h_rec, conv_w, g_w_r, g_w_i, b_r, b_i, lam, name="a_core_bwd")
    dproj_a = jnp.concatenate([dxp, dgate_a], axis=1)
    dh_a = _mm_nt(dproj_a, g_a_w_in, name="a_in_proj_bwd")
    dw_a_in = _mm_tn(h_a, dproj_a, name="a_in_proj_wgrad", shards=N_DEV)
    grad_x, d_a_norm = _rms_bwd(xs, dx1, [dh_a], [a_norm], name="a_norm_bwd")
    return (loss_part, grad_x, dw_a_in, dw_a_out, dw_kv, dw_b_in, dw_b_out, dw_r, dw_i, d_a_norm,
            d_conv_w, d_conv_b, d_b_r, d_b_i, d_lambda, d_kv_norm, d_b_norm, d_final_norm)


def kernel(x, a_norm, a_w_in, a_conv_w, a_conv_b, a_w_r, a_b_r, a_w_i, a_b_i, a_lambda, a_w_out, kv_norm, w_kv, b_norm, b_w_in, b_w_out, final_norm, loss_target, m_a_norm, m_a_w_in, m_a_conv_w, m_a_conv_b, m_a_w_r, m_a_b_r, m_a_w_i, m_a_b_i, m_a_lambda, m_a_w_out, m_kv_norm, m_w_kv, m_b_norm, m_b_w_in, m_b_w_out, m_final_norm, v_a_norm, v_a_w_in, v_a_conv_w, v_a_conv_b, v_a_w_r, v_a_b_r, v_a_w_i, v_a_b_i, v_a_lambda, v_a_w_out, v_kv_norm, v_w_kv, v_b_norm, v_b_w_in, v_b_w_out, v_final_norm):
    T, D = x.shape[1], x.shape[2]
    nb, bw = a_w_r.shape[1], a_w_r.shape[3]
    C = nb * bw
    me = 4 * lax.axis_index("x") + 2 * lax.axis_index("y") + lax.axis_index("c")
    xs = x[0]
    target = loss_target[0]

    big = [a_w_in[0], a_w_out[0], w_kv, b_w_in[0], b_w_out[0], a_w_r[0], a_w_i[0]]
    sizes = [w.size // LANES for w in big]
    packed = jnp.concatenate([_rows(w.astype(BF16)) for w in big], axis=0)
    small_f32 = jnp.concatenate([_rows(a_conv_w[0]), _rows(b_norm[0])], axis=0)
    pad = (-small_f32.shape[0]) % SUBLANES
    small_f32 = jnp.pad(small_f32, ((0, pad), (0, 0)))
    packed_all, small_all = _weights_gather([packed, small_f32])

    offs = [0]
    for s in sizes:
        offs.append(offs[-1] + s)
    pieces = [packed_all[:, offs[i]:offs[i + 1], :] for i in range(len(big))]
    g_a_w_in = _whole_from_columns(pieces[0].reshape(N_DEV, D, a_w_in.shape[2]), name="a_w_in_whole")
    g_a_w_out = pieces[1].reshape(C, D)
    g_w_kv = _whole_from_columns(pieces[2].reshape(N_DEV, D, w_kv.shape[1]), name="w_kv_whole")
    g_b_w_in = _whole_from_columns(pieces[3].reshape(N_DEV, D, b_w_in.shape[2]), name="b_w_in_whole")
    g_b_w_out = pieces[4].reshape(b_w_out.shape[1] * N_DEV, D)
    rows_r = a_w_r.shape[2]
    g_w_r = pieces[5].reshape(N_DEV, nb, rows_r, bw).transpose(1, 0, 2, 3).reshape(nb, bw, bw)
    g_w_i = pieces[6].reshape(N_DEV, nb, rows_r, bw).transpose(1, 0, 2, 3).reshape(nb, bw, bw)
    cw_rows = a_conv_w.shape[1] * a_conv_w.shape[2] // LANES
    conv_w_full = small_all[:, :cw_rows, :].reshape(N_DEV, CONV_W, a_conv_w.shape[2])
    conv_w_full = conv_w_full.transpose(1, 0, 2).reshape(CONV_W, C)
    bn_rows = b_norm.shape[1] // LANES
    b_norm_full = small_all[:, cw_rows:cw_rows + bn_rows, :].reshape(1, D)
    kv_norm2, final_norm2 = kv_norm.reshape(1, D), final_norm.reshape(1, D)

    (loss_part, grad_x, dw_a_in, dw_a_out, dw_kv, dw_b_in, dw_b_out, dw_r, dw_i, d_a_norm, d_conv_w,
     d_conv_b, d_b_r, d_b_i, d_lambda, d_kv_norm, d_b_norm, d_final_norm) = _forward_backward(
         xs, target, a_norm, g_a_w_in, conv_w_full, a_conv_b, g_w_r, g_w_i, a_b_r, a_b_i, a_lambda,
         g_a_w_out, kv_norm2, g_w_kv, b_norm_full, g_b_w_in, g_b_w_out, final_norm2)

    def lru_shards(dw):
        return dw.reshape(nb, N_DEV, rows_r, bw).transpose(1, 0, 2, 3).reshape(N_DEV, nb * rows_r, bw)

    full = [dw_a_in, dw_a_out.reshape(N_DEV, a_w_out.shape[1], D), dw_kv, dw_b_in,
            dw_b_out.reshape(N_DEV, b_w_out.shape[1], D), lru_shards(dw_r), lru_shards(dw_i)]
    small_parts = [d_a_norm, d_conv_w, d_conv_b, d_b_r, d_b_i, d_lambda, d_kv_norm, d_b_norm, d_final_norm]
    small_sizes = [p.size // LANES for p in small_parts]
    small = jnp.concatenate([_rows(p) for p in small_parts], axis=0)
    outs = _grads_to_sibling(full, small)
    got, small_everyone = outs[:len(full)], outs[-1]
    sums = [_pair_sum(f_, g_, name=f"pair_sum_{i}") for i, (f_, g_) in enumerate(zip(full, got))]
    others = _grads_to_chips([s[1] for s in sums])

    def shard2d(w):
        return w.reshape(-1, w.shape[-1])

    names_big = [(a_w_in, m_a_w_in, v_a_w_in), (a_w_out, m_a_w_out, v_a_w_out), (w_kv, m_w_kv, v_w_kv),
                 (b_w_in, m_b_w_in, v_b_w_in), (b_w_out, m_b_w_out, v_b_w_out),
                 (a_w_r, m_a_w_r, v_a_w_r), (a_w_i, m_a_w_i, v_a_w_i)]
    upd_big = []
    for i, (w, m, v) in enumerate(names_big):
        res = _adamw(shard2d(w), shard2d(m), shard2d(v), [others[i]], chip_sums=sums[i][0], name=f"adamw_{i}")
        upd_big.append([r.reshape(w.shape) for r in res])

    soffs = [0]
    for s in small_sizes:
        soffs.append(soffs[-1] + s)

    def small_piece(i):
        return small_everyone[:, soffs[i]:soffs[i + 1], :]

    cw_cols = a_conv_w.shape[2]
    conv_piece = small_piece(1).reshape(N_DEV, CONV_W, C)
    conv_piece = lax.dynamic_slice_in_dim(conv_piece, me * cw_cols, cw_cols, axis=2)
    conv_piece = conv_piece.reshape(N_DEV, CONV_W * cw_cols // LANES, LANES)
    bn_piece = lax.dynamic_slice_in_dim(small_piece(7), me * bn_rows, bn_rows, axis=1)
    small_g = jnp.concatenate([small_piece(0), conv_piece, small_piece(2), small_piece(3), small_piece(4),
                               small_piece(5), small_piece(6), bn_piece, small_piece(8)], axis=1)
    small_w = [(a_norm, m_a_norm, v_a_norm), (a_conv_w, m_a_conv_w, v_a_conv_w),
               (a_conv_b, m_a_conv_b, v_a_conv_b), (a_b_r, m_a_b_r, v_a_b_r), (a_b_i, m_a_b_i, v_a_b_i),
               (a_lambda, m_a_lambda, v_a_lambda), (kv_norm, m_kv_norm, v_kv_norm),
               (b_norm, m_b_norm, v_b_norm), (final_norm, m_final_norm, v_final_norm)]
    pack = lambda idx: jnp.concatenate([_rows(t[idx]) for t in small_w], axis=0)
    res_small = _adamw(pack(0), pack(1), pack(2), [small_g], name="adamw_small")
    woffs = [0]
    for t in small_w:
        woffs.append(woffs[-1] + t[0].size // LANES)
    upd_small = [[r[woffs[i]:woffs[i + 1]].reshape(small_w[i][0].shape) for r in res_small]
                 for i in range(len(small_w))]

    order = [("s", 0), ("b", 0), ("s", 1), ("s", 2), ("b", 5), ("s", 3), ("b", 6), ("s", 4), ("s", 5),
             ("b", 1), ("s", 6), ("b", 2), ("s", 7), ("b", 3), ("b", 4), ("s", 8)]
    per_weight = [(upd_big if kind == "b" else upd_small)[i] for kind, i in order]
    loss = lax.psum(loss_part[0, 0], ("x", "y", "c"))
    result = [loss, grad_x[None]]
    for field in range(4):
        result += [u[field] for u in per_weight]
    return tuple(result)
```

```python
import functools
import math

import jax
import jax.numpy as jnp
from jax import lax
from jax.experimental import pallas as pl
from jax.experimental.pallas import tpu as pltpu

F32 = jnp.float32
BF16 = jnp.bfloat16
MESH = pl.DeviceIdType.MESH

EPS = 1e-6
LOG2E = 1.4426950408889634
LRU_C = 8.0
CONV_W = 4
HEAD_DIM = 128
ADAM_LR = 0.001
ADAM_B1 = 0.9
ADAM_B2 = 0.999
ADAM_EPS = 1e-08
ADAM_WD = 0.01
ADAM_STEP = 10

N_DEV = 8
LANES = 128
SUBLANES = 8
VMEM_LIMIT = 56 * 1024 * 1024

ATT_KEY_BLOCK = 256
ATT_QUERY_BLOCK = 512
SCAN_BLOCK = 256
ROW_BLOCK = 256
MM_TOKEN_BLOCK = 512
MM_WEIGHT_TILE = 1280
MM_CONTRACT_TOKENS = 2048
ANY = pl.BlockSpec(memory_space=pl.ANY)


def _pcall(body, **kw):
    return pl.pallas_call(body, **kw)


def _params(*sem):
    return pltpu.CompilerParams(dimension_semantics=sem, vmem_limit_bytes=VMEM_LIMIT)


def _pick(n, cap):
    if n <= cap:
        return n
    best = None
    for t in range(LANES, cap + 1, LANES):
        if n % t == 0:
            best = t
    assert best is not None, (n, cap)
    return best


def _sigmoid(x):
    return 1.0 / (1.0 + jnp.exp(-x))


def _dot(a, b, ca, cb):
    return lax.dot_general(a, b, (((ca,), (cb,)), ((), ())), preferred_element_type=F32)


def _mm_nn(a, w, *, name, out_dtype, col_off=0, cols=None, res=None):
    T, K = a.shape
    K2, N = w.shape
    assert K == K2
    cols = N if cols is None else cols
    tm = min(T, MM_TOKEN_BLOCK)
    tn = _pick(cols, MM_WEIGHT_TILE)
    assert col_off % tn == 0
    off = col_off // tn
    has_res = res is not None

    def body(a_ref, b_ref, *rest):
        o_ref = rest[-1]
        acc = jnp.dot(a_ref[...].astype(BF16), b_ref[...], preferred_element_type=F32)
        if has_res:
            acc = acc + rest[0][...]
        o_ref[...] = acc.astype(out_dtype)

    in_specs = [pl.BlockSpec((tm, K), lambda j, i: (i, 0)),
                pl.BlockSpec((K, tn), lambda j, i: (0, off + j))]
    args = [a, w]
    if has_res:
        in_specs.append(pl.BlockSpec((tm, tn), lambda j, i: (i, j)))
        args.append(res)
    return _pcall(
        body, name=name, grid=(cols // tn, T // tm), in_specs=in_specs,
        out_specs=pl.BlockSpec((tm, tn), lambda j, i: (i, j)),
        out_shape=jax.ShapeDtypeStruct((T, cols), out_dtype),
        compiler_params=_params("parallel", "parallel"))(*args)


def _mm_nt(a, w, *, name, out_dtype=F32):
    T, K = a.shape
    N, K2 = w.shape
    assert K == K2
    tm = min(T, MM_TOKEN_BLOCK)
    tn = _pick(N, MM_WEIGHT_TILE)

    def body(a_ref, b_ref, o_ref):
        o_ref[...] = _dot(a_ref[...].astype(BF16), b_ref[...], 1, 1).astype(out_dtype)

    return _pcall(
        body, name=name, grid=(N // tn, T // tm),
        in_specs=[pl.BlockSpec((tm, K), lambda j, i: (i, 0)), pl.BlockSpec((tn, K), lambda j, i: (j, 0))],
        out_specs=pl.BlockSpec((tm, tn), lambda j, i: (i, j)),
        out_shape=jax.ShapeDtypeStruct((T, N), out_dtype),
        compiler_params=_params("parallel", "parallel"))(a, w)


def _mm_tn(a, b, *, name, shards=1):
    T, Ko = a.shape
    T2, N = b.shape
    assert T == T2
    n = N // shards
    tt = min(T, MM_CONTRACT_TOKENS)
    tko = _pick(Ko, 1024)
    tn = _pick(n, 1024)
    per = n // tn

    def body(a_ref, b_ref, o_ref):
        t = pl.program_id(2)
        p = _dot(a_ref[...].astype(BF16), b_ref[...].astype(BF16), 0, 0)

        @pl.when(t == 0)
        def _():
            o_ref[...] = p

        @pl.when(t > 0)
        def _():
            o_ref[...] += p

    if shards == 1:
        out_spec = pl.BlockSpec((tko, tn), lambda i, j, t: (i, j))
        out_shape = jax.ShapeDtypeStruct((Ko, N), F32)
    else:
        out_spec = pl.BlockSpec((None, tko, tn), lambda i, j, t: (j // per, i, j % per))
        out_shape = jax.ShapeDtypeStruct((shards, Ko, n), F32)
    return _pcall(
        body, name=name, grid=(Ko // tko, N // tn, T // tt),
        in_specs=[pl.BlockSpec((tt, tko), lambda i, j, t: (t, i)),
                  pl.BlockSpec((tt, tn), lambda i, j, t: (t, j))],
        out_specs=out_spec, out_shape=out_shape,
        compiler_params=_params("parallel", "parallel", "arbitrary"))(a, b)


def _rms_fwd(x, gains, *, name):
    T, D = x.shape
    tm = min(T, ROW_BLOCK)
    n = len(gains)

    def body(x_ref, *refs):
        xv = x_ref[...]
        xh = xv * lax.rsqrt(jnp.mean(xv * xv, axis=-1, keepdims=True) + EPS)
        for g_ref, o_ref in zip(refs[:n], refs[n:]):
            o_ref[...] = (xh * g_ref[...]).astype(BF16)

    row = pl.BlockSpec((tm, D), lambda i: (i, 0))
    vec = pl.BlockSpec((1, D), lambda i: (0, 0))
    return _pcall(
        body, name=name, grid=(T // tm,), in_specs=[row] + [vec] * n, out_specs=[row] * n,
        out_shape=[jax.ShapeDtypeStruct((T, D), BF16)] * n,
        compiler_params=_params("parallel"))(x, *gains)


def _rms_bwd(x, dres, dhs, gains, *, name, rider=None):
    T, D = x.shape
    tm = min(T, ROW_BLOCK)
    steps = T // tm
    n = len(gains)
    rider = rider or _Rider([], [], [], None)
    nri, nro = len(rider.inputs), len(rider.out_shapes)

    def body(x_ref, dres_ref, *refs):
        dh_refs, g_refs = refs[:n], refs[n:2 * n]
        refs = refs[2 * n:]
        rider_in, refs = refs[:nri], refs[nri:]
        dx_ref, dg_refs = refs[0], refs[1:1 + n]
        rider_out, sems = refs[1 + n:1 + n + nro], refs[1 + n + nro:]
        i = pl.program_id(0)
        if nro:
            start, finish = rider.bind(rider_in, rider_out, sems)
            pl.when(i == 0)(start)
        xv = x_ref[...]
        r = lax.rsqrt(jnp.mean(xv * xv, axis=-1, keepdims=True) + EPS)
        xh = xv * r
        dxh = jnp.zeros_like(xv)
        for dh_ref, g_ref, dg_ref in zip(dh_refs, g_refs, dg_refs):
            dh = dh_ref[...]
            part = jnp.sum(dh * xh, axis=0, keepdims=True)

            @pl.when(i == 0)
            def _():
                dg_ref[...] = part

            @pl.when(i > 0)
            def _():
                dg_ref[...] += part

            dxh = dxh + dh * g_ref[...]
        dx_ref[...] = dres_ref[...] + r * (dxh - xh * jnp.mean(dxh * xh, axis=-1, keepdims=True))
        if nro:
            pl.when(i == steps - 1)(finish)

    row = pl.BlockSpec((tm, D), lambda i: (i, 0))
    vec = pl.BlockSpec((1, D), lambda i: (0, 0))
    return _pcall(
        body, name=name, grid=(steps,), in_specs=[row, row] + [row] * n + [vec] * n + [ANY] * nri,
        out_specs=[row] + [vec] * n + [ANY] * nro,
        out_shape=[jax.ShapeDtypeStruct((T, D), F32)] + [jax.ShapeDtypeStruct((1, D), F32)] * n
                  + rider.out_shapes,
        scratch_shapes=rider.scratch,
        compiler_params=_params("arbitrary"))(x, dres, *dhs, *gains, *rider.inputs)


def _final_loss(x, target, gain, *, name):
    T, D = x.shape
    tm = min(T, ROW_BLOCK)

    def body(x_ref, t_ref, g_ref, dx_ref, dg_ref, loss_ref):
        i = pl.program_id(0)
        xv = x_ref[...]
        g = g_ref[...]
        r = lax.rsqrt(jnp.mean(xv * xv, axis=-1, keepdims=True) + EPS)
        xh = xv * r
        err = xh * g - t_ref[...]
        part_loss = 0.5 * jnp.sum(jnp.mean(err * err, axis=-1, keepdims=True), axis=0, keepdims=True)
        dy = err * (1.0 / D)
        part_g = jnp.sum(dy * xh, axis=0, keepdims=True)

        @pl.when(i == 0)
        def _():
            dg_ref[...] = part_g
            loss_ref[...] = jnp.broadcast_to(part_loss, loss_ref.shape)

        @pl.when(i > 0)
        def _():
            dg_ref[...] += part_g
            loss_ref[...] += jnp.broadcast_to(part_loss, loss_ref.shape)

        dxh = dy * g
        dx_ref[...] = r * (dxh - xh * jnp.mean(dxh * xh, axis=-1, keepdims=True))

    row = pl.BlockSpec((tm, D), lambda i: (i, 0))
    vec = pl.BlockSpec((1, D), lambda i: (0, 0))
    return _pcall(
        body, name=name, grid=(T // tm,), in_specs=[row, row, vec],
        out_specs=[row, vec, pl.BlockSpec((1, LANES), lambda i: (0, 0))],
        out_shape=[jax.ShapeDtypeStruct((T, D), F32), jax.ShapeDtypeStruct((1, D), F32),
                   jax.ShapeDtypeStruct((1, LANES), F32)],
        compiler_params=_params("arbitrary"))(x, target, gain)


def _shift_down(x, prev_tail, j, row):
    tb = x.shape[0]
    prev = jnp.tile(prev_tail, (tb // SUBLANES, 1))
    return jnp.where(row >= j, pltpu.roll(x, j, 0), pltpu.roll(prev, j, 0))


def _shift_up(x, next_head, j, row):
    tb = x.shape[0]
    nxt = jnp.tile(next_head, (tb // SUBLANES, 1))
    return jnp.where(row < tb - j, pltpu.roll(x, tb - j, 0), pltpu.roll(nxt, tb - j, 0))


def _lru_gates(xb, wr, wi, br, bi, lam):
    xbb = xb.astype(BF16)
    r = _sigmoid(jnp.dot(xbb, wr, preferred_element_type=F32) + br)
    i = _sigmoid(jnp.dot(xbb, wi, preferred_element_type=F32) + bi)
    sp = jnp.maximum(-lam, 0.0) + jnp.log1p(jnp.exp(-jnp.abs(lam)))
    log_a = (-LRU_C) * r * sp
    a = jnp.exp(log_a)
    a2 = a * a
    mult = jnp.sqrt(jnp.maximum(-jnp.tanh(log_a) * (1.0 + a2), 0.0))
    return xbb, r, i, sp, a, a2, mult


def _acore_fwd(proj, conv_w, conv_b, w_r, w_i, b_r, b_i, lam, *, name, riders=()):
    T, C2 = proj.shape
    C = C2 // 2
    nb, bw, _ = w_r.shape
    tb = min(T, SCAN_BLOCK)
    nt = T // tb
    nr = len(riders)

    def body(xp_ref, gate_ref, cw_ref, cb_ref, wr_ref, wi_ref, br_ref, bi_ref, lam_ref, *refs):
        rider_in, refs = refs[:nr], refs[nr:]
        xb_ref, h_ref, yg_ref = refs[:3]
        rider_out, refs = refs[3:3 + nr], refs[3 + nr:]
        tail_ref, hlast_ref = refs[:2]
        t = pl.program_id(1)
        if nr:
            gather = _Gather(rider_in, rider_out, *refs[2:])
            pl.when((pl.program_id(0) == 0) & (t == 0))(gather.start)

        @pl.when(t == 0)
        def _():
            tail_ref[...] = jnp.zeros_like(tail_ref)
            hlast_ref[...] = jnp.zeros_like(hlast_ref)

        row = lax.broadcasted_iota(jnp.int32, (tb, bw), 0)
        xp = xp_ref[...]
        tail = tail_ref[...]
        xb = cb_ref[...] + cw_ref[CONV_W - 1:CONV_W, :] * xp
        for j in range(1, CONV_W):
            xb = xb + cw_ref[CONV_W - 1 - j:CONV_W - j, :] * _shift_down(xp, tail, j, row)
        tail_ref[...] = xp[tb - SUBLANES:, :]
        xb_ref[...] = xb

        _, r, i, sp, a, a2, mult = _lru_gates(xb, wr_ref[...], wi_ref[...], br_ref[...], bi_ref[...],
                                              lam_ref[...])
        ca, cb = a, mult * (i * xb)
        s = 1
        while s < tb:
            m = row >= s
            cb = jnp.where(m, ca * pltpu.roll(cb, s, 0) + cb, cb)
            ca = jnp.where(m, ca * pltpu.roll(ca, s, 0), ca)
            s *= 2
        h = cb + ca * hlast_ref[SUBLANES - 1:SUBLANES, :]
        hlast_ref[...] = h[tb - SUBLANES:, :]
        h_ref[...] = h
        gate = gate_ref[...]
        yg_ref[...] = (h * (gate * _sigmoid(gate))).astype(BF16)
        if nr:
            pl.when((pl.program_id(0) == nb - 1) & (t == nt - 1))(gather.finish)

    blk = lambda off: pl.BlockSpec((tb, bw), lambda n, t: (t, off + n))
    vec = pl.BlockSpec((1, bw), lambda n, t: (0, n))
    wspec = pl.BlockSpec((None, bw, bw), lambda n, t: (n, 0, 0))
    return _pcall(
        body, name=name, grid=(nb, nt),
        in_specs=[blk(0), blk(nb), pl.BlockSpec((CONV_W, bw), lambda n, t: (0, n)), vec, wspec, wspec,
                  vec, vec, vec] + [ANY] * nr,
        out_specs=[blk(0), blk(0), blk(0)] + [ANY] * nr,
        out_shape=[jax.ShapeDtypeStruct((T, C), F32), jax.ShapeDtypeStruct((T, C), F32),
                   jax.ShapeDtypeStruct((T, C), BF16)]
                  + [jax.ShapeDtypeStruct((N_DEV,) + r.shape, r.dtype) for r in riders],
        scratch_shapes=[pltpu.VMEM((SUBLANES, bw), F32), pltpu.VMEM((SUBLANES, bw), F32)]
                       + (_Gather.scratch(nr) if nr else []),
        compiler_params=_params("arbitrary" if nr else "parallel", "arbitrary"))(
            proj, proj, conv_w, conv_b, w_r, w_i, b_r, b_i, lam, *riders)


def _acore_bwd(dyg, proj, xb_all, h_all, conv_w, w_r, w_i, b_r, b_i, lam, *, name, rider=None):
    T, C2 = proj.shape
    C = C2 // 2
    nb, bw, _ = w_r.shape
    tb = min(T, SCAN_BLOCK)
    nt = T // tb
    per8 = tb // SUBLANES
    rider = rider or _Rider([], [], [], None)
    nri, nro = len(rider.inputs), len(rider.out_shapes)

    def body(dyg_ref, xp_ref, gate_ref, xb_ref, h_ref, xp_prev_ref, h_prev_ref, cw_ref,
             wr_ref, wi_ref, br_ref, bi_ref, lam_ref, *refs):
        rider_in, refs = refs[:nri], refs[nri:]
        dxp_ref, dgate_ref, dcw_ref, dcb_ref, dbr_ref, dbi_ref, dlam_ref, dwr_ref, dwi_ref = refs[:9]
        rider_out, refs = refs[9:9 + nro], refs[9 + nro:]
        gh_next_ref, a_next_ref, dxb_next_ref = refs[:3]
        step = pl.program_id(1)
        first_block = step == nt - 1
        if nro:
            start, finish = rider.bind(rider_in, rider_out, refs[3:])
            pl.when((pl.program_id(0) == 0) & (step == 0))(start)

        @pl.when(step == 0)
        def _():
            gh_next_ref[...] = jnp.zeros_like(gh_next_ref)
            a_next_ref[...] = jnp.zeros_like(a_next_ref)
            dxb_next_ref[...] = jnp.zeros_like(dxb_next_ref)

        row = lax.broadcasted_iota(jnp.int32, (tb, bw), 0)
        keep = jnp.where(first_block, 0.0, 1.0)
        h_prev = h_prev_ref[...] * keep
        xp_prev = xp_prev_ref[...] * keep
        xp, gate, xb, h, dyg_v = xp_ref[...], gate_ref[...], xb_ref[...], h_ref[...], dyg_ref[...]
        lam_v = lam_ref[...]
        wr, wi = wr_ref[...], wi_ref[...]

        sg = _sigmoid(gate)
        dh = dyg_v * (gate * sg)
        dgate_ref[...] = (dyg_v * h * (sg * (1.0 + gate * (1.0 - sg)))).astype(BF16)

        xbb, r, i, sp, a, a2, mult = _lru_gates(xb, wr, wi, br_ref[...], bi_ref[...], lam_v)

        cg = dh
        cc = _shift_up(a, a_next_ref[...], 1, row)
        s = 1
        while s < tb:
            m = row < tb - s
            cg = jnp.where(m, cc * pltpu.roll(cg, tb - s, 0) + cg, cg)
            cc = jnp.where(m, cc * pltpu.roll(cc, tb - s, 0), cc)
            s *= 2
        gh = cg + cc * gh_next_ref[0:1, :]
        gh_next_ref[...] = gh[0:SUBLANES, :]
        a_next_ref[...] = a[0:SUBLANES, :]

        da = gh * _shift_down(h, h_prev, 1, row)
        dmult = gh * (i * xb)
        di = gh * mult * xb
        dxb = gh * mult * i
        dla = da * a - dmult * jnp.where(mult > 0.0, a2 / mult, 0.0)
        dr = dla * ((-LRU_C) * sp)
        dsp = jnp.sum(dla * ((-LRU_C) * r), axis=0, keepdims=True)
        dlam_part = dsp * (-_sigmoid(-lam_v))
        dpr = dr * r * (1.0 - r)
        dpi = di * i * (1.0 - i)
        dbr_part = jnp.sum(dpr, axis=0, keepdims=True)
        dbi_part = jnp.sum(dpi, axis=0, keepdims=True)
        dprb, dpib = dpr.astype(BF16), dpi.astype(BF16)
        dwr_part = _dot(xbb, dprb, 0, 0)
        dwi_part = _dot(xbb, dpib, 0, 0)
        dxb = dxb + _dot(dprb, wr, 1, 1) + _dot(dpib, wi, 1, 1)

        dxb_next = dxb_next_ref[...]
        dxp = cw_ref[CONV_W - 1:CONV_W, :] * dxb
        for j in range(1, CONV_W):
            dxp = dxp + cw_ref[CONV_W - 1 - j:CONV_W - j, :] * _shift_up(dxb, dxb_next, j, row)
        dxb_next_ref[...] = dxb[0:SUBLANES, :]
        dxp_ref[...] = dxp.astype(BF16)
        dcb_part = jnp.sum(dxb, axis=0, keepdims=True)
        dcw_rows = []
        for k in range(CONV_W):
            j = CONV_W - 1 - k
            sh = xp if j == 0 else _shift_down(xp, xp_prev, j, row)
            dcw_rows.append(jnp.sum(dxb * sh, axis=0, keepdims=True))

        @pl.when(step == 0)
        def _():
            for k in range(CONV_W):
                dcw_ref[k:k + 1, :] = dcw_rows[k]
            dcb_ref[...] = dcb_part
            dbr_ref[...] = dbr_part
            dbi_ref[...] = dbi_part
            dlam_ref[...] = dlam_part
            dwr_ref[...] = dwr_part
            dwi_ref[...] = dwi_part

        @pl.when(step > 0)
        def _():
            for k in range(CONV_W):
                dcw_ref[k:k + 1, :] += dcw_rows[k]
            dcb_ref[...] += dcb_part
            dbr_ref[...] += dbr_part
            dbi_ref[...] += dbi_part
            dlam_ref[...] += dlam_part
            dwr_ref[...] += dwr_part
            dwi_ref[...] += dwi_part

        if nro:
            pl.when((pl.program_id(0) == nb - 1) & (step == nt - 1))(finish)

    rev = lambda s: nt - 1 - s
    blk = lambda off: pl.BlockSpec((tb, bw), lambda n, s: (rev(s), off + n))
    prev8 = lambda off: pl.BlockSpec(
        (SUBLANES, bw), lambda n, s: (jnp.maximum(rev(s) * per8 - 1, 0), off + n))
    vec = pl.BlockSpec((1, bw), lambda n, s: (0, n))
    wspec = pl.BlockSpec((None, bw, bw), lambda n, s: (n, 0, 0))
    cwspec = pl.BlockSpec((CONV_W, bw), lambda n, s: (0, n))
    vshape = jax.ShapeDtypeStruct((1, C), F32)
    wshape = jax.ShapeDtypeStruct((nb, bw, bw), F32)
    return _pcall(
        body, name=name, grid=(nb, nt),
        in_specs=[blk(0), blk(0), blk(nb), blk(0), blk(0), prev8(0), prev8(0), cwspec,
                  wspec, wspec, vec, vec, vec] + [ANY] * nri,
        out_specs=[blk(0), blk(0), cwspec, vec, vec, vec, vec, wspec, wspec] + [ANY] * nro,
        out_shape=[jax.ShapeDtypeStruct((T, C), BF16), jax.ShapeDtypeStruct((T, C), BF16),
                   jax.ShapeDtypeStruct((CONV_W, C), F32), vshape, vshape, vshape, vshape,
                   wshape, wshape] + rider.out_shapes,
        scratch_shapes=[pltpu.VMEM((SUBLANES, bw), F32)] * 3 + rider.scratch,
        compiler_params=_params("arbitrary" if nro else "parallel", "arbitrary"))(
            dyg, proj, proj, xb_all, h_all, proj, h_all, conv_w, w_r, w_i, b_r, b_i, lam, *rider.inputs)


def _later_sum(lk, tri):
    return jnp.dot(lk.astype(BF16), tri, preferred_element_type=F32)


def _log2_sigmoids(y):
    t = jnp.log(1.0 + jnp.exp2(-jnp.abs(y))) * LOG2E
    ls = jnp.minimum(y, 0.0) - t
    return ls, ls - y


def _attn_blocks(T):
    bk = min(T, ATT_KEY_BLOCK)
    bq = min(T, ATT_QUERY_BLOCK)
    return bq, bk, bq // bk


def _attn_fwd(q, kv, gate, *, name):
    T, HD = q.shape
    H = HD // HEAD_DIM
    bq, bk, per = _attn_blocks(T)
    scale = 1.0 / math.sqrt(HEAD_DIM)

    def body(q_ref, k_ref, v_ref, g_ref, o_ref, og_ref, lt_ref, w_ref):
        i = pl.program_id(1)
        qv = q_ref[...]
        tr = lax.broadcasted_iota(jnp.int32, (bk, bk), 0)
        tc = lax.broadcasted_iota(jnp.int32, (bk, bk), 1)
        tri = (tr > tc).astype(BF16)
        ahead = (lax.broadcasted_iota(jnp.int32, (bq, bk), 0)
                 - lax.broadcasted_iota(jnp.int32, (bq, bk), 1))

        def starts_of(top):
            return [pl.multiple_of((top - d) * bk, bk) for d in range(per)]

        def scores(top):
            return [_dot(qv, k_ref[pl.ds(ks, bk), :], 1, 1) for ks in starts_of(top)]

        def weights(top, zs, c, mask):
            lss, sums, css, causals = [], [], [], []
            for ks, z in zip(starts_of(top), zs):
                ls, lk = _log2_sigmoids(z * (scale * LOG2E))
                if mask:
                    causals.append(ahead > ks - i * bq)
                    lk = jnp.where(causals[-1], lk, 0.0)
                lss.append(ls)
                sums.append(jnp.sum(lk, axis=1, keepdims=True))
                css.append(_later_sum(lk, tri))
            for d in range(per):
                w = jnp.exp2(lss[d] + (css[d] + c))
                if mask:
                    w = jnp.where(causals[d], w, 0.0)
                w_ref[d] = w.astype(BF16)
                c = c + sums[d]
            return c

        def values(top, acc):
            for d, ks in enumerate(starts_of(top)):
                acc = acc + jnp.dot(w_ref[d], v_ref[pl.ds(ks, bk), :], preferred_element_type=F32)
            return acc

        def step(gg, state):
            acc, c = state
            top = (i - gg) * per + per - 1
            zs = scores(top)
            acc = values(top + per, acc)
            return acc, weights(top, zs, c, False)

        diag_top = i * per + per - 1
        c = weights(diag_top, scores(diag_top), jnp.zeros((bq, 1), F32), True)
        acc, c = lax.fori_loop(1, i + 1, step, (jnp.zeros((bq, HEAD_DIM), F32), c))
        acc = values(per - 1, acc)
        o_ref[...] = acc
        g = g_ref[...]
        og_ref[...] = (acc * (g * _sigmoid(g))).astype(BF16)
        lt_ref[...] = jnp.broadcast_to(c, (bq, HEAD_DIM))

    qspec = pl.BlockSpec((bq, HEAD_DIM), lambda h, i: (i, h))
    return _pcall(
        body, name=name, grid=(H, T // bq),
        in_specs=[qspec, pl.BlockSpec((T, HEAD_DIM), lambda h, i: (0, h)),
                  pl.BlockSpec((T, HEAD_DIM), lambda h, i: (0, H + h)), qspec],
        out_specs=[qspec, qspec, qspec],
        out_shape=[jax.ShapeDtypeStruct((T, HD), F32), jax.ShapeDtypeStruct((T, HD), BF16),
                   jax.ShapeDtypeStruct((T, HD), F32)],
        scratch_shapes=[pltpu.VMEM((per, bq, bk), BF16)],
        compiler_params=_params("parallel", "arbitrary"))(q, kv, kv, gate)


def _attn_bwd(q, kv, gate, o, ltot, dog, *, name):
    T, HD = q.shape
    H = HD // HEAD_DIM
    bq, bk, per = _attn_blocks(T)
    nq = T // bq
    scale = 1.0 / math.sqrt(HEAD_DIM)

    def body(q_ref, k_ref, v_ref, g_ref, o_ref, lt_ref, dog_ref,
             dq_ref, dg_ref, dk_ref, dv_ref, dk_acc, dv_acc, dz_ref, w_ref):
        i = pl.program_id(1)

        @pl.when(i == 0)
        def _():
            dk_acc[...] = jnp.zeros_like(dk_acc)
            dv_acc[...] = jnp.zeros_like(dv_acc)

        qv = q_ref[...]
        g, ov, dogv = g_ref[...], o_ref[...], dog_ref[...]
        sg = _sigmoid(g)
        do = dogv * (g * sg)
        dg_ref[...] = (dogv * ov * (sg * (1.0 + g * (1.0 - sg)))).astype(BF16)
        dob = do.astype(BF16)
        ltot_v = lt_ref[:, 0:1]
        tr = lax.broadcasted_iota(jnp.int32, (bk, bk), 0)
        tc = lax.broadcasted_iota(jnp.int32, (bk, bk), 1)
        tri_later = (tr > tc).astype(BF16)
        tri_excl = (tr < tc).astype(BF16)
        ahead = (lax.broadcasted_iota(jnp.int32, (bq, bk), 0)
                 - lax.broadcasted_iota(jnp.int32, (bq, bk), 1))

        def starts_of(first):
            return [pl.multiple_of((first + d) * bk, bk) for d in range(per)]

        def scores(first):
            return ([_dot(qv, k_ref[pl.ds(ks, bk), :], 1, 1) for ks in starts_of(first)],
                    [_dot(dob, v_ref[pl.ds(ks, bk), :], 1, 1) for ks in starts_of(first)])

        def front(first, zs, dws, p_lk, p_g, mask):
            lss, css, causals = [], [], []
            for ks, z in zip(starts_of(first), zs):
                ls, lk = _log2_sigmoids(z * (scale * LOG2E))
                if mask:
                    causals.append(ahead > ks - i * bq)
                    lk = jnp.where(causals[-1], lk, 0.0)
                lss.append(ls)
                p_lk = p_lk + jnp.sum(lk, axis=1, keepdims=True)
                css.append((ltot_v - p_lk) + _later_sum(lk, tri_later))
            gms, befores = [], []
            for d in range(per):
                w = jnp.exp2(lss[d] + css[d])
                if mask:
                    w = jnp.where(causals[d], w, 0.0)
                gm = dws[d] * w
                gms.append(gm)
                w_ref[d] = w.astype(BF16)
                befores.append(jnp.dot(gm.astype(BF16), tri_excl, preferred_element_type=F32) + p_g)
                p_g = p_g + jnp.sum(gm, axis=1, keepdims=True)
            for d in range(per):
                dz = gms[d] - jnp.exp2(lss[d]) * (gms[d] + befores[d])
                if mask:
                    dz = jnp.where(causals[d], dz, 0.0)
                dz_ref[d] = (dz * scale).astype(BF16)
            return p_lk, p_g

        def back(first, dq):
            for d, ks in enumerate(starts_of(first)):
                dzb = dz_ref[d]
                dq = dq + jnp.dot(dzb, k_ref[pl.ds(ks, bk), :], preferred_element_type=F32)
                dk_acc[pl.ds(ks, bk), :] += _dot(dzb, qv, 0, 0)
                dv_acc[pl.ds(ks, bk), :] += _dot(w_ref[d], dob, 0, 0)
            return dq

        def step(mask):
            def trip(g, state):
                dq, p_lk, p_g = state
                zs, dws = scores(g * per)
                dq = back((g - 1) * per, dq)
                return (dq,) + front(g * per, zs, dws, p_lk, p_g, mask)
            return trip

        zero = jnp.zeros((bq, 1), F32)
        state = (jnp.zeros((bq, HEAD_DIM), F32),) + front(0, *scores(0), zero, zero, True)
        state = lax.fori_loop(1, i, step(False), state)
        state = lax.fori_loop(jnp.maximum(i, 1), i + 1, step(True), state)
        dq_ref[...] = back(i * per, state[0]).astype(BF16)

        @pl.when(i == nq - 1)
        def _():
            dk_ref[...] = dk_acc[...].astype(BF16)
            dv_ref[...] = dv_acc[...].astype(BF16)

    qspec = pl.BlockSpec((bq, HEAD_DIM), lambda h, i: (i, h))
    kspec = pl.BlockSpec((T, HEAD_DIM), lambda h, i: (0, h))
    return _pcall(
        body, name=name, grid=(H, nq),
        in_specs=[qspec, kspec, pl.BlockSpec((T, HEAD_DIM), lambda h, i: (0, H + h)),
                  qspec, qspec, qspec, qspec],
        out_specs=[qspec, qspec, kspec, kspec],
        out_shape=[jax.ShapeDtypeStruct((T, HD), BF16)] * 4,
        scratch_shapes=[pltpu.VMEM((T, HEAD_DIM), F32)] * 2 + [pltpu.VMEM((per, bq, bk), BF16)] * 2,
        compiler_params=_params("parallel", "arbitrary"))(q, kv, kv, gate, o, ltot, dog)


def _position():
    return lax.axis_index("x"), lax.axis_index("y"), lax.axis_index("c")


def _chip_of(k, x, y):
    return (1 - x if k & 1 else x), (1 - y if k & 2 else y)


class _Gather:
    @staticmethod
    def scratch(n):
        return [pltpu.SemaphoreType.DMA((n, 7)), pltpu.SemaphoreType.DMA((n, 7)),
                pltpu.SemaphoreType.DMA((n,))]

    def __init__(self, ins, outs, send_sems, recv_sems, local_sems):
        self.ins, self.outs, self.n = ins, outs, len(ins)
        self.send_sems, self.recv_sems, self.local_sems = send_sems, recv_sems, local_sems
        x, y, c = _position()
        self.me, self.sibling = (x, y, c), (x, y, 1 - c)
        self.chips = [_chip_of(k, x, y) for k in (1, 2, 3)]

    def copy(self, a, k, block, to, src=None):
        slot = self.outs[a].at[4 * block[0] + 2 * block[1] + block[2]]
        return pltpu.make_async_remote_copy(
            src_ref=slot if src is None else src, dst_ref=slot,
            send_sem=self.send_sems.at[a, k], recv_sem=self.recv_sems.at[a, k],
            device_id=to, device_id_type=MESH)

    def own_copies(self):
        x, y, c = self.me
        mine = [pltpu.make_async_copy(self.ins[a], self.outs[a].at[4 * x + 2 * y + c], self.local_sems.at[a])
                for a in range(self.n)]
        first = []
        for a in range(self.n):
            first.append(self.copy(a, 0, self.me, self.sibling, src=self.ins[a]))
            first += [self.copy(a, 1 + j, self.me, (*chip, c), src=self.ins[a])
                      for j, chip in enumerate(self.chips)]
        return mine, first

    def start(self):
        mine, first = self.own_copies()
        for cp in mine + first:
            cp.start()

    def finish(self):
        c = self.me[2]
        mine, first = self.own_copies()
        passed = []
        for j, chip in enumerate(self.chips):
            for a in range(self.n):
                self.copy(a, 1 + j, (*chip, c), self.me).wait_recv()
                fwd = self.copy(a, 4 + j, (*chip, c), self.sibling)
                fwd.start()
                passed.append(fwd)
        for a in range(self.n):
            self.copy(a, 0, self.sibling, self.me).wait_recv()
            for j, chip in enumerate(self.chips):
                self.copy(a, 4 + j, (*chip, 1 - c), self.me).wait_recv()
        for cp in first + passed:
            cp.wait_send()
        for cp in mine:
            cp.wait()


def _weights_gather(shards):
    n = len(shards)

    def body(*refs):
        gather = _Gather(refs[:n], refs[n:2 * n], *refs[2 * n:])
        gather.start()
        gather.finish()

    return _pcall(
        body, name="weights_gather", in_specs=[ANY] * n, out_specs=[ANY] * n,
        out_shape=[jax.ShapeDtypeStruct((N_DEV,) + s.shape, s.dtype) for s in shards],
        scratch_shapes=_Gather.scratch(n))(*shards)


class _Rider:
    def __init__(self, inputs, out_shapes, scratch, copies):
        self.inputs, self.out_shapes, self.scratch, self.copies = inputs, out_shapes, scratch, copies

    def bind(self, ins, outs, sems):
        def start():
            for cp in self.copies(ins, outs, sems):
                cp.start()

        def finish():
            cps = self.copies(ins, outs, sems)
            for cp in cps:
                cp.wait_send()
            for cp in cps:
                cp.wait_recv()

        return start, finish


def _sibling_rider(grads):
    n = len(grads)

    def copies(ins, outs, sems):
        x, y, c = _position()
        return [pltpu.make_async_remote_copy(
            src_ref=ins[a].at[2 * chip + (1 - c)], dst_ref=outs[a].at[chip],
            send_sem=sems[0].at[a, chip], recv_sem=sems[1].at[a, chip],
            device_id=(x, y, 1 - c), device_id_type=MESH) for a in range(n) for chip in range(4)]

    return _Rider(list(grads), [jax.ShapeDtypeStruct((4,) + g.shape[1:], g.dtype) for g in grads],
                  [pltpu.SemaphoreType.DMA((n, 4)), pltpu.SemaphoreType.DMA((n, 4))], copies)


def _chips_rider(parts):
    n = len(parts)

    def copies(ins, outs, sems):
        x, y, c = _position()
        cps = []
        for a in range(n):
            for k in range(3):
                cx, cy = _chip_of(k + 1, x, y)
                cps.append(pltpu.make_async_remote_copy(
                    src_ref=ins[a].at[2 * cx + cy], dst_ref=outs[a].at[k],
                    send_sem=sems[0].at[a, k], recv_sem=sems[1].at[a, k],
                    device_id=(cx, cy, c), device_id_type=MESH))
        return cps

    return _Rider(list(parts), [jax.ShapeDtypeStruct((3,) + p.shape[1:], p.dtype) for p in parts],
                  [pltpu.SemaphoreType.DMA((n, 3)), pltpu.SemaphoreType.DMA((n, 3))], copies)


def _grads_to_sibling(grads, small):
    n = len(grads)
    rider = _sibling_rider(grads)

    def body(*refs):
        g_refs, small_ref = refs[:n], refs[n]
        got, small_all = refs[n + 1:2 * n + 1], refs[2 * n + 1]
        send_sems, recv_sems, small_send, small_recv, small_local = refs[2 * n + 2:]
        x, y, c = _position()
        me = 4 * x + 2 * y + c
        remote = rider.copies(g_refs, got, (send_sems, recv_sems))
        peers = []
        for m in range(1, N_DEV):
            px, py, pc = x ^ (m >> 2), y ^ ((m >> 1) & 1), c ^ (m & 1)
            peers.append(pltpu.make_async_remote_copy(
                src_ref=small_ref, dst_ref=small_all.at[me],
                send_sem=small_send.at[m - 1], recv_sem=small_recv.at[m - 1],
                device_id=(px, py, pc), device_id_type=MESH))
        own = pltpu.make_async_copy(small_ref, small_all.at[me], small_local)
        for cp in peers + remote + [own]:
            cp.start()
        for cp in peers + remote:
            cp.wait_send()
        for m in range(1, N_DEV):
            px, py, pc = x ^ (m >> 2), y ^ ((m >> 1) & 1), c ^ (m & 1)
            pltpu.make_async_remote_copy(
                src_ref=small_ref, dst_ref=small_all.at[4 * px + 2 * py + pc],
                send_sem=small_send.at[m - 1], recv_sem=small_recv.at[m - 1],
                device_id=(px, py, pc), device_id_type=MESH).wait_recv()
        for cp in remote:
            cp.wait_recv()
        own.wait()

    part = [jax.ShapeDtypeStruct((4,) + g.shape[1:], g.dtype) for g in grads]
    return _pcall(
        body, name="grads_to_sibling", in_specs=[ANY] * (n + 1), out_specs=[ANY] * (n + 1),
        out_shape=part + [jax.ShapeDtypeStruct((N_DEV,) + small.shape, small.dtype)],
        scratch_shapes=[pltpu.SemaphoreType.DMA((n, 4)), pltpu.SemaphoreType.DMA((n, 4)),
                        pltpu.SemaphoreType.DMA((7,)), pltpu.SemaphoreType.DMA((7,)),
                        pltpu.SemaphoreType.DMA])(*grads, small)


def _grads_to_chips(parts):
    n = len(parts)
    rider = _chips_rider(parts)

    def body(*refs):
        start, finish = rider.bind(refs[:n], refs[n:2 * n], refs[2 * n:])
        start()
        finish()

    return _pcall(
        body, name="grads_to_chips", in_specs=[ANY] * n, out_specs=[ANY] * n,
        out_shape=rider.out_shapes, scratch_shapes=rider.scratch)(*parts)


def _pair_sum(grad, got, *, name):
    _, R, C = got.shape
    tr = _pick8(R, max(2 * SUBLANES, (1 << 17) // C))

    def body(g_ref, b_ref, o_ref, ob_ref):
        north = lax.axis_index("c") == 1
        for chip in range(4):
            s = jnp.where(north, g_ref[chip, 1], g_ref[chip, 0]) + b_ref[chip]
            o_ref[chip] = s
            ob_ref[chip] = s.astype(BF16)

    spec = pl.BlockSpec((4, tr, C), lambda i: (0, i, 0))
    return _pcall(
        body, name=name, grid=(R // tr,),
        in_specs=[pl.BlockSpec((4, 2, tr, C), lambda i: (0, 0, i, 0)), spec],
        out_specs=[spec, spec],
        out_shape=[jax.ShapeDtypeStruct((4, R, C), F32), jax.ShapeDtypeStruct((4, R, C), BF16)],
        compiler_params=_params("parallel"))(grad.reshape(4, 2, R, C), got)


def _pick8(n, cap):
    if n <= cap:
        return n
    best = None
    for t in range(SUBLANES, cap + 1, SUBLANES):
        if n % t == 0:
            best = t
    assert best is not None, (n, cap)
    return best


def _adamw(w, m, v, parts, *, name, chip_sums=None):
    R, C = w.shape
    tr = _pick8(R, max(SUBLANES, (1 << 17) // C))
    c1 = 1.0 - ADAM_B1 ** ADAM_STEP
    c2 = 1.0 - ADAM_B2 ** ADAM_STEP
    parts = list(parts) if chip_sums is None else [chip_sums] + list(parts)
    np_ = len(parts)

    def body(w_ref, m_ref, v_ref, *refs):
        p_refs = refs[:np_]
        g_ref, d_ref, nm_ref, nv_ref = refs[np_:]
        g = None
        if chip_sums is not None:
            s_ref, p_refs = p_refs[0], p_refs[1:]
            x1, y1 = lax.axis_index("x") == 1, lax.axis_index("y") == 1
            g = jnp.where(x1, jnp.where(y1, s_ref[3], s_ref[2]), jnp.where(y1, s_ref[1], s_ref[0]))
        for p_ref in p_refs:
            for t in [p_ref[k].astype(F32) for k in range(p_ref.shape[0])]:
                g = t if g is None else g + t
        mn = ADAM_B1 * m_ref[...] + (1.0 - ADAM_B1) * g
        vn = ADAM_B2 * v_ref[...] + (1.0 - ADAM_B2) * (g * g)
        d_ref[...] = -ADAM_LR * ((mn / c1) / (jnp.sqrt(vn / c2) + ADAM_EPS) + ADAM_WD * w_ref[...])
        g_ref[...] = g
        nm_ref[...] = mn
        nv_ref[...] = vn

    spec = pl.BlockSpec((tr, C), lambda i: (i, 0))
    pspecs = [pl.BlockSpec((p.shape[0], tr, C), lambda i: (0, i, 0)) for p in parts]
    return _pcall(
        body, name=name, grid=(R // tr,), in_specs=[spec] * 3 + pspecs, out_specs=[spec] * 4,
        out_shape=[jax.ShapeDtypeStruct((R, C), F32)] * 4,
        compiler_params=_params("parallel"))(w, m, v, *parts)


def _rows(a):
    return a.reshape(-1, LANES)


def _whole_from_columns(shards, *, name):
    S, K, n = shards.shape
    tk = _pick8(K, 1024)

    def body(s_ref, o_ref):
        o_ref[...] = s_ref[...]

    return _pcall(
        body, name=name, grid=(K // tk, S),
        in_specs=[pl.BlockSpec((None, tk, n), lambda i, s: (s, i, 0))],
        out_specs=pl.BlockSpec((tk, n), lambda i, s: (i, s)),
        out_shape=jax.ShapeDtypeStruct((K, S * n), shards.dtype),
        compiler_params=_params("parallel", "parallel"))(shards)


def _late_weights(a_w_out_rows, w_kv_cols, b_w_in_cols, b_w_out_rows):
    whole_rows = lambda g: g.reshape(g.shape[0] * g.shape[1], g.shape[2])
    return (whole_rows(a_w_out_rows), _whole_from_columns(w_kv_cols, name="w_kv_whole"),
            _whole_from_columns(b_w_in_cols, name="b_w_in_whole"), whole_rows(b_w_out_rows))


def _forward_backward(xs, target, a_norm, g_a_w_in, conv_w, conv_b, g_w_r, g_w_i, b_r, b_i, lam,
                      kv_norm, b_norm, final_norm, *, late_weights=None, late_shards=None):
    (h_a,) = _rms_fwd(xs, [a_norm], name="a_norm_fwd")
    proj_a = _mm_nn(h_a, g_a_w_in, name="a_in_proj", out_dtype=F32)
    xb, h_rec, yg, *gathered = _acore_fwd(proj_a, conv_w, conv_b, g_w_r, g_w_i, b_r, b_i, lam,
                                          name="a_core_fwd", riders=late_shards or ())
    g_a_w_out, g_w_kv, g_b_w_in, g_b_w_out = _late_weights(*gathered) if late_shards else late_weights
    x1 = _mm_nn(yg, g_a_w_out, name="a_out_proj", out_dtype=F32, res=xs)
    hk, hb = _rms_fwd(x1, [kv_norm, b_norm], name="kv_b_norm_fwd")
    kv = _mm_nn(hk, g_w_kv, name="kv_proj", out_dtype=BF16)
    hd = g_b_w_in.shape[1] // 2
    q = _mm_nn(hb, g_b_w_in, name="q_proj", out_dtype=BF16, col_off=0, cols=hd)
    gate_b = _mm_nn(hb, g_b_w_in, name="b_gate_proj", out_dtype=F32, col_off=hd, cols=hd)
    o, og, ltot = _attn_fwd(q, kv, gate_b, name="attn_fwd")
    x2 = _mm_nn(og, g_b_w_out, name="b_out_proj", out_dtype=F32, res=x1)
    dx2, d_final_norm, loss_part = _final_loss(x2, target, final_norm, name="final_norm_loss")

    dog = _mm_nt(dx2, g_b_w_out, name="b_out_proj_bwd")
    dw_b_out = _mm_tn(og, dx2, name="b_out_proj_wgrad")
    dq, dgate_b, dk, dv = _attn_bwd(q, kv, gate_b, o, ltot, dog, name="attn_bwd")
    dproj_b = jnp.concatenate([dq, dgate_b], axis=1)
    dkv = jnp.concatenate([dk, dv], axis=1)
    dhb = _mm_nt(dproj_b, g_b_w_in, name="b_in_proj_bwd")
    dw_b_in = _mm_tn(hb, dproj_b, name="b_in_proj_wgrad", shards=N_DEV)
    dhk = _mm_nt(dkv, g_w_kv, name="kv_proj_bwd")
    dw_kv = _mm_tn(hk, dkv, name="kv_proj_wgrad", shards=N_DEV)
    early = [dw_kv, dw_b_in, dw_b_out.reshape(N_DEV, -1, dw_b_out.shape[1])] if late_shards else []
    dx1, d_b_norm, d_kv_norm, *got = _rms_bwd(x1, dx2, [dhb, dhk], [b_norm, kv_norm], name="kv_b_norm_bwd",
                                              rider=_sibling_rider(early) if early else None)
    early_sums = [_pair_sum(f_, g_, name=f"pair_sum_early_{i}") for i, (f_, g_) in enumerate(zip(early, got))]
    dyg = _mm_nt(dx1, g_a_w_out, name="a_out_proj_bwd")
    dw_a_out = _mm_tn(yg, dx1, name="a_out_proj_wgrad")
    (dxp, dgate_a, d_conv_w, d_conv_b, d_b_r, d_b_i, d_lambda, dw_r, dw_i, *early_others) = _acore_bwd(
        dyg, proj_a, xb, h_rec, conv_w, g_w_r, g_w_i, b_r, b_i, lam, name="a_core_bwd",
        rider=_chips_rider([s[1] for s in early_sums]) if early else None)
    dproj_a = jnp.concatenate([dxp, dgate_a], axis=1)
    dh_a = _mm_nt(dproj_a, g_a_w_in, name="a_in_proj_bwd")
    dw_a_in = _mm_tn(h_a, dproj_a, name="a_in_proj_wgrad", shards=N_DEV)
    grad_x, d_a_norm = _rms_bwd(xs, dx1, [dh_a], [a_norm], name="a_norm_bwd")
    return (loss_part, grad_x, dw_a_in, dw_a_out, dw_kv, dw_b_in, dw_b_out, dw_r, dw_i, d_a_norm,
            d_conv_w, d_conv_b, d_b_r, d_b_i, d_lambda, d_kv_norm, d_b_norm, d_final_norm,
            early_sums, early_others)


def kernel(x, a_norm, a_w_in, a_conv_w, a_conv_b, a_w_r, a_b_r, a_w_i, a_b_i, a_lambda, a_w_out, kv_norm, w_kv, b_norm, b_w_in, b_w_out, final_norm, loss_target, m_a_norm, m_a_w_in, m_a_conv_w, m_a_conv_b, m_a_w_r, m_a_b_r, m_a_w_i, m_a_b_i, m_a_lambda, m_a_w_out, m_kv_norm, m_w_kv, m_b_norm, m_b_w_in, m_b_w_out, m_final_norm, v_a_norm, v_a_w_in, v_a_conv_w, v_a_conv_b, v_a_w_r, v_a_b_r, v_a_w_i, v_a_b_i, v_a_lambda, v_a_w_out, v_kv_norm, v_w_kv, v_b_norm, v_b_w_in, v_b_w_out, v_final_norm):
    T, D = x.shape[1], x.shape[2]
    nb, bw = a_w_r.shape[1], a_w_r.shape[3]
    C = nb * bw
    me = 4 * lax.axis_index("x") + 2 * lax.axis_index("y") + lax.axis_index("c")
    xs = x[0]
    target = loss_target[0]

    rows_r = a_w_r.shape[2]
    small_f32 = jnp.concatenate([_rows(a_conv_w[0]), _rows(b_norm[0])], axis=0)
    pad = (-small_f32.shape[0]) % SUBLANES
    small_f32 = jnp.pad(small_f32, ((0, pad), (0, 0)))
    a_w_in_cols, w_r_rows, w_i_rows, small_all = _weights_gather(
        [a_w_in[0].astype(BF16), a_w_r[0].reshape(nb * rows_r, bw).astype(BF16),
         a_w_i[0].reshape(nb * rows_r, bw).astype(BF16), small_f32])
    late_shards = [a_w_out[0].astype(BF16), w_kv.astype(BF16), b_w_in[0].astype(BF16), b_w_out[0].astype(BF16)]
    g_a_w_in = _whole_from_columns(a_w_in_cols, name="a_w_in_whole")
    g_w_r = w_r_rows.reshape(N_DEV, nb, rows_r, bw).transpose(1, 0, 2, 3).reshape(nb, bw, bw)
    g_w_i = w_i_rows.reshape(N_DEV, nb, rows_r, bw).transpose(1, 0, 2, 3).reshape(nb, bw, bw)
    cw_rows = a_conv_w.shape[1] * a_conv_w.shape[2] // LANES
    conv_w_full = small_all[:, :cw_rows, :].reshape(N_DEV, CONV_W, a_conv_w.shape[2])
    conv_w_full = conv_w_full.transpose(1, 0, 2).reshape(CONV_W, C)
    bn_rows = b_norm.shape[1] // LANES
    b_norm_full = small_all[:, cw_rows:cw_rows + bn_rows, :].reshape(1, D)
    kv_norm2, final_norm2 = kv_norm.reshape(1, D), final_norm.reshape(1, D)

    (loss_part, grad_x, dw_a_in, dw_a_out, dw_kv, dw_b_in, dw_b_out, dw_r, dw_i, d_a_norm, d_conv_w,
     d_conv_b, d_b_r, d_b_i, d_lambda, d_kv_norm, d_b_norm, d_final_norm, early_sums,
     early_others) = _forward_backward(
         xs, target, a_norm, g_a_w_in, conv_w_full, a_conv_b, g_w_r, g_w_i, a_b_r, a_b_i, a_lambda,
         kv_norm2, b_norm_full, final_norm2, late_shards=late_shards)

    def lru_shards(dw):
        return dw.reshape(nb, N_DEV, rows_r, bw).transpose(1, 0, 2, 3).reshape(N_DEV, nb * rows_r, bw)

    full = [dw_a_in, dw_a_out.reshape(N_DEV, a_w_out.shape[1], D), lru_shards(dw_r), lru_shards(dw_i)]
    small_parts = [d_a_norm, d_conv_w, d_conv_b, d_b_r, d_b_i, d_lambda, d_kv_norm, d_b_norm, d_final_norm]
    small_sizes = [p.size // LANES for p in small_parts]
    small = jnp.concatenate([_rows(p) for p in small_parts], axis=0)
    outs = _grads_to_sibling(full, small)
    got, small_everyone = outs[:len(full)], outs[-1]
    sums = [_pair_sum(f_, g_, name=f"pair_sum_{i}") for i, (f_, g_) in enumerate(zip(full, got))]
    others = _grads_to_chips([s[1] for s in sums])
    sums = sums[:2] + list(early_sums) + sums[2:]
    others = list(others[:2]) + list(early_others) + list(others[2:])

    def shard2d(w):
        return w.reshape(-1, w.shape[-1])

    names_big = [(a_w_in, m_a_w_in, v_a_w_in), (a_w_out, m_a_w_out, v_a_w_out), (w_kv, m_w_kv, v_w_kv),
                 (b_w_in, m_b_w_in, v_b_w_in), (b_w_out, m_b_w_out, v_b_w_out),
                 (a_w_r, m_a_w_r, v_a_w_r), (a_w_i, m_a_w_i, v_a_w_i)]
    upd_big = []
    for i, (w, m, v) in enumerate(names_big):
        res = _adamw(shard2d(w), shard2d(m), shard2d(v), [others[i]], chip_sums=sums[i][0], name=f"adamw_{i}")
        upd_big.append([r.reshape(w.shape) for r in res])

    soffs = [0]
    for s in small_sizes:
        soffs.append(soffs[-1] + s)

    def small_piece(i):
        return small_everyone[:, soffs[i]:soffs[i + 1], :]

    cw_cols = a_conv_w.shape[2]
    conv_piece = small_piece(1).reshape(N_DEV, CONV_W, C)
    conv_piece = lax.dynamic_slice_in_dim(conv_piece, me * cw_cols, cw_cols, axis=2)
    conv_piece = conv_piece.reshape(N_DEV, CONV_W * cw_cols // LANES, LANES)
    bn_piece = lax.dynamic_slice_in_dim(small_piece(7), me * bn_rows, bn_rows, axis=1)
    small_g = jnp.concatenate([small_piece(0), conv_piece, small_piece(2), small_piece(3), small_piece(4),
                               small_piece(5), small_piece(6), bn_piece, small_piece(8)], axis=1)
    small_w = [(a_norm, m_a_norm, v_a_norm), (a_conv_w, m_a_conv_w, v_a_conv_w),
               (a_conv_b, m_a_conv_b, v_a_conv_b), (a_b_r, m_a_b_r, v_a_b_r), (a_b_i, m_a_b_i, v_a_b_i),
               (a_lambda, m_a_lambda, v_a_lambda), (kv_norm, m_kv_norm, v_kv_norm),
               (b_norm, m_b_norm, v_b_norm), (final_norm, m_final_norm, v_final_norm)]
    pack = lambda idx: jnp.concatenate([_rows(t[idx]) for t in small_w], axis=0)
    res_small = _adamw(pack(0), pack(1), pack(2), [small_g], name="adamw_small")
    woffs = [0]
    for t in small_w:
        woffs.append(woffs[-1] + t[0].size // LANES)
    upd_small = [[r[woffs[i]:woffs[i + 1]].reshape(small_w[i][0].shape) for r in res_small]
                 for i in range(len(small_w))]

    order = [("s", 0), ("b", 0), ("s", 1), ("s", 2), ("b", 5), ("s", 3), ("b", 6), ("s", 4), ("s", 5),
             ("b", 1), ("s", 6), ("b", 2), ("s", 7), ("b", 3), ("b", 4), ("s", 8)]
    per_weight = [(upd_big if kind == "b" else upd_small)[i] for kind, i in order]
    loss = lax.psum(loss_part[0, 0], ("x", "y", "c"))
    result = [loss, grad_x[None]]
    for field in range(4):
        result += [u[field] for u in per_weight]
    return tuple(result)
```

```python
import functools
import math

import jax
import jax.numpy as jnp
from jax import lax
from jax.experimental import pallas as pl
from jax.experimental.pallas import tpu as pltpu

F32 = jnp.float32
BF16 = jnp.bfloat16
MESH = pl.DeviceIdType.MESH

EPS = 1e-6
LOG2E = 1.4426950408889634
LRU_C = 8.0
CONV_W = 4
HEAD_DIM = 128
ADAM_LR = 0.001
ADAM_B1 = 0.9
ADAM_B2 = 0.999
ADAM_EPS = 1e-08
ADAM_WD = 0.01
ADAM_STEP = 10

N_DEV = 8
LANES = 128
SUBLANES = 8
VMEM_LIMIT = 56 * 1024 * 1024

ATT_KEY_BLOCK = 256
ATT_QUERY_BLOCK = 512
SCAN_BLOCK = 256
ROW_BLOCK = 256
MM_TOKEN_BLOCK = 512
MM_WEIGHT_TILE = 1280
MM_CONTRACT_TOKENS = 2048
ANY = pl.BlockSpec(memory_space=pl.ANY)


def _pcall(body, **kw):
    return pl.pallas_call(body, **kw)


def _params(*sem):
    return pltpu.CompilerParams(dimension_semantics=sem, vmem_limit_bytes=VMEM_LIMIT)


def _pick(n, cap):
    if n <= cap:
        return n
    best = None
    for t in range(LANES, cap + 1, LANES):
        if n % t == 0:
            best = t
    assert best is not None, (n, cap)
    return best


def _sigmoid(x):
    return 1.0 / (1.0 + jnp.exp(-x))


def _dot(a, b, ca, cb):
    return lax.dot_general(a, b, (((ca,), (cb,)), ((), ())), preferred_element_type=F32)


def _mm_nn(a, w, *, name, out_dtype, col_off=0, cols=None, res=None):
    T, K = a.shape
    K2, N = w.shape
    assert K == K2
    cols = N if cols is None else cols
    tm = min(T, MM_TOKEN_BLOCK)
    tn = _pick(cols, MM_WEIGHT_TILE)
    assert col_off % tn == 0
    off = col_off // tn
    has_res = res is not None

    def body(a_ref, b_ref, *rest):
        o_ref = rest[-1]
        acc = jnp.dot(a_ref[...].astype(BF16), b_ref[...], preferred_element_type=F32)
        if has_res:
            acc = acc + rest[0][...]
        o_ref[...] = acc.astype(out_dtype)

    in_specs = [pl.BlockSpec((tm, K), lambda j, i: (i, 0)),
                pl.BlockSpec((K, tn), lambda j, i: (0, off + j))]
    args = [a, w]
    if has_res:
        in_specs.append(pl.BlockSpec((tm, tn), lambda j, i: (i, j)))
        args.append(res)
    return _pcall(
        body, name=name, grid=(cols // tn, T // tm), in_specs=in_specs,
        out_specs=pl.BlockSpec((tm, tn), lambda j, i: (i, j)),
        out_shape=jax.ShapeDtypeStruct((T, cols), out_dtype),
        compiler_params=_params("parallel", "parallel"))(*args)


def _mm_nt(a, w, *, name, out_dtype=F32):
    T, K = a.shape
    N, K2 = w.shape
    assert K == K2
    tm = min(T, MM_TOKEN_BLOCK)
    tn = _pick(N, MM_WEIGHT_TILE)

    def body(a_ref, b_ref, o_ref):
        o_ref[...] = _dot(a_ref[...].astype(BF16), b_ref[...], 1, 1).astype(out_dtype)

    return _pcall(
        body, name=name, grid=(N // tn, T // tm),
        in_specs=[pl.BlockSpec((tm, K), lambda j, i: (i, 0)), pl.BlockSpec((tn, K), lambda j, i: (j, 0))],
        out_specs=pl.BlockSpec((tm, tn), lambda j, i: (i, j)),
        out_shape=jax.ShapeDtypeStruct((T, N), out_dtype),
        compiler_params=_params("parallel", "parallel"))(a, w)


def _mm_tn(a, b, *, name, shards=1):
    T, Ko = a.shape
    T2, N = b.shape
    assert T == T2
    n = N // shards
    tt = min(T, MM_CONTRACT_TOKENS)
    tko = _pick(Ko, 1024)
    tn = _pick(n, 1024)
    per = n // tn

    def body(a_ref, b_ref, o_ref):
        t = pl.program_id(2)
        p = _dot(a_ref[...].astype(BF16), b_ref[...].astype(BF16), 0, 0)

        @pl.when(t == 0)
        def _():
            o_ref[...] = p

        @pl.when(t > 0)
        def _():
            o_ref[...] += p

    if shards == 1:
        out_spec = pl.BlockSpec((tko, tn), lambda i, j, t: (i, j))
        out_shape = jax.ShapeDtypeStruct((Ko, N), F32)
    else:
        out_spec = pl.BlockSpec((None, tko, tn), lambda i, j, t: (j // per, i, j % per))
        out_shape = jax.ShapeDtypeStruct((shards, Ko, n), F32)
    return _pcall(
        body, name=name, grid=(Ko // tko, N // tn, T // tt),
        in_specs=[pl.BlockSpec((tt, tko), lambda i, j, t: (t, i)),
                  pl.BlockSpec((tt, tn), lambda i, j, t: (t, j))],
        out_specs=out_spec, out_shape=out_shape,
        compiler_params=_params("parallel", "parallel", "arbitrary"))(a, b)


def _rms_fwd(x, gains, *, name):
    T, D = x.shape
    tm = min(T, ROW_BLOCK)
    n = len(gains)

    def body(x_ref, *refs):
        xv = x_ref[...]
        xh = xv * lax.rsqrt(jnp.mean(xv * xv, axis=-1, keepdims=True) + EPS)
        for g_ref, o_ref in zip(refs[:n], refs[n:]):
            o_ref[...] = (xh * g_ref[...]).astype(BF16)

    row = pl.BlockSpec((tm, D), lambda i: (i, 0))
    vec = pl.BlockSpec((1, D), lambda i: (0, 0))
    return _pcall(
        body, name=name, grid=(T // tm,), in_specs=[row] + [vec] * n, out_specs=[row] * n,
        out_shape=[jax.ShapeDtypeStruct((T, D), BF16)] * n,
        compiler_params=_params("parallel"))(x, *gains)


def _rms_bwd(x, dres, dhs, gains, *, name, rider=None):
    T, D = x.shape
    tm = min(T, ROW_BLOCK)
    steps = T // tm
    n = len(gains)
    rider = rider or _Rider([], [], [], None)
    nri, nro = len(rider.inputs), len(rider.out_shapes)

    def body(x_ref, dres_ref, *refs):
        dh_refs, g_refs = refs[:n], refs[n:2 * n]
        refs = refs[2 * n:]
        rider_in, refs = refs[:nri], refs[nri:]
        dx_ref, dg_refs = refs[0], refs[1:1 + n]
        rider_out, sems = refs[1 + n:1 + n + nro], refs[1 + n + nro:]
        i = pl.program_id(0)
        if nro:
            start, finish = rider.bind(rider_in, rider_out, sems)
            pl.when(i == 0)(start)
        xv = x_ref[...]
        r = lax.rsqrt(jnp.mean(xv * xv, axis=-1, keepdims=True) + EPS)
        xh = xv * r
        dxh = jnp.zeros_like(xv)
        for dh_ref, g_ref, dg_ref in zip(dh_refs, g_refs, dg_refs):
            dh = dh_ref[...]
            part = jnp.sum(dh * xh, axis=0, keepdims=True)

            @pl.when(i == 0)
            def _():
                dg_ref[...] = part

            @pl.when(i > 0)
            def _():
                dg_ref[...] += part

            dxh = dxh + dh * g_ref[...]
        dx_ref[...] = dres_ref[...] + r * (dxh - xh * jnp.mean(dxh * xh, axis=-1, keepdims=True))
        if nro:
            pl.when(i == steps - 1)(finish)

    row = pl.BlockSpec((tm, D), lambda i: (i, 0))
    vec = pl.BlockSpec((1, D), lambda i: (0, 0))
    return _pcall(
        body, name=name, grid=(steps,), in_specs=[row, row] + [row] * n + [vec] * n + [ANY] * nri,
        out_specs=[row] + [vec] * n + [ANY] * nro,
        out_shape=[jax.ShapeDtypeStruct((T, D), F32)] + [jax.ShapeDtypeStruct((1, D), F32)] * n
                  + rider.out_shapes,
        scratch_shapes=rider.scratch,
        compiler_params=_params("arbitrary"))(x, dres, *dhs, *gains, *rider.inputs)


def _final_loss(x, target, gain, *, name):
    T, D = x.shape
    tm = min(T, ROW_BLOCK)

    def body(x_ref, t_ref, g_ref, dx_ref, dg_ref, loss_ref):
        i = pl.program_id(0)
        xv = x_ref[...]
        g = g_ref[...]
        r = lax.rsqrt(jnp.mean(xv * xv, axis=-1, keepdims=True) + EPS)
        xh = xv * r
        err = xh * g - t_ref[...]
        part_loss = 0.5 * jnp.sum(jnp.mean(err * err, axis=-1, keepdims=True), axis=0, keepdims=True)
        dy = err * (1.0 / D)
        part_g = jnp.sum(dy * xh, axis=0, keepdims=True)

        @pl.when(i == 0)
        def _():
            dg_ref[...] = part_g
            loss_ref[...] = jnp.broadcast_to(part_loss, loss_ref.shape)

        @pl.when(i > 0)
        def _():
            dg_ref[...] += part_g
            loss_ref[...] += jnp.broadcast_to(part_loss, loss_ref.shape)

        dxh = dy * g
        dx_ref[...] = r * (dxh - xh * jnp.mean(dxh * xh, axis=-1, keepdims=True))

    row = pl.BlockSpec((tm, D), lambda i: (i, 0))
    vec = pl.BlockSpec((1, D), lambda i: (0, 0))
    return _pcall(
        body, name=name, grid=(T // tm,), in_specs=[row, row, vec],
        out_specs=[row, vec, pl.BlockSpec((1, LANES), lambda i: (0, 0))],
        out_shape=[jax.ShapeDtypeStruct((T, D), F32), jax.ShapeDtypeStruct((1, D), F32),
                   jax.ShapeDtypeStruct((1, LANES), F32)],
        compiler_params=_params("arbitrary"))(x, target, gain)


def _shift_down(x, prev_tail, j, row):
    tb = x.shape[0]
    prev = jnp.tile(prev_tail, (tb // SUBLANES, 1))
    return jnp.where(row >= j, pltpu.roll(x, j, 0), pltpu.roll(prev, j, 0))


def _shift_up(x, next_head, j, row):
    tb = x.shape[0]
    nxt = jnp.tile(next_head, (tb // SUBLANES, 1))
    return jnp.where(row < tb - j, pltpu.roll(x, tb - j, 0), pltpu.roll(nxt, tb - j, 0))


def _lru_gates(xb, wr, wi, br, bi, lam):
    xbb = xb.astype(BF16)
    r = _sigmoid(jnp.dot(xbb, wr, preferred_element_type=F32) + br)
    i = _sigmoid(jnp.dot(xbb, wi, preferred_element_type=F32) + bi)
    sp = jnp.maximum(-lam, 0.0) + jnp.log1p(jnp.exp(-jnp.abs(lam)))
    log_a = (-LRU_C) * r * sp
    a = jnp.exp(log_a)
    a2 = a * a
    mult = jnp.sqrt(jnp.maximum(-jnp.tanh(log_a) * (1.0 + a2), 0.0))
    return xbb, r, i, sp, a, a2, mult


def _acore_fwd(proj, conv_w, conv_b, w_r, w_i, b_r, b_i, lam, *, name, riders=()):
    T, C2 = proj.shape
    C = C2 // 2
    nb, bw, _ = w_r.shape
    tb = min(T, SCAN_BLOCK)
    nt = T // tb
    nr = len(riders)

    def body(xp_ref, gate_ref, cw_ref, cb_ref, wr_ref, wi_ref, br_ref, bi_ref, lam_ref, *refs):
        rider_in, refs = refs[:nr], refs[nr:]
        xb_ref, h_ref, yg_ref = refs[:3]
        rider_out, refs = refs[3:3 + nr], refs[3 + nr:]
        tail_ref, hlast_ref = refs[:2]
        t = pl.program_id(1)
        if nr:
            gather = _Gather(rider_in, rider_out, *refs[2:])
            pl.when((pl.program_id(0) == 0) & (t == 0))(gather.start)

        @pl.when(t == 0)
        def _():
            tail_ref[...] = jnp.zeros_like(tail_ref)
            hlast_ref[...] = jnp.zeros_like(hlast_ref)

        row = lax.broadcasted_iota(jnp.int32, (tb, bw), 0)
        xp = xp_ref[...]
        tail = tail_ref[...]
        xb = cb_ref[...] + cw_ref[CONV_W - 1:CONV_W, :] * xp
        for j in range(1, CONV_W):
            xb = xb + cw_ref[CONV_W - 1 - j:CONV_W - j, :] * _shift_down(xp, tail, j, row)
        tail_ref[...] = xp[tb - SUBLANES:, :]
        xb_ref[...] = xb

        _, r, i, sp, a, a2, mult = _lru_gates(xb, wr_ref[...], wi_ref[...], br_ref[...], bi_ref[...],
                                              lam_ref[...])
        ca, cb = a, mult * (i * xb)
        s = 1
        while s < tb:
            m = row >= s
            cb = jnp.where(m, ca * pltpu.roll(cb, s, 0) + cb, cb)
            ca = jnp.where(m, ca * pltpu.roll(ca, s, 0), ca)
            s *= 2
        h = cb + ca * hlast_ref[SUBLANES - 1:SUBLANES, :]
        hlast_ref[...] = h[tb - SUBLANES:, :]
        h_ref[...] = h
        gate = gate_ref[...]
        yg_ref[...] = (h * (gate * _sigmoid(gate))).astype(BF16)
        if nr:
            pl.when((pl.program_id(0) == nb - 1) & (t == nt - 1))(gather.finish)

    blk = lambda off: pl.BlockSpec((tb, bw), lambda n, t: (t, off + n))
    vec = pl.BlockSpec((1, bw), lambda n, t: (0, n))
    wspec = pl.BlockSpec((None, bw, bw), lambda n, t: (n, 0, 0))
    return _pcall(
        body, name=name, grid=(nb, nt),
        in_specs=[blk(0), blk(nb), pl.BlockSpec((CONV_W, bw), lambda n, t: (0, n)), vec, wspec, wspec,
                  vec, vec, vec] + [ANY] * nr,
        out_specs=[blk(0), blk(0), blk(0)] + [ANY] * nr,
        out_shape=[jax.ShapeDtypeStruct((T, C), F32), jax.ShapeDtypeStruct((T, C), F32),
                   jax.ShapeDtypeStruct((T, C), BF16)]
                  + [jax.ShapeDtypeStruct((N_DEV,) + r.shape, r.dtype) for r in riders],
        scratch_shapes=[pltpu.VMEM((SUBLANES, bw), F32), pltpu.VMEM((SUBLANES, bw), F32)]
                       + (_Gather.scratch(nr) if nr else []),
        compiler_params=_params("arbitrary" if nr else "parallel", "arbitrary"))(
            proj, proj, conv_w, conv_b, w_r, w_i, b_r, b_i, lam, *riders)


def _acore_bwd(dyg, proj, xb_all, h_all, conv_w, w_r, w_i, b_r, b_i, lam, *, name, rider=None):
    T, C2 = proj.shape
    C = C2 // 2
    nb, bw, _ = w_r.shape
    tb = min(T, SCAN_BLOCK)
    nt = T // tb
    per8 = tb // SUBLANES
    rider = rider or _Rider([], [], [], None)
    nri, nro = len(rider.inputs), len(rider.out_shapes)

    def body(dyg_ref, xp_ref, gate_ref, xb_ref, h_ref, xp_prev_ref, h_prev_ref, cw_ref,
             wr_ref, wi_ref, br_ref, bi_ref, lam_ref, *refs):
        rider_in, refs = refs[:nri], refs[nri:]
        dxp_ref, dgate_ref, dcw_ref, dcb_ref, dbr_ref, dbi_ref, dlam_ref, dwr_ref, dwi_ref = refs[:9]
        rider_out, refs = refs[9:9 + nro], refs[9 + nro:]
        gh_next_ref, a_next_ref, dxb_next_ref = refs[:3]
        step = pl.program_id(1)
        first_block = step == nt - 1
        if nro:
            start, finish = rider.bind(rider_in, rider_out, refs[3:])
            pl.when((pl.program_id(0) == 0) & (step == 0))(start)

        @pl.when(step == 0)
        def _():
            gh_next_ref[...] = jnp.zeros_like(gh_next_ref)
            a_next_ref[...] = jnp.zeros_like(a_next_ref)
            dxb_next_ref[...] = jnp.zeros_like(dxb_next_ref)

        row = lax.broadcasted_iota(jnp.int32, (tb, bw), 0)
        keep = jnp.where(first_block, 0.0, 1.0)
        h_prev = h_prev_ref[...] * keep
        xp_prev = xp_prev_ref[...] * keep
        xp, gate, xb, h, dyg_v = xp_ref[...], gate_ref[...], xb_ref[...], h_ref[...], dyg_ref[...]
        lam_v = lam_ref[...]
        wr, wi = wr_ref[...], wi_ref[...]

        sg = _sigmoid(gate)
        dh = dyg_v * (gate * sg)
        dgate_ref[...] = (dyg_v * h * (sg * (1.0 + gate * (1.0 - sg)))).astype(BF16)

        xbb, r, i, sp, a, a2, mult = _lru_gates(xb, wr, wi, br_ref[...], bi_ref[...], lam_v)

        cg = dh
        cc = _shift_up(a, a_next_ref[...], 1, row)
        s = 1
        while s < tb:
            m = row < tb - s
            cg = jnp.where(m, cc * pltpu.roll(cg, tb - s, 0) + cg, cg)
            cc = jnp.where(m, cc * pltpu.roll(cc, tb - s, 0), cc)
            s *= 2
        gh = cg + cc * gh_next_ref[0:1, :]
        gh_next_ref[...] = gh[0:SUBLANES, :]
        a_next_ref[...] = a[0:SUBLANES, :]

        da = gh * _shift_down(h, h_prev, 1, row)
        dmult = gh * (i * xb)
        di = gh * mult * xb
        dxb = gh * mult * i
        dla = da * a - dmult * jnp.where(mult > 0.0, a2 / mult, 0.0)
        dr = dla * ((-LRU_C) * sp)
        dsp = jnp.sum(dla * ((-LRU_C) * r), axis=0, keepdims=True)
        dlam_part = dsp * (-_sigmoid(-lam_v))
        dpr = dr * r * (1.0 - r)
        dpi = di * i * (1.0 - i)
        dbr_part = jnp.sum(dpr, axis=0, keepdims=True)
        dbi_part = jnp.sum(dpi, axis=0, keepdims=True)
        dprb, dpib = dpr.astype(BF16), dpi.astype(BF16)
        dwr_part = _dot(xbb, dprb, 0, 0)
        dwi_part = _dot(xbb, dpib, 0, 0)
        dxb = dxb + _dot(dprb, wr, 1, 1) + _dot(dpib, wi, 1, 1)

        dxb_next = dxb_next_ref[...]
        dxp = cw_ref[CONV_W - 1:CONV_W, :] * dxb
        for j in range(1, CONV_W):
            dxp = dxp + cw_ref[CONV_W - 1 - j:CONV_W - j, :] * _shift_up(dxb, dxb_next, j, row)
        dxb_next_ref[...] = dxb[0:SUBLANES, :]
        dxp_ref[...] = dxp.astype(BF16)
        dcb_part = jnp.sum(dxb, axis=0, keepdims=True)
        dcw_rows = []
        for k in range(CONV_W):
            j = CONV_W - 1 - k
            sh = xp if j == 0 else _shift_down(xp, xp_prev, j, row)
            dcw_rows.append(jnp.sum(dxb * sh, axis=0, keepdims=True))

        @pl.when(step == 0)
        def _():
            for k in range(CONV_W):
                dcw_ref[k:k + 1, :] = dcw_rows[k]
            dcb_ref[...] = dcb_part
            dbr_ref[...] = dbr_part
            dbi_ref[...] = dbi_part
            dlam_ref[...] = dlam_part
            dwr_ref[...] = dwr_part
            dwi_ref[...] = dwi_part

        @pl.when(step > 0)
        def _():
            for k in range(CONV_W):
                dcw_ref[k:k + 1, :] += dcw_rows[k]
            dcb_ref[...] += dcb_part
            dbr_ref[...] += dbr_part
            dbi_ref[...] += dbi_part
            dlam_ref[...] += dlam_part
            dwr_ref[...] += dwr_part
            dwi_ref[...] += dwi_part

        if nro:
            pl.when((pl.program_id(0) == nb - 1) & (step == nt - 1))(finish)

    rev = lambda s: nt - 1 - s
    blk = lambda off: pl.BlockSpec((tb, bw), lambda n, s: (rev(s), off + n))
    prev8 = lambda off: pl.BlockSpec(
        (SUBLANES, bw), lambda n, s: (jnp.maximum(rev(s) * per8 - 1, 0), off + n))
    vec = pl.BlockSpec((1, bw), lambda n, s: (0, n))
    wspec = pl.BlockSpec((None, bw, bw), lambda n, s: (n, 0, 0))
    cwspec = pl.BlockSpec((CONV_W, bw), lambda n, s: (0, n))
    vshape = jax.ShapeDtypeStruct((1, C), F32)
    wshape = jax.ShapeDtypeStruct((nb, bw, bw), F32)
    return _pcall(
        body, name=name, grid=(nb, nt),
        in_specs=[blk(0), blk(0), blk(nb), blk(0), blk(0), prev8(0), prev8(0), cwspec,
                  wspec, wspec, vec, vec, vec] + [ANY] * nri,
        out_specs=[blk(0), blk(0), cwspec, vec, vec, vec, vec, wspec, wspec] + [ANY] * nro,
        out_shape=[jax.ShapeDtypeStruct((T, C), BF16), jax.ShapeDtypeStruct((T, C), BF16),
                   jax.ShapeDtypeStruct((CONV_W, C), F32), vshape, vshape, vshape, vshape,
                   wshape, wshape] + rider.out_shapes,
        scratch_shapes=[pltpu.VMEM((SUBLANES, bw), F32)] * 3 + rider.scratch,
        compiler_params=_params("arbitrary" if nro else "parallel", "arbitrary"))(
            dyg, proj, proj, xb_all, h_all, proj, h_all, conv_w, w_r, w_i, b_r, b_i, lam, *rider.inputs)


def _later_sum(lk, tri):
    return jnp.dot(lk.astype(BF16), tri, preferred_element_type=F32)


def _log2_sigmoids(y):
    t = jnp.log(1.0 + jnp.exp2(-jnp.abs(y))) * LOG2E
    ls = jnp.minimum(y, 0.0) - t
    return ls, ls - y


def _attn_blocks(T):
    bk = min(T, ATT_KEY_BLOCK)
    bq = min(T, ATT_QUERY_BLOCK)
    return bq, bk, bq // bk


def _attn_fwd(q, kv, gate, *, name):
    T, HD = q.shape
    H = HD // HEAD_DIM
    bq, bk, per = _attn_blocks(T)
    scale = 1.0 / math.sqrt(HEAD_DIM)

    def body(q_ref, k_ref, v_ref, g_ref, o_ref, og_ref, lt_ref, w_ref, z_ref):
        i = pl.program_id(1)
        qv = q_ref[...]
        tr = lax.broadcasted_iota(jnp.int32, (bk, bk), 0)
        tc = lax.broadcasted_iota(jnp.int32, (bk, bk), 1)
        tri = (tr > tc).astype(BF16)
        ahead = (lax.broadcasted_iota(jnp.int32, (bq, bk), 0)
                 - lax.broadcasted_iota(jnp.int32, (bq, bk), 1))

        def starts_of(top):
            return [pl.multiple_of((top - d) * bk, bk) for d in range(per)]

        def scores(top):
            return [_dot(qv, k_ref[pl.ds(ks, bk), :], 1, 1) for ks in starts_of(top)]

        def weights(top, zs, c, mask):
            lss, sums, css, causals = [], [], [], []
            for ks, z in zip(starts_of(top), zs):
                ls, lk = _log2_sigmoids(z * (scale * LOG2E))
                if mask:
                    causals.append(ahead > ks - i * bq)
                    lk = jnp.where(causals[-1], lk, 0.0)
                lss.append(ls)
                sums.append(jnp.sum(lk, axis=1, keepdims=True))
                css.append(_later_sum(lk, tri))
            for d in range(per):
                w = jnp.exp2(lss[d] + (css[d] + c))
                if mask:
                    w = jnp.where(causals[d], w, 0.0)
                w_ref[d] = w.astype(BF16)
                c = c + sums[d]
            return c

        def values(top, acc):
            for d, ks in enumerate(starts_of(top)):
                acc = acc + jnp.dot(w_ref[d], v_ref[pl.ds(ks, bk), :], preferred_element_type=F32)
            return acc

        def step(gg, state):
            acc, c = state
            top = (i - gg) * per + per - 1
            zs = [z_ref[d] for d in range(per)]
            next_zs = scores(jnp.maximum(top - per, per - 1))
            acc = values(top + per, acc)
            c = weights(top, zs, c, False)
            for d in range(per):
                z_ref[d] = next_zs[d]
            return acc, c

        diag_top = i * per + per - 1
        diag_zs = scores(diag_top)
        next_zs = scores(jnp.maximum(diag_top - per, per - 1))
        c = weights(diag_top, diag_zs, jnp.zeros((bq, 1), F32), True)
        for d in range(per):
            z_ref[d] = next_zs[d]
        acc, c = lax.fori_loop(1, i + 1, step, (jnp.zeros((bq, HEAD_DIM), F32), c))
        acc = values(per - 1, acc)
        o_ref[...] = acc
        g = g_ref[...]
        og_ref[...] = (acc * (g * _sigmoid(g))).astype(BF16)
        lt_ref[...] = jnp.broadcast_to(c, (bq, HEAD_DIM))

    qspec = pl.BlockSpec((bq, HEAD_DIM), lambda h, i: (i, h))
    return _pcall(
        body, name=name, grid=(H, T // bq),
        in_specs=[qspec, pl.BlockSpec((T, HEAD_DIM), lambda h, i: (0, h)),
                  pl.BlockSpec((T, HEAD_DIM), lambda h, i: (0, H + h)), qspec],
        out_specs=[qspec, qspec, qspec],
        out_shape=[jax.ShapeDtypeStruct((T, HD), F32), jax.ShapeDtypeStruct((T, HD), BF16),
                   jax.ShapeDtypeStruct((T, HD), F32)],
        scratch_shapes=[pltpu.VMEM((per, bq, bk), BF16), pltpu.VMEM((per, bq, bk), F32)],
        compiler_params=_params("parallel", "arbitrary"))(q, kv, kv, gate)


def _attn_bwd(q, kv, gate, o, ltot, dog, *, name):
    T, HD = q.shape
    H = HD // HEAD_DIM
    bq, bk, per = _attn_blocks(T)
    nq = T // bq
    scale = 1.0 / math.sqrt(HEAD_DIM)

    def body(q_ref, k_ref, v_ref, g_ref, o_ref, lt_ref, dog_ref,
             dq_ref, dg_ref, dk_ref, dv_ref, dk_acc, dv_acc, dz_ref, w_ref):
        i = pl.program_id(1)

        @pl.when(i == 0)
        def _():
            dk_acc[...] = jnp.zeros_like(dk_acc)
            dv_acc[...] = jnp.zeros_like(dv_acc)

        qv = q_ref[...]
        g, ov, dogv = g_ref[...], o_ref[...], dog_ref[...]
        sg = _sigmoid(g)
        do = dogv * (g * sg)
        dg_ref[...] = (dogv * ov * (sg * (1.0 + g * (1.0 - sg)))).astype(BF16)
        dob = do.astype(BF16)
        q_t = qv.astype(F32).T.astype(BF16)
        do_t = do.T.astype(BF16)
        ltot_v = lt_ref[:, 0:1]
        tr = lax.broadcasted_iota(jnp.int32, (bk, bk), 0)
        tc = lax.broadcasted_iota(jnp.int32, (bk, bk), 1)
        tri_later = (tr > tc).astype(BF16)
        tri_excl = (tr < tc).astype(BF16)
        ahead = (lax.broadcasted_iota(jnp.int32, (bq, bk), 0)
                 - lax.broadcasted_iota(jnp.int32, (bq, bk), 1))

        def starts_of(first):
            return [pl.multiple_of((first + d) * bk, bk) for d in range(per)]

        def scores(first):
            return ([_dot(qv, k_ref[pl.ds(ks, bk), :], 1, 1) for ks in starts_of(first)],
                    [_dot(dob, v_ref[pl.ds(ks, bk), :], 1, 1) for ks in starts_of(first)])

        def front(first, zs, dws, p_lk, p_g, mask):
            lss, css, causals = [], [], []
            for ks, z in zip(starts_of(first), zs):
                ls, lk = _log2_sigmoids(z * (scale * LOG2E))
                if mask:
                    causals.append(ahead > ks - i * bq)
                    lk = jnp.where(causals[-1], lk, 0.0)
                lss.append(ls)
                p_lk = p_lk + jnp.sum(lk, axis=1, keepdims=True)
                css.append((ltot_v - p_lk) + _later_sum(lk, tri_later))
            gms, befores = [], []
            for d in range(per):
                w = jnp.exp2(lss[d] + css[d])
                if mask:
                    w = jnp.where(causals[d], w, 0.0)
                gm = dws[d] * w
                gms.append(gm)
                w_ref[d] = w.astype(BF16)
                befores.append(jnp.dot(gm.astype(BF16), tri_excl, preferred_element_type=F32) + p_g)
                p_g = p_g + jnp.sum(gm, axis=1, keepdims=True)
            for d in range(per):
                dz = gms[d] - jnp.exp2(lss[d]) * (gms[d] + befores[d])
                if mask:
                    dz = jnp.where(causals[d], dz, 0.0)
                dz_ref[d] = (dz * scale).astype(BF16)
            return p_lk, p_g

        def back(first, dq):
            for d, ks in enumerate(starts_of(first)):
                dzb = dz_ref[d]
                dq = dq + jnp.dot(dzb, k_ref[pl.ds(ks, bk), :], preferred_element_type=F32)
                dk_acc[first + d] += jnp.dot(q_t, dzb, preferred_element_type=F32)
                dv_acc[first + d] += jnp.dot(do_t, w_ref[d], preferred_element_type=F32)
            return dq

        def step(mask):
            def trip(g, state):
                dq, p_lk, p_g = state
                zs, dws = scores(g * per)
                dq = back((g - 1) * per, dq)
                return (dq,) + front(g * per, zs, dws, p_lk, p_g, mask)
            return trip

        zero = jnp.zeros((bq, 1), F32)
        state = (jnp.zeros((bq, HEAD_DIM), F32),) + front(0, *scores(0), zero, zero, True)
        state = lax.fori_loop(1, i, step(False), state)
        state = lax.fori_loop(jnp.maximum(i, 1), i + 1, step(True), state)
        dq_ref[...] = back(i * per, state[0]).astype(BF16)

        @pl.when(i == nq - 1)
        def _():
            for j in range(T // bk):
                dk_ref[j * bk:(j + 1) * bk, :] = dk_acc[j].T.astype(BF16)
                dv_ref[j * bk:(j + 1) * bk, :] = dv_acc[j].T.astype(BF16)

    qspec = pl.BlockSpec((bq, HEAD_DIM), lambda h, i: (i, h))
    kspec = pl.BlockSpec((T, HEAD_DIM), lambda h, i: (0, h))
    return _pcall(
        body, name=name, grid=(H, nq),
        in_specs=[qspec, kspec, pl.BlockSpec((T, HEAD_DIM), lambda h, i: (0, H + h)),
                  qspec, qspec, qspec, qspec],
        out_specs=[qspec, qspec, kspec, kspec],
        out_shape=[jax.ShapeDtypeStruct((T, HD), BF16)] * 4,
        scratch_shapes=[pltpu.VMEM((T // bk, HEAD_DIM, bk), F32)] * 2 + [pltpu.VMEM((per, bq, bk), BF16)] * 2,
        compiler_params=_params("parallel", "arbitrary"))(q, kv, kv, gate, o, ltot, dog)


def _position():
    return lax.axis_index("x"), lax.axis_index("y"), lax.axis_index("c")


def _chip_of(k, x, y):
    return (1 - x if k & 1 else x), (1 - y if k & 2 else y)


class _Gather:
    @staticmethod
    def scratch(n):
        return [pltpu.SemaphoreType.DMA((n, 7)), pltpu.SemaphoreType.DMA((n, 7)),
                pltpu.SemaphoreType.DMA((n,))]

    def __init__(self, ins, outs, send_sems, recv_sems, local_sems):
        self.ins, self.outs, self.n = ins, outs, len(ins)
        self.send_sems, self.recv_sems, self.local_sems = send_sems, recv_sems, local_sems
        x, y, c = _position()
        self.me, self.sibling = (x, y, c), (x, y, 1 - c)
        self.chips = [_chip_of(k, x, y) for k in (1, 2, 3)]

    def copy(self, a, k, block, to, src=None):
        slot = self.outs[a].at[4 * block[0] + 2 * block[1] + block[2]]
        return pltpu.make_async_remote_copy(
            src_ref=slot if src is None else src, dst_ref=slot,
            send_sem=self.send_sems.at[a, k], recv_sem=self.recv_sems.at[a, k],
            device_id=to, device_id_type=MESH)

    def own_copies(self):
        x, y, c = self.me
        mine = [pltpu.make_async_copy(self.ins[a], self.outs[a].at[4 * x + 2 * y + c], self.local_sems.at[a])
                for a in range(self.n)]
        first = []
        for a in range(self.n):
            first.append(self.copy(a, 0, self.me, self.sibling, src=self.ins[a]))
            first += [self.copy(a, 1 + j, self.me, (*chip, c), src=self.ins[a])
                      for j, chip in enumerate(self.chips)]
        return mine, first

    def start(self):
        mine, first = self.own_copies()
        for cp in mine + first:
            cp.start()

    def finish(self):
        c = self.me[2]
        mine, first = self.own_copies()
        passed = []
        for j, chip in enumerate(self.chips):
            for a in range(self.n):
                self.copy(a, 1 + j, (*chip, c), self.me).wait_recv()
                fwd = self.copy(a, 4 + j, (*chip, c), self.sibling)
                fwd.start()
                passed.append(fwd)
        for a in range(self.n):
            self.copy(a, 0, self.sibling, self.me).wait_recv()
            for j, chip in enumerate(self.chips):
                self.copy(a, 4 + j, (*chip, 1 - c), self.me).wait_recv()
        for cp in first + passed:
            cp.wait_send()
        for cp in mine:
            cp.wait()


def _weights_gather(shards):
    n = len(shards)

    def body(*refs):
        gather = _Gather(refs[:n], refs[n:2 * n], *refs[2 * n:])
        gather.start()
        gather.finish()

    return _pcall(
        body, name="weights_gather", in_specs=[ANY] * n, out_specs=[ANY] * n,
        out_shape=[jax.ShapeDtypeStruct((N_DEV,) + s.shape, s.dtype) for s in shards],
        scratch_shapes=_Gather.scratch(n))(*shards)


class _Rider:
    def __init__(self, inputs, out_shapes, scratch, copies):
        self.inputs, self.out_shapes, self.scratch, self.copies = inputs, out_shapes, scratch, copies

    def bind(self, ins, outs, sems):
        def start():
            for cp in self.copies(ins, outs, sems):
                cp.start()

        def finish():
            cps = self.copies(ins, outs, sems)
            for cp in cps:
                cp.wait_send()
            for cp in cps:
                cp.wait_recv()

        return start, finish


def _sibling_rider(grads):
    n = len(grads)

    def copies(ins, outs, sems):
        x, y, c = _position()
        return [pltpu.make_async_remote_copy(
            src_ref=ins[a].at[2 * chip + (1 - c)], dst_ref=outs[a].at[chip],
            send_sem=sems[0].at[a, chip], recv_sem=sems[1].at[a, chip],
            device_id=(x, y, 1 - c), device_id_type=MESH) for a in range(n) for chip in range(4)]

    return _Rider(list(grads), [jax.ShapeDtypeStruct((4,) + g.shape[1:], g.dtype) for g in grads],
                  [pltpu.SemaphoreType.DMA((n, 4)), pltpu.SemaphoreType.DMA((n, 4))], copies)


def _chips_rider(parts):
    n = len(parts)

    def copies(ins, outs, sems):
        x, y, c = _position()
        cps = []
        for a in range(n):
            for k in range(3):
                cx, cy = _chip_of(k + 1, x, y)
                cps.append(pltpu.make_async_remote_copy(
                    src_ref=ins[a].at[2 * cx + cy], dst_ref=outs[a].at[k],
                    send_sem=sems[0].at[a, k], recv_sem=sems[1].at[a, k],
                    device_id=(cx, cy, c), device_id_type=MESH))
        return cps

    return _Rider(list(parts), [jax.ShapeDtypeStruct((3,) + p.shape[1:], p.dtype) for p in parts],
                  [pltpu.SemaphoreType.DMA((n, 3)), pltpu.SemaphoreType.DMA((n, 3))], copies)


def _grads_to_sibling(grads, small):
    n = len(grads)
    rider = _sibling_rider(grads)

    def body(*refs):
        g_refs, small_ref = refs[:n], refs[n]
        got, small_all = refs[n + 1:2 * n + 1], refs[2 * n + 1]
        send_sems, recv_sems, small_send, small_recv, small_local = refs[2 * n + 2:]
        x, y, c = _position()
        me = 4 * x + 2 * y + c
        remote = rider.copies(g_refs, got, (send_sems, recv_sems))
        peers = []
        for m in range(1, N_DEV):
            px, py, pc = x ^ (m >> 2), y ^ ((m >> 1) & 1), c ^ (m & 1)
            peers.append(pltpu.make_async_remote_copy(
                src_ref=small_ref, dst_ref=small_all.at[me],
                send_sem=small_send.at[m - 1], recv_sem=small_recv.at[m - 1],
                device_id=(px, py, pc), device_id_type=MESH))
        own = pltpu.make_async_copy(small_ref, small_all.at[me], small_local)
        for cp in peers + remote + [own]:
            cp.start()
        for cp in peers + remote:
            cp.wait_send()
        for m in range(1, N_DEV):
            px, py, pc = x ^ (m >> 2), y ^ ((m >> 1) & 1), c ^ (m & 1)
            pltpu.make_async_remote_copy(
                src_ref=small_ref, dst_ref=small_all.at[4 * px + 2 * py + pc],
                send_sem=small_send.at[m - 1], recv_sem=small_recv.at[m - 1],
                device_id=(px, py, pc), device_id_type=MESH).wait_recv()
        for cp in remote:
            cp.wait_recv()
        own.wait()

    part = [jax.ShapeDtypeStruct((4,) + g.shape[1:], g.dtype) for g in grads]
    return _pcall(
        body, name="grads_to_sibling", in_specs=[ANY] * (n + 1), out_specs=[ANY] * (n + 1),
        out_shape=part + [jax.ShapeDtypeStruct((N_DEV,) + small.shape, small.dtype)],
        scratch_shapes=[pltpu.SemaphoreType.DMA((n, 4)), pltpu.SemaphoreType.DMA((n, 4)),
                        pltpu.SemaphoreType.DMA((7,)), pltpu.SemaphoreType.DMA((7,)),
                        pltpu.SemaphoreType.DMA])(*grads, small)


def _grads_to_chips(parts):
    n = len(parts)
    rider = _chips_rider(parts)

    def body(*refs):
        start, finish = rider.bind(refs[:n], refs[n:2 * n], refs[2 * n:])
        start()
        finish()

    return _pcall(
        body, name="grads_to_chips", in_specs=[ANY] * n, out_specs=[ANY] * n,
        out_shape=rider.out_shapes, scratch_shapes=rider.scratch)(*parts)


def _pair_sum(grad, got, *, name):
    _, R, C = got.shape
    tr = _pick8(R, max(2 * SUBLANES, (1 << 17) // C))

    def body(g_ref, b_ref, o_ref, ob_ref):
        north = lax.axis_index("c") == 1
        for chip in range(4):
            s = jnp.where(north, g_ref[chip, 1], g_ref[chip, 0]) + b_ref[chip]
            o_ref[chip] = s
            ob_ref[chip] = s.astype(BF16)

    spec = pl.BlockSpec((4, tr, C), lambda i: (0, i, 0))
    return _pcall(
        body, name=name, grid=(R // tr,),
        in_specs=[pl.BlockSpec((4, 2, tr, C), lambda i: (0, 0, i, 0)), spec],
        out_specs=[spec, spec],
        out_shape=[jax.ShapeDtypeStruct((4, R, C), F32), jax.ShapeDtypeStruct((4, R, C), BF16)],
        compiler_params=_params("parallel"))(grad.reshape(4, 2, R, C), got)


def _pick8(n, cap):
    if n <= cap:
        return n
    best = None
    for t in range(SUBLANES, cap + 1, SUBLANES):
        if n % t == 0:
            best = t
    assert best is not None, (n, cap)
    return best


def _adamw(w, m, v, parts, *, name, chip_sums=None):
    R, C = w.shape
    tr = _pick8(R, max(SUBLANES, (1 << 17) // C))
    c1 = 1.0 - ADAM_B1 ** ADAM_STEP
    c2 = 1.0 - ADAM_B2 ** ADAM_STEP
    parts = list(parts) if chip_sums is None else [chip_sums] + list(parts)
    np_ = len(parts)

    def body(w_ref, m_ref, v_ref, *refs):
        p_refs = refs[:np_]
        g_ref, d_ref, nm_ref, nv_ref = refs[np_:]
        g = None
        if chip_sums is not None:
            s_ref, p_refs = p_refs[0], p_refs[1:]
            x1, y1 = lax.axis_index("x") == 1, lax.axis_index("y") == 1
            g = jnp.where(x1, jnp.where(y1, s_ref[3], s_ref[2]), jnp.where(y1, s_ref[1], s_ref[0]))
        for p_ref in p_refs:
            for t in [p_ref[k].astype(F32) for k in range(p_ref.shape[0])]:
                g = t if g is None else g + t
        mn = ADAM_B1 * m_ref[...] + (1.0 - ADAM_B1) * g
        vn = ADAM_B2 * v_ref[...] + (1.0 - ADAM_B2) * (g * g)
        d_ref[...] = -ADAM_LR * ((mn / c1) / (jnp.sqrt(vn / c2) + ADAM_EPS) + ADAM_WD * w_ref[...])
        g_ref[...] = g
        nm_ref[...] = mn
        nv_ref[...] = vn

    spec = pl.BlockSpec((tr, C), lambda i: (i, 0))
    pspecs = [pl.BlockSpec((p.shape[0], tr, C), lambda i: (0, i, 0)) for p in parts]
    return _pcall(
        body, name=name, grid=(R // tr,), in_specs=[spec] * 3 + pspecs, out_specs=[spec] * 4,
        out_shape=[jax.ShapeDtypeStruct((R, C), F32)] * 4,
        compiler_params=_params("parallel"))(w, m, v, *parts)


def _rows(a):
    return a.reshape(-1, LANES)


def _whole_from_columns(shards, *, name):
    S, K, n = shards.shape
    tk = _pick8(K, 1024)

    def body(s_ref, o_ref):
        o_ref[...] = s_ref[...]

    return _pcall(
        body, name=name, grid=(K // tk, S),
        in_specs=[pl.BlockSpec((None, tk, n), lambda i, s: (s, i, 0))],
        out_specs=pl.BlockSpec((tk, n), lambda i, s: (i, s)),
        out_shape=jax.ShapeDtypeStruct((K, S * n), shards.dtype),
        compiler_params=_params("parallel", "parallel"))(shards)


def _late_weights(a_w_out_rows, w_kv_cols, b_w_in_cols, b_w_out_rows):
    whole_rows = lambda g: g.reshape(g.shape[0] * g.shape[1], g.shape[2])
    return (whole_rows(a_w_out_rows), _whole_from_columns(w_kv_cols, name="w_kv_whole"),
            _whole_from_columns(b_w_in_cols, name="b_w_in_whole"), whole_rows(b_w_out_rows))


def _forward_backward(xs, target, a_norm, g_a_w_in, conv_w, conv_b, g_w_r, g_w_i, b_r, b_i, lam,
                      kv_norm, b_norm, final_norm, *, late_weights=None, late_shards=None):
    (h_a,) = _rms_fwd(xs, [a_norm], name="a_norm_fwd")
    proj_a = _mm_nn(h_a, g_a_w_in, name="a_in_proj", out_dtype=F32)
    xb, h_rec, yg, *gathered = _acore_fwd(proj_a, conv_w, conv_b, g_w_r, g_w_i, b_r, b_i, lam,
                                          name="a_core_fwd", riders=late_shards or ())
    g_a_w_out, g_w_kv, g_b_w_in, g_b_w_out = _late_weights(*gathered) if late_shards else late_weights
    x1 = _mm_nn(yg, g_a_w_out, name="a_out_proj", out_dtype=F32, res=xs)
    hk, hb = _rms_fwd(x1, [kv_norm, b_norm], name="kv_b_norm_fwd")
    kv = _mm_nn(hk, g_w_kv, name="kv_proj", out_dtype=BF16)
    hd = g_b_w_in.shape[1] // 2
    q = _mm_nn(hb, g_b_w_in, name="q_proj", out_dtype=BF16, col_off=0, cols=hd)
    gate_b = _mm_nn(hb, g_b_w_in, name="b_gate_proj", out_dtype=F32, col_off=hd, cols=hd)
    o, og, ltot = _attn_fwd(q, kv, gate_b, name="attn_fwd")
    x2 = _mm_nn(og, g_b_w_out, name="b_out_proj", out_dtype=F32, res=x1)
    dx2, d_final_norm, loss_part = _final_loss(x2, target, final_norm, name="final_norm_loss")

    dog = _mm_nt(dx2, g_b_w_out, name="b_out_proj_bwd")
    dw_b_out = _mm_tn(og, dx2, name="b_out_proj_wgrad")
    dq, dgate_b, dk, dv = _attn_bwd(q, kv, gate_b, o, ltot, dog, name="attn_bwd")
    dproj_b = jnp.concatenate([dq, dgate_b], axis=1)
    dkv = jnp.concatenate([dk, dv], axis=1)
    dhb = _mm_nt(dproj_b, g_b_w_in, name="b_in_proj_bwd")
    dw_b_in = _mm_tn(hb, dproj_b, name="b_in_proj_wgrad", shards=N_DEV)
    dhk = _mm_nt(dkv, g_w_kv, name="kv_proj_bwd")
    dw_kv = _mm_tn(hk, dkv, name="kv_proj_wgrad", shards=N_DEV)
    early = [dw_kv, dw_b_in, dw_b_out.reshape(N_DEV, -1, dw_b_out.shape[1])] if late_shards else []
    dx1, d_b_norm, d_kv_norm, *got = _rms_bwd(x1, dx2, [dhb, dhk], [b_norm, kv_norm], name="kv_b_norm_bwd",
                                              rider=_sibling_rider(early) if early else None)
    early_sums = [_pair_sum(f_, g_, name=f"pair_sum_early_{i}") for i, (f_, g_) in enumerate(zip(early, got))]
    dyg = _mm_nt(dx1, g_a_w_out, name="a_out_proj_bwd")
    dw_a_out = _mm_tn(yg, dx1, name="a_out_proj_wgrad")
    (dxp, dgate_a, d_conv_w, d_conv_b, d_b_r, d_b_i, d_lambda, dw_r, dw_i, *early_others) = _acore_bwd(
        dyg, proj_a, xb, h_rec, conv_w, g_w_r, g_w_i, b_r, b_i, lam, name="a_core_bwd",
        rider=_chips_rider([s[1] for s in early_sums]) if early else None)
    dproj_a = jnp.concatenate([dxp, dgate_a], axis=1)
    dh_a = _mm_nt(dproj_a, g_a_w_in, name="a_in_proj_bwd")
    dw_a_in = _mm_tn(h_a, dproj_a, name="a_in_proj_wgrad", shards=N_DEV)
    grad_x, d_a_norm = _rms_bwd(xs, dx1, [dh_a], [a_norm], name="a_norm_bwd")
    return (loss_part, grad_x, dw_a_in, dw_a_out, dw_kv, dw_b_in, dw_b_out, dw_r, dw_i, d_a_norm,
            d_conv_w, d_conv_b, d_b_r, d_b_i, d_lambda, d_kv_norm, d_b_norm, d_final_norm,
            early_sums, early_others)


def kernel(x, a_norm, a_w_in, a_conv_w, a_conv_b, a_w_r, a_b_r, a_w_i, a_b_i, a_lambda, a_w_out, kv_norm, w_kv, b_norm, b_w_in, b_w_out, final_norm, loss_target, m_a_norm, m_a_w_in, m_a_conv_w, m_a_conv_b, m_a_w_r, m_a_b_r, m_a_w_i, m_a_b_i, m_a_lambda, m_a_w_out, m_kv_norm, m_w_kv, m_b_norm, m_b_w_in, m_b_w_out, m_final_norm, v_a_norm, v_a_w_in, v_a_conv_w, v_a_conv_b, v_a_w_r, v_a_b_r, v_a_w_i, v_a_b_i, v_a_lambda, v_a_w_out, v_kv_norm, v_w_kv, v_b_norm, v_b_w_in, v_b_w_out, v_final_norm):
    T, D = x.shape[1], x.shape[2]
    nb, bw = a_w_r.shape[1], a_w_r.shape[3]
    C = nb * bw
    me = 4 * lax.axis_index("x") + 2 * lax.axis_index("y") + lax.axis_index("c")
    xs = x[0]
    target = loss_target[0]

    rows_r = a_w_r.shape[2]
    small_f32 = jnp.concatenate([_rows(a_conv_w[0]), _rows(b_norm[0])], axis=0)
    pad = (-small_f32.shape[0]) % SUBLANES
    small_f32 = jnp.pad(small_f32, ((0, pad), (0, 0)))
    a_w_in_cols, w_r_rows, w_i_rows, small_all = _weights_gather(
        [a_w_in[0].astype(BF16), a_w_r[0].reshape(nb * rows_r, bw).astype(BF16),
         a_w_i[0].reshape(nb * rows_r, bw).astype(BF16), small_f32])
    late_shards = [a_w_out[0].astype(BF16), w_kv.astype(BF16), b_w_in[0].astype(BF16), b_w_out[0].astype(BF16)]
    g_a_w_in = _whole_from_columns(a_w_in_cols, name="a_w_in_whole")
    g_w_r = w_r_rows.reshape(N_DEV, nb, rows_r, bw).transpose(1, 0, 2, 3).reshape(nb, bw, bw)
    g_w_i = w_i_rows.reshape(N_DEV, nb, rows_r, bw).transpose(1, 0, 2, 3).reshape(nb, bw, bw)
    cw_rows = a_conv_w.shape[1] * a_conv_w.shape[2] // LANES
    conv_w_full = small_all[:, :cw_rows, :].reshape(N_DEV, CONV_W, a_conv_w.shape[2])
    conv_w_full = conv_w_full.transpose(1, 0, 2).reshape(CONV_W, C)
    bn_rows = b_norm.shape[1] // LANES
    b_norm_full = small_all[:, cw_rows:cw_rows + bn_rows, :].reshape(1, D)
    kv_norm2, final_norm2 = kv_norm.reshape(1, D), final_norm.reshape(1, D)

    (loss_part, grad_x, dw_a_in, dw_a_out, dw_kv, dw_b_in, dw_b_out, dw_r, dw_i, d_a_norm, d_conv_w,
     d_conv_b, d_b_r, d_b_i, d_lambda, d_kv_norm, d_b_norm, d_final_norm, early_sums,
     early_others) = _forward_backward(
         xs, target, a_norm, g_a_w_in, conv_w_full, a_conv_b, g_w_r, g_w_i, a_b_r, a_b_i, a_lambda,
         kv_norm2, b_norm_full, final_norm2, late_shards=late_shards)

    def lru_shards(dw):
        return dw.reshape(nb, N_DEV, rows_r, bw).transpose(1, 0, 2, 3).reshape(N_DEV, nb * rows_r, bw)

    full = [dw_a_in, dw_a_out.reshape(N_DEV, a_w_out.shape[1], D), lru_shards(dw_r), lru_shards(dw_i)]
    small_parts = [d_a_norm, d_conv_w, d_conv_b, d_b_r, d_b_i, d_lambda, d_kv_norm, d_b_norm, d_final_norm]
    small_sizes = [p.size // LANES for p in small_parts]
    small = jnp.concatenate([_rows(p) for p in small_parts], axis=0)
    outs = _grads_to_sibling(full, small)
    got, small_everyone = outs[:len(full)], outs[-1]
    sums = [_pair_sum(f_, g_, name=f"pair_sum_{i}") for i, (f_, g_) in enumerate(zip(full, got))]
    others = _grads_to_chips([s[1] for s in sums])
    sums = sums[:2] + list(early_sums) + sums[2:]
    others = list(others[:2]) + list(early_others) + list(others[2:])

    def shard2d(w):
        return w.reshape(-1, w.shape[-1])

    names_big = [(a_w_in, m_a_w_in, v_a_w_in), (a_w_out, m_a_w_out, v_a_w_out), (w_kv, m_w_kv, v_w_kv),
                 (b_w_in, m_b_w_in, v_b_w_in), (b_w_out, m_b_w_out, v_b_w_out),
                 (a_w_r, m_a_w_r, v_a_w_r), (a_w_i, m_a_w_i, v_a_w_i)]
    upd_big = []
    for i, (w, m, v) in enumerate(names_big):
        res = _adamw(shard2d(w), shard2d(m), shard2d(v), [others[i]], chip_sums=sums[i][0], name=f"adamw_{i}")
        upd_big.append([r.reshape(w.shape) for r in res])

    soffs = [0]
    for s in small_sizes:
        soffs.append(soffs[-1] + s)

    def small_piece(i):
        return small_everyone[:, soffs[i]:soffs[i + 1], :]

    cw_cols = a_conv_w.shape[2]
    conv_piece = small_piece(1).reshape(N_DEV, CONV_W, C)
    conv_piece = lax.dynamic_slice_in_dim(conv_piece, me * cw_cols, cw_cols, axis=2)
    conv_piece = conv_piece.reshape(N_DEV, CONV_W * cw_cols // LANES, LANES)
    bn_piece = lax.dynamic_slice_in_dim(small_piece(7), me * bn_rows, bn_rows, axis=1)
    small_g = jnp.concatenate([small_piece(0), conv_piece, small_piece(2), small_piece(3), small_piece(4),
                               small_piece(5), small_piece(6), bn_piece, small_piece(8)], axis=1)
    small_w = [(a_norm, m_a_norm, v_a_norm), (a_conv_w, m_a_conv_w, v_a_conv_w),
               (a_conv_b, m_a_conv_b, v_a_conv_b), (a_b_r, m_a_b_r, v_a_b_r), (a_b_i, m_a_b_i, v_a_b_i),
               (a_lambda, m_a_lambda, v_a_lambda), (kv_norm, m_kv_norm, v_kv_norm),
               (b_norm, m_b_norm, v_b_norm), (final_norm, m_final_norm, v_final_norm)]
    pack = lambda idx: jnp.concatenate([_rows(t[idx]) for t in small_w], axis=0)
    res_small = _adamw(pack(0), pack(1), pack(2), [small_g], name="adamw_small")
    woffs = [0]
    for t in small_w:
        woffs.append(woffs[-1] + t[0].size // LANES)
    upd_small = [[r[woffs[i]:woffs[i + 1]].reshape(small_w[i][0].shape) for r in res_small]
                 for i in range(len(small_w))]

    order = [("s", 0), ("b", 0), ("s", 1), ("s", 2), ("b", 5), ("s", 3), ("b", 6), ("s", 4), ("s", 5),
             ("b", 1), ("s", 6), ("b", 2), ("s", 7), ("b", 3), ("b", 4), ("s", 8)]
    per_weight = [(upd_big if kind == "b" else upd_small)[i] for kind, i in order]
    loss = lax.psum(loss_part[0, 0], ("x", "y", "c"))
    result = [loss, grad_x[None]]
    for field in range(4):
        result += [u[field] for u in per_weight]
    return tuple(result)
```

```python
import functools
import math

import jax
import jax.numpy as jnp
from jax import lax
from jax.experimental import pallas as pl
from jax.experimental.pallas import tpu as pltpu

F32 = jnp.float32
BF16 = jnp.bfloat16
MESH = pl.DeviceIdType.MESH

EPS = 1e-6
LOG2E = 1.4426950408889634
WEIGHT_FLOOR_LOG2 = -200.0
LRU_C = 8.0
CONV_W = 4
HEAD_DIM = 128
ADAM_LR = 0.001
ADAM_B1 = 0.9
ADAM_B2 = 0.999
ADAM_EPS = 1e-08
ADAM_WD = 0.01
ADAM_STEP = 10

N_DEV = 8
LANES = 128
SUBLANES = 8
VMEM_LIMIT = 56 * 1024 * 1024

ATT_KEY_BLOCK = 256
ATT_QUERY_BLOCK = 512
SCAN_BLOCK = 256
ROW_BLOCK = 256
MM_TOKEN_BLOCK = 512
MM_WEIGHT_TILE = 1280
MM_CONTRACT_TOKENS = 2048
ANY = pl.BlockSpec(memory_space=pl.ANY)


def _pcall(body, **kw):
    return pl.pallas_call(body, **kw)


def _params(*sem):
    return pltpu.CompilerParams(dimension_semantics=sem, vmem_limit_bytes=VMEM_LIMIT)


def _pick(n, cap):
    if n <= cap:
        return n
    best = None
    for t in range(LANES, cap + 1, LANES):
        if n % t == 0:
            best = t
    assert best is not None, (n, cap)
    return best


def _sigmoid(x):
    return 1.0 / (1.0 + jnp.exp(-x))


def _dot(a, b, ca, cb):
    return lax.dot_general(a, b, (((ca,), (cb,)), ((), ())), preferred_element_type=F32)


def _mm_nn(a, w, *, name, out_dtype, col_off=0, cols=None, res=None):
    T, K = a.shape
    K2, N = w.shape
    assert K == K2
    cols = N if cols is None else cols
    tm = min(T, MM_TOKEN_BLOCK)
    tn = _pick(cols, MM_WEIGHT_TILE)
    assert col_off % tn == 0
    off = col_off // tn
    has_res = res is not None

    def body(a_ref, b_ref, *rest):
        o_ref = rest[-1]
        acc = jnp.dot(a_ref[...].astype(BF16), b_ref[...], preferred_element_type=F32)
        if has_res:
            acc = acc + rest[0][...]
        o_ref[...] = acc.astype(out_dtype)

    in_specs = [pl.BlockSpec((tm, K), lambda j, i: (i, 0)),
                pl.BlockSpec((K, tn), lambda j, i: (0, off + j))]
    args = [a, w]
    if has_res:
        in_specs.append(pl.BlockSpec((tm, tn), lambda j, i: (i, j)))
        args.append(res)
    return _pcall(
        body, name=name, grid=(cols // tn, T // tm), in_specs=in_specs,
        out_specs=pl.BlockSpec((tm, tn), lambda j, i: (i, j)),
        out_shape=jax.ShapeDtypeStruct((T, cols), out_dtype),
        compiler_params=_params("parallel", "parallel"))(*args)


def _mm_nt(a, w, *, name, out_dtype=F32):
    T, K = a.shape
    N, K2 = w.shape
    assert K == K2
    tm = min(T, MM_TOKEN_BLOCK)
    tn = _pick(N, MM_WEIGHT_TILE)

    def body(a_ref, b_ref, o_ref):
        o_ref[...] = _dot(a_ref[...].astype(BF16), b_ref[...], 1, 1).astype(out_dtype)

    return _pcall(
        body, name=name, grid=(N // tn, T // tm),
        in_specs=[pl.BlockSpec((tm, K), lambda j, i: (i, 0)), pl.BlockSpec((tn, K), lambda j, i: (j, 0))],
        out_specs=pl.BlockSpec((tm, tn), lambda j, i: (i, j)),
        out_shape=jax.ShapeDtypeStruct((T, N), out_dtype),
        compiler_params=_params("parallel", "parallel"))(a, w)


def _mm_tn(a, b, *, name, shards=1):
    T, Ko = a.shape
    T2, N = b.shape
    assert T == T2
    n = N // shards
    tt = min(T, MM_CONTRACT_TOKENS)
    tko = _pick(Ko, 1024)
    tn = _pick(n, 1024)
    per = n // tn

    def body(a_ref, b_ref, o_ref):
        t = pl.program_id(2)
        p = _dot(a_ref[...].astype(BF16), b_ref[...].astype(BF16), 0, 0)

        @pl.when(t == 0)
        def _():
            o_ref[...] = p

        @pl.when(t > 0)
        def _():
            o_ref[...] += p

    if shards == 1:
        out_spec = pl.BlockSpec((tko, tn), lambda i, j, t: (i, j))
        out_shape = jax.ShapeDtypeStruct((Ko, N), F32)
    else:
        out_spec = pl.BlockSpec((None, tko, tn), lambda i, j, t: (j // per, i, j % per))
        out_shape = jax.ShapeDtypeStruct((shards, Ko, n), F32)
    return _pcall(
        body, name=name, grid=(Ko // tko, N // tn, T // tt),
        in_specs=[pl.BlockSpec((tt, tko), lambda i, j, t: (t, i)),
                  pl.BlockSpec((tt, tn), lambda i, j, t: (t, j))],
        out_specs=out_spec, out_shape=out_shape,
        compiler_params=_params("parallel", "parallel", "arbitrary"))(a, b)


def _rms_fwd(x, gains, *, name):
    T, D = x.shape
    tm = min(T, ROW_BLOCK)
    n = len(gains)

    def body(x_ref, *refs):
        xv = x_ref[...]
        xh = xv * lax.rsqrt(jnp.mean(xv * xv, axis=-1, keepdims=True) + EPS)
        for g_ref, o_ref in zip(refs[:n], refs[n:]):
            o_ref[...] = (xh * g_ref[...]).astype(BF16)

    row = pl.BlockSpec((tm, D), lambda i: (i, 0))
    vec = pl.BlockSpec((1, D), lambda i: (0, 0))
    return _pcall(
        body, name=name, grid=(T // tm,), in_specs=[row] + [vec] * n, out_specs=[row] * n,
        out_shape=[jax.ShapeDtypeStruct((T, D), BF16)] * n,
        compiler_params=_params("parallel"))(x, *gains)


def _rms_bwd(x, dres, dhs, gains, *, name, rider=None):
    T, D = x.shape
    tm = min(T, ROW_BLOCK)
    steps = T // tm
    n = len(gains)
    rider = rider or _Rider([], [], [], None)
    nri, nro = len(rider.inputs), len(rider.out_shapes)

    def body(x_ref, dres_ref, *refs):
        dh_refs, g_refs = refs[:n], refs[n:2 * n]
        refs = refs[2 * n:]
        rider_in, refs = refs[:nri], refs[nri:]
        dx_ref, dg_refs = refs[0], refs[1:1 + n]
        rider_out, sems = refs[1 + n:1 + n + nro], refs[1 + n + nro:]
        i = pl.program_id(0)
        if nro:
            start, finish = rider.bind(rider_in, rider_out, sems)
            pl.when(i == 0)(start)
        xv = x_ref[...]
        r = lax.rsqrt(jnp.mean(xv * xv, axis=-1, keepdims=True) + EPS)
        xh = xv * r
        dxh = jnp.zeros_like(xv)
        for dh_ref, g_ref, dg_ref in zip(dh_refs, g_refs, dg_refs):
            dh = dh_ref[...]
            part = jnp.sum(dh * xh, axis=0, keepdims=True)

            @pl.when(i == 0)
            def _():
                dg_ref[...] = part

            @pl.when(i > 0)
            def _():
                dg_ref[...] += part

            dxh = dxh + dh * g_ref[...]
        dx_ref[...] = dres_ref[...] + r * (dxh - xh * jnp.mean(dxh * xh, axis=-1, keepdims=True))
        if nro:
            pl.when(i == steps - 1)(finish)

    row = pl.BlockSpec((tm, D), lambda i: (i, 0))
    vec = pl.BlockSpec((1, D), lambda i: (0, 0))
    return _pcall(
        body, name=name, grid=(steps,), in_specs=[row, row] + [row] * n + [vec] * n + [ANY] * nri,
        out_specs=[row] + [vec] * n + [ANY] * nro,
        out_shape=[jax.ShapeDtypeStruct((T, D), F32)] + [jax.ShapeDtypeStruct((1, D), F32)] * n
                  + rider.out_shapes,
        scratch_shapes=rider.scratch,
        compiler_params=_params("arbitrary"))(x, dres, *dhs, *gains, *rider.inputs)


def _final_loss(x, target, gain, *, name):
    T, D = x.shape
    tm = min(T, ROW_BLOCK)

    def body(x_ref, t_ref, g_ref, dx_ref, dg_ref, loss_ref):
        i = pl.program_id(0)
        xv = x_ref[...]
        g = g_ref[...]
        r = lax.rsqrt(jnp.mean(xv * xv, axis=-1, keepdims=True) + EPS)
        xh = xv * r
        err = xh * g - t_ref[...]
        part_loss = 0.5 * jnp.sum(jnp.mean(err * err, axis=-1, keepdims=True), axis=0, keepdims=True)
        dy = err * (1.0 / D)
        part_g = jnp.sum(dy * xh, axis=0, keepdims=True)

        @pl.when(i == 0)
        def _():
            dg_ref[...] = part_g
            loss_ref[...] = jnp.broadcast_to(part_loss, loss_ref.shape)

        @pl.when(i > 0)
        def _():
            dg_ref[...] += part_g
            loss_ref[...] += jnp.broadcast_to(part_loss, loss_ref.shape)

        dxh = dy * g
        dx_ref[...] = r * (dxh - xh * jnp.mean(dxh * xh, axis=-1, keepdims=True))

    row = pl.BlockSpec((tm, D), lambda i: (i, 0))
    vec = pl.BlockSpec((1, D), lambda i: (0, 0))
    return _pcall(
        body, name=name, grid=(T // tm,), in_specs=[row, row, vec],
        out_specs=[row, vec, pl.BlockSpec((1, LANES), lambda i: (0, 0))],
        out_shape=[jax.ShapeDtypeStruct((T, D), F32), jax.ShapeDtypeStruct((1, D), F32),
                   jax.ShapeDtypeStruct((1, LANES), F32)],
        compiler_params=_params("arbitrary"))(x, target, gain)


def _shift_down(x, prev_tail, j, row):
    tb = x.shape[0]
    prev = jnp.tile(prev_tail, (tb // SUBLANES, 1))
    return jnp.where(row >= j, pltpu.roll(x, j, 0), pltpu.roll(prev, j, 0))


def _shift_up(x, next_head, j, row):
    tb = x.shape[0]
    nxt = jnp.tile(next_head, (tb // SUBLANES, 1))
    return jnp.where(row < tb - j, pltpu.roll(x, tb - j, 0), pltpu.roll(nxt, tb - j, 0))


def _lru_gates(xb, wr, wi, br, bi, lam):
    xbb = xb.astype(BF16)
    r = _sigmoid(jnp.dot(xbb, wr, preferred_element_type=F32) + br)
    i = _sigmoid(jnp.dot(xbb, wi, preferred_element_type=F32) + bi)
    sp = jnp.maximum(-lam, 0.0) + jnp.log1p(jnp.exp(-jnp.abs(lam)))
    log_a = (-LRU_C) * r * sp
    a = jnp.exp(log_a)
    a2 = a * a
    mult = jnp.sqrt(jnp.maximum(-jnp.tanh(log_a) * (1.0 + a2), 0.0))
    return xbb, r, i, sp, a, a2, mult


def _acore_fwd(proj, conv_w, conv_b, w_r, w_i, b_r, b_i, lam, *, name, riders=()):
    T, C2 = proj.shape
    C = C2 // 2
    nb, bw, _ = w_r.shape
    tb = min(T, SCAN_BLOCK)
    nt = T // tb
    nr = len(riders)

    def body(xp_ref, gate_ref, cw_ref, cb_ref, wr_ref, wi_ref, br_ref, bi_ref, lam_ref, *refs):
        rider_in, refs = refs[:nr], refs[nr:]
        xb_ref, h_ref, yg_ref = refs[:3]
        rider_out, refs = refs[3:3 + nr], refs[3 + nr:]
        tail_ref, hlast_ref = refs[:2]
        t = pl.program_id(1)
        if nr:
            gather = _Gather(rider_in, rider_out, *refs[2:])
            pl.when((pl.program_id(0) == 0) & (t == 0))(gather.start)

        @pl.when(t == 0)
        def _():
            tail_ref[...] = jnp.zeros_like(tail_ref)
            hlast_ref[...] = jnp.zeros_like(hlast_ref)

        row = lax.broadcasted_iota(jnp.int32, (tb, bw), 0)
        xp = xp_ref[...]
        tail = tail_ref[...]
        xb = cb_ref[...] + cw_ref[CONV_W - 1:CONV_W, :] * xp
        for j in range(1, CONV_W):
            xb = xb + cw_ref[CONV_W - 1 - j:CONV_W - j, :] * _shift_down(xp, tail, j, row)
        tail_ref[...] = xp[tb - SUBLANES:, :]
        xb_ref[...] = xb

        _, r, i, sp, a, a2, mult = _lru_gates(xb, wr_ref[...], wi_ref[...], br_ref[...], bi_ref[...],
                                              lam_ref[...])
        ca, cb = a, mult * (i * xb)
        s = 1
        while s < tb:
            m = row >= s
            cb = jnp.where(m, ca * pltpu.roll(cb, s, 0) + cb, cb)
            ca = jnp.where(m, ca * pltpu.roll(ca, s, 0), ca)
            s *= 2
        h = cb + ca * hlast_ref[SUBLANES - 1:SUBLANES, :]
        hlast_ref[...] = h[tb - SUBLANES:, :]
        h_ref[...] = h
        gate = gate_ref[...]
        yg_ref[...] = (h * (gate * _sigmoid(gate))).astype(BF16)
        if nr:
            pl.when((pl.program_id(0) == nb - 1) & (t == nt - 1))(gather.finish)

    blk = lambda off: pl.BlockSpec((tb, bw), lambda n, t: (t, off + n))
    vec = pl.BlockSpec((1, bw), lambda n, t: (0, n))
    wspec = pl.BlockSpec((None, bw, bw), lambda n, t: (n, 0, 0))
    return _pcall(
        body, name=name, grid=(nb, nt),
        in_specs=[blk(0), blk(nb), pl.BlockSpec((CONV_W, bw), lambda n, t: (0, n)), vec, wspec, wspec,
                  vec, vec, vec] + [ANY] * nr,
        out_specs=[blk(0), blk(0), blk(0)] + [ANY] * nr,
        out_shape=[jax.ShapeDtypeStruct((T, C), F32), jax.ShapeDtypeStruct((T, C), F32),
                   jax.ShapeDtypeStruct((T, C), BF16)]
                  + [jax.ShapeDtypeStruct((N_DEV,) + r.shape, r.dtype) for r in riders],
        scratch_shapes=[pltpu.VMEM((SUBLANES, bw), F32), pltpu.VMEM((SUBLANES, bw), F32)]
                       + (_Gather.scratch(nr) if nr else []),
        compiler_params=_params("arbitrary" if nr else "parallel", "arbitrary"))(
            proj, proj, conv_w, conv_b, w_r, w_i, b_r, b_i, lam, *riders)


def _acore_bwd(dyg, proj, xb_all, h_all, conv_w, w_r, w_i, b_r, b_i, lam, *, name, rider=None):
    T, C2 = proj.shape
    C = C2 // 2
    nb, bw, _ = w_r.shape
    tb = min(T, SCAN_BLOCK)
    nt = T // tb
    per8 = tb // SUBLANES
    rider = rider or _Rider([], [], [], None)
    nri, nro = len(rider.inputs), len(rider.out_shapes)

    def body(dyg_ref, xp_ref, gate_ref, xb_ref, h_ref, xp_prev_ref, h_prev_ref, cw_ref,
             wr_ref, wi_ref, br_ref, bi_ref, lam_ref, *refs):
        rider_in, refs = refs[:nri], refs[nri:]
        dxp_ref, dgate_ref, dcw_ref, dcb_ref, dbr_ref, dbi_ref, dlam_ref, dwr_ref, dwi_ref = refs[:9]
        rider_out, refs = refs[9:9 + nro], refs[9 + nro:]
        gh_next_ref, a_next_ref, dxb_next_ref = refs[:3]
        step = pl.program_id(1)
        first_block = step == nt - 1
        if nro:
            start, finish = rider.bind(rider_in, rider_out, refs[3:])
            pl.when((pl.program_id(0) == 0) & (step == 0))(start)

        @pl.when(step == 0)
        def _():
            gh_next_ref[...] = jnp.zeros_like(gh_next_ref)
            a_next_ref[...] = jnp.zeros_like(a_next_ref)
            dxb_next_ref[...] = jnp.zeros_like(dxb_next_ref)

        row = lax.broadcasted_iota(jnp.int32, (tb, bw), 0)
        keep = jnp.where(first_block, 0.0, 1.0)
        h_prev = h_prev_ref[...] * keep
        xp_prev = xp_prev_ref[...] * keep
        xp, gate, xb, h, dyg_v = xp_ref[...], gate_ref[...], xb_ref[...], h_ref[...], dyg_ref[...]
        lam_v = lam_ref[...]
        wr, wi = wr_ref[...], wi_ref[...]

        sg = _sigmoid(gate)
        dh = dyg_v * (gate * sg)
        dgate_ref[...] = (dyg_v * h * (sg * (1.0 + gate * (1.0 - sg)))).astype(BF16)

        xbb, r, i, sp, a, a2, mult = _lru_gates(xb, wr, wi, br_ref[...], bi_ref[...], lam_v)

        cg = dh
        cc = _shift_up(a, a_next_ref[...], 1, row)
        s = 1
        while s < tb:
            m = row < tb - s
            cg = jnp.where(m, cc * pltpu.roll(cg, tb - s, 0) + cg, cg)
            cc = jnp.where(m, cc * pltpu.roll(cc, tb - s, 0), cc)
            s *= 2
        gh = cg + cc * gh_next_ref[0:1, :]
        gh_next_ref[...] = gh[0:SUBLANES, :]
        a_next_ref[...] = a[0:SUBLANES, :]

        da = gh * _shift_down(h, h_prev, 1, row)
        dmult = gh * (i * xb)
        di = gh * mult * xb
        dxb = gh * mult * i
        dla = da * a - dmult * jnp.where(mult > 0.0, a2 / mult, 0.0)
        dr = dla * ((-LRU_C) * sp)
        dsp = jnp.sum(dla * ((-LRU_C) * r), axis=0, keepdims=True)
        dlam_part = dsp * (-_sigmoid(-lam_v))
        dpr = dr * r * (1.0 - r)
        dpi = di * i * (1.0 - i)
        dbr_part = jnp.sum(dpr, axis=0, keepdims=True)
        dbi_part = jnp.sum(dpi, axis=0, keepdims=True)
        dprb, dpib = dpr.astype(BF16), dpi.astype(BF16)
        dwr_part = _dot(xbb, dprb, 0, 0)
        dwi_part = _dot(xbb, dpib, 0, 0)
        dxb = dxb + _dot(dprb, wr, 1, 1) + _dot(dpib, wi, 1, 1)

        dxb_next = dxb_next_ref[...]
        dxp = cw_ref[CONV_W - 1:CONV_W, :] * dxb
        for j in range(1, CONV_W):
            dxp = dxp + cw_ref[CONV_W - 1 - j:CONV_W - j, :] * _shift_up(dxb, dxb_next, j, row)
        dxb_next_ref[...] = dxb[0:SUBLANES, :]
        dxp_ref[...] = dxp.astype(BF16)
        dcb_part = jnp.sum(dxb, axis=0, keepdims=True)
        dcw_rows = []
        for k in range(CONV_W):
            j = CONV_W - 1 - k
            sh = xp if j == 0 else _shift_down(xp, xp_prev, j, row)
            dcw_rows.append(jnp.sum(dxb * sh, axis=0, keepdims=True))

        @pl.when(step == 0)
        def _():
            for k in range(CONV_W):
                dcw_ref[k:k + 1, :] = dcw_rows[k]
            dcb_ref[...] = dcb_part
            dbr_ref[...] = dbr_part
            dbi_ref[...] = dbi_part
            dlam_ref[...] = dlam_part
            dwr_ref[...] = dwr_part
            dwi_ref[...] = dwi_part

        @pl.when(step > 0)
        def _():
            for k in range(CONV_W):
                dcw_ref[k:k + 1, :] += dcw_rows[k]
            dcb_ref[...] += dcb_part
            dbr_ref[...] += dbr_part
            dbi_ref[...] += dbi_part
            dlam_ref[...] += dlam_part
            dwr_ref[...] += dwr_part
            dwi_ref[...] += dwi_part

        if nro:
            pl.when((pl.program_id(0) == nb - 1) & (step == nt - 1))(finish)

    rev = lambda s: nt - 1 - s
    blk = lambda off: pl.BlockSpec((tb, bw), lambda n, s: (rev(s), off + n))
    prev8 = lambda off: pl.BlockSpec(
        (SUBLANES, bw), lambda n, s: (jnp.maximum(rev(s) * per8 - 1, 0), off + n))
    vec = pl.BlockSpec((1, bw), lambda n, s: (0, n))
    wspec = pl.BlockSpec((None, bw, bw), lambda n, s: (n, 0, 0))
    cwspec = pl.BlockSpec((CONV_W, bw), lambda n, s: (0, n))
    vshape = jax.ShapeDtypeStruct((1, C), F32)
    wshape = jax.ShapeDtypeStruct((nb, bw, bw), F32)
    return _pcall(
        body, name=name, grid=(nb, nt),
        in_specs=[blk(0), blk(0), blk(nb), blk(0), blk(0), prev8(0), prev8(0), cwspec,
                  wspec, wspec, vec, vec, vec] + [ANY] * nri,
        out_specs=[blk(0), blk(0), cwspec, vec, vec, vec, vec, wspec, wspec] + [ANY] * nro,
        out_shape=[jax.ShapeDtypeStruct((T, C), BF16), jax.ShapeDtypeStruct((T, C), BF16),
                   jax.ShapeDtypeStruct((CONV_W, C), F32), vshape, vshape, vshape, vshape,
                   wshape, wshape] + rider.out_shapes,
        scratch_shapes=[pltpu.VMEM((SUBLANES, bw), F32)] * 3 + rider.scratch,
        compiler_params=_params("arbitrary" if nro else "parallel", "arbitrary"))(
            dyg, proj, proj, xb_all, h_all, proj, h_all, conv_w, w_r, w_i, b_r, b_i, lam, *rider.inputs)


def _later_sum(lk, tri):
    return jnp.dot(lk.astype(BF16), tri, preferred_element_type=F32)


def _log2_sigmoids(y):
    t = jnp.log(1.0 + jnp.exp2(-jnp.abs(y))) * LOG2E
    ls = jnp.minimum(y, 0.0) - t
    return ls, ls - y


def _attn_blocks(T):
    bk = min(T, ATT_KEY_BLOCK)
    bq = min(T, ATT_QUERY_BLOCK)
    return bq, bk, bq // bk


def _attn_fwd(q, kv, gate, *, name):
    T, HD = q.shape
    H = HD // HEAD_DIM
    bq, bk, per = _attn_blocks(T)
    scale = 1.0 / math.sqrt(HEAD_DIM)

    def body(q_ref, k_ref, v_ref, g_ref, o_ref, og_ref, lt_ref, w_ref):
        i = pl.program_id(1)
        qv = q_ref[...]
        tr = lax.broadcasted_iota(jnp.int32, (bk, bk), 0)
        tc = lax.broadcasted_iota(jnp.int32, (bk, bk), 1)
        tri = (tr > tc).astype(BF16)
        ahead = (lax.broadcasted_iota(jnp.int32, (bq, bk), 0)
                 - lax.broadcasted_iota(jnp.int32, (bq, bk), 1))

        def starts_of(top):
            return [pl.multiple_of((top - d) * bk, bk) for d in range(per)]

        def scores(top):
            return [_dot(qv, k_ref[pl.ds(ks, bk), :], 1, 1) for ks in starts_of(top)]

        def weights(top, zs, c, mask):
            lss, sums, css, causals = [], [], [], []
            for ks, z in zip(starts_of(top), zs):
                ls, lk = _log2_sigmoids(z * (scale * LOG2E))
                if mask:
                    causals.append(ahead > ks - i * bq)
                    lk = jnp.where(causals[-1], lk, 0.0)
                lss.append(ls)
                sums.append(jnp.sum(lk, axis=1, keepdims=True))
                css.append(_later_sum(lk, tri))
            for d in range(per):
                w = jnp.exp2(lss[d] + (css[d] + c))
                if mask:
                    w = jnp.where(causals[d], w, 0.0)
                w_ref[d] = w.astype(BF16)
                c = c + sums[d]
            return c

        def values(top, acc):
            for d, ks in enumerate(starts_of(top)):
                acc = acc + jnp.dot(w_ref[d], v_ref[pl.ds(ks, bk), :], preferred_element_type=F32)
            return acc

        def more(state):
            gg, _, _, largest = state
            return (gg <= i) & (largest > WEIGHT_FLOOR_LOG2)

        def step(state):
            gg, acc, c, _ = state
            top = (i - gg) * per + per - 1
            zs = scores(top)
            acc = values(top + per, acc)
            c = weights(top, zs, c, False)
            return gg + 1, acc, c, jnp.max(c)

        diag_top = i * per + per - 1
        c = weights(diag_top, scores(diag_top), jnp.zeros((bq, 1), F32), True)
        gg, acc, c, _ = lax.while_loop(more, step, (1, jnp.zeros((bq, HEAD_DIM), F32), c, jnp.max(c)))
        acc = values((i - gg + 1) * per + per - 1, acc)
        o_ref[...] = acc
        g = g_ref[...]
        og_ref[...] = (acc * (g * _sigmoid(g))).astype(BF16)
        lane = lax.broadcasted_iota(jnp.int32, (bq, HEAD_DIM), 1)
        lt_ref[...] = jnp.where(lane == 1, (i - gg + 1).astype(F32), jnp.broadcast_to(c, (bq, HEAD_DIM)))

    qspec = pl.BlockSpec((bq, HEAD_DIM), lambda h, i: (i, h))
    return _pcall(
        body, name=name, grid=(H, T // bq),
        in_specs=[qspec, pl.BlockSpec((T, HEAD_DIM), lambda h, i: (0, h)),
                  pl.BlockSpec((T, HEAD_DIM), lambda h, i: (0, H + h)), qspec],
        out_specs=[qspec, qspec, qspec],
        out_shape=[jax.ShapeDtypeStruct((T, HD), F32), jax.ShapeDtypeStruct((T, HD), BF16),
                   jax.ShapeDtypeStruct((T, HD), F32)],
        scratch_shapes=[pltpu.VMEM((per, bq, bk), BF16)],
        compiler_params=_params("parallel", "arbitrary"))(q, kv, kv, gate)


def _attn_bwd(q, kv, gate, o, ltot, dog, *, name):
    T, HD = q.shape
    H = HD // HEAD_DIM
    bq, bk, per = _attn_blocks(T)
    nq = T // bq
    scale = 1.0 / math.sqrt(HEAD_DIM)

    def body(q_ref, k_ref, v_ref, g_ref, o_ref, lt_ref, dog_ref,
             dq_ref, dg_ref, dk_ref, dv_ref, dk_acc, dv_acc, dz_ref, w_ref):
        i = pl.program_id(1)

        @pl.when(i == 0)
        def _():
            dk_acc[...] = jnp.zeros_like(dk_acc)
            dv_acc[...] = jnp.zeros_like(dv_acc)

        qv = q_ref[...]
        g, ov, dogv = g_ref[...], o_ref[...], dog_ref[...]
        sg = _sigmoid(g)
        do = dogv * (g * sg)
        dg_ref[...] = (dogv * ov * (sg * (1.0 + g * (1.0 - sg)))).astype(BF16)
        dob = do.astype(BF16)
        ltot_v = lt_ref[:, 0:1]
        tr = lax.broadcasted_iota(jnp.int32, (bk, bk), 0)
        tc = lax.broadcasted_iota(jnp.int32, (bk, bk), 1)
        tri_later = (tr > tc).astype(BF16)
        tri_excl = (tr < tc).astype(BF16)
        ahead = (lax.broadcasted_iota(jnp.int32, (bq, bk), 0)
                 - lax.broadcasted_iota(jnp.int32, (bq, bk), 1))

        def starts_of(first):
            return [pl.multiple_of((first + d) * bk, bk) for d in range(per)]

        def scores(first):
            return ([_dot(qv, k_ref[pl.ds(ks, bk), :], 1, 1) for ks in starts_of(first)],
                    [_dot(dob, v_ref[pl.ds(ks, bk), :], 1, 1) for ks in starts_of(first)])

        def front(first, zs, dws, p_lk, p_g, mask):
            lss, css, causals = [], [], []
            for ks, z in zip(starts_of(first), zs):
                ls, lk = _log2_sigmoids(z * (scale * LOG2E))
                if mask:
                    causals.append(ahead > ks - i * bq)
                    lk = jnp.where(causals[-1], lk, 0.0)
                lss.append(ls)
                p_lk = p_lk + jnp.sum(lk, axis=1, keepdims=True)
                css.append((ltot_v - p_lk) + _later_sum(lk, tri_later))
            gms, befores = [], []
            for d in range(per):
                w = jnp.exp2(lss[d] + css[d])
                if mask:
                    w = jnp.where(causals[d], w, 0.0)
                gm = dws[d] * w
                gms.append(gm)
                w_ref[d] = w.astype(BF16)
                befores.append(jnp.dot(gm.astype(BF16), tri_excl, preferred_element_type=F32) + p_g)
                p_g = p_g + jnp.sum(gm, axis=1, keepdims=True)
            for d in range(per):
                dz = gms[d] - jnp.exp2(lss[d]) * (gms[d] + befores[d])
                if mask:
                    dz = jnp.where(causals[d], dz, 0.0)
                dz_ref[d] = (dz * scale).astype(BF16)
            return p_lk, p_g

        def back(first, dq):
            for d, ks in enumerate(starts_of(first)):
                dzb = dz_ref[d]
                dq = dq + jnp.dot(dzb, k_ref[pl.ds(ks, bk), :], preferred_element_type=F32)
                dk_acc[pl.ds(ks, bk), :] += _dot(dzb, qv, 0, 0)
                dv_acc[pl.ds(ks, bk), :] += _dot(w_ref[d], dob, 0, 0)
            return dq

        def step(mask):
            def trip(g, state):
                dq, p_lk, p_g = state
                zs, dws = scores(g * per)
                dq = back((g - 1) * per, dq)
                return (dq,) + front(g * per, zs, dws, p_lk, p_g, mask)
            return trip

        g0 = jnp.max(lt_ref[0:1, 1:2]).astype(jnp.int32)
        zero = jnp.zeros((bq, 1), F32)
        state = (jnp.zeros((bq, HEAD_DIM), F32),) + front(g0 * per, *scores(g0 * per), zero, zero, True)
        state = lax.fori_loop(g0 + 1, i, step(False), state)
        state = lax.fori_loop(jnp.maximum(i, g0 + 1), i + 1, step(True), state)
        dq_ref[...] = back(i * per, state[0]).astype(BF16)

        @pl.when(i == nq - 1)
        def _():
            dk_ref[...] = dk_acc[...].astype(BF16)
            dv_ref[...] = dv_acc[...].astype(BF16)

    qspec = pl.BlockSpec((bq, HEAD_DIM), lambda h, i: (i, h))
    kspec = pl.BlockSpec((T, HEAD_DIM), lambda h, i: (0, h))
    return _pcall(
        body, name=name, grid=(H, nq),
        in_specs=[qspec, kspec, pl.BlockSpec((T, HEAD_DIM), lambda h, i: (0, H + h)),
                  qspec, qspec, qspec, qspec],
        out_specs=[qspec, qspec, kspec, kspec],
        out_shape=[jax.ShapeDtypeStruct((T, HD), BF16)] * 4,
        scratch_shapes=[pltpu.VMEM((T, HEAD_DIM), F32)] * 2 + [pltpu.VMEM((per, bq, bk), BF16)] * 2,
        compiler_params=_params("parallel", "arbitrary"))(q, kv, kv, gate, o, ltot, dog)


def _position():
    return lax.axis_index("x"), lax.axis_index("y"), lax.axis_index("c")


def _chip_of(k, x, y):
    return (1 - x if k & 1 else x), (1 - y if k & 2 else y)


class _Gather:
    @staticmethod
    def scratch(n):
        return [pltpu.SemaphoreType.DMA((n, 7)), pltpu.SemaphoreType.DMA((n, 7)),
                pltpu.SemaphoreType.DMA((n,))]

    def __init__(self, ins, outs, send_sems, recv_sems, local_sems):
        self.ins, self.outs, self.n = ins, outs, len(ins)
        self.send_sems, self.recv_sems, self.local_sems = send_sems, recv_sems, local_sems
        x, y, c = _position()
        self.me, self.sibling = (x, y, c), (x, y, 1 - c)
        self.chips = [_chip_of(k, x, y) for k in (1, 2, 3)]

    def copy(self, a, k, block, to, src=None):
        slot = self.outs[a].at[4 * block[0] + 2 * block[1] + block[2]]
        return pltpu.make_async_remote_copy(
            src_ref=slot if src is None else src, dst_ref=slot,
            send_sem=self.send_sems.at[a, k], recv_sem=self.recv_sems.at[a, k],
            device_id=to, device_id_type=MESH)

    def own_copies(self):
        x, y, c = self.me
        mine = [pltpu.make_async_copy(self.ins[a], self.outs[a].at[4 * x + 2 * y + c], self.local_sems.at[a])
                for a in range(self.n)]
        first = []
        for a in range(self.n):
            first.append(self.copy(a, 0, self.me, self.sibling, src=self.ins[a]))
            first += [self.copy(a, 1 + j, self.me, (*chip, c), src=self.ins[a])
                      for j, chip in enumerate(self.chips)]
        return mine, first

    def start(self):
        mine, first = self.own_copies()
        for cp in mine + first:
            cp.start()

    def finish(self):
        c = self.me[2]
        mine, first = self.own_copies()
        passed = []
        for j, chip in enumerate(self.chips):
            for a in range(self.n):
                self.copy(a, 1 + j, (*chip, c), self.me).wait_recv()
                fwd = self.copy(a, 4 + j, (*chip, c), self.sibling)
                fwd.start()
                passed.append(fwd)
        for a in range(self.n):
            self.copy(a, 0, self.sibling, self.me).wait_recv()
            for j, chip in enumerate(self.chips):
                self.copy(a, 4 + j, (*chip, 1 - c), self.me).wait_recv()
        for cp in first + passed:
            cp.wait_send()
        for cp in mine:
            cp.wait()


def _weights_gather(shards):
    n = len(shards)

    def body(*refs):
        gather = _Gather(refs[:n], refs[n:2 * n], *refs[2 * n:])
        gather.start()
        gather.finish()

    return _pcall(
        body, name="weights_gather", in_specs=[ANY] * n, out_specs=[ANY] * n,
        out_shape=[jax.ShapeDtypeStruct((N_DEV,) + s.shape, s.dtype) for s in shards],
        scratch_shapes=_Gather.scratch(n))(*shards)


class _Rider:
    def __init__(self, inputs, out_shapes, scratch, copies):
        self.inputs, self.out_shapes, self.scratch, self.copies = inputs, out_shapes, scratch, copies

    def bind(self, ins, outs, sems):
        def start():
            for cp in self.copies(ins, outs, sems):
                cp.start()

        def finish():
            cps = self.copies(ins, outs, sems)
            for cp in cps:
                cp.wait_send()
            for cp in cps:
                cp.wait_recv()

        return start, finish


def _sibling_rider(grads):
    n = len(grads)

    def copies(ins, outs, sems):
        x, y, c = _position()
        return [pltpu.make_async_remote_copy(
            src_ref=ins[a].at[2 * chip + (1 - c)], dst_ref=outs[a].at[chip],
            send_sem=sems[0].at[a, chip], recv_sem=sems[1].at[a, chip],
            device_id=(x, y, 1 - c), device_id_type=MESH) for a in range(n) for chip in range(4)]

    return _Rider(list(grads), [jax.ShapeDtypeStruct((4,) + g.shape[1:], g.dtype) for g in grads],
                  [pltpu.SemaphoreType.DMA((n, 4)), pltpu.SemaphoreType.DMA((n, 4))], copies)


def _chips_rider(parts):
    n = len(parts)

    def copies(ins, outs, sems):
        x, y, c = _position()
        cps = []
        for a in range(n):
            for k in range(3):
                cx, cy = _chip_of(k + 1, x, y)
                cps.append(pltpu.make_async_remote_copy(
                    src_ref=ins[a].at[2 * cx + cy], dst_ref=outs[a].at[k],
                    send_sem=sems[0].at[a, k], recv_sem=sems[1].at[a, k],
                    device_id=(cx, cy, c), device_id_type=MESH))
        return cps

    return _Rider(list(parts), [jax.ShapeDtypeStruct((3,) + p.shape[1:], p.dtype) for p in parts],
                  [pltpu.SemaphoreType.DMA((n, 3)), pltpu.SemaphoreType.DMA((n, 3))], copies)


def _grads_to_sibling(grads, small):
    n = len(grads)
    rider = _sibling_rider(grads)

    def body(*refs):
        g_refs, small_ref = refs[:n], refs[n]
        got, small_all = refs[n + 1:2 * n + 1], refs[2 * n + 1]
        send_sems, recv_sems, small_send, small_recv, small_local = refs[2 * n + 2:]
        x, y, c = _position()
        me = 4 * x + 2 * y + c
        remote = rider.copies(g_refs, got, (send_sems, recv_sems))
        peers = []
        for m in range(1, N_DEV):
            px, py, pc = x ^ (m >> 2), y ^ ((m >> 1) & 1), c ^ (m & 1)
            peers.append(pltpu.make_async_remote_copy(
                src_ref=small_ref, dst_ref=small_all.at[me],
                send_sem=small_send.at[m - 1], recv_sem=small_recv.at[m - 1],
                device_id=(px, py, pc), device_id_type=MESH))
        own = pltpu.make_async_copy(small_ref, small_all.at[me], small_local)
        for cp in peers + remote + [own]:
            cp.start()
        for cp in peers + remote:
            cp.wait_send()
        for m in range(1, N_DEV):
            px, py, pc = x ^ (m >> 2), y ^ ((m >> 1) & 1), c ^ (m & 1)
            pltpu.make_async_remote_copy(
                src_ref=small_ref, dst_ref=small_all.at[4 * px + 2 * py + pc],
                send_sem=small_send.at[m - 1], recv_sem=small_recv.at[m - 1],
                device_id=(px, py, pc), device_id_type=MESH).wait_recv()
        for cp in remote:
            cp.wait_recv()
        own.wait()

    part = [jax.ShapeDtypeStruct((4,) + g.shape[1:], g.dtype) for g in grads]
    return _pcall(
        body, name="grads_to_sibling", in_specs=[ANY] * (n + 1), out_specs=[ANY] * (n + 1),
        out_shape=part + [jax.ShapeDtypeStruct((N_DEV,) + small.shape, small.dtype)],
        scratch_shapes=[pltpu.SemaphoreType.DMA((n, 4)), pltpu.SemaphoreType.DMA((n, 4)),
                        pltpu.SemaphoreType.DMA((7,)), pltpu.SemaphoreType.DMA((7,)),
                        pltpu.SemaphoreType.DMA])(*grads, small)


def _grads_to_chips(parts):
    n = len(parts)
    rider = _chips_rider(parts)

    def body(*refs):
        start, finish = rider.bind(refs[:n], refs[n:2 * n], refs[2 * n:])
        start()
        finish()

    return _pcall(
        body, name="grads_to_chips", in_specs=[ANY] * n, out_specs=[ANY] * n,
        out_shape=rider.out_shapes, scratch_shapes=rider.scratch)(*parts)


def _pair_sum(grad, got, *, name):
    _, R, C = got.shape
    tr = _pick8(R, max(2 * SUBLANES, (1 << 17) // C))

    def body(g_ref, b_ref, o_ref, ob_ref):
        north = lax.axis_index("c") == 1
        for chip in range(4):
            s = jnp.where(north, g_ref[chip, 1], g_ref[chip, 0]) + b_ref[chip]
            o_ref[chip] = s
            ob_ref[chip] = s.astype(BF16)

    spec = pl.BlockSpec((4, tr, C), lambda i: (0, i, 0))
    return _pcall(
        body, name=name, grid=(R // tr,),
        in_specs=[pl.BlockSpec((4, 2, tr, C), lambda i: (0, 0, i, 0)), spec],
        out_specs=[spec, spec],
        out_shape=[jax.ShapeDtypeStruct((4, R, C), F32), jax.ShapeDtypeStruct((4, R, C), BF16)],
        compiler_params=_params("parallel"))(grad.reshape(4, 2, R, C), got)


def _pick8(n, cap):
    if n <= cap:
        return n
    best = None
    for t in range(SUBLANES, cap + 1, SUBLANES):
        if n % t == 0:
            best = t
    assert best is not None, (n, cap)
    return best


def _adamw(w, m, v, parts, *, name, chip_sums=None):
    R, C = w.shape
    tr = _pick8(R, max(SUBLANES, (1 << 17) // C))
    c1 = 1.0 - ADAM_B1 ** ADAM_STEP
    c2 = 1.0 - ADAM_B2 ** ADAM_STEP
    parts = list(parts) if chip_sums is None else [chip_sums] + list(parts)
    np_ = len(parts)

    def body(w_ref, m_ref, v_ref, *refs):
        p_refs = refs[:np_]
        g_ref, d_ref, nm_ref, nv_ref = refs[np_:]
        g = None
        if chip_sums is not None:
            s_ref, p_refs = p_refs[0], p_refs[1:]
            x1, y1 = lax.axis_index("x") == 1, lax.axis_index("y") == 1
            g = jnp.where(x1, jnp.where(y1, s_ref[3], s_ref[2]), jnp.where(y1, s_ref[1], s_ref[0]))
        for p_ref in p_refs:
            for t in [p_ref[k].astype(F32) for k in range(p_ref.shape[0])]:
                g = t if g is None else g + t
        mn = ADAM_B1 * m_ref[...] + (1.0 - ADAM_B1) * g
        vn = ADAM_B2 * v_ref[...] + (1.0 - ADAM_B2) * (g * g)
        d_ref[...] = -ADAM_LR * ((mn / c1) / (jnp.sqrt(vn / c2) + ADAM_EPS) + ADAM_WD * w_ref[...])
        g_ref[...] = g
        nm_ref[...] = mn
        nv_ref[...] = vn

    spec = pl.BlockSpec((tr, C), lambda i: (i, 0))
    pspecs = [pl.BlockSpec((p.shape[0], tr, C), lambda i: (0, i, 0)) for p in parts]
    return _pcall(
        body, name=name, grid=(R // tr,), in_specs=[spec] * 3 + pspecs, out_specs=[spec] * 4,
        out_shape=[jax.ShapeDtypeStruct((R, C), F32)] * 4,
        compiler_params=_params("parallel"))(w, m, v, *parts)


def _rows(a):
    return a.reshape(-1, LANES)


def _whole_from_columns(shards, *, name):
    S, K, n = shards.shape
    tk = _pick8(K, 1024)

    def body(s_ref, o_ref):
        o_ref[...] = s_ref[...]

    return _pcall(
        body, name=name, grid=(K // tk, S),
        in_specs=[pl.BlockSpec((None, tk, n), lambda i, s: (s, i, 0))],
        out_specs=pl.BlockSpec((tk, n), lambda i, s: (i, s)),
        out_shape=jax.ShapeDtypeStruct((K, S * n), shards.dtype),
        compiler_params=_params("parallel", "parallel"))(shards)


def _late_weights(a_w_out_rows, w_kv_cols, b_w_in_cols, b_w_out_rows):
    whole_rows = lambda g: g.reshape(g.shape[0] * g.shape[1], g.shape[2])
    return (whole_rows(a_w_out_rows), _whole_from_columns(w_kv_cols, name="w_kv_whole"),
            _whole_from_columns(b_w_in_cols, name="b_w_in_whole"), whole_rows(b_w_out_rows))


def _forward_backward(xs, target, a_norm, g_a_w_in, conv_w, conv_b, g_w_r, g_w_i, b_r, b_i, lam,
                      kv_norm, b_norm, final_norm, *, late_weights=None, late_shards=None):
    (h_a,) = _rms_fwd(xs, [a_norm], name="a_norm_fwd")
    proj_a = _mm_nn(h_a, g_a_w_in, name="a_in_proj", out_dtype=F32)
    xb, h_rec, yg, *gathered = _acore_fwd(proj_a, conv_w, conv_b, g_w_r, g_w_i, b_r, b_i, lam,
                                          name="a_core_fwd", riders=late_shards or ())
    g_a_w_out, g_w_kv, g_b_w_in, g_b_w_out = _late_weights(*gathered) if late_shards else late_weights
    x1 = _mm_nn(yg, g_a_w_out, name="a_out_proj", out_dtype=F32, res=xs)
    hk, hb = _rms_fwd(x1, [kv_norm, b_norm], name="kv_b_norm_fwd")
    kv = _mm_nn(hk, g_w_kv, name="kv_proj", out_dtype=BF16)
    hd = g_b_w_in.shape[1] // 2
    q = _mm_nn(hb, g_b_w_in, name="q_proj", out_dtype=BF16, col_off=0, cols=hd)
    gate_b = _mm_nn(hb, g_b_w_in, name="b_gate_proj", out_dtype=F32, col_off=hd, cols=hd)
    o, og, ltot = _attn_fwd(q, kv, gate_b, name="attn_fwd")
    x2 = _mm_nn(og, g_b_w_out, name="b_out_proj", out_dtype=F32, res=x1)
    dx2, d_final_norm, loss_part = _final_loss(x2, target, final_norm, name="final_norm_loss")

    dog = _mm_nt(dx2, g_b_w_out, name="b_out_proj_bwd")
    dw_b_out = _mm_tn(og, dx2, name="b_out_proj_wgrad")
    dq, dgate_b, dk, dv = _attn_bwd(q, kv, gate_b, o, ltot, dog, name="attn_bwd")
    dproj_b = jnp.concatenate([dq, dgate_b], axis=1)
    dkv = jnp.concatenate([dk, dv], axis=1)
    dhb = _mm_nt(dproj_b, g_b_w_in, name="b_in_proj_bwd")
    dw_b_in = _mm_tn(hb, dproj_b, name="b_in_proj_wgrad", shards=N_DEV)
    dhk = _mm_nt(dkv, g_w_kv, name="kv_proj_bwd")
    dw_kv = _mm_tn(hk, dkv, name="kv_proj_wgrad", shards=N_DEV)
    early = [dw_kv, dw_b_in, dw_b_out.reshape(N_DEV, -1, dw_b_out.shape[1])] if late_shards else []
    dx1, d_b_norm, d_kv_norm, *got = _rms_bwd(x1, dx2, [dhb, dhk], [b_norm, kv_norm], name="kv_b_norm_bwd",
                                              rider=_sibling_rider(early) if early else None)
    early_sums = [_pair_sum(f_, g_, name=f"pair_sum_early_{i}") for i, (f_, g_) in enumerate(zip(early, got))]
    dyg = _mm_nt(dx1, g_a_w_out, name="a_out_proj_bwd")
    dw_a_out = _mm_tn(yg, dx1, name="a_out_proj_wgrad")
    (dxp, dgate_a, d_conv_w, d_conv_b, d_b_r, d_b_i, d_lambda, dw_r, dw_i, *early_others) = _acore_bwd(
        dyg, proj_a, xb, h_rec, conv_w, g_w_r, g_w_i, b_r, b_i, lam, name="a_core_bwd",
        rider=_chips_rider([s[1] for s in early_sums]) if early else None)
    dproj_a = jnp.concatenate([dxp, dgate_a], axis=1)
    dh_a = _mm_nt(dproj_a, g_a_w_in, name="a_in_proj_bwd")
    dw_a_in = _mm_tn(h_a, dproj_a, name="a_in_proj_wgrad", shards=N_DEV)
    grad_x, d_a_norm = _rms_bwd(xs, dx1, [dh_a], [a_norm], name="a_norm_bwd")
    return (loss_part, grad_x, dw_a_in, dw_a_out, dw_kv, dw_b_in, dw_b_out, dw_r, dw_i, d_a_norm,
            d_conv_w, d_conv_b, d_b_r, d_b_i, d_lambda, d_kv_norm, d_b_norm, d_final_norm,
            early_sums, early_others)


def kernel(x, a_norm, a_w_in, a_conv_w, a_conv_b, a_w_r, a_b_r, a_w_i, a_b_i, a_lambda, a_w_out, kv_norm, w_kv, b_norm, b_w_in, b_w_out, final_norm, loss_target, m_a_norm, m_a_w_in, m_a_conv_w, m_a_conv_b, m_a_w_r, m_a_b_r, m_a_w_i, m_a_b_i, m_a_lambda, m_a_w_out, m_kv_norm, m_w_kv, m_b_norm, m_b_w_in, m_b_w_out, m_final_norm, v_a_norm, v_a_w_in, v_a_conv_w, v_a_conv_b, v_a_w_r, v_a_b_r, v_a_w_i, v_a_b_i, v_a_lambda, v_a_w_out, v_kv_norm, v_w_kv, v_b_norm, v_b_w_in, v_b_w_out, v_final_norm):
    T, D = x.shape[1], x.shape[2]
    nb, bw = a_w_r.shape[1], a_w_r.shape[3]
    C = nb * bw
    me = 4 * lax.axis_index("x") + 2 * lax.axis_index("y") + lax.axis_index("c")
    xs = x[0]
    target = loss_target[0]

    rows_r = a_w_r.shape[2]
    small_f32 = jnp.concatenate([_rows(a_conv_w[0]), _rows(b_norm[0])], axis=0)
    pad = (-small_f32.shape[0]) % SUBLANES
    small_f32 = jnp.pad(small_f32, ((0, pad), (0, 0)))
    a_w_in_cols, w_r_rows, w_i_rows, small_all = _weights_gather(
        [a_w_in[0].astype(BF16), a_w_r[0].reshape(nb * rows_r, bw).astype(BF16),
         a_w_i[0].reshape(nb * rows_r, bw).astype(BF16), small_f32])
    late_shards = [a_w_out[0].astype(BF16), w_kv.astype(BF16), b_w_in[0].astype(BF16), b_w_out[0].astype(BF16)]
    g_a_w_in = _whole_from_columns(a_w_in_cols, name="a_w_in_whole")
    g_w_r = w_r_rows.reshape(N_DEV, nb, rows_r, bw).transpose(1, 0, 2, 3).reshape(nb, bw, bw)
    g_w_i = w_i_rows.reshape(N_DEV, nb, rows_r, bw).transpose(1, 0, 2, 3).reshape(nb, bw, bw)
    cw_rows = a_conv_w.shape[1] * a_conv_w.shape[2] // LANES
    conv_w_full = small_all[:, :cw_rows, :].reshape(N_DEV, CONV_W, a_conv_w.shape[2])
    conv_w_full = conv_w_full.transpose(1, 0, 2).reshape(CONV_W, C)
    bn_rows = b_norm.shape[1] // LANES
    b_norm_full = small_all[:, cw_rows:cw_rows + bn_rows, :].reshape(1, D)
    kv_norm2, final_norm2 = kv_norm.reshape(1, D), final_norm.reshape(1, D)

    (loss_part, grad_x, dw_a_in, dw_a_out, dw_kv, dw_b_in, dw_b_out, dw_r, dw_i, d_a_norm, d_conv_w,
     d_conv_b, d_b_r, d_b_i, d_lambda, d_kv_norm, d_b_norm, d_final_norm, early_sums,
     early_others) = _forward_backward(
         xs, target, a_norm, g_a_w_in, conv_w_full, a_conv_b, g_w_r, g_w_i, a_b_r, a_b_i, a_lambda,
         kv_norm2, b_norm_full, final_norm2, late_shards=late_shards)

    def lru_shards(dw):
        return dw.reshape(nb, N_DEV, rows_r, bw).transpose(1, 0, 2, 3).reshape(N_DEV, nb * rows_r, bw)

    full = [dw_a_in, dw_a_out.reshape(N_DEV, a_w_out.shape[1], D), lru_shards(dw_r), lru_shards(dw_i)]
    small_parts = [d_a_norm, d_conv_w, d_conv_b, d_b_r, d_b_i, d_lambda, d_kv_norm, d_b_norm, d_final_norm]
    small_sizes = [p.size // LANES for p in small_parts]
    small = jnp.concatenate([_rows(p) for p in small_parts], axis=0)
    outs = _grads_to_sibling(full, small)
    got, small_everyone = outs[:len(full)], outs[-1]
    sums = [_pair_sum(f_, g_, name=f"pair_sum_{i}") for i, (f_, g_) in enumerate(zip(full, got))]
    others = _grads_to_chips([s[1] for s in sums])
    sums = sums[:2] + list(early_sums) + sums[2:]
    others = list(others[:2]) + list(early_others) + list(others[2:])

    def shard2d(w):
        return w.reshape(-1, w.shape[-1])

    names_big = [(a_w_in, m_a_w_in, v_a_w_in), (a_w_out, m_a_w_out, v_a_w_out), (w_kv, m_w_kv, v_w_kv),
                 (b_w_in, m_b_w_in, v_b_w_in), (b_w_out, m_b_w_out, v_b_w_out),
                 (a_w_r, m_a_w_r, v_a_w_r), (a_w_i, m_a_w_i, v_a_w_i)]
    upd_big = []
    for i, (w, m, v) in enumerate(names_big):
        res = _adamw(shard2d(w), shard2d(m), shard2d(v), [others[i]], chip_sums=sums[i][0], name=f"adamw_{i}")
        upd_big.append([r.reshape(w.shape) for r in res])

    soffs = [0]
    for s in small_sizes:
        soffs.append(soffs[-1] + s)

    def small_piece(i):
        return small_everyone[:, soffs[i]:soffs[i + 1], :]

    cw_cols = a_conv_w.shape[2]
    conv_piece = small_piece(1).reshape(N_DEV, CONV_W, C)
    conv_piece = lax.dynamic_slice_in_dim(conv_piece, me * cw_cols, cw_cols, axis=2)
    conv_piece = conv_piece.reshape(N_DEV, CONV_W * cw_cols // LANES, LANES)
    bn_piece = lax.dynamic_slice_in_dim(small_piece(7), me * bn_rows, bn_rows, axis=1)
    small_g = jnp.concatenate([small_piece(0), conv_piece, small_piece(2), small_piece(3), small_piece(4),
                               small_piece(5), small_piece(6), bn_piece, small_piece(8)], axis=1)
    small_w = [(a_norm, m_a_norm, v_a_norm), (a_conv_w, m_a_conv_w, v_a_conv_w),
               (a_conv_b, m_a_conv_b, v_a_conv_b), (a_b_r, m_a_b_r, v_a_b_r), (a_b_i, m_a_b_i, v_a_b_i),
               (a_lambda, m_a_lambda, v_a_lambda), (kv_norm, m_kv_norm, v_kv_norm),
               (b_norm, m_b_norm, v_b_norm), (final_norm, m_final_norm, v_final_norm)]
    pack = lambda idx: jnp.concatenate([_rows(t[idx]) for t in small_w], axis=0)
    res_small = _adamw(pack(0), pack(1), pack(2), [small_g], name="adamw_small")
    woffs = [0]
    for t in small_w:
        woffs.append(woffs[-1] + t[0].size // LANES)
    upd_small = [[r[woffs[i]:woffs[i + 1]].reshape(small_w[i][0].shape) for r in res_small]
                 for i in range(len(small_w))]

    order = [("s", 0), ("b", 0), ("s", 1), ("s", 2), ("b", 5), ("s", 3), ("b", 6), ("s", 4), ("s", 5),
             ("b", 1), ("s", 6), ("b", 2), ("s", 7), ("b", 3), ("b", 4), ("s", 8)]
    per_weight = [(upd_big if kind == "b" else upd_small)[i] for kind, i in order]
    loss = lax.psum(loss_part[0, 0], ("x", "y", "c"))
    result = [loss, grad_x[None]]
    for field in range(4):
        result += [u[field] for u in per_weight]
    return tuple(result)
```

```python
import functools
import math

import jax
import jax.numpy as jnp
from jax import lax
from jax.experimental import pallas as pl
from jax.experimental.pallas import tpu as pltpu

F32 = jnp.float32
BF16 = jnp.bfloat16
MESH = pl.DeviceIdType.MESH

EPS = 1e-6
LOG2E = 1.4426950408889634
WEIGHT_FLOOR_LOG2 = -200.0
LRU_C = 8.0
CONV_W = 4
HEAD_DIM = 128
ADAM_LR = 0.001
ADAM_B1 = 0.9
ADAM_B2 = 0.999
ADAM_EPS = 1e-08
ADAM_WD = 0.01
ADAM_STEP = 10

N_DEV = 8
LANES = 128
SUBLANES = 8
VMEM_LIMIT = 56 * 1024 * 1024

ATT_KEY_BLOCK = 256
ATT_QUERY_BLOCK = 256
SCAN_BLOCK = 256
ROW_BLOCK = 256
MM_TOKEN_BLOCK = 512
MM_WEIGHT_TILE = 1280
MM_CONTRACT_TOKENS = 2048
ANY = pl.BlockSpec(memory_space=pl.ANY)


def _pcall(body, **kw):
    return pl.pallas_call(body, **kw)


def _params(*sem):
    return pltpu.CompilerParams(dimension_semantics=sem, vmem_limit_bytes=VMEM_LIMIT)


def _pick(n, cap):
    if n <= cap:
        return n
    best = None
    for t in range(LANES, cap + 1, LANES):
        if n % t == 0:
            best = t
    assert best is not None, (n, cap)
    return best


def _sigmoid(x):
    return 1.0 / (1.0 + jnp.exp(-x))


def _dot(a, b, ca, cb):
    return lax.dot_general(a, b, (((ca,), (cb,)), ((), ())), preferred_element_type=F32)


def _mm_nn(a, w, *, name, out_dtype, col_off=0, cols=None, res=None):
    T, K = a.shape
    K2, N = w.shape
    assert K == K2
    cols = N if cols is None else cols
    tm = min(T, MM_TOKEN_BLOCK)
    tn = _pick(cols, MM_WEIGHT_TILE)
    assert col_off % tn == 0
    off = col_off // tn
    has_res = res is not None

    def body(a_ref, b_ref, *rest):
        o_ref = rest[-1]
        acc = jnp.dot(a_ref[...].astype(BF16), b_ref[...], preferred_element_type=F32)
        if has_res:
            acc = acc + rest[0][...]
        o_ref[...] = acc.astype(out_dtype)

    in_specs = [pl.BlockSpec((tm, K), lambda j, i: (i, 0)),
                pl.BlockSpec((K, tn), lambda j, i: (0, off + j))]
    args = [a, w]
    if has_res:
        in_specs.append(pl.BlockSpec((tm, tn), lambda j, i: (i, j)))
        args.append(res)
    return _pcall(
        body, name=name, grid=(cols // tn, T // tm), in_specs=in_specs,
        out_specs=pl.BlockSpec((tm, tn), lambda j, i: (i, j)),
        out_shape=jax.ShapeDtypeStruct((T, cols), out_dtype),
        compiler_params=_params("parallel", "parallel"))(*args)


def _mm_nt(a, w, *, name, out_dtype=F32):
    T, K = a.shape
    N, K2 = w.shape
    assert K == K2
    tm = min(T, MM_TOKEN_BLOCK)
    tn = _pick(N, MM_WEIGHT_TILE)

    def body(a_ref, b_ref, o_ref):
        o_ref[...] = _dot(a_ref[...].astype(BF16), b_ref[...], 1, 1).astype(out_dtype)

    return _pcall(
        body, name=name, grid=(N // tn, T // tm),
        in_specs=[pl.BlockSpec((tm, K), lambda j, i: (i, 0)), pl.BlockSpec((tn, K), lambda j, i: (j, 0))],
        out_specs=pl.BlockSpec((tm, tn), lambda j, i: (i, j)),
        out_shape=jax.ShapeDtypeStruct((T, N), out_dtype),
        compiler_params=_params("parallel", "parallel"))(a, w)


def _mm_tn(a, b, *, name, shards=1):
    T, Ko = a.shape
    T2, N = b.shape
    assert T == T2
    n = N // shards
    tt = min(T, MM_CONTRACT_TOKENS)
    tko = _pick(Ko, 1024)
    tn = _pick(n, 1024)
    per = n // tn

    def body(a_ref, b_ref, o_ref):
        t = pl.program_id(2)
        p = _dot(a_ref[...].astype(BF16), b_ref[...].astype(BF16), 0, 0)

        @pl.when(t == 0)
        def _():
            o_ref[...] = p

        @pl.when(t > 0)
        def _():
            o_ref[...] += p

    if shards == 1:
        out_spec = pl.BlockSpec((tko, tn), lambda i, j, t: (i, j))
        out_shape = jax.ShapeDtypeStruct((Ko, N), F32)
    else:
        out_spec = pl.BlockSpec((None, tko, tn), lambda i, j, t: (j // per, i, j % per))
        out_shape = jax.ShapeDtypeStruct((shards, Ko, n), F32)
    return _pcall(
        body, name=name, grid=(Ko // tko, N // tn, T // tt),
        in_specs=[pl.BlockSpec((tt, tko), lambda i, j, t: (t, i)),
                  pl.BlockSpec((tt, tn), lambda i, j, t: (t, j))],
        out_specs=out_spec, out_shape=out_shape,
        compiler_params=_params("parallel", "parallel", "arbitrary"))(a, b)


def _rms_fwd(x, gains, *, name):
    T, D = x.shape
    tm = min(T, ROW_BLOCK)
    n = len(gains)

    def body(x_ref, *refs):
        xv = x_ref[...]
        xh = xv * lax.rsqrt(jnp.mean(xv * xv, axis=-1, keepdims=True) + EPS)
        for g_ref, o_ref in zip(refs[:n], refs[n:]):
            o_ref[...] = (xh * g_ref[...]).astype(BF16)

    row = pl.BlockSpec((tm, D), lambda i: (i, 0))
    vec = pl.BlockSpec((1, D), lambda i: (0, 0))
    return _pcall(
        body, name=name, grid=(T // tm,), in_specs=[row] + [vec] * n, out_specs=[row] * n,
        out_shape=[jax.ShapeDtypeStruct((T, D), BF16)] * n,
        compiler_params=_params("parallel"))(x, *gains)


def _rms_bwd(x, dres, dhs, gains, *, name, rider=None):
    T, D = x.shape
    tm = min(T, ROW_BLOCK)
    steps = T // tm
    n = len(gains)
    rider = rider or _Rider([], [], [], None)
    nri, nro = len(rider.inputs), len(rider.out_shapes)

    def body(x_ref, dres_ref, *refs):
        dh_refs, g_refs = refs[:n], refs[n:2 * n]
        refs = refs[2 * n:]
        rider_in, refs = refs[:nri], refs[nri:]
        dx_ref, dg_refs = refs[0], refs[1:1 + n]
        rider_out, sems = refs[1 + n:1 + n + nro], refs[1 + n + nro:]
        i = pl.program_id(0)
        if nro:
            start, finish = rider.bind(rider_in, rider_out, sems)
            pl.when(i == 0)(start)
        xv = x_ref[...]
        r = lax.rsqrt(jnp.mean(xv * xv, axis=-1, keepdims=True) + EPS)
        xh = xv * r
        dxh = jnp.zeros_like(xv)
        for dh_ref, g_ref, dg_ref in zip(dh_refs, g_refs, dg_refs):
            dh = dh_ref[...]
            part = jnp.sum(dh * xh, axis=0, keepdims=True)

            @pl.when(i == 0)
            def _():
                dg_ref[...] = part

            @pl.when(i > 0)
            def _():
                dg_ref[...] += part

            dxh = dxh + dh * g_ref[...]
        dx_ref[...] = dres_ref[...] + r * (dxh - xh * jnp.mean(dxh * xh, axis=-1, keepdims=True))
        if nro:
            pl.when(i == steps - 1)(finish)

    row = pl.BlockSpec((tm, D), lambda i: (i, 0))
    vec = pl.BlockSpec((1, D), lambda i: (0, 0))
    return _pcall(
        body, name=name, grid=(steps,), in_specs=[row, row] + [row] * n + [vec] * n + [ANY] * nri,
        out_specs=[row] + [vec] * n + [ANY] * nro,
        out_shape=[jax.ShapeDtypeStruct((T, D), F32)] + [jax.ShapeDtypeStruct((1, D), F32)] * n
                  + rider.out_shapes,
        scratch_shapes=rider.scratch,
        compiler_params=_params("arbitrary"))(x, dres, *dhs, *gains, *rider.inputs)


def _final_loss(x, target, gain, *, name):
    T, D = x.shape
    tm = min(T, ROW_BLOCK)

    def body(x_ref, t_ref, g_ref, dx_ref, dg_ref, loss_ref):
        i = pl.program_id(0)
        xv = x_ref[...]
        g = g_ref[...]
        r = lax.rsqrt(jnp.mean(xv * xv, axis=-1, keepdims=True) + EPS)
        xh = xv * r
        err = xh * g - t_ref[...]
        part_loss = 0.5 * jnp.sum(jnp.mean(err * err, axis=-1, keepdims=True), axis=0, keepdims=True)
        dy = err * (1.0 / D)
        part_g = jnp.sum(dy * xh, axis=0, keepdims=True)

        @pl.when(i == 0)
        def _():
            dg_ref[...] = part_g
            loss_ref[...] = jnp.broadcast_to(part_loss, loss_ref.shape)

        @pl.when(i > 0)
        def _():
            dg_ref[...] += part_g
            loss_ref[...] += jnp.broadcast_to(part_loss, loss_ref.shape)

        dxh = dy * g
        dx_ref[...] = r * (dxh - xh * jnp.mean(dxh * xh, axis=-1, keepdims=True))

    row = pl.BlockSpec((tm, D), lambda i: (i, 0))
    vec = pl.BlockSpec((1, D), lambda i: (0, 0))
    return _pcall(
        body, name=name, grid=(T // tm,), in_specs=[row, row, vec],
        out_specs=[row, vec, pl.BlockSpec((1, LANES), lambda i: (0, 0))],
        out_shape=[jax.ShapeDtypeStruct((T, D), F32), jax.ShapeDtypeStruct((1, D), F32),
                   jax.ShapeDtypeStruct((1, LANES), F32)],
        compiler_params=_params("arbitrary"))(x, target, gain)


def _shift_down(x, prev_tail, j, row):
    tb = x.shape[0]
    prev = jnp.tile(prev_tail, (tb // SUBLANES, 1))
    return jnp.where(row >= j, pltpu.roll(x, j, 0), pltpu.roll(prev, j, 0))


def _shift_up(x, next_head, j, row):
    tb = x.shape[0]
    nxt = jnp.tile(next_head, (tb // SUBLANES, 1))
    return jnp.where(row < tb - j, pltpu.roll(x, tb - j, 0), pltpu.roll(nxt, tb - j, 0))


def _lru_gates(xb, wr, wi, br, bi, lam):
    xbb = xb.astype(BF16)
    r = _sigmoid(jnp.dot(xbb, wr, preferred_element_type=F32) + br)
    i = _sigmoid(jnp.dot(xbb, wi, preferred_element_type=F32) + bi)
    sp = jnp.maximum(-lam, 0.0) + jnp.log1p(jnp.exp(-jnp.abs(lam)))
    log_a = (-LRU_C) * r * sp
    a = jnp.exp(log_a)
    a2 = a * a
    mult = jnp.sqrt(jnp.maximum(-jnp.tanh(log_a) * (1.0 + a2), 0.0))
    return xbb, r, i, sp, a, a2, mult


def _acore_fwd(proj, conv_w, conv_b, w_r, w_i, b_r, b_i, lam, *, name, riders=()):
    T, C2 = proj.shape
    C = C2 // 2
    nb, bw, _ = w_r.shape
    tb = min(T, SCAN_BLOCK)
    nt = T // tb
    nr = len(riders)

    def body(xp_ref, gate_ref, cw_ref, cb_ref, wr_ref, wi_ref, br_ref, bi_ref, lam_ref, *refs):
        rider_in, refs = refs[:nr], refs[nr:]
        xb_ref, h_ref, yg_ref = refs[:3]
        rider_out, refs = refs[3:3 + nr], refs[3 + nr:]
        tail_ref, hlast_ref = refs[:2]
        t = pl.program_id(1)
        if nr:
            gather = _Gather(rider_in, rider_out, *refs[2:])
            pl.when((pl.program_id(0) == 0) & (t == 0))(gather.start)

        @pl.when(t == 0)
        def _():
            tail_ref[...] = jnp.zeros_like(tail_ref)
            hlast_ref[...] = jnp.zeros_like(hlast_ref)

        row = lax.broadcasted_iota(jnp.int32, (tb, bw), 0)
        xp = xp_ref[...]
        tail = tail_ref[...]
        xb = cb_ref[...] + cw_ref[CONV_W - 1:CONV_W, :] * xp
        for j in range(1, CONV_W):
            xb = xb + cw_ref[CONV_W - 1 - j:CONV_W - j, :] * _shift_down(xp, tail, j, row)
        tail_ref[...] = xp[tb - SUBLANES:, :]
        xb_ref[...] = xb

        _, r, i, sp, a, a2, mult = _lru_gates(xb, wr_ref[...], wi_ref[...], br_ref[...], bi_ref[...],
                                              lam_ref[...])
        ca, cb = a, mult * (i * xb)
        s = 1
        while s < tb:
            m = row >= s
            cb = jnp.where(m, ca * pltpu.roll(cb, s, 0) + cb, cb)
            ca = jnp.where(m, ca * pltpu.roll(ca, s, 0), ca)
            s *= 2
        h = cb + ca * hlast_ref[SUBLANES - 1:SUBLANES, :]
        hlast_ref[...] = h[tb - SUBLANES:, :]
        h_ref[...] = h
        gate = gate_ref[...]
        yg_ref[...] = (h * (gate * _sigmoid(gate))).astype(BF16)
        if nr:
            pl.when((pl.program_id(0) == nb - 1) & (t == nt - 1))(gather.finish)

    blk = lambda off: pl.BlockSpec((tb, bw), lambda n, t: (t, off + n))
    vec = pl.BlockSpec((1, bw), lambda n, t: (0, n))
    wspec = pl.BlockSpec((None, bw, bw), lambda n, t: (n, 0, 0))
    return _pcall(
        body, name=name, grid=(nb, nt),
        in_specs=[blk(0), blk(nb), pl.BlockSpec((CONV_W, bw), lambda n, t: (0, n)), vec, wspec, wspec,
                  vec, vec, vec] + [ANY] * nr,
        out_specs=[blk(0), blk(0), blk(0)] + [ANY] * nr,
        out_shape=[jax.ShapeDtypeStruct((T, C), F32), jax.ShapeDtypeStruct((T, C), F32),
                   jax.ShapeDtypeStruct((T, C), BF16)]
                  + [jax.ShapeDtypeStruct((N_DEV,) + r.shape, r.dtype) for r in riders],
        scratch_shapes=[pltpu.VMEM((SUBLANES, bw), F32), pltpu.VMEM((SUBLANES, bw), F32)]
                       + (_Gather.scratch(nr) if nr else []),
        compiler_params=_params("arbitrary" if nr else "parallel", "arbitrary"))(
            proj, proj, conv_w, conv_b, w_r, w_i, b_r, b_i, lam, *riders)


def _acore_bwd(dyg, proj, xb_all, h_all, conv_w, w_r, w_i, b_r, b_i, lam, *, name, rider=None):
    T, C2 = proj.shape
    C = C2 // 2
    nb, bw, _ = w_r.shape
    tb = min(T, SCAN_BLOCK)
    nt = T // tb
    per8 = tb // SUBLANES
    rider = rider or _Rider([], [], [], None)
    nri, nro = len(rider.inputs), len(rider.out_shapes)

    def body(dyg_ref, xp_ref, gate_ref, xb_ref, h_ref, xp_prev_ref, h_prev_ref, cw_ref,
             wr_ref, wi_ref, br_ref, bi_ref, lam_ref, *refs):
        rider_in, refs = refs[:nri], refs[nri:]
        dxp_ref, dgate_ref, dcw_ref, dcb_ref, dbr_ref, dbi_ref, dlam_ref, dwr_ref, dwi_ref = refs[:9]
        rider_out, refs = refs[9:9 + nro], refs[9 + nro:]
        gh_next_ref, a_next_ref, dxb_next_ref = refs[:3]
        step = pl.program_id(1)
        first_block = step == nt - 1
        if nro:
            start, finish = rider.bind(rider_in, rider_out, refs[3:])
            pl.when((pl.program_id(0) == 0) & (step == 0))(start)

        @pl.when(step == 0)
        def _():
            gh_next_ref[...] = jnp.zeros_like(gh_next_ref)
            a_next_ref[...] = jnp.zeros_like(a_next_ref)
            dxb_next_ref[...] = jnp.zeros_like(dxb_next_ref)

        row = lax.broadcasted_iota(jnp.int32, (tb, bw), 0)
        keep = jnp.where(first_block, 0.0, 1.0)
        h_prev = h_prev_ref[...] * keep
        xp_prev = xp_prev_ref[...] * keep
        xp, gate, xb, h, dyg_v = xp_ref[...], gate_ref[...], xb_ref[...], h_ref[...], dyg_ref[...]
        lam_v = lam_ref[...]
        wr, wi = wr_ref[...], wi_ref[...]

        sg = _sigmoid(gate)
        dh = dyg_v * (gate * sg)
        dgate_ref[...] = (dyg_v * h * (sg * (1.0 + gate * (1.0 - sg)))).astype(BF16)

        xbb, r, i, sp, a, a2, mult = _lru_gates(xb, wr, wi, br_ref[...], bi_ref[...], lam_v)

        cg = dh
        cc = _shift_up(a, a_next_ref[...], 1, row)
        s = 1
        while s < tb:
            m = row < tb - s
            cg = jnp.where(m, cc * pltpu.roll(cg, tb - s, 0) + cg, cg)
            cc = jnp.where(m, cc * pltpu.roll(cc, tb - s, 0), cc)
            s *= 2
        gh = cg + cc * gh_next_ref[0:1, :]
        gh_next_ref[...] = gh[0:SUBLANES, :]
        a_next_ref[...] = a[0:SUBLANES, :]

        da = gh * _shift_down(h, h_prev, 1, row)
        dmult = gh * (i * xb)
        di = gh * mult * xb
        dxb = gh * mult * i
        dla = da * a - dmult * jnp.where(mult > 0.0, a2 / mult, 0.0)
        dr = dla * ((-LRU_C) * sp)
        dsp = jnp.sum(dla * ((-LRU_C) * r), axis=0, keepdims=True)
        dlam_part = dsp * (-_sigmoid(-lam_v))
        dpr = dr * r * (1.0 - r)
        dpi = di * i * (1.0 - i)
        dbr_part = jnp.sum(dpr, axis=0, keepdims=True)
        dbi_part = jnp.sum(dpi, axis=0, keepdims=True)
        dprb, dpib = dpr.astype(BF16), dpi.astype(BF16)
        dwr_part = _dot(xbb, dprb, 0, 0)
        dwi_part = _dot(xbb, dpib, 0, 0)
        dxb = dxb + _dot(dprb, wr, 1, 1) + _dot(dpib, wi, 1, 1)

        dxb_next = dxb_next_ref[...]
        dxp = cw_ref[CONV_W - 1:CONV_W, :] * dxb
        for j in range(1, CONV_W):
            dxp = dxp + cw_ref[CONV_W - 1 - j:CONV_W - j, :] * _shift_up(dxb, dxb_next, j, row)
        dxb_next_ref[...] = dxb[0:SUBLANES, :]
        dxp_ref[...] = dxp.astype(BF16)
        dcb_part = jnp.sum(dxb, axis=0, keepdims=True)
        dcw_rows = []
        for k in range(CONV_W):
            j = CONV_W - 1 - k
            sh = xp if j == 0 else _shift_down(xp, xp_prev, j, row)
            dcw_rows.append(jnp.sum(dxb * sh, axis=0, keepdims=True))

        @pl.when(step == 0)
        def _():
            for k in range(CONV_W):
                dcw_ref[k:k + 1, :] = dcw_rows[k]
            dcb_ref[...] = dcb_part
            dbr_ref[...] = dbr_part
            dbi_ref[...] = dbi_part
            dlam_ref[...] = dlam_part
            dwr_ref[...] = dwr_part
            dwi_ref[...] = dwi_part

        @pl.when(step > 0)
        def _():
            for k in range(CONV_W):
                dcw_ref[k:k + 1, :] += dcw_rows[k]
            dcb_ref[...] += dcb_part
            dbr_ref[...] += dbr_part
            dbi_ref[...] += dbi_part
            dlam_ref[...] += dlam_part
            dwr_ref[...] += dwr_part
            dwi_ref[...] += dwi_part

        if nro:
            pl.when((pl.program_id(0) == nb - 1) & (step == nt - 1))(finish)

    rev = lambda s: nt - 1 - s
    blk = lambda off: pl.BlockSpec((tb, bw), lambda n, s: (rev(s), off + n))
    prev8 = lambda off: pl.BlockSpec(
        (SUBLANES, bw), lambda n, s: (jnp.maximum(rev(s) * per8 - 1, 0), off + n))
    vec = pl.BlockSpec((1, bw), lambda n, s: (0, n))
    wspec = pl.BlockSpec((None, bw, bw), lambda n, s: (n, 0, 0))
    cwspec = pl.BlockSpec((CONV_W, bw), lambda n, s: (0, n))
    vshape = jax.ShapeDtypeStruct((1, C), F32)
    wshape = jax.ShapeDtypeStruct((nb, bw, bw), F32)
    return _pcall(
        body, name=name, grid=(nb, nt),
        in_specs=[blk(0), blk(0), blk(nb), blk(0), blk(0), prev8(0), prev8(0), cwspec,
                  wspec, wspec, vec, vec, vec] + [ANY] * nri,
        out_specs=[blk(0), blk(0), cwspec, vec, vec, vec, vec, wspec, wspec] + [ANY] * nro,
        out_shape=[jax.ShapeDtypeStruct((T, C), BF16), jax.ShapeDtypeStruct((T, C), BF16),
                   jax.ShapeDtypeStruct((CONV_W, C), F32), vshape, vshape, vshape, vshape,
                   wshape, wshape] + rider.out_shapes,
        scratch_shapes=[pltpu.VMEM((SUBLANES, bw), F32)] * 3 + rider.scratch,
        compiler_params=_params("arbitrary" if nro else "parallel", "arbitrary"))(
            dyg, proj, proj, xb_all, h_all, proj, h_all, conv_w, w_r, w_i, b_r, b_i, lam, *rider.inputs)


def _later_sum(lk, tri):
    return jnp.dot(lk.astype(BF16), tri, preferred_element_type=F32)


def _log2_sigmoids(y):
    t = jnp.log(1.0 + jnp.exp2(-jnp.abs(y))) * LOG2E
    ls = jnp.minimum(y, 0.0) - t
    return ls, ls - y


def _attn_blocks(T):
    bk = min(T, ATT_KEY_BLOCK)
    bq = min(T, ATT_QUERY_BLOCK)
    return bq, bk, bq // bk


def _attn_fwd(q, kv, gate, *, name):
    T, HD = q.shape
    H = HD // HEAD_DIM
    bq, bk, per = _attn_blocks(T)
    scale = 1.0 / math.sqrt(HEAD_DIM)

    def body(q_ref, k_ref, v_ref, g_ref, o_ref, og_ref, lt_ref, w_ref):
        i = pl.program_id(1)
        qv = q_ref[...]
        tr = lax.broadcasted_iota(jnp.int32, (bk, bk), 0)
        tc = lax.broadcasted_iota(jnp.int32, (bk, bk), 1)
        tri = (tr > tc).astype(BF16)
        ahead = (lax.broadcasted_iota(jnp.int32, (bq, bk), 0)
                 - lax.broadcasted_iota(jnp.int32, (bq, bk), 1))

        def starts_of(top):
            return [pl.multiple_of((top - d) * bk, bk) for d in range(per)]

        def scores(top):
            return [_dot(qv, k_ref[pl.ds(ks, bk), :], 1, 1) for ks in starts_of(top)]

        def weights(top, zs, c, mask):
            lss, sums, css, causals = [], [], [], []
            for ks, z in zip(starts_of(top), zs):
                ls, lk = _log2_sigmoids(z * (scale * LOG2E))
                if mask:
                    causals.append(ahead > ks - i * bq)
                    lk = jnp.where(causals[-1], lk, 0.0)
                lss.append(ls)
                sums.append(jnp.sum(lk, axis=1, keepdims=True))
                css.append(_later_sum(lk, tri))
            for d in range(per):
                w = jnp.exp2(lss[d] + (css[d] + c))
                if mask:
                    w = jnp.where(causals[d], w, 0.0)
                w_ref[d] = w.astype(BF16)
                c = c + sums[d]
            return c

        def values(top, acc):
            for d, ks in enumerate(starts_of(top)):
                acc = acc + jnp.dot(w_ref[d], v_ref[pl.ds(ks, bk), :], preferred_element_type=F32)
            return acc

        def more(state):
            gg, _, _, largest = state
            return (gg <= i) & (largest > WEIGHT_FLOOR_LOG2)

        def step(state):
            gg, acc, c, _ = state
            top = (i - gg) * per + per - 1
            zs = scores(top)
            acc = values(top + per, acc)
            c = weights(top, zs, c, False)
            return gg + 1, acc, c, jnp.max(c)

        diag_top = i * per + per - 1
        c = weights(diag_top, scores(diag_top), jnp.zeros((bq, 1), F32), True)
        gg, acc, c, _ = lax.while_loop(more, step, (1, jnp.zeros((bq, HEAD_DIM), F32), c, jnp.max(c)))
        acc = values((i - gg + 1) * per + per - 1, acc)
        o_ref[...] = acc
        g = g_ref[...]
        og_ref[...] = (acc * (g * _sigmoid(g))).astype(BF16)
        lane = lax.broadcasted_iota(jnp.int32, (bq, HEAD_DIM), 1)
        lt_ref[...] = jnp.where(lane == 1, (i - gg + 1).astype(F32), jnp.broadcast_to(c, (bq, HEAD_DIM)))

    qspec = pl.BlockSpec((bq, HEAD_DIM), lambda h, i: (i, h))
    return _pcall(
        body, name=name, grid=(H, T // bq),
        in_specs=[qspec, pl.BlockSpec((T, HEAD_DIM), lambda h, i: (0, h)),
                  pl.BlockSpec((T, HEAD_DIM), lambda h, i: (0, H + h)), qspec],
        out_specs=[qspec, qspec, qspec],
        out_shape=[jax.ShapeDtypeStruct((T, HD), F32), jax.ShapeDtypeStruct((T, HD), BF16),
                   jax.ShapeDtypeStruct((T, HD), F32)],
        scratch_shapes=[pltpu.VMEM((per, bq, bk), BF16)],
        compiler_params=_params("parallel", "arbitrary"))(q, kv, kv, gate)


def _attn_bwd(q, kv, gate, o, ltot, dog, *, name):
    T, HD = q.shape
    H = HD // HEAD_DIM
    bq, bk, per = _attn_blocks(T)
    nq = T // bq
    scale = 1.0 / math.sqrt(HEAD_DIM)

    def body(q_ref, k_ref, v_ref, g_ref, o_ref, lt_ref, dog_ref,
             dq_ref, dg_ref, dk_ref, dv_ref, dk_acc, dv_acc, dz_ref, w_ref):
        i = pl.program_id(1)

        @pl.when(i == 0)
        def _():
            dk_acc[...] = jnp.zeros_like(dk_acc)
            dv_acc[...] = jnp.zeros_like(dv_acc)

        qv = q_ref[...]
        g, ov, dogv = g_ref[...], o_ref[...], dog_ref[...]
        sg = _sigmoid(g)
        do = dogv * (g * sg)
        dg_ref[...] = (dogv * ov * (sg * (1.0 + g * (1.0 - sg)))).astype(BF16)
        dob = do.astype(BF16)
        ltot_v = lt_ref[:, 0:1]
        tr = lax.broadcasted_iota(jnp.int32, (bk, bk), 0)
        tc = lax.broadcasted_iota(jnp.int32, (bk, bk), 1)
        tri_later = (tr > tc).astype(BF16)
        tri_excl = (tr < tc).astype(BF16)
        ahead = (lax.broadcasted_iota(jnp.int32, (bq, bk), 0)
                 - lax.broadcasted_iota(jnp.int32, (bq, bk), 1))

        def starts_of(first):
            return [pl.multiple_of((first + d) * bk, bk) for d in range(per)]

        def scores(first):
            return ([_dot(qv, k_ref[pl.ds(ks, bk), :], 1, 1) for ks in starts_of(first)],
                    [_dot(dob, v_ref[pl.ds(ks, bk), :], 1, 1) for ks in starts_of(first)])

        def front(first, zs, dws, p_lk, p_g, mask):
            lss, css, causals = [], [], []
            for ks, z in zip(starts_of(first), zs):
                ls, lk = _log2_sigmoids(z * (scale * LOG2E))
                if mask:
                    causals.append(ahead > ks - i * bq)
                    lk = jnp.where(causals[-1], lk, 0.0)
                lss.append(ls)
                p_lk = p_lk + jnp.sum(lk, axis=1, keepdims=True)
                css.append((ltot_v - p_lk) + _later_sum(lk, tri_later))
            gms, befores = [], []
            for d in range(per):
                w = jnp.exp2(lss[d] + css[d])
                if mask:
                    w = jnp.where(causals[d], w, 0.0)
                gm = dws[d] * w
                gms.append(gm)
                w_ref[d] = w.astype(BF16)
                befores.append(jnp.dot(gm.astype(BF16), tri_excl, preferred_element_type=F32) + p_g)
                p_g = p_g + jnp.sum(gm, axis=1, keepdims=True)
            for d in range(per):
                dz = gms[d] - jnp.exp2(lss[d]) * (gms[d] + befores[d])
                if mask:
                    dz = jnp.where(causals[d], dz, 0.0)
                dz_ref[d] = (dz * scale).astype(BF16)
            return p_lk, p_g

        def back(first, dq):
            for d, ks in enumerate(starts_of(first)):
                dzb = dz_ref[d]
                dq = dq + jnp.dot(dzb, k_ref[pl.ds(ks, bk), :], preferred_element_type=F32)
                dk_acc[pl.ds(ks, bk), :] += _dot(dzb, qv, 0, 0)
                dv_acc[pl.ds(ks, bk), :] += _dot(w_ref[d], dob, 0, 0)
            return dq

        def step(mask):
            def trip(g, state):
                dq, p_lk, p_g = state
                zs, dws = scores(g * per)
                dq = back((g - 1) * per, dq)
                return (dq,) + front(g * per, zs, dws, p_lk, p_g, mask)
            return trip

        g0 = jnp.max(lt_ref[0:1, 1:2]).astype(jnp.int32)
        zero = jnp.zeros((bq, 1), F32)
        state = (jnp.zeros((bq, HEAD_DIM), F32),) + front(g0 * per, *scores(g0 * per), zero, zero, True)
        state = lax.fori_loop(g0 + 1, i, step(False), state)
        state = lax.fori_loop(jnp.maximum(i, g0 + 1), i + 1, step(True), state)
        dq_ref[...] = back(i * per, state[0]).astype(BF16)

        @pl.when(i == nq - 1)
        def _():
            dk_ref[...] = dk_acc[...].astype(BF16)
            dv_ref[...] = dv_acc[...].astype(BF16)

    qspec = pl.BlockSpec((bq, HEAD_DIM), lambda h, i: (i, h))
    kspec = pl.BlockSpec((T, HEAD_DIM), lambda h, i: (0, h))
    return _pcall(
        body, name=name, grid=(H, nq),
        in_specs=[qspec, kspec, pl.BlockSpec((T, HEAD_DIM), lambda h, i: (0, H + h)),
                  qspec, qspec, qspec, qspec],
        out_specs=[qspec, qspec, kspec, kspec],
        out_shape=[jax.ShapeDtypeStruct((T, HD), BF16)] * 4,
        scratch_shapes=[pltpu.VMEM((T, HEAD_DIM), F32)] * 2 + [pltpu.VMEM((per, bq, bk), BF16)] * 2,
        compiler_params=_params("parallel", "arbitrary"))(q, kv, kv, gate, o, ltot, dog)


def _position():
    return lax.axis_index("x"), lax.axis_index("y"), lax.axis_index("c")


def _chip_of(k, x, y):
    return (1 - x if k & 1 else x), (1 - y if k & 2 else y)


class _Gather:
    @staticmethod
    def scratch(n):
        return [pltpu.SemaphoreType.DMA((n, 7)), pltpu.SemaphoreType.DMA((n, 7)),
                pltpu.SemaphoreType.DMA((n,))]

    def __init__(self, ins, outs, send_sems, recv_sems, local_sems):
        self.ins, self.outs, self.n = ins, outs, len(ins)
        self.send_sems, self.recv_sems, self.local_sems = send_sems, recv_sems, local_sems
        x, y, c = _position()
        self.me, self.sibling = (x, y, c), (x, y, 1 - c)
        self.chips = [_chip_of(k, x, y) for k in (1, 2, 3)]

    def copy(self, a, k, block, to, src=None):
        slot = self.outs[a].at[4 * block[0] + 2 * block[1] + block[2]]
        return pltpu.make_async_remote_copy(
            src_ref=slot if src is None else src, dst_ref=slot,
            send_sem=self.send_sems.at[a, k], recv_sem=self.recv_sems.at[a, k],
            device_id=to, device_id_type=MESH)

    def own_copies(self):
        x, y, c = self.me
        mine = [pltpu.make_async_copy(self.ins[a], self.outs[a].at[4 * x + 2 * y + c], self.local_sems.at[a])
                for a in range(self.n)]
        first = []
        for a in range(self.n):
            first.append(self.copy(a, 0, self.me, self.sibling, src=self.ins[a]))
            first += [self.copy(a, 1 + j, self.me, (*chip, c), src=self.ins[a])
                      for j, chip in enumerate(self.chips)]
        return mine, first

    def start(self):
        mine, first = self.own_copies()
        for cp in mine + first:
            cp.start()

    def finish(self):
        c = self.me[2]
        mine, first = self.own_copies()
        passed = []
        for j, chip in enumerate(self.chips):
            for a in range(self.n):
                self.copy(a, 1 + j, (*chip, c), self.me).wait_recv()
                fwd = self.copy(a, 4 + j, (*chip, c), self.sibling)
                fwd.start()
                passed.append(fwd)
        for a in range(self.n):
            self.copy(a, 0, self.sibling, self.me).wait_recv()
            for j, chip in enumerate(self.chips):
                self.copy(a, 4 + j, (*chip, 1 - c), self.me).wait_recv()
        for cp in first + passed:
            cp.wait_send()
        for cp in mine:
            cp.wait()


def _weights_gather(shards):
    n = len(shards)

    def body(*refs):
        gather = _Gather(refs[:n], refs[n:2 * n], *refs[2 * n:])
        gather.start()
        gather.finish()

    return _pcall(
        body, name="weights_gather", in_specs=[ANY] * n, out_specs=[ANY] * n,
        out_shape=[jax.ShapeDtypeStruct((N_DEV,) + s.shape, s.dtype) for s in shards],
        scratch_shapes=_Gather.scratch(n))(*shards)


class _Rider:
    def __init__(self, inputs, out_shapes, scratch, copies):
        self.inputs, self.out_shapes, self.scratch, self.copies = inputs, out_shapes, scratch, copies

    def bind(self, ins, outs, sems):
        def start():
            for cp in self.copies(ins, outs, sems):
                cp.start()

        def finish():
            cps = self.copies(ins, outs, sems)
            for cp in cps:
                cp.wait_send()
            for cp in cps:
                cp.wait_recv()

        return start, finish


def _sibling_rider(grads):
    n = len(grads)

    def copies(ins, outs, sems):
        x, y, c = _position()
        return [pltpu.make_async_remote_copy(
            src_ref=ins[a].at[2 * chip + (1 - c)], dst_ref=outs[a].at[chip],
            send_sem=sems[0].at[a, chip], recv_sem=sems[1].at[a, chip],
            device_id=(x, y, 1 - c), device_id_type=MESH) for a in range(n) for chip in range(4)]

    return _Rider(list(grads), [jax.ShapeDtypeStruct((4,) + g.shape[1:], g.dtype) for g in grads],
                  [pltpu.SemaphoreType.DMA((n, 4)), pltpu.SemaphoreType.DMA((n, 4))], copies)


def _chips_rider(parts):
    n = len(parts)

    def copies(ins, outs, sems):
        x, y, c = _position()
        cps = []
        for a in range(n):
            for k in range(3):
                cx, cy = _chip_of(k + 1, x, y)
                cps.append(pltpu.make_async_remote_copy(
                    src_ref=ins[a].at[2 * cx + cy], dst_ref=outs[a].at[k],
                    send_sem=sems[0].at[a, k], recv_sem=sems[1].at[a, k],
                    device_id=(cx, cy, c), device_id_type=MESH))
        return cps

    return _Rider(list(parts), [jax.ShapeDtypeStruct((3,) + p.shape[1:], p.dtype) for p in parts],
                  [pltpu.SemaphoreType.DMA((n, 3)), pltpu.SemaphoreType.DMA((n, 3))], copies)


def _grads_to_sibling(grads, small):
    n = len(grads)
    rider = _sibling_rider(grads)

    def body(*refs):
        g_refs, small_ref = refs[:n], refs[n]
        got, small_all = refs[n + 1:2 * n + 1], refs[2 * n + 1]
        send_sems, recv_sems, small_send, small_recv, small_local = refs[2 * n + 2:]
        x, y, c = _position()
        me = 4 * x + 2 * y + c
        remote = rider.copies(g_refs, got, (send_sems, recv_sems))
        peers = []
        for m in range(1, N_DEV):
            px, py, pc = x ^ (m >> 2), y ^ ((m >> 1) & 1), c ^ (m & 1)
            peers.append(pltpu.make_async_remote_copy(
                src_ref=small_ref, dst_ref=small_all.at[me],
                send_sem=small_send.at[m - 1], recv_sem=small_recv.at[m - 1],
                device_id=(px, py, pc), device_id_type=MESH))
        own = pltpu.make_async_copy(small_ref, small_all.at[me], small_local)
        for cp in peers + remote + [own]:
            cp.start()
        for cp in peers + remote:
            cp.wait_send()
        for m in range(1, N_DEV):
            px, py, pc = x ^ (m >> 2), y ^ ((m >> 1) & 1), c ^ (m & 1)
            pltpu.make_async_remote_copy(
                src_ref=small_ref, dst_ref=small_all.at[4 * px + 2 * py + pc],
                send_sem=small_send.at[m - 1], recv_sem=small_recv.at[m - 1],
                device_id=(px, py, pc), device_id_type=MESH).wait_recv()
        for cp in remote:
            cp.wait_recv()
        own.wait()

    part = [jax.ShapeDtypeStruct((4,) + g.shape[1:], g.dtype) for g in grads]
    return _pcall(
        body, name="grads_to_sibling", in_specs=[ANY] * (n + 1), out_specs=[ANY] * (n + 1),
        out_shape=part + [jax.ShapeDtypeStruct((N_DEV,) + small.shape, small.dtype)],
        scratch_shapes=[pltpu.SemaphoreType.DMA((n, 4)), pltpu.SemaphoreType.DMA((n, 4)),
                        pltpu.SemaphoreType.DMA((7,)), pltpu.SemaphoreType.DMA((7,)),
                        pltpu.SemaphoreType.DMA])(*grads, small)


def _grads_to_chips(parts):
    n = len(parts)
    rider = _chips_rider(parts)

    def body(*refs):
        start, finish = rider.bind(refs[:n], refs[n:2 * n], refs[2 * n:])
        start()
        finish()

    return _pcall(
        body, name="grads_to_chips", in_specs=[ANY] * n, out_specs=[ANY] * n,
        out_shape=rider.out_shapes, scratch_shapes=rider.scratch)(*parts)


def _pair_sum(grad, got, *, name):
    _, R, C = got.shape
    tr = _pick8(R, max(2 * SUBLANES, (1 << 17) // C))

    def body(g_ref, b_ref, o_ref, ob_ref):
        north = lax.axis_index("c") == 1
        for chip in range(4):
            s = jnp.where(north, g_ref[chip, 1], g_ref[chip, 0]) + b_ref[chip]
            o_ref[chip] = s
            ob_ref[chip] = s.astype(BF16)

    spec = pl.BlockSpec((4, tr, C), lambda i: (0, i, 0))
    return _pcall(
        body, name=name, grid=(R // tr,),
        in_specs=[pl.BlockSpec((4, 2, tr, C), lambda i: (0, 0, i, 0)), spec],
        out_specs=[spec, spec],
        out_shape=[jax.ShapeDtypeStruct((4, R, C), F32), jax.ShapeDtypeStruct((4, R, C), BF16)],
        compiler_params=_params("parallel"))(grad.reshape(4, 2, R, C), got)


def _pick8(n, cap):
    if n <= cap:
        return n
    best = None
    for t in range(SUBLANES, cap + 1, SUBLANES):
        if n % t == 0:
            best = t
    assert best is not None, (n, cap)
    return best


def _adamw(w, m, v, parts, *, name, chip_sums=None):
    R, C = w.shape
    tr = _pick8(R, max(SUBLANES, (1 << 17) // C))
    c1 = 1.0 - ADAM_B1 ** ADAM_STEP
    c2 = 1.0 - ADAM_B2 ** ADAM_STEP
    parts = list(parts) if chip_sums is None else [chip_sums] + list(parts)
    np_ = len(parts)

    def body(w_ref, m_ref, v_ref, *refs):
        p_refs = refs[:np_]
        g_ref, d_ref, nm_ref, nv_ref = refs[np_:]
        g = None
        if chip_sums is not None:
            s_ref, p_refs = p_refs[0], p_refs[1:]
            x1, y1 = lax.axis_index("x") == 1, lax.axis_index("y") == 1
            g = jnp.where(x1, jnp.where(y1, s_ref[3], s_ref[2]), jnp.where(y1, s_ref[1], s_ref[0]))
        for p_ref in p_refs:
            for t in [p_ref[k].astype(F32) for k in range(p_ref.shape[0])]:
                g = t if g is None else g + t
        mn = ADAM_B1 * m_ref[...] + (1.0 - ADAM_B1) * g
        vn = ADAM_B2 * v_ref[...] + (1.0 - ADAM_B2) * (g * g)
        d_ref[...] = -ADAM_LR * ((mn / c1) / (jnp.sqrt(vn / c2) + ADAM_EPS) + ADAM_WD * w_ref[...])
        g_ref[...] = g
        nm_ref[...] = mn
        nv_ref[...] = vn

    spec = pl.BlockSpec((tr, C), lambda i: (i, 0))
    pspecs = [pl.BlockSpec((p.shape[0], tr, C), lambda i: (0, i, 0)) for p in parts]
    return _pcall(
        body, name=name, grid=(R // tr,), in_specs=[spec] * 3 + pspecs, out_specs=[spec] * 4,
        out_shape=[jax.ShapeDtypeStruct((R, C), F32)] * 4,
        compiler_params=_params("parallel"))(w, m, v, *parts)


def _rows(a):
    return a.reshape(-1, LANES)


def _whole_from_columns(shards, *, name):
    S, K, n = shards.shape
    tk = _pick8(K, 1024)

    def body(s_ref, o_ref):
        o_ref[...] = s_ref[...]

    return _pcall(
        body, name=name, grid=(K // tk, S),
        in_specs=[pl.BlockSpec((None, tk, n), lambda i, s: (s, i, 0))],
        out_specs=pl.BlockSpec((tk, n), lambda i, s: (i, s)),
        out_shape=jax.ShapeDtypeStruct((K, S * n), shards.dtype),
        compiler_params=_params("parallel", "parallel"))(shards)


def _late_weights(a_w_out_rows, w_kv_cols, b_w_in_cols, b_w_out_rows):
    whole_rows = lambda g: g.reshape(g.shape[0] * g.shape[1], g.shape[2])
    return (whole_rows(a_w_out_rows), _whole_from_columns(w_kv_cols, name="w_kv_whole"),
            _whole_from_columns(b_w_in_cols, name="b_w_in_whole"), whole_rows(b_w_out_rows))


def _forward_backward(xs, target, a_norm, g_a_w_in, conv_w, conv_b, g_w_r, g_w_i, b_r, b_i, lam,
                      kv_norm, b_norm, final_norm, *, late_weights=None, late_shards=None):
    (h_a,) = _rms_fwd(xs, [a_norm], name="a_norm_fwd")
    proj_a = _mm_nn(h_a, g_a_w_in, name="a_in_proj", out_dtype=F32)
    xb, h_rec, yg, *gathered = _acore_fwd(proj_a, conv_w, conv_b, g_w_r, g_w_i, b_r, b_i, lam,
                                          name="a_core_fwd", riders=late_shards or ())
    g_a_w_out, g_w_kv, g_b_w_in, g_b_w_out = _late_weights(*gathered) if late_shards else late_weights
    x1 = _mm_nn(yg, g_a_w_out, name="a_out_proj", out_dtype=F32, res=xs)
    hk, hb = _rms_fwd(x1, [kv_norm, b_norm], name="kv_b_norm_fwd")
    kv = _mm_nn(hk, g_w_kv, name="kv_proj", out_dtype=BF16)
    hd = g_b_w_in.shape[1] // 2
    q = _mm_nn(hb, g_b_w_in, name="q_proj", out_dtype=BF16, col_off=0, cols=hd)
    gate_b = _mm_nn(hb, g_b_w_in, name="b_gate_proj", out_dtype=F32, col_off=hd, cols=hd)
    o, og, ltot = _attn_fwd(q, kv, gate_b, name="attn_fwd")
    x2 = _mm_nn(og, g_b_w_out, name="b_out_proj", out_dtype=F32, res=x1)
    dx2, d_final_norm, loss_part = _final_loss(x2, target, final_norm, name="final_norm_loss")

    dog = _mm_nt(dx2, g_b_w_out, name="b_out_proj_bwd")
    dw_b_out = _mm_tn(og, dx2, name="b_out_proj_wgrad")
    dq, dgate_b, dk, dv = _attn_bwd(q, kv, gate_b, o, ltot, dog, name="attn_bwd")
    dproj_b = jnp.concatenate([dq, dgate_b], axis=1)
    dkv = jnp.concatenate([dk, dv], axis=1)
    dhb = _mm_nt(dproj_b, g_b_w_in, name="b_in_proj_bwd")
    dw_b_in = _mm_tn(hb, dproj_b, name="b_in_proj_wgrad", shards=N_DEV)
    dhk = _mm_nt(dkv, g_w_kv, name="kv_proj_bwd")
    dw_kv = _mm_tn(hk, dkv, name="kv_proj_wgrad", shards=N_DEV)
    early = [dw_kv, dw_b_in, dw_b_out.reshape(N_DEV, -1, dw_b_out.shape[1])] if late_shards else []
    dx1, d_b_norm, d_kv_norm, *got = _rms_bwd(x1, dx2, [dhb, dhk], [b_norm, kv_norm], name="kv_b_norm_bwd",
                                              rider=_sibling_rider(early) if early else None)
    early_sums = [_pair_sum(f_, g_, name=f"pair_sum_early_{i}") for i, (f_, g_) in enumerate(zip(early, got))]
    dyg = _mm_nt(dx1, g_a_w_out, name="a_out_proj_bwd")
    dw_a_out = _mm_tn(yg, dx1, name="a_out_proj_wgrad")
    (dxp, dgate_a, d_conv_w, d_conv_b, d_b_r, d_b_i, d_lambda, dw_r, dw_i, *early_others) = _acore_bwd(
        dyg, proj_a, xb, h_rec, conv_w, g_w_r, g_w_i, b_r, b_i, lam, name="a_core_bwd",
        rider=_chips_rider([s[1] for s in early_sums]) if early else None)
    dproj_a = jnp.concatenate([dxp, dgate_a], axis=1)
    dh_a = _mm_nt(dproj_a, g_a_w_in, name="a_in_proj_bwd")
    dw_a_in = _mm_tn(h_a, dproj_a, name="a_in_proj_wgrad", shards=N_DEV)
    grad_x, d_a_norm = _rms_bwd(xs, dx1, [dh_a], [a_norm], name="a_norm_bwd")
    return (loss_part, grad_x, dw_a_in, dw_a_out, dw_kv, dw_b_in, dw_b_out, dw_r, dw_i, d_a_norm,
            d_conv_w, d_conv_b, d_b_r, d_b_i, d_lambda, d_kv_norm, d_b_norm, d_final_norm,
            early_sums, early_others)


def kernel(x, a_norm, a_w_in, a_conv_w, a_conv_b, a_w_r, a_b_r, a_w_i, a_b_i, a_lambda, a_w_out, kv_norm, w_kv, b_norm, b_w_in, b_w_out, final_norm, loss_target, m_a_norm, m_a_w_in, m_a_conv_w, m_a_conv_b, m_a_w_r, m_a_b_r, m_a_w_i, m_a_b_i, m_a_lambda, m_a_w_out, m_kv_norm, m_w_kv, m_b_norm, m_b_w_in, m_b_w_out, m_final_norm, v_a_norm, v_a_w_in, v_a_conv_w, v_a_conv_b, v_a_w_r, v_a_b_r, v_a_w_i, v_a_b_i, v_a_lambda, v_a_w_out, v_kv_norm, v_w_kv, v_b_norm, v_b_w_in, v_b_w_out, v_final_norm):
    T, D = x.shape[1], x.shape[2]
    nb, bw = a_w_r.shape[1], a_w_r.shape[3]
    C = nb * bw
    me = 4 * lax.axis_index("x") + 2 * lax.axis_index("y") + lax.axis_index("c")
    xs = x[0]
    target = loss_target[0]

    rows_r = a_w_r.shape[2]
    small_f32 = jnp.concatenate([_rows(a_conv_w[0]), _rows(b_norm[0])], axis=0)
    pad = (-small_f32.shape[0]) % SUBLANES
    small_f32 = jnp.pad(small_f32, ((0, pad), (0, 0)))
    a_w_in_cols, w_r_rows, w_i_rows, small_all = _weights_gather(
        [a_w_in[0].astype(BF16), a_w_r[0].reshape(nb * rows_r, bw).astype(BF16),
         a_w_i[0].reshape(nb * rows_r, bw).astype(BF16), small_f32])
    late_shards = [a_w_out[0].astype(BF16), w_kv.astype(BF16), b_w_in[0].astype(BF16), b_w_out[0].astype(BF16)]
    g_a_w_in = _whole_from_columns(a_w_in_cols, name="a_w_in_whole")
    g_w_r = w_r_rows.reshape(N_DEV, nb, rows_r, bw).transpose(1, 0, 2, 3).reshape(nb, bw, bw)
    g_w_i = w_i_rows.reshape(N_DEV, nb, rows_r, bw).transpose(1, 0, 2, 3).reshape(nb, bw, bw)
    cw_rows = a_conv_w.shape[1] * a_conv_w.shape[2] // LANES
    conv_w_full = small_all[:, :cw_rows, :].reshape(N_DEV, CONV_W, a_conv_w.shape[2])
    conv_w_full = conv_w_full.transpose(1, 0, 2).reshape(CONV_W, C)
    bn_rows = b_norm.shape[1] // LANES
    b_norm_full = small_all[:, cw_rows:cw_rows + bn_rows, :].reshape(1, D)
    kv_norm2, final_norm2 = kv_norm.reshape(1, D), final_norm.reshape(1, D)

    (loss_part, grad_x, dw_a_in, dw_a_out, dw_kv, dw_b_in, dw_b_out, dw_r, dw_i, d_a_norm, d_conv_w,
     d_conv_b, d_b_r, d_b_i, d_lambda, d_kv_norm, d_b_norm, d_final_norm, early_sums,
     early_others) = _forward_backward(
         xs, target, a_norm, g_a_w_in, conv_w_full, a_conv_b, g_w_r, g_w_i, a_b_r, a_b_i, a_lambda,
         kv_norm2, b_norm_full, final_norm2, late_shards=late_shards)

    def lru_shards(dw):
        return dw.reshape(nb, N_DEV, rows_r, bw).transpose(1, 0, 2, 3).reshape(N_DEV, nb * rows_r, bw)

    full = [dw_a_in, dw_a_out.reshape(N_DEV, a_w_out.shape[1], D), lru_shards(dw_r), lru_shards(dw_i)]
    small_parts = [d_a_norm, d_conv_w, d_conv_b, d_b_r, d_b_i, d_lambda, d_kv_norm, d_b_norm, d_final_norm]
    small_sizes = [p.size // LANES for p in small_parts]
    small = jnp.concatenate([_rows(p) for p in small_parts], axis=0)
    outs = _grads_to_sibling(full, small)
    got, small_everyone = outs[:len(full)], outs[-1]
    sums = [_pair_sum(f_, g_, name=f"pair_sum_{i}") for i, (f_, g_) in enumerate(zip(full, got))]
    others = _grads_to_chips([s[1] for s in sums])
    sums = sums[:2] + list(early_sums) + sums[2:]
    others = list(others[:2]) + list(early_others) + list(others[2:])

    def shard2d(w):
        return w.reshape(-1, w.shape[-1])

    names_big = [(a_w_in, m_a_w_in, v_a_w_in), (a_w_out, m_a_w_out, v_a_w_out), (w_kv, m_w_kv, v_w_kv),
                 (b_w_in, m_b_w_in, v_b_w_in), (b_w_out, m_b_w_out, v_b_w_out),
                 (a_w_r, m_a_w_r, v_a_w_r), (a_w_i, m_a_w_i, v_a_w_i)]
    upd_big = []
    for i, (w, m, v) in enumerate(names_big):
        res = _adamw(shard2d(w), shard2d(m), shard2d(v), [others[i]], chip_sums=sums[i][0], name=f"adamw_{i}")
        upd_big.append([r.reshape(w.shape) for r in res])

    soffs = [0]
    for s in small_sizes:
        soffs.append(soffs[-1] + s)

    def small_piece(i):
        return small_everyone[:, soffs[i]:soffs[i + 1], :]

    cw_cols = a_conv_w.shape[2]
    conv_piece = small_piece(1).reshape(N_DEV, CONV_W, C)
    conv_piece = lax.dynamic_slice_in_dim(conv_piece, me * cw_cols, cw_cols, axis=2)
    conv_piece = conv_piece.reshape(N_DEV, CONV_W * cw_cols // LANES, LANES)
    bn_piece = lax.dynamic_slice_in_dim(small_piece(7), me * bn_rows, bn_rows, axis=1)
    small_g = jnp.concatenate([small_piece(0), conv_piece, small_piece(2), small_piece(3), small_piece(4),
                               small_piece(5), small_piece(6), bn_piece, small_piece(8)], axis=1)
    small_w = [(a_norm, m_a_norm, v_a_norm), (a_conv_w, m_a_conv_w, v_a_conv_w),
               (a_conv_b, m_a_conv_b, v_a_conv_b), (a_b_r, m_a_b_r, v_a_b_r), (a_b_i, m_a_b_i, v_a_b_i),
               (a_lambda, m_a_lambda, v_a_lambda), (kv_norm, m_kv_norm, v_kv_norm),
               (b_norm, m_b_norm, v_b_norm), (final_norm, m_final_norm, v_final_norm)]
    pack = lambda idx: jnp.concatenate([_rows(t[idx]) for t in small_w], axis=0)
    res_small = _adamw(pack(0), pack(1), pack(2), [small_g], name="adamw_small")
    woffs = [0]
    for t in small_w:
        woffs.append(woffs[-1] + t[0].size // LANES)
    upd_small = [[r[woffs[i]:woffs[i + 1]].reshape(small_w[i][0].shape) for r in res_small]
                 for i in range(len(small_w))]

    order = [("s", 0), ("b", 0), ("s", 1), ("s", 2), ("b", 5), ("s", 3), ("b", 6), ("s", 4), ("s", 5),
             ("b", 1), ("s", 6), ("b", 2), ("s", 7), ("b", 3), ("b", 4), ("s", 8)]
    per_weight = [(upd_big if kind == "b" else upd_small)[i] for kind, i in order]
    loss = lax.psum(loss_part[0, 0], ("x", "y", "c"))
    result = [loss, grad_x[None]]
    for field in range(4):
        result += [u[field] for u in per_weight]
    return tuple(result)
```

```python
import functools
import math

import jax
import jax.numpy as jnp
from jax import lax
from jax.experimental import pallas as pl
from jax.experimental.pallas import tpu as pltpu

F32 = jnp.float32
BF16 = jnp.bfloat16
MESH = pl.DeviceIdType.MESH

EPS = 1e-6
LOG2E = 1.4426950408889634
WEIGHT_FLOOR_LOG2 = -200.0
LRU_C = 8.0
CONV_W = 4
HEAD_DIM = 128
ADAM_LR = 0.001
ADAM_B1 = 0.9
ADAM_B2 = 0.999
ADAM_EPS = 1e-08
ADAM_WD = 0.01
ADAM_STEP = 10

N_DEV = 8
LANES = 128
SUBLANES = 8
VMEM_LIMIT = 56 * 1024 * 1024

ATT_KEY_BLOCK = 256
ATT_QUERY_BLOCK = 256
SCAN_BLOCK = 256
ROW_BLOCK = 256
MM_TOKEN_BLOCK = 512
MM_WEIGHT_TILE = 1280
MM_CONTRACT_TOKENS = 2048
ANY = pl.BlockSpec(memory_space=pl.ANY)


def _pcall(body, **kw):
    return pl.pallas_call(body, **kw)


def _params(*sem):
    return pltpu.CompilerParams(dimension_semantics=sem, vmem_limit_bytes=VMEM_LIMIT)


def _pick(n, cap):
    if n <= cap:
        return n
    best = None
    for t in range(LANES, cap + 1, LANES):
        if n % t == 0:
            best = t
    assert best is not None, (n, cap)
    return best


def _sigmoid(x):
    return 1.0 / (1.0 + jnp.exp(-x))


def _dot(a, b, ca, cb):
    return lax.dot_general(a, b, (((ca,), (cb,)), ((), ())), preferred_element_type=F32)


def _mm_nn(a, w, *, name, out_dtype, col_off=0, cols=None, res=None):
    T, K = a.shape
    K2, N = w.shape
    assert K == K2
    cols = N if cols is None else cols
    tm = min(T, MM_TOKEN_BLOCK)
    tn = _pick(cols, MM_WEIGHT_TILE)
    assert col_off % tn == 0
    off = col_off // tn
    has_res = res is not None

    def body(a_ref, b_ref, *rest):
        o_ref = rest[-1]
        acc = jnp.dot(a_ref[...].astype(BF16), b_ref[...], preferred_element_type=F32)
        if has_res:
            acc = acc + rest[0][...]
        o_ref[...] = acc.astype(out_dtype)

    in_specs = [pl.BlockSpec((tm, K), lambda j, i: (i, 0)),
                pl.BlockSpec((K, tn), lambda j, i: (0, off + j))]
    args = [a, w]
    if has_res:
        in_specs.append(pl.BlockSpec((tm, tn), lambda j, i: (i, j)))
        args.append(res)
    return _pcall(
        body, name=name, grid=(cols // tn, T // tm), in_specs=in_specs,
        out_specs=pl.BlockSpec((tm, tn), lambda j, i: (i, j)),
        out_shape=jax.ShapeDtypeStruct((T, cols), out_dtype),
        compiler_params=_params("parallel", "parallel"))(*args)


def _mm_nt(a, w, *, name, out_dtype=F32, rider=None):
    T, K = a.shape
    N, K2 = w.shape
    assert K == K2
    tm = min(T, MM_TOKEN_BLOCK)
    tn = _pick(N, MM_WEIGHT_TILE)
    nj, ni = N // tn, T // tm
    rider = rider or _Rider([], [], [], None)
    nri, nro = len(rider.inputs), len(rider.out_shapes)

    def body(a_ref, b_ref, *refs):
        o_ref = refs[nri]
        j, i = pl.program_id(0), pl.program_id(1)
        if nro:
            start, finish = rider.bind(refs[:nri], refs[nri + 1:nri + 1 + nro], refs[nri + 1 + nro:])
            pl.when((j == 0) & (i == 0))(start)
        o_ref[...] = _dot(a_ref[...].astype(BF16), b_ref[...], 1, 1).astype(out_dtype)
        if nro:
            pl.when((j == nj - 1) & (i == ni - 1))(finish)

    sem = ("arbitrary", "arbitrary") if nro else ("parallel", "parallel")
    out = _pcall(
        body, name=name, grid=(nj, ni),
        in_specs=[pl.BlockSpec((tm, K), lambda j, i: (i, 0)), pl.BlockSpec((tn, K), lambda j, i: (j, 0))]
                 + [ANY] * nri,
        out_specs=[pl.BlockSpec((tm, tn), lambda j, i: (i, j))] + [ANY] * nro,
        out_shape=[jax.ShapeDtypeStruct((T, N), out_dtype)] + rider.out_shapes,
        scratch_shapes=rider.scratch,
        compiler_params=_params(*sem))(a, w, *rider.inputs)
    return out if nro else out[0]


def _mm_tn(a, b, *, name, shards=1):
    T, Ko = a.shape
    T2, N = b.shape
    assert T == T2
    n = N // shards
    tt = min(T, MM_CONTRACT_TOKENS)
    tko = _pick(Ko, 1024)
    tn = _pick(n, 1024)
    per = n // tn

    def body(a_ref, b_ref, o_ref):
        t = pl.program_id(2)
        p = _dot(a_ref[...].astype(BF16), b_ref[...].astype(BF16), 0, 0)

        @pl.when(t == 0)
        def _():
            o_ref[...] = p

        @pl.when(t > 0)
        def _():
            o_ref[...] += p

    if shards == 1:
        out_spec = pl.BlockSpec((tko, tn), lambda i, j, t: (i, j))
        out_shape = jax.ShapeDtypeStruct((Ko, N), F32)
    else:
        out_spec = pl.BlockSpec((None, tko, tn), lambda i, j, t: (j // per, i, j % per))
        out_shape = jax.ShapeDtypeStruct((shards, Ko, n), F32)
    return _pcall(
        body, name=name, grid=(Ko // tko, N // tn, T // tt),
        in_specs=[pl.BlockSpec((tt, tko), lambda i, j, t: (t, i)),
                  pl.BlockSpec((tt, tn), lambda i, j, t: (t, j))],
        out_specs=out_spec, out_shape=out_shape,
        compiler_params=_params("parallel", "parallel", "arbitrary"))(a, b)


def _rms_fwd(x, gains, *, name):
    T, D = x.shape
    tm = min(T, ROW_BLOCK)
    n = len(gains)

    def body(x_ref, *refs):
        xv = x_ref[...]
        xh = xv * lax.rsqrt(jnp.mean(xv * xv, axis=-1, keepdims=True) + EPS)
        for g_ref, o_ref in zip(refs[:n], refs[n:]):
            o_ref[...] = (xh * g_ref[...]).astype(BF16)

    row = pl.BlockSpec((tm, D), lambda i: (i, 0))
    vec = pl.BlockSpec((1, D), lambda i: (0, 0))
    return _pcall(
        body, name=name, grid=(T // tm,), in_specs=[row] + [vec] * n, out_specs=[row] * n,
        out_shape=[jax.ShapeDtypeStruct((T, D), BF16)] * n,
        compiler_params=_params("parallel"))(x, *gains)


def _rms_bwd(x, dres, dhs, gains, *, name, rider=None):
    T, D = x.shape
    tm = min(T, ROW_BLOCK)
    steps = T // tm
    n = len(gains)
    rider = rider or _Rider([], [], [], None)
    nri, nro = len(rider.inputs), len(rider.out_shapes)

    def body(x_ref, dres_ref, *refs):
        dh_refs, g_refs = refs[:n], refs[n:2 * n]
        refs = refs[2 * n:]
        rider_in, refs = refs[:nri], refs[nri:]
        dx_ref, dg_refs = refs[0], refs[1:1 + n]
        rider_out, sems = refs[1 + n:1 + n + nro], refs[1 + n + nro:]
        i = pl.program_id(0)
        if nro:
            start, finish = rider.bind(rider_in, rider_out, sems)
            pl.when(i == 0)(start)
        xv = x_ref[...]
        r = lax.rsqrt(jnp.mean(xv * xv, axis=-1, keepdims=True) + EPS)
        xh = xv * r
        dxh = jnp.zeros_like(xv)
        for dh_ref, g_ref, dg_ref in zip(dh_refs, g_refs, dg_refs):
            dh = dh_ref[...]
            part = jnp.sum(dh * xh, axis=0, keepdims=True)

            @pl.when(i == 0)
            def _():
                dg_ref[...] = part

            @pl.when(i > 0)
            def _():
                dg_ref[...] += part

            dxh = dxh + dh * g_ref[...]
        dx_ref[...] = dres_ref[...] + r * (dxh - xh * jnp.mean(dxh * xh, axis=-1, keepdims=True))
        if nro:
            pl.when(i == steps - 1)(finish)

    row = pl.BlockSpec((tm, D), lambda i: (i, 0))
    vec = pl.BlockSpec((1, D), lambda i: (0, 0))
    return _pcall(
        body, name=name, grid=(steps,), in_specs=[row, row] + [row] * n + [vec] * n + [ANY] * nri,
        out_specs=[row] + [vec] * n + [ANY] * nro,
        out_shape=[jax.ShapeDtypeStruct((T, D), F32)] + [jax.ShapeDtypeStruct((1, D), F32)] * n
                  + rider.out_shapes,
        scratch_shapes=rider.scratch,
        compiler_params=_params("arbitrary"))(x, dres, *dhs, *gains, *rider.inputs)


def _final_loss(x, target, gain, *, name):
    T, D = x.shape
    tm = min(T, ROW_BLOCK)

    def body(x_ref, t_ref, g_ref, dx_ref, dg_ref, loss_ref):
        i = pl.program_id(0)
        xv = x_ref[...]
        g = g_ref[...]
        r = lax.rsqrt(jnp.mean(xv * xv, axis=-1, keepdims=True) + EPS)
        xh = xv * r
        err = xh * g - t_ref[...]
        part_loss = 0.5 * jnp.sum(jnp.mean(err * err, axis=-1, keepdims=True), axis=0, keepdims=True)
        dy = err * (1.0 / D)
        part_g = jnp.sum(dy * xh, axis=0, keepdims=True)

        @pl.when(i == 0)
        def _():
            dg_ref[...] = part_g
            loss_ref[...] = jnp.broadcast_to(part_loss, loss_ref.shape)

        @pl.when(i > 0)
        def _():
            dg_ref[...] += part_g
            loss_ref[...] += jnp.broadcast_to(part_loss, loss_ref.shape)

        dxh = dy * g
        dx_ref[...] = r * (dxh - xh * jnp.mean(dxh * xh, axis=-1, keepdims=True))

    row = pl.BlockSpec((tm, D), lambda i: (i, 0))
    vec = pl.BlockSpec((1, D), lambda i: (0, 0))
    return _pcall(
        body, name=name, grid=(T // tm,), in_specs=[row, row, vec],
        out_specs=[row, vec, pl.BlockSpec((1, LANES), lambda i: (0, 0))],
        out_shape=[jax.ShapeDtypeStruct((T, D), F32), jax.ShapeDtypeStruct((1, D), F32),
                   jax.ShapeDtypeStruct((1, LANES), F32)],
        compiler_params=_params("arbitrary"))(x, target, gain)


def _shift_down(x, prev_tail, j, row):
    tb = x.shape[0]
    prev = jnp.tile(prev_tail, (tb // SUBLANES, 1))
    return jnp.where(row >= j, pltpu.roll(x, j, 0), pltpu.roll(prev, j, 0))


def _shift_up(x, next_head, j, row):
    tb = x.shape[0]
    nxt = jnp.tile(next_head, (tb // SUBLANES, 1))
    return jnp.where(row < tb - j, pltpu.roll(x, tb - j, 0), pltpu.roll(nxt, tb - j, 0))


def _lru_gates(xb, wr, wi, br, bi, lam):
    xbb = xb.astype(BF16)
    r = _sigmoid(jnp.dot(xbb, wr, preferred_element_type=F32) + br)
    i = _sigmoid(jnp.dot(xbb, wi, preferred_element_type=F32) + bi)
    sp = jnp.maximum(-lam, 0.0) + jnp.log1p(jnp.exp(-jnp.abs(lam)))
    log_a = (-LRU_C) * r * sp
    a = jnp.exp(log_a)
    a2 = a * a
    mult = jnp.sqrt(jnp.maximum(-jnp.tanh(log_a) * (1.0 + a2), 0.0))
    return xbb, r, i, sp, a, a2, mult


def _scan_rows(coef, val, edge, reverse):
    tb, C = coef.shape
    a, b = coef, val
    row = lax.broadcasted_iota(jnp.int32, (tb, C), 0)
    s = 1
    while s < tb:
        m = (row < tb - s) if reverse else (row >= s)
        shift = tb - s if reverse else s
        b = jnp.where(m, a * pltpu.roll(b, shift, 0) + b, b)
        a = jnp.where(m, a * pltpu.roll(a, shift, 0), a)
        s *= 2
    return b + a * edge


def _acore_fwd(proj, conv_w, conv_b, w_r, w_i, b_r, b_i, lam, *, name, riders=()):
    T, C2 = proj.shape
    C = C2 // 2
    nb, bw, _ = w_r.shape
    tb = min(T, SCAN_BLOCK)
    nt = T // tb
    nr = len(riders)

    def body(xp_ref, gate_ref, cw_ref, cb_ref, wr_ref, wi_ref, br_ref, bi_ref, lam_ref, *refs):
        rider_in, refs = refs[:nr], refs[nr:]
        xb_ref, h_ref, yg_ref = refs[:3]
        rider_out, refs = refs[3:3 + nr], refs[3 + nr:]
        tail_ref, hlast_ref = refs[:2]
        t = pl.program_id(1)
        if nr:
            gather = _Gather(rider_in, rider_out, *refs[2:])
            pl.when((pl.program_id(0) == 0) & (t == 0))(gather.start)

        @pl.when(t == 0)
        def _():
            tail_ref[...] = jnp.zeros_like(tail_ref)
            hlast_ref[...] = jnp.zeros_like(hlast_ref)

        row = lax.broadcasted_iota(jnp.int32, (tb, bw), 0)
        xp = xp_ref[...]
        tail = tail_ref[...]
        xb = cb_ref[...] + cw_ref[CONV_W - 1:CONV_W, :] * xp
        for j in range(1, CONV_W):
            xb = xb + cw_ref[CONV_W - 1 - j:CONV_W - j, :] * _shift_down(xp, tail, j, row)
        tail_ref[...] = xp[tb - SUBLANES:, :]
        xb_ref[...] = xb

        _, r, i, sp, a, a2, mult = _lru_gates(xb, wr_ref[...], wi_ref[...], br_ref[...], bi_ref[...],
                                              lam_ref[...])
        h = _scan_rows(a, mult * (i * xb), hlast_ref[SUBLANES - 1:SUBLANES, :], False)
        hlast_ref[...] = h[tb - SUBLANES:, :]
        h_ref[...] = h
        gate = gate_ref[...]
        yg_ref[...] = (h * (gate * _sigmoid(gate))).astype(BF16)
        if nr:
            pl.when((pl.program_id(0) == nb - 1) & (t == nt - 1))(gather.finish)

    blk = lambda off: pl.BlockSpec((tb, bw), lambda n, t: (t, off + n))
    vec = pl.BlockSpec((1, bw), lambda n, t: (0, n))
    wspec = pl.BlockSpec((None, bw, bw), lambda n, t: (n, 0, 0))
    return _pcall(
        body, name=name, grid=(nb, nt),
        in_specs=[blk(0), blk(nb), pl.BlockSpec((CONV_W, bw), lambda n, t: (0, n)), vec, wspec, wspec,
                  vec, vec, vec] + [ANY] * nr,
        out_specs=[blk(0), blk(0), blk(0)] + [ANY] * nr,
        out_shape=[jax.ShapeDtypeStruct((T, C), F32), jax.ShapeDtypeStruct((T, C), F32),
                   jax.ShapeDtypeStruct((T, C), BF16)]
                  + [jax.ShapeDtypeStruct((N_DEV,) + r.shape, r.dtype) for r in riders],
        scratch_shapes=[pltpu.VMEM((SUBLANES, bw), F32), pltpu.VMEM((SUBLANES, bw), F32)]
                       + (_Gather.scratch(nr) if nr else []),
        compiler_params=_params("arbitrary" if nr else "parallel", "arbitrary"))(
            proj, proj, conv_w, conv_b, w_r, w_i, b_r, b_i, lam, *riders)


def _acore_bwd(dyg, proj, xb_all, h_all, conv_w, w_r, w_i, b_r, b_i, lam, *, name, rider=None):
    T, C2 = proj.shape
    C = C2 // 2
    nb, bw, _ = w_r.shape
    tb = min(T, SCAN_BLOCK)
    nt = T // tb
    per8 = tb // SUBLANES
    rider = rider or _Rider([], [], [], None)
    nri, nro = len(rider.inputs), len(rider.out_shapes)

    def body(dyg_ref, xp_ref, gate_ref, xb_ref, h_ref, xp_prev_ref, h_prev_ref, cw_ref,
             wr_ref, wi_ref, br_ref, bi_ref, lam_ref, *refs):
        rider_in, refs = refs[:nri], refs[nri:]
        dxp_ref, dgate_ref, dcw_ref, dcb_ref, dbr_ref, dbi_ref, dlam_ref, dwr_ref, dwi_ref = refs[:9]
        rider_out, refs = refs[9:9 + nro], refs[9 + nro:]
        gh_next_ref, a_next_ref, dxb_next_ref = refs[:3]
        step = pl.program_id(1)
        first_block = step == nt - 1
        if nro:
            start, finish = rider.bind(rider_in, rider_out, refs[3:])
            pl.when((pl.program_id(0) == 0) & (step == 0))(start)

        @pl.when(step == 0)
        def _():
            gh_next_ref[...] = jnp.zeros_like(gh_next_ref)
            a_next_ref[...] = jnp.zeros_like(a_next_ref)
            dxb_next_ref[...] = jnp.zeros_like(dxb_next_ref)

        row = lax.broadcasted_iota(jnp.int32, (tb, bw), 0)
        keep = jnp.where(first_block, 0.0, 1.0)
        h_prev = h_prev_ref[...] * keep
        xp_prev = xp_prev_ref[...] * keep
        xp, gate, xb, h, dyg_v = xp_ref[...], gate_ref[...], xb_ref[...], h_ref[...], dyg_ref[...]
        lam_v = lam_ref[...]
        wr, wi = wr_ref[...], wi_ref[...]

        sg = _sigmoid(gate)
        dh = dyg_v * (gate * sg)
        dgate_ref[...] = (dyg_v * h * (sg * (1.0 + gate * (1.0 - sg)))).astype(BF16)

        xbb, r, i, sp, a, a2, mult = _lru_gates(xb, wr, wi, br_ref[...], bi_ref[...], lam_v)

        gh = _scan_rows(_shift_up(a, a_next_ref[...], 1, row), dh, gh_next_ref[0:1, :], True)
        gh_next_ref[...] = gh[0:SUBLANES, :]
        a_next_ref[...] = a[0:SUBLANES, :]

        da = gh * _shift_down(h, h_prev, 1, row)
        dmult = gh * (i * xb)
        di = gh * mult * xb
        dxb = gh * mult * i
        dla = da * a - dmult * jnp.where(mult > 0.0, a2 / mult, 0.0)
        dr = dla * ((-LRU_C) * sp)
        dsp = jnp.sum(dla * ((-LRU_C) * r), axis=0, keepdims=True)
        dlam_part = dsp * (-_sigmoid(-lam_v))
        dpr = dr * r * (1.0 - r)
        dpi = di * i * (1.0 - i)
        dbr_part = jnp.sum(dpr, axis=0, keepdims=True)
        dbi_part = jnp.sum(dpi, axis=0, keepdims=True)
        dprb, dpib = dpr.astype(BF16), dpi.astype(BF16)
        dwr_part = _dot(xbb, dprb, 0, 0)
        dwi_part = _dot(xbb, dpib, 0, 0)
        dxb = dxb + _dot(dprb, wr, 1, 1) + _dot(dpib, wi, 1, 1)

        dxb_next = dxb_next_ref[...]
        dxp = cw_ref[CONV_W - 1:CONV_W, :] * dxb
        for j in range(1, CONV_W):
            dxp = dxp + cw_ref[CONV_W - 1 - j:CONV_W - j, :] * _shift_up(dxb, dxb_next, j, row)
        dxb_next_ref[...] = dxb[0:SUBLANES, :]
        dxp_ref[...] = dxp.astype(BF16)
        dcb_part = jnp.sum(dxb, axis=0, keepdims=True)
        dcw_rows = []
        for k in range(CONV_W):
            j = CONV_W - 1 - k
            sh = xp if j == 0 else _shift_down(xp, xp_prev, j, row)
            dcw_rows.append(jnp.sum(dxb * sh, axis=0, keepdims=True))

        @pl.when(step == 0)
        def _():
            for k in range(CONV_W):
                dcw_ref[k:k + 1, :] = dcw_rows[k]
            dcb_ref[...] = dcb_part
            dbr_ref[...] = dbr_part
            dbi_ref[...] = dbi_part
            dlam_ref[...] = dlam_part
            dwr_ref[...] = dwr_part
            dwi_ref[...] = dwi_part

        @pl.when(step > 0)
        def _():
            for k in range(CONV_W):
                dcw_ref[k:k + 1, :] += dcw_rows[k]
            dcb_ref[...] += dcb_part
            dbr_ref[...] += dbr_part
            dbi_ref[...] += dbi_part
            dlam_ref[...] += dlam_part
            dwr_ref[...] += dwr_part
            dwi_ref[...] += dwi_part

        if nro:
            pl.when((pl.program_id(0) == nb - 1) & (step == nt - 1))(finish)

    rev = lambda s: nt - 1 - s
    blk = lambda off: pl.BlockSpec((tb, bw), lambda n, s: (rev(s), off + n))
    prev8 = lambda off: pl.BlockSpec(
        (SUBLANES, bw), lambda n, s: (jnp.maximum(rev(s) * per8 - 1, 0), off + n))
    vec = pl.BlockSpec((1, bw), lambda n, s: (0, n))
    wspec = pl.BlockSpec((None, bw, bw), lambda n, s: (n, 0, 0))
    cwspec = pl.BlockSpec((CONV_W, bw), lambda n, s: (0, n))
    vshape = jax.ShapeDtypeStruct((1, C), F32)
    wshape = jax.ShapeDtypeStruct((nb, bw, bw), F32)
    return _pcall(
        body, name=name, grid=(nb, nt),
        in_specs=[blk(0), blk(0), blk(nb), blk(0), blk(0), prev8(0), prev8(0), cwspec,
                  wspec, wspec, vec, vec, vec] + [ANY] * nri,
        out_specs=[blk(0), blk(0), cwspec, vec, vec, vec, vec, wspec, wspec] + [ANY] * nro,
        out_shape=[jax.ShapeDtypeStruct((T, C), BF16), jax.ShapeDtypeStruct((T, C), BF16),
                   jax.ShapeDtypeStruct((CONV_W, C), F32), vshape, vshape, vshape, vshape,
                   wshape, wshape] + rider.out_shapes,
        scratch_shapes=[pltpu.VMEM((SUBLANES, bw), F32)] * 3 + rider.scratch,
        compiler_params=_params("arbitrary" if nro else "parallel", "arbitrary"))(
            dyg, proj, proj, xb_all, h_all, proj, h_all, conv_w, w_r, w_i, b_r, b_i, lam, *rider.inputs)


def _later_sum(lk, tri):
    return jnp.dot(lk.astype(BF16), tri, preferred_element_type=F32)


def _log2_sigmoids(y):
    t = jnp.log(1.0 + jnp.exp2(-jnp.abs(y))) * LOG2E
    ls = jnp.minimum(y, 0.0) - t
    return ls, ls - y


def _attn_blocks(T):
    bk = min(T, ATT_KEY_BLOCK)
    bq = min(T, ATT_QUERY_BLOCK)
    return bq, bk, bq // bk


def _attn_fwd(q, kv, gate, *, name):
    T, HD = q.shape
    H = HD // HEAD_DIM
    bq, bk, per = _attn_blocks(T)
    scale = 1.0 / math.sqrt(HEAD_DIM)

    def body(q_ref, k_ref, v_ref, g_ref, o_ref, og_ref, lt_ref, w_ref):
        i = pl.program_id(1)
        qv = q_ref[...]
        tr = lax.broadcasted_iota(jnp.int32, (bk, bk), 0)
        tc = lax.broadcasted_iota(jnp.int32, (bk, bk), 1)
        tri = (tr > tc).astype(BF16)
        ahead = (lax.broadcasted_iota(jnp.int32, (bq, bk), 0)
                 - lax.broadcasted_iota(jnp.int32, (bq, bk), 1))

        def starts_of(top):
            return [pl.multiple_of((top - d) * bk, bk) for d in range(per)]

        def scores(top):
            return [_dot(qv, k_ref[pl.ds(ks, bk), :], 1, 1) for ks in starts_of(top)]

        def weights(top, zs, c, mask):
            lss, sums, css, causals = [], [], [], []
            for ks, z in zip(starts_of(top), zs):
                ls, lk = _log2_sigmoids(z * (scale * LOG2E))
                if mask:
                    causals.append(ahead > ks - i * bq)
                    lk = jnp.where(causals[-1], lk, 0.0)
                lss.append(ls)
                sums.append(jnp.sum(lk, axis=1, keepdims=True))
                css.append(_later_sum(lk, tri))
            for d in range(per):
                w = jnp.exp2(lss[d] + (css[d] + c))
                if mask:
                    w = jnp.where(causals[d], w, 0.0)
                w_ref[d] = w.astype(BF16)
                c = c + sums[d]
            return c

        def values(top, acc):
            for d, ks in enumerate(starts_of(top)):
                acc = acc + jnp.dot(w_ref[d], v_ref[pl.ds(ks, bk), :], preferred_element_type=F32)
            return acc

        def more(state):
            gg, _, _, largest = state
            return (gg <= i) & (largest > WEIGHT_FLOOR_LOG2)

        def step(state):
            gg, acc, c, _ = state
            top = (i - gg) * per + per - 1
            zs = scores(top)
            acc = values(top + per, acc)
            c = weights(top, zs, c, False)
            return gg + 1, acc, c, jnp.max(c)

        diag_top = i * per + per - 1
        c = weights(diag_top, scores(diag_top), jnp.zeros((bq, 1), F32), True)
        gg, acc, c, _ = lax.while_loop(more, step, (1, jnp.zeros((bq, HEAD_DIM), F32), c, jnp.max(c)))
        acc = values((i - gg + 1) * per + per - 1, acc)
        o_ref[...] = acc
        g = g_ref[...]
        og_ref[...] = (acc * (g * _sigmoid(g))).astype(BF16)
        lane = lax.broadcasted_iota(jnp.int32, (bq, HEAD_DIM), 1)
        lt_ref[...] = jnp.where(lane == 1, (i - gg + 1).astype(F32), jnp.broadcast_to(c, (bq, HEAD_DIM)))

    qspec = pl.BlockSpec((bq, HEAD_DIM), lambda h, i: (i, h))
    return _pcall(
        body, name=name, grid=(H, T // bq),
        in_specs=[qspec, pl.BlockSpec((T, HEAD_DIM), lambda h, i: (0, h)),
                  pl.BlockSpec((T, HEAD_DIM), lambda h, i: (0, H + h)), qspec],
        out_specs=[qspec, qspec, qspec],
        out_shape=[jax.ShapeDtypeStruct((T, HD), F32), jax.ShapeDtypeStruct((T, HD), BF16),
                   jax.ShapeDtypeStruct((T, HD), F32)],
        scratch_shapes=[pltpu.VMEM((per, bq, bk), BF16)],
        compiler_params=_params("parallel", "arbitrary"))(q, kv, kv, gate)


def _attn_bwd(q, kv, gate, o, ltot, dog, *, name):
    T, HD = q.shape
    H = HD // HEAD_DIM
    bq, bk, per = _attn_blocks(T)
    nq = T // bq
    scale = 1.0 / math.sqrt(HEAD_DIM)

    def body(q_ref, k_ref, v_ref, g_ref, o_ref, lt_ref, dog_ref,
             dq_ref, dg_ref, dk_ref, dv_ref, dk_acc, dv_acc, dz_ref, w_ref):
        i = pl.program_id(1)

        @pl.when(i == 0)
        def _():
            dk_acc[...] = jnp.zeros_like(dk_acc)
            dv_acc[...] = jnp.zeros_like(dv_acc)

        qv = q_ref[...]
        g, ov, dogv = g_ref[...], o_ref[...], dog_ref[...]
        sg = _sigmoid(g)
        do = dogv * (g * sg)
        dg_ref[...] = (dogv * ov * (sg * (1.0 + g * (1.0 - sg)))).astype(BF16)
        dob = do.astype(BF16)
        ltot_v = lt_ref[:, 0:1]
        tr = lax.broadcasted_iota(jnp.int32, (bk, bk), 0)
        tc = lax.broadcasted_iota(jnp.int32, (bk, bk), 1)
        tri_later = (tr > tc).astype(BF16)
        tri_excl = (tr < tc).astype(BF16)
        ahead = (lax.broadcasted_iota(jnp.int32, (bq, bk), 0)
                 - lax.broadcasted_iota(jnp.int32, (bq, bk), 1))

        def starts_of(first):
            return [pl.multiple_of((first + d) * bk, bk) for d in range(per)]

        def scores(first):
            return ([_dot(qv, k_ref[pl.ds(ks, bk), :], 1, 1) for ks in starts_of(first)],
                    [_dot(dob, v_ref[pl.ds(ks, bk), :], 1, 1) for ks in starts_of(first)])

        def front(first, zs, dws, p_lk, p_g, mask):
            lss, css, causals = [], [], []
            for ks, z in zip(starts_of(first), zs):
                ls, lk = _log2_sigmoids(z * (scale * LOG2E))
                if mask:
                    causals.append(ahead > ks - i * bq)
                    lk = jnp.where(causals[-1], lk, 0.0)
                lss.append(ls)
                p_lk = p_lk + jnp.sum(lk, axis=1, keepdims=True)
                css.append((ltot_v - p_lk) + _later_sum(lk, tri_later))
            gms, befores = [], []
            for d in range(per):
                w = jnp.exp2(lss[d] + css[d])
                if mask:
                    w = jnp.where(causals[d], w, 0.0)
                gm = dws[d] * w
                gms.append(gm)
                w_ref[d] = w.astype(BF16)
                befores.append(jnp.dot(gm.astype(BF16), tri_excl, preferred_element_type=F32) + p_g)
                p_g = p_g + jnp.sum(gm, axis=1, keepdims=True)
            for d in range(per):
                dz = gms[d] - jnp.exp2(lss[d]) * (gms[d] + befores[d])
                if mask:
                    dz = jnp.where(causals[d], dz, 0.0)
                dz_ref[d] = (dz * scale).astype(BF16)
            return p_lk, p_g

        def back(first, dq):
            for d, ks in enumerate(starts_of(first)):
                dzb = dz_ref[d]
                dq = dq + jnp.dot(dzb, k_ref[pl.ds(ks, bk), :], preferred_element_type=F32)
                dk_acc[pl.ds(ks, bk), :] += _dot(dzb, qv, 0, 0)
                dv_acc[pl.ds(ks, bk), :] += _dot(w_ref[d], dob, 0, 0)
            return dq

        def step(mask):
            def trip(g, state):
                dq, p_lk, p_g = state
                zs, dws = scores(g * per)
                dq = back((g - 1) * per, dq)
                return (dq,) + front(g * per, zs, dws, p_lk, p_g, mask)
            return trip

        g0 = jnp.max(lt_ref[0:1, 1:2]).astype(jnp.int32)
        zero = jnp.zeros((bq, 1), F32)
        state = (jnp.zeros((bq, HEAD_DIM), F32),) + front(g0 * per, *scores(g0 * per), zero, zero, True)
        state = lax.fori_loop(g0 + 1, i, step(False), state)
        state = lax.fori_loop(jnp.maximum(i, g0 + 1), i + 1, step(True), state)
        dq_ref[...] = back(i * per, state[0]).astype(BF16)

        @pl.when(i == nq - 1)
        def _():
            dk_ref[...] = dk_acc[...].astype(BF16)
            dv_ref[...] = dv_acc[...].astype(BF16)

    qspec = pl.BlockSpec((bq, HEAD_DIM), lambda h, i: (i, h))
    kspec = pl.BlockSpec((T, HEAD_DIM), lambda h, i: (0, h))
    return _pcall(
        body, name=name, grid=(H, nq),
        in_specs=[qspec, kspec, pl.BlockSpec((T, HEAD_DIM), lambda h, i: (0, H + h)),
                  qspec, qspec, qspec, qspec],
        out_specs=[qspec, qspec, kspec, kspec],
        out_shape=[jax.ShapeDtypeStruct((T, HD), BF16)] * 4,
        scratch_shapes=[pltpu.VMEM((T, HEAD_DIM), F32)] * 2 + [pltpu.VMEM((per, bq, bk), BF16)] * 2,
        compiler_params=_params("parallel", "arbitrary"))(q, kv, kv, gate, o, ltot, dog)


def _position():
    return lax.axis_index("x"), lax.axis_index("y"), lax.axis_index("c")


def _chip_of(k, x, y):
    return (1 - x if k & 1 else x), (1 - y if k & 2 else y)


class _Gather:
    @staticmethod
    def scratch(n):
        return [pltpu.SemaphoreType.DMA((n, 7)), pltpu.SemaphoreType.DMA((n, 7)),
                pltpu.SemaphoreType.DMA((n,))]

    def __init__(self, ins, outs, send_sems, recv_sems, local_sems):
        self.ins, self.outs, self.n = ins, outs, len(ins)
        self.send_sems, self.recv_sems, self.local_sems = send_sems, recv_sems, local_sems
        x, y, c = _position()
        self.me, self.sibling = (x, y, c), (x, y, 1 - c)
        self.chips = [_chip_of(k, x, y) for k in (1, 2, 3)]

    def copy(self, a, k, block, to, src=None):
        slot = self.outs[a].at[4 * block[0] + 2 * block[1] + block[2]]
        return pltpu.make_async_remote_copy(
            src_ref=slot if src is None else src, dst_ref=slot,
            send_sem=self.send_sems.at[a, k], recv_sem=self.recv_sems.at[a, k],
            device_id=to, device_id_type=MESH)

    def own_copies(self):
        x, y, c = self.me
        mine = [pltpu.make_async_copy(self.ins[a], self.outs[a].at[4 * x + 2 * y + c], self.local_sems.at[a])
                for a in range(self.n)]
        first = []
        for a in range(self.n):
            first.append(self.copy(a, 0, self.me, self.sibling, src=self.ins[a]))
            first += [self.copy(a, 1 + j, self.me, (*chip, c), src=self.ins[a])
                      for j, chip in enumerate(self.chips)]
        return mine, first

    def start(self):
        mine, first = self.own_copies()
        for cp in mine + first:
            cp.start()

    def finish(self):
        c = self.me[2]
        mine, first = self.own_copies()
        passed = []
        for j, chip in enumerate(self.chips):
            for a in range(self.n):
                self.copy(a, 1 + j, (*chip, c), self.me).wait_recv()
                fwd = self.copy(a, 4 + j, (*chip, c), self.sibling)
                fwd.start()
                passed.append(fwd)
        for a in range(self.n):
            self.copy(a, 0, self.sibling, self.me).wait_recv()
            for j, chip in enumerate(self.chips):
                self.copy(a, 4 + j, (*chip, 1 - c), self.me).wait_recv()
        for cp in first + passed:
            cp.wait_send()
        for cp in mine:
            cp.wait()


def _weights_gather(shards):
    n = len(shards)

    def body(*refs):
        gather = _Gather(refs[:n], refs[n:2 * n], *refs[2 * n:])
        gather.start()
        gather.finish()

    return _pcall(
        body, name="weights_gather", in_specs=[ANY] * n, out_specs=[ANY] * n,
        out_shape=[jax.ShapeDtypeStruct((N_DEV,) + s.shape, s.dtype) for s in shards],
        scratch_shapes=_Gather.scratch(n))(*shards)


class _Rider:
    def __init__(self, inputs, out_shapes, scratch, copies):
        self.inputs, self.out_shapes, self.scratch, self.copies = inputs, out_shapes, scratch, copies

    def bind(self, ins, outs, sems):
        def start():
            for cp in self.copies(ins, outs, sems):
                cp.start()

        def finish():
            cps = self.copies(ins, outs, sems)
            for cp in cps:
                cp.wait_send()
            for cp in cps:
                cp.wait_recv()

        return start, finish


def _sibling_rider(grads):
    n = len(grads)

    def copies(ins, outs, sems):
        x, y, c = _position()
        return [pltpu.make_async_remote_copy(
            src_ref=ins[a].at[2 * chip + (1 - c)], dst_ref=outs[a].at[chip],
            send_sem=sems[0].at[a, chip], recv_sem=sems[1].at[a, chip],
            device_id=(x, y, 1 - c), device_id_type=MESH) for a in range(n) for chip in range(4)]

    return _Rider(list(grads), [jax.ShapeDtypeStruct((4,) + g.shape[1:], g.dtype) for g in grads],
                  [pltpu.SemaphoreType.DMA((n, 4)), pltpu.SemaphoreType.DMA((n, 4))], copies)


def _chips_rider(parts):
    n = len(parts)

    def copies(ins, outs, sems):
        x, y, c = _position()
        cps = []
        for a in range(n):
            for k in range(3):
                cx, cy = _chip_of(k + 1, x, y)
                cps.append(pltpu.make_async_remote_copy(
                    src_ref=ins[a].at[2 * cx + cy], dst_ref=outs[a].at[k],
                    send_sem=sems[0].at[a, k], recv_sem=sems[1].at[a, k],
                    device_id=(cx, cy, c), device_id_type=MESH))
        return cps

    return _Rider(list(parts), [jax.ShapeDtypeStruct((3,) + p.shape[1:], p.dtype) for p in parts],
                  [pltpu.SemaphoreType.DMA((n, 3)), pltpu.SemaphoreType.DMA((n, 3))], copies)


def _small_gather(small):
    def body(small_ref, small_all, send_sems, recv_sems, local_sem):
        x, y, c = _position()
        me = 4 * x + 2 * y + c
        peers = [(x ^ (m >> 2), y ^ ((m >> 1) & 1), c ^ (m & 1)) for m in range(1, N_DEV)]
        sends = [pltpu.make_async_remote_copy(
            src_ref=small_ref, dst_ref=small_all.at[me], send_sem=send_sems.at[m], recv_sem=recv_sems.at[m],
            device_id=peer, device_id_type=MESH) for m, peer in enumerate(peers)]
        own = pltpu.make_async_copy(small_ref, small_all.at[me], local_sem)
        for cp in sends + [own]:
            cp.start()
        for cp in sends:
            cp.wait_send()
        for m, (px, py, pc) in enumerate(peers):
            pltpu.make_async_remote_copy(
                src_ref=small_ref, dst_ref=small_all.at[4 * px + 2 * py + pc],
                send_sem=send_sems.at[m], recv_sem=recv_sems.at[m],
                device_id=(px, py, pc), device_id_type=MESH).wait_recv()
        own.wait()

    return _pcall(
        body, name="small_gather", in_specs=[ANY], out_specs=ANY,
        out_shape=jax.ShapeDtypeStruct((N_DEV,) + small.shape, small.dtype),
        scratch_shapes=[pltpu.SemaphoreType.DMA((7,)), pltpu.SemaphoreType.DMA((7,)),
                        pltpu.SemaphoreType.DMA])(small)


def _pair_sum(grad, got, *, name):
    _, R, C = got.shape
    tr = _pick8(R, max(2 * SUBLANES, (1 << 17) // C))

    def body(g_ref, b_ref, o_ref, ob_ref):
        north = lax.axis_index("c") == 1
        for chip in range(4):
            s = jnp.where(north, g_ref[chip, 1], g_ref[chip, 0]) + b_ref[chip]
            o_ref[chip] = s
            ob_ref[chip] = s.astype(BF16)

    spec = pl.BlockSpec((4, tr, C), lambda i: (0, i, 0))
    return _pcall(
        body, name=name, grid=(R // tr,),
        in_specs=[pl.BlockSpec((4, 2, tr, C), lambda i: (0, 0, i, 0)), spec],
        out_specs=[spec, spec],
        out_shape=[jax.ShapeDtypeStruct((4, R, C), F32), jax.ShapeDtypeStruct((4, R, C), BF16)],
        compiler_params=_params("parallel"))(grad.reshape(4, 2, R, C), got)


def _pick8(n, cap):
    if n <= cap:
        return n
    best = None
    for t in range(SUBLANES, cap + 1, SUBLANES):
        if n % t == 0:
            best = t
    assert best is not None, (n, cap)
    return best


def _adamw(w, m, v, parts, *, name, chip_sums=None):
    R, C = w.shape
    tr = _pick8(R, max(SUBLANES, (1 << 17) // C))
    c1 = 1.0 - ADAM_B1 ** ADAM_STEP
    c2 = 1.0 - ADAM_B2 ** ADAM_STEP
    parts = list(parts) if chip_sums is None else [chip_sums] + list(parts)
    np_ = len(parts)

    def body(w_ref, m_ref, v_ref, *refs):
        p_refs = refs[:np_]
        g_ref, d_ref, nm_ref, nv_ref = refs[np_:]
        g = None
        if chip_sums is not None:
            s_ref, p_refs = p_refs[0], p_refs[1:]
            x1, y1 = lax.axis_index("x") == 1, lax.axis_index("y") == 1
            g = jnp.where(x1, jnp.where(y1, s_ref[3], s_ref[2]), jnp.where(y1, s_ref[1], s_ref[0]))
        for p_ref in p_refs:
            for t in [p_ref[k].astype(F32) for k in range(p_ref.shape[0])]:
                g = t if g is None else g + t
        mn = ADAM_B1 * m_ref[...] + (1.0 - ADAM_B1) * g
        vn = ADAM_B2 * v_ref[...] + (1.0 - ADAM_B2) * (g * g)
        d_ref[...] = -ADAM_LR * ((mn / c1) / (jnp.sqrt(vn / c2) + ADAM_EPS) + ADAM_WD * w_ref[...])
        g_ref[...] = g
        nm_ref[...] = mn
        nv_ref[...] = vn

    spec = pl.BlockSpec((tr, C), lambda i: (i, 0))
    pspecs = [pl.BlockSpec((p.shape[0], tr, C), lambda i: (0, i, 0)) for p in parts]
    return _pcall(
        body, name=name, grid=(R // tr,), in_specs=[spec] * 3 + pspecs, out_specs=[spec] * 4,
        out_shape=[jax.ShapeDtypeStruct((R, C), F32)] * 4,
        compiler_params=_params("parallel"))(w, m, v, *parts)


def _rows(a):
    return a.reshape(-1, LANES)


def _whole_from_columns(shards, *, name):
    S, K, n = shards.shape
    tk = _pick8(K, 1024)

    def body(s_ref, o_ref):
        o_ref[...] = s_ref[...]

    return _pcall(
        body, name=name, grid=(K // tk, S),
        in_specs=[pl.BlockSpec((None, tk, n), lambda i, s: (s, i, 0))],
        out_specs=pl.BlockSpec((tk, n), lambda i, s: (i, s)),
        out_shape=jax.ShapeDtypeStruct((K, S * n), shards.dtype),
        compiler_params=_params("parallel", "parallel"))(shards)


def _late_weights(a_w_out_rows, w_kv_cols, b_w_in_cols, b_w_out_rows):
    whole_rows = lambda g: g.reshape(g.shape[0] * g.shape[1], g.shape[2])
    return (whole_rows(a_w_out_rows), _whole_from_columns(w_kv_cols, name="w_kv_whole"),
            _whole_from_columns(b_w_in_cols, name="b_w_in_whole"), whole_rows(b_w_out_rows))


def _forward_backward(xs, target, a_norm, g_a_w_in, conv_w, conv_b, g_w_r, g_w_i, b_r, b_i, lam,
                      kv_norm, b_norm, final_norm, *, late_weights=None, late_shards=None):
    (h_a,) = _rms_fwd(xs, [a_norm], name="a_norm_fwd")
    proj_a = _mm_nn(h_a, g_a_w_in, name="a_in_proj", out_dtype=F32)
    xb, h_rec, yg, *gathered = _acore_fwd(proj_a, conv_w, conv_b, g_w_r, g_w_i, b_r, b_i, lam,
                                          name="a_core_fwd", riders=late_shards or ())
    g_a_w_out, g_w_kv, g_b_w_in, g_b_w_out = _late_weights(*gathered) if late_shards else late_weights
    x1 = _mm_nn(yg, g_a_w_out, name="a_out_proj", out_dtype=F32, res=xs)
    hk, hb = _rms_fwd(x1, [kv_norm, b_norm], name="kv_b_norm_fwd")
    kv = _mm_nn(hk, g_w_kv, name="kv_proj", out_dtype=BF16)
    hd = g_b_w_in.shape[1] // 2
    q = _mm_nn(hb, g_b_w_in, name="q_proj", out_dtype=BF16, col_off=0, cols=hd)
    gate_b = _mm_nn(hb, g_b_w_in, name="b_gate_proj", out_dtype=F32, col_off=hd, cols=hd)
    o, og, ltot = _attn_fwd(q, kv, gate_b, name="attn_fwd")
    x2 = _mm_nn(og, g_b_w_out, name="b_out_proj", out_dtype=F32, res=x1)
    dx2, d_final_norm, loss_part = _final_loss(x2, target, final_norm, name="final_norm_loss")

    dog = _mm_nt(dx2, g_b_w_out, name="b_out_proj_bwd")
    dw_b_out = _mm_tn(og, dx2, name="b_out_proj_wgrad")
    dq, dgate_b, dk, dv = _attn_bwd(q, kv, gate_b, o, ltot, dog, name="attn_bwd")
    dproj_b = jnp.concatenate([dq, dgate_b], axis=1)
    dkv = jnp.concatenate([dk, dv], axis=1)
    dhb = _mm_nt(dproj_b, g_b_w_in, name="b_in_proj_bwd")
    dw_b_in = _mm_tn(hb, dproj_b, name="b_in_proj_wgrad", shards=N_DEV)
    dhk = _mm_nt(dkv, g_w_kv, name="kv_proj_bwd")
    dw_kv = _mm_tn(hk, dkv, name="kv_proj_wgrad", shards=N_DEV)
    early = [dw_kv, dw_b_in, dw_b_out.reshape(N_DEV, -1, dw_b_out.shape[1])] if late_shards else []
    dx1, d_b_norm, d_kv_norm, *got = _rms_bwd(x1, dx2, [dhb, dhk], [b_norm, kv_norm], name="kv_b_norm_bwd",
                                              rider=_sibling_rider(early) if early else None)
    early_sums = [_pair_sum(f_, g_, name=f"pair_sum_early_{i}") for i, (f_, g_) in enumerate(zip(early, got))]
    dyg = _mm_nt(dx1, g_a_w_out, name="a_out_proj_bwd")
    dw_a_out = _mm_tn(yg, dx1, name="a_out_proj_wgrad")
    (dxp, dgate_a, d_conv_w, d_conv_b, d_b_r, d_b_i, d_lambda, dw_r, dw_i, *early_others) = _acore_bwd(
        dyg, proj_a, xb, h_rec, conv_w, g_w_r, g_w_i, b_r, b_i, lam, name="a_core_bwd",
        rider=_chips_rider([s[1] for s in early_sums]) if early else None)
    dproj_a = jnp.concatenate([dxp, dgate_a], axis=1)
    dw_a_in = _mm_tn(h_a, dproj_a, name="a_in_proj_wgrad", shards=N_DEV)
    rows = dw_r.shape[1] // N_DEV
    lru = lambda dw: dw.reshape(-1, N_DEV, rows, dw.shape[2]).transpose(1, 0, 2, 3).reshape(N_DEV, -1, dw.shape[2])
    late = [dw_a_in, dw_a_out.reshape(N_DEV, -1, dw_a_out.shape[1]), lru(dw_r), lru(dw_i)] if late_shards else []
    dh_a, *got = _mm_nt(dproj_a, g_a_w_in, name="a_in_proj_bwd", rider=_sibling_rider(late)) if late else (
        _mm_nt(dproj_a, g_a_w_in, name="a_in_proj_bwd"),)
    late_sums = [_pair_sum(f_, g_, name=f"pair_sum_late_{i}") for i, (f_, g_) in enumerate(zip(late, got))]
    grad_x, d_a_norm, *late_others = _rms_bwd(xs, dx1, [dh_a], [a_norm], name="a_norm_bwd",
                                              rider=_chips_rider([s[1] for s in late_sums]) if late else None)
    sums = late_sums[:2] + early_sums + late_sums[2:]
    others = late_others[:2] + early_others + late_others[2:]
    return (loss_part, grad_x, dw_a_in, dw_a_out, dw_kv, dw_b_in, dw_b_out, dw_r, dw_i, d_a_norm,
            d_conv_w, d_conv_b, d_b_r, d_b_i, d_lambda, d_kv_norm, d_b_norm, d_final_norm, sums, others)


def kernel(x, a_norm, a_w_in, a_conv_w, a_conv_b, a_w_r, a_b_r, a_w_i, a_b_i, a_lambda, a_w_out, kv_norm, w_kv, b_norm, b_w_in, b_w_out, final_norm, loss_target, m_a_norm, m_a_w_in, m_a_conv_w, m_a_conv_b, m_a_w_r, m_a_b_r, m_a_w_i, m_a_b_i, m_a_lambda, m_a_w_out, m_kv_norm, m_w_kv, m_b_norm, m_b_w_in, m_b_w_out, m_final_norm, v_a_norm, v_a_w_in, v_a_conv_w, v_a_conv_b, v_a_w_r, v_a_b_r, v_a_w_i, v_a_b_i, v_a_lambda, v_a_w_out, v_kv_norm, v_w_kv, v_b_norm, v_b_w_in, v_b_w_out, v_final_norm):
    T, D = x.shape[1], x.shape[2]
    nb, bw = a_w_r.shape[1], a_w_r.shape[3]
    C = nb * bw
    me = 4 * lax.axis_index("x") + 2 * lax.axis_index("y") + lax.axis_index("c")
    xs = x[0]
    target = loss_target[0]

    rows_r = a_w_r.shape[2]
    small_f32 = jnp.concatenate([_rows(a_conv_w[0]), _rows(b_norm[0])], axis=0)
    pad = (-small_f32.shape[0]) % SUBLANES
    small_f32 = jnp.pad(small_f32, ((0, pad), (0, 0)))
    a_w_in_cols, w_r_rows, w_i_rows, small_all = _weights_gather(
        [a_w_in[0].astype(BF16), a_w_r[0].reshape(nb * rows_r, bw).astype(BF16),
         a_w_i[0].reshape(nb * rows_r, bw).astype(BF16), small_f32])
    late_shards = [a_w_out[0].astype(BF16), w_kv.astype(BF16), b_w_in[0].astype(BF16), b_w_out[0].astype(BF16)]
    g_a_w_in = _whole_from_columns(a_w_in_cols, name="a_w_in_whole")
    g_w_r = w_r_rows.reshape(N_DEV, nb, rows_r, bw).transpose(1, 0, 2, 3).reshape(nb, bw, bw)
    g_w_i = w_i_rows.reshape(N_DEV, nb, rows_r, bw).transpose(1, 0, 2, 3).reshape(nb, bw, bw)
    cw_rows = a_conv_w.shape[1] * a_conv_w.shape[2] // LANES
    conv_w_full = small_all[:, :cw_rows, :].reshape(N_DEV, CONV_W, a_conv_w.shape[2])
    conv_w_full = conv_w_full.transpose(1, 0, 2).reshape(CONV_W, C)
    bn_rows = b_norm.shape[1] // LANES
    b_norm_full = small_all[:, cw_rows:cw_rows + bn_rows, :].reshape(1, D)
    kv_norm2, final_norm2 = kv_norm.reshape(1, D), final_norm.reshape(1, D)

    (loss_part, grad_x, dw_a_in, dw_a_out, dw_kv, dw_b_in, dw_b_out, dw_r, dw_i, d_a_norm, d_conv_w,
     d_conv_b, d_b_r, d_b_i, d_lambda, d_kv_norm, d_b_norm, d_final_norm, sums,
     others) = _forward_backward(
         xs, target, a_norm, g_a_w_in, conv_w_full, a_conv_b, g_w_r, g_w_i, a_b_r, a_b_i, a_lambda,
         kv_norm2, b_norm_full, final_norm2, late_shards=late_shards)

    small_parts = [d_a_norm, d_conv_w, d_conv_b, d_b_r, d_b_i, d_lambda, d_kv_norm, d_b_norm, d_final_norm]
    small_sizes = [p.size // LANES for p in small_parts]
    small = jnp.concatenate([_rows(p) for p in small_parts], axis=0)
    small_everyone = _small_gather(small)

    def shard2d(w):
        return w.reshape(-1, w.shape[-1])

    names_big = [(a_w_in, m_a_w_in, v_a_w_in), (a_w_out, m_a_w_out, v_a_w_out), (w_kv, m_w_kv, v_w_kv),
                 (b_w_in, m_b_w_in, v_b_w_in), (b_w_out, m_b_w_out, v_b_w_out),
                 (a_w_r, m_a_w_r, v_a_w_r), (a_w_i, m_a_w_i, v_a_w_i)]
    upd_big = []
    for i, (w, m, v) in enumerate(names_big):
        res = _adamw(shard2d(w), shard2d(m), shard2d(v), [others[i]], chip_sums=sums[i][0], name=f"adamw_{i}")
        upd_big.append([r.reshape(w.shape) for r in res])

    soffs = [0]
    for s in small_sizes:
        soffs.append(soffs[-1] + s)

    def small_piece(i):
        return small_everyone[:, soffs[i]:soffs[i + 1], :]

    cw_cols = a_conv_w.shape[2]
    conv_piece = small_piece(1).reshape(N_DEV, CONV_W, C)
    conv_piece = lax.dynamic_slice_in_dim(conv_piece, me * cw_cols, cw_cols, axis=2)
    conv_piece = conv_piece.reshape(N_DEV, CONV_W * cw_cols // LANES, LANES)
    bn_piece = lax.dynamic_slice_in_dim(small_piece(7), me * bn_rows, bn_rows, axis=1)
    small_g = jnp.concatenate([small_piece(0), conv_piece, small_piece(2), small_piece(3), small_piece(4),
                               small_piece(5), small_piece(6), bn_piece, small_piece(8)], axis=1)
    small_w = [(a_norm, m_a_norm, v_a_norm), (a_conv_w, m_a_conv_w, v_a_conv_w),
               (a_conv_b, m_a_conv_b, v_a_conv_b), (a_b_r, m_a_b_r, v_a_b_r), (a_b_i, m_a_b_i, v_a_b_i),
               (a_lambda, m_a_lambda, v_a_lambda), (kv_norm, m_kv_norm, v_kv_norm),
               (b_norm, m_b_norm, v_b_norm), (final_norm, m_final_norm, v_final_norm)]
    pack = lambda idx: jnp.concatenate([_rows(t[idx]) for t in small_w], axis=0)
    res_small = _adamw(pack(0), pack(1), pack(2), [small_g], name="adamw_small")
    woffs = [0]
    for t in small_w:
        woffs.append(woffs[-1] + t[0].size // LANES)
    upd_small = [[r[woffs[i]:woffs[i + 1]].reshape(small_w[i][0].shape) for r in res_small]
                 for i in range(len(small_w))]

    order = [("s", 0), ("b", 0), ("s", 1), ("s", 2), ("b", 5), ("s", 3), ("b", 6), ("s", 4), ("s", 5),
             ("b", 1), ("s", 6), ("b", 2), ("s", 7), ("b", 3), ("b", 4), ("s", 8)]
    per_weight = [(upd_big if kind == "b" else upd_small)[i] for kind, i in order]
    loss = lax.psum(loss_part[0, 0], ("x", "y", "c"))
    result = [loss, grad_x[None]]
    for field in range(4):
        result += [u[field] for u in per_weight]
    return tuple(result)
```

```python
import functools
import math

import jax
import jax.numpy as jnp
from jax import lax
from jax.experimental import pallas as pl
from jax.experimental.pallas import tpu as pltpu

F32 = jnp.float32
BF16 = jnp.bfloat16
MESH = pl.DeviceIdType.MESH

EPS = 1e-6
LOG2E = 1.4426950408889634
WEIGHT_FLOOR_LOG2 = -200.0
LRU_C = 8.0
CONV_W = 4
HEAD_DIM = 128
ADAM_LR = 0.001
ADAM_B1 = 0.9
ADAM_B2 = 0.999
ADAM_EPS = 1e-08
ADAM_WD = 0.01
ADAM_STEP = 10

N_DEV = 8
LANES = 128
SUBLANES = 8
VMEM_LIMIT = 56 * 1024 * 1024

ATT_KEY_BLOCK = 256
ATT_QUERY_BLOCK = 256
SCAN_BLOCK = 256
ROW_BLOCK = 256
MM_TOKEN_BLOCK = 512
MM_WEIGHT_TILE = 1280
MM_CONTRACT_TOKENS = 2048
ANY = pl.BlockSpec(memory_space=pl.ANY)


def _pcall(body, **kw):
    return pl.pallas_call(body, **kw)


def _params(*sem):
    return pltpu.CompilerParams(dimension_semantics=sem, vmem_limit_bytes=VMEM_LIMIT)


def _pick(n, cap):
    if n <= cap:
        return n
    best = None
    for t in range(LANES, cap + 1, LANES):
        if n % t == 0:
            best = t
    assert best is not None, (n, cap)
    return best


def _sigmoid(x):
    return 1.0 / (1.0 + jnp.exp(-x))


def _dot(a, b, ca, cb):
    return lax.dot_general(a, b, (((ca,), (cb,)), ((), ())), preferred_element_type=F32)


def _mm_nn(a, w, *, name, out_dtype, col_off=0, cols=None, res=None):
    T, K = a.shape
    K2, N = w.shape
    assert K == K2
    cols = N if cols is None else cols
    tm = min(T, MM_TOKEN_BLOCK)
    tn = _pick(cols, MM_WEIGHT_TILE)
    assert col_off % tn == 0
    off = col_off // tn
    has_res = res is not None

    def body(a_ref, b_ref, *rest):
        o_ref = rest[-1]
        acc = jnp.dot(a_ref[...].astype(BF16), b_ref[...], preferred_element_type=F32)
        if has_res:
            acc = acc + rest[0][...]
        o_ref[...] = acc.astype(out_dtype)

    in_specs = [pl.BlockSpec((tm, K), lambda j, i: (i, 0)),
                pl.BlockSpec((K, tn), lambda j, i: (0, off + j))]
    args = [a, w]
    if has_res:
        in_specs.append(pl.BlockSpec((tm, tn), lambda j, i: (i, j)))
        args.append(res)
    return _pcall(
        body, name=name, grid=(cols // tn, T // tm), in_specs=in_specs,
        out_specs=pl.BlockSpec((tm, tn), lambda j, i: (i, j)),
        out_shape=jax.ShapeDtypeStruct((T, cols), out_dtype),
        compiler_params=_params("parallel", "parallel"))(*args)


def _mm_nt(a, w, *, name, out_dtype=F32, rider=None):
    parts = a.shape[0] if a.ndim == 3 else 1
    T, kp = a.shape[-2:]
    N, K = w.shape
    assert K == parts * kp
    tm = min(T, MM_TOKEN_BLOCK)
    tn = _pick(N, MM_WEIGHT_TILE)
    nj, ni = N // tn, T // tm
    rider = rider or _Rider([], [], [], None)
    nri, nro = len(rider.inputs), len(rider.out_shapes)

    def body(a_ref, b_ref, *refs):
        o_ref = refs[nri]
        j, i = pl.program_id(0), pl.program_id(1)
        if nro:
            start, finish = rider.bind(refs[:nri], refs[nri + 1:nri + 1 + nro], refs[nri + 1 + nro:])
            pl.when((j == 0) & (i == 0))(start)
        if a.ndim == 3:
            acc = None
            for p in range(parts):
                term = _dot(a_ref[p], b_ref[:, p * kp:(p + 1) * kp], 1, 1)
                acc = term if acc is None else acc + term
        else:
            acc = _dot(a_ref[...].astype(BF16), b_ref[...], 1, 1)
        o_ref[...] = acc.astype(out_dtype)
        if nro:
            pl.when((j == nj - 1) & (i == ni - 1))(finish)

    a_spec = (pl.BlockSpec((parts, tm, kp), lambda j, i: (0, i, 0)) if a.ndim == 3
              else pl.BlockSpec((tm, K), lambda j, i: (i, 0)))
    sem = ("arbitrary", "arbitrary") if nro else ("parallel", "parallel")
    out = _pcall(
        body, name=name, grid=(nj, ni),
        in_specs=[a_spec, pl.BlockSpec((tn, K), lambda j, i: (j, 0))] + [ANY] * nri,
        out_specs=[pl.BlockSpec((tm, tn), lambda j, i: (i, j))] + [ANY] * nro,
        out_shape=[jax.ShapeDtypeStruct((T, N), out_dtype)] + rider.out_shapes,
        scratch_shapes=rider.scratch,
        compiler_params=_params(*sem))(a, w, *rider.inputs)
    return out if nro else out[0]


def _mm_tn(a, b, *, name, shards=1):
    T, Ko = a.shape
    parts = b.shape[0] if b.ndim == 3 else 1
    T2, n_part = b.shape[-2:]
    N = parts * n_part
    assert T == T2
    n = N // shards
    tt = min(T, MM_CONTRACT_TOKENS)
    tko = _pick(Ko, 1024)
    tn = _pick(n, 1024)
    per = n // tn
    assert n_part % tn == 0
    per_part = n_part // tn

    def body(a_ref, b_ref, o_ref):
        t = pl.program_id(2)
        p = _dot(a_ref[...].astype(BF16), b_ref[...].astype(BF16), 0, 0)

        @pl.when(t == 0)
        def _():
            o_ref[...] = p

        @pl.when(t > 0)
        def _():
            o_ref[...] += p

    if shards == 1:
        out_spec = pl.BlockSpec((tko, tn), lambda i, j, t: (i, j))
        out_shape = jax.ShapeDtypeStruct((Ko, N), F32)
    else:
        out_spec = pl.BlockSpec((None, tko, tn), lambda i, j, t: (j // per, i, j % per))
        out_shape = jax.ShapeDtypeStruct((shards, Ko, n), F32)
    b_spec = (pl.BlockSpec((None, tt, tn), lambda i, j, t: (j // per_part, t, j % per_part)) if b.ndim == 3
              else pl.BlockSpec((tt, tn), lambda i, j, t: (t, j)))
    return _pcall(
        body, name=name, grid=(Ko // tko, N // tn, T // tt),
        in_specs=[pl.BlockSpec((tt, tko), lambda i, j, t: (t, i)), b_spec],
        out_specs=out_spec, out_shape=out_shape,
        compiler_params=_params("parallel", "parallel", "arbitrary"))(a, b)


def _rms_fwd(x, gains, *, name):
    T, D = x.shape
    tm = min(T, ROW_BLOCK)
    n = len(gains)

    def body(x_ref, *refs):
        xv = x_ref[...]
        xh = xv * lax.rsqrt(jnp.mean(xv * xv, axis=-1, keepdims=True) + EPS)
        for g_ref, o_ref in zip(refs[:n], refs[n:]):
            o_ref[...] = (xh * g_ref[...]).astype(BF16)

    row = pl.BlockSpec((tm, D), lambda i: (i, 0))
    vec = pl.BlockSpec((1, D), lambda i: (0, 0))
    return _pcall(
        body, name=name, grid=(T // tm,), in_specs=[row] + [vec] * n, out_specs=[row] * n,
        out_shape=[jax.ShapeDtypeStruct((T, D), BF16)] * n,
        compiler_params=_params("parallel"))(x, *gains)


def _rms_bwd(x, dres, dhs, gains, *, name, rider=None):
    T, D = x.shape
    tm = min(T, ROW_BLOCK)
    steps = T // tm
    n = len(gains)
    rider = rider or _Rider([], [], [], None)
    nri, nro = len(rider.inputs), len(rider.out_shapes)

    def body(x_ref, dres_ref, *refs):
        dh_refs, g_refs = refs[:n], refs[n:2 * n]
        refs = refs[2 * n:]
        rider_in, refs = refs[:nri], refs[nri:]
        dx_ref, dg_refs = refs[0], refs[1:1 + n]
        rider_out, sems = refs[1 + n:1 + n + nro], refs[1 + n + nro:]
        i = pl.program_id(0)
        if nro:
            start, finish = rider.bind(rider_in, rider_out, sems)
            pl.when(i == 0)(start)
        xv = x_ref[...]
        r = lax.rsqrt(jnp.mean(xv * xv, axis=-1, keepdims=True) + EPS)
        xh = xv * r
        dxh = jnp.zeros_like(xv)
        for dh_ref, g_ref, dg_ref in zip(dh_refs, g_refs, dg_refs):
            dh = dh_ref[...]
            part = jnp.sum(dh * xh, axis=0, keepdims=True)

            @pl.when(i == 0)
            def _():
                dg_ref[...] = part

            @pl.when(i > 0)
            def _():
                dg_ref[...] += part

            dxh = dxh + dh * g_ref[...]
        dx_ref[...] = dres_ref[...] + r * (dxh - xh * jnp.mean(dxh * xh, axis=-1, keepdims=True))
        if nro:
            pl.when(i == steps - 1)(finish)

    row = pl.BlockSpec((tm, D), lambda i: (i, 0))
    vec = pl.BlockSpec((1, D), lambda i: (0, 0))
    return _pcall(
        body, name=name, grid=(steps,), in_specs=[row, row] + [row] * n + [vec] * n + [ANY] * nri,
        out_specs=[row] + [vec] * n + [ANY] * nro,
        out_shape=[jax.ShapeDtypeStruct((T, D), F32)] + [jax.ShapeDtypeStruct((1, D), F32)] * n
                  + rider.out_shapes,
        scratch_shapes=rider.scratch,
        compiler_params=_params("arbitrary"))(x, dres, *dhs, *gains, *rider.inputs)


def _final_loss(x, target, gain, *, name):
    T, D = x.shape
    tm = min(T, ROW_BLOCK)

    def body(x_ref, t_ref, g_ref, dx_ref, dg_ref, loss_ref):
        i = pl.program_id(0)
        xv = x_ref[...]
        g = g_ref[...]
        r = lax.rsqrt(jnp.mean(xv * xv, axis=-1, keepdims=True) + EPS)
        xh = xv * r
        err = xh * g - t_ref[...]
        part_loss = 0.5 * jnp.sum(jnp.mean(err * err, axis=-1, keepdims=True), axis=0, keepdims=True)
        dy = err * (1.0 / D)
        part_g = jnp.sum(dy * xh, axis=0, keepdims=True)

        @pl.when(i == 0)
        def _():
            dg_ref[...] = part_g
            loss_ref[...] = jnp.broadcast_to(part_loss, loss_ref.shape)

        @pl.when(i > 0)
        def _():
            dg_ref[...] += part_g
            loss_ref[...] += jnp.broadcast_to(part_loss, loss_ref.shape)

        dxh = dy * g
        dx_ref[...] = r * (dxh - xh * jnp.mean(dxh * xh, axis=-1, keepdims=True))

    row = pl.BlockSpec((tm, D), lambda i: (i, 0))
    vec = pl.BlockSpec((1, D), lambda i: (0, 0))
    return _pcall(
        body, name=name, grid=(T // tm,), in_specs=[row, row, vec],
        out_specs=[row, vec, pl.BlockSpec((1, LANES), lambda i: (0, 0))],
        out_shape=[jax.ShapeDtypeStruct((T, D), F32), jax.ShapeDtypeStruct((1, D), F32),
                   jax.ShapeDtypeStruct((1, LANES), F32)],
        compiler_params=_params("arbitrary"))(x, target, gain)


def _shift_down(x, prev_tail, j, row):
    tb = x.shape[0]
    prev = jnp.tile(prev_tail, (tb // SUBLANES, 1))
    return jnp.where(row >= j, pltpu.roll(x, j, 0), pltpu.roll(prev, j, 0))


def _shift_up(x, next_head, j, row):
    tb = x.shape[0]
    nxt = jnp.tile(next_head, (tb // SUBLANES, 1))
    return jnp.where(row < tb - j, pltpu.roll(x, tb - j, 0), pltpu.roll(nxt, tb - j, 0))


def _lru_gates(xb, wr, wi, br, bi, lam):
    xbb = xb.astype(BF16)
    r = _sigmoid(jnp.dot(xbb, wr, preferred_element_type=F32) + br)
    i = _sigmoid(jnp.dot(xbb, wi, preferred_element_type=F32) + bi)
    sp = jnp.maximum(-lam, 0.0) + jnp.log1p(jnp.exp(-jnp.abs(lam)))
    log_a = (-LRU_C) * r * sp
    a = jnp.exp(log_a)
    a2 = a * a
    mult = jnp.sqrt(jnp.maximum(-jnp.tanh(log_a) * (1.0 + a2), 0.0))
    return xbb, r, i, sp, a, a2, mult


def _scan_rows(coef, val, edge, reverse):
    tb, C = coef.shape
    a, b = coef, val
    row = lax.broadcasted_iota(jnp.int32, (tb, C), 0)
    s = 1
    while s < tb:
        m = (row < tb - s) if reverse else (row >= s)
        shift = tb - s if reverse else s
        b = jnp.where(m, a * pltpu.roll(b, shift, 0) + b, b)
        a = jnp.where(m, a * pltpu.roll(a, shift, 0), a)
        s *= 2
    return b + a * edge


def _acore_fwd(proj, conv_w, conv_b, w_r, w_i, b_r, b_i, lam, *, name, riders=()):
    T, C2 = proj.shape
    C = C2 // 2
    nb, bw, _ = w_r.shape
    tb = min(T, SCAN_BLOCK)
    nt = T // tb
    nr = len(riders)

    def body(xp_ref, gate_ref, cw_ref, cb_ref, wr_ref, wi_ref, br_ref, bi_ref, lam_ref, *refs):
        rider_in, refs = refs[:nr], refs[nr:]
        xb_ref, h_ref, yg_ref = refs[:3]
        rider_out, refs = refs[3:3 + nr], refs[3 + nr:]
        tail_ref, hlast_ref = refs[:2]
        t = pl.program_id(1)
        if nr:
            gather = _Gather(rider_in, rider_out, *refs[2:])
            pl.when((pl.program_id(0) == 0) & (t == 0))(gather.start)

        @pl.when(t == 0)
        def _():
            tail_ref[...] = jnp.zeros_like(tail_ref)
            hlast_ref[...] = jnp.zeros_like(hlast_ref)

        row = lax.broadcasted_iota(jnp.int32, (tb, bw), 0)
        xp = xp_ref[...]
        tail = tail_ref[...]
        xb = cb_ref[...] + cw_ref[CONV_W - 1:CONV_W, :] * xp
        for j in range(1, CONV_W):
            xb = xb + cw_ref[CONV_W - 1 - j:CONV_W - j, :] * _shift_down(xp, tail, j, row)
        tail_ref[...] = xp[tb - SUBLANES:, :]
        xb_ref[...] = xb

        _, r, i, sp, a, a2, mult = _lru_gates(xb, wr_ref[...], wi_ref[...], br_ref[...], bi_ref[...],
                                              lam_ref[...])
        h = _scan_rows(a, mult * (i * xb), hlast_ref[SUBLANES - 1:SUBLANES, :], False)
        hlast_ref[...] = h[tb - SUBLANES:, :]
        h_ref[...] = h
        gate = gate_ref[...]
        yg_ref[...] = (h * (gate * _sigmoid(gate))).astype(BF16)
        if nr:
            pl.when((pl.program_id(0) == nb - 1) & (t == nt - 1))(gather.finish)

    blk = lambda off: pl.BlockSpec((tb, bw), lambda n, t: (t, off + n))
    vec = pl.BlockSpec((1, bw), lambda n, t: (0, n))
    wspec = pl.BlockSpec((None, bw, bw), lambda n, t: (n, 0, 0))
    return _pcall(
        body, name=name, grid=(nb, nt),
        in_specs=[blk(0), blk(nb), pl.BlockSpec((CONV_W, bw), lambda n, t: (0, n)), vec, wspec, wspec,
                  vec, vec, vec] + [ANY] * nr,
        out_specs=[blk(0), blk(0), blk(0)] + [ANY] * nr,
        out_shape=[jax.ShapeDtypeStruct((T, C), F32), jax.ShapeDtypeStruct((T, C), F32),
                   jax.ShapeDtypeStruct((T, C), BF16)]
                  + [jax.ShapeDtypeStruct((N_DEV,) + r.shape, r.dtype) for r in riders],
        scratch_shapes=[pltpu.VMEM((SUBLANES, bw), F32), pltpu.VMEM((SUBLANES, bw), F32)]
                       + (_Gather.scratch(nr) if nr else []),
        compiler_params=_params("arbitrary" if nr else "parallel", "arbitrary"))(
            proj, proj, conv_w, conv_b, w_r, w_i, b_r, b_i, lam, *riders)


def _acore_bwd(dyg, proj, xb_all, h_all, conv_w, w_r, w_i, b_r, b_i, lam, *, name, rider=None):
    T, C2 = proj.shape
    C = C2 // 2
    nb, bw, _ = w_r.shape
    tb = min(T, SCAN_BLOCK)
    nt = T // tb
    per8 = tb // SUBLANES
    rider = rider or _Rider([], [], [], None)
    nri, nro = len(rider.inputs), len(rider.out_shapes)

    def body(dyg_ref, xp_ref, gate_ref, xb_ref, h_ref, xp_prev_ref, h_prev_ref, cw_ref,
             wr_ref, wi_ref, br_ref, bi_ref, lam_ref, *refs):
        rider_in, refs = refs[:nri], refs[nri:]
        dproj_ref, dcw_ref, dcb_ref, dbr_ref, dbi_ref, dlam_ref, dwr_ref, dwi_ref = refs[:8]
        rider_out, refs = refs[8:8 + nro], refs[8 + nro:]
        gh_next_ref, a_next_ref, dxb_next_ref = refs[:3]
        step = pl.program_id(1)
        first_block = step == nt - 1
        if nro:
            start, finish = rider.bind(rider_in, rider_out, refs[3:])
            pl.when((pl.program_id(0) == 0) & (step == 0))(start)

        @pl.when(step == 0)
        def _():
            gh_next_ref[...] = jnp.zeros_like(gh_next_ref)
            a_next_ref[...] = jnp.zeros_like(a_next_ref)
            dxb_next_ref[...] = jnp.zeros_like(dxb_next_ref)

        row = lax.broadcasted_iota(jnp.int32, (tb, bw), 0)
        keep = jnp.where(first_block, 0.0, 1.0)
        h_prev = h_prev_ref[...] * keep
        xp_prev = xp_prev_ref[...] * keep
        xp, gate, xb, h, dyg_v = xp_ref[...], gate_ref[...], xb_ref[...], h_ref[...], dyg_ref[...]
        lam_v = lam_ref[...]
        wr, wi = wr_ref[...], wi_ref[...]

        sg = _sigmoid(gate)
        dh = dyg_v * (gate * sg)
        dproj_ref[1] = (dyg_v * h * (sg * (1.0 + gate * (1.0 - sg)))).astype(BF16)

        xbb, r, i, sp, a, a2, mult = _lru_gates(xb, wr, wi, br_ref[...], bi_ref[...], lam_v)

        gh = _scan_rows(_shift_up(a, a_next_ref[...], 1, row), dh, gh_next_ref[0:1, :], True)
        gh_next_ref[...] = gh[0:SUBLANES, :]
        a_next_ref[...] = a[0:SUBLANES, :]

        da = gh * _shift_down(h, h_prev, 1, row)
        dmult = gh * (i * xb)
        di = gh * mult * xb
        dxb = gh * mult * i
        dla = da * a - dmult * jnp.where(mult > 0.0, a2 / mult, 0.0)
        dr = dla * ((-LRU_C) * sp)
        dsp = jnp.sum(dla * ((-LRU_C) * r), axis=0, keepdims=True)
        dlam_part = dsp * (-_sigmoid(-lam_v))
        dpr = dr * r * (1.0 - r)
        dpi = di * i * (1.0 - i)
        dbr_part = jnp.sum(dpr, axis=0, keepdims=True)
        dbi_part = jnp.sum(dpi, axis=0, keepdims=True)
        dprb, dpib = dpr.astype(BF16), dpi.astype(BF16)
        dwr_part = _dot(xbb, dprb, 0, 0)
        dwi_part = _dot(xbb, dpib, 0, 0)
        dxb = dxb + _dot(dprb, wr, 1, 1) + _dot(dpib, wi, 1, 1)

        dxb_next = dxb_next_ref[...]
        dxp = cw_ref[CONV_W - 1:CONV_W, :] * dxb
        for j in range(1, CONV_W):
            dxp = dxp + cw_ref[CONV_W - 1 - j:CONV_W - j, :] * _shift_up(dxb, dxb_next, j, row)
        dxb_next_ref[...] = dxb[0:SUBLANES, :]
        dproj_ref[0] = dxp.astype(BF16)
        dcb_part = jnp.sum(dxb, axis=0, keepdims=True)
        dcw_rows = []
        for k in range(CONV_W):
            j = CONV_W - 1 - k
            sh = xp if j == 0 else _shift_down(xp, xp_prev, j, row)
            dcw_rows.append(jnp.sum(dxb * sh, axis=0, keepdims=True))

        @pl.when(step == 0)
        def _():
            for k in range(CONV_W):
                dcw_ref[k:k + 1, :] = dcw_rows[k]
            dcb_ref[...] = dcb_part
            dbr_ref[...] = dbr_part
            dbi_ref[...] = dbi_part
            dlam_ref[...] = dlam_part
            dwr_ref[...] = dwr_part
            dwi_ref[...] = dwi_part

        @pl.when(step > 0)
        def _():
            for k in range(CONV_W):
                dcw_ref[k:k + 1, :] += dcw_rows[k]
            dcb_ref[...] += dcb_part
            dbr_ref[...] += dbr_part
            dbi_ref[...] += dbi_part
            dlam_ref[...] += dlam_part
            dwr_ref[...] += dwr_part
            dwi_ref[...] += dwi_part

        if nro:
            pl.when((pl.program_id(0) == nb - 1) & (step == nt - 1))(finish)

    rev = lambda s: nt - 1 - s
    blk = lambda off: pl.BlockSpec((tb, bw), lambda n, s: (rev(s), off + n))
    prev8 = lambda off: pl.BlockSpec(
        (SUBLANES, bw), lambda n, s: (jnp.maximum(rev(s) * per8 - 1, 0), off + n))
    vec = pl.BlockSpec((1, bw), lambda n, s: (0, n))
    wspec = pl.BlockSpec((None, bw, bw), lambda n, s: (n, 0, 0))
    cwspec = pl.BlockSpec((CONV_W, bw), lambda n, s: (0, n))
    vshape = jax.ShapeDtypeStruct((1, C), F32)
    wshape = jax.ShapeDtypeStruct((nb, bw, bw), F32)
    return _pcall(
        body, name=name, grid=(nb, nt),
        in_specs=[blk(0), blk(0), blk(nb), blk(0), blk(0), prev8(0), prev8(0), cwspec,
                  wspec, wspec, vec, vec, vec] + [ANY] * nri,
        out_specs=[pl.BlockSpec((2, tb, bw), lambda n, s: (0, rev(s), n)), cwspec, vec, vec, vec, vec,
                   wspec, wspec] + [ANY] * nro,
        out_shape=[jax.ShapeDtypeStruct((2, T, C), BF16),
                   jax.ShapeDtypeStruct((CONV_W, C), F32), vshape, vshape, vshape, vshape,
                   wshape, wshape] + rider.out_shapes,
        scratch_shapes=[pltpu.VMEM((SUBLANES, bw), F32)] * 3 + rider.scratch,
        compiler_params=_params("arbitrary" if nro else "parallel", "arbitrary"))(
            dyg, proj, proj, xb_all, h_all, proj, h_all, conv_w, w_r, w_i, b_r, b_i, lam, *rider.inputs)


def _later_sum(lk, tri):
    return jnp.dot(lk.astype(BF16), tri, preferred_element_type=F32)


def _log2_sigmoids(y):
    t = jnp.log(1.0 + jnp.exp2(-jnp.abs(y))) * LOG2E
    ls = jnp.minimum(y, 0.0) - t
    return ls, ls - y


def _attn_blocks(T):
    bk = min(T, ATT_KEY_BLOCK)
    bq = min(T, ATT_QUERY_BLOCK)
    return bq, bk, bq // bk


def _attn_fwd(q, kv, gate, *, name):
    T, HD = q.shape
    H = HD // HEAD_DIM
    bq, bk, per = _attn_blocks(T)
    scale = 1.0 / math.sqrt(HEAD_DIM)

    def body(q_ref, k_ref, v_ref, g_ref, o_ref, og_ref, lt_ref, w_ref):
        i = pl.program_id(1)
        qv = q_ref[...]
        tr = lax.broadcasted_iota(jnp.int32, (bk, bk), 0)
        tc = lax.broadcasted_iota(jnp.int32, (bk, bk), 1)
        tri = (tr > tc).astype(BF16)
        ahead = (lax.broadcasted_iota(jnp.int32, (bq, bk), 0)
                 - lax.broadcasted_iota(jnp.int32, (bq, bk), 1))

        def starts_of(top):
            return [pl.multiple_of((top - d) * bk, bk) for d in range(per)]

        def scores(top):
            return [_dot(qv, k_ref[pl.ds(ks, bk), :], 1, 1) for ks in starts_of(top)]

        def weights(top, zs, c, mask):
            lss, sums, css, causals = [], [], [], []
            for ks, z in zip(starts_of(top), zs):
                ls, lk = _log2_sigmoids(z * (scale * LOG2E))
                if mask:
                    causals.append(ahead > ks - i * bq)
                    lk = jnp.where(causals[-1], lk, 0.0)
                lss.append(ls)
                sums.append(jnp.sum(lk, axis=1, keepdims=True))
                css.append(_later_sum(lk, tri))
            for d in range(per):
                w = jnp.exp2(lss[d] + (css[d] + c))
                if mask:
                    w = jnp.where(causals[d], w, 0.0)
                w_ref[d] = w.astype(BF16)
                c = c + sums[d]
            return c

        def values(top, acc):
            for d, ks in enumerate(starts_of(top)):
                acc = acc + jnp.dot(w_ref[d], v_ref[pl.ds(ks, bk), :], preferred_element_type=F32)
            return acc

        def more(state):
            gg, _, _, largest = state
            return (gg <= i) & (largest > WEIGHT_FLOOR_LOG2)

        def step(state):
            gg, acc, c, _ = state
            top = (i - gg) * per + per - 1
            zs = scores(top)
            acc = values(top + per, acc)
            c = weights(top, zs, c, False)
            return gg + 1, acc, c, jnp.max(c)

        diag_top = i * per + per - 1
        c = weights(diag_top, scores(diag_top), jnp.zeros((bq, 1), F32), True)
        gg, acc, c, _ = lax.while_loop(more, step, (1, jnp.zeros((bq, HEAD_DIM), F32), c, jnp.max(c)))
        acc = values((i - gg + 1) * per + per - 1, acc)
        o_ref[...] = acc
        g = g_ref[...]
        og_ref[...] = (acc * (g * _sigmoid(g))).astype(BF16)
        lane = lax.broadcasted_iota(jnp.int32, (bq, HEAD_DIM), 1)
        lt_ref[...] = jnp.where(lane == 1, (i - gg + 1).astype(F32), jnp.broadcast_to(c, (bq, HEAD_DIM)))

    qspec = pl.BlockSpec((bq, HEAD_DIM), lambda h, i: (i, h))
    return _pcall(
        body, name=name, grid=(H, T // bq),
        in_specs=[qspec, pl.BlockSpec((T, HEAD_DIM), lambda h, i: (0, h)),
                  pl.BlockSpec((T, HEAD_DIM), lambda h, i: (0, H + h)), qspec],
        out_specs=[qspec, qspec, qspec],
        out_shape=[jax.ShapeDtypeStruct((T, HD), F32), jax.ShapeDtypeStruct((T, HD), BF16),
                   jax.ShapeDtypeStruct((T, HD), F32)],
        scratch_shapes=[pltpu.VMEM((per, bq, bk), BF16)],
        compiler_params=_params("parallel", "arbitrary"))(q, kv, kv, gate)


def _attn_bwd(q, kv, gate, o, ltot, dog, *, name):
    T, HD = q.shape
    H = HD // HEAD_DIM
    bq, bk, per = _attn_blocks(T)
    nq = T // bq
    scale = 1.0 / math.sqrt(HEAD_DIM)

    def body(q_ref, k_ref, v_ref, g_ref, o_ref, lt_ref, dog_ref,
             dqg_ref, dkv_ref, dk_acc, dv_acc, dz_ref, w_ref):
        i = pl.program_id(1)

        @pl.when(i == 0)
        def _():
            dk_acc[...] = jnp.zeros_like(dk_acc)
            dv_acc[...] = jnp.zeros_like(dv_acc)

        qv = q_ref[...]
        g, ov, dogv = g_ref[...], o_ref[...], dog_ref[...]
        sg = _sigmoid(g)
        do = dogv * (g * sg)
        dqg_ref[1] = (dogv * ov * (sg * (1.0 + g * (1.0 - sg)))).astype(BF16)
        dob = do.astype(BF16)
        ltot_v = lt_ref[:, 0:1]
        tr = lax.broadcasted_iota(jnp.int32, (bk, bk), 0)
        tc = lax.broadcasted_iota(jnp.int32, (bk, bk), 1)
        tri_later = (tr > tc).astype(BF16)
        tri_excl = (tr < tc).astype(BF16)
        ahead = (lax.broadcasted_iota(jnp.int32, (bq, bk), 0)
                 - lax.broadcasted_iota(jnp.int32, (bq, bk), 1))

        def starts_of(first):
            return [pl.multiple_of((first + d) * bk, bk) for d in range(per)]

        def scores(first):
            return ([_dot(qv, k_ref[pl.ds(ks, bk), :], 1, 1) for ks in starts_of(first)],
                    [_dot(dob, v_ref[pl.ds(ks, bk), :], 1, 1) for ks in starts_of(first)])

        def front(first, zs, dws, p_lk, p_g, mask):
            lss, css, causals = [], [], []
            for ks, z in zip(starts_of(first), zs):
                ls, lk = _log2_sigmoids(z * (scale * LOG2E))
                if mask:
                    causals.append(ahead > ks - i * bq)
                    lk = jnp.where(causals[-1], lk, 0.0)
                lss.append(ls)
                p_lk = p_lk + jnp.sum(lk, axis=1, keepdims=True)
                css.append((ltot_v - p_lk) + _later_sum(lk, tri_later))
            gms, befores = [], []
            for d in range(per):
                w = jnp.exp2(lss[d] + css[d])
                if mask:
                    w = jnp.where(causals[d], w, 0.0)
                gm = dws[d] * w
                gms.append(gm)
                w_ref[d] = w.astype(BF16)
                befores.append(jnp.dot(gm.astype(BF16), tri_excl, preferred_element_type=F32) + p_g)
                p_g = p_g + jnp.sum(gm, axis=1, keepdims=True)
            for d in range(per):
                dz = gms[d] - jnp.exp2(lss[d]) * (gms[d] + befores[d])
                if mask:
                    dz = jnp.where(causals[d], dz, 0.0)
                dz_ref[d] = (dz * scale).astype(BF16)
            return p_lk, p_g

        def back(first, dq):
            for d, ks in enumerate(starts_of(first)):
                dzb = dz_ref[d]
                dq = dq + jnp.dot(dzb, k_ref[pl.ds(ks, bk), :], preferred_element_type=F32)
                dk_acc[pl.ds(ks, bk), :] += _dot(dzb, qv, 0, 0)
                dv_acc[pl.ds(ks, bk), :] += _dot(w_ref[d], dob, 0, 0)
            return dq

        def step(mask):
            def trip(g, state):
                dq, p_lk, p_g = state
                zs, dws = scores(g * per)
                dq = back((g - 1) * per, dq)
                return (dq,) + front(g * per, zs, dws, p_lk, p_g, mask)
            return trip

        g0 = jnp.max(lt_ref[0:1, 1:2]).astype(jnp.int32)
        zero = jnp.zeros((bq, 1), F32)
        state = (jnp.zeros((bq, HEAD_DIM), F32),) + front(g0 * per, *scores(g0 * per), zero, zero, True)
        state = lax.fori_loop(g0 + 1, i, step(False), state)
        state = lax.fori_loop(jnp.maximum(i, g0 + 1), i + 1, step(True), state)
        dqg_ref[0] = back(i * per, state[0]).astype(BF16)

        @pl.when(i == nq - 1)
        def _():
            dkv_ref[0] = dk_acc[...].astype(BF16)
            dkv_ref[1] = dv_acc[...].astype(BF16)

    qspec = pl.BlockSpec((bq, HEAD_DIM), lambda h, i: (i, h))
    kspec = pl.BlockSpec((T, HEAD_DIM), lambda h, i: (0, h))
    return _pcall(
        body, name=name, grid=(H, nq),
        in_specs=[qspec, kspec, pl.BlockSpec((T, HEAD_DIM), lambda h, i: (0, H + h)),
                  qspec, qspec, qspec, qspec],
        out_specs=[pl.BlockSpec((2, bq, HEAD_DIM), lambda h, i: (0, i, h)),
                   pl.BlockSpec((2, T, HEAD_DIM), lambda h, i: (0, 0, h))],
        out_shape=[jax.ShapeDtypeStruct((2, T, HD), BF16)] * 2,
        scratch_shapes=[pltpu.VMEM((T, HEAD_DIM), F32)] * 2 + [pltpu.VMEM((per, bq, bk), BF16)] * 2,
        compiler_params=_params("parallel", "arbitrary"))(q, kv, kv, gate, o, ltot, dog)


def _position():
    return lax.axis_index("x"), lax.axis_index("y"), lax.axis_index("c")


def _chip_of(k, x, y):
    return (1 - x if k & 1 else x), (1 - y if k & 2 else y)


class _Gather:
    @staticmethod
    def scratch(n):
        return [pltpu.SemaphoreType.DMA((n, 7)), pltpu.SemaphoreType.DMA((n, 7)),
                pltpu.SemaphoreType.DMA((n,))]

    def __init__(self, ins, outs, send_sems, recv_sems, local_sems):
        self.ins, self.outs, self.n = ins, outs, len(ins)
        self.send_sems, self.recv_sems, self.local_sems = send_sems, recv_sems, local_sems
        x, y, c = _position()
        self.me, self.sibling = (x, y, c), (x, y, 1 - c)
        self.chips = [_chip_of(k, x, y) for k in (1, 2, 3)]

    def copy(self, a, k, block, to, src=None):
        slot = self.outs[a].at[4 * block[0] + 2 * block[1] + block[2]]
        return pltpu.make_async_remote_copy(
            src_ref=slot if src is None else src, dst_ref=slot,
            send_sem=self.send_sems.at[a, k], recv_sem=self.recv_sems.at[a, k],
            device_id=to, device_id_type=MESH)

    def own_copies(self):
        x, y, c = self.me
        mine = [pltpu.make_async_copy(self.ins[a], self.outs[a].at[4 * x + 2 * y + c], self.local_sems.at[a])
                for a in range(self.n)]
        first = []
        for a in range(self.n):
            first.append(self.copy(a, 0, self.me, self.sibling, src=self.ins[a]))
            first += [self.copy(a, 1 + j, self.me, (*chip, c), src=self.ins[a])
                      for j, chip in enumerate(self.chips)]
        return mine, first

    def start(self):
        mine, first = self.own_copies()
        for cp in mine + first:
            cp.start()

    def finish(self):
        c = self.me[2]
        mine, first = self.own_copies()
        passed = []
        for j, chip in enumerate(self.chips):
            for a in range(self.n):
                self.copy(a, 1 + j, (*chip, c), self.me).wait_recv()
                fwd = self.copy(a, 4 + j, (*chip, c), self.sibling)
                fwd.start()
                passed.append(fwd)
        for a in range(self.n):
            self.copy(a, 0, self.sibling, self.me).wait_recv()
            for j, chip in enumerate(self.chips):
                self.copy(a, 4 + j, (*chip, 1 - c), self.me).wait_recv()
        for cp in first + passed:
            cp.wait_send()
        for cp in mine:
            cp.wait()


def _weights_gather(shards):
    n = len(shards)

    def body(*refs):
        gather = _Gather(refs[:n], refs[n:2 * n], *refs[2 * n:])
        gather.start()
        gather.finish()

    return _pcall(
        body, name="weights_gather", in_specs=[ANY] * n, out_specs=[ANY] * n,
        out_shape=[jax.ShapeDtypeStruct((N_DEV,) + s.shape, s.dtype) for s in shards],
        scratch_shapes=_Gather.scratch(n))(*shards)


class _Rider:
    def __init__(self, inputs, out_shapes, scratch, copies):
        self.inputs, self.out_shapes, self.scratch, self.copies = inputs, out_shapes, scratch, copies

    def bind(self, ins, outs, sems):
        def start():
            for cp in self.copies(ins, outs, sems):
                cp.start()

        def finish():
            cps = self.copies(ins, outs, sems)
            for cp in cps:
                cp.wait_send()
            for cp in cps:
                cp.wait_recv()

        return start, finish


def _sibling_rider(grads):
    n = len(grads)

    def copies(ins, outs, sems):
        x, y, c = _position()
        return [pltpu.make_async_remote_copy(
            src_ref=ins[a].at[2 * chip + (1 - c)], dst_ref=outs[a].at[chip],
            send_sem=sems[0].at[a, chip], recv_sem=sems[1].at[a, chip],
            device_id=(x, y, 1 - c), device_id_type=MESH) for a in range(n) for chip in range(4)]

    return _Rider(list(grads), [jax.ShapeDtypeStruct((4,) + g.shape[1:], g.dtype) for g in grads],
                  [pltpu.SemaphoreType.DMA((n, 4)), pltpu.SemaphoreType.DMA((n, 4))], copies)


def _chips_rider(parts):
    n = len(parts)

    def copies(ins, outs, sems):
        x, y, c = _position()
        cps = []
        for a in range(n):
            for k in range(3):
                cx, cy = _chip_of(k + 1, x, y)
                cps.append(pltpu.make_async_remote_copy(
                    src_ref=ins[a].at[2 * cx + cy], dst_ref=outs[a].at[k],
                    send_sem=sems[0].at[a, k], recv_sem=sems[1].at[a, k],
                    device_id=(cx, cy, c), device_id_type=MESH))
        return cps

    return _Rider(list(parts), [jax.ShapeDtypeStruct((3,) + p.shape[1:], p.dtype) for p in parts],
                  [pltpu.SemaphoreType.DMA((n, 3)), pltpu.SemaphoreType.DMA((n, 3))], copies)


def _small_gather(small):
    def body(small_ref, small_all, send_sems, recv_sems, local_sem):
        x, y, c = _position()
        me = 4 * x + 2 * y + c
        peers = [(x ^ (m >> 2), y ^ ((m >> 1) & 1), c ^ (m & 1)) for m in range(1, N_DEV)]
        sends = [pltpu.make_async_remote_copy(
            src_ref=small_ref, dst_ref=small_all.at[me], send_sem=send_sems.at[m], recv_sem=recv_sems.at[m],
            device_id=peer, device_id_type=MESH) for m, peer in enumerate(peers)]
        own = pltpu.make_async_copy(small_ref, small_all.at[me], local_sem)
        for cp in sends + [own]:
            cp.start()
        for cp in sends:
            cp.wait_send()
        for m, (px, py, pc) in enumerate(peers):
            pltpu.make_async_remote_copy(
                src_ref=small_ref, dst_ref=small_all.at[4 * px + 2 * py + pc],
                send_sem=send_sems.at[m], recv_sem=recv_sems.at[m],
                device_id=(px, py, pc), device_id_type=MESH).wait_recv()
        own.wait()

    return _pcall(
        body, name="small_gather", in_specs=[ANY], out_specs=ANY,
        out_shape=jax.ShapeDtypeStruct((N_DEV,) + small.shape, small.dtype),
        scratch_shapes=[pltpu.SemaphoreType.DMA((7,)), pltpu.SemaphoreType.DMA((7,)),
                        pltpu.SemaphoreType.DMA])(small)


def _pair_sum(grad, got, *, name):
    _, R, C = got.shape
    tr = _pick8(R, max(2 * SUBLANES, (1 << 17) // C))

    def body(g_ref, b_ref, o_ref, ob_ref):
        north = lax.axis_index("c") == 1
        for chip in range(4):
            s = jnp.where(north, g_ref[chip, 1], g_ref[chip, 0]) + b_ref[chip]
            o_ref[chip] = s
            ob_ref[chip] = s.astype(BF16)

    spec = pl.BlockSpec((4, tr, C), lambda i: (0, i, 0))
    return _pcall(
        body, name=name, grid=(R // tr,),
        in_specs=[pl.BlockSpec((4, 2, tr, C), lambda i: (0, 0, i, 0)), spec],
        out_specs=[spec, spec],
        out_shape=[jax.ShapeDtypeStruct((4, R, C), F32), jax.ShapeDtypeStruct((4, R, C), BF16)],
        compiler_params=_params("parallel"))(grad.reshape(4, 2, R, C), got)


def _pick8(n, cap):
    if n <= cap:
        return n
    best = None
    for t in range(SUBLANES, cap + 1, SUBLANES):
        if n % t == 0:
            best = t
    assert best is not None, (n, cap)
    return best


def _adamw(w, m, v, parts, *, name, chip_sums=None):
    R, C = w.shape
    tr = _pick8(R, max(SUBLANES, (1 << 17) // C))
    c1 = 1.0 - ADAM_B1 ** ADAM_STEP
    c2 = 1.0 - ADAM_B2 ** ADAM_STEP
    parts = list(parts) if chip_sums is None else [chip_sums] + list(parts)
    np_ = len(parts)

    def body(w_ref, m_ref, v_ref, *refs):
        p_refs = refs[:np_]
        g_ref, d_ref, nm_ref, nv_ref = refs[np_:]
        g = None
        if chip_sums is not None:
            s_ref, p_refs = p_refs[0], p_refs[1:]
            x1, y1 = lax.axis_index("x") == 1, lax.axis_index("y") == 1
            g = jnp.where(x1, jnp.where(y1, s_ref[3], s_ref[2]), jnp.where(y1, s_ref[1], s_ref[0]))
        for p_ref in p_refs:
            for t in [p_ref[k].astype(F32) for k in range(p_ref.shape[0])]:
                g = t if g is None else g + t
        mn = ADAM_B1 * m_ref[...] + (1.0 - ADAM_B1) * g
        vn = ADAM_B2 * v_ref[...] + (1.0 - ADAM_B2) * (g * g)
        d_ref[...] = -ADAM_LR * ((mn / c1) / (jnp.sqrt(vn / c2) + ADAM_EPS) + ADAM_WD * w_ref[...])
        g_ref[...] = g
        nm_ref[...] = mn
        nv_ref[...] = vn

    spec = pl.BlockSpec((tr, C), lambda i: (i, 0))
    pspecs = [pl.BlockSpec((p.shape[0], tr, C), lambda i: (0, i, 0)) for p in parts]
    return _pcall(
        body, name=name, grid=(R // tr,), in_specs=[spec] * 3 + pspecs, out_specs=[spec] * 4,
        out_shape=[jax.ShapeDtypeStruct((R, C), F32)] * 4,
        compiler_params=_params("parallel"))(w, m, v, *parts)


def _rows(a):
    return a.reshape(-1, LANES)


def _whole_from_columns(shards, *, name):
    S, K, n = shards.shape
    tk = _pick8(K, 1024)

    def body(s_ref, o_ref):
        o_ref[...] = s_ref[...]

    return _pcall(
        body, name=name, grid=(K // tk, S),
        in_specs=[pl.BlockSpec((None, tk, n), lambda i, s: (s, i, 0))],
        out_specs=pl.BlockSpec((tk, n), lambda i, s: (i, s)),
        out_shape=jax.ShapeDtypeStruct((K, S * n), shards.dtype),
        compiler_params=_params("parallel", "parallel"))(shards)


def _late_weights(a_w_out_rows, w_kv_cols, b_w_in_cols, b_w_out_rows):
    whole_rows = lambda g: g.reshape(g.shape[0] * g.shape[1], g.shape[2])
    return (whole_rows(a_w_out_rows), _whole_from_columns(w_kv_cols, name="w_kv_whole"),
            _whole_from_columns(b_w_in_cols, name="b_w_in_whole"), whole_rows(b_w_out_rows))


def _forward_backward(xs, target, a_norm, g_a_w_in, conv_w, conv_b, g_w_r, g_w_i, b_r, b_i, lam,
                      kv_norm, b_norm, final_norm, *, late_weights=None, late_shards=None):
    (h_a,) = _rms_fwd(xs, [a_norm], name="a_norm_fwd")
    proj_a = _mm_nn(h_a, g_a_w_in, name="a_in_proj", out_dtype=F32)
    xb, h_rec, yg, *gathered = _acore_fwd(proj_a, conv_w, conv_b, g_w_r, g_w_i, b_r, b_i, lam,
                                          name="a_core_fwd", riders=late_shards or ())
    g_a_w_out, g_w_kv, g_b_w_in, g_b_w_out = _late_weights(*gathered) if late_shards else late_weights
    x1 = _mm_nn(yg, g_a_w_out, name="a_out_proj", out_dtype=F32, res=xs)
    hk, hb = _rms_fwd(x1, [kv_norm, b_norm], name="kv_b_norm_fwd")
    kv = _mm_nn(hk, g_w_kv, name="kv_proj", out_dtype=BF16)
    hd = g_b_w_in.shape[1] // 2
    q = _mm_nn(hb, g_b_w_in, name="q_proj", out_dtype=BF16, col_off=0, cols=hd)
    gate_b = _mm_nn(hb, g_b_w_in, name="b_gate_proj", out_dtype=F32, col_off=hd, cols=hd)
    o, og, ltot = _attn_fwd(q, kv, gate_b, name="attn_fwd")
    x2 = _mm_nn(og, g_b_w_out, name="b_out_proj", out_dtype=F32, res=x1)
    dx2, d_final_norm, loss_part = _final_loss(x2, target, final_norm, name="final_norm_loss")

    dog = _mm_nt(dx2, g_b_w_out, name="b_out_proj_bwd")
    dw_b_out = _mm_tn(og, dx2, name="b_out_proj_wgrad")
    dproj_b, dkv = _attn_bwd(q, kv, gate_b, o, ltot, dog, name="attn_bwd")
    dhb = _mm_nt(dproj_b, g_b_w_in, name="b_in_proj_bwd")
    dw_b_in = _mm_tn(hb, dproj_b, name="b_in_proj_wgrad", shards=N_DEV)
    dhk = _mm_nt(dkv, g_w_kv, name="kv_proj_bwd")
    dw_kv = _mm_tn(hk, dkv, name="kv_proj_wgrad", shards=N_DEV)
    early = [dw_kv, dw_b_in, dw_b_out.reshape(N_DEV, -1, dw_b_out.shape[1])] if late_shards else []
    dx1, d_b_norm, d_kv_norm, *got = _rms_bwd(x1, dx2, [dhb, dhk], [b_norm, kv_norm], name="kv_b_norm_bwd",
                                              rider=_sibling_rider(early) if early else None)
    early_sums = [_pair_sum(f_, g_, name=f"pair_sum_early_{i}") for i, (f_, g_) in enumerate(zip(early, got))]
    dyg = _mm_nt(dx1, g_a_w_out, name="a_out_proj_bwd")
    dw_a_out = _mm_tn(yg, dx1, name="a_out_proj_wgrad")
    (dproj_a, d_conv_w, d_conv_b, d_b_r, d_b_i, d_lambda, dw_r, dw_i, *early_others) = _acore_bwd(
        dyg, proj_a, xb, h_rec, conv_w, g_w_r, g_w_i, b_r, b_i, lam, name="a_core_bwd",
        rider=_chips_rider([s[1] for s in early_sums]) if early else None)
    dw_a_in = _mm_tn(h_a, dproj_a, name="a_in_proj_wgrad", shards=N_DEV)
    rows = dw_r.shape[1] // N_DEV
    lru = lambda dw: dw.reshape(-1, N_DEV, rows, dw.shape[2]).transpose(1, 0, 2, 3).reshape(N_DEV, -1, dw.shape[2])
    late = [dw_a_in, dw_a_out.reshape(N_DEV, -1, dw_a_out.shape[1]), lru(dw_r), lru(dw_i)] if late_shards else []
    dh_a, *got = _mm_nt(dproj_a, g_a_w_in, name="a_in_proj_bwd", rider=_sibling_rider(late)) if late else (
        _mm_nt(dproj_a, g_a_w_in, name="a_in_proj_bwd"),)
    late_sums = [_pair_sum(f_, g_, name=f"pair_sum_late_{i}") for i, (f_, g_) in enumerate(zip(late, got))]
    grad_x, d_a_norm, *late_others = _rms_bwd(xs, dx1, [dh_a], [a_norm], name="a_norm_bwd",
                                              rider=_chips_rider([s[1] for s in late_sums]) if late else None)
    sums = late_sums[:2] + early_sums + late_sums[2:]
    others = late_others[:2] + early_others + late_others[2:]
    return (loss_part, grad_x, dw_a_in, dw_a_out, dw_kv, dw_b_in, dw_b_out, dw_r, dw_i, d_a_norm,
            d_conv_w, d_conv_b, d_b_r, d_b_i, d_lambda, d_kv_norm, d_b_norm, d_final_norm, sums, others)


def kernel(x, a_norm, a_w_in, a_conv_w, a_conv_b, a_w_r, a_b_r, a_w_i, a_b_i, a_lambda, a_w_out, kv_norm, w_kv, b_norm, b_w_in, b_w_out, final_norm, loss_target, m_a_norm, m_a_w_in, m_a_conv_w, m_a_conv_b, m_a_w_r, m_a_b_r, m_a_w_i, m_a_b_i, m_a_lambda, m_a_w_out, m_kv_norm, m_w_kv, m_b_norm, m_b_w_in, m_b_w_out, m_final_norm, v_a_norm, v_a_w_in, v_a_conv_w, v_a_conv_b, v_a_w_r, v_a_b_r, v_a_w_i, v_a_b_i, v_a_lambda, v_a_w_out, v_kv_norm, v_w_kv, v_b_norm, v_b_w_in, v_b_w_out, v_final_norm):
    T, D = x.shape[1], x.shape[2]
    nb, bw = a_w_r.shape[1], a_w_r.shape[3]
    C = nb * bw
    me = 4 * lax.axis_index("x") + 2 * lax.axis_index("y") + lax.axis_index("c")
    xs = x[0]
    target = loss_target[0]

    rows_r = a_w_r.shape[2]
    small_f32 = jnp.concatenate([_rows(a_conv_w[0]), _rows(b_norm[0])], axis=0)
    pad = (-small_f32.shape[0]) % SUBLANES
    small_f32 = jnp.pad(small_f32, ((0, pad), (0, 0)))
    a_w_in_cols, w_r_rows, w_i_rows, small_all = _weights_gather(
        [a_w_in[0].astype(BF16), a_w_r[0].reshape(nb * rows_r, bw).astype(BF16),
         a_w_i[0].reshape(nb * rows_r, bw).astype(BF16), small_f32])
    late_shards = [a_w_out[0].astype(BF16), w_kv.astype(BF16), b_w_in[0].astype(BF16), b_w_out[0].astype(BF16)]
    g_a_w_in = _whole_from_columns(a_w_in_cols, name="a_w_in_whole")
    g_w_r = w_r_rows.reshape(N_DEV, nb, rows_r, bw).transpose(1, 0, 2, 3).reshape(nb, bw, bw)
    g_w_i = w_i_rows.reshape(N_DEV, nb, rows_r, bw).transpose(1, 0, 2, 3).reshape(nb, bw, bw)
    cw_rows = a_conv_w.shape[1] * a_conv_w.shape[2] // LANES
    conv_w_full = small_all[:, :cw_rows, :].reshape(N_DEV, CONV_W, a_conv_w.shape[2])
    conv_w_full = conv_w_full.transpose(1, 0, 2).reshape(CONV_W, C)
    bn_rows = b_norm.shape[1] // LANES
    b_norm_full = small_all[:, cw_rows:cw_rows + bn_rows, :].reshape(1, D)
    kv_norm2, final_norm2 = kv_norm.reshape(1, D), final_norm.reshape(1, D)

    (loss_part, grad_x, dw_a_in, dw_a_out, dw_kv, dw_b_in, dw_b_out, dw_r, dw_i, d_a_norm, d_conv_w,
     d_conv_b, d_b_r, d_b_i, d_lambda, d_kv_norm, d_b_norm, d_final_norm, sums,
     others) = _forward_backward(
         xs, target, a_norm, g_a_w_in, conv_w_full, a_conv_b, g_w_r, g_w_i, a_b_r, a_b_i, a_lambda,
         kv_norm2, b_norm_full, final_norm2, late_shards=late_shards)

    small_parts = [d_a_norm, d_conv_w, d_conv_b, d_b_r, d_b_i, d_lambda, d_kv_norm, d_b_norm, d_final_norm]
    small_sizes = [p.size // LANES for p in small_parts]
    small = jnp.concatenate([_rows(p) for p in small_parts], axis=0)
    small_everyone = _small_gather(small)

    def shard2d(w):
        return w.reshape(-1, w.shape[-1])

    names_big = [(a_w_in, m_a_w_in, v_a_w_in), (a_w_out, m_a_w_out, v_a_w_out), (w_kv, m_w_kv, v_w_kv),
                 (b_w_in, m_b_w_in, v_b_w_in), (b_w_out, m_b_w_out, v_b_w_out),
                 (a_w_r, m_a_w_r, v_a_w_r), (a_w_i, m_a_w_i, v_a_w_i)]
    upd_big = []
    for i, (w, m, v) in enumerate(names_big):
        res = _adamw(shard2d(w), shard2d(m), shard2d(v), [others[i]], chip_sums=sums[i][0], name=f"adamw_{i}")
        upd_big.append([r.reshape(w.shape) for r in res])

    soffs = [0]
    for s in small_sizes:
        soffs.append(soffs[-1] + s)

    def small_piece(i):
        return small_everyone[:, soffs[i]:soffs[i + 1], :]

    cw_cols = a_conv_w.shape[2]
    conv_piece = small_piece(1).reshape(N_DEV, CONV_W, C)
    conv_piece = lax.dynamic_slice_in_dim(conv_piece, me * cw_cols, cw_cols, axis=2)
    conv_piece = conv_piece.reshape(N_DEV, CONV_W * cw_cols // LANES, LANES)
    bn_piece = lax.dynamic_slice_in_dim(small_piece(7), me * bn_rows, bn_rows, axis=1)
    small_g = jnp.concatenate([small_piece(0), conv_piece, small_piece(2), small_piece(3), small_piece(4),
                               small_piece(5), small_piece(6), bn_piece, small_piece(8)], axis=1)
    small_w = [(a_norm, m_a_norm, v_a_norm), (a_conv_w, m_a_conv_w, v_a_conv_w),
               (a_conv_b, m_a_conv_b, v_a_conv_b), (a_b_r, m_a_b_r, v_a_b_r), (a_b_i, m_a_b_i, v_a_b_i),
               (a_lambda, m_a_lambda, v_a_lambda), (kv_norm, m_kv_norm, v_kv_norm),
               (b_norm, m_b_norm, v_b_norm), (final_norm, m_final_norm, v_final_norm)]
    pack = lambda idx: jnp.concatenate([_rows(t[idx]) for t in small_w], axis=0)
    res_small = _adamw(pack(0), pack(1), pack(2), [small_g], name="adamw_small")
    woffs = [0]
    for t in small_w:
        woffs.append(woffs[-1] + t[0].size // LANES)
    upd_small = [[r[woffs[i]:woffs[i + 1]].reshape(small_w[i][0].shape) for r in res_small]
                 for i in range(len(small_w))]

    order = [("s", 0), ("b", 0), ("s", 1), ("s", 2), ("b", 5), ("s", 3), ("b", 6), ("s", 4), ("s", 5),
             ("b", 1), ("s", 6), ("b", 2), ("s", 7), ("b", 3), ("b", 4), ("s", 8)]
    per_weight = [(upd_big if kind == "b" else upd_small)[i] for kind, i in order]
    loss = lax.psum(loss_part[0, 0], ("x", "y", "c"))
    result = [loss, grad_x[None]]
    for field in range(4):
        result += [u[field] for u in per_weight]
    return tuple(result)
```

```python
import math

import jax
import jax.numpy as jnp
from jax import lax
from jax.experimental import pallas as pl
from jax.experimental.pallas import tpu as pltpu

F32 = jnp.float32
BF16 = jnp.bfloat16
MESH = pl.DeviceIdType.MESH

EPS = 1e-6
LOG2E = 1.4426950408889634
WEIGHT_FLOOR_LOG2 = -200.0
LRU_C = 8.0
CONV_W = 4
HEAD_DIM = 128
ADAM_LR = 0.001
ADAM_B1 = 0.9
ADAM_B2 = 0.999
ADAM_EPS = 1e-08
ADAM_WD = 0.01
ADAM_STEP = 10

N_DEV = 8
LANES = 128
SUBLANES = 8
VMEM_LIMIT = 56 * 1024 * 1024

ATT_KEY_BLOCK = 256
ATT_QUERY_BLOCK = 256
SCAN_BLOCK = 256
ROW_BLOCK = 256
MM_TOKEN_BLOCK = 512
MM_WEIGHT_TILE = 1280
MM_CONTRACT_TOKENS = 2048
ANY = pl.BlockSpec(memory_space=pl.ANY)


def _pcall(body, **kw):
    return pl.pallas_call(body, **kw)


def _params(*sem):
    return pltpu.CompilerParams(dimension_semantics=sem, vmem_limit_bytes=VMEM_LIMIT)


def _pick(n, cap):
    if n <= cap:
        return n
    best = None
    for t in range(LANES, cap + 1, LANES):
        if n % t == 0:
            best = t
    assert best is not None, (n, cap)
    return best


def _sigmoid(x):
    return 1.0 / (1.0 + jnp.exp(-x))


def _dot(a, b, ca, cb):
    return lax.dot_general(a, b, (((ca,), (cb,)), ((), ())), preferred_element_type=F32)


def _mm_nn(a, w, *, name, out_dtype, col_off=0, cols=None, res=None):
    T, K = a.shape
    K2, N = w.shape
    assert K == K2
    cols = N if cols is None else cols
    tm = min(T, MM_TOKEN_BLOCK)
    tn = _pick(cols, MM_WEIGHT_TILE)
    assert col_off % tn == 0
    off = col_off // tn
    has_res = res is not None

    def body(a_ref, b_ref, *rest):
        o_ref = rest[-1]
        acc = jnp.dot(a_ref[...].astype(BF16), b_ref[...], preferred_element_type=F32)
        if has_res:
            acc = acc + rest[0][...]
        o_ref[...] = acc.astype(out_dtype)

    in_specs = [pl.BlockSpec((tm, K), lambda j, i: (i, 0)),
                pl.BlockSpec((K, tn), lambda j, i: (0, off + j))]
    args = [a, w]
    if has_res:
        in_specs.append(pl.BlockSpec((tm, tn), lambda j, i: (i, j)))
        args.append(res)
    return _pcall(
        body, name=name, grid=(cols // tn, T // tm), in_specs=in_specs,
        out_specs=pl.BlockSpec((tm, tn), lambda j, i: (i, j)),
        out_shape=jax.ShapeDtypeStruct((T, cols), out_dtype),
        compiler_params=_params("parallel", "parallel"))(*args)


def _mm_nt(a, w, *, name, out_dtype=F32, rider=None):
    parts = a.shape[0] if a.ndim == 3 else 1
    T, kp = a.shape[-2:]
    N, K = w.shape
    assert K == parts * kp
    tm = min(T, MM_TOKEN_BLOCK)
    tn = _pick(N, MM_WEIGHT_TILE)
    nj, ni = N // tn, T // tm
    rider = rider or _Rider([], [], [], None)
    nri, nro = len(rider.inputs), len(rider.out_shapes)

    def body(a_ref, b_ref, *refs):
        o_ref = refs[nri]
        j, i = pl.program_id(0), pl.program_id(1)
        if nro:
            start, finish = rider.bind(refs[:nri], refs[nri + 1:nri + 1 + nro], refs[nri + 1 + nro:])
            pl.when((j == 0) & (i == 0))(start)
        if a.ndim == 3:
            acc = None
            for p in range(parts):
                term = _dot(a_ref[p], b_ref[:, p * kp:(p + 1) * kp], 1, 1)
                acc = term if acc is None else acc + term
        else:
            acc = _dot(a_ref[...].astype(BF16), b_ref[...], 1, 1)
        o_ref[...] = acc.astype(out_dtype)
        if nro:
            pl.when((j == nj - 1) & (i == ni - 1))(finish)

    a_spec = (pl.BlockSpec((parts, tm, kp), lambda j, i: (0, i, 0)) if a.ndim == 3
              else pl.BlockSpec((tm, K), lambda j, i: (i, 0)))
    sem = ("arbitrary", "arbitrary") if nro else ("parallel", "parallel")
    out = _pcall(
        body, name=name, grid=(nj, ni),
        in_specs=[a_spec, pl.BlockSpec((tn, K), lambda j, i: (j, 0))] + [ANY] * nri,
        out_specs=[pl.BlockSpec((tm, tn), lambda j, i: (i, j))] + [ANY] * nro,
        out_shape=[jax.ShapeDtypeStruct((T, N), out_dtype)] + rider.out_shapes,
        scratch_shapes=rider.scratch,
        compiler_params=_params(*sem))(a, w, *rider.inputs)
    return out if nro else out[0]


def _mm_tn(a, b, *, name, shards=1):
    T, Ko = a.shape
    parts = b.shape[0] if b.ndim == 3 else 1
    T2, n_part = b.shape[-2:]
    N = parts * n_part
    assert T == T2
    n = N // shards
    tt = min(T, MM_CONTRACT_TOKENS)
    tko = _pick(Ko, 1024)
    tn = _pick(n, 1024)
    per = n // tn
    assert n_part % tn == 0
    per_part = n_part // tn

    def body(a_ref, b_ref, o_ref):
        t = pl.program_id(2)
        p = _dot(a_ref[...].astype(BF16), b_ref[...].astype(BF16), 0, 0)

        @pl.when(t == 0)
        def _():
            o_ref[...] = p

        @pl.when(t > 0)
        def _():
            o_ref[...] += p

    if shards == 1:
        out_spec = pl.BlockSpec((tko, tn), lambda i, j, t: (i, j))
        out_shape = jax.ShapeDtypeStruct((Ko, N), F32)
    else:
        out_spec = pl.BlockSpec((None, tko, tn), lambda i, j, t: (j // per, i, j % per))
        out_shape = jax.ShapeDtypeStruct((shards, Ko, n), F32)
    b_spec = (pl.BlockSpec((None, tt, tn), lambda i, j, t: (j // per_part, t, j % per_part)) if b.ndim == 3
              else pl.BlockSpec((tt, tn), lambda i, j, t: (t, j)))
    return _pcall(
        body, name=name, grid=(Ko // tko, N // tn, T // tt),
        in_specs=[pl.BlockSpec((tt, tko), lambda i, j, t: (t, i)), b_spec],
        out_specs=out_spec, out_shape=out_shape,
        compiler_params=_params("parallel", "parallel", "arbitrary"))(a, b)


def _rms_fwd(x, gains, *, name, riders=()):
    T, D = x.shape
    tm = min(T, ROW_BLOCK)
    steps = T // tm
    n, nr = len(gains), len(riders)

    def body(x_ref, *refs):
        g_refs, rider_in = refs[:n], refs[n:n + nr]
        o_refs, rider_out = refs[n + nr:2 * n + nr], refs[2 * n + nr:2 * n + 2 * nr]
        if nr:
            gather = _Gather(rider_in, rider_out, *refs[2 * n + 2 * nr:])
            pl.when(pl.program_id(0) == 0)(gather.start)
        xv = x_ref[...]
        xh = xv * lax.rsqrt(jnp.mean(xv * xv, axis=-1, keepdims=True) + EPS)
        for g_ref, o_ref in zip(g_refs, o_refs):
            o_ref[...] = (xh * g_ref[...]).astype(BF16)
        if nr:
            pl.when(pl.program_id(0) == steps - 1)(gather.finish)

    row = pl.BlockSpec((tm, D), lambda i: (i, 0))
    vec = pl.BlockSpec((1, D), lambda i: (0, 0))
    return _pcall(
        body, name=name, grid=(steps,), in_specs=[row] + [vec] * n + [ANY] * nr,
        out_specs=[row] * n + [ANY] * nr,
        out_shape=[jax.ShapeDtypeStruct((T, D), BF16)] * n
                  + [jax.ShapeDtypeStruct((N_DEV,) + r.shape, r.dtype) for r in riders],
        scratch_shapes=_Gather.scratch(nr) if nr else [],
        compiler_params=_params("arbitrary" if nr else "parallel"))(x, *gains, *riders)


def _rms_bwd(x, dres, dhs, gains, *, name, rider=None):
    T, D = x.shape
    tm = min(T, ROW_BLOCK)
    steps = T // tm
    n = len(gains)
    rider = rider or _Rider([], [], [], None)
    nri, nro = len(rider.inputs), len(rider.out_shapes)

    def body(x_ref, dres_ref, *refs):
        dh_refs, g_refs = refs[:n], refs[n:2 * n]
        refs = refs[2 * n:]
        rider_in, refs = refs[:nri], refs[nri:]
        dx_ref, dg_refs = refs[0], refs[1:1 + n]
        rider_out, sems = refs[1 + n:1 + n + nro], refs[1 + n + nro:]
        i = pl.program_id(0)
        if nro:
            start, finish = rider.bind(rider_in, rider_out, sems)
            pl.when(i == 0)(start)
        xv = x_ref[...]
        r = lax.rsqrt(jnp.mean(xv * xv, axis=-1, keepdims=True) + EPS)
        xh = xv * r
        dxh = jnp.zeros_like(xv)
        for dh_ref, g_ref, dg_ref in zip(dh_refs, g_refs, dg_refs):
            dh = dh_ref[...]
            part = jnp.sum(dh * xh, axis=0, keepdims=True)

            @pl.when(i == 0)
            def _():
                dg_ref[...] = part

            @pl.when(i > 0)
            def _():
                dg_ref[...] += part

            dxh = dxh + dh * g_ref[...]
        dx_ref[...] = dres_ref[...] + r * (dxh - xh * jnp.mean(dxh * xh, axis=-1, keepdims=True))
        if nro:
            pl.when(i == steps - 1)(finish)

    row = pl.BlockSpec((tm, D), lambda i: (i, 0))
    vec = pl.BlockSpec((1, D), lambda i: (0, 0))
    return _pcall(
        body, name=name, grid=(steps,), in_specs=[row, row] + [row] * n + [vec] * n + [ANY] * nri,
        out_specs=[row] + [vec] * n + [ANY] * nro,
        out_shape=[jax.ShapeDtypeStruct((T, D), F32)] + [jax.ShapeDtypeStruct((1, D), F32)] * n
                  + rider.out_shapes,
        scratch_shapes=rider.scratch,
        compiler_params=_params("arbitrary"))(x, dres, *dhs, *gains, *rider.inputs)


def _final_loss(x, target, gain, *, name):
    T, D = x.shape
    tm = min(T, ROW_BLOCK)

    def body(x_ref, t_ref, g_ref, dx_ref, dg_ref, loss_ref):
        i = pl.program_id(0)
        xv = x_ref[...]
        g = g_ref[...]
        r = lax.rsqrt(jnp.mean(xv * xv, axis=-1, keepdims=True) + EPS)
        xh = xv * r
        err = xh * g - t_ref[...]
        part_loss = 0.5 * jnp.sum(jnp.mean(err * err, axis=-1, keepdims=True), axis=0, keepdims=True)
        dy = err * (1.0 / D)
        part_g = jnp.sum(dy * xh, axis=0, keepdims=True)

        @pl.when(i == 0)
        def _():
            dg_ref[...] = part_g
            loss_ref[...] = jnp.broadcast_to(part_loss, loss_ref.shape)

        @pl.when(i > 0)
        def _():
            dg_ref[...] += part_g
            loss_ref[...] += jnp.broadcast_to(part_loss, loss_ref.shape)

        dxh = dy * g
        dx_ref[...] = r * (dxh - xh * jnp.mean(dxh * xh, axis=-1, keepdims=True))

    row = pl.BlockSpec((tm, D), lambda i: (i, 0))
    vec = pl.BlockSpec((1, D), lambda i: (0, 0))
    return _pcall(
        body, name=name, grid=(T // tm,), in_specs=[row, row, vec],
        out_specs=[row, vec, pl.BlockSpec((1, LANES), lambda i: (0, 0))],
        out_shape=[jax.ShapeDtypeStruct((T, D), F32), jax.ShapeDtypeStruct((1, D), F32),
                   jax.ShapeDtypeStruct((1, LANES), F32)],
        compiler_params=_params("arbitrary"))(x, target, gain)


def _shift_down(x, prev_tail, j, row):
    tb = x.shape[0]
    prev = jnp.tile(prev_tail, (tb // SUBLANES, 1))
    return jnp.where(row >= j, pltpu.roll(x, j, 0), pltpu.roll(prev, j, 0))


def _shift_up(x, next_head, j, row):
    tb = x.shape[0]
    nxt = jnp.tile(next_head, (tb // SUBLANES, 1))
    return jnp.where(row < tb - j, pltpu.roll(x, tb - j, 0), pltpu.roll(nxt, tb - j, 0))


def _lru_gates(xb, wr, wi, br, bi, lam):
    xbb = xb.astype(BF16)
    r = _sigmoid(jnp.dot(xbb, wr, preferred_element_type=F32) + br)
    i = _sigmoid(jnp.dot(xbb, wi, preferred_element_type=F32) + bi)
    sp = jnp.maximum(-lam, 0.0) + jnp.log1p(jnp.exp(-jnp.abs(lam)))
    log_a = (-LRU_C) * r * sp
    a = jnp.exp(log_a)
    a2 = a * a
    mult = jnp.sqrt(jnp.maximum(-jnp.tanh(log_a) * (1.0 + a2), 0.0))
    return xbb, r, i, sp, a, a2, mult


def _scan_rows(coef, val, edge, reverse):
    tb, C = coef.shape
    a, b = coef, val
    row = lax.broadcasted_iota(jnp.int32, (tb, C), 0)
    s = 1
    while s < tb:
        m = (row < tb - s) if reverse else (row >= s)
        shift = tb - s if reverse else s
        b = jnp.where(m, a * pltpu.roll(b, shift, 0) + b, b)
        a = jnp.where(m, a * pltpu.roll(a, shift, 0), a)
        s *= 2
    return b + a * edge


def _acore_fwd(proj, conv_w, conv_b, w_r, w_i, b_r, b_i, lam, *, name, riders=()):
    T, C2 = proj.shape
    C = C2 // 2
    nb, bw, _ = w_r.shape
    tb = min(T, SCAN_BLOCK)
    nt = T // tb
    nr = len(riders)

    def body(xp_ref, gate_ref, cw_ref, cb_ref, wr_ref, wi_ref, br_ref, bi_ref, lam_ref, *refs):
        rider_in, refs = refs[:nr], refs[nr:]
        xb_ref, h_ref, yg_ref = refs[:3]
        rider_out, refs = refs[3:3 + nr], refs[3 + nr:]
        tail_ref, hlast_ref = refs[:2]
        t = pl.program_id(1)
        if nr:
            gather = _Gather(rider_in, rider_out, *refs[2:])
            pl.when((pl.program_id(0) == 0) & (t == 0))(gather.start)

        @pl.when(t == 0)
        def _():
            tail_ref[...] = jnp.zeros_like(tail_ref)
            hlast_ref[...] = jnp.zeros_like(hlast_ref)

        row = lax.broadcasted_iota(jnp.int32, (tb, bw), 0)
        xp = xp_ref[...]
        tail = tail_ref[...]
        xb = cb_ref[...] + cw_ref[CONV_W - 1:CONV_W, :] * xp
        for j in range(1, CONV_W):
            xb = xb + cw_ref[CONV_W - 1 - j:CONV_W - j, :] * _shift_down(xp, tail, j, row)
        tail_ref[...] = xp[tb - SUBLANES:, :]
        xb_ref[...] = xb

        _, r, i, sp, a, a2, mult = _lru_gates(xb, wr_ref[...], wi_ref[...], br_ref[...], bi_ref[...],
                                              lam_ref[...])
        h = _scan_rows(a, mult * (i * xb), hlast_ref[SUBLANES - 1:SUBLANES, :], False)
        hlast_ref[...] = h[tb - SUBLANES:, :]
        h_ref[...] = h
        gate = gate_ref[...]
        yg_ref[...] = (h * (gate * _sigmoid(gate))).astype(BF16)
        if nr:
            pl.when((pl.program_id(0) == nb - 1) & (t == nt - 1))(gather.finish)

    blk = lambda off: pl.BlockSpec((tb, bw), lambda n, t: (t, off + n))
    vec = pl.BlockSpec((1, bw), lambda n, t: (0, n))
    wspec = pl.BlockSpec((None, bw, bw), lambda n, t: (n, 0, 0))
    return _pcall(
        body, name=name, grid=(nb, nt),
        in_specs=[blk(0), blk(nb), pl.BlockSpec((CONV_W, bw), lambda n, t: (0, n)), vec, wspec, wspec,
                  vec, vec, vec] + [ANY] * nr,
        out_specs=[blk(0), blk(0), blk(0)] + [ANY] * nr,
        out_shape=[jax.ShapeDtypeStruct((T, C), F32), jax.ShapeDtypeStruct((T, C), F32),
                   jax.ShapeDtypeStruct((T, C), BF16)]
                  + [jax.ShapeDtypeStruct((N_DEV,) + r.shape, r.dtype) for r in riders],
        scratch_shapes=[pltpu.VMEM((SUBLANES, bw), F32), pltpu.VMEM((SUBLANES, bw), F32)]
                       + (_Gather.scratch(nr) if nr else []),
        compiler_params=_params("arbitrary" if nr else "parallel", "arbitrary"))(
            proj, proj, conv_w, conv_b, w_r, w_i, b_r, b_i, lam, *riders)


def _acore_bwd(dyg, proj, xb_all, h_all, conv_w, w_r, w_i, b_r, b_i, lam, *, name, rider=None):
    T, C2 = proj.shape
    C = C2 // 2
    nb, bw, _ = w_r.shape
    tb = min(T, SCAN_BLOCK)
    nt = T // tb
    per8 = tb // SUBLANES
    rider = rider or _Rider([], [], [], None)
    nri, nro = len(rider.inputs), len(rider.out_shapes)

    def body(dyg_ref, xp_ref, gate_ref, xb_ref, h_ref, xp_prev_ref, h_prev_ref, cw_ref,
             wr_ref, wi_ref, br_ref, bi_ref, lam_ref, *refs):
        rider_in, refs = refs[:nri], refs[nri:]
        dproj_ref, dcw_ref, dcb_ref, dbr_ref, dbi_ref, dlam_ref, dwr_ref, dwi_ref = refs[:8]
        rider_out, refs = refs[8:8 + nro], refs[8 + nro:]
        gh_next_ref, a_next_ref, dxb_next_ref = refs[:3]
        step = pl.program_id(1)
        first_block = step == nt - 1
        if nro:
            start, finish = rider.bind(rider_in, rider_out, refs[3:])
            pl.when((pl.program_id(0) == 0) & (step == 0))(start)

        @pl.when(step == 0)
        def _():
            gh_next_ref[...] = jnp.zeros_like(gh_next_ref)
            a_next_ref[...] = jnp.zeros_like(a_next_ref)
            dxb_next_ref[...] = jnp.zeros_like(dxb_next_ref)

        row = lax.broadcasted_iota(jnp.int32, (tb, bw), 0)
        keep = jnp.where(first_block, 0.0, 1.0)
        h_prev = h_prev_ref[...] * keep
        xp_prev = xp_prev_ref[...] * keep
        xp, gate, xb, h, dyg_v = xp_ref[...], gate_ref[...], xb_ref[...], h_ref[...], dyg_ref[...]
        lam_v = lam_ref[...]
        wr, wi = wr_ref[...], wi_ref[...]

        sg = _sigmoid(gate)
        dh = dyg_v * (gate * sg)
        dproj_ref[1] = (dyg_v * h * (sg * (1.0 + gate * (1.0 - sg)))).astype(BF16)

        xbb, r, i, sp, a, a2, mult = _lru_gates(xb, wr, wi, br_ref[...], bi_ref[...], lam_v)

        gh = _scan_rows(_shift_up(a, a_next_ref[...], 1, row), dh, gh_next_ref[0:1, :], True)
        gh_next_ref[...] = gh[0:SUBLANES, :]
        a_next_ref[...] = a[0:SUBLANES, :]

        da = gh * _shift_down(h, h_prev, 1, row)
        dmult = gh * (i * xb)
        di = gh * mult * xb
        dxb = gh * mult * i
        dla = da * a - dmult * jnp.where(mult > 0.0, a2 / mult, 0.0)
        dr = dla * ((-LRU_C) * sp)
        dsp = jnp.sum(dla * ((-LRU_C) * r), axis=0, keepdims=True)
        dlam_part = dsp * (-_sigmoid(-lam_v))
        dpr = dr * r * (1.0 - r)
        dpi = di * i * (1.0 - i)
        dbr_part = jnp.sum(dpr, axis=0, keepdims=True)
        dbi_part = jnp.sum(dpi, axis=0, keepdims=True)
        dprb, dpib = dpr.astype(BF16), dpi.astype(BF16)
        dwr_part = _dot(xbb, dprb, 0, 0)
        dwi_part = _dot(xbb, dpib, 0, 0)
        dxb = dxb + _dot(dprb, wr, 1, 1) + _dot(dpib, wi, 1, 1)

        dxb_next = dxb_next_ref[...]
        dxp = cw_ref[CONV_W - 1:CONV_W, :] * dxb
        for j in range(1, CONV_W):
            dxp = dxp + cw_ref[CONV_W - 1 - j:CONV_W - j, :] * _shift_up(dxb, dxb_next, j, row)
        dxb_next_ref[...] = dxb[0:SUBLANES, :]
        dproj_ref[0] = dxp.astype(BF16)
        dcb_part = jnp.sum(dxb, axis=0, keepdims=True)
        dcw_rows = []
        for k in range(CONV_W):
            j = CONV_W - 1 - k
            sh = xp if j == 0 else _shift_down(xp, xp_prev, j, row)
            dcw_rows.append(jnp.sum(dxb * sh, axis=0, keepdims=True))

        @pl.when(step == 0)
        def _():
            for k in range(CONV_W):
                dcw_ref[k:k + 1, :] = dcw_rows[k]
            dcb_ref[...] = dcb_part
            dbr_ref[...] = dbr_part
            dbi_ref[...] = dbi_part
            dlam_ref[...] = dlam_part
            dwr_ref[...] = dwr_part
            dwi_ref[...] = dwi_part

        @pl.when(step > 0)
        def _():
            for k in range(CONV_W):
                dcw_ref[k:k + 1, :] += dcw_rows[k]
            dcb_ref[...] += dcb_part
            dbr_ref[...] += dbr_part
            dbi_ref[...] += dbi_part
            dlam_ref[...] += dlam_part
            dwr_ref[...] += dwr_part
            dwi_ref[...] += dwi_part

        if nro:
            pl.when((pl.program_id(0) == nb - 1) & (step == nt - 1))(finish)

    rev = lambda s: nt - 1 - s
    blk = lambda off: pl.BlockSpec((tb, bw), lambda n, s: (rev(s), off + n))
    prev8 = lambda off: pl.BlockSpec(
        (SUBLANES, bw), lambda n, s: (jnp.maximum(rev(s) * per8 - 1, 0), off + n))
    vec = pl.BlockSpec((1, bw), lambda n, s: (0, n))
    wspec = pl.BlockSpec((None, bw, bw), lambda n, s: (n, 0, 0))
    cwspec = pl.BlockSpec((CONV_W, bw), lambda n, s: (0, n))
    vshape = jax.ShapeDtypeStruct((1, C), F32)
    wshape = jax.ShapeDtypeStruct((nb, bw, bw), F32)
    return _pcall(
        body, name=name, grid=(nb, nt),
        in_specs=[blk(0), blk(0), blk(nb), blk(0), blk(0), prev8(0), prev8(0), cwspec,
                  wspec, wspec, vec, vec, vec] + [ANY] * nri,
        out_specs=[pl.BlockSpec((2, tb, bw), lambda n, s: (0, rev(s), n)), cwspec, vec, vec, vec, vec,
                   wspec, wspec] + [ANY] * nro,
        out_shape=[jax.ShapeDtypeStruct((2, T, C), BF16),
                   jax.ShapeDtypeStruct((CONV_W, C), F32), vshape, vshape, vshape, vshape,
                   wshape, wshape] + rider.out_shapes,
        scratch_shapes=[pltpu.VMEM((SUBLANES, bw), F32)] * 3 + rider.scratch,
        compiler_params=_params("arbitrary" if nro else "parallel", "arbitrary"))(
            dyg, proj, proj, xb_all, h_all, proj, h_all, conv_w, w_r, w_i, b_r, b_i, lam, *rider.inputs)


def _later_sum(lk, tri):
    return jnp.dot(lk.astype(BF16), tri, preferred_element_type=F32)


def _log2_sigmoids(y):
    t = jnp.log(1.0 + jnp.exp2(-jnp.abs(y))) * LOG2E
    ls = jnp.minimum(y, 0.0) - t
    return ls, ls - y


def _attn_blocks(T):
    bk = min(T, ATT_KEY_BLOCK)
    bq = min(T, ATT_QUERY_BLOCK)
    return bq, bk, bq // bk


def _attn_fwd(q, kv, gate, *, name):
    T, HD = q.shape
    H = HD // HEAD_DIM
    bq, bk, per = _attn_blocks(T)
    scale = 1.0 / math.sqrt(HEAD_DIM)

    def body(q_ref, k_ref, v_ref, g_ref, o_ref, og_ref, lt_ref, w_ref):
        i = pl.program_id(1)
        qv = q_ref[...]
        tr = lax.broadcasted_iota(jnp.int32, (bk, bk), 0)
        tc = lax.broadcasted_iota(jnp.int32, (bk, bk), 1)
        tri = (tr > tc).astype(BF16)
        ahead = (lax.broadcasted_iota(jnp.int32, (bq, bk), 0)
                 - lax.broadcasted_iota(jnp.int32, (bq, bk), 1))

        def starts_of(top):
            return [pl.multiple_of((top - d) * bk, bk) for d in range(per)]

        def scores(top):
            return [_dot(qv, k_ref[pl.ds(ks, bk), :], 1, 1) for ks in starts_of(top)]

        def weights(top, zs, c, mask):
            lss, sums, css, causals = [], [], [], []
            for ks, z in zip(starts_of(top), zs):
                ls, lk = _log2_sigmoids(z * (scale * LOG2E))
                if mask:
                    causals.append(ahead > ks - i * bq)
                    lk = jnp.where(causals[-1], lk, 0.0)
                lss.append(ls)
                sums.append(jnp.sum(lk, axis=1, keepdims=True))
                css.append(_later_sum(lk, tri))
            for d in range(per):
                w = jnp.exp2(lss[d] + (css[d] + c))
                if mask:
                    w = jnp.where(causals[d], w, 0.0)
                w_ref[d] = w.astype(BF16)
                c = c + sums[d]
            return c

        def values(top, acc):
            for d, ks in enumerate(starts_of(top)):
                acc = acc + jnp.dot(w_ref[d], v_ref[pl.ds(ks, bk), :], preferred_element_type=F32)
            return acc

        def more(state):
            gg, _, _, largest = state
            return (gg <= i) & (largest > WEIGHT_FLOOR_LOG2)

        def step(state):
            gg, acc, c, _ = state
            top = (i - gg) * per + per - 1
            zs = scores(top)
            acc = values(top + per, acc)
            c = weights(top, zs, c, False)
            return gg + 1, acc, c, jnp.max(c)

        diag_top = i * per + per - 1
        c = weights(diag_top, scores(diag_top), jnp.zeros((bq, 1), F32), True)
        gg, acc, c, _ = lax.while_loop(more, step, (1, jnp.zeros((bq, HEAD_DIM), F32), c, jnp.max(c)))
        acc = values((i - gg + 1) * per + per - 1, acc)
        o_ref[...] = acc
        g = g_ref[...]
        og_ref[...] = (acc * (g * _sigmoid(g))).astype(BF16)
        lane = lax.broadcasted_iota(jnp.int32, (bq, HEAD_DIM), 1)
        lt_ref[...] = jnp.where(lane == 1, (i - gg + 1).astype(F32), jnp.broadcast_to(c, (bq, HEAD_DIM)))

    qspec = pl.BlockSpec((bq, HEAD_DIM), lambda h, i: (i, h))
    return _pcall(
        body, name=name, grid=(H, T // bq),
        in_specs=[qspec, pl.BlockSpec((T, HEAD_DIM), lambda h, i: (0, h)),
                  pl.BlockSpec((T, HEAD_DIM), lambda h, i: (0, H + h)), qspec],
        out_specs=[qspec, qspec, qspec],
        out_shape=[jax.ShapeDtypeStruct((T, HD), F32), jax.ShapeDtypeStruct((T, HD), BF16),
                   jax.ShapeDtypeStruct((T, HD), F32)],
        scratch_shapes=[pltpu.VMEM((per, bq, bk), BF16)],
        compiler_params=_params("parallel", "arbitrary"))(q, kv, kv, gate)


def _attn_bwd(q, kv, gate, o, ltot, dog, *, name):
    T, HD = q.shape
    H = HD // HEAD_DIM
    bq, bk, per = _attn_blocks(T)
    nq = T // bq
    scale = 1.0 / math.sqrt(HEAD_DIM)

    def body(q_ref, k_ref, v_ref, g_ref, o_ref, lt_ref, dog_ref,
             dqg_ref, dkv_ref, dk_acc, dv_acc, dz_ref, w_ref):
        i = pl.program_id(1)

        @pl.when(i == 0)
        def _():
            dk_acc[...] = jnp.zeros_like(dk_acc)
            dv_acc[...] = jnp.zeros_like(dv_acc)

        qv = q_ref[...]
        g, ov, dogv = g_ref[...], o_ref[...], dog_ref[...]
        sg = _sigmoid(g)
        do = dogv * (g * sg)
        dqg_ref[1] = (dogv * ov * (sg * (1.0 + g * (1.0 - sg)))).astype(BF16)
        dob = do.astype(BF16)
        ltot_v = lt_ref[:, 0:1]
        tr = lax.broadcasted_iota(jnp.int32, (bk, bk), 0)
        tc = lax.broadcasted_iota(jnp.int32, (bk, bk), 1)
        tri_later = (tr > tc).astype(BF16)
        tri_excl = (tr < tc).astype(BF16)
        ahead = (lax.broadcasted_iota(jnp.int32, (bq, bk), 0)
                 - lax.broadcasted_iota(jnp.int32, (bq, bk), 1))

        def starts_of(first):
            return [pl.multiple_of((first + d) * bk, bk) for d in range(per)]

        def scores(first):
            return ([_dot(qv, k_ref[pl.ds(ks, bk), :], 1, 1) for ks in starts_of(first)],
                    [_dot(dob, v_ref[pl.ds(ks, bk), :], 1, 1) for ks in starts_of(first)])

        def front(first, zs, dws, p_lk, p_g, mask):
            lss, css, causals = [], [], []
            for ks, z in zip(starts_of(first), zs):
                ls, lk = _log2_sigmoids(z * (scale * LOG2E))
                if mask:
                    causals.append(ahead > ks - i * bq)
                    lk = jnp.where(causals[-1], lk, 0.0)
                lss.append(ls)
                p_lk = p_lk + jnp.sum(lk, axis=1, keepdims=True)
                css.append((ltot_v - p_lk) + _later_sum(lk, tri_later))
            gms, befores = [], []
            for d in range(per):
                w = jnp.exp2(lss[d] + css[d])
                if mask:
                    w = jnp.where(causals[d], w, 0.0)
                gm = dws[d] * w
                gms.append(gm)
                w_ref[d] = w.astype(BF16)
                befores.append(jnp.dot(gm.astype(BF16), tri_excl, preferred_element_type=F32) + p_g)
                p_g = p_g + jnp.sum(gm, axis=1, keepdims=True)
            for d in range(per):
                dz = gms[d] - jnp.exp2(lss[d]) * (gms[d] + befores[d])
                if mask:
                    dz = jnp.where(causals[d], dz, 0.0)
                dz_ref[d] = (dz * scale).astype(BF16)
            return p_lk, p_g

        def back(first, dq):
            for d, ks in enumerate(starts_of(first)):
                dzb = dz_ref[d]
                dq = dq + jnp.dot(dzb, k_ref[pl.ds(ks, bk), :], preferred_element_type=F32)
                dk_acc[pl.ds(ks, bk), :] += _dot(dzb, qv, 0, 0)
                dv_acc[pl.ds(ks, bk), :] += _dot(w_ref[d], dob, 0, 0)
            return dq

        def step(mask):
            def trip(g, state):
                dq, p_lk, p_g = state
                zs, dws = scores(g * per)
                dq = back((g - 1) * per, dq)
                return (dq,) + front(g * per, zs, dws, p_lk, p_g, mask)
            return trip

        g0 = jnp.max(lt_ref[0:1, 1:2]).astype(jnp.int32)
        zero = jnp.zeros((bq, 1), F32)
        state = (jnp.zeros((bq, HEAD_DIM), F32),) + front(g0 * per, *scores(g0 * per), zero, zero, True)
        state = lax.fori_loop(g0 + 1, i, step(False), state)
        state = lax.fori_loop(jnp.maximum(i, g0 + 1), i + 1, step(True), state)
        dqg_ref[0] = back(i * per, state[0]).astype(BF16)

        @pl.when(i == nq - 1)
        def _():
            dkv_ref[0] = dk_acc[...].astype(BF16)
            dkv_ref[1] = dv_acc[...].astype(BF16)

    qspec = pl.BlockSpec((bq, HEAD_DIM), lambda h, i: (i, h))
    kspec = pl.BlockSpec((T, HEAD_DIM), lambda h, i: (0, h))
    return _pcall(
        body, name=name, grid=(H, nq),
        in_specs=[qspec, kspec, pl.BlockSpec((T, HEAD_DIM), lambda h, i: (0, H + h)),
                  qspec, qspec, qspec, qspec],
        out_specs=[pl.BlockSpec((2, bq, HEAD_DIM), lambda h, i: (0, i, h)),
                   pl.BlockSpec((2, T, HEAD_DIM), lambda h, i: (0, 0, h))],
        out_shape=[jax.ShapeDtypeStruct((2, T, HD), BF16)] * 2,
        scratch_shapes=[pltpu.VMEM((T, HEAD_DIM), F32)] * 2 + [pltpu.VMEM((per, bq, bk), BF16)] * 2,
        compiler_params=_params("parallel", "arbitrary"))(q, kv, kv, gate, o, ltot, dog)


def _position():
    return lax.axis_index("x"), lax.axis_index("y"), lax.axis_index("c")


def _chip_of(k, x, y):
    return (1 - x if k & 1 else x), (1 - y if k & 2 else y)


class _Gather:
    @staticmethod
    def scratch(n):
        return [pltpu.SemaphoreType.DMA((n, 7)), pltpu.SemaphoreType.DMA((n, 7)),
                pltpu.SemaphoreType.DMA((n,))]

    def __init__(self, ins, outs, send_sems, recv_sems, local_sems):
        self.ins, self.outs, self.n = ins, outs, len(ins)
        self.send_sems, self.recv_sems, self.local_sems = send_sems, recv_sems, local_sems
        x, y, c = _position()
        self.me, self.sibling = (x, y, c), (x, y, 1 - c)
        self.chips = [_chip_of(k, x, y) for k in (1, 2, 3)]

    def copy(self, a, k, block, to, src=None):
        slot = self.outs[a].at[4 * block[0] + 2 * block[1] + block[2]]
        return pltpu.make_async_remote_copy(
            src_ref=slot if src is None else src, dst_ref=slot,
            send_sem=self.send_sems.at[a, k], recv_sem=self.recv_sems.at[a, k],
            device_id=to, device_id_type=MESH)

    def own_copies(self):
        x, y, c = self.me
        mine = [pltpu.make_async_copy(self.ins[a], self.outs[a].at[4 * x + 2 * y + c], self.local_sems.at[a])
                for a in range(self.n)]
        first = []
        for a in range(self.n):
            first.append(self.copy(a, 0, self.me, self.sibling, src=self.ins[a]))
            first += [self.copy(a, 1 + j, self.me, (*chip, c), src=self.ins[a])
                      for j, chip in enumerate(self.chips)]
        return mine, first

    def start(self):
        mine, first = self.own_copies()
        for cp in mine + first:
            cp.start()

    def finish(self):
        c = self.me[2]
        mine, first = self.own_copies()
        passed = []
        for j, chip in enumerate(self.chips):
            for a in range(self.n):
                self.copy(a, 1 + j, (*chip, c), self.me).wait_recv()
                fwd = self.copy(a, 4 + j, (*chip, c), self.sibling)
                fwd.start()
                passed.append(fwd)
        for a in range(self.n):
            self.copy(a, 0, self.sibling, self.me).wait_recv()
            for j, chip in enumerate(self.chips):
                self.copy(a, 4 + j, (*chip, 1 - c), self.me).wait_recv()
        for cp in first + passed:
            cp.wait_send()
        for cp in mine:
            cp.wait()


class _Rider:
    def __init__(self, inputs, out_shapes, scratch, copies):
        self.inputs, self.out_shapes, self.scratch, self.copies = inputs, out_shapes, scratch, copies

    def bind(self, ins, outs, sems):
        def start():
            for cp in self.copies(ins, outs, sems):
                cp.start()

        def finish():
            cps = self.copies(ins, outs, sems)
            for cp in cps:
                cp.wait_send()
            for cp in cps:
                cp.wait_recv()

        return start, finish


def _sibling_rider(grads):
    n = len(grads)

    def copies(ins, outs, sems):
        x, y, c = _position()
        return [pltpu.make_async_remote_copy(
            src_ref=ins[a].at[2 * chip + (1 - c)], dst_ref=outs[a].at[chip],
            send_sem=sems[0].at[a, chip], recv_sem=sems[1].at[a, chip],
            device_id=(x, y, 1 - c), device_id_type=MESH) for a in range(n) for chip in range(4)]

    return _Rider(list(grads), [jax.ShapeDtypeStruct((4,) + g.shape[1:], g.dtype) for g in grads],
                  [pltpu.SemaphoreType.DMA((n, 4)), pltpu.SemaphoreType.DMA((n, 4))], copies)


def _chips_rider(parts):
    n = len(parts)

    def copies(ins, outs, sems):
        x, y, c = _position()
        cps = []
        for a in range(n):
            for k in range(3):
                cx, cy = _chip_of(k + 1, x, y)
                cps.append(pltpu.make_async_remote_copy(
                    src_ref=ins[a].at[2 * cx + cy], dst_ref=outs[a].at[k],
                    send_sem=sems[0].at[a, k], recv_sem=sems[1].at[a, k],
                    device_id=(cx, cy, c), device_id_type=MESH))
        return cps

    return _Rider(list(parts), [jax.ShapeDtypeStruct((3,) + p.shape[1:], p.dtype) for p in parts],
                  [pltpu.SemaphoreType.DMA((n, 3)), pltpu.SemaphoreType.DMA((n, 3))], copies)


def _small_gather(small):
    def body(small_ref, small_all, send_sems, recv_sems, local_sem):
        x, y, c = _position()
        me = 4 * x + 2 * y + c
        peers = [(x ^ (m >> 2), y ^ ((m >> 1) & 1), c ^ (m & 1)) for m in range(1, N_DEV)]
        sends = [pltpu.make_async_remote_copy(
            src_ref=small_ref, dst_ref=small_all.at[me], send_sem=send_sems.at[m], recv_sem=recv_sems.at[m],
            device_id=peer, device_id_type=MESH) for m, peer in enumerate(peers)]
        own = pltpu.make_async_copy(small_ref, small_all.at[me], local_sem)
        for cp in sends + [own]:
            cp.start()
        for cp in sends:
            cp.wait_send()
        for m, (px, py, pc) in enumerate(peers):
            pltpu.make_async_remote_copy(
                src_ref=small_ref, dst_ref=small_all.at[4 * px + 2 * py + pc],
                send_sem=send_sems.at[m], recv_sem=recv_sems.at[m],
                device_id=(px, py, pc), device_id_type=MESH).wait_recv()
        own.wait()

    return _pcall(
        body, name="small_gather", in_specs=[ANY], out_specs=ANY,
        out_shape=jax.ShapeDtypeStruct((N_DEV,) + small.shape, small.dtype),
        scratch_shapes=[pltpu.SemaphoreType.DMA((7,)), pltpu.SemaphoreType.DMA((7,)),
                        pltpu.SemaphoreType.DMA])(small)


def _pair_sum(grad, got, *, name):
    _, R, C = got.shape
    tr = _pick8(R, max(2 * SUBLANES, (1 << 17) // C))

    def body(g_ref, b_ref, own_ref, ob_ref):
        north = lax.axis_index("c") == 1
        x1, y1 = lax.axis_index("x") == 1, lax.axis_index("y") == 1
        sums = []
        for chip in range(4):
            sums.append(jnp.where(north, g_ref[chip, 1], g_ref[chip, 0]) + b_ref[chip])
            ob_ref[chip] = sums[-1].astype(BF16)
        own_ref[...] = jnp.where(x1, jnp.where(y1, sums[3], sums[2]), jnp.where(y1, sums[1], sums[0]))

    spec = pl.BlockSpec((4, tr, C), lambda i: (0, i, 0))
    return _pcall(
        body, name=name, grid=(R // tr,),
        in_specs=[pl.BlockSpec((4, 2, tr, C), lambda i: (0, 0, i, 0)), spec],
        out_specs=[pl.BlockSpec((tr, C), lambda i: (i, 0)), spec],
        out_shape=[jax.ShapeDtypeStruct((R, C), F32), jax.ShapeDtypeStruct((4, R, C), BF16)],
        compiler_params=_params("parallel"))(grad.reshape(4, 2, R, C), got)


def _pick8(n, cap):
    if n <= cap:
        return n
    best = None
    for t in range(SUBLANES, cap + 1, SUBLANES):
        if n % t == 0:
            best = t
    assert best is not None, (n, cap)
    return best


def _adamw(w, m, v, parts, *, name):
    R, C = w.shape
    tr = _pick8(R, max(SUBLANES, (1 << 17) // C))
    c1 = 1.0 - ADAM_B1 ** ADAM_STEP
    c2 = 1.0 - ADAM_B2 ** ADAM_STEP
    np_ = len(parts)

    def body(w_ref, m_ref, v_ref, *refs):
        p_refs = refs[:np_]
        g_ref, d_ref, nm_ref, nv_ref = refs[np_:]
        g = None
        for p_ref in p_refs:
            terms = [p_ref[...]] if len(p_ref.shape) == 2 else [p_ref[k] for k in range(p_ref.shape[0])]
            for t in terms:
                g = t.astype(F32) if g is None else g + t.astype(F32)
        mn = ADAM_B1 * m_ref[...] + (1.0 - ADAM_B1) * g
        vn = ADAM_B2 * v_ref[...] + (1.0 - ADAM_B2) * (g * g)
        d_ref[...] = -ADAM_LR * ((mn / c1) / (jnp.sqrt(vn / c2) + ADAM_EPS) + ADAM_WD * w_ref[...])
        g_ref[...] = g
        nm_ref[...] = mn
        nv_ref[...] = vn

    spec = pl.BlockSpec((tr, C), lambda i: (i, 0))
    pspecs = [spec if p.ndim == 2 else pl.BlockSpec((p.shape[0], tr, C), lambda i: (0, i, 0)) for p in parts]
    return _pcall(
        body, name=name, grid=(R // tr,), in_specs=[spec] * 3 + pspecs, out_specs=[spec] * 4,
        out_shape=[jax.ShapeDtypeStruct((R, C), F32)] * 4,
        compiler_params=_params("parallel"))(w, m, v, *parts)


def _rows(a):
    return a.reshape(-1, LANES)


def _whole_from_columns(shards, *, name):
    S, K, n = shards.shape
    tk = _pick8(K, 1024)

    def body(s_ref, o_ref):
        o_ref[...] = s_ref[...]

    return _pcall(
        body, name=name, grid=(K // tk, S),
        in_specs=[pl.BlockSpec((None, tk, n), lambda i, s: (s, i, 0))],
        out_specs=pl.BlockSpec((tk, n), lambda i, s: (i, s)),
        out_shape=jax.ShapeDtypeStruct((K, S * n), shards.dtype),
        compiler_params=_params("parallel", "parallel"))(shards)


def _late_weights(a_w_out_rows, w_kv_cols, b_w_in_cols, b_w_out_rows):
    whole_rows = lambda g: g.reshape(g.shape[0] * g.shape[1], g.shape[2])
    return (whole_rows(a_w_out_rows), _whole_from_columns(w_kv_cols, name="w_kv_whole"),
            _whole_from_columns(b_w_in_cols, name="b_w_in_whole"), whole_rows(b_w_out_rows))


def _forward_backward(xs, target, a_norm, g_a_w_in, conv_w, conv_b, g_w_r, g_w_i, b_r, b_i, lam,
                      kv_norm, b_norm, final_norm, *, late_weights=None, late_shards=None, h_a=None):
    if h_a is None:
        (h_a,) = _rms_fwd(xs, [a_norm], name="a_norm_fwd")
    proj_a = _mm_nn(h_a, g_a_w_in, name="a_in_proj", out_dtype=F32)
    xb, h_rec, yg, *gathered = _acore_fwd(proj_a, conv_w, conv_b, g_w_r, g_w_i, b_r, b_i, lam,
                                          name="a_core_fwd", riders=late_shards or ())
    g_a_w_out, g_w_kv, g_b_w_in, g_b_w_out = _late_weights(*gathered) if late_shards else late_weights
    x1 = _mm_nn(yg, g_a_w_out, name="a_out_proj", out_dtype=F32, res=xs)
    hk, hb = _rms_fwd(x1, [kv_norm, b_norm], name="kv_b_norm_fwd")
    kv = _mm_nn(hk, g_w_kv, name="kv_proj", out_dtype=BF16)
    hd = g_b_w_in.shape[1] // 2
    q = _mm_nn(hb, g_b_w_in, name="q_proj", out_dtype=BF16, col_off=0, cols=hd)
    gate_b = _mm_nn(hb, g_b_w_in, name="b_gate_proj", out_dtype=F32, col_off=hd, cols=hd)
    o, og, ltot = _attn_fwd(q, kv, gate_b, name="attn_fwd")
    x2 = _mm_nn(og, g_b_w_out, name="b_out_proj", out_dtype=F32, res=x1)
    dx2, d_final_norm, loss_part = _final_loss(x2, target, final_norm, name="final_norm_loss")

    dog = _mm_nt(dx2, g_b_w_out, name="b_out_proj_bwd")
    dw_b_out = _mm_tn(og, dx2, name="b_out_proj_wgrad")
    dproj_b, dkv = _attn_bwd(q, kv, gate_b, o, ltot, dog, name="attn_bwd")
    dhb = _mm_nt(dproj_b, g_b_w_in, name="b_in_proj_bwd")
    dw_b_in = _mm_tn(hb, dproj_b, name="b_in_proj_wgrad", shards=N_DEV)
    dhk = _mm_nt(dkv, g_w_kv, name="kv_proj_bwd")
    dw_kv = _mm_tn(hk, dkv, name="kv_proj_wgrad", shards=N_DEV)
    early = [dw_kv, dw_b_in, dw_b_out.reshape(N_DEV, -1, dw_b_out.shape[1])] if late_shards else []
    dx1, d_b_norm, d_kv_norm, *got = _rms_bwd(x1, dx2, [dhb, dhk], [b_norm, kv_norm], name="kv_b_norm_bwd",
                                              rider=_sibling_rider(early) if early else None)
    early_sums = [_pair_sum(f_, g_, name=f"pair_sum_early_{i}") for i, (f_, g_) in enumerate(zip(early, got))]
    dyg = _mm_nt(dx1, g_a_w_out, name="a_out_proj_bwd")
    dw_a_out = _mm_tn(yg, dx1, name="a_out_proj_wgrad")
    (dproj_a, d_conv_w, d_conv_b, d_b_r, d_b_i, d_lambda, dw_r, dw_i, *early_others) = _acore_bwd(
        dyg, proj_a, xb, h_rec, conv_w, g_w_r, g_w_i, b_r, b_i, lam, name="a_core_bwd",
        rider=_chips_rider([s[1] for s in early_sums]) if early else None)
    dw_a_in = _mm_tn(h_a, dproj_a, name="a_in_proj_wgrad", shards=N_DEV)
    rows = dw_r.shape[1] // N_DEV
    lru = lambda dw: dw.reshape(-1, N_DEV, rows, dw.shape[2]).transpose(1, 0, 2, 3).reshape(N_DEV, -1, dw.shape[2])
    late = [dw_a_in, dw_a_out.reshape(N_DEV, -1, dw_a_out.shape[1]), lru(dw_r), lru(dw_i)] if late_shards else []
    dh_a, *got = _mm_nt(dproj_a, g_a_w_in, name="a_in_proj_bwd", rider=_sibling_rider(late)) if late else (
        _mm_nt(dproj_a, g_a_w_in, name="a_in_proj_bwd"),)
    late_sums = [_pair_sum(f_, g_, name=f"pair_sum_late_{i}") for i, (f_, g_) in enumerate(zip(late, got))]
    grad_x, d_a_norm, *late_others = _rms_bwd(xs, dx1, [dh_a], [a_norm], name="a_norm_bwd",
                                              rider=_chips_rider([s[1] for s in late_sums]) if late else None)
    sums = late_sums[:2] + early_sums + late_sums[2:]
    others = late_others[:2] + early_others + late_others[2:]
    return (loss_part, grad_x, dw_a_in, dw_a_out, dw_kv, dw_b_in, dw_b_out, dw_r, dw_i, d_a_norm,
            d_conv_w, d_conv_b, d_b_r, d_b_i, d_lambda, d_kv_norm, d_b_norm, d_final_norm, sums, others)


def kernel(x, a_norm, a_w_in, a_conv_w, a_conv_b, a_w_r, a_b_r, a_w_i, a_b_i, a_lambda, a_w_out, kv_norm, w_kv, b_norm, b_w_in, b_w_out, final_norm, loss_target, m_a_norm, m_a_w_in, m_a_conv_w, m_a_conv_b, m_a_w_r, m_a_b_r, m_a_w_i, m_a_b_i, m_a_lambda, m_a_w_out, m_kv_norm, m_w_kv, m_b_norm, m_b_w_in, m_b_w_out, m_final_norm, v_a_norm, v_a_w_in, v_a_conv_w, v_a_conv_b, v_a_w_r, v_a_b_r, v_a_w_i, v_a_b_i, v_a_lambda, v_a_w_out, v_kv_norm, v_w_kv, v_b_norm, v_b_w_in, v_b_w_out, v_final_norm):
    T, D = x.shape[1], x.shape[2]
    nb, bw = a_w_r.shape[1], a_w_r.shape[3]
    C = nb * bw
    me = 4 * lax.axis_index("x") + 2 * lax.axis_index("y") + lax.axis_index("c")
    xs = x[0]
    target = loss_target[0]

    rows_r = a_w_r.shape[2]
    small_f32 = jnp.concatenate([_rows(a_conv_w[0]), _rows(b_norm[0])], axis=0)
    pad = (-small_f32.shape[0]) % SUBLANES
    small_f32 = jnp.pad(small_f32, ((0, pad), (0, 0)))
    h_a, a_w_in_cols, w_r_rows, w_i_rows, small_all = _rms_fwd(
        xs, [a_norm], name="a_norm_fwd",
        riders=[a_w_in[0].astype(BF16), a_w_r[0].reshape(nb * rows_r, bw).astype(BF16),
                a_w_i[0].reshape(nb * rows_r, bw).astype(BF16), small_f32])
    late_shards = [a_w_out[0].astype(BF16), w_kv.astype(BF16), b_w_in[0].astype(BF16), b_w_out[0].astype(BF16)]
    g_a_w_in = _whole_from_columns(a_w_in_cols, name="a_w_in_whole")
    g_w_r = w_r_rows.reshape(N_DEV, nb, rows_r, bw).transpose(1, 0, 2, 3).reshape(nb, bw, bw)
    g_w_i = w_i_rows.reshape(N_DEV, nb, rows_r, bw).transpose(1, 0, 2, 3).reshape(nb, bw, bw)
    cw_rows = a_conv_w.shape[1] * a_conv_w.shape[2] // LANES
    conv_w_full = small_all[:, :cw_rows, :].reshape(N_DEV, CONV_W, a_conv_w.shape[2])
    conv_w_full = conv_w_full.transpose(1, 0, 2).reshape(CONV_W, C)
    bn_rows = b_norm.shape[1] // LANES
    b_norm_full = small_all[:, cw_rows:cw_rows + bn_rows, :].reshape(1, D)
    kv_norm2, final_norm2 = kv_norm.reshape(1, D), final_norm.reshape(1, D)

    (loss_part, grad_x, dw_a_in, dw_a_out, dw_kv, dw_b_in, dw_b_out, dw_r, dw_i, d_a_norm, d_conv_w,
     d_conv_b, d_b_r, d_b_i, d_lambda, d_kv_norm, d_b_norm, d_final_norm, sums,
     others) = _forward_backward(
         xs, target, a_norm, g_a_w_in, conv_w_full, a_conv_b, g_w_r, g_w_i, a_b_r, a_b_i, a_lambda,
         kv_norm2, b_norm_full, final_norm2, late_shards=late_shards, h_a=h_a)

    small_parts = [d_a_norm, d_conv_w, d_conv_b, d_b_r, d_b_i, d_lambda, d_kv_norm, d_b_norm, d_final_norm]
    small_sizes = [p.size // LANES for p in small_parts]
    small = jnp.concatenate([_rows(p) for p in small_parts], axis=0)
    small_everyone = _small_gather(small)

    def shard2d(w):
        return w.reshape(-1, w.shape[-1])

    names_big = [(a_w_in, m_a_w_in, v_a_w_in), (a_w_out, m_a_w_out, v_a_w_out), (w_kv, m_w_kv, v_w_kv),
                 (b_w_in, m_b_w_in, v_b_w_in), (b_w_out, m_b_w_out, v_b_w_out),
                 (a_w_r, m_a_w_r, v_a_w_r), (a_w_i, m_a_w_i, v_a_w_i)]
    upd_big = []
    for i, (w, m, v) in enumerate(names_big):
        res = _adamw(shard2d(w), shard2d(m), shard2d(v), [sums[i][0], others[i]], name=f"adamw_{i}")
        upd_big.append([r.reshape(w.shape) for r in res])

    soffs = [0]
    for s in small_sizes:
        soffs.append(soffs[-1] + s)

    def small_piece(i):
        return small_everyone[:, soffs[i]:soffs[i + 1], :]

    cw_cols = a_conv_w.shape[2]
    conv_piece = small_piece(1).reshape(N_DEV, CONV_W, C)
    conv_piece = lax.dynamic_slice_in_dim(conv_piece, me * cw_cols, cw_cols, axis=2)
    conv_piece = conv_piece.reshape(N_DEV, CONV_W * cw_cols // LANES, LANES)
    bn_piece = lax.dynamic_slice_in_dim(small_piece(7), me * bn_rows, bn_rows, axis=1)
    small_g = jnp.concatenate([small_piece(0), conv_piece, small_piece(2), small_piece(3), small_piece(4),
                               small_piece(5), small_piece(6), bn_piece, small_piece(8)], axis=1)
    small_w = [(a_norm, m_a_norm, v_a_norm), (a_conv_w, m_a_conv_w, v_a_conv_w),
               (a_conv_b, m_a_conv_b, v_a_conv_b), (a_b_r, m_a_b_r, v_a_b_r), (a_b_i, m_a_b_i, v_a_b_i),
               (a_lambda, m_a_lambda, v_a_lambda), (kv_norm, m_kv_norm, v_kv_norm),
               (b_norm, m_b_norm, v_b_norm), (final_norm, m_final_norm, v_final_norm)]
    pack = lambda idx: jnp.concatenate([_rows(t[idx]) for t in small_w], axis=0)
    res_small = _adamw(pack(0), pack(1), pack(2), [small_g], name="adamw_small")
    woffs = [0]
    for t in small_w:
        woffs.append(woffs[-1] + t[0].size // LANES)
    upd_small = [[r[woffs[i]:woffs[i + 1]].reshape(small_w[i][0].shape) for r in res_small]
                 for i in range(len(small_w))]

    order = [("s", 0), ("b", 0), ("s", 1), ("s", 2), ("b", 5), ("s", 3), ("b", 6), ("s", 4), ("s", 5),
             ("b", 1), ("s", 6), ("b", 2), ("s", 7), ("b", 3), ("b", 4), ("s", 8)]
    per_weight = [(upd_big if kind == "b" else upd_small)[i] for kind, i in order]
    loss = lax.psum(loss_part[0, 0], ("x", "y", "c"))
    result = [loss, grad_x[None]]
    for field in range(4):
        result += [u[field] for u in per_weight]
    return tuple(result)
```

```python
import math

import jax
import jax.numpy as jnp
from jax import lax
from jax.experimental import pallas as pl
from jax.experimental.pallas import tpu as pltpu

F32 = jnp.float32
BF16 = jnp.bfloat16
MESH = pl.DeviceIdType.MESH

EPS = 1e-6
LOG2E = 1.4426950408889634
WEIGHT_FLOOR_LOG2 = -200.0
LRU_C = 8.0
CONV_W = 4
HEAD_DIM = 128
ADAM_LR = 0.001
ADAM_B1 = 0.9
ADAM_B2 = 0.999
ADAM_EPS = 1e-08
ADAM_WD = 0.01
ADAM_STEP = 10

N_DEV = 8
LANES = 128
SUBLANES = 8
VMEM_LIMIT = 56 * 1024 * 1024

ATT_KEY_BLOCK = 256
ATT_QUERY_BLOCK = 256
ATT_STEP_BLOCKS = 2
SCAN_BLOCK = 256
ROW_BLOCK = 256
MM_TOKEN_BLOCK = 512
MM_WEIGHT_TILE = 1280
MM_CONTRACT_TOKENS = 2048
ANY = pl.BlockSpec(memory_space=pl.ANY)


def _pcall(body, **kw):
    return pl.pallas_call(body, **kw)


def _params(*sem):
    return pltpu.CompilerParams(dimension_semantics=sem, vmem_limit_bytes=VMEM_LIMIT)


def _pick(n, cap):
    if n <= cap:
        return n
    best = None
    for t in range(LANES, cap + 1, LANES):
        if n % t == 0:
            best = t
    assert best is not None, (n, cap)
    return best


def _sigmoid(x):
    return 1.0 / (1.0 + jnp.exp(-x))


def _dot(a, b, ca, cb):
    return lax.dot_general(a, b, (((ca,), (cb,)), ((), ())), preferred_element_type=F32)


def _mm_nn(a, w, *, name, out_dtype, col_off=0, cols=None, res=None):
    T, K = a.shape
    K2, N = w.shape
    assert K == K2
    cols = N if cols is None else cols
    tm = min(T, MM_TOKEN_BLOCK)
    tn = _pick(cols, MM_WEIGHT_TILE)
    assert col_off % tn == 0
    off = col_off // tn
    has_res = res is not None

    def body(a_ref, b_ref, *rest):
        o_ref = rest[-1]
        acc = jnp.dot(a_ref[...].astype(BF16), b_ref[...], preferred_element_type=F32)
        if has_res:
            acc = acc + rest[0][...]
        o_ref[...] = acc.astype(out_dtype)

    in_specs = [pl.BlockSpec((tm, K), lambda j, i: (i, 0)),
                pl.BlockSpec((K, tn), lambda j, i: (0, off + j))]
    args = [a, w]
    if has_res:
        in_specs.append(pl.BlockSpec((tm, tn), lambda j, i: (i, j)))
        args.append(res)
    return _pcall(
        body, name=name, grid=(cols // tn, T // tm), in_specs=in_specs,
        out_specs=pl.BlockSpec((tm, tn), lambda j, i: (i, j)),
        out_shape=jax.ShapeDtypeStruct((T, cols), out_dtype),
        compiler_params=_params("parallel", "parallel"))(*args)


def _mm_nt(a, w, *, name, out_dtype=F32, rider=None):
    parts = a.shape[0] if a.ndim == 3 else 1
    T, kp = a.shape[-2:]
    N, K = w.shape
    assert K == parts * kp
    tm = min(T, MM_TOKEN_BLOCK)
    tn = _pick(N, MM_WEIGHT_TILE)
    nj, ni = N // tn, T // tm
    rider = rider or _Rider([], [], [], None)
    nri, nro = len(rider.inputs), len(rider.out_shapes)

    def body(a_ref, b_ref, *refs):
        o_ref = refs[nri]
        j, i = pl.program_id(0), pl.program_id(1)
        if nro:
            start, finish = rider.bind(refs[:nri], refs[nri + 1:nri + 1 + nro], refs[nri + 1 + nro:])
            pl.when((j == 0) & (i == 0))(start)
        if a.ndim == 3:
            acc = None
            for p in range(parts):
                term = _dot(a_ref[p], b_ref[:, p * kp:(p + 1) * kp], 1, 1)
                acc = term if acc is None else acc + term
        else:
            acc = _dot(a_ref[...].astype(BF16), b_ref[...], 1, 1)
        o_ref[...] = acc.astype(out_dtype)
        if nro:
            pl.when((j == nj - 1) & (i == ni - 1))(finish)

    a_spec = (pl.BlockSpec((parts, tm, kp), lambda j, i: (0, i, 0)) if a.ndim == 3
              else pl.BlockSpec((tm, K), lambda j, i: (i, 0)))
    sem = ("arbitrary", "arbitrary") if nro else ("parallel", "parallel")
    out = _pcall(
        body, name=name, grid=(nj, ni),
        in_specs=[a_spec, pl.BlockSpec((tn, K), lambda j, i: (j, 0))] + [ANY] * nri,
        out_specs=[pl.BlockSpec((tm, tn), lambda j, i: (i, j))] + [ANY] * nro,
        out_shape=[jax.ShapeDtypeStruct((T, N), out_dtype)] + rider.out_shapes,
        scratch_shapes=rider.scratch,
        compiler_params=_params(*sem))(a, w, *rider.inputs)
    return out if nro else out[0]


def _mm_tn(a, b, *, name, shards=1):
    T, Ko = a.shape
    parts = b.shape[0] if b.ndim == 3 else 1
    T2, n_part = b.shape[-2:]
    N = parts * n_part
    assert T == T2
    n = N // shards
    tt = min(T, MM_CONTRACT_TOKENS)
    tko = _pick(Ko, 1024)
    tn = _pick(n, 1024)
    per = n // tn
    assert n_part % tn == 0
    per_part = n_part // tn

    def body(a_ref, b_ref, o_ref):
        t = pl.program_id(2)
        p = _dot(a_ref[...].astype(BF16), b_ref[...].astype(BF16), 0, 0)

        @pl.when(t == 0)
        def _():
            o_ref[...] = p

        @pl.when(t > 0)
        def _():
            o_ref[...] += p

    if shards == 1:
        out_spec = pl.BlockSpec((tko, tn), lambda i, j, t: (i, j))
        out_shape = jax.ShapeDtypeStruct((Ko, N), F32)
    else:
        out_spec = pl.BlockSpec((None, tko, tn), lambda i, j, t: (j // per, i, j % per))
        out_shape = jax.ShapeDtypeStruct((shards, Ko, n), F32)
    b_spec = (pl.BlockSpec((None, tt, tn), lambda i, j, t: (j // per_part, t, j % per_part)) if b.ndim == 3
              else pl.BlockSpec((tt, tn), lambda i, j, t: (t, j)))
    return _pcall(
        body, name=name, grid=(Ko // tko, N // tn, T // tt),
        in_specs=[pl.BlockSpec((tt, tko), lambda i, j, t: (t, i)), b_spec],
        out_specs=out_spec, out_shape=out_shape,
        compiler_params=_params("parallel", "parallel", "arbitrary"))(a, b)


def _rms_fwd(x, gains, *, name, riders=()):
    T, D = x.shape
    tm = min(T, ROW_BLOCK)
    steps = T // tm
    n, nr = len(gains), len(riders)

    def body(x_ref, *refs):
        g_refs, rider_in = refs[:n], refs[n:n + nr]
        o_refs, rider_out = refs[n + nr:2 * n + nr], refs[2 * n + nr:2 * n + 2 * nr]
        if nr:
            gather = _Gather(rider_in, rider_out, *refs[2 * n + 2 * nr:])
            pl.when(pl.program_id(0) == 0)(gather.start)
        xv = x_ref[...]
        xh = xv * lax.rsqrt(jnp.mean(xv * xv, axis=-1, keepdims=True) + EPS)
        for g_ref, o_ref in zip(g_refs, o_refs):
            o_ref[...] = (xh * g_ref[...]).astype(BF16)
        if nr:
            pl.when(pl.program_id(0) == steps - 1)(gather.finish)

    row = pl.BlockSpec((tm, D), lambda i: (i, 0))
    vec = pl.BlockSpec((1, D), lambda i: (0, 0))
    return _pcall(
        body, name=name, grid=(steps,), in_specs=[row] + [vec] * n + [ANY] * nr,
        out_specs=[row] * n + [ANY] * nr,
        out_shape=[jax.ShapeDtypeStruct((T, D), BF16)] * n
                  + [jax.ShapeDtypeStruct((N_DEV,) + r.shape, r.dtype) for r in riders],
        scratch_shapes=_Gather.scratch(nr) if nr else [],
        compiler_params=_params("arbitrary" if nr else "parallel"))(x, *gains, *riders)


def _rms_bwd(x, dres, dhs, gains, *, name, rider=None):
    T, D = x.shape
    tm = min(T, ROW_BLOCK)
    steps = T // tm
    n = len(gains)
    rider = rider or _Rider([], [], [], None)
    nri, nro = len(rider.inputs), len(rider.out_shapes)

    def body(x_ref, dres_ref, *refs):
        dh_refs, g_refs = refs[:n], refs[n:2 * n]
        refs = refs[2 * n:]
        rider_in, refs = refs[:nri], refs[nri:]
        dx_ref, dg_refs = refs[0], refs[1:1 + n]
        rider_out, sems = refs[1 + n:1 + n + nro], refs[1 + n + nro:]
        i = pl.program_id(0)
        if nro:
            start, finish = rider.bind(rider_in, rider_out, sems)
            pl.when(i == 0)(start)
        xv = x_ref[...]
        r = lax.rsqrt(jnp.mean(xv * xv, axis=-1, keepdims=True) + EPS)
        xh = xv * r
        dxh = jnp.zeros_like(xv)
        for dh_ref, g_ref, dg_ref in zip(dh_refs, g_refs, dg_refs):
            dh = dh_ref[...]
            part = jnp.sum(dh * xh, axis=0, keepdims=True)

            @pl.when(i == 0)
            def _():
                dg_ref[...] = part

            @pl.when(i > 0)
            def _():
                dg_ref[...] += part

            dxh = dxh + dh * g_ref[...]
        dx_ref[...] = dres_ref[...] + r * (dxh - xh * jnp.mean(dxh * xh, axis=-1, keepdims=True))
        if nro:
            pl.when(i == steps - 1)(finish)

    row = pl.BlockSpec((tm, D), lambda i: (i, 0))
    vec = pl.BlockSpec((1, D), lambda i: (0, 0))
    return _pcall(
        body, name=name, grid=(steps,), in_specs=[row, row] + [row] * n + [vec] * n + [ANY] * nri,
        out_specs=[row] + [vec] * n + [ANY] * nro,
        out_shape=[jax.ShapeDtypeStruct((T, D), F32)] + [jax.ShapeDtypeStruct((1, D), F32)] * n
                  + rider.out_shapes,
        scratch_shapes=rider.scratch,
        compiler_params=_params("arbitrary"))(x, dres, *dhs, *gains, *rider.inputs)


def _final_loss(x, target, gain, *, name):
    T, D = x.shape
    tm = min(T, ROW_BLOCK)

    def body(x_ref, t_ref, g_ref, dx_ref, dg_ref, loss_ref):
        i = pl.program_id(0)
        xv = x_ref[...]
        g = g_ref[...]
        r = lax.rsqrt(jnp.mean(xv * xv, axis=-1, keepdims=True) + EPS)
        xh = xv * r
        err = xh * g - t_ref[...]
        part_loss = 0.5 * jnp.sum(jnp.mean(err * err, axis=-1, keepdims=True), axis=0, keepdims=True)
        dy = err * (1.0 / D)
        part_g = jnp.sum(dy * xh, axis=0, keepdims=True)

        @pl.when(i == 0)
        def _():
            dg_ref[...] = part_g
            loss_ref[...] = jnp.broadcast_to(part_loss, loss_ref.shape)

        @pl.when(i > 0)
        def _():
            dg_ref[...] += part_g
            loss_ref[...] += jnp.broadcast_to(part_loss, loss_ref.shape)

        dxh = dy * g
        dx_ref[...] = r * (dxh - xh * jnp.mean(dxh * xh, axis=-1, keepdims=True))

    row = pl.BlockSpec((tm, D), lambda i: (i, 0))
    vec = pl.BlockSpec((1, D), lambda i: (0, 0))
    return _pcall(
        body, name=name, grid=(T // tm,), in_specs=[row, row, vec],
        out_specs=[row, vec, pl.BlockSpec((1, LANES), lambda i: (0, 0))],
        out_shape=[jax.ShapeDtypeStruct((T, D), F32), jax.ShapeDtypeStruct((1, D), F32),
                   jax.ShapeDtypeStruct((1, LANES), F32)],
        compiler_params=_params("arbitrary"))(x, target, gain)


def _shift_down(x, prev_tail, j, row):
    tb = x.shape[0]
    prev = jnp.tile(prev_tail, (tb // SUBLANES, 1))
    return jnp.where(row >= j, pltpu.roll(x, j, 0), pltpu.roll(prev, j, 0))


def _shift_up(x, next_head, j, row):
    tb = x.shape[0]
    nxt = jnp.tile(next_head, (tb // SUBLANES, 1))
    return jnp.where(row < tb - j, pltpu.roll(x, tb - j, 0), pltpu.roll(nxt, tb - j, 0))


def _lru_gates(xb, wr, wi, br, bi, lam):
    xbb = xb.astype(BF16)
    r = _sigmoid(jnp.dot(xbb, wr, preferred_element_type=F32) + br)
    i = _sigmoid(jnp.dot(xbb, wi, preferred_element_type=F32) + bi)
    sp = jnp.maximum(-lam, 0.0) + jnp.log1p(jnp.exp(-jnp.abs(lam)))
    log_a = (-LRU_C) * r * sp
    a = jnp.exp(log_a)
    a2 = a * a
    mult = jnp.sqrt(jnp.maximum(-jnp.tanh(log_a) * (1.0 + a2), 0.0))
    return xbb, r, i, sp, a, a2, mult


def _scan_rows(coef, val, edge, reverse):
    tb, C = coef.shape
    a, b = coef, val
    row = lax.broadcasted_iota(jnp.int32, (tb, C), 0)
    s = 1
    while s < tb:
        m = (row < tb - s) if reverse else (row >= s)
        shift = tb - s if reverse else s
        b = jnp.where(m, a * pltpu.roll(b, shift, 0) + b, b)
        a = jnp.where(m, a * pltpu.roll(a, shift, 0), a)
        s *= 2
    return b + a * edge


def _acore_fwd(proj, conv_w, conv_b, w_r, w_i, b_r, b_i, lam, *, name, riders=()):
    T, C2 = proj.shape
    C = C2 // 2
    nb, bw, _ = w_r.shape
    tb = min(T, SCAN_BLOCK)
    nt = T // tb
    nr = len(riders)

    def body(xp_ref, gate_ref, cw_ref, cb_ref, wr_ref, wi_ref, br_ref, bi_ref, lam_ref, *refs):
        rider_in, refs = refs[:nr], refs[nr:]
        xb_ref, h_ref, yg_ref = refs[:3]
        rider_out, refs = refs[3:3 + nr], refs[3 + nr:]
        tail_ref, hlast_ref = refs[:2]
        t = pl.program_id(1)
        if nr:
            gather = _Gather(rider_in, rider_out, *refs[2:])
            pl.when((pl.program_id(0) == 0) & (t == 0))(gather.start)

        @pl.when(t == 0)
        def _():
            tail_ref[...] = jnp.zeros_like(tail_ref)
            hlast_ref[...] = jnp.zeros_like(hlast_ref)

        row = lax.broadcasted_iota(jnp.int32, (tb, bw), 0)
        xp = xp_ref[...]
        tail = tail_ref[...]
        xb = cb_ref[...] + cw_ref[CONV_W - 1:CONV_W, :] * xp
        for j in range(1, CONV_W):
            xb = xb + cw_ref[CONV_W - 1 - j:CONV_W - j, :] * _shift_down(xp, tail, j, row)
        tail_ref[...] = xp[tb - SUBLANES:, :]
        xb_ref[...] = xb

        _, r, i, sp, a, a2, mult = _lru_gates(xb, wr_ref[...], wi_ref[...], br_ref[...], bi_ref[...],
                                              lam_ref[...])
        h = _scan_rows(a, mult * (i * xb), hlast_ref[SUBLANES - 1:SUBLANES, :], False)
        hlast_ref[...] = h[tb - SUBLANES:, :]
        h_ref[...] = h
        gate = gate_ref[...]
        yg_ref[...] = (h * (gate * _sigmoid(gate))).astype(BF16)
        if nr:
            pl.when((pl.program_id(0) == nb - 1) & (t == nt - 1))(gather.finish)

    blk = lambda off: pl.BlockSpec((tb, bw), lambda n, t: (t, off + n))
    vec = pl.BlockSpec((1, bw), lambda n, t: (0, n))
    wspec = pl.BlockSpec((None, bw, bw), lambda n, t: (n, 0, 0))
    return _pcall(
        body, name=name, grid=(nb, nt),
        in_specs=[blk(0), blk(nb), pl.BlockSpec((CONV_W, bw), lambda n, t: (0, n)), vec, wspec, wspec,
                  vec, vec, vec] + [ANY] * nr,
        out_specs=[blk(0), blk(0), blk(0)] + [ANY] * nr,
        out_shape=[jax.ShapeDtypeStruct((T, C), F32), jax.ShapeDtypeStruct((T, C), F32),
                   jax.ShapeDtypeStruct((T, C), BF16)]
                  + [jax.ShapeDtypeStruct((N_DEV,) + r.shape, r.dtype) for r in riders],
        scratch_shapes=[pltpu.VMEM((SUBLANES, bw), F32), pltpu.VMEM((SUBLANES, bw), F32)]
                       + (_Gather.scratch(nr) if nr else []),
        compiler_params=_params("arbitrary" if nr else "parallel", "arbitrary"))(
            proj, proj, conv_w, conv_b, w_r, w_i, b_r, b_i, lam, *riders)


def _acore_bwd(dyg, proj, xb_all, h_all, conv_w, w_r, w_i, b_r, b_i, lam, *, name, rider=None):
    T, C2 = proj.shape
    C = C2 // 2
    nb, bw, _ = w_r.shape
    tb = min(T, SCAN_BLOCK)
    nt = T // tb
    per8 = tb // SUBLANES
    rider = rider or _Rider([], [], [], None)
    nri, nro = len(rider.inputs), len(rider.out_shapes)

    def body(dyg_ref, xp_ref, gate_ref, xb_ref, h_ref, xp_prev_ref, h_prev_ref, cw_ref,
             wr_ref, wi_ref, br_ref, bi_ref, lam_ref, *refs):
        rider_in, refs = refs[:nri], refs[nri:]
        dproj_ref, dcw_ref, dcb_ref, dbr_ref, dbi_ref, dlam_ref, dwr_ref, dwi_ref = refs[:8]
        rider_out, refs = refs[8:8 + nro], refs[8 + nro:]
        gh_next_ref, a_next_ref, dxb_next_ref = refs[:3]
        step = pl.program_id(1)
        first_block = step == nt - 1
        if nro:
            start, finish = rider.bind(rider_in, rider_out, refs[3:])
            pl.when((pl.program_id(0) == 0) & (step == 0))(start)

        @pl.when(step == 0)
        def _():
            gh_next_ref[...] = jnp.zeros_like(gh_next_ref)
            a_next_ref[...] = jnp.zeros_like(a_next_ref)
            dxb_next_ref[...] = jnp.zeros_like(dxb_next_ref)

        row = lax.broadcasted_iota(jnp.int32, (tb, bw), 0)
        keep = jnp.where(first_block, 0.0, 1.0)
        h_prev = h_prev_ref[...] * keep
        xp_prev = xp_prev_ref[...] * keep
        xp, gate, xb, h, dyg_v = xp_ref[...], gate_ref[...], xb_ref[...], h_ref[...], dyg_ref[...]
        lam_v = lam_ref[...]
        wr, wi = wr_ref[...], wi_ref[...]

        sg = _sigmoid(gate)
        dh = dyg_v * (gate * sg)
        dproj_ref[1] = (dyg_v * h * (sg * (1.0 + gate * (1.0 - sg)))).astype(BF16)

        xbb, r, i, sp, a, a2, mult = _lru_gates(xb, wr, wi, br_ref[...], bi_ref[...], lam_v)

        gh = _scan_rows(_shift_up(a, a_next_ref[...], 1, row), dh, gh_next_ref[0:1, :], True)
        gh_next_ref[...] = gh[0:SUBLANES, :]
        a_next_ref[...] = a[0:SUBLANES, :]

        da = gh * _shift_down(h, h_prev, 1, row)
        dmult = gh * (i * xb)
        di = gh * mult * xb
        dxb = gh * mult * i
        dla = da * a - dmult * jnp.where(mult > 0.0, a2 / mult, 0.0)
        dr = dla * ((-LRU_C) * sp)
        dsp = jnp.sum(dla * ((-LRU_C) * r), axis=0, keepdims=True)
        dlam_part = dsp * (-_sigmoid(-lam_v))
        dpr = dr * r * (1.0 - r)
        dpi = di * i * (1.0 - i)
        dbr_part = jnp.sum(dpr, axis=0, keepdims=True)
        dbi_part = jnp.sum(dpi, axis=0, keepdims=True)
        dprb, dpib = dpr.astype(BF16), dpi.astype(BF16)
        dwr_part = _dot(xbb, dprb, 0, 0)
        dwi_part = _dot(xbb, dpib, 0, 0)
        dxb = dxb + _dot(dprb, wr, 1, 1) + _dot(dpib, wi, 1, 1)

        dxb_next = dxb_next_ref[...]
        dxp = cw_ref[CONV_W - 1:CONV_W, :] * dxb
        for j in range(1, CONV_W):
            dxp = dxp + cw_ref[CONV_W - 1 - j:CONV_W - j, :] * _shift_up(dxb, dxb_next, j, row)
        dxb_next_ref[...] = dxb[0:SUBLANES, :]
        dproj_ref[0] = dxp.astype(BF16)
        dcb_part = jnp.sum(dxb, axis=0, keepdims=True)
        dcw_rows = []
        for k in range(CONV_W):
            j = CONV_W - 1 - k
            sh = xp if j == 0 else _shift_down(xp, xp_prev, j, row)
            dcw_rows.append(jnp.sum(dxb * sh, axis=0, keepdims=True))

        @pl.when(step == 0)
        def _():
            for k in range(CONV_W):
                dcw_ref[k:k + 1, :] = dcw_rows[k]
            dcb_ref[...] = dcb_part
            dbr_ref[...] = dbr_part
            dbi_ref[...] = dbi_part
            dlam_ref[...] = dlam_part
            dwr_ref[...] = dwr_part
            dwi_ref[...] = dwi_part

        @pl.when(step > 0)
        def _():
            for k in range(CONV_W):
                dcw_ref[k:k + 1, :] += dcw_rows[k]
            dcb_ref[...] += dcb_part
            dbr_ref[...] += dbr_part
            dbi_ref[...] += dbi_part
            dlam_ref[...] += dlam_part
            dwr_ref[...] += dwr_part
            dwi_ref[...] += dwi_part

        if nro:
            pl.when((pl.program_id(0) == nb - 1) & (step == nt - 1))(finish)

    rev = lambda s: nt - 1 - s
    blk = lambda off: pl.BlockSpec((tb, bw), lambda n, s: (rev(s), off + n))
    prev8 = lambda off: pl.BlockSpec(
        (SUBLANES, bw), lambda n, s: (jnp.maximum(rev(s) * per8 - 1, 0), off + n))
    vec = pl.BlockSpec((1, bw), lambda n, s: (0, n))
    wspec = pl.BlockSpec((None, bw, bw), lambda n, s: (n, 0, 0))
    cwspec = pl.BlockSpec((CONV_W, bw), lambda n, s: (0, n))
    vshape = jax.ShapeDtypeStruct((1, C), F32)
    wshape = jax.ShapeDtypeStruct((nb, bw, bw), F32)
    return _pcall(
        body, name=name, grid=(nb, nt),
        in_specs=[blk(0), blk(0), blk(nb), blk(0), blk(0), prev8(0), prev8(0), cwspec,
                  wspec, wspec, vec, vec, vec] + [ANY] * nri,
        out_specs=[pl.BlockSpec((2, tb, bw), lambda n, s: (0, rev(s), n)), cwspec, vec, vec, vec, vec,
                   wspec, wspec] + [ANY] * nro,
        out_shape=[jax.ShapeDtypeStruct((2, T, C), BF16),
                   jax.ShapeDtypeStruct((CONV_W, C), F32), vshape, vshape, vshape, vshape,
                   wshape, wshape] + rider.out_shapes,
        scratch_shapes=[pltpu.VMEM((SUBLANES, bw), F32)] * 3 + rider.scratch,
        compiler_params=_params("arbitrary" if nro else "parallel", "arbitrary"))(
            dyg, proj, proj, xb_all, h_all, proj, h_all, conv_w, w_r, w_i, b_r, b_i, lam, *rider.inputs)


def _later_sum(lk, tri):
    return jnp.dot(lk.astype(BF16), tri, preferred_element_type=F32)


def _log2_sigmoids(y):
    t = jnp.log(1.0 + jnp.exp2(-jnp.abs(y))) * LOG2E
    ls = jnp.minimum(y, 0.0) - t
    return ls, ls - y


def _attn_blocks(T):
    bk = min(T, ATT_KEY_BLOCK)
    bq = min(T, ATT_QUERY_BLOCK)
    return bq, bk, bq // bk


def _attn_step_blocks(T, bq):
    return ATT_STEP_BLOCKS if (T // bq) % ATT_STEP_BLOCKS == 0 else 1


def _attn_fwd(q, kv, gate, *, name):
    T, HD = q.shape
    H = HD // HEAD_DIM
    bq, bk, per = _attn_blocks(T)
    scale = 1.0 / math.sqrt(HEAD_DIM)

    nsub = _attn_step_blocks(T, bq)

    def body(q_ref, k_ref, v_ref, g_ref, o_ref, og_ref, lt_ref, w_ref):
        for sub in range(nsub):
            rows = pl.ds(sub * bq, bq)
            one_block(pl.program_id(1) * nsub + sub, q_ref.at[rows], k_ref, v_ref, g_ref.at[rows],
                      o_ref.at[rows], og_ref.at[rows], lt_ref.at[rows], w_ref)

    def one_block(i, q_ref, k_ref, v_ref, g_ref, o_ref, og_ref, lt_ref, w_ref):
        qv = q_ref[...]
        tr = lax.broadcasted_iota(jnp.int32, (bk, bk), 0)
        tc = lax.broadcasted_iota(jnp.int32, (bk, bk), 1)
        tri = (tr > tc).astype(BF16)
        ahead = (lax.broadcasted_iota(jnp.int32, (bq, bk), 0)
                 - lax.broadcasted_iota(jnp.int32, (bq, bk), 1))

        def starts_of(top):
            return [pl.multiple_of((top - d) * bk, bk) for d in range(per)]

        def scores(top):
            return [_dot(qv, k_ref[pl.ds(ks, bk), :], 1, 1) for ks in starts_of(top)]

        def weights(top, zs, c, mask):
            lss, sums, css, causals = [], [], [], []
            for ks, z in zip(starts_of(top), zs):
                ls, lk = _log2_sigmoids(z * (scale * LOG2E))
                if mask:
                    causals.append(ahead > ks - i * bq)
                    lk = jnp.where(causals[-1], lk, 0.0)
                lss.append(ls)
                sums.append(jnp.sum(lk, axis=1, keepdims=True))
                css.append(_later_sum(lk, tri))
            for d in range(per):
                w = jnp.exp2(lss[d] + (css[d] + c))
                if mask:
                    w = jnp.where(causals[d], w, 0.0)
                w_ref[d] = w.astype(BF16)
                c = c + sums[d]
            return c

        def values(top, acc):
            for d, ks in enumerate(starts_of(top)):
                acc = acc + jnp.dot(w_ref[d], v_ref[pl.ds(ks, bk), :], preferred_element_type=F32)
            return acc

        def more(state):
            gg, _, _, largest = state
            return (gg <= i) & (largest > WEIGHT_FLOOR_LOG2)

        def step(state):
            gg, acc, c, _ = state
            top = (i - gg) * per + per - 1
            zs = scores(top)
            acc = values(top + per, acc)
            c = weights(top, zs, c, False)
            return gg + 1, acc, c, jnp.max(c)

        diag_top = i * per + per - 1
        c = weights(diag_top, scores(diag_top), jnp.zeros((bq, 1), F32), True)
        gg, acc, c, _ = lax.while_loop(more, step, (1, jnp.zeros((bq, HEAD_DIM), F32), c, jnp.max(c)))
        acc = values((i - gg + 1) * per + per - 1, acc)
        o_ref[...] = acc
        g = g_ref[...]
        og_ref[...] = (acc * (g * _sigmoid(g))).astype(BF16)
        lane = lax.broadcasted_iota(jnp.int32, (bq, HEAD_DIM), 1)
        lt_ref[...] = jnp.where(lane == 1, (i - gg + 1).astype(F32), jnp.broadcast_to(c, (bq, HEAD_DIM)))

    qspec = pl.BlockSpec((nsub * bq, HEAD_DIM), lambda h, i: (i, h))
    return _pcall(
        body, name=name, grid=(H, T // (nsub * bq)),
        in_specs=[qspec, pl.BlockSpec((T, HEAD_DIM), lambda h, i: (0, h)),
                  pl.BlockSpec((T, HEAD_DIM), lambda h, i: (0, H + h)), qspec],
        out_specs=[qspec, qspec, qspec],
        out_shape=[jax.ShapeDtypeStruct((T, HD), F32), jax.ShapeDtypeStruct((T, HD), BF16),
                   jax.ShapeDtypeStruct((T, HD), F32)],
        scratch_shapes=[pltpu.VMEM((per, bq, bk), BF16)],
        compiler_params=_params("parallel", "arbitrary"))(q, kv, kv, gate)


def _attn_bwd(q, kv, gate, o, ltot, dog, *, name):
    T, HD = q.shape
    H = HD // HEAD_DIM
    bq, bk, per = _attn_blocks(T)
    nq = T // bq
    scale = 1.0 / math.sqrt(HEAD_DIM)

    nsub = _attn_step_blocks(T, bq)

    def body(q_ref, k_ref, v_ref, g_ref, o_ref, lt_ref, dog_ref,
             dqg_ref, dkv_ref, dk_acc, dv_acc, dz_ref, w_ref):
        for sub in range(nsub):
            rows = pl.ds(sub * bq, bq)
            one_block(pl.program_id(1) * nsub + sub, q_ref.at[rows], k_ref, v_ref, g_ref.at[rows],
                      o_ref.at[rows], lt_ref.at[rows], dog_ref.at[rows], dqg_ref.at[:, rows], dkv_ref,
                      dk_acc, dv_acc, dz_ref, w_ref)

    def one_block(i, q_ref, k_ref, v_ref, g_ref, o_ref, lt_ref, dog_ref,
                  dqg_ref, dkv_ref, dk_acc, dv_acc, dz_ref, w_ref):
        @pl.when(i == 0)
        def _():
            dk_acc[...] = jnp.zeros_like(dk_acc)
            dv_acc[...] = jnp.zeros_like(dv_acc)

        qv = q_ref[...]
        g, ov, dogv = g_ref[...], o_ref[...], dog_ref[...]
        sg = _sigmoid(g)
        do = dogv * (g * sg)
        dqg_ref[1] = (dogv * ov * (sg * (1.0 + g * (1.0 - sg)))).astype(BF16)
        dob = do.astype(BF16)
        ltot_v = lt_ref[:, 0:1]
        tr = lax.broadcasted_iota(jnp.int32, (bk, bk), 0)
        tc = lax.broadcasted_iota(jnp.int32, (bk, bk), 1)
        tri_later = (tr > tc).astype(BF16)
        tri_excl = (tr < tc).astype(BF16)
        ahead = (lax.broadcasted_iota(jnp.int32, (bq, bk), 0)
                 - lax.broadcasted_iota(jnp.int32, (bq, bk), 1))

        def starts_of(first):
            return [pl.multiple_of((first + d) * bk, bk) for d in range(per)]

        def scores(first):
            return ([_dot(qv, k_ref[pl.ds(ks, bk), :], 1, 1) for ks in starts_of(first)],
                    [_dot(dob, v_ref[pl.ds(ks, bk), :], 1, 1) for ks in starts_of(first)])

        def front(first, zs, dws, p_lk, p_g, mask):
            lss, css, causals = [], [], []
            for ks, z in zip(starts_of(first), zs):
                ls, lk = _log2_sigmoids(z * (scale * LOG2E))
                if mask:
                    causals.append(ahead > ks - i * bq)
                    lk = jnp.where(causals[-1], lk, 0.0)
                lss.append(ls)
                p_lk = p_lk + jnp.sum(lk, axis=1, keepdims=True)
                css.append((ltot_v - p_lk) + _later_sum(lk, tri_later))
            gms, befores = [], []
            for d in range(per):
                w = jnp.exp2(lss[d] + css[d])
                if mask:
                    w = jnp.where(causals[d], w, 0.0)
                gm = dws[d] * w
                gms.append(gm)
                w_ref[d] = w.astype(BF16)
                befores.append(jnp.dot(gm.astype(BF16), tri_excl, preferred_element_type=F32) + p_g)
                p_g = p_g + jnp.sum(gm, axis=1, keepdims=True)
            for d in range(per):
                dz = gms[d] - jnp.exp2(lss[d]) * (gms[d] + befores[d])
                if mask:
                    dz = jnp.where(causals[d], dz, 0.0)
                dz_ref[d] = (dz * scale).astype(BF16)
            return p_lk, p_g

        def back(first, dq):
            for d, ks in enumerate(starts_of(first)):
                dzb = dz_ref[d]
                dq = dq + jnp.dot(dzb, k_ref[pl.ds(ks, bk), :], preferred_element_type=F32)
                dk_acc[pl.ds(ks, bk), :] += _dot(dzb, qv, 0, 0)
                dv_acc[pl.ds(ks, bk), :] += _dot(w_ref[d], dob, 0, 0)
            return dq

        def step(mask):
            def trip(g, state):
                dq, p_lk, p_g = state
                zs, dws = scores(g * per)
                dq = back((g - 1) * per, dq)
                return (dq,) + front(g * per, zs, dws, p_lk, p_g, mask)
            return trip

        g0 = jnp.max(lt_ref[0:1, 1:2]).astype(jnp.int32)
        zero = jnp.zeros((bq, 1), F32)
        state = (jnp.zeros((bq, HEAD_DIM), F32),) + front(g0 * per, *scores(g0 * per), zero, zero, True)
        state = lax.fori_loop(g0 + 1, i, step(False), state)
        state = lax.fori_loop(jnp.maximum(i, g0 + 1), i + 1, step(True), state)
        dqg_ref[0] = back(i * per, state[0]).astype(BF16)

        @pl.when(i == nq - 1)
        def _():
            dkv_ref[0] = dk_acc[...].astype(BF16)
            dkv_ref[1] = dv_acc[...].astype(BF16)

    qspec = pl.BlockSpec((nsub * bq, HEAD_DIM), lambda h, i: (i, h))
    kspec = pl.BlockSpec((T, HEAD_DIM), lambda h, i: (0, h))
    return _pcall(
        body, name=name, grid=(H, nq // nsub),
        in_specs=[qspec, kspec, pl.BlockSpec((T, HEAD_DIM), lambda h, i: (0, H + h)),
                  qspec, qspec, qspec, qspec],
        out_specs=[pl.BlockSpec((2, nsub * bq, HEAD_DIM), lambda h, i: (0, i, h)),
                   pl.BlockSpec((2, T, HEAD_DIM), lambda h, i: (0, 0, h))],
        out_shape=[jax.ShapeDtypeStruct((2, T, HD), BF16)] * 2,
        scratch_shapes=[pltpu.VMEM((T, HEAD_DIM), F32)] * 2 + [pltpu.VMEM((per, bq, bk), BF16)] * 2,
        compiler_params=_params("parallel", "arbitrary"))(q, kv, kv, gate, o, ltot, dog)


def _position():
    return lax.axis_index("x"), lax.axis_index("y"), lax.axis_index("c")


def _chip_of(k, x, y):
    return (1 - x if k & 1 else x), (1 - y if k & 2 else y)


class _Gather:
    @staticmethod
    def scratch(n):
        return [pltpu.SemaphoreType.DMA((n, 7)), pltpu.SemaphoreType.DMA((n, 7)),
                pltpu.SemaphoreType.DMA((n,))]

    def __init__(self, ins, outs, send_sems, recv_sems, local_sems):
        self.ins, self.outs, self.n = ins, outs, len(ins)
        self.send_sems, self.recv_sems, self.local_sems = send_sems, recv_sems, local_sems
        x, y, c = _position()
        self.me, self.sibling = (x, y, c), (x, y, 1 - c)
        self.chips = [_chip_of(k, x, y) for k in (1, 2, 3)]

    def copy(self, a, k, block, to, src=None):
        slot = self.outs[a].at[4 * block[0] + 2 * block[1] + block[2]]
        return pltpu.make_async_remote_copy(
            src_ref=slot if src is None else src, dst_ref=slot,
            send_sem=self.send_sems.at[a, k], recv_sem=self.recv_sems.at[a, k],
            device_id=to, device_id_type=MESH)

    def own_copies(self):
        x, y, c = self.me
        mine = [pltpu.make_async_copy(self.ins[a], self.outs[a].at[4 * x + 2 * y + c], self.local_sems.at[a])
                for a in range(self.n)]
        first = []
        for a in range(self.n):
            first.append(self.copy(a, 0, self.me, self.sibling, src=self.ins[a]))
            first += [self.copy(a, 1 + j, self.me, (*chip, c), src=self.ins[a])
                      for j, chip in enumerate(self.chips)]
        return mine, first

    def start(self):
        mine, first = self.own_copies()
        for cp in mine + first:
            cp.start()

    def finish(self):
        c = self.me[2]
        mine, first = self.own_copies()
        passed = []
        for j, chip in enumerate(self.chips):
            for a in range(self.n):
                self.copy(a, 1 + j, (*chip, c), self.me).wait_recv()
                fwd = self.copy(a, 4 + j, (*chip, c), self.sibling)
                fwd.start()
                passed.append(fwd)
        for a in range(self.n):
            self.copy(a, 0, self.sibling, self.me).wait_recv()
            for j, chip in enumerate(self.chips):
                self.copy(a, 4 + j, (*chip, 1 - c), self.me).wait_recv()
        for cp in first + passed:
            cp.wait_send()
        for cp in mine:
            cp.wait()


class _Rider:
    def __init__(self, inputs, out_shapes, scratch, copies):
        self.inputs, self.out_shapes, self.scratch, self.copies = inputs, out_shapes, scratch, copies

    def bind(self, ins, outs, sems):
        def start():
            for cp in self.copies(ins, outs, sems):
                cp.start()

        def finish():
            cps = self.copies(ins, outs, sems)
            for cp in cps:
                cp.wait_send()
            for cp in cps:
                cp.wait_recv()

        return start, finish


def _sibling_rider(grads):
    n = len(grads)

    def copies(ins, outs, sems):
        x, y, c = _position()
        return [pltpu.make_async_remote_copy(
            src_ref=ins[a].at[2 * chip + (1 - c)], dst_ref=outs[a].at[chip],
            send_sem=sems[0].at[a, chip], recv_sem=sems[1].at[a, chip],
            device_id=(x, y, 1 - c), device_id_type=MESH) for a in range(n) for chip in range(4)]

    return _Rider(list(grads), [jax.ShapeDtypeStruct((4,) + g.shape[1:], g.dtype) for g in grads],
                  [pltpu.SemaphoreType.DMA((n, 4)), pltpu.SemaphoreType.DMA((n, 4))], copies)


def _chips_rider(parts):
    n = len(parts)

    def copies(ins, outs, sems):
        x, y, c = _position()
        cps = []
        for a in range(n):
            for k in range(3):
                cx, cy = _chip_of(k + 1, x, y)
                cps.append(pltpu.make_async_remote_copy(
                    src_ref=ins[a].at[2 * cx + cy], dst_ref=outs[a].at[k],
                    send_sem=sems[0].at[a, k], recv_sem=sems[1].at[a, k],
                    device_id=(cx, cy, c), device_id_type=MESH))
        return cps

    return _Rider(list(parts), [jax.ShapeDtypeStruct((3,) + p.shape[1:], p.dtype) for p in parts],
                  [pltpu.SemaphoreType.DMA((n, 3)), pltpu.SemaphoreType.DMA((n, 3))], copies)


def _small_gather(small):
    def body(small_ref, small_all, send_sems, recv_sems, local_sem):
        x, y, c = _position()
        me = 4 * x + 2 * y + c
        peers = [(x ^ (m >> 2), y ^ ((m >> 1) & 1), c ^ (m & 1)) for m in range(1, N_DEV)]
        sends = [pltpu.make_async_remote_copy(
            src_ref=small_ref, dst_ref=small_all.at[me], send_sem=send_sems.at[m], recv_sem=recv_sems.at[m],
            device_id=peer, device_id_type=MESH) for m, peer in enumerate(peers)]
        own = pltpu.make_async_copy(small_ref, small_all.at[me], local_sem)
        for cp in sends + [own]:
            cp.start()
        for cp in sends:
            cp.wait_send()
        for m, (px, py, pc) in enumerate(peers):
            pltpu.make_async_remote_copy(
                src_ref=small_ref, dst_ref=small_all.at[4 * px + 2 * py + pc],
                send_sem=send_sems.at[m], recv_sem=recv_sems.at[m],
                device_id=(px, py, pc), device_id_type=MESH).wait_recv()
        own.wait()

    return _pcall(
        body, name="small_gather", in_specs=[ANY], out_specs=ANY,
        out_shape=jax.ShapeDtypeStruct((N_DEV,) + small.shape, small.dtype),
        scratch_shapes=[pltpu.SemaphoreType.DMA((7,)), pltpu.SemaphoreType.DMA((7,)),
                        pltpu.SemaphoreType.DMA])(small)


def _pair_sum(grad, got, *, name):
    _, R, C = got.shape
    tr = _pick8(R, max(2 * SUBLANES, (1 << 17) // C))

    def body(g_ref, b_ref, own_ref, ob_ref):
        north = lax.axis_index("c") == 1
        x1, y1 = lax.axis_index("x") == 1, lax.axis_index("y") == 1
        sums = []
        for chip in range(4):
            sums.append(jnp.where(north, g_ref[chip, 1], g_ref[chip, 0]) + b_ref[chip])
            ob_ref[chip] = sums[-1].astype(BF16)
        own_ref[...] = jnp.where(x1, jnp.where(y1, sums[3], sums[2]), jnp.where(y1, sums[1], sums[0]))

    spec = pl.BlockSpec((4, tr, C), lambda i: (0, i, 0))
    return _pcall(
        body, name=name, grid=(R // tr,),
        in_specs=[pl.BlockSpec((4, 2, tr, C), lambda i: (0, 0, i, 0)), spec],
        out_specs=[pl.BlockSpec((tr, C), lambda i: (i, 0)), spec],
        out_shape=[jax.ShapeDtypeStruct((R, C), F32), jax.ShapeDtypeStruct((4, R, C), BF16)],
        compiler_params=_params("parallel"))(grad.reshape(4, 2, R, C), got)


def _pick8(n, cap):
    if n <= cap:
        return n
    best = None
    for t in range(SUBLANES, cap + 1, SUBLANES):
        if n % t == 0:
            best = t
    assert best is not None, (n, cap)
    return best


def _adamw(w, m, v, parts, *, name):
    R, C = w.shape
    tr = _pick8(R, max(SUBLANES, (1 << 17) // C))
    c1 = 1.0 - ADAM_B1 ** ADAM_STEP
    c2 = 1.0 - ADAM_B2 ** ADAM_STEP
    np_ = len(parts)

    def body(w_ref, m_ref, v_ref, *refs):
        p_refs = refs[:np_]
        g_ref, d_ref, nm_ref, nv_ref = refs[np_:]
        g = None
        for p_ref in p_refs:
            terms = [p_ref[...]] if len(p_ref.shape) == 2 else [p_ref[k] for k in range(p_ref.shape[0])]
            for t in terms:
                g = t.astype(F32) if g is None else g + t.astype(F32)
        mn = ADAM_B1 * m_ref[...] + (1.0 - ADAM_B1) * g
        vn = ADAM_B2 * v_ref[...] + (1.0 - ADAM_B2) * (g * g)
        d_ref[...] = -ADAM_LR * ((mn / c1) / (jnp.sqrt(vn / c2) + ADAM_EPS) + ADAM_WD * w_ref[...])
        g_ref[...] = g
        nm_ref[...] = mn
        nv_ref[...] = vn

    spec = pl.BlockSpec((tr, C), lambda i: (i, 0))
    pspecs = [spec if p.ndim == 2 else pl.BlockSpec((p.shape[0], tr, C), lambda i: (0, i, 0)) for p in parts]
    return _pcall(
        body, name=name, grid=(R // tr,), in_specs=[spec] * 3 + pspecs, out_specs=[spec] * 4,
        out_shape=[jax.ShapeDtypeStruct((R, C), F32)] * 4,
        compiler_params=_params("parallel"))(w, m, v, *parts)


def _rows(a):
    return a.reshape(-1, LANES)


def _whole_from_columns(shards, *, name):
    S, K, n = shards.shape
    tk = _pick8(K, 1024)

    def body(s_ref, o_ref):
        o_ref[...] = s_ref[...]

    return _pcall(
        body, name=name, grid=(K // tk, S),
        in_specs=[pl.BlockSpec((None, tk, n), lambda i, s: (s, i, 0))],
        out_specs=pl.BlockSpec((tk, n), lambda i, s: (i, s)),
        out_shape=jax.ShapeDtypeStruct((K, S * n), shards.dtype),
        compiler_params=_params("parallel", "parallel"))(shards)


def _late_weights(a_w_out_rows, w_kv_cols, b_w_in_cols, b_w_out_rows):
    whole_rows = lambda g: g.reshape(g.shape[0] * g.shape[1], g.shape[2])
    return (whole_rows(a_w_out_rows), _whole_from_columns(w_kv_cols, name="w_kv_whole"),
            _whole_from_columns(b_w_in_cols, name="b_w_in_whole"), whole_rows(b_w_out_rows))


def _forward_backward(xs, target, a_norm, g_a_w_in, conv_w, conv_b, g_w_r, g_w_i, b_r, b_i, lam,
                      kv_norm, b_norm, final_norm, *, late_weights=None, late_shards=None, h_a=None):
    if h_a is None:
        (h_a,) = _rms_fwd(xs, [a_norm], name="a_norm_fwd")
    proj_a = _mm_nn(h_a, g_a_w_in, name="a_in_proj", out_dtype=F32)
    xb, h_rec, yg, *gathered = _acore_fwd(proj_a, conv_w, conv_b, g_w_r, g_w_i, b_r, b_i, lam,
                                          name="a_core_fwd", riders=late_shards or ())
    g_a_w_out, g_w_kv, g_b_w_in, g_b_w_out = _late_weights(*gathered) if late_shards else late_weights
    x1 = _mm_nn(yg, g_a_w_out, name="a_out_proj", out_dtype=F32, res=xs)
    hk, hb = _rms_fwd(x1, [kv_norm, b_norm], name="kv_b_norm_fwd")
    kv = _mm_nn(hk, g_w_kv, name="kv_proj", out_dtype=BF16)
    hd = g_b_w_in.shape[1] // 2
    q = _mm_nn(hb, g_b_w_in, name="q_proj", out_dtype=BF16, col_off=0, cols=hd)
    gate_b = _mm_nn(hb, g_b_w_in, name="b_gate_proj", out_dtype=F32, col_off=hd, cols=hd)
    o, og, ltot = _attn_fwd(q, kv, gate_b, name="attn_fwd")
    x2 = _mm_nn(og, g_b_w_out, name="b_out_proj", out_dtype=F32, res=x1)
    dx2, d_final_norm, loss_part = _final_loss(x2, target, final_norm, name="final_norm_loss")

    dog = _mm_nt(dx2, g_b_w_out, name="b_out_proj_bwd")
    dw_b_out = _mm_tn(og, dx2, name="b_out_proj_wgrad")
    dproj_b, dkv = _attn_bwd(q, kv, gate_b, o, ltot, dog, name="attn_bwd")
    dhb = _mm_nt(dproj_b, g_b_w_in, name="b_in_proj_bwd")
    dw_b_in = _mm_tn(hb, dproj_b, name="b_in_proj_wgrad", shards=N_DEV)
    dhk = _mm_nt(dkv, g_w_kv, name="kv_proj_bwd")
    dw_kv = _mm_tn(hk, dkv, name="kv_proj_wgrad", shards=N_DEV)
    early = [dw_kv, dw_b_in, dw_b_out.reshape(N_DEV, -1, dw_b_out.shape[1])] if late_shards else []
    dx1, d_b_norm, d_kv_norm, *got = _rms_bwd(x1, dx2, [dhb, dhk], [b_norm, kv_norm], name="kv_b_norm_bwd",
                                              rider=_sibling_rider(early) if early else None)
    early_sums = [_pair_sum(f_, g_, name=f"pair_sum_early_{i}") for i, (f_, g_) in enumerate(zip(early, got))]
    dyg = _mm_nt(dx1, g_a_w_out, name="a_out_proj_bwd")
    dw_a_out = _mm_tn(yg, dx1, name="a_out_proj_wgrad")
    (dproj_a, d_conv_w, d_conv_b, d_b_r, d_b_i, d_lambda, dw_r, dw_i, *early_others) = _acore_bwd(
        dyg, proj_a, xb, h_rec, conv_w, g_w_r, g_w_i, b_r, b_i, lam, name="a_core_bwd",
        rider=_chips_rider([s[1] for s in early_sums]) if early else None)
    dw_a_in = _mm_tn(h_a, dproj_a, name="a_in_proj_wgrad", shards=N_DEV)
    rows = dw_r.shape[1] // N_DEV
    lru = lambda dw: dw.reshape(-1, N_DEV, rows, dw.shape[2]).transpose(1, 0, 2, 3).reshape(N_DEV, -1, dw.shape[2])
    late = [dw_a_in, dw_a_out.reshape(N_DEV, -1, dw_a_out.shape[1]), lru(dw_r), lru(dw_i)] if late_shards else []
    dh_a, *got = _mm_nt(dproj_a, g_a_w_in, name="a_in_proj_bwd", rider=_sibling_rider(late)) if late else (
        _mm_nt(dproj_a, g_a_w_in, name="a_in_proj_bwd"),)
    late_sums = [_pair_sum(f_, g_, name=f"pair_sum_late_{i}") for i, (f_, g_) in enumerate(zip(late, got))]
    grad_x, d_a_norm, *late_others = _rms_bwd(xs, dx1, [dh_a], [a_norm], name="a_norm_bwd",
                                              rider=_chips_rider([s[1] for s in late_sums]) if late else None)
    sums = late_sums[:2] + early_sums + late_sums[2:]
    others = late_others[:2] + early_others + late_others[2:]
    return (loss_part, grad_x, dw_a_in, dw_a_out, dw_kv, dw_b_in, dw_b_out, dw_r, dw_i, d_a_norm,
            d_conv_w, d_conv_b, d_b_r, d_b_i, d_lambda, d_kv_norm, d_b_norm, d_final_norm, sums, others)


def kernel(x, a_norm, a_w_in, a_conv_w, a_conv_b, a_w_r, a_b_r, a_w_i, a_b_i, a_lambda, a_w_out, kv_norm, w_kv, b_norm, b_w_in, b_w_out, final_norm, loss_target, m_a_norm, m_a_w_in, m_a_conv_w, m_a_conv_b, m_a_w_r, m_a_b_r, m_a_w_i, m_a_b_i, m_a_lambda, m_a_w_out, m_kv_norm, m_w_kv, m_b_norm, m_b_w_in, m_b_w_out, m_final_norm, v_a_norm, v_a_w_in, v_a_conv_w, v_a_conv_b, v_a_w_r, v_a_b_r, v_a_w_i, v_a_b_i, v_a_lambda, v_a_w_out, v_kv_norm, v_w_kv, v_b_norm, v_b_w_in, v_b_w_out, v_final_norm):
    T, D = x.shape[1], x.shape[2]
    nb, bw = a_w_r.shape[1], a_w_r.shape[3]
    C = nb * bw
    me = 4 * lax.axis_index("x") + 2 * lax.axis_index("y") + lax.axis_index("c")
    xs = x[0]
    target = loss_target[0]

    rows_r = a_w_r.shape[2]
    small_f32 = jnp.concatenate([_rows(a_conv_w[0]), _rows(b_norm[0])], axis=0)
    pad = (-small_f32.shape[0]) % SUBLANES
    small_f32 = jnp.pad(small_f32, ((0, pad), (0, 0)))
    h_a, a_w_in_cols, w_r_rows, w_i_rows, small_all = _rms_fwd(
        xs, [a_norm], name="a_norm_fwd",
        riders=[a_w_in[0].astype(BF16), a_w_r[0].reshape(nb * rows_r, bw).astype(BF16),
                a_w_i[0].reshape(nb * rows_r, bw).astype(BF16), small_f32])
    late_shards = [a_w_out[0].astype(BF16), w_kv.astype(BF16), b_w_in[0].astype(BF16), b_w_out[0].astype(BF16)]
    g_a_w_in = _whole_from_columns(a_w_in_cols, name="a_w_in_whole")
    g_w_r = w_r_rows.reshape(N_DEV, nb, rows_r, bw).transpose(1, 0, 2, 3).reshape(nb, bw, bw)
    g_w_i = w_i_rows.reshape(N_DEV, nb, rows_r, bw).transpose(1, 0, 2, 3).reshape(nb, bw, bw)
    cw_rows = a_conv_w.shape[1] * a_conv_w.shape[2] // LANES
    conv_w_full = small_all[:, :cw_rows, :].reshape(N_DEV, CONV_W, a_conv_w.shape[2])
    conv_w_full = conv_w_full.transpose(1, 0, 2).reshape(CONV_W, C)
    bn_rows = b_norm.shape[1] // LANES
    b_norm_full = small_all[:, cw_rows:cw_rows + bn_rows, :].reshape(1, D)
    kv_norm2, final_norm2 = kv_norm.reshape(1, D), final_norm.reshape(1, D)

    (loss_part, grad_x, dw_a_in, dw_a_out, dw_kv, dw_b_in, dw_b_out, dw_r, dw_i, d_a_norm, d_conv_w,
     d_conv_b, d_b_r, d_b_i, d_lambda, d_kv_norm, d_b_norm, d_final_norm, sums,
     others) = _forward_backward(
         xs, target, a_norm, g_a_w_in, conv_w_full, a_conv_b, g_w_r, g_w_i, a_b_r, a_b_i, a_lambda,
         kv_norm2, b_norm_full, final_norm2, late_shards=late_shards, h_a=h_a)

    small_parts = [d_a_norm, d_conv_w, d_conv_b, d_b_r, d_b_i, d_lambda, d_kv_norm, d_b_norm, d_final_norm]
    small_sizes = [p.size // LANES for p in small_parts]
    small = jnp.concatenate([_rows(p) for p in small_parts], axis=0)
    small_everyone = _small_gather(small)

    def shard2d(w):
        return w.reshape(-1, w.shape[-1])

    names_big = [(a_w_in, m_a_w_in, v_a_w_in), (a_w_out, m_a_w_out, v_a_w_out), (w_kv, m_w_kv, v_w_kv),
                 (b_w_in, m_b_w_in, v_b_w_in), (b_w_out, m_b_w_out, v_b_w_out),
                 (a_w_r, m_a_w_r, v_a_w_r), (a_w_i, m_a_w_i, v_a_w_i)]
    upd_big = []
    for i, (w, m, v) in enumerate(names_big):
        res = _adamw(shard2d(w), shard2d(m), shard2d(v), [sums[i][0], others[i]], name=f"adamw_{i}")
        upd_big.append([r.reshape(w.shape) for r in res])

    soffs = [0]
    for s in small_sizes:
        soffs.append(soffs[-1] + s)

    def small_piece(i):
        return small_everyone[:, soffs[i]:soffs[i + 1], :]

    cw_cols = a_conv_w.shape[2]
    conv_piece = small_piece(1).reshape(N_DEV, CONV_W, C)
    conv_piece = lax.dynamic_slice_in_dim(conv_piece, me * cw_cols, cw_cols, axis=2)
    conv_piece = conv_piece.reshape(N_DEV, CONV_W * cw_cols // LANES, LANES)
    bn_piece = lax.dynamic_slice_in_dim(small_piece(7), me * bn_rows, bn_rows, axis=1)
    small_g = jnp.concatenate([small_piece(0), conv_piece, small_piece(2), small_piece(3), small_piece(4),
                               small_piece(5), small_piece(6), bn_piece, small_piece(8)], axis=1)
    small_w = [(a_norm, m_a_norm, v_a_norm), (a_conv_w, m_a_conv_w, v_a_conv_w),
               (a_conv_b, m_a_conv_b, v_a_conv_b), (a_b_r, m_a_b_r, v_a_b_r), (a_b_i, m_a_b_i, v_a_b_i),
               (a_lambda, m_a_lambda, v_a_lambda), (kv_norm, m_kv_norm, v_kv_norm),
               (b_norm, m_b_norm, v_b_norm), (final_norm, m_final_norm, v_final_norm)]
    pack = lambda idx: jnp.concatenate([_rows(t[idx]) for t in small_w], axis=0)
    res_small = _adamw(pack(0), pack(1), pack(2), [small_g], name="adamw_small")
    woffs = [0]
    for t in small_w:
        woffs.append(woffs[-1] + t[0].size // LANES)
    upd_small = [[r[woffs[i]:woffs[i + 1]].reshape(small_w[i][0].shape) for r in res_small]
                 for i in range(len(small_w))]

    order = [("s", 0), ("b", 0), ("s", 1), ("s", 2), ("b", 5), ("s", 3), ("b", 6), ("s", 4), ("s", 5),
             ("b", 1), ("s", 6), ("b", 2), ("s", 7), ("b", 3), ("b", 4), ("s", 8)]
    per_weight = [(upd_big if kind == "b" else upd_small)[i] for kind, i in order]
    loss = lax.psum(loss_part[0, 0], ("x", "y", "c"))
    result = [loss, grad_x[None]]
    for field in range(4):
        result += [u[field] for u in per_weight]
    return tuple(result)
```

```python
import math

import jax
import jax.numpy as jnp
from jax import lax
from jax.experimental import pallas as pl
from jax.experimental.pallas import tpu as pltpu

F32 = jnp.float32
BF16 = jnp.bfloat16
MESH = pl.DeviceIdType.MESH

EPS = 1e-6
LOG2E = 1.4426950408889634
WEIGHT_FLOOR_LOG2 = -200.0
LRU_C = 8.0
CONV_W = 4
HEAD_DIM = 128
ADAM_LR = 0.001
ADAM_B1 = 0.9
ADAM_B2 = 0.999
ADAM_EPS = 1e-08
ADAM_WD = 0.01
ADAM_STEP = 10

N_DEV = 8
LANES = 128
SUBLANES = 8
VMEM_LIMIT = 56 * 1024 * 1024

ATT_KEY_BLOCK = 256
ATT_QUERY_BLOCK = 256
ATT_STEP_BLOCKS = 4
SCAN_BLOCK = 256
ROW_BLOCK = 256
MM_TOKEN_BLOCK = 512
MM_WEIGHT_TILE = 1280
MM_CONTRACT_TOKENS = 2048
ANY = pl.BlockSpec(memory_space=pl.ANY)


def _pcall(body, **kw):
    return pl.pallas_call(body, **kw)


def _params(*sem):
    return pltpu.CompilerParams(dimension_semantics=sem, vmem_limit_bytes=VMEM_LIMIT)


def _pick(n, cap):
    if n <= cap:
        return n
    best = None
    for t in range(LANES, cap + 1, LANES):
        if n % t == 0:
            best = t
    assert best is not None, (n, cap)
    return best


def _sigmoid(x):
    return 1.0 / (1.0 + jnp.exp(-x))


def _dot(a, b, ca, cb):
    return lax.dot_general(a, b, (((ca,), (cb,)), ((), ())), preferred_element_type=F32)


def _mm_nn(a, w, *, name, out_dtype, col_off=0, cols=None, res=None):
    T, K = a.shape
    K2, N = w.shape
    assert K == K2
    cols = N if cols is None else cols
    tm = min(T, MM_TOKEN_BLOCK)
    tn = _pick(cols, MM_WEIGHT_TILE)
    assert col_off % tn == 0
    off = col_off // tn
    has_res = res is not None

    def body(a_ref, b_ref, *rest):
        o_ref = rest[-1]
        acc = jnp.dot(a_ref[...].astype(BF16), b_ref[...], preferred_element_type=F32)
        if has_res:
            acc = acc + rest[0][...]
        o_ref[...] = acc.astype(out_dtype)

    in_specs = [pl.BlockSpec((tm, K), lambda j, i: (i, 0)),
                pl.BlockSpec((K, tn), lambda j, i: (0, off + j))]
    args = [a, w]
    if has_res:
        in_specs.append(pl.BlockSpec((tm, tn), lambda j, i: (i, j)))
        args.append(res)
    return _pcall(
        body, name=name, grid=(cols // tn, T // tm), in_specs=in_specs,
        out_specs=pl.BlockSpec((tm, tn), lambda j, i: (i, j)),
        out_shape=jax.ShapeDtypeStruct((T, cols), out_dtype),
        compiler_params=_params("parallel", "parallel"))(*args)


def _mm_nt(a, w, *, name, out_dtype=F32, rider=None):
    parts = a.shape[0] if a.ndim == 3 else 1
    T, kp = a.shape[-2:]
    N, K = w.shape
    assert K == parts * kp
    tm = min(T, MM_TOKEN_BLOCK)
    tn = _pick(N, MM_WEIGHT_TILE)
    nj, ni = N // tn, T // tm
    rider = rider or _Rider([], [], [], None)
    nri, nro = len(rider.inputs), len(rider.out_shapes)

    def body(a_ref, b_ref, *refs):
        o_ref = refs[nri]
        j, i = pl.program_id(0), pl.program_id(1)
        if nro:
            start, finish = rider.bind(refs[:nri], refs[nri + 1:nri + 1 + nro], refs[nri + 1 + nro:])
            pl.when((j == 0) & (i == 0))(start)
        if a.ndim == 3:
            acc = None
            for p in range(parts):
                term = _dot(a_ref[p], b_ref[:, p * kp:(p + 1) * kp], 1, 1)
                acc = term if acc is None else acc + term
        else:
            acc = _dot(a_ref[...].astype(BF16), b_ref[...], 1, 1)
        o_ref[...] = acc.astype(out_dtype)
        if nro:
            pl.when((j == nj - 1) & (i == ni - 1))(finish)

    a_spec = (pl.BlockSpec((parts, tm, kp), lambda j, i: (0, i, 0)) if a.ndim == 3
              else pl.BlockSpec((tm, K), lambda j, i: (i, 0)))
    sem = ("arbitrary", "arbitrary") if nro else ("parallel", "parallel")
    out = _pcall(
        body, name=name, grid=(nj, ni),
        in_specs=[a_spec, pl.BlockSpec((tn, K), lambda j, i: (j, 0))] + [ANY] * nri,
        out_specs=[pl.BlockSpec((tm, tn), lambda j, i: (i, j))] + [ANY] * nro,
        out_shape=[jax.ShapeDtypeStruct((T, N), out_dtype)] + rider.out_shapes,
        scratch_shapes=rider.scratch,
        compiler_params=_params(*sem))(a, w, *rider.inputs)
    return out if nro else out[0]


def _mm_tn(a, b, *, name, shards=1):
    T, Ko = a.shape
    parts = b.shape[0] if b.ndim == 3 else 1
    T2, n_part = b.shape[-2:]
    N = parts * n_part
    assert T == T2
    n = N // shards
    tt = min(T, MM_CONTRACT_TOKENS)
    tko = _pick(Ko, 1024)
    tn = _pick(n, 1024)
    per = n // tn
    assert n_part % tn == 0
    per_part = n_part // tn

    def body(a_ref, b_ref, o_ref):
        t = pl.program_id(2)
        p = _dot(a_ref[...].astype(BF16), b_ref[...].astype(BF16), 0, 0)

        @pl.when(t == 0)
        def _():
            o_ref[...] = p

        @pl.when(t > 0)
        def _():
            o_ref[...] += p

    if shards == 1:
        out_spec = pl.BlockSpec((tko, tn), lambda i, j, t: (i, j))
        out_shape = jax.ShapeDtypeStruct((Ko, N), F32)
    else:
        out_spec = pl.BlockSpec((None, tko, tn), lambda i, j, t: (j // per, i, j % per))
        out_shape = jax.ShapeDtypeStruct((shards, Ko, n), F32)
    b_spec = (pl.BlockSpec((None, tt, tn), lambda i, j, t: (j // per_part, t, j % per_part)) if b.ndim == 3
              else pl.BlockSpec((tt, tn), lambda i, j, t: (t, j)))
    return _pcall(
        body, name=name, grid=(Ko // tko, N // tn, T // tt),
        in_specs=[pl.BlockSpec((tt, tko), lambda i, j, t: (t, i)), b_spec],
        out_specs=out_spec, out_shape=out_shape,
        compiler_params=_params("parallel", "parallel", "arbitrary"))(a, b)


def _rms_fwd(x, gains, *, name, riders=()):
    T, D = x.shape
    tm = min(T, ROW_BLOCK)
    steps = T // tm
    n, nr = len(gains), len(riders)

    def body(x_ref, *refs):
        g_refs, rider_in = refs[:n], refs[n:n + nr]
        o_refs, rider_out = refs[n + nr:2 * n + nr], refs[2 * n + nr:2 * n + 2 * nr]
        if nr:
            gather = _Gather(rider_in, rider_out, *refs[2 * n + 2 * nr:])
            pl.when(pl.program_id(0) == 0)(gather.start)
        xv = x_ref[...]
        xh = xv * lax.rsqrt(jnp.mean(xv * xv, axis=-1, keepdims=True) + EPS)
        for g_ref, o_ref in zip(g_refs, o_refs):
            o_ref[...] = (xh * g_ref[...]).astype(BF16)
        if nr:
            pl.when(pl.program_id(0) == steps - 1)(gather.finish)

    row = pl.BlockSpec((tm, D), lambda i: (i, 0))
    vec = pl.BlockSpec((1, D), lambda i: (0, 0))
    return _pcall(
        body, name=name, grid=(steps,), in_specs=[row] + [vec] * n + [ANY] * nr,
        out_specs=[row] * n + [ANY] * nr,
        out_shape=[jax.ShapeDtypeStruct((T, D), BF16)] * n
                  + [jax.ShapeDtypeStruct((N_DEV,) + r.shape, r.dtype) for r in riders],
        scratch_shapes=_Gather.scratch(nr) if nr else [],
        compiler_params=_params("arbitrary" if nr else "parallel"))(x, *gains, *riders)


def _rms_bwd(x, dres, dhs, gains, *, name, rider=None):
    T, D = x.shape
    tm = min(T, ROW_BLOCK)
    steps = T // tm
    n = len(gains)
    rider = rider or _Rider([], [], [], None)
    nri, nro = len(rider.inputs), len(rider.out_shapes)

    def body(x_ref, dres_ref, *refs):
        dh_refs, g_refs = refs[:n], refs[n:2 * n]
        refs = refs[2 * n:]
        rider_in, refs = refs[:nri], refs[nri:]
        dx_ref, dg_refs = refs[0], refs[1:1 + n]
        rider_out, sems = refs[1 + n:1 + n + nro], refs[1 + n + nro:]
        i = pl.program_id(0)
        if nro:
            start, finish = rider.bind(rider_in, rider_out, sems)
            pl.when(i == 0)(start)
        xv = x_ref[...]
        r = lax.rsqrt(jnp.mean(xv * xv, axis=-1, keepdims=True) + EPS)
        xh = xv * r
        dxh = jnp.zeros_like(xv)
        for dh_ref, g_ref, dg_ref in zip(dh_refs, g_refs, dg_refs):
            dh = dh_ref[...]
            part = jnp.sum(dh * xh, axis=0, keepdims=True)

            @pl.when(i == 0)
            def _():
                dg_ref[...] = part

            @pl.when(i > 0)
            def _():
                dg_ref[...] += part

            dxh = dxh + dh * g_ref[...]
        dx_ref[...] = dres_ref[...] + r * (dxh - xh * jnp.mean(dxh * xh, axis=-1, keepdims=True))
        if nro:
            pl.when(i == steps - 1)(finish)

    row = pl.BlockSpec((tm, D), lambda i: (i, 0))
    vec = pl.BlockSpec((1, D), lambda i: (0, 0))
    return _pcall(
        body, name=name, grid=(steps,), in_specs=[row, row] + [row] * n + [vec] * n + [ANY] * nri,
        out_specs=[row] + [vec] * n + [ANY] * nro,
        out_shape=[jax.ShapeDtypeStruct((T, D), F32)] + [jax.ShapeDtypeStruct((1, D), F32)] * n
                  + rider.out_shapes,
        scratch_shapes=rider.scratch,
        compiler_params=_params("arbitrary"))(x, dres, *dhs, *gains, *rider.inputs)


def _final_loss(x, target, gain, *, name):
    T, D = x.shape
    tm = min(T, ROW_BLOCK)

    def body(x_ref, t_ref, g_ref, dx_ref, dg_ref, loss_ref):
        i = pl.program_id(0)
        xv = x_ref[...]
        g = g_ref[...]
        r = lax.rsqrt(jnp.mean(xv * xv, axis=-1, keepdims=True) + EPS)
        xh = xv * r
        err = xh * g - t_ref[...]
        part_loss = 0.5 * jnp.sum(jnp.mean(err * err, axis=-1, keepdims=True), axis=0, keepdims=True)
        dy = err * (1.0 / D)
        part_g = jnp.sum(dy * xh, axis=0, keepdims=True)

        @pl.when(i == 0)
        def _():
            dg_ref[...] = part_g
            loss_ref[...] = jnp.broadcast_to(part_loss, loss_ref.shape)

        @pl.when(i > 0)
        def _():
            dg_ref[...] += part_g
            loss_ref[...] += jnp.broadcast_to(part_loss, loss_ref.shape)

        dxh = dy * g
        dx_ref[...] = r * (dxh - xh * jnp.mean(dxh * xh, axis=-1, keepdims=True))

    row = pl.BlockSpec((tm, D), lambda i: (i, 0))
    vec = pl.BlockSpec((1, D), lambda i: (0, 0))
    return _pcall(
        body, name=name, grid=(T // tm,), in_specs=[row, row, vec],
        out_specs=[row, vec, pl.BlockSpec((1, LANES), lambda i: (0, 0))],
        out_shape=[jax.ShapeDtypeStruct((T, D), F32), jax.ShapeDtypeStruct((1, D), F32),
                   jax.ShapeDtypeStruct((1, LANES), F32)],
        compiler_params=_params("arbitrary"))(x, target, gain)


def _shift_down(x, prev_tail, j, row):
    tb = x.shape[0]
    prev = jnp.tile(prev_tail, (tb // SUBLANES, 1))
    return jnp.where(row >= j, pltpu.roll(x, j, 0), pltpu.roll(prev, j, 0))


def _shift_up(x, next_head, j, row):
    tb = x.shape[0]
    nxt = jnp.tile(next_head, (tb // SUBLANES, 1))
    return jnp.where(row < tb - j, pltpu.roll(x, tb - j, 0), pltpu.roll(nxt, tb - j, 0))


def _lru_gates(xb, wr, wi, br, bi, lam):
    xbb = xb.astype(BF16)
    r = _sigmoid(jnp.dot(xbb, wr, preferred_element_type=F32) + br)
    i = _sigmoid(jnp.dot(xbb, wi, preferred_element_type=F32) + bi)
    sp = jnp.maximum(-lam, 0.0) + jnp.log1p(jnp.exp(-jnp.abs(lam)))
    log_a = (-LRU_C) * r * sp
    a = jnp.exp(log_a)
    a2 = a * a
    mult = jnp.sqrt(jnp.maximum(-jnp.tanh(log_a) * (1.0 + a2), 0.0))
    return xbb, r, i, sp, a, a2, mult


def _scan_rows(coef, val, edge, reverse):
    tb, C = coef.shape
    a, b = coef, val
    row = lax.broadcasted_iota(jnp.int32, (tb, C), 0)
    s = 1
    while s < tb:
        m = (row < tb - s) if reverse else (row >= s)
        shift = tb - s if reverse else s
        b = jnp.where(m, a * pltpu.roll(b, shift, 0) + b, b)
        a = jnp.where(m, a * pltpu.roll(a, shift, 0), a)
        s *= 2
    return b + a * edge


def _acore_fwd(proj, conv_w, conv_b, w_r, w_i, b_r, b_i, lam, *, name, riders=()):
    T, C2 = proj.shape
    C = C2 // 2
    nb, bw, _ = w_r.shape
    tb = min(T, SCAN_BLOCK)
    nt = T // tb
    nr = len(riders)

    def body(xp_ref, gate_ref, cw_ref, cb_ref, wr_ref, wi_ref, br_ref, bi_ref, lam_ref, *refs):
        rider_in, refs = refs[:nr], refs[nr:]
        xb_ref, h_ref, yg_ref = refs[:3]
        rider_out, refs = refs[3:3 + nr], refs[3 + nr:]
        tail_ref, hlast_ref = refs[:2]
        t = pl.program_id(1)
        if nr:
            gather = _Gather(rider_in, rider_out, *refs[2:])
            pl.when((pl.program_id(0) == 0) & (t == 0))(gather.start)

        @pl.when(t == 0)
        def _():
            tail_ref[...] = jnp.zeros_like(tail_ref)
            hlast_ref[...] = jnp.zeros_like(hlast_ref)

        row = lax.broadcasted_iota(jnp.int32, (tb, bw), 0)
        xp = xp_ref[...]
        tail = tail_ref[...]
        xb = cb_ref[...] + cw_ref[CONV_W - 1:CONV_W, :] * xp
        for j in range(1, CONV_W):
            xb = xb + cw_ref[CONV_W - 1 - j:CONV_W - j, :] * _shift_down(xp, tail, j, row)
        tail_ref[...] = xp[tb - SUBLANES:, :]
        xb_ref[...] = xb

        _, r, i, sp, a, a2, mult = _lru_gates(xb, wr_ref[...], wi_ref[...], br_ref[...], bi_ref[...],
                                              lam_ref[...])
        h = _scan_rows(a, mult * (i * xb), hlast_ref[SUBLANES - 1:SUBLANES, :], False)
        hlast_ref[...] = h[tb - SUBLANES:, :]
        h_ref[...] = h
        gate = gate_ref[...]
        yg_ref[...] = (h * (gate * _sigmoid(gate))).astype(BF16)
        if nr:
            pl.when((pl.program_id(0) == nb - 1) & (t == nt - 1))(gather.finish)

    blk = lambda off: pl.BlockSpec((tb, bw), lambda n, t: (t, off + n))
    vec = pl.BlockSpec((1, bw), lambda n, t: (0, n))
    wspec = pl.BlockSpec((None, bw, bw), lambda n, t: (n, 0, 0))
    return _pcall(
        body, name=name, grid=(nb, nt),
        in_specs=[blk(0), blk(nb), pl.BlockSpec((CONV_W, bw), lambda n, t: (0, n)), vec, wspec, wspec,
                  vec, vec, vec] + [ANY] * nr,
        out_specs=[blk(0), blk(0), blk(0)] + [ANY] * nr,
        out_shape=[jax.ShapeDtypeStruct((T, C), F32), jax.ShapeDtypeStruct((T, C), F32),
                   jax.ShapeDtypeStruct((T, C), BF16)]
                  + [jax.ShapeDtypeStruct((N_DEV,) + r.shape, r.dtype) for r in riders],
        scratch_shapes=[pltpu.VMEM((SUBLANES, bw), F32), pltpu.VMEM((SUBLANES, bw), F32)]
                       + (_Gather.scratch(nr) if nr else []),
        compiler_params=_params("arbitrary" if nr else "parallel", "arbitrary"))(
            proj, proj, conv_w, conv_b, w_r, w_i, b_r, b_i, lam, *riders)


def _acore_bwd(dyg, proj, xb_all, h_all, conv_w, w_r, w_i, b_r, b_i, lam, *, name, rider=None):
    T, C2 = proj.shape
    C = C2 // 2
    nb, bw, _ = w_r.shape
    tb = min(T, SCAN_BLOCK)
    nt = T // tb
    per8 = tb // SUBLANES
    rider = rider or _Rider([], [], [], None)
    nri, nro = len(rider.inputs), len(rider.out_shapes)

    def body(dyg_ref, xp_ref, gate_ref, xb_ref, h_ref, xp_prev_ref, h_prev_ref, cw_ref,
             wr_ref, wi_ref, br_ref, bi_ref, lam_ref, *refs):
        rider_in, refs = refs[:nri], refs[nri:]
        dproj_ref, dcw_ref, dcb_ref, dbr_ref, dbi_ref, dlam_ref, dwr_ref, dwi_ref = refs[:8]
        rider_out, refs = refs[8:8 + nro], refs[8 + nro:]
        gh_next_ref, a_next_ref, dxb_next_ref = refs[:3]
        step = pl.program_id(1)
        first_block = step == nt - 1
        if nro:
            start, finish = rider.bind(rider_in, rider_out, refs[3:])
            pl.when((pl.program_id(0) == 0) & (step == 0))(start)

        @pl.when(step == 0)
        def _():
            gh_next_ref[...] = jnp.zeros_like(gh_next_ref)
            a_next_ref[...] = jnp.zeros_like(a_next_ref)
            dxb_next_ref[...] = jnp.zeros_like(dxb_next_ref)

        row = lax.broadcasted_iota(jnp.int32, (tb, bw), 0)
        keep = jnp.where(first_block, 0.0, 1.0)
        h_prev = h_prev_ref[...] * keep
        xp_prev = xp_prev_ref[...] * keep
        xp, gate, xb, h, dyg_v = xp_ref[...], gate_ref[...], xb_ref[...], h_ref[...], dyg_ref[...]
        lam_v = lam_ref[...]
        wr, wi = wr_ref[...], wi_ref[...]

        sg = _sigmoid(gate)
        dh = dyg_v * (gate * sg)
        dproj_ref[1] = (dyg_v * h * (sg * (1.0 + gate * (1.0 - sg)))).astype(BF16)

        xbb, r, i, sp, a, a2, mult = _lru_gates(xb, wr, wi, br_ref[...], bi_ref[...], lam_v)

        gh = _scan_rows(_shift_up(a, a_next_ref[...], 1, row), dh, gh_next_ref[0:1, :], True)
        gh_next_ref[...] = gh[0:SUBLANES, :]
        a_next_ref[...] = a[0:SUBLANES, :]

        da = gh * _shift_down(h, h_prev, 1, row)
        dmult = gh * (i * xb)
        di = gh * mult * xb
        dxb = gh * mult * i
        dla = da * a - dmult * jnp.where(mult > 0.0, a2 / mult, 0.0)
        dr = dla * ((-LRU_C) * sp)
        dsp = jnp.sum(dla * ((-LRU_C) * r), axis=0, keepdims=True)
        dlam_part = dsp * (-_sigmoid(-lam_v))
        dpr = dr * r * (1.0 - r)
        dpi = di * i * (1.0 - i)
        dbr_part = jnp.sum(dpr, axis=0, keepdims=True)
        dbi_part = jnp.sum(dpi, axis=0, keepdims=True)
        dprb, dpib = dpr.astype(BF16), dpi.astype(BF16)
        dwr_part = _dot(xbb, dprb, 0, 0)
        dwi_part = _dot(xbb, dpib, 0, 0)
        dxb = dxb + _dot(dprb, wr, 1, 1) + _dot(dpib, wi, 1, 1)

        dxb_next = dxb_next_ref[...]
        dxp = cw_ref[CONV_W - 1:CONV_W, :] * dxb
        for j in range(1, CONV_W):
            dxp = dxp + cw_ref[CONV_W - 1 - j:CONV_W - j, :] * _shift_up(dxb, dxb_next, j, row)
        dxb_next_ref[...] = dxb[0:SUBLANES, :]
        dproj_ref[0] = dxp.astype(BF16)
        dcb_part = jnp.sum(dxb, axis=0, keepdims=True)
        dcw_rows = []
        for k in range(CONV_W):
            j = CONV_W - 1 - k
            sh = xp if j == 0 else _shift_down(xp, xp_prev, j, row)
            dcw_rows.append(jnp.sum(dxb * sh, axis=0, keepdims=True))

        @pl.when(step == 0)
        def _():
            for k in range(CONV_W):
                dcw_ref[k:k + 1, :] = dcw_rows[k]
            dcb_ref[...] = dcb_part
            dbr_ref[...] = dbr_part
            dbi_ref[...] = dbi_part
            dlam_ref[...] = dlam_part
            dwr_ref[...] = dwr_part
            dwi_ref[...] = dwi_part

        @pl.when(step > 0)
        def _():
            for k in range(CONV_W):
                dcw_ref[k:k + 1, :] += dcw_rows[k]
            dcb_ref[...] += dcb_part
            dbr_ref[...] += dbr_part
            dbi_ref[...] += dbi_part
            dlam_ref[...] += dlam_part
            dwr_ref[...] += dwr_part
            dwi_ref[...] += dwi_part

        if nro:
            pl.when((pl.program_id(0) == nb - 1) & (step == nt - 1))(finish)

    rev = lambda s: nt - 1 - s
    blk = lambda off: pl.BlockSpec((tb, bw), lambda n, s: (rev(s), off + n))
    prev8 = lambda off: pl.BlockSpec(
        (SUBLANES, bw), lambda n, s: (jnp.maximum(rev(s) * per8 - 1, 0), off + n))
    vec = pl.BlockSpec((1, bw), lambda n, s: (0, n))
    wspec = pl.BlockSpec((None, bw, bw), lambda n, s: (n, 0, 0))
    cwspec = pl.BlockSpec((CONV_W, bw), lambda n, s: (0, n))
    vshape = jax.ShapeDtypeStruct((1, C), F32)
    wshape = jax.ShapeDtypeStruct((nb, bw, bw), F32)
    return _pcall(
        body, name=name, grid=(nb, nt),
        in_specs=[blk(0), blk(0), blk(nb), blk(0), blk(0), prev8(0), prev8(0), cwspec,
                  wspec, wspec, vec, vec, vec] + [ANY] * nri,
        out_specs=[pl.BlockSpec((2, tb, bw), lambda n, s: (0, rev(s), n)), cwspec, vec, vec, vec, vec,
                   wspec, wspec] + [ANY] * nro,
        out_shape=[jax.ShapeDtypeStruct((2, T, C), BF16),
                   jax.ShapeDtypeStruct((CONV_W, C), F32), vshape, vshape, vshape, vshape,
                   wshape, wshape] + rider.out_shapes,
        scratch_shapes=[pltpu.VMEM((SUBLANES, bw), F32)] * 3 + rider.scratch,
        compiler_params=_params("arbitrary" if nro else "parallel", "arbitrary"))(
            dyg, proj, proj, xb_all, h_all, proj, h_all, conv_w, w_r, w_i, b_r, b_i, lam, *rider.inputs)


def _later_sum(lk, tri):
    return jnp.dot(lk.astype(BF16), tri, preferred_element_type=F32)


def _log2_sigmoids(y):
    t = jnp.log(1.0 + jnp.exp2(-jnp.abs(y))) * LOG2E
    ls = jnp.minimum(y, 0.0) - t
    return ls, ls - y


def _attn_blocks(T):
    bk = min(T, ATT_KEY_BLOCK)
    bq = min(T, ATT_QUERY_BLOCK)
    return bq, bk, bq // bk


def _attn_step_blocks(T, bq):
    return ATT_STEP_BLOCKS if (T // bq) % ATT_STEP_BLOCKS == 0 else 1


def _attn_fwd(q, kv, gate, *, name):
    T, HD = q.shape
    H = HD // HEAD_DIM
    bq, bk, per = _attn_blocks(T)
    scale = 1.0 / math.sqrt(HEAD_DIM)

    nsub = _attn_step_blocks(T, bq)

    def body(q_ref, k_ref, v_ref, g_ref, o_ref, og_ref, lt_ref, w_ref):
        for sub in range(nsub):
            rows = pl.ds(sub * bq, bq)
            one_block(pl.program_id(1) * nsub + sub, q_ref.at[rows], k_ref, v_ref, g_ref.at[rows],
                      o_ref.at[rows], og_ref.at[rows], lt_ref.at[rows], w_ref)

    def one_block(i, q_ref, k_ref, v_ref, g_ref, o_ref, og_ref, lt_ref, w_ref):
        qv = q_ref[...]
        tr = lax.broadcasted_iota(jnp.int32, (bk, bk), 0)
        tc = lax.broadcasted_iota(jnp.int32, (bk, bk), 1)
        tri = (tr > tc).astype(BF16)
        ahead = (lax.broadcasted_iota(jnp.int32, (bq, bk), 0)
                 - lax.broadcasted_iota(jnp.int32, (bq, bk), 1))

        def starts_of(top):
            return [pl.multiple_of((top - d) * bk, bk) for d in range(per)]

        def scores(top):
            return [_dot(qv, k_ref[pl.ds(ks, bk), :], 1, 1) for ks in starts_of(top)]

        def weights(top, zs, c, mask):
            lss, sums, css, causals = [], [], [], []
            for ks, z in zip(starts_of(top), zs):
                ls, lk = _log2_sigmoids(z * (scale * LOG2E))
                if mask:
                    causals.append(ahead > ks - i * bq)
                    lk = jnp.where(causals[-1], lk, 0.0)
                lss.append(ls)
                sums.append(jnp.sum(lk, axis=1, keepdims=True))
                css.append(_later_sum(lk, tri))
            for d in range(per):
                w = jnp.exp2(lss[d] + (css[d] + c))
                if mask:
                    w = jnp.where(causals[d], w, 0.0)
                w_ref[d] = w.astype(BF16)
                c = c + sums[d]
            return c

        def values(top, acc):
            for d, ks in enumerate(starts_of(top)):
                acc = acc + jnp.dot(w_ref[d], v_ref[pl.ds(ks, bk), :], preferred_element_type=F32)
            return acc

        def more(state):
            gg, _, _, largest = state
            return (gg <= i) & (largest > WEIGHT_FLOOR_LOG2)

        def step(state):
            gg, acc, c, _ = state
            top = (i - gg) * per + per - 1
            zs = scores(top)
            acc = values(top + per, acc)
            c = weights(top, zs, c, False)
            return gg + 1, acc, c, jnp.max(c)

        diag_top = i * per + per - 1
        c = weights(diag_top, scores(diag_top), jnp.zeros((bq, 1), F32), True)
        gg, acc, c, _ = lax.while_loop(more, step, (1, jnp.zeros((bq, HEAD_DIM), F32), c, jnp.max(c)))
        acc = values((i - gg + 1) * per + per - 1, acc)
        o_ref[...] = acc
        g = g_ref[...]
        og_ref[...] = (acc * (g * _sigmoid(g))).astype(BF16)
        lane = lax.broadcasted_iota(jnp.int32, (bq, HEAD_DIM), 1)
        lt_ref[...] = jnp.where(lane == 1, (i - gg + 1).astype(F32), jnp.broadcast_to(c, (bq, HEAD_DIM)))

    qspec = pl.BlockSpec((nsub * bq, HEAD_DIM), lambda h, i: (i, h))
    return _pcall(
        body, name=name, grid=(H, T // (nsub * bq)),
        in_specs=[qspec, pl.BlockSpec((T, HEAD_DIM), lambda h, i: (0, h)),
                  pl.BlockSpec((T, HEAD_DIM), lambda h, i: (0, H + h)), qspec],
        out_specs=[qspec, qspec, qspec],
        out_shape=[jax.ShapeDtypeStruct((T, HD), F32), jax.ShapeDtypeStruct((T, HD), BF16),
                   jax.ShapeDtypeStruct((T, HD), F32)],
        scratch_shapes=[pltpu.VMEM((per, bq, bk), BF16)],
        compiler_params=_params("parallel", "arbitrary"))(q, kv, kv, gate)


def _attn_bwd(q, kv, gate, o, ltot, dog, *, name):
    T, HD = q.shape
    H = HD // HEAD_DIM
    bq, bk, per = _attn_blocks(T)
    nq = T // bq
    scale = 1.0 / math.sqrt(HEAD_DIM)

    nsub = _attn_step_blocks(T, bq)

    def body(q_ref, k_ref, v_ref, g_ref, o_ref, lt_ref, dog_ref,
             dqg_ref, dkv_ref, dk_acc, dv_acc, dz_ref, w_ref):
        for sub in range(nsub):
            rows = pl.ds(sub * bq, bq)
            one_block(pl.program_id(1) * nsub + sub, q_ref.at[rows], k_ref, v_ref, g_ref.at[rows],
                      o_ref.at[rows], lt_ref.at[rows], dog_ref.at[rows], dqg_ref.at[:, rows], dkv_ref,
                      dk_acc, dv_acc, dz_ref, w_ref)

    def one_block(i, q_ref, k_ref, v_ref, g_ref, o_ref, lt_ref, dog_ref,
                  dqg_ref, dkv_ref, dk_acc, dv_acc, dz_ref, w_ref):
        @pl.when(i == 0)
        def _():
            dk_acc[...] = jnp.zeros_like(dk_acc)
            dv_acc[...] = jnp.zeros_like(dv_acc)

        qv = q_ref[...]
        g, ov, dogv = g_ref[...], o_ref[...], dog_ref[...]
        sg = _sigmoid(g)
        do = dogv * (g * sg)
        dqg_ref[1] = (dogv * ov * (sg * (1.0 + g * (1.0 - sg)))).astype(BF16)
        dob = do.astype(BF16)
        ltot_v = lt_ref[:, 0:1]
        tr = lax.broadcasted_iota(jnp.int32, (bk, bk), 0)
        tc = lax.broadcasted_iota(jnp.int32, (bk, bk), 1)
        tri_later = (tr > tc).astype(BF16)
        tri_excl = (tr < tc).astype(BF16)
        ahead = (lax.broadcasted_iota(jnp.int32, (bq, bk), 0)
                 - lax.broadcasted_iota(jnp.int32, (bq, bk), 1))

        def starts_of(first):
            return [pl.multiple_of((first + d) * bk, bk) for d in range(per)]

        def scores(first):
            return ([_dot(qv, k_ref[pl.ds(ks, bk), :], 1, 1) for ks in starts_of(first)],
                    [_dot(dob, v_ref[pl.ds(ks, bk), :], 1, 1) for ks in starts_of(first)])

        def front(first, zs, dws, p_lk, p_g, mask):
            lss, css, causals = [], [], []
            for ks, z in zip(starts_of(first), zs):
                ls, lk = _log2_sigmoids(z * (scale * LOG2E))
                if mask:
                    causals.append(ahead > ks - i * bq)
                    lk = jnp.where(causals[-1], lk, 0.0)
                lss.append(ls)
                p_lk = p_lk + jnp.sum(lk, axis=1, keepdims=True)
                css.append((ltot_v - p_lk) + _later_sum(lk, tri_later))
            gms, befores = [], []
            for d in range(per):
                w = jnp.exp2(lss[d] + css[d])
                if mask:
                    w = jnp.where(causals[d], w, 0.0)
                gm = dws[d] * w
                gms.append(gm)
                w_ref[d] = w.astype(BF16)
                befores.append(jnp.dot(gm.astype(BF16), tri_excl, preferred_element_type=F32) + p_g)
                p_g = p_g + jnp.sum(gm, axis=1, keepdims=True)
            for d in range(per):
                dz = gms[d] - jnp.exp2(lss[d]) * (gms[d] + befores[d])
                if mask:
                    dz = jnp.where(causals[d], dz, 0.0)
                dz_ref[d] = (dz * scale).astype(BF16)
            return p_lk, p_g

        def back(first, dq):
            for d, ks in enumerate(starts_of(first)):
                dzb = dz_ref[d]
                dq = dq + jnp.dot(dzb, k_ref[pl.ds(ks, bk), :], preferred_element_type=F32)
                dk_acc[pl.ds(ks, bk), :] += _dot(dzb, qv, 0, 0)
                dv_acc[pl.ds(ks, bk), :] += _dot(w_ref[d], dob, 0, 0)
            return dq

        def step(mask):
            def trip(g, state):
                dq, p_lk, p_g = state
                zs, dws = scores(g * per)
                dq = back((g - 1) * per, dq)
                return (dq,) + front(g * per, zs, dws, p_lk, p_g, mask)
            return trip

        g0 = jnp.max(lt_ref[0:1, 1:2]).astype(jnp.int32)
        zero = jnp.zeros((bq, 1), F32)
        state = (jnp.zeros((bq, HEAD_DIM), F32),) + front(g0 * per, *scores(g0 * per), zero, zero, True)
        state = lax.fori_loop(g0 + 1, i, step(False), state)
        state = lax.fori_loop(jnp.maximum(i, g0 + 1), i + 1, step(True), state)
        dqg_ref[0] = back(i * per, state[0]).astype(BF16)

        @pl.when(i == nq - 1)
        def _():
            dkv_ref[0] = dk_acc[...].astype(BF16)
            dkv_ref[1] = dv_acc[...].astype(BF16)

    qspec = pl.BlockSpec((nsub * bq, HEAD_DIM), lambda h, i: (i, h))
    kspec = pl.BlockSpec((T, HEAD_DIM), lambda h, i: (0, h))
    return _pcall(
        body, name=name, grid=(H, nq // nsub),
        in_specs=[qspec, kspec, pl.BlockSpec((T, HEAD_DIM), lambda h, i: (0, H + h)),
                  qspec, qspec, qspec, qspec],
        out_specs=[pl.BlockSpec((2, nsub * bq, HEAD_DIM), lambda h, i: (0, i, h)),
                   pl.BlockSpec((2, T, HEAD_DIM), lambda h, i: (0, 0, h))],
        out_shape=[jax.ShapeDtypeStruct((2, T, HD), BF16)] * 2,
        scratch_shapes=[pltpu.VMEM((T, HEAD_DIM), F32)] * 2 + [pltpu.VMEM((per, bq, bk), BF16)] * 2,
        compiler_params=_params("parallel", "arbitrary"))(q, kv, kv, gate, o, ltot, dog)


def _position():
    return lax.axis_index("x"), lax.axis_index("y"), lax.axis_index("c")


def _chip_of(k, x, y):
    return (1 - x if k & 1 else x), (1 - y if k & 2 else y)


class _Gather:
    @staticmethod
    def scratch(n):
        return [pltpu.SemaphoreType.DMA((n, 7)), pltpu.SemaphoreType.DMA((n, 7)),
                pltpu.SemaphoreType.DMA((n,))]

    def __init__(self, ins, outs, send_sems, recv_sems, local_sems):
        self.ins, self.outs, self.n = ins, outs, len(ins)
        self.send_sems, self.recv_sems, self.local_sems = send_sems, recv_sems, local_sems
        x, y, c = _position()
        self.me, self.sibling = (x, y, c), (x, y, 1 - c)
        self.chips = [_chip_of(k, x, y) for k in (1, 2, 3)]

    def copy(self, a, k, block, to, src=None):
        slot = self.outs[a].at[4 * block[0] + 2 * block[1] + block[2]]
        return pltpu.make_async_remote_copy(
            src_ref=slot if src is None else src, dst_ref=slot,
            send_sem=self.send_sems.at[a, k], recv_sem=self.recv_sems.at[a, k],
            device_id=to, device_id_type=MESH)

    def own_copies(self):
        x, y, c = self.me
        mine = [pltpu.make_async_copy(self.ins[a], self.outs[a].at[4 * x + 2 * y + c], self.local_sems.at[a])
                for a in range(self.n)]
        first = []
        for a in range(self.n):
            first.append(self.copy(a, 0, self.me, self.sibling, src=self.ins[a]))
            first += [self.copy(a, 1 + j, self.me, (*chip, c), src=self.ins[a])
                      for j, chip in enumerate(self.chips)]
        return mine, first

    def start(self):
        mine, first = self.own_copies()
        for cp in mine + first:
            cp.start()

    def finish(self):
        c = self.me[2]
        mine, first = self.own_copies()
        passed = []
        for j, chip in enumerate(self.chips):
            for a in range(self.n):
                self.copy(a, 1 + j, (*chip, c), self.me).wait_recv()
                fwd = self.copy(a, 4 + j, (*chip, c), self.sibling)
                fwd.start()
                passed.append(fwd)
        for a in range(self.n):
            self.copy(a, 0, self.sibling, self.me).wait_recv()
            for j, chip in enumerate(self.chips):
                self.copy(a, 4 + j, (*chip, 1 - c), self.me).wait_recv()
        for cp in first + passed:
            cp.wait_send()
        for cp in mine:
            cp.wait()


class _Rider:
    def __init__(self, inputs, out_shapes, scratch, copies):
        self.inputs, self.out_shapes, self.scratch, self.copies = inputs, out_shapes, scratch, copies

    def bind(self, ins, outs, sems):
        def start():
            for cp in self.copies(ins, outs, sems):
                cp.start()

        def finish():
            cps = self.copies(ins, outs, sems)
            for cp in cps:
                cp.wait_send()
            for cp in cps:
                cp.wait_recv()

        return start, finish


def _sibling_rider(grads):
    n = len(grads)

    def copies(ins, outs, sems):
        x, y, c = _position()
        return [pltpu.make_async_remote_copy(
            src_ref=ins[a].at[2 * chip + (1 - c)], dst_ref=outs[a].at[chip],
            send_sem=sems[0].at[a, chip], recv_sem=sems[1].at[a, chip],
            device_id=(x, y, 1 - c), device_id_type=MESH) for a in range(n) for chip in range(4)]

    return _Rider(list(grads), [jax.ShapeDtypeStruct((4,) + g.shape[1:], g.dtype) for g in grads],
                  [pltpu.SemaphoreType.DMA((n, 4)), pltpu.SemaphoreType.DMA((n, 4))], copies)


def _chips_rider(parts):
    n = len(parts)

    def copies(ins, outs, sems):
        x, y, c = _position()
        cps = []
        for a in range(n):
            for k in range(3):
                cx, cy = _chip_of(k + 1, x, y)
                cps.append(pltpu.make_async_remote_copy(
                    src_ref=ins[a].at[2 * cx + cy], dst_ref=outs[a].at[k],
                    send_sem=sems[0].at[a, k], recv_sem=sems[1].at[a, k],
                    device_id=(cx, cy, c), device_id_type=MESH))
        return cps

    return _Rider(list(parts), [jax.ShapeDtypeStruct((3,) + p.shape[1:], p.dtype) for p in parts],
                  [pltpu.SemaphoreType.DMA((n, 3)), pltpu.SemaphoreType.DMA((n, 3))], copies)


def _small_gather(small):
    def body(small_ref, small_all, send_sems, recv_sems, local_sem):
        x, y, c = _position()
        me = 4 * x + 2 * y + c
        peers = [(x ^ (m >> 2), y ^ ((m >> 1) & 1), c ^ (m & 1)) for m in range(1, N_DEV)]
        sends = [pltpu.make_async_remote_copy(
            src_ref=small_ref, dst_ref=small_all.at[me], send_sem=send_sems.at[m], recv_sem=recv_sems.at[m],
            device_id=peer, device_id_type=MESH) for m, peer in enumerate(peers)]
        own = pltpu.make_async_copy(small_ref, small_all.at[me], local_sem)
        for cp in sends + [own]:
            cp.start()
        for cp in sends:
            cp.wait_send()
        for m, (px, py, pc) in enumerate(peers):
            pltpu.make_async_remote_copy(
                src_ref=small_ref, dst_ref=small_all.at[4 * px + 2 * py + pc],
                send_sem=send_sems.at[m], recv_sem=recv_sems.at[m],
                device_id=(px, py, pc), device_id_type=MESH).wait_recv()
        own.wait()

    return _pcall(
        body, name="small_gather", in_specs=[ANY], out_specs=ANY,
        out_shape=jax.ShapeDtypeStruct((N_DEV,) + small.shape, small.dtype),
        scratch_shapes=[pltpu.SemaphoreType.DMA((7,)), pltpu.SemaphoreType.DMA((7,)),
                        pltpu.SemaphoreType.DMA])(small)


def _pair_sum(grad, got, *, name):
    _, R, C = got.shape
    tr = _pick8(R, max(2 * SUBLANES, (1 << 17) // C))

    def body(g_ref, b_ref, own_ref, ob_ref):
        north = lax.axis_index("c") == 1
        x1, y1 = lax.axis_index("x") == 1, lax.axis_index("y") == 1
        sums = []
        for chip in range(4):
            sums.append(jnp.where(north, g_ref[chip, 1], g_ref[chip, 0]) + b_ref[chip])
            ob_ref[chip] = sums[-1].astype(BF16)
        own_ref[...] = jnp.where(x1, jnp.where(y1, sums[3], sums[2]), jnp.where(y1, sums[1], sums[0]))

    spec = pl.BlockSpec((4, tr, C), lambda i: (0, i, 0))
    return _pcall(
        body, name=name, grid=(R // tr,),
        in_specs=[pl.BlockSpec((4, 2, tr, C), lambda i: (0, 0, i, 0)), spec],
        out_specs=[pl.BlockSpec((tr, C), lambda i: (i, 0)), spec],
        out_shape=[jax.ShapeDtypeStruct((R, C), F32), jax.ShapeDtypeStruct((4, R, C), BF16)],
        compiler_params=_params("parallel"))(grad.reshape(4, 2, R, C), got)


def _pick8(n, cap):
    if n <= cap:
        return n
    best = None
    for t in range(SUBLANES, cap + 1, SUBLANES):
        if n % t == 0:
            best = t
    assert best is not None, (n, cap)
    return best


def _adamw(w, m, v, parts, *, name):
    R, C = w.shape
    tr = _pick8(R, max(SUBLANES, (1 << 17) // C))
    c1 = 1.0 - ADAM_B1 ** ADAM_STEP
    c2 = 1.0 - ADAM_B2 ** ADAM_STEP
    np_ = len(parts)

    def body(w_ref, m_ref, v_ref, *refs):
        p_refs = refs[:np_]
        g_ref, d_ref, nm_ref, nv_ref = refs[np_:]
        g = None
        for p_ref in p_refs:
            terms = [p_ref[...]] if len(p_ref.shape) == 2 else [p_ref[k] for k in range(p_ref.shape[0])]
            for t in terms:
                g = t.astype(F32) if g is None else g + t.astype(F32)
        mn = ADAM_B1 * m_ref[...] + (1.0 - ADAM_B1) * g
        vn = ADAM_B2 * v_ref[...] + (1.0 - ADAM_B2) * (g * g)
        d_ref[...] = -ADAM_LR * ((mn / c1) / (jnp.sqrt(vn / c2) + ADAM_EPS) + ADAM_WD * w_ref[...])
        g_ref[...] = g
        nm_ref[...] = mn
        nv_ref[...] = vn

    spec = pl.BlockSpec((tr, C), lambda i: (i, 0))
    pspecs = [spec if p.ndim == 2 else pl.BlockSpec((p.shape[0], tr, C), lambda i: (0, i, 0)) for p in parts]
    return _pcall(
        body, name=name, grid=(R // tr,), in_specs=[spec] * 3 + pspecs, out_specs=[spec] * 4,
        out_shape=[jax.ShapeDtypeStruct((R, C), F32)] * 4,
        compiler_params=_params("parallel"))(w, m, v, *parts)


def _rows(a):
    return a.reshape(-1, LANES)


def _whole_from_columns(shards, *, name):
    S, K, n = shards.shape
    tk = _pick8(K, 1024)

    def body(s_ref, o_ref):
        o_ref[...] = s_ref[...]

    return _pcall(
        body, name=name, grid=(K // tk, S),
        in_specs=[pl.BlockSpec((None, tk, n), lambda i, s: (s, i, 0))],
        out_specs=pl.BlockSpec((tk, n), lambda i, s: (i, s)),
        out_shape=jax.ShapeDtypeStruct((K, S * n), shards.dtype),
        compiler_params=_params("parallel", "parallel"))(shards)


def _late_weights(a_w_out_rows, w_kv_cols, b_w_in_cols, b_w_out_rows):
    whole_rows = lambda g: g.reshape(g.shape[0] * g.shape[1], g.shape[2])
    return (whole_rows(a_w_out_rows), _whole_from_columns(w_kv_cols, name="w_kv_whole"),
            _whole_from_columns(b_w_in_cols, name="b_w_in_whole"), whole_rows(b_w_out_rows))


def _forward_backward(xs, target, a_norm, g_a_w_in, conv_w, conv_b, g_w_r, g_w_i, b_r, b_i, lam,
                      kv_norm, b_norm, final_norm, *, late_weights=None, late_shards=None, h_a=None):
    if h_a is None:
        (h_a,) = _rms_fwd(xs, [a_norm], name="a_norm_fwd")
    proj_a = _mm_nn(h_a, g_a_w_in, name="a_in_proj", out_dtype=F32)
    xb, h_rec, yg, *gathered = _acore_fwd(proj_a, conv_w, conv_b, g_w_r, g_w_i, b_r, b_i, lam,
                                          name="a_core_fwd", riders=late_shards or ())
    g_a_w_out, g_w_kv, g_b_w_in, g_b_w_out = _late_weights(*gathered) if late_shards else late_weights
    x1 = _mm_nn(yg, g_a_w_out, name="a_out_proj", out_dtype=F32, res=xs)
    hk, hb = _rms_fwd(x1, [kv_norm, b_norm], name="kv_b_norm_fwd")
    kv = _mm_nn(hk, g_w_kv, name="kv_proj", out_dtype=BF16)
    hd = g_b_w_in.shape[1] // 2
    q = _mm_nn(hb, g_b_w_in, name="q_proj", out_dtype=BF16, col_off=0, cols=hd)
    gate_b = _mm_nn(hb, g_b_w_in, name="b_gate_proj", out_dtype=F32, col_off=hd, cols=hd)
    o, og, ltot = _attn_fwd(q, kv, gate_b, name="attn_fwd")
    x2 = _mm_nn(og, g_b_w_out, name="b_out_proj", out_dtype=F32, res=x1)
    dx2, d_final_norm, loss_part = _final_loss(x2, target, final_norm, name="final_norm_loss")

    dog = _mm_nt(dx2, g_b_w_out, name="b_out_proj_bwd")
    dw_b_out = _mm_tn(og, dx2, name="b_out_proj_wgrad")
    dproj_b, dkv = _attn_bwd(q, kv, gate_b, o, ltot, dog, name="attn_bwd")
    dhb = _mm_nt(dproj_b, g_b_w_in, name="b_in_proj_bwd")
    dw_b_in = _mm_tn(hb, dproj_b, name="b_in_proj_wgrad", shards=N_DEV)
    dhk = _mm_nt(dkv, g_w_kv, name="kv_proj_bwd")
    dw_kv = _mm_tn(hk, dkv, name="kv_proj_wgrad", shards=N_DEV)
    early = [dw_kv, dw_b_in, dw_b_out.reshape(N_DEV, -1, dw_b_out.shape[1])] if late_shards else []
    dx1, d_b_norm, d_kv_norm, *got = _rms_bwd(x1, dx2, [dhb, dhk], [b_norm, kv_norm], name="kv_b_norm_bwd",
                                              rider=_sibling_rider(early) if early else None)
    early_sums = [_pair_sum(f_, g_, name=f"pair_sum_early_{i}") for i, (f_, g_) in enumerate(zip(early, got))]
    dyg = _mm_nt(dx1, g_a_w_out, name="a_out_proj_bwd")
    dw_a_out = _mm_tn(yg, dx1, name="a_out_proj_wgrad")
    (dproj_a, d_conv_w, d_conv_b, d_b_r, d_b_i, d_lambda, dw_r, dw_i, *early_others) = _acore_bwd(
        dyg, proj_a, xb, h_rec, conv_w, g_w_r, g_w_i, b_r, b_i, lam, name="a_core_bwd",
        rider=_chips_rider([s[1] for s in early_sums]) if early else None)
    dw_a_in = _mm_tn(h_a, dproj_a, name="a_in_proj_wgrad", shards=N_DEV)
    rows = dw_r.shape[1] // N_DEV
    lru = lambda dw: dw.reshape(-1, N_DEV, rows, dw.shape[2]).transpose(1, 0, 2, 3).reshape(N_DEV, -1, dw.shape[2])
    late = [dw_a_in, dw_a_out.reshape(N_DEV, -1, dw_a_out.shape[1]), lru(dw_r), lru(dw_i)] if late_shards else []
    dh_a, *got = _mm_nt(dproj_a, g_a_w_in, name="a_in_proj_bwd", rider=_sibling_rider(late)) if late else (
        _mm_nt(dproj_a, g_a_w_in, name="a_in_proj_bwd"),)
    late_sums = [_pair_sum(f_, g_, name=f"pair_sum_late_{i}") for i, (f_, g_) in enumerate(zip(late, got))]
    grad_x, d_a_norm, *late_others = _rms_bwd(xs, dx1, [dh_a], [a_norm], name="a_norm_bwd",
                                              rider=_chips_rider([s[1] for s in late_sums]) if late else None)
    sums = late_sums[:2] + early_sums + late_sums[2:]
    others = late_others[:2] + early_others + late_others[2:]
    return (loss_part, grad_x, dw_a_in, dw_a_out, dw_kv, dw_b_in, dw_b_out, dw_r, dw_i, d_a_norm,
            d_conv_w, d_conv_b, d_b_r, d_b_i, d_lambda, d_kv_norm, d_b_norm, d_final_norm, sums, others)


def kernel(x, a_norm, a_w_in, a_conv_w, a_conv_b, a_w_r, a_b_r, a_w_i, a_b_i, a_lambda, a_w_out, kv_norm, w_kv, b_norm, b_w_in, b_w_out, final_norm, loss_target, m_a_norm, m_a_w_in, m_a_conv_w, m_a_conv_b, m_a_w_r, m_a_b_r, m_a_w_i, m_a_b_i, m_a_lambda, m_a_w_out, m_kv_norm, m_w_kv, m_b_norm, m_b_w_in, m_b_w_out, m_final_norm, v_a_norm, v_a_w_in, v_a_conv_w, v_a_conv_b, v_a_w_r, v_a_b_r, v_a_w_i, v_a_b_i, v_a_lambda, v_a_w_out, v_kv_norm, v_w_kv, v_b_norm, v_b_w_in, v_b_w_out, v_final_norm):
    T, D = x.shape[1], x.shape[2]
    nb, bw = a_w_r.shape[1], a_w_r.shape[3]
    C = nb * bw
    me = 4 * lax.axis_index("x") + 2 * lax.axis_index("y") + lax.axis_index("c")
    xs = x[0]
    target = loss_target[0]

    rows_r = a_w_r.shape[2]
    small_f32 = jnp.concatenate([_rows(a_conv_w[0]), _rows(b_norm[0])], axis=0)
    pad = (-small_f32.shape[0]) % SUBLANES
    small_f32 = jnp.pad(small_f32, ((0, pad), (0, 0)))
    h_a, a_w_in_cols, w_r_rows, w_i_rows, small_all = _rms_fwd(
        xs, [a_norm], name="a_norm_fwd",
        riders=[a_w_in[0].astype(BF16), a_w_r[0].reshape(nb * rows_r, bw).astype(BF16),
                a_w_i[0].reshape(nb * rows_r, bw).astype(BF16), small_f32])
    late_shards = [a_w_out[0].astype(BF16), w_kv.astype(BF16), b_w_in[0].astype(BF16), b_w_out[0].astype(BF16)]
    g_a_w_in = _whole_from_columns(a_w_in_cols, name="a_w_in_whole")
    g_w_r = w_r_rows.reshape(N_DEV, nb, rows_r, bw).transpose(1, 0, 2, 3).reshape(nb, bw, bw)
    g_w_i = w_i_rows.reshape(N_DEV, nb, rows_r, bw).transpose(1, 0, 2, 3).reshape(nb, bw, bw)
    cw_rows = a_conv_w.shape[1] * a_conv_w.shape[2] // LANES
    conv_w_full = small_all[:, :cw_rows, :].reshape(N_DEV, CONV_W, a_conv_w.shape[2])
    conv_w_full = conv_w_full.transpose(1, 0, 2).reshape(CONV_W, C)
    bn_rows = b_norm.shape[1] // LANES
    b_norm_full = small_all[:, cw_rows:cw_rows + bn_rows, :].reshape(1, D)
    kv_norm2, final_norm2 = kv_norm.reshape(1, D), final_norm.reshape(1, D)

    (loss_part, grad_x, dw_a_in, dw_a_out, dw_kv, dw_b_in, dw_b_out, dw_r, dw_i, d_a_norm, d_conv_w,
     d_conv_b, d_b_r, d_b_i, d_lambda, d_kv_norm, d_b_norm, d_final_norm, sums,
     others) = _forward_backward(
         xs, target, a_norm, g_a_w_in, conv_w_full, a_conv_b, g_w_r, g_w_i, a_b_r, a_b_i, a_lambda,
         kv_norm2, b_norm_full, final_norm2, late_shards=late_shards, h_a=h_a)

    small_parts = [d_a_norm, d_conv_w, d_conv_b, d_b_r, d_b_i, d_lambda, d_kv_norm, d_b_norm, d_final_norm]
    small_sizes = [p.size // LANES for p in small_parts]
    small = jnp.concatenate([_rows(p) for p in small_parts], axis=0)
    small_everyone = _small_gather(small)

    def shard2d(w):
        return w.reshape(-1, w.shape[-1])

    names_big = [(a_w_in, m_a_w_in, v_a_w_in), (a_w_out, m_a_w_out, v_a_w_out), (w_kv, m_w_kv, v_w_kv),
                 (b_w_in, m_b_w_in, v_b_w_in), (b_w_out, m_b_w_out, v_b_w_out),
                 (a_w_r, m_a_w_r, v_a_w_r), (a_w_i, m_a_w_i, v_a_w_i)]
    upd_big = []
    for i, (w, m, v) in enumerate(names_big):
        res = _adamw(shard2d(w), shard2d(m), shard2d(v), [sums[i][0], others[i]], name=f"adamw_{i}")
        upd_big.append([r.reshape(w.shape) for r in res])

    soffs = [0]
    for s in small_sizes:
        soffs.append(soffs[-1] + s)

    def small_piece(i):
        return small_everyone[:, soffs[i]:soffs[i + 1], :]

    cw_cols = a_conv_w.shape[2]
    conv_piece = small_piece(1).reshape(N_DEV, CONV_W, C)
    conv_piece = lax.dynamic_slice_in_dim(conv_piece, me * cw_cols, cw_cols, axis=2)
    conv_piece = conv_piece.reshape(N_DEV, CONV_W * cw_cols // LANES, LANES)
    bn_piece = lax.dynamic_slice_in_dim(small_piece(7), me * bn_rows, bn_rows, axis=1)
    small_g = jnp.concatenate([small_piece(0), conv_piece, small_piece(2), small_piece(3), small_piece(4),
                               small_piece(5), small_piece(6), bn_piece, small_piece(8)], axis=1)
    small_w = [(a_norm, m_a_norm, v_a_norm), (a_conv_w, m_a_conv_w, v_a_conv_w),
               (a_conv_b, m_a_conv_b, v_a_conv_b), (a_b_r, m_a_b_r, v_a_b_r), (a_b_i, m_a_b_i, v_a_b_i),
               (a_lambda, m_a_lambda, v_a_lambda), (kv_norm, m_kv_norm, v_kv_norm),
               (b_norm, m_b_norm, v_b_norm), (final_norm, m_final_norm, v_final_norm)]
    pack = lambda idx: jnp.concatenate([_rows(t[idx]) for t in small_w], axis=0)
    res_small = _adamw(pack(0), pack(1), pack(2), [small_g], name="adamw_small")
    woffs = [0]
    for t in small_w:
        woffs.append(woffs[-1] + t[0].size // LANES)
    upd_small = [[r[woffs[i]:woffs[i + 1]].reshape(small_w[i][0].shape) for r in res_small]
                 for i in range(len(small_w))]

    order = [("s", 0), ("b", 0), ("s", 1), ("s", 2), ("b", 5), ("s", 3), ("b", 6), ("s", 4), ("s", 5),
             ("b", 1), ("s", 6), ("b", 2), ("s", 7), ("b", 3), ("b", 4), ("s", 8)]
    per_weight = [(upd_big if kind == "b" else upd_small)[i] for kind, i in order]
    loss = lax.psum(loss_part[0, 0], ("x", "y", "c"))
    result = [loss, grad_x[None]]
    for field in range(4):
        result += [u[field] for u in per_weight]
    return tuple(result)
```

```python
import math

import jax
import jax.numpy as jnp
from jax import lax
from jax.experimental import pallas as pl
from jax.experimental.pallas import tpu as pltpu

F32 = jnp.float32
BF16 = jnp.bfloat16
MESH = pl.DeviceIdType.MESH

EPS = 1e-6
LOG2E = 1.4426950408889634
WEIGHT_FLOOR_LOG2 = -200.0
LRU_C = 8.0
CONV_W = 4
HEAD_DIM = 128
ADAM_LR = 0.001
ADAM_B1 = 0.9
ADAM_B2 = 0.999
ADAM_EPS = 1e-08
ADAM_WD = 0.01
ADAM_STEP = 10

N_DEV = 8
LANES = 128
SUBLANES = 8
VMEM_LIMIT = 56 * 1024 * 1024

ATT_KEY_BLOCK = 256
ATT_QUERY_BLOCK = 256
ATT_STEP_BLOCKS = 4
SCAN_BLOCK = 256
ROW_BLOCK = 256
MM_TOKEN_BLOCK = 1024
MM_WIDE_K = 2560
MM_WEIGHT_TILE = 1280
MM_CONTRACT_TOKENS = 2048
ANY = pl.BlockSpec(memory_space=pl.ANY)


def _pcall(body, **kw):
    return pl.pallas_call(body, **kw)


def _params(*sem):
    return pltpu.CompilerParams(dimension_semantics=sem, vmem_limit_bytes=VMEM_LIMIT)


def _pick(n, cap):
    if n <= cap:
        return n
    best = None
    for t in range(LANES, cap + 1, LANES):
        if n % t == 0:
            best = t
    assert best is not None, (n, cap)
    return best


def _token_block(T, K):
    tm = MM_TOKEN_BLOCK if K <= MM_WIDE_K else MM_TOKEN_BLOCK // 2
    return tm if T % tm == 0 else T


def _sigmoid(x):
    return 1.0 / (1.0 + jnp.exp(-x))


def _dot(a, b, ca, cb):
    return lax.dot_general(a, b, (((ca,), (cb,)), ((), ())), preferred_element_type=F32)


def _mm_nn(a, w, *, name, out_dtype, col_off=0, cols=None, res=None):
    T, K = a.shape
    K2, N = w.shape
    assert K == K2
    cols = N if cols is None else cols
    tm = _token_block(T, K)
    tn = _pick(cols, MM_WEIGHT_TILE)
    assert col_off % tn == 0
    off = col_off // tn
    has_res = res is not None

    def body(a_ref, b_ref, *rest):
        o_ref = rest[-1]
        acc = jnp.dot(a_ref[...].astype(BF16), b_ref[...], preferred_element_type=F32)
        if has_res:
            acc = acc + rest[0][...]
        o_ref[...] = acc.astype(out_dtype)

    in_specs = [pl.BlockSpec((tm, K), lambda j, i: (i, 0)),
                pl.BlockSpec((K, tn), lambda j, i: (0, off + j))]
    args = [a, w]
    if has_res:
        in_specs.append(pl.BlockSpec((tm, tn), lambda j, i: (i, j)))
        args.append(res)
    return _pcall(
        body, name=name, grid=(cols // tn, T // tm), in_specs=in_specs,
        out_specs=pl.BlockSpec((tm, tn), lambda j, i: (i, j)),
        out_shape=jax.ShapeDtypeStruct((T, cols), out_dtype),
        compiler_params=_params("parallel", "parallel"))(*args)


def _mm_nt(a, w, *, name, out_dtype=F32, rider=None):
    parts = a.shape[0] if a.ndim == 3 else 1
    T, kp = a.shape[-2:]
    N, K = w.shape
    assert K == parts * kp
    tm = _token_block(T, K)
    tn = _pick(N, MM_WEIGHT_TILE)
    nj, ni = N // tn, T // tm
    rider = rider or _Rider([], [], [], None)
    nri, nro = len(rider.inputs), len(rider.out_shapes)

    def body(a_ref, b_ref, *refs):
        o_ref = refs[nri]
        j, i = pl.program_id(0), pl.program_id(1)
        if nro:
            start, finish = rider.bind(refs[:nri], refs[nri + 1:nri + 1 + nro], refs[nri + 1 + nro:])
            pl.when((j == 0) & (i == 0))(start)
        if a.ndim == 3:
            acc = None
            for p in range(parts):
                term = _dot(a_ref[p], b_ref[:, p * kp:(p + 1) * kp], 1, 1)
                acc = term if acc is None else acc + term
        else:
            acc = _dot(a_ref[...].astype(BF16), b_ref[...], 1, 1)
        o_ref[...] = acc.astype(out_dtype)
        if nro:
            pl.when((j == nj - 1) & (i == ni - 1))(finish)

    a_spec = (pl.BlockSpec((parts, tm, kp), lambda j, i: (0, i, 0)) if a.ndim == 3
              else pl.BlockSpec((tm, K), lambda j, i: (i, 0)))
    sem = ("arbitrary", "arbitrary") if nro else ("parallel", "parallel")
    out = _pcall(
        body, name=name, grid=(nj, ni),
        in_specs=[a_spec, pl.BlockSpec((tn, K), lambda j, i: (j, 0))] + [ANY] * nri,
        out_specs=[pl.BlockSpec((tm, tn), lambda j, i: (i, j))] + [ANY] * nro,
        out_shape=[jax.ShapeDtypeStruct((T, N), out_dtype)] + rider.out_shapes,
        scratch_shapes=rider.scratch,
        compiler_params=_params(*sem))(a, w, *rider.inputs)
    return out if nro else out[0]


def _mm_tn(a, b, *, name, shards=1):
    T, Ko = a.shape
    parts = b.shape[0] if b.ndim == 3 else 1
    T2, n_part = b.shape[-2:]
    N = parts * n_part
    assert T == T2
    n = N // shards
    tt = min(T, MM_CONTRACT_TOKENS)
    tko = _pick(Ko, 1024)
    tn = _pick(n, 1024)
    per = n // tn
    assert n_part % tn == 0
    per_part = n_part // tn

    def body(a_ref, b_ref, o_ref):
        t = pl.program_id(2)
        p = _dot(a_ref[...].astype(BF16), b_ref[...].astype(BF16), 0, 0)

        @pl.when(t == 0)
        def _():
            o_ref[...] = p

        @pl.when(t > 0)
        def _():
            o_ref[...] += p

    if shards == 1:
        out_spec = pl.BlockSpec((tko, tn), lambda i, j, t: (i, j))
        out_shape = jax.ShapeDtypeStruct((Ko, N), F32)
    else:
        out_spec = pl.BlockSpec((None, tko, tn), lambda i, j, t: (j // per, i, j % per))
        out_shape = jax.ShapeDtypeStruct((shards, Ko, n), F32)
    b_spec = (pl.BlockSpec((None, tt, tn), lambda i, j, t: (j // per_part, t, j % per_part)) if b.ndim == 3
              else pl.BlockSpec((tt, tn), lambda i, j, t: (t, j)))
    return _pcall(
        body, name=name, grid=(Ko // tko, N // tn, T // tt),
        in_specs=[pl.BlockSpec((tt, tko), lambda i, j, t: (t, i)), b_spec],
        out_specs=out_spec, out_shape=out_shape,
        compiler_params=_params("parallel", "parallel", "arbitrary"))(a, b)


def _rms_fwd(x, gains, *, name, riders=()):
    T, D = x.shape
    tm = min(T, ROW_BLOCK)
    steps = T // tm
    n, nr = len(gains), len(riders)

    def body(x_ref, *refs):
        g_refs, rider_in = refs[:n], refs[n:n + nr]
        o_refs, rider_out = refs[n + nr:2 * n + nr], refs[2 * n + nr:2 * n + 2 * nr]
        if nr:
            gather = _Gather(rider_in, rider_out, *refs[2 * n + 2 * nr:])
            pl.when(pl.program_id(0) == 0)(gather.start)
        xv = x_ref[...]
        xh = xv * lax.rsqrt(jnp.mean(xv * xv, axis=-1, keepdims=True) + EPS)
        for g_ref, o_ref in zip(g_refs, o_refs):
            o_ref[...] = (xh * g_ref[...]).astype(BF16)
        if nr:
            pl.when(pl.program_id(0) == steps - 1)(gather.finish)

    row = pl.BlockSpec((tm, D), lambda i: (i, 0))
    vec = pl.BlockSpec((1, D), lambda i: (0, 0))
    return _pcall(
        body, name=name, grid=(steps,), in_specs=[row] + [vec] * n + [ANY] * nr,
        out_specs=[row] * n + [ANY] * nr,
        out_shape=[jax.ShapeDtypeStruct((T, D), BF16)] * n
                  + [jax.ShapeDtypeStruct((N_DEV,) + r.shape, r.dtype) for r in riders],
        scratch_shapes=_Gather.scratch(nr) if nr else [],
        compiler_params=_params("arbitrary" if nr else "parallel"))(x, *gains, *riders)


def _rms_bwd(x, dres, dhs, gains, *, name, rider=None):
    T, D = x.shape
    tm = min(T, ROW_BLOCK)
    steps = T // tm
    n = len(gains)
    rider = rider or _Rider([], [], [], None)
    nri, nro = len(rider.inputs), len(rider.out_shapes)

    def body(x_ref, dres_ref, *refs):
        dh_refs, g_refs = refs[:n], refs[n:2 * n]
        refs = refs[2 * n:]
        rider_in, refs = refs[:nri], refs[nri:]
        dx_ref, dg_refs = refs[0], refs[1:1 + n]
        rider_out, sems = refs[1 + n:1 + n + nro], refs[1 + n + nro:]
        i = pl.program_id(0)
        if nro:
            start, finish = rider.bind(rider_in, rider_out, sems)
            pl.when(i == 0)(start)
        xv = x_ref[...]
        r = lax.rsqrt(jnp.mean(xv * xv, axis=-1, keepdims=True) + EPS)
        xh = xv * r
        dxh = jnp.zeros_like(xv)
        for dh_ref, g_ref, dg_ref in zip(dh_refs, g_refs, dg_refs):
            dh = dh_ref[...]
            part = jnp.sum(dh * xh, axis=0, keepdims=True)

            @pl.when(i == 0)
            def _():
                dg_ref[...] = part

            @pl.when(i > 0)
            def _():
                dg_ref[...] += part

            dxh = dxh + dh * g_ref[...]
        dx_ref[...] = dres_ref[...] + r * (dxh - xh * jnp.mean(dxh * xh, axis=-1, keepdims=True))
        if nro:
            pl.when(i == steps - 1)(finish)

    row = pl.BlockSpec((tm, D), lambda i: (i, 0))
    vec = pl.BlockSpec((1, D), lambda i: (0, 0))
    return _pcall(
        body, name=name, grid=(steps,), in_specs=[row, row] + [row] * n + [vec] * n + [ANY] * nri,
        out_specs=[row] + [vec] * n + [ANY] * nro,
        out_shape=[jax.ShapeDtypeStruct((T, D), F32)] + [jax.ShapeDtypeStruct((1, D), F32)] * n
                  + rider.out_shapes,
        scratch_shapes=rider.scratch,
        compiler_params=_params("arbitrary"))(x, dres, *dhs, *gains, *rider.inputs)


def _final_loss(x, target, gain, *, name):
    T, D = x.shape
    tm = min(T, ROW_BLOCK)

    def body(x_ref, t_ref, g_ref, dx_ref, dg_ref, loss_ref):
        i = pl.program_id(0)
        xv = x_ref[...]
        g = g_ref[...]
        r = lax.rsqrt(jnp.mean(xv * xv, axis=-1, keepdims=True) + EPS)
        xh = xv * r
        err = xh * g - t_ref[...]
        part_loss = 0.5 * jnp.sum(jnp.mean(err * err, axis=-1, keepdims=True), axis=0, keepdims=True)
        dy = err * (1.0 / D)
        part_g = jnp.sum(dy * xh, axis=0, keepdims=True)

        @pl.when(i == 0)
        def _():
            dg_ref[...] = part_g
            loss_ref[...] = jnp.broadcast_to(part_loss, loss_ref.shape)

        @pl.when(i > 0)
        def _():
            dg_ref[...] += part_g
            loss_ref[...] += jnp.broadcast_to(part_loss, loss_ref.shape)

        dxh = dy * g
        dx_ref[...] = r * (dxh - xh * jnp.mean(dxh * xh, axis=-1, keepdims=True))

    row = pl.BlockSpec((tm, D), lambda i: (i, 0))
    vec = pl.BlockSpec((1, D), lambda i: (0, 0))
    return _pcall(
        body, name=name, grid=(T // tm,), in_specs=[row, row, vec],
        out_specs=[row, vec, pl.BlockSpec((1, LANES), lambda i: (0, 0))],
        out_shape=[jax.ShapeDtypeStruct((T, D), F32), jax.ShapeDtypeStruct((1, D), F32),
                   jax.ShapeDtypeStruct((1, LANES), F32)],
        compiler_params=_params("arbitrary"))(x, target, gain)


def _shift_down(x, prev_tail, j, row):
    tb = x.shape[0]
    prev = jnp.tile(prev_tail, (tb // SUBLANES, 1))
    return jnp.where(row >= j, pltpu.roll(x, j, 0), pltpu.roll(prev, j, 0))


def _shift_up(x, next_head, j, row):
    tb = x.shape[0]
    nxt = jnp.tile(next_head, (tb // SUBLANES, 1))
    return jnp.where(row < tb - j, pltpu.roll(x, tb - j, 0), pltpu.roll(nxt, tb - j, 0))


def _lru_gates(xb, wr, wi, br, bi, lam):
    xbb = xb.astype(BF16)
    r = _sigmoid(jnp.dot(xbb, wr, preferred_element_type=F32) + br)
    i = _sigmoid(jnp.dot(xbb, wi, preferred_element_type=F32) + bi)
    sp = jnp.maximum(-lam, 0.0) + jnp.log1p(jnp.exp(-jnp.abs(lam)))
    log_a = (-LRU_C) * r * sp
    a = jnp.exp(log_a)
    a2 = a * a
    mult = jnp.sqrt(jnp.maximum(-jnp.tanh(log_a) * (1.0 + a2), 0.0))
    return xbb, r, i, sp, a, a2, mult


def _scan_rows(coef, val, edge, reverse):
    tb, C = coef.shape
    a, b = coef, val
    row = lax.broadcasted_iota(jnp.int32, (tb, C), 0)
    s = 1
    while s < tb:
        m = (row < tb - s) if reverse else (row >= s)
        shift = tb - s if reverse else s
        b = jnp.where(m, a * pltpu.roll(b, shift, 0) + b, b)
        a = jnp.where(m, a * pltpu.roll(a, shift, 0), a)
        s *= 2
    return b + a * edge


def _acore_fwd(proj, conv_w, conv_b, w_r, w_i, b_r, b_i, lam, *, name, riders=()):
    T, C2 = proj.shape
    C = C2 // 2
    nb, bw, _ = w_r.shape
    tb = min(T, SCAN_BLOCK)
    nt = T // tb
    nr = len(riders)

    def body(xp_ref, gate_ref, cw_ref, cb_ref, wr_ref, wi_ref, br_ref, bi_ref, lam_ref, *refs):
        rider_in, refs = refs[:nr], refs[nr:]
        xb_ref, h_ref, yg_ref = refs[:3]
        rider_out, refs = refs[3:3 + nr], refs[3 + nr:]
        tail_ref, hlast_ref = refs[:2]
        t = pl.program_id(1)
        if nr:
            gather = _Gather(rider_in, rider_out, *refs[2:])
            pl.when((pl.program_id(0) == 0) & (t == 0))(gather.start)

        @pl.when(t == 0)
        def _():
            tail_ref[...] = jnp.zeros_like(tail_ref)
            hlast_ref[...] = jnp.zeros_like(hlast_ref)

        row = lax.broadcasted_iota(jnp.int32, (tb, bw), 0)
        xp = xp_ref[...]
        tail = tail_ref[...]
        xb = cb_ref[...] + cw_ref[CONV_W - 1:CONV_W, :] * xp
        for j in range(1, CONV_W):
            xb = xb + cw_ref[CONV_W - 1 - j:CONV_W - j, :] * _shift_down(xp, tail, j, row)
        tail_ref[...] = xp[tb - SUBLANES:, :]
        xb_ref[...] = xb

        _, r, i, sp, a, a2, mult = _lru_gates(xb, wr_ref[...], wi_ref[...], br_ref[...], bi_ref[...],
                                              lam_ref[...])
        h = _scan_rows(a, mult * (i * xb), hlast_ref[SUBLANES - 1:SUBLANES, :], False)
        hlast_ref[...] = h[tb - SUBLANES:, :]
        h_ref[...] = h
        gate = gate_ref[...]
        yg_ref[...] = (h * (gate * _sigmoid(gate))).astype(BF16)
        if nr:
            pl.when((pl.program_id(0) == nb - 1) & (t == nt - 1))(gather.finish)

    blk = lambda off: pl.BlockSpec((tb, bw), lambda n, t: (t, off + n))
    vec = pl.BlockSpec((1, bw), lambda n, t: (0, n))
    wspec = pl.BlockSpec((None, bw, bw), lambda n, t: (n, 0, 0))
    return _pcall(
        body, name=name, grid=(nb, nt),
        in_specs=[blk(0), blk(nb), pl.BlockSpec((CONV_W, bw), lambda n, t: (0, n)), vec, wspec, wspec,
                  vec, vec, vec] + [ANY] * nr,
        out_specs=[blk(0), blk(0), blk(0)] + [ANY] * nr,
        out_shape=[jax.ShapeDtypeStruct((T, C), F32), jax.ShapeDtypeStruct((T, C), F32),
                   jax.ShapeDtypeStruct((T, C), BF16)]
                  + [jax.ShapeDtypeStruct((N_DEV,) + r.shape, r.dtype) for r in riders],
        scratch_shapes=[pltpu.VMEM((SUBLANES, bw), F32), pltpu.VMEM((SUBLANES, bw), F32)]
                       + (_Gather.scratch(nr) if nr else []),
        compiler_params=_params("arbitrary" if nr else "parallel", "arbitrary"))(
            proj, proj, conv_w, conv_b, w_r, w_i, b_r, b_i, lam, *riders)


def _acore_bwd(dyg, proj, xb_all, h_all, conv_w, w_r, w_i, b_r, b_i, lam, *, name, rider=None):
    T, C2 = proj.shape
    C = C2 // 2
    nb, bw, _ = w_r.shape
    tb = min(T, SCAN_BLOCK)
    nt = T // tb
    per8 = tb // SUBLANES
    rider = rider or _Rider([], [], [], None)
    nri, nro = len(rider.inputs), len(rider.out_shapes)

    def body(dyg_ref, xp_ref, gate_ref, xb_ref, h_ref, xp_prev_ref, h_prev_ref, cw_ref,
             wr_ref, wi_ref, br_ref, bi_ref, lam_ref, *refs):
        rider_in, refs = refs[:nri], refs[nri:]
        dproj_ref, dcw_ref, dcb_ref, dbr_ref, dbi_ref, dlam_ref, dwr_ref, dwi_ref = refs[:8]
        rider_out, refs = refs[8:8 + nro], refs[8 + nro:]
        gh_next_ref, a_next_ref, dxb_next_ref = refs[:3]
        step = pl.program_id(1)
        first_block = step == nt - 1
        if nro:
            start, finish = rider.bind(rider_in, rider_out, refs[3:])
            pl.when((pl.program_id(0) == 0) & (step == 0))(start)

        @pl.when(step == 0)
        def _():
            gh_next_ref[...] = jnp.zeros_like(gh_next_ref)
            a_next_ref[...] = jnp.zeros_like(a_next_ref)
            dxb_next_ref[...] = jnp.zeros_like(dxb_next_ref)

        row = lax.broadcasted_iota(jnp.int32, (tb, bw), 0)
        keep = jnp.where(first_block, 0.0, 1.0)
        h_prev = h_prev_ref[...] * keep
        xp_prev = xp_prev_ref[...] * keep
        xp, gate, xb, h, dyg_v = xp_ref[...], gate_ref[...], xb_ref[...], h_ref[...], dyg_ref[...]
        lam_v = lam_ref[...]
        wr, wi = wr_ref[...], wi_ref[...]

        sg = _sigmoid(gate)
        dh = dyg_v * (gate * sg)
        dproj_ref[1] = (dyg_v * h * (sg * (1.0 + gate * (1.0 - sg)))).astype(BF16)

        xbb, r, i, sp, a, a2, mult = _lru_gates(xb, wr, wi, br_ref[...], bi_ref[...], lam_v)

        gh = _scan_rows(_shift_up(a, a_next_ref[...], 1, row), dh, gh_next_ref[0:1, :], True)
        gh_next_ref[...] = gh[0:SUBLANES, :]
        a_next_ref[...] = a[0:SUBLANES, :]

        da = gh * _shift_down(h, h_prev, 1, row)
        dmult = gh * (i * xb)
        di = gh * mult * xb
        dxb = gh * mult * i
        dla = da * a - dmult * jnp.where(mult > 0.0, a2 / mult, 0.0)
        dr = dla * ((-LRU_C) * sp)
        dsp = jnp.sum(dla * ((-LRU_C) * r), axis=0, keepdims=True)
        dlam_part = dsp * (-_sigmoid(-lam_v))
        dpr = dr * r * (1.0 - r)
        dpi = di * i * (1.0 - i)
        dbr_part = jnp.sum(dpr, axis=0, keepdims=True)
        dbi_part = jnp.sum(dpi, axis=0, keepdims=True)
        dprb, dpib = dpr.astype(BF16), dpi.astype(BF16)
        dwr_part = _dot(xbb, dprb, 0, 0)
        dwi_part = _dot(xbb, dpib, 0, 0)
        dxb = dxb + _dot(dprb, wr, 1, 1) + _dot(dpib, wi, 1, 1)

        dxb_next = dxb_next_ref[...]
        dxp = cw_ref[CONV_W - 1:CONV_W, :] * dxb
        for j in range(1, CONV_W):
            dxp = dxp + cw_ref[CONV_W - 1 - j:CONV_W - j, :] * _shift_up(dxb, dxb_next, j, row)
        dxb_next_ref[...] = dxb[0:SUBLANES, :]
        dproj_ref[0] = dxp.astype(BF16)
        dcb_part = jnp.sum(dxb, axis=0, keepdims=True)
        dcw_rows = []
        for k in range(CONV_W):
            j = CONV_W - 1 - k
            sh = xp if j == 0 else _shift_down(xp, xp_prev, j, row)
            dcw_rows.append(jnp.sum(dxb * sh, axis=0, keepdims=True))

        @pl.when(step == 0)
        def _():
            for k in range(CONV_W):
                dcw_ref[k:k + 1, :] = dcw_rows[k]
            dcb_ref[...] = dcb_part
            dbr_ref[...] = dbr_part
            dbi_ref[...] = dbi_part
            dlam_ref[...] = dlam_part
            dwr_ref[...] = dwr_part
            dwi_ref[...] = dwi_part

        @pl.when(step > 0)
        def _():
            for k in range(CONV_W):
                dcw_ref[k:k + 1, :] += dcw_rows[k]
            dcb_ref[...] += dcb_part
            dbr_ref[...] += dbr_part
            dbi_ref[...] += dbi_part
            dlam_ref[...] += dlam_part
            dwr_ref[...] += dwr_part
            dwi_ref[...] += dwi_part

        if nro:
            pl.when((pl.program_id(0) == nb - 1) & (step == nt - 1))(finish)

    rev = lambda s: nt - 1 - s
    blk = lambda off: pl.BlockSpec((tb, bw), lambda n, s: (rev(s), off + n))
    prev8 = lambda off: pl.BlockSpec(
        (SUBLANES, bw), lambda n, s: (jnp.maximum(rev(s) * per8 - 1, 0), off + n))
    vec = pl.BlockSpec((1, bw), lambda n, s: (0, n))
    wspec = pl.BlockSpec((None, bw, bw), lambda n, s: (n, 0, 0))
    cwspec = pl.BlockSpec((CONV_W, bw), lambda n, s: (0, n))
    vshape = jax.ShapeDtypeStruct((1, C), F32)
    wshape = jax.ShapeDtypeStruct((nb, bw, bw), F32)
    return _pcall(
        body, name=name, grid=(nb, nt),
        in_specs=[blk(0), blk(0), blk(nb), blk(0), blk(0), prev8(0), prev8(0), cwspec,
                  wspec, wspec, vec, vec, vec] + [ANY] * nri,
        out_specs=[pl.BlockSpec((2, tb, bw), lambda n, s: (0, rev(s), n)), cwspec, vec, vec, vec, vec,
                   wspec, wspec] + [ANY] * nro,
        out_shape=[jax.ShapeDtypeStruct((2, T, C), BF16),
                   jax.ShapeDtypeStruct((CONV_W, C), F32), vshape, vshape, vshape, vshape,
                   wshape, wshape] + rider.out_shapes,
        scratch_shapes=[pltpu.VMEM((SUBLANES, bw), F32)] * 3 + rider.scratch,
        compiler_params=_params("arbitrary" if nro else "parallel", "arbitrary"))(
            dyg, proj, proj, xb_all, h_all, proj, h_all, conv_w, w_r, w_i, b_r, b_i, lam, *rider.inputs)


def _later_sum(lk, tri):
    return jnp.dot(lk.astype(BF16), tri, preferred_element_type=F32)


def _log2_sigmoids(y):
    t = jnp.log(1.0 + jnp.exp2(-jnp.abs(y))) * LOG2E
    ls = jnp.minimum(y, 0.0) - t
    return ls, ls - y


def _attn_blocks(T):
    bk = min(T, ATT_KEY_BLOCK)
    bq = min(T, ATT_QUERY_BLOCK)
    return bq, bk, bq // bk


def _attn_step_blocks(T, bq):
    return ATT_STEP_BLOCKS if (T // bq) % ATT_STEP_BLOCKS == 0 else 1


def _attn_fwd(q, kv, gate, *, name):
    T, HD = q.shape
    H = HD // HEAD_DIM
    bq, bk, per = _attn_blocks(T)
    scale = 1.0 / math.sqrt(HEAD_DIM)

    nsub = _attn_step_blocks(T, bq)

    def body(q_ref, k_ref, v_ref, g_ref, o_ref, og_ref, lt_ref, w_ref):
        for sub in range(nsub):
            rows = pl.ds(sub * bq, bq)
            one_block(pl.program_id(1) * nsub + sub, q_ref.at[rows], k_ref, v_ref, g_ref.at[rows],
                      o_ref.at[rows], og_ref.at[rows], lt_ref.at[rows], w_ref)

    def one_block(i, q_ref, k_ref, v_ref, g_ref, o_ref, og_ref, lt_ref, w_ref):
        qv = q_ref[...]
        tr = lax.broadcasted_iota(jnp.int32, (bk, bk), 0)
        tc = lax.broadcasted_iota(jnp.int32, (bk, bk), 1)
        tri = (tr > tc).astype(BF16)
        ahead = (lax.broadcasted_iota(jnp.int32, (bq, bk), 0)
                 - lax.broadcasted_iota(jnp.int32, (bq, bk), 1))

        def starts_of(top):
            return [pl.multiple_of((top - d) * bk, bk) for d in range(per)]

        def scores(top):
            return [_dot(qv, k_ref[pl.ds(ks, bk), :], 1, 1) for ks in starts_of(top)]

        def weights(top, zs, c, mask):
            lss, sums, css, causals = [], [], [], []
            for ks, z in zip(starts_of(top), zs):
                ls, lk = _log2_sigmoids(z * (scale * LOG2E))
                if mask:
                    causals.append(ahead > ks - i * bq)
                    lk = jnp.where(causals[-1], lk, 0.0)
                lss.append(ls)
                sums.append(jnp.sum(lk, axis=1, keepdims=True))
                css.append(_later_sum(lk, tri))
            for d in range(per):
                w = jnp.exp2(lss[d] + (css[d] + c))
                if mask:
                    w = jnp.where(causals[d], w, 0.0)
                w_ref[d] = w.astype(BF16)
                c = c + sums[d]
            return c

        def values(top, acc):
            for d, ks in enumerate(starts_of(top)):
                acc = acc + jnp.dot(w_ref[d], v_ref[pl.ds(ks, bk), :], preferred_element_type=F32)
            return acc

        def more(state):
            gg, _, _, largest = state
            return (gg <= i) & (largest > WEIGHT_FLOOR_LOG2)

        def step(state):
            gg, acc, c, _ = state
            top = (i - gg) * per + per - 1
            zs = scores(top)
            acc = values(top + per, acc)
            c = weights(top, zs, c, False)
            return gg + 1, acc, c, jnp.max(c)

        diag_top = i * per + per - 1
        c = weights(diag_top, scores(diag_top), jnp.zeros((bq, 1), F32), True)
        gg, acc, c, _ = lax.while_loop(more, step, (1, jnp.zeros((bq, HEAD_DIM), F32), c, jnp.max(c)))
        acc = values((i - gg + 1) * per + per - 1, acc)
        o_ref[...] = acc
        g = g_ref[...]
        og_ref[...] = (acc * (g * _sigmoid(g))).astype(BF16)
        lane = lax.broadcasted_iota(jnp.int32, (bq, HEAD_DIM), 1)
        lt_ref[...] = jnp.where(lane == 1, (i - gg + 1).astype(F32), jnp.broadcast_to(c, (bq, HEAD_DIM)))

    qspec = pl.BlockSpec((nsub * bq, HEAD_DIM), lambda h, i: (i, h))
    return _pcall(
        body, name=name, grid=(H, T // (nsub * bq)),
        in_specs=[qspec, pl.BlockSpec((T, HEAD_DIM), lambda h, i: (0, h)),
                  pl.BlockSpec((T, HEAD_DIM), lambda h, i: (0, H + h)), qspec],
        out_specs=[qspec, qspec, qspec],
        out_shape=[jax.ShapeDtypeStruct((T, HD), F32), jax.ShapeDtypeStruct((T, HD), BF16),
                   jax.ShapeDtypeStruct((T, HD), F32)],
        scratch_shapes=[pltpu.VMEM((per, bq, bk), BF16)],
        compiler_params=_params("parallel", "arbitrary"))(q, kv, kv, gate)


def _attn_bwd(q, kv, gate, o, ltot, dog, *, name):
    T, HD = q.shape
    H = HD // HEAD_DIM
    bq, bk, per = _attn_blocks(T)
    nq = T // bq
    scale = 1.0 / math.sqrt(HEAD_DIM)

    nsub = _attn_step_blocks(T, bq)

    def body(q_ref, k_ref, v_ref, g_ref, o_ref, lt_ref, dog_ref,
             dqg_ref, dkv_ref, dk_acc, dv_acc, dz_ref, w_ref):
        for sub in range(nsub):
            rows = pl.ds(sub * bq, bq)
            one_block(pl.program_id(1) * nsub + sub, q_ref.at[rows], k_ref, v_ref, g_ref.at[rows],
                      o_ref.at[rows], lt_ref.at[rows], dog_ref.at[rows], dqg_ref.at[:, rows], dkv_ref,
                      dk_acc, dv_acc, dz_ref, w_ref)

    def one_block(i, q_ref, k_ref, v_ref, g_ref, o_ref, lt_ref, dog_ref,
                  dqg_ref, dkv_ref, dk_acc, dv_acc, dz_ref, w_ref):
        @pl.when(i == 0)
        def _():
            dk_acc[...] = jnp.zeros_like(dk_acc)
            dv_acc[...] = jnp.zeros_like(dv_acc)

        qv = q_ref[...]
        g, ov, dogv = g_ref[...], o_ref[...], dog_ref[...]
        sg = _sigmoid(g)
        do = dogv * (g * sg)
        dqg_ref[1] = (dogv * ov * (sg * (1.0 + g * (1.0 - sg)))).astype(BF16)
        dob = do.astype(BF16)
        ltot_v = lt_ref[:, 0:1]
        tr = lax.broadcasted_iota(jnp.int32, (bk, bk), 0)
        tc = lax.broadcasted_iota(jnp.int32, (bk, bk), 1)
        tri_later = (tr > tc).astype(BF16)
        tri_excl = (tr < tc).astype(BF16)
        ahead = (lax.broadcasted_iota(jnp.int32, (bq, bk), 0)
                 - lax.broadcasted_iota(jnp.int32, (bq, bk), 1))

        def starts_of(first):
            return [pl.multiple_of((first + d) * bk, bk) for d in range(per)]

        def scores(first):
            return ([_dot(qv, k_ref[pl.ds(ks, bk), :], 1, 1) for ks in starts_of(first)],
                    [_dot(dob, v_ref[pl.ds(ks, bk), :], 1, 1) for ks in starts_of(first)])

        def front(first, zs, dws, p_lk, p_g, mask):
            lss, css, causals = [], [], []
            for ks, z in zip(starts_of(first), zs):
                ls, lk = _log2_sigmoids(z * (scale * LOG2E))
                if mask:
                    causals.append(ahead > ks - i * bq)
                    lk = jnp.where(causals[-1], lk, 0.0)
                lss.append(ls)
                p_lk = p_lk + jnp.sum(lk, axis=1, keepdims=True)
                css.append((ltot_v - p_lk) + _later_sum(lk, tri_later))
            gms, befores = [], []
            for d in range(per):
                w = jnp.exp2(lss[d] + css[d])
                if mask:
                    w = jnp.where(causals[d], w, 0.0)
                gm = dws[d] * w
                gms.append(gm)
                w_ref[d] = w.astype(BF16)
                befores.append(jnp.dot(gm.astype(BF16), tri_excl, preferred_element_type=F32) + p_g)
                p_g = p_g + jnp.sum(gm, axis=1, keepdims=True)
            for d in range(per):
                dz = gms[d] - jnp.exp2(lss[d]) * (gms[d] + befores[d])
                if mask:
                    dz = jnp.where(causals[d], dz, 0.0)
                dz_ref[d] = (dz * scale).astype(BF16)
            return p_lk, p_g

        def back(first, dq):
            for d, ks in enumerate(starts_of(first)):
                dzb = dz_ref[d]
                dq = dq + jnp.dot(dzb, k_ref[pl.ds(ks, bk), :], preferred_element_type=F32)
                dk_acc[pl.ds(ks, bk), :] += _dot(dzb, qv, 0, 0)
                dv_acc[pl.ds(ks, bk), :] += _dot(w_ref[d], dob, 0, 0)
            return dq

        def step(mask):
            def trip(g, state):
                dq, p_lk, p_g = state
                zs, dws = scores(g * per)
                dq = back((g - 1) * per, dq)
                return (dq,) + front(g * per, zs, dws, p_lk, p_g, mask)
            return trip

        g0 = jnp.max(lt_ref[0:1, 1:2]).astype(jnp.int32)
        zero = jnp.zeros((bq, 1), F32)
        state = (jnp.zeros((bq, HEAD_DIM), F32),) + front(g0 * per, *scores(g0 * per), zero, zero, True)
        state = lax.fori_loop(g0 + 1, i, step(False), state)
        state = lax.fori_loop(jnp.maximum(i, g0 + 1), i + 1, step(True), state)
        dqg_ref[0] = back(i * per, state[0]).astype(BF16)

        @pl.when(i == nq - 1)
        def _():
            dkv_ref[0] = dk_acc[...].astype(BF16)
            dkv_ref[1] = dv_acc[...].astype(BF16)

    qspec = pl.BlockSpec((nsub * bq, HEAD_DIM), lambda h, i: (i, h))
    kspec = pl.BlockSpec((T, HEAD_DIM), lambda h, i: (0, h))
    return _pcall(
        body, name=name, grid=(H, nq // nsub),
        in_specs=[qspec, kspec, pl.BlockSpec((T, HEAD_DIM), lambda h, i: (0, H + h)),
                  qspec, qspec, qspec, qspec],
        out_specs=[pl.BlockSpec((2, nsub * bq, HEAD_DIM), lambda h, i: (0, i, h)),
                   pl.BlockSpec((2, T, HEAD_DIM), lambda h, i: (0, 0, h))],
        out_shape=[jax.ShapeDtypeStruct((2, T, HD), BF16)] * 2,
        scratch_shapes=[pltpu.VMEM((T, HEAD_DIM), F32)] * 2 + [pltpu.VMEM((per, bq, bk), BF16)] * 2,
        compiler_params=_params("parallel", "arbitrary"))(q, kv, kv, gate, o, ltot, dog)


def _position():
    return lax.axis_index("x"), lax.axis_index("y"), lax.axis_index("c")


def _chip_of(k, x, y):
    return (1 - x if k & 1 else x), (1 - y if k & 2 else y)


class _Gather:
    @staticmethod
    def scratch(n):
        return [pltpu.SemaphoreType.DMA((n, 7)), pltpu.SemaphoreType.DMA((n, 7)),
                pltpu.SemaphoreType.DMA((n,))]

    def __init__(self, ins, outs, send_sems, recv_sems, local_sems):
        self.ins, self.outs, self.n = ins, outs, len(ins)
        self.send_sems, self.recv_sems, self.local_sems = send_sems, recv_sems, local_sems
        x, y, c = _position()
        self.me, self.sibling = (x, y, c), (x, y, 1 - c)
        self.chips = [_chip_of(k, x, y) for k in (1, 2, 3)]

    def copy(self, a, k, block, to, src=None):
        slot = self.outs[a].at[4 * block[0] + 2 * block[1] + block[2]]
        return pltpu.make_async_remote_copy(
            src_ref=slot if src is None else src, dst_ref=slot,
            send_sem=self.send_sems.at[a, k], recv_sem=self.recv_sems.at[a, k],
            device_id=to, device_id_type=MESH)

    def own_copies(self):
        x, y, c = self.me
        mine = [pltpu.make_async_copy(self.ins[a], self.outs[a].at[4 * x + 2 * y + c], self.local_sems.at[a])
                for a in range(self.n)]
        first = []
        for a in range(self.n):
            first.append(self.copy(a, 0, self.me, self.sibling, src=self.ins[a]))
            first += [self.copy(a, 1 + j, self.me, (*chip, c), src=self.ins[a])
                      for j, chip in enumerate(self.chips)]
        return mine, first

    def start(self):
        mine, first = self.own_copies()
        for cp in mine + first:
            cp.start()

    def finish(self):
        c = self.me[2]
        mine, first = self.own_copies()
        passed = []
        for j, chip in enumerate(self.chips):
            for a in range(self.n):
                self.copy(a, 1 + j, (*chip, c), self.me).wait_recv()
                fwd = self.copy(a, 4 + j, (*chip, c), self.sibling)
                fwd.start()
                passed.append(fwd)
        for a in range(self.n):
            self.copy(a, 0, self.sibling, self.me).wait_recv()
            for j, chip in enumerate(self.chips):
                self.copy(a, 4 + j, (*chip, 1 - c), self.me).wait_recv()
        for cp in first + passed:
            cp.wait_send()
        for cp in mine:
            cp.wait()


class _Rider:
    def __init__(self, inputs, out_shapes, scratch, copies):
        self.inputs, self.out_shapes, self.scratch, self.copies = inputs, out_shapes, scratch, copies

    def bind(self, ins, outs, sems):
        def start():
            for cp in self.copies(ins, outs, sems):
                cp.start()

        def finish():
            cps = self.copies(ins, outs, sems)
            for cp in cps:
                cp.wait_send()
            for cp in cps:
                cp.wait_recv()

        return start, finish


def _sibling_rider(grads):
    n = len(grads)

    def copies(ins, outs, sems):
        x, y, c = _position()
        return [pltpu.make_async_remote_copy(
            src_ref=ins[a].at[2 * chip + (1 - c)], dst_ref=outs[a].at[chip],
            send_sem=sems[0].at[a, chip], recv_sem=sems[1].at[a, chip],
            device_id=(x, y, 1 - c), device_id_type=MESH) for a in range(n) for chip in range(4)]

    return _Rider(list(grads), [jax.ShapeDtypeStruct((4,) + g.shape[1:], g.dtype) for g in grads],
                  [pltpu.SemaphoreType.DMA((n, 4)), pltpu.SemaphoreType.DMA((n, 4))], copies)


def _chips_rider(parts):
    n = len(parts)

    def copies(ins, outs, sems):
        x, y, c = _position()
        cps = []
        for a in range(n):
            for k in range(3):
                cx, cy = _chip_of(k + 1, x, y)
                cps.append(pltpu.make_async_remote_copy(
                    src_ref=ins[a].at[2 * cx + cy], dst_ref=outs[a].at[k],
                    send_sem=sems[0].at[a, k], recv_sem=sems[1].at[a, k],
                    device_id=(cx, cy, c), device_id_type=MESH))
        return cps

    return _Rider(list(parts), [jax.ShapeDtypeStruct((3,) + p.shape[1:], p.dtype) for p in parts],
                  [pltpu.SemaphoreType.DMA((n, 3)), pltpu.SemaphoreType.DMA((n, 3))], copies)


def _small_gather(small):
    def body(small_ref, small_all, send_sems, recv_sems, local_sem):
        x, y, c = _position()
        me = 4 * x + 2 * y + c
        peers = [(x ^ (m >> 2), y ^ ((m >> 1) & 1), c ^ (m & 1)) for m in range(1, N_DEV)]
        sends = [pltpu.make_async_remote_copy(
            src_ref=small_ref, dst_ref=small_all.at[me], send_sem=send_sems.at[m], recv_sem=recv_sems.at[m],
            device_id=peer, device_id_type=MESH) for m, peer in enumerate(peers)]
        own = pltpu.make_async_copy(small_ref, small_all.at[me], local_sem)
        for cp in sends + [own]:
            cp.start()
        for cp in sends:
            cp.wait_send()
        for m, (px, py, pc) in enumerate(peers):
            pltpu.make_async_remote_copy(
                src_ref=small_ref, dst_ref=small_all.at[4 * px + 2 * py + pc],
                send_sem=send_sems.at[m], recv_sem=recv_sems.at[m],
                device_id=(px, py, pc), device_id_type=MESH).wait_recv()
        own.wait()

    return _pcall(
        body, name="small_gather", in_specs=[ANY], out_specs=ANY,
        out_shape=jax.ShapeDtypeStruct((N_DEV,) + small.shape, small.dtype),
        scratch_shapes=[pltpu.SemaphoreType.DMA((7,)), pltpu.SemaphoreType.DMA((7,)),
                        pltpu.SemaphoreType.DMA])(small)


def _pair_sum(grad, got, *, name):
    _, R, C = got.shape
    tr = _pick8(R, max(2 * SUBLANES, (1 << 17) // C))

    def body(g_ref, b_ref, own_ref, ob_ref):
        north = lax.axis_index("c") == 1
        x1, y1 = lax.axis_index("x") == 1, lax.axis_index("y") == 1
        sums = []
        for chip in range(4):
            sums.append(jnp.where(north, g_ref[chip, 1], g_ref[chip, 0]) + b_ref[chip])
            ob_ref[chip] = sums[-1].astype(BF16)
        own_ref[...] = jnp.where(x1, jnp.where(y1, sums[3], sums[2]), jnp.where(y1, sums[1], sums[0]))

    spec = pl.BlockSpec((4, tr, C), lambda i: (0, i, 0))
    return _pcall(
        body, name=name, grid=(R // tr,),
        in_specs=[pl.BlockSpec((4, 2, tr, C), lambda i: (0, 0, i, 0)), spec],
        out_specs=[pl.BlockSpec((tr, C), lambda i: (i, 0)), spec],
        out_shape=[jax.ShapeDtypeStruct((R, C), F32), jax.ShapeDtypeStruct((4, R, C), BF16)],
        compiler_params=_params("parallel"))(grad.reshape(4, 2, R, C), got)


def _pick8(n, cap):
    if n <= cap:
        return n
    best = None
    for t in range(SUBLANES, cap + 1, SUBLANES):
        if n % t == 0:
            best = t
    assert best is not None, (n, cap)
    return best


def _adamw(w, m, v, parts, *, name):
    R, C = w.shape
    tr = _pick8(R, max(SUBLANES, (1 << 17) // C))
    c1 = 1.0 - ADAM_B1 ** ADAM_STEP
    c2 = 1.0 - ADAM_B2 ** ADAM_STEP
    np_ = len(parts)

    def body(w_ref, m_ref, v_ref, *refs):
        p_refs = refs[:np_]
        g_ref, d_ref, nm_ref, nv_ref = refs[np_:]
        g = None
        for p_ref in p_refs:
            terms = [p_ref[...]] if len(p_ref.shape) == 2 else [p_ref[k] for k in range(p_ref.shape[0])]
            for t in terms:
                g = t.astype(F32) if g is None else g + t.astype(F32)
        mn = ADAM_B1 * m_ref[...] + (1.0 - ADAM_B1) * g
        vn = ADAM_B2 * v_ref[...] + (1.0 - ADAM_B2) * (g * g)
        d_ref[...] = -ADAM_LR * ((mn / c1) / (jnp.sqrt(vn / c2) + ADAM_EPS) + ADAM_WD * w_ref[...])
        g_ref[...] = g
        nm_ref[...] = mn
        nv_ref[...] = vn

    spec = pl.BlockSpec((tr, C), lambda i: (i, 0))
    pspecs = [spec if p.ndim == 2 else pl.BlockSpec((p.shape[0], tr, C), lambda i: (0, i, 0)) for p in parts]
    return _pcall(
        body, name=name, grid=(R // tr,), in_specs=[spec] * 3 + pspecs, out_specs=[spec] * 4,
        out_shape=[jax.ShapeDtypeStruct((R, C), F32)] * 4,
        compiler_params=_params("parallel"))(w, m, v, *parts)


def _rows(a):
    return a.reshape(-1, LANES)


def _whole_from_columns(shards, *, name):
    S, K, n = shards.shape
    tk = _pick8(K, 1024)

    def body(s_ref, o_ref):
        o_ref[...] = s_ref[...]

    return _pcall(
        body, name=name, grid=(K // tk, S),
        in_specs=[pl.BlockSpec((None, tk, n), lambda i, s: (s, i, 0))],
        out_specs=pl.BlockSpec((tk, n), lambda i, s: (i, s)),
        out_shape=jax.ShapeDtypeStruct((K, S * n), shards.dtype),
        compiler_params=_params("parallel", "parallel"))(shards)


def _late_weights(a_w_out_rows, w_kv_cols, b_w_in_cols, b_w_out_rows):
    whole_rows = lambda g: g.reshape(g.shape[0] * g.shape[1], g.shape[2])
    return (whole_rows(a_w_out_rows), _whole_from_columns(w_kv_cols, name="w_kv_whole"),
            _whole_from_columns(b_w_in_cols, name="b_w_in_whole"), whole_rows(b_w_out_rows))


def _forward_backward(xs, target, a_norm, g_a_w_in, conv_w, conv_b, g_w_r, g_w_i, b_r, b_i, lam,
                      kv_norm, b_norm, final_norm, *, late_weights=None, late_shards=None, h_a=None):
    if h_a is None:
        (h_a,) = _rms_fwd(xs, [a_norm], name="a_norm_fwd")
    proj_a = _mm_nn(h_a, g_a_w_in, name="a_in_proj", out_dtype=F32)
    xb, h_rec, yg, *gathered = _acore_fwd(proj_a, conv_w, conv_b, g_w_r, g_w_i, b_r, b_i, lam,
                                          name="a_core_fwd", riders=late_shards or ())
    g_a_w_out, g_w_kv, g_b_w_in, g_b_w_out = _late_weights(*gathered) if late_shards else late_weights
    x1 = _mm_nn(yg, g_a_w_out, name="a_out_proj", out_dtype=F32, res=xs)
    hk, hb = _rms_fwd(x1, [kv_norm, b_norm], name="kv_b_norm_fwd")
    kv = _mm_nn(hk, g_w_kv, name="kv_proj", out_dtype=BF16)
    hd = g_b_w_in.shape[1] // 2
    q = _mm_nn(hb, g_b_w_in, name="q_proj", out_dtype=BF16, col_off=0, cols=hd)
    gate_b = _mm_nn(hb, g_b_w_in, name="b_gate_proj", out_dtype=F32, col_off=hd, cols=hd)
    o, og, ltot = _attn_fwd(q, kv, gate_b, name="attn_fwd")
    x2 = _mm_nn(og, g_b_w_out, name="b_out_proj", out_dtype=F32, res=x1)
    dx2, d_final_norm, loss_part = _final_loss(x2, target, final_norm, name="final_norm_loss")

    dog = _mm_nt(dx2, g_b_w_out, name="b_out_proj_bwd")
    dw_b_out = _mm_tn(og, dx2, name="b_out_proj_wgrad")
    dproj_b, dkv = _attn_bwd(q, kv, gate_b, o, ltot, dog, name="attn_bwd")
    dhb = _mm_nt(dproj_b, g_b_w_in, name="b_in_proj_bwd")
    dw_b_in = _mm_tn(hb, dproj_b, name="b_in_proj_wgrad", shards=N_DEV)
    dhk = _mm_nt(dkv, g_w_kv, name="kv_proj_bwd")
    dw_kv = _mm_tn(hk, dkv, name="kv_proj_wgrad", shards=N_DEV)
    early = [dw_kv, dw_b_in, dw_b_out.reshape(N_DEV, -1, dw_b_out.shape[1])] if late_shards else []
    dx1, d_b_norm, d_kv_norm, *got = _rms_bwd(x1, dx2, [dhb, dhk], [b_norm, kv_norm], name="kv_b_norm_bwd",
                                              rider=_sibling_rider(early) if early else None)
    early_sums = [_pair_sum(f_, g_, name=f"pair_sum_early_{i}") for i, (f_, g_) in enumerate(zip(early, got))]
    dyg = _mm_nt(dx1, g_a_w_out, name="a_out_proj_bwd")
    dw_a_out = _mm_tn(yg, dx1, name="a_out_proj_wgrad")
    (dproj_a, d_conv_w, d_conv_b, d_b_r, d_b_i, d_lambda, dw_r, dw_i, *early_others) = _acore_bwd(
        dyg, proj_a, xb, h_rec, conv_w, g_w_r, g_w_i, b_r, b_i, lam, name="a_core_bwd",
        rider=_chips_rider([s[1] for s in early_sums]) if early else None)
    dw_a_in = _mm_tn(h_a, dproj_a, name="a_in_proj_wgrad", shards=N_DEV)
    rows = dw_r.shape[1] // N_DEV
    lru = lambda dw: dw.reshape(-1, N_DEV, rows, dw.shape[2]).transpose(1, 0, 2, 3).reshape(N_DEV, -1, dw.shape[2])
    late = [dw_a_in, dw_a_out.reshape(N_DEV, -1, dw_a_out.shape[1]), lru(dw_r), lru(dw_i)] if late_shards else []
    dh_a, *got = _mm_nt(dproj_a, g_a_w_in, name="a_in_proj_bwd", rider=_sibling_rider(late)) if late else (
        _mm_nt(dproj_a, g_a_w_in, name="a_in_proj_bwd"),)
    late_sums = [_pair_sum(f_, g_, name=f"pair_sum_late_{i}") for i, (f_, g_) in enumerate(zip(late, got))]
    grad_x, d_a_norm, *late_others = _rms_bwd(xs, dx1, [dh_a], [a_norm], name="a_norm_bwd",
                                              rider=_chips_rider([s[1] for s in late_sums]) if late else None)
    sums = late_sums[:2] + early_sums + late_sums[2:]
    others = late_others[:2] + early_others + late_others[2:]
    return (loss_part, grad_x, dw_a_in, dw_a_out, dw_kv, dw_b_in, dw_b_out, dw_r, dw_i, d_a_norm,
            d_conv_w, d_conv_b, d_b_r, d_b_i, d_lambda, d_kv_norm, d_b_norm, d_final_norm, sums, others)


def kernel(x, a_norm, a_w_in, a_conv_w, a_conv_b, a_w_r, a_b_r, a_w_i, a_b_i, a_lambda, a_w_out, kv_norm, w_kv, b_norm, b_w_in, b_w_out, final_norm, loss_target, m_a_norm, m_a_w_in, m_a_conv_w, m_a_conv_b, m_a_w_r, m_a_b_r, m_a_w_i, m_a_b_i, m_a_lambda, m_a_w_out, m_kv_norm, m_w_kv, m_b_norm, m_b_w_in, m_b_w_out, m_final_norm, v_a_norm, v_a_w_in, v_a_conv_w, v_a_conv_b, v_a_w_r, v_a_b_r, v_a_w_i, v_a_b_i, v_a_lambda, v_a_w_out, v_kv_norm, v_w_kv, v_b_norm, v_b_w_in, v_b_w_out, v_final_norm):
    T, D = x.shape[1], x.shape[2]
    nb, bw = a_w_r.shape[1], a_w_r.shape[3]
    C = nb * bw
    me = 4 * lax.axis_index("x") + 2 * lax.axis_index("y") + lax.axis_index("c")
    xs = x[0]
    target = loss_target[0]

    rows_r = a_w_r.shape[2]
    small_f32 = jnp.concatenate([_rows(a_conv_w[0]), _rows(b_norm[0])], axis=0)
    pad = (-small_f32.shape[0]) % SUBLANES
    small_f32 = jnp.pad(small_f32, ((0, pad), (0, 0)))
    h_a, a_w_in_cols, w_r_rows, w_i_rows, small_all = _rms_fwd(
        xs, [a_norm], name="a_norm_fwd",
        riders=[a_w_in[0].astype(BF16), a_w_r[0].reshape(nb * rows_r, bw).astype(BF16),
                a_w_i[0].reshape(nb * rows_r, bw).astype(BF16), small_f32])
    late_shards = [a_w_out[0].astype(BF16), w_kv.astype(BF16), b_w_in[0].astype(BF16), b_w_out[0].astype(BF16)]
    g_a_w_in = _whole_from_columns(a_w_in_cols, name="a_w_in_whole")
    g_w_r = w_r_rows.reshape(N_DEV, nb, rows_r, bw).transpose(1, 0, 2, 3).reshape(nb, bw, bw)
    g_w_i = w_i_rows.reshape(N_DEV, nb, rows_r, bw).transpose(1, 0, 2, 3).reshape(nb, bw, bw)
    cw_rows = a_conv_w.shape[1] * a_conv_w.shape[2] // LANES
    conv_w_full = small_all[:, :cw_rows, :].reshape(N_DEV, CONV_W, a_conv_w.shape[2])
    conv_w_full = conv_w_full.transpose(1, 0, 2).reshape(CONV_W, C)
    bn_rows = b_norm.shape[1] // LANES
    b_norm_full = small_all[:, cw_rows:cw_rows + bn_rows, :].reshape(1, D)
    kv_norm2, final_norm2 = kv_norm.reshape(1, D), final_norm.reshape(1, D)

    (loss_part, grad_x, dw_a_in, dw_a_out, dw_kv, dw_b_in, dw_b_out, dw_r, dw_i, d_a_norm, d_conv_w,
     d_conv_b, d_b_r, d_b_i, d_lambda, d_kv_norm, d_b_norm, d_final_norm, sums,
     others) = _forward_backward(
         xs, target, a_norm, g_a_w_in, conv_w_full, a_conv_b, g_w_r, g_w_i, a_b_r, a_b_i, a_lambda,
         kv_norm2, b_norm_full, final_norm2, late_shards=late_shards, h_a=h_a)

    small_parts = [d_a_norm, d_conv_w, d_conv_b, d_b_r, d_b_i, d_lambda, d_kv_norm, d_b_norm, d_final_norm]
    small_sizes = [p.size // LANES for p in small_parts]
    small = jnp.concatenate([_rows(p) for p in small_parts], axis=0)
    small_everyone = _small_gather(small)

    def shard2d(w):
        return w.reshape(-1, w.shape[-1])

    names_big = [(a_w_in, m_a_w_in, v_a_w_in), (a_w_out, m_a_w_out, v_a_w_out), (w_kv, m_w_kv, v_w_kv),
                 (b_w_in, m_b_w_in, v_b_w_in), (b_w_out, m_b_w_out, v_b_w_out),
                 (a_w_r, m_a_w_r, v_a_w_r), (a_w_i, m_a_w_i, v_a_w_i)]
    upd_big = []
    for i, (w, m, v) in enumerate(names_big):
        res = _adamw(shard2d(w), shard2d(m), shard2d(v), [sums[i][0], others[i]], name=f"adamw_{i}")
        upd_big.append([r.reshape(w.shape) for r in res])

    soffs = [0]
    for s in small_sizes:
        soffs.append(soffs[-1] + s)

    def small_piece(i):
        return small_everyone[:, soffs[i]:soffs[i + 1], :]

    cw_cols = a_conv_w.shape[2]
    conv_piece = small_piece(1).reshape(N_DEV, CONV_W, C)
    conv_piece = lax.dynamic_slice_in_dim(conv_piece, me * cw_cols, cw_cols, axis=2)
    conv_piece = conv_piece.reshape(N_DEV, CONV_W * cw_cols // LANES, LANES)
    bn_piece = lax.dynamic_slice_in_dim(small_piece(7), me * bn_rows, bn_rows, axis=1)
    small_g = jnp.concatenate([small_piece(0), conv_piece, small_piece(2), small_piece(3), small_piece(4),
                               small_piece(5), small_piece(6), bn_piece, small_piece(8)], axis=1)
    small_w = [(a_norm, m_a_norm, v_a_norm), (a_conv_w, m_a_conv_w, v_a_conv_w),
               (a_conv_b, m_a_conv_b, v_a_conv_b), (a_b_r, m_a_b_r, v_a_b_r), (a_b_i, m_a_b_i, v_a_b_i),
               (a_lambda, m_a_lambda, v_a_lambda), (kv_norm, m_kv_norm, v_kv_norm),
               (b_norm, m_b_norm, v_b_norm), (final_norm, m_final_norm, v_final_norm)]
    pack = lambda idx: jnp.concatenate([_rows(t[idx]) for t in small_w], axis=0)
    res_small = _adamw(pack(0), pack(1), pack(2), [small_g], name="adamw_small")
    woffs = [0]
    for t in small_w:
        woffs.append(woffs[-1] + t[0].size // LANES)
    upd_small = [[r[woffs[i]:woffs[i + 1]].reshape(small_w[i][0].shape) for r in res_small]
                 for i in range(len(small_w))]

    order = [("s", 0), ("b", 0), ("s", 1), ("s", 2), ("b", 5), ("s", 3), ("b", 6), ("s", 4), ("s", 5),
             ("b", 1), ("s", 6), ("b", 2), ("s", 7), ("b", 3), ("b", 4), ("s", 8)]
    per_weight = [(upd_big if kind == "b" else upd_small)[i] for kind, i in order]
    loss = lax.psum(loss_part[0, 0], ("x", "y", "c"))
    result = [loss, grad_x[None]]
    for field in range(4):
        result += [u[field] for u in per_weight]
    return tuple(result)
```

```python
import math

import jax
import jax.numpy as jnp
from jax import lax
from jax.experimental import pallas as pl
from jax.experimental.pallas import tpu as pltpu

F32 = jnp.float32
BF16 = jnp.bfloat16
MESH = pl.DeviceIdType.MESH

EPS = 1e-6
LOG2E = 1.4426950408889634
WEIGHT_FLOOR_LOG2 = -200.0
LRU_C = 8.0
CONV_W = 4
HEAD_DIM = 128
ADAM_LR = 0.001
ADAM_B1 = 0.9
ADAM_B2 = 0.999
ADAM_EPS = 1e-08
ADAM_WD = 0.01
ADAM_STEP = 10

N_DEV = 8
LANES = 128
SUBLANES = 8
VMEM_LIMIT = 56 * 1024 * 1024

ATT_KEY_BLOCK = 256
ATT_QUERY_BLOCK = 256
ATT_STEP_BLOCKS = 4
SCAN_BLOCK = 256
ROW_BLOCK = 256
MM_TOKEN_BLOCK = 1024
MM_WIDE_K = 2560
MM_WEIGHT_TILE = 1280
MM_CONTRACT_TOKENS = 2048
ANY = pl.BlockSpec(memory_space=pl.ANY)


def _pcall(body, **kw):
    return pl.pallas_call(body, **kw)


def _params(*sem):
    return pltpu.CompilerParams(dimension_semantics=sem, vmem_limit_bytes=VMEM_LIMIT)


def _pick(n, cap):
    if n <= cap:
        return n
    best = None
    for t in range(LANES, cap + 1, LANES):
        if n % t == 0:
            best = t
    assert best is not None, (n, cap)
    return best


def _token_block(T, K):
    tm = MM_TOKEN_BLOCK if K <= MM_WIDE_K else MM_TOKEN_BLOCK // 2
    return tm if T % tm == 0 else T


def _sigmoid(x):
    return 1.0 / (1.0 + jnp.exp(-x))


def _dot(a, b, ca, cb):
    return lax.dot_general(a, b, (((ca,), (cb,)), ((), ())), preferred_element_type=F32)


def _mm_nn(a, w, *, name, out_dtype, col_off=0, cols=None, res=None):
    T, K = a.shape
    K2, N = w.shape
    assert K == K2
    cols = N if cols is None else cols
    tm = _token_block(T, K)
    tn = _pick(cols, MM_WEIGHT_TILE)
    assert col_off % tn == 0
    off = col_off // tn
    has_res = res is not None

    def body(a_ref, b_ref, *rest):
        o_ref = rest[-1]
        acc = jnp.dot(a_ref[...].astype(BF16), b_ref[...], preferred_element_type=F32)
        if has_res:
            acc = acc + rest[0][...]
        o_ref[...] = acc.astype(out_dtype)

    in_specs = [pl.BlockSpec((tm, K), lambda j, i: (i, 0)),
                pl.BlockSpec((K, tn), lambda j, i: (0, off + j))]
    args = [a, w]
    if has_res:
        in_specs.append(pl.BlockSpec((tm, tn), lambda j, i: (i, j)))
        args.append(res)
    return _pcall(
        body, name=name, grid=(cols // tn, T // tm), in_specs=in_specs,
        out_specs=pl.BlockSpec((tm, tn), lambda j, i: (i, j)),
        out_shape=jax.ShapeDtypeStruct((T, cols), out_dtype),
        compiler_params=_params("parallel", "parallel"))(*args)


def _mm_nt(a, w, *, name, out_dtype=F32, rider=None):
    parts = a.shape[0] if a.ndim == 3 else 1
    T, kp = a.shape[-2:]
    N, K = w.shape
    assert K == parts * kp
    tm = _token_block(T, K)
    tn = _pick(N, MM_WEIGHT_TILE)
    nj, ni = N // tn, T // tm
    rider = rider or _Rider([], [], [], None)
    nri, nro = len(rider.inputs), len(rider.out_shapes)

    def body(a_ref, b_ref, *refs):
        o_ref = refs[nri]
        j, i = pl.program_id(0), pl.program_id(1)
        if nro:
            start, finish = rider.bind(refs[:nri], refs[nri + 1:nri + 1 + nro], refs[nri + 1 + nro:])
            pl.when((j == 0) & (i == 0))(start)
        if a.ndim == 3:
            acc = None
            for p in range(parts):
                term = _dot(a_ref[p], b_ref[:, p * kp:(p + 1) * kp], 1, 1)
                acc = term if acc is None else acc + term
        else:
            acc = _dot(a_ref[...].astype(BF16), b_ref[...], 1, 1)
        o_ref[...] = acc.astype(out_dtype)
        if nro:
            pl.when((j == nj - 1) & (i == ni - 1))(finish)

    a_spec = (pl.BlockSpec((parts, tm, kp), lambda j, i: (0, i, 0)) if a.ndim == 3
              else pl.BlockSpec((tm, K), lambda j, i: (i, 0)))
    sem = ("arbitrary", "arbitrary") if nro else ("parallel", "parallel")
    out = _pcall(
        body, name=name, grid=(nj, ni),
        in_specs=[a_spec, pl.BlockSpec((tn, K), lambda j, i: (j, 0))] + [ANY] * nri,
        out_specs=[pl.BlockSpec((tm, tn), lambda j, i: (i, j))] + [ANY] * nro,
        out_shape=[jax.ShapeDtypeStruct((T, N), out_dtype)] + rider.out_shapes,
        scratch_shapes=rider.scratch,
        compiler_params=_params(*sem))(a, w, *rider.inputs)
    return out if nro else out[0]


def _mm_tn(a, b, *, name, shards=1):
    T, Ko = a.shape
    parts = b.shape[0] if b.ndim == 3 else 1
    T2, n_part = b.shape[-2:]
    N = parts * n_part
    assert T == T2
    n = N // shards
    tt = min(T, MM_CONTRACT_TOKENS)
    tko = _pick(Ko, 1024)
    span = 2 if shards > 1 and 2 * n <= MM_WEIGHT_TILE and n_part % (2 * n) == 0 else 1
    tn = span * n if span > 1 else _pick(n, 1024)
    per = max(n // tn, 1)
    assert n_part % tn == 0
    per_part = n_part // tn

    def body(a_ref, b_ref, o_ref):
        t = pl.program_id(2)
        p = _dot(a_ref[...].astype(BF16), b_ref[...].astype(BF16), 0, 0)
        pieces = [p] if span == 1 else [p[:, s * n:(s + 1) * n] for s in range(span)]

        @pl.when(t == 0)
        def _():
            for s, piece in enumerate(pieces):
                o_ref[(s,) if span > 1 else ...] = piece

        @pl.when(t > 0)
        def _():
            for s, piece in enumerate(pieces):
                o_ref[(s,) if span > 1 else ...] += piece

    if shards == 1:
        out_spec = pl.BlockSpec((tko, tn), lambda i, j, t: (i, j))
        out_shape = jax.ShapeDtypeStruct((Ko, N), F32)
    elif span > 1:
        out_spec = pl.BlockSpec((span, tko, n), lambda i, j, t: (j, i, 0))
        out_shape = jax.ShapeDtypeStruct((shards, Ko, n), F32)
    else:
        out_spec = pl.BlockSpec((None, tko, tn), lambda i, j, t: (j // per, i, j % per))
        out_shape = jax.ShapeDtypeStruct((shards, Ko, n), F32)
    b_spec = (pl.BlockSpec((None, tt, tn), lambda i, j, t: (j // per_part, t, j % per_part)) if b.ndim == 3
              else pl.BlockSpec((tt, tn), lambda i, j, t: (t, j)))
    return _pcall(
        body, name=name, grid=(Ko // tko, N // tn, T // tt),
        in_specs=[pl.BlockSpec((tt, tko), lambda i, j, t: (t, i)), b_spec],
        out_specs=out_spec, out_shape=out_shape,
        compiler_params=_params("parallel", "parallel", "arbitrary"))(a, b)


def _rms_fwd(x, gains, *, name, riders=()):
    T, D = x.shape
    tm = min(T, ROW_BLOCK)
    steps = T // tm
    n, nr = len(gains), len(riders)

    def body(x_ref, *refs):
        g_refs, rider_in = refs[:n], refs[n:n + nr]
        o_refs, rider_out = refs[n + nr:2 * n + nr], refs[2 * n + nr:2 * n + 2 * nr]
        if nr:
            gather = _Gather(rider_in, rider_out, *refs[2 * n + 2 * nr:])
            pl.when(pl.program_id(0) == 0)(gather.start)
        xv = x_ref[...]
        xh = xv * lax.rsqrt(jnp.mean(xv * xv, axis=-1, keepdims=True) + EPS)
        for g_ref, o_ref in zip(g_refs, o_refs):
            o_ref[...] = (xh * g_ref[...]).astype(BF16)
        if nr:
            pl.when(pl.program_id(0) == steps - 1)(gather.finish)

    row = pl.BlockSpec((tm, D), lambda i: (i, 0))
    vec = pl.BlockSpec((1, D), lambda i: (0, 0))
    return _pcall(
        body, name=name, grid=(steps,), in_specs=[row] + [vec] * n + [ANY] * nr,
        out_specs=[row] * n + [ANY] * nr,
        out_shape=[jax.ShapeDtypeStruct((T, D), BF16)] * n
                  + [jax.ShapeDtypeStruct((N_DEV,) + r.shape, r.dtype) for r in riders],
        scratch_shapes=_Gather.scratch(nr) if nr else [],
        compiler_params=_params("arbitrary" if nr else "parallel"))(x, *gains, *riders)


def _rms_bwd(x, dres, dhs, gains, *, name, rider=None):
    T, D = x.shape
    tm = min(T, ROW_BLOCK)
    steps = T // tm
    n = len(gains)
    rider = rider or _Rider([], [], [], None)
    nri, nro = len(rider.inputs), len(rider.out_shapes)

    def body(x_ref, dres_ref, *refs):
        dh_refs, g_refs = refs[:n], refs[n:2 * n]
        refs = refs[2 * n:]
        rider_in, refs = refs[:nri], refs[nri:]
        dx_ref, dg_refs = refs[0], refs[1:1 + n]
        rider_out, sems = refs[1 + n:1 + n + nro], refs[1 + n + nro:]
        i = pl.program_id(0)
        if nro:
            start, finish = rider.bind(rider_in, rider_out, sems)
            pl.when(i == 0)(start)
        xv = x_ref[...]
        r = lax.rsqrt(jnp.mean(xv * xv, axis=-1, keepdims=True) + EPS)
        xh = xv * r
        dxh = jnp.zeros_like(xv)
        for dh_ref, g_ref, dg_ref in zip(dh_refs, g_refs, dg_refs):
            dh = dh_ref[...]
            part = jnp.sum(dh * xh, axis=0, keepdims=True)

            @pl.when(i == 0)
            def _():
                dg_ref[...] = part

            @pl.when(i > 0)
            def _():
                dg_ref[...] += part

            dxh = dxh + dh * g_ref[...]
        dx_ref[...] = dres_ref[...] + r * (dxh - xh * jnp.mean(dxh * xh, axis=-1, keepdims=True))
        if nro:
            pl.when(i == steps - 1)(finish)

    row = pl.BlockSpec((tm, D), lambda i: (i, 0))
    vec = pl.BlockSpec((1, D), lambda i: (0, 0))
    return _pcall(
        body, name=name, grid=(steps,), in_specs=[row, row] + [row] * n + [vec] * n + [ANY] * nri,
        out_specs=[row] + [vec] * n + [ANY] * nro,
        out_shape=[jax.ShapeDtypeStruct((T, D), F32)] + [jax.ShapeDtypeStruct((1, D), F32)] * n
                  + rider.out_shapes,
        scratch_shapes=rider.scratch,
        compiler_params=_params("arbitrary"))(x, dres, *dhs, *gains, *rider.inputs)


def _final_loss(x, target, gain, *, name):
    T, D = x.shape
    tm = min(T, ROW_BLOCK)

    def body(x_ref, t_ref, g_ref, dx_ref, dg_ref, loss_ref):
        i = pl.program_id(0)
        xv = x_ref[...]
        g = g_ref[...]
        r = lax.rsqrt(jnp.mean(xv * xv, axis=-1, keepdims=True) + EPS)
        xh = xv * r
        err = xh * g - t_ref[...]
        part_loss = 0.5 * jnp.sum(jnp.mean(err * err, axis=-1, keepdims=True), axis=0, keepdims=True)
        dy = err * (1.0 / D)
        part_g = jnp.sum(dy * xh, axis=0, keepdims=True)

        @pl.when(i == 0)
        def _():
            dg_ref[...] = part_g
            loss_ref[...] = jnp.broadcast_to(part_loss, loss_ref.shape)

        @pl.when(i > 0)
        def _():
            dg_ref[...] += part_g
            loss_ref[...] += jnp.broadcast_to(part_loss, loss_ref.shape)

        dxh = dy * g
        dx_ref[...] = r * (dxh - xh * jnp.mean(dxh * xh, axis=-1, keepdims=True))

    row = pl.BlockSpec((tm, D), lambda i: (i, 0))
    vec = pl.BlockSpec((1, D), lambda i: (0, 0))
    return _pcall(
        body, name=name, grid=(T // tm,), in_specs=[row, row, vec],
        out_specs=[row, vec, pl.BlockSpec((1, LANES), lambda i: (0, 0))],
        out_shape=[jax.ShapeDtypeStruct((T, D), F32), jax.ShapeDtypeStruct((1, D), F32),
                   jax.ShapeDtypeStruct((1, LANES), F32)],
        compiler_params=_params("arbitrary"))(x, target, gain)


def _shift_down(x, prev_tail, j, row):
    tb = x.shape[0]
    prev = jnp.tile(prev_tail, (tb // SUBLANES, 1))
    return jnp.where(row >= j, pltpu.roll(x, j, 0), pltpu.roll(prev, j, 0))


def _shift_up(x, next_head, j, row):
    tb = x.shape[0]
    nxt = jnp.tile(next_head, (tb // SUBLANES, 1))
    return jnp.where(row < tb - j, pltpu.roll(x, tb - j, 0), pltpu.roll(nxt, tb - j, 0))


def _lru_gates(xb, wr, wi, br, bi, lam):
    xbb = xb.astype(BF16)
    r = _sigmoid(jnp.dot(xbb, wr, preferred_element_type=F32) + br)
    i = _sigmoid(jnp.dot(xbb, wi, preferred_element_type=F32) + bi)
    sp = jnp.maximum(-lam, 0.0) + jnp.log1p(jnp.exp(-jnp.abs(lam)))
    log_a = (-LRU_C) * r * sp
    a = jnp.exp(log_a)
    a2 = a * a
    mult = jnp.sqrt(jnp.maximum(-jnp.tanh(log_a) * (1.0 + a2), 0.0))
    return xbb, r, i, sp, a, a2, mult


def _scan_rows(coef, val, edge, reverse):
    tb, C = coef.shape
    a, b = coef, val
    row = lax.broadcasted_iota(jnp.int32, (tb, C), 0)
    s = 1
    while s < tb:
        m = (row < tb - s) if reverse else (row >= s)
        shift = tb - s if reverse else s
        b = jnp.where(m, a * pltpu.roll(b, shift, 0) + b, b)
        a = jnp.where(m, a * pltpu.roll(a, shift, 0), a)
        s *= 2
    return b + a * edge


def _acore_fwd(proj, conv_w, conv_b, w_r, w_i, b_r, b_i, lam, *, name, riders=()):
    T, C2 = proj.shape
    C = C2 // 2
    nb, bw, _ = w_r.shape
    tb = min(T, SCAN_BLOCK)
    nt = T // tb
    nr = len(riders)

    def body(xp_ref, gate_ref, cw_ref, cb_ref, wr_ref, wi_ref, br_ref, bi_ref, lam_ref, *refs):
        rider_in, refs = refs[:nr], refs[nr:]
        xb_ref, h_ref, yg_ref = refs[:3]
        rider_out, refs = refs[3:3 + nr], refs[3 + nr:]
        tail_ref, hlast_ref = refs[:2]
        t = pl.program_id(1)
        if nr:
            gather = _Gather(rider_in, rider_out, *refs[2:])
            pl.when((pl.program_id(0) == 0) & (t == 0))(gather.start)

        @pl.when(t == 0)
        def _():
            tail_ref[...] = jnp.zeros_like(tail_ref)
            hlast_ref[...] = jnp.zeros_like(hlast_ref)

        row = lax.broadcasted_iota(jnp.int32, (tb, bw), 0)
        xp = xp_ref[...]
        tail = tail_ref[...]
        xb = cb_ref[...] + cw_ref[CONV_W - 1:CONV_W, :] * xp
        for j in range(1, CONV_W):
            xb = xb + cw_ref[CONV_W - 1 - j:CONV_W - j, :] * _shift_down(xp, tail, j, row)
        tail_ref[...] = xp[tb - SUBLANES:, :]
        xb_ref[...] = xb

        _, r, i, sp, a, a2, mult = _lru_gates(xb, wr_ref[...], wi_ref[...], br_ref[...], bi_ref[...],
                                              lam_ref[...])
        h = _scan_rows(a, mult * (i * xb), hlast_ref[SUBLANES - 1:SUBLANES, :], False)
        hlast_ref[...] = h[tb - SUBLANES:, :]
        h_ref[...] = h
        gate = gate_ref[...]
        yg_ref[...] = (h * (gate * _sigmoid(gate))).astype(BF16)
        if nr:
            pl.when((pl.program_id(0) == nb - 1) & (t == nt - 1))(gather.finish)

    blk = lambda off: pl.BlockSpec((tb, bw), lambda n, t: (t, off + n))
    vec = pl.BlockSpec((1, bw), lambda n, t: (0, n))
    wspec = pl.BlockSpec((None, bw, bw), lambda n, t: (n, 0, 0))
    return _pcall(
        body, name=name, grid=(nb, nt),
        in_specs=[blk(0), blk(nb), pl.BlockSpec((CONV_W, bw), lambda n, t: (0, n)), vec, wspec, wspec,
                  vec, vec, vec] + [ANY] * nr,
        out_specs=[blk(0), blk(0), blk(0)] + [ANY] * nr,
        out_shape=[jax.ShapeDtypeStruct((T, C), F32), jax.ShapeDtypeStruct((T, C), F32),
                   jax.ShapeDtypeStruct((T, C), BF16)]
                  + [jax.ShapeDtypeStruct((N_DEV,) + r.shape, r.dtype) for r in riders],
        scratch_shapes=[pltpu.VMEM((SUBLANES, bw), F32), pltpu.VMEM((SUBLANES, bw), F32)]
                       + (_Gather.scratch(nr) if nr else []),
        compiler_params=_params("arbitrary" if nr else "parallel", "arbitrary"))(
            proj, proj, conv_w, conv_b, w_r, w_i, b_r, b_i, lam, *riders)


def _acore_bwd(dyg, proj, xb_all, h_all, conv_w, w_r, w_i, b_r, b_i, lam, *, name, rider=None):
    T, C2 = proj.shape
    C = C2 // 2
    nb, bw, _ = w_r.shape
    tb = min(T, SCAN_BLOCK)
    nt = T // tb
    per8 = tb // SUBLANES
    rider = rider or _Rider([], [], [], None)
    nri, nro = len(rider.inputs), len(rider.out_shapes)

    def body(dyg_ref, xp_ref, gate_ref, xb_ref, h_ref, xp_prev_ref, h_prev_ref, cw_ref,
             wr_ref, wi_ref, br_ref, bi_ref, lam_ref, *refs):
        rider_in, refs = refs[:nri], refs[nri:]
        dproj_ref, dcw_ref, dcb_ref, dbr_ref, dbi_ref, dlam_ref, dwr_ref, dwi_ref = refs[:8]
        rider_out, refs = refs[8:8 + nro], refs[8 + nro:]
        gh_next_ref, a_next_ref, dxb_next_ref = refs[:3]
        step = pl.program_id(1)
        first_block = step == nt - 1
        if nro:
            start, finish = rider.bind(rider_in, rider_out, refs[3:])
            pl.when((pl.program_id(0) == 0) & (step == 0))(start)

        @pl.when(step == 0)
        def _():
            gh_next_ref[...] = jnp.zeros_like(gh_next_ref)
            a_next_ref[...] = jnp.zeros_like(a_next_ref)
            dxb_next_ref[...] = jnp.zeros_like(dxb_next_ref)

        row = lax.broadcasted_iota(jnp.int32, (tb, bw), 0)
        keep = jnp.where(first_block, 0.0, 1.0)
        h_prev = h_prev_ref[...] * keep
        xp_prev = xp_prev_ref[...] * keep
        xp, gate, xb, h, dyg_v = xp_ref[...], gate_ref[...], xb_ref[...], h_ref[...], dyg_ref[...]
        lam_v = lam_ref[...]
        wr, wi = wr_ref[...], wi_ref[...]

        sg = _sigmoid(gate)
        dh = dyg_v * (gate * sg)
        dproj_ref[1] = (dyg_v * h * (sg * (1.0 + gate * (1.0 - sg)))).astype(BF16)

        xbb, r, i, sp, a, a2, mult = _lru_gates(xb, wr, wi, br_ref[...], bi_ref[...], lam_v)

        gh = _scan_rows(_shift_up(a, a_next_ref[...], 1, row), dh, gh_next_ref[0:1, :], True)
        gh_next_ref[...] = gh[0:SUBLANES, :]
        a_next_ref[...] = a[0:SUBLANES, :]

        da = gh * _shift_down(h, h_prev, 1, row)
        dmult = gh * (i * xb)
        di = gh * mult * xb
        dxb = gh * mult * i
        dla = da * a - dmult * jnp.where(mult > 0.0, a2 / mult, 0.0)
        dr = dla * ((-LRU_C) * sp)
        dsp = jnp.sum(dla * ((-LRU_C) * r), axis=0, keepdims=True)
        dlam_part = dsp * (-_sigmoid(-lam_v))
        dpr = dr * r * (1.0 - r)
        dpi = di * i * (1.0 - i)
        dbr_part = jnp.sum(dpr, axis=0, keepdims=True)
        dbi_part = jnp.sum(dpi, axis=0, keepdims=True)
        dprb, dpib = dpr.astype(BF16), dpi.astype(BF16)
        dwr_part = _dot(xbb, dprb, 0, 0)
        dwi_part = _dot(xbb, dpib, 0, 0)
        dxb = dxb + _dot(dprb, wr, 1, 1) + _dot(dpib, wi, 1, 1)

        dxb_next = dxb_next_ref[...]
        dxp = cw_ref[CONV_W - 1:CONV_W, :] * dxb
        for j in range(1, CONV_W):
            dxp = dxp + cw_ref[CONV_W - 1 - j:CONV_W - j, :] * _shift_up(dxb, dxb_next, j, row)
        dxb_next_ref[...] = dxb[0:SUBLANES, :]
        dproj_ref[0] = dxp.astype(BF16)
        dcb_part = jnp.sum(dxb, axis=0, keepdims=True)
        dcw_rows = []
        for k in range(CONV_W):
            j = CONV_W - 1 - k
            sh = xp if j == 0 else _shift_down(xp, xp_prev, j, row)
            dcw_rows.append(jnp.sum(dxb * sh, axis=0, keepdims=True))

        @pl.when(step == 0)
        def _():
            for k in range(CONV_W):
                dcw_ref[k:k + 1, :] = dcw_rows[k]
            dcb_ref[...] = dcb_part
            dbr_ref[...] = dbr_part
            dbi_ref[...] = dbi_part
            dlam_ref[...] = dlam_part
            dwr_ref[...] = dwr_part
            dwi_ref[...] = dwi_part

        @pl.when(step > 0)
        def _():
            for k in range(CONV_W):
                dcw_ref[k:k + 1, :] += dcw_rows[k]
            dcb_ref[...] += dcb_part
            dbr_ref[...] += dbr_part
            dbi_ref[...] += dbi_part
            dlam_ref[...] += dlam_part
            dwr_ref[...] += dwr_part
            dwi_ref[...] += dwi_part

        if nro:
            pl.when((pl.program_id(0) == nb - 1) & (step == nt - 1))(finish)

    rev = lambda s: nt - 1 - s
    blk = lambda off: pl.BlockSpec((tb, bw), lambda n, s: (rev(s), off + n))
    prev8 = lambda off: pl.BlockSpec(
        (SUBLANES, bw), lambda n, s: (jnp.maximum(rev(s) * per8 - 1, 0), off + n))
    vec = pl.BlockSpec((1, bw), lambda n, s: (0, n))
    wspec = pl.BlockSpec((None, bw, bw), lambda n, s: (n, 0, 0))
    cwspec = pl.BlockSpec((CONV_W, bw), lambda n, s: (0, n))
    vshape = jax.ShapeDtypeStruct((1, C), F32)
    wshape = jax.ShapeDtypeStruct((nb, bw, bw), F32)
    return _pcall(
        body, name=name, grid=(nb, nt),
        in_specs=[blk(0), blk(0), blk(nb), blk(0), blk(0), prev8(0), prev8(0), cwspec,
                  wspec, wspec, vec, vec, vec] + [ANY] * nri,
        out_specs=[pl.BlockSpec((2, tb, bw), lambda n, s: (0, rev(s), n)), cwspec, vec, vec, vec, vec,
                   wspec, wspec] + [ANY] * nro,
        out_shape=[jax.ShapeDtypeStruct((2, T, C), BF16),
                   jax.ShapeDtypeStruct((CONV_W, C), F32), vshape, vshape, vshape, vshape,
                   wshape, wshape] + rider.out_shapes,
        scratch_shapes=[pltpu.VMEM((SUBLANES, bw), F32)] * 3 + rider.scratch,
        compiler_params=_params("arbitrary" if nro else "parallel", "arbitrary"))(
            dyg, proj, proj, xb_all, h_all, proj, h_all, conv_w, w_r, w_i, b_r, b_i, lam, *rider.inputs)


def _later_sum(lk, tri):
    return jnp.dot(lk.astype(BF16), tri, preferred_element_type=F32)


def _log2_sigmoids(y):
    t = jnp.log(1.0 + jnp.exp2(-jnp.abs(y))) * LOG2E
    ls = jnp.minimum(y, 0.0) - t
    return ls, ls - y


def _attn_blocks(T):
    bk = min(T, ATT_KEY_BLOCK)
    bq = min(T, ATT_QUERY_BLOCK)
    return bq, bk, bq // bk


def _attn_step_blocks(T, bq):
    return ATT_STEP_BLOCKS if (T // bq) % ATT_STEP_BLOCKS == 0 else 1


def _attn_fwd(q, kv, gate, *, name):
    T, HD = q.shape
    H = HD // HEAD_DIM
    bq, bk, per = _attn_blocks(T)
    scale = 1.0 / math.sqrt(HEAD_DIM)

    nsub = _attn_step_blocks(T, bq)

    def body(q_ref, k_ref, v_ref, g_ref, o_ref, og_ref, lt_ref, w_ref):
        for sub in range(nsub):
            rows = pl.ds(sub * bq, bq)
            one_block(pl.program_id(1) * nsub + sub, q_ref.at[rows], k_ref, v_ref, g_ref.at[rows],
                      o_ref.at[rows], og_ref.at[rows], lt_ref.at[rows], w_ref)

    def one_block(i, q_ref, k_ref, v_ref, g_ref, o_ref, og_ref, lt_ref, w_ref):
        qv = q_ref[...]
        tr = lax.broadcasted_iota(jnp.int32, (bk, bk), 0)
        tc = lax.broadcasted_iota(jnp.int32, (bk, bk), 1)
        tri = (tr > tc).astype(BF16)
        ahead = (lax.broadcasted_iota(jnp.int32, (bq, bk), 0)
                 - lax.broadcasted_iota(jnp.int32, (bq, bk), 1))

        def starts_of(top):
            return [pl.multiple_of((top - d) * bk, bk) for d in range(per)]

        def scores(top):
            return [_dot(qv, k_ref[pl.ds(ks, bk), :], 1, 1) for ks in starts_of(top)]

        def weights(top, zs, c, mask):
            lss, sums, css, causals = [], [], [], []
            for ks, z in zip(starts_of(top), zs):
                ls, lk = _log2_sigmoids(z * (scale * LOG2E))
                if mask:
                    causals.append(ahead > ks - i * bq)
                    lk = jnp.where(causals[-1], lk, 0.0)
                lss.append(ls)
                sums.append(jnp.sum(lk, axis=1, keepdims=True))
                css.append(_later_sum(lk, tri))
            for d in range(per):
                w = jnp.exp2(lss[d] + (css[d] + c))
                if mask:
                    w = jnp.where(causals[d], w, 0.0)
                w_ref[d] = w.astype(BF16)
                c = c + sums[d]
            return c

        def values(top, acc):
            for d, ks in enumerate(starts_of(top)):
                acc = acc + jnp.dot(w_ref[d], v_ref[pl.ds(ks, bk), :], preferred_element_type=F32)
            return acc

        def more(state):
            gg, _, _, largest = state
            return (gg <= i) & (largest > WEIGHT_FLOOR_LOG2)

        def step(state):
            gg, acc, c, _ = state
            top = (i - gg) * per + per - 1
            zs = scores(top)
            acc = values(top + per, acc)
            c = weights(top, zs, c, False)
            return gg + 1, acc, c, jnp.max(c)

        diag_top = i * per + per - 1
        c = weights(diag_top, scores(diag_top), jnp.zeros((bq, 1), F32), True)
        gg, acc, c, _ = lax.while_loop(more, step, (1, jnp.zeros((bq, HEAD_DIM), F32), c, jnp.max(c)))
        acc = values((i - gg + 1) * per + per - 1, acc)
        o_ref[...] = acc
        g = g_ref[...]
        og_ref[...] = (acc * (g * _sigmoid(g))).astype(BF16)
        lane = lax.broadcasted_iota(jnp.int32, (bq, HEAD_DIM), 1)
        lt_ref[...] = jnp.where(lane == 1, (i - gg + 1).astype(F32), jnp.broadcast_to(c, (bq, HEAD_DIM)))

    qspec = pl.BlockSpec((nsub * bq, HEAD_DIM), lambda h, i: (i, h))
    return _pcall(
        body, name=name, grid=(H, T // (nsub * bq)),
        in_specs=[qspec, pl.BlockSpec((T, HEAD_DIM), lambda h, i: (0, h)),
                  pl.BlockSpec((T, HEAD_DIM), lambda h, i: (0, H + h)), qspec],
        out_specs=[qspec, qspec, qspec],
        out_shape=[jax.ShapeDtypeStruct((T, HD), F32), jax.ShapeDtypeStruct((T, HD), BF16),
                   jax.ShapeDtypeStruct((T, HD), F32)],
        scratch_shapes=[pltpu.VMEM((per, bq, bk), BF16)],
        compiler_params=_params("parallel", "arbitrary"))(q, kv, kv, gate)


def _attn_bwd(q, kv, gate, o, ltot, dog, *, name):
    T, HD = q.shape
    H = HD // HEAD_DIM
    bq, bk, per = _attn_blocks(T)
    nq = T // bq
    scale = 1.0 / math.sqrt(HEAD_DIM)

    nsub = _attn_step_blocks(T, bq)

    def body(q_ref, k_ref, v_ref, g_ref, o_ref, lt_ref, dog_ref,
             dqg_ref, dkv_ref, dk_acc, dv_acc, dz_ref, w_ref):
        for sub in range(nsub):
            rows = pl.ds(sub * bq, bq)
            one_block(pl.program_id(1) * nsub + sub, q_ref.at[rows], k_ref, v_ref, g_ref.at[rows],
                      o_ref.at[rows], lt_ref.at[rows], dog_ref.at[rows], dqg_ref.at[:, rows], dkv_ref,
                      dk_acc, dv_acc, dz_ref, w_ref)

    def one_block(i, q_ref, k_ref, v_ref, g_ref, o_ref, lt_ref, dog_ref,
                  dqg_ref, dkv_ref, dk_acc, dv_acc, dz_ref, w_ref):
        @pl.when(i == 0)
        def _():
            dk_acc[...] = jnp.zeros_like(dk_acc)
            dv_acc[...] = jnp.zeros_like(dv_acc)

        qv = q_ref[...]
        g, ov, dogv = g_ref[...], o_ref[...], dog_ref[...]
        sg = _sigmoid(g)
        do = dogv * (g * sg)
        dqg_ref[1] = (dogv * ov * (sg * (1.0 + g * (1.0 - sg)))).astype(BF16)
        dob = do.astype(BF16)
        ltot_v = lt_ref[:, 0:1]
        tr = lax.broadcasted_iota(jnp.int32, (bk, bk), 0)
        tc = lax.broadcasted_iota(jnp.int32, (bk, bk), 1)
        tri_later = (tr > tc).astype(BF16)
        tri_excl = (tr < tc).astype(BF16)
        ahead = (lax.broadcasted_iota(jnp.int32, (bq, bk), 0)
                 - lax.broadcasted_iota(jnp.int32, (bq, bk), 1))

        def starts_of(first):
            return [pl.multiple_of((first + d) * bk, bk) for d in range(per)]

        def scores(first):
            return ([_dot(qv, k_ref[pl.ds(ks, bk), :], 1, 1) for ks in starts_of(first)],
                    [_dot(dob, v_ref[pl.ds(ks, bk), :], 1, 1) for ks in starts_of(first)])

        def front(first, zs, dws, p_lk, p_g, mask):
            lss, css, causals = [], [], []
            for ks, z in zip(starts_of(first), zs):
                ls, lk = _log2_sigmoids(z * (scale * LOG2E))
                if mask:
                    causals.append(ahead > ks - i * bq)
                    lk = jnp.where(causals[-1], lk, 0.0)
                lss.append(ls)
                p_lk = p_lk + jnp.sum(lk, axis=1, keepdims=True)
                css.append((ltot_v - p_lk) + _later_sum(lk, tri_later))
            gms, befores = [], []
            for d in range(per):
                w = jnp.exp2(lss[d] + css[d])
                if mask:
                    w = jnp.where(causals[d], w, 0.0)
                gm = dws[d] * w
                gms.append(gm)
                w_ref[d] = w.astype(BF16)
                befores.append(jnp.dot(gm.astype(BF16), tri_excl, preferred_element_type=F32) + p_g)
                p_g = p_g + jnp.sum(gm, axis=1, keepdims=True)
            for d in range(per):
                dz = gms[d] - jnp.exp2(lss[d]) * (gms[d] + befores[d])
                if mask:
                    dz = jnp.where(causals[d], dz, 0.0)
                dz_ref[d] = (dz * scale).astype(BF16)
            return p_lk, p_g

        def back(first, dq):
            for d, ks in enumerate(starts_of(first)):
                dzb = dz_ref[d]
                dq = dq + jnp.dot(dzb, k_ref[pl.ds(ks, bk), :], preferred_element_type=F32)
                dk_acc[pl.ds(ks, bk), :] += _dot(dzb, qv, 0, 0)
                dv_acc[pl.ds(ks, bk), :] += _dot(w_ref[d], dob, 0, 0)
            return dq

        def step(mask):
            def trip(g, state):
                dq, p_lk, p_g = state
                zs, dws = scores(g * per)
                dq = back((g - 1) * per, dq)
                return (dq,) + front(g * per, zs, dws, p_lk, p_g, mask)
            return trip

        g0 = jnp.max(lt_ref[0:1, 1:2]).astype(jnp.int32)
        zero = jnp.zeros((bq, 1), F32)
        state = (jnp.zeros((bq, HEAD_DIM), F32),) + front(g0 * per, *scores(g0 * per), zero, zero, True)
        state = lax.fori_loop(g0 + 1, i, step(False), state)
        state = lax.fori_loop(jnp.maximum(i, g0 + 1), i + 1, step(True), state)
        dqg_ref[0] = back(i * per, state[0]).astype(BF16)

        @pl.when(i == nq - 1)
        def _():
            dkv_ref[0] = dk_acc[...].astype(BF16)
            dkv_ref[1] = dv_acc[...].astype(BF16)

    qspec = pl.BlockSpec((nsub * bq, HEAD_DIM), lambda h, i: (i, h))
    kspec = pl.BlockSpec((T, HEAD_DIM), lambda h, i: (0, h))
    return _pcall(
        body, name=name, grid=(H, nq // nsub),
        in_specs=[qspec, kspec, pl.BlockSpec((T, HEAD_DIM), lambda h, i: (0, H + h)),
                  qspec, qspec, qspec, qspec],
        out_specs=[pl.BlockSpec((2, nsub * bq, HEAD_DIM), lambda h, i: (0, i, h)),
                   pl.BlockSpec((2, T, HEAD_DIM), lambda h, i: (0, 0, h))],
        out_shape=[jax.ShapeDtypeStruct((2, T, HD), BF16)] * 2,
        scratch_shapes=[pltpu.VMEM((T, HEAD_DIM), F32)] * 2 + [pltpu.VMEM((per, bq, bk), BF16)] * 2,
        compiler_params=_params("parallel", "arbitrary"))(q, kv, kv, gate, o, ltot, dog)


def _position():
    return lax.axis_index("x"), lax.axis_index("y"), lax.axis_index("c")


def _chip_of(k, x, y):
    return (1 - x if k & 1 else x), (1 - y if k & 2 else y)


class _Gather:
    @staticmethod
    def scratch(n):
        return [pltpu.SemaphoreType.DMA((n, 7)), pltpu.SemaphoreType.DMA((n, 7)),
                pltpu.SemaphoreType.DMA((n,))]

    def __init__(self, ins, outs, send_sems, recv_sems, local_sems):
        self.ins, self.outs, self.n = ins, outs, len(ins)
        self.send_sems, self.recv_sems, self.local_sems = send_sems, recv_sems, local_sems
        x, y, c = _position()
        self.me, self.sibling = (x, y, c), (x, y, 1 - c)
        self.chips = [_chip_of(k, x, y) for k in (1, 2, 3)]

    def copy(self, a, k, block, to, src=None):
        slot = self.outs[a].at[4 * block[0] + 2 * block[1] + block[2]]
        return pltpu.make_async_remote_copy(
            src_ref=slot if src is None else src, dst_ref=slot,
            send_sem=self.send_sems.at[a, k], recv_sem=self.recv_sems.at[a, k],
            device_id=to, device_id_type=MESH)

    def own_copies(self):
        x, y, c = self.me
        mine = [pltpu.make_async_copy(self.ins[a], self.outs[a].at[4 * x + 2 * y + c], self.local_sems.at[a])
                for a in range(self.n)]
        first = []
        for a in range(self.n):
            first.append(self.copy(a, 0, self.me, self.sibling, src=self.ins[a]))
            first += [self.copy(a, 1 + j, self.me, (*chip, c), src=self.ins[a])
                      for j, chip in enumerate(self.chips)]
        return mine, first

    def start(self):
        mine, first = self.own_copies()
        for cp in mine + first:
            cp.start()

    def finish(self):
        c = self.me[2]
        mine, first = self.own_copies()
        passed = []
        for j, chip in enumerate(self.chips):
            for a in range(self.n):
                self.copy(a, 1 + j, (*chip, c), self.me).wait_recv()
                fwd = self.copy(a, 4 + j, (*chip, c), self.sibling)
                fwd.start()
                passed.append(fwd)
        for a in range(self.n):
            self.copy(a, 0, self.sibling, self.me).wait_recv()
            for j, chip in enumerate(self.chips):
                self.copy(a, 4 + j, (*chip, 1 - c), self.me).wait_recv()
        for cp in first + passed:
            cp.wait_send()
        for cp in mine:
            cp.wait()


class _Rider:
    def __init__(self, inputs, out_shapes, scratch, copies):
        self.inputs, self.out_shapes, self.scratch, self.copies = inputs, out_shapes, scratch, copies

    def bind(self, ins, outs, sems):
        def start():
            for cp in self.copies(ins, outs, sems):
                cp.start()

        def finish():
            cps = self.copies(ins, outs, sems)
            for cp in cps:
                cp.wait_send()
            for cp in cps:
                cp.wait_recv()

        return start, finish


def _sibling_rider(grads):
    n = len(grads)

    def copies(ins, outs, sems):
        x, y, c = _position()
        return [pltpu.make_async_remote_copy(
            src_ref=ins[a].at[2 * chip + (1 - c)], dst_ref=outs[a].at[chip],
            send_sem=sems[0].at[a, chip], recv_sem=sems[1].at[a, chip],
            device_id=(x, y, 1 - c), device_id_type=MESH) for a in range(n) for chip in range(4)]

    return _Rider(list(grads), [jax.ShapeDtypeStruct((4,) + g.shape[1:], g.dtype) for g in grads],
                  [pltpu.SemaphoreType.DMA((n, 4)), pltpu.SemaphoreType.DMA((n, 4))], copies)


def _chips_rider(parts):
    n = len(parts)

    def copies(ins, outs, sems):
        x, y, c = _position()
        cps = []
        for a in range(n):
            for k in range(3):
                cx, cy = _chip_of(k + 1, x, y)
                cps.append(pltpu.make_async_remote_copy(
                    src_ref=ins[a].at[2 * cx + cy], dst_ref=outs[a].at[k],
                    send_sem=sems[0].at[a, k], recv_sem=sems[1].at[a, k],
                    device_id=(cx, cy, c), device_id_type=MESH))
        return cps

    return _Rider(list(parts), [jax.ShapeDtypeStruct((3,) + p.shape[1:], p.dtype) for p in parts],
                  [pltpu.SemaphoreType.DMA((n, 3)), pltpu.SemaphoreType.DMA((n, 3))], copies)


def _small_gather(small):
    def body(small_ref, small_all, send_sems, recv_sems, local_sem):
        x, y, c = _position()
        me = 4 * x + 2 * y + c
        peers = [(x ^ (m >> 2), y ^ ((m >> 1) & 1), c ^ (m & 1)) for m in range(1, N_DEV)]
        sends = [pltpu.make_async_remote_copy(
            src_ref=small_ref, dst_ref=small_all.at[me], send_sem=send_sems.at[m], recv_sem=recv_sems.at[m],
            device_id=peer, device_id_type=MESH) for m, peer in enumerate(peers)]
        own = pltpu.make_async_copy(small_ref, small_all.at[me], local_sem)
        for cp in sends + [own]:
            cp.start()
        for cp in sends:
            cp.wait_send()
        for m, (px, py, pc) in enumerate(peers):
            pltpu.make_async_remote_copy(
                src_ref=small_ref, dst_ref=small_all.at[4 * px + 2 * py + pc],
                send_sem=send_sems.at[m], recv_sem=recv_sems.at[m],
                device_id=(px, py, pc), device_id_type=MESH).wait_recv()
        own.wait()

    return _pcall(
        body, name="small_gather", in_specs=[ANY], out_specs=ANY,
        out_shape=jax.ShapeDtypeStruct((N_DEV,) + small.shape, small.dtype),
        scratch_shapes=[pltpu.SemaphoreType.DMA((7,)), pltpu.SemaphoreType.DMA((7,)),
                        pltpu.SemaphoreType.DMA])(small)


def _pair_sum(grad, got, *, name):
    _, R, C = got.shape
    tr = _pick8(R, max(2 * SUBLANES, (1 << 17) // C))

    def body(g_ref, b_ref, own_ref, ob_ref):
        north = lax.axis_index("c") == 1
        x1, y1 = lax.axis_index("x") == 1, lax.axis_index("y") == 1
        sums = []
        for chip in range(4):
            sums.append(jnp.where(north, g_ref[chip, 1], g_ref[chip, 0]) + b_ref[chip])
            ob_ref[chip] = sums[-1].astype(BF16)
        own_ref[...] = jnp.where(x1, jnp.where(y1, sums[3], sums[2]), jnp.where(y1, sums[1], sums[0]))

    spec = pl.BlockSpec((4, tr, C), lambda i: (0, i, 0))
    return _pcall(
        body, name=name, grid=(R // tr,),
        in_specs=[pl.BlockSpec((4, 2, tr, C), lambda i: (0, 0, i, 0)), spec],
        out_specs=[pl.BlockSpec((tr, C), lambda i: (i, 0)), spec],
        out_shape=[jax.ShapeDtypeStruct((R, C), F32), jax.ShapeDtypeStruct((4, R, C), BF16)],
        compiler_params=_params("parallel"))(grad.reshape(4, 2, R, C), got)


def _pick8(n, cap):
    if n <= cap:
        return n
    best = None
    for t in range(SUBLANES, cap + 1, SUBLANES):
        if n % t == 0:
            best = t
    assert best is not None, (n, cap)
    return best


def _adamw(w, m, v, parts, *, name):
    R, C = w.shape
    tr = _pick8(R, max(SUBLANES, (1 << 17) // C))
    c1 = 1.0 - ADAM_B1 ** ADAM_STEP
    c2 = 1.0 - ADAM_B2 ** ADAM_STEP
    np_ = len(parts)

    def body(w_ref, m_ref, v_ref, *refs):
        p_refs = refs[:np_]
        g_ref, d_ref, nm_ref, nv_ref = refs[np_:]
        g = None
        for p_ref in p_refs:
            terms = [p_ref[...]] if len(p_ref.shape) == 2 else [p_ref[k] for k in range(p_ref.shape[0])]
            for t in terms:
                g = t.astype(F32) if g is None else g + t.astype(F32)
        mn = ADAM_B1 * m_ref[...] + (1.0 - ADAM_B1) * g
        vn = ADAM_B2 * v_ref[...] + (1.0 - ADAM_B2) * (g * g)
        d_ref[...] = -ADAM_LR * ((mn / c1) / (jnp.sqrt(vn / c2) + ADAM_EPS) + ADAM_WD * w_ref[...])
        g_ref[...] = g
        nm_ref[...] = mn
        nv_ref[...] = vn

    spec = pl.BlockSpec((tr, C), lambda i: (i, 0))
    pspecs = [spec if p.ndim == 2 else pl.BlockSpec((p.shape[0], tr, C), lambda i: (0, i, 0)) for p in parts]
    return _pcall(
        body, name=name, grid=(R // tr,), in_specs=[spec] * 3 + pspecs, out_specs=[spec] * 4,
        out_shape=[jax.ShapeDtypeStruct((R, C), F32)] * 4,
        compiler_params=_params("parallel"))(w, m, v, *parts)


def _rows(a):
    return a.reshape(-1, LANES)


def _whole_from_columns(shards, *, name):
    S, K, n = shards.shape
    tk = _pick8(K, 1024)

    def body(s_ref, o_ref):
        o_ref[...] = s_ref[...]

    return _pcall(
        body, name=name, grid=(K // tk, S),
        in_specs=[pl.BlockSpec((None, tk, n), lambda i, s: (s, i, 0))],
        out_specs=pl.BlockSpec((tk, n), lambda i, s: (i, s)),
        out_shape=jax.ShapeDtypeStruct((K, S * n), shards.dtype),
        compiler_params=_params("parallel", "parallel"))(shards)


def _late_weights(a_w_out_rows, w_kv_cols, b_w_in_cols, b_w_out_rows):
    whole_rows = lambda g: g.reshape(g.shape[0] * g.shape[1], g.shape[2])
    return (whole_rows(a_w_out_rows), _whole_from_columns(w_kv_cols, name="w_kv_whole"),
            _whole_from_columns(b_w_in_cols, name="b_w_in_whole"), whole_rows(b_w_out_rows))


def _forward_backward(xs, target, a_norm, g_a_w_in, conv_w, conv_b, g_w_r, g_w_i, b_r, b_i, lam,
                      kv_norm, b_norm, final_norm, *, late_weights=None, late_shards=None, h_a=None):
    if h_a is None:
        (h_a,) = _rms_fwd(xs, [a_norm], name="a_norm_fwd")
    proj_a = _mm_nn(h_a, g_a_w_in, name="a_in_proj", out_dtype=F32)
    xb, h_rec, yg, *gathered = _acore_fwd(proj_a, conv_w, conv_b, g_w_r, g_w_i, b_r, b_i, lam,
                                          name="a_core_fwd", riders=late_shards or ())
    g_a_w_out, g_w_kv, g_b_w_in, g_b_w_out = _late_weights(*gathered) if late_shards else late_weights
    x1 = _mm_nn(yg, g_a_w_out, name="a_out_proj", out_dtype=F32, res=xs)
    hk, hb = _rms_fwd(x1, [kv_norm, b_norm], name="kv_b_norm_fwd")
    kv = _mm_nn(hk, g_w_kv, name="kv_proj", out_dtype=BF16)
    hd = g_b_w_in.shape[1] // 2
    q = _mm_nn(hb, g_b_w_in, name="q_proj", out_dtype=BF16, col_off=0, cols=hd)
    gate_b = _mm_nn(hb, g_b_w_in, name="b_gate_proj", out_dtype=F32, col_off=hd, cols=hd)
    o, og, ltot = _attn_fwd(q, kv, gate_b, name="attn_fwd")
    x2 = _mm_nn(og, g_b_w_out, name="b_out_proj", out_dtype=F32, res=x1)
    dx2, d_final_norm, loss_part = _final_loss(x2, target, final_norm, name="final_norm_loss")

    dog = _mm_nt(dx2, g_b_w_out, name="b_out_proj_bwd")
    dw_b_out = _mm_tn(og, dx2, name="b_out_proj_wgrad")
    dproj_b, dkv = _attn_bwd(q, kv, gate_b, o, ltot, dog, name="attn_bwd")
    dhb = _mm_nt(dproj_b, g_b_w_in, name="b_in_proj_bwd")
    dw_b_in = _mm_tn(hb, dproj_b, name="b_in_proj_wgrad", shards=N_DEV)
    dhk = _mm_nt(dkv, g_w_kv, name="kv_proj_bwd")
    dw_kv = _mm_tn(hk, dkv, name="kv_proj_wgrad", shards=N_DEV)
    early = [dw_kv, dw_b_in, dw_b_out.reshape(N_DEV, -1, dw_b_out.shape[1])] if late_shards else []
    dx1, d_b_norm, d_kv_norm, *got = _rms_bwd(x1, dx2, [dhb, dhk], [b_norm, kv_norm], name="kv_b_norm_bwd",
                                              rider=_sibling_rider(early) if early else None)
    early_sums = [_pair_sum(f_, g_, name=f"pair_sum_early_{i}") for i, (f_, g_) in enumerate(zip(early, got))]
    dyg = _mm_nt(dx1, g_a_w_out, name="a_out_proj_bwd")
    dw_a_out = _mm_tn(yg, dx1, name="a_out_proj_wgrad")
    (dproj_a, d_conv_w, d_conv_b, d_b_r, d_b_i, d_lambda, dw_r, dw_i, *early_others) = _acore_bwd(
        dyg, proj_a, xb, h_rec, conv_w, g_w_r, g_w_i, b_r, b_i, lam, name="a_core_bwd",
        rider=_chips_rider([s[1] for s in early_sums]) if early else None)
    dw_a_in = _mm_tn(h_a, dproj_a, name="a_in_proj_wgrad", shards=N_DEV)
    rows = dw_r.shape[1] // N_DEV
    lru = lambda dw: dw.reshape(-1, N_DEV, rows, dw.shape[2]).transpose(1, 0, 2, 3).reshape(N_DEV, -1, dw.shape[2])
    late = [dw_a_in, dw_a_out.reshape(N_DEV, -1, dw_a_out.shape[1]), lru(dw_r), lru(dw_i)] if late_shards else []
    dh_a, *got = _mm_nt(dproj_a, g_a_w_in, name="a_in_proj_bwd", rider=_sibling_rider(late)) if late else (
        _mm_nt(dproj_a, g_a_w_in, name="a_in_proj_bwd"),)
    late_sums = [_pair_sum(f_, g_, name=f"pair_sum_late_{i}") for i, (f_, g_) in enumerate(zip(late, got))]
    grad_x, d_a_norm, *late_others = _rms_bwd(xs, dx1, [dh_a], [a_norm], name="a_norm_bwd",
                                              rider=_chips_rider([s[1] for s in late_sums]) if late else None)
    sums = late_sums[:2] + early_sums + late_sums[2:]
    others = late_others[:2] + early_others + late_others[2:]
    return (loss_part, grad_x, dw_a_in, dw_a_out, dw_kv, dw_b_in, dw_b_out, dw_r, dw_i, d_a_norm,
            d_conv_w, d_conv_b, d_b_r, d_b_i, d_lambda, d_kv_norm, d_b_norm, d_final_norm, sums, others)


def kernel(x, a_norm, a_w_in, a_conv_w, a_conv_b, a_w_r, a_b_r, a_w_i, a_b_i, a_lambda, a_w_out, kv_norm, w_kv, b_norm, b_w_in, b_w_out, final_norm, loss_target, m_a_norm, m_a_w_in, m_a_conv_w, m_a_conv_b, m_a_w_r, m_a_b_r, m_a_w_i, m_a_b_i, m_a_lambda, m_a_w_out, m_kv_norm, m_w_kv, m_b_norm, m_b_w_in, m_b_w_out, m_final_norm, v_a_norm, v_a_w_in, v_a_conv_w, v_a_conv_b, v_a_w_r, v_a_b_r, v_a_w_i, v_a_b_i, v_a_lambda, v_a_w_out, v_kv_norm, v_w_kv, v_b_norm, v_b_w_in, v_b_w_out, v_final_norm):
    T, D = x.shape[1], x.shape[2]
    nb, bw = a_w_r.shape[1], a_w_r.shape[3]
    C = nb * bw
    me = 4 * lax.axis_index("x") + 2 * lax.axis_index("y") + lax.axis_index("c")
    xs = x[0]
    target = loss_target[0]

    rows_r = a_w_r.shape[2]
    small_f32 = jnp.concatenate([_rows(a_conv_w[0]), _rows(b_norm[0])], axis=0)
    pad = (-small_f32.shape[0]) % SUBLANES
    small_f32 = jnp.pad(small_f32, ((0, pad), (0, 0)))
    h_a, a_w_in_cols, w_r_rows, w_i_rows, small_all = _rms_fwd(
        xs, [a_norm], name="a_norm_fwd",
        riders=[a_w_in[0].astype(BF16), a_w_r[0].reshape(nb * rows_r, bw).astype(BF16),
                a_w_i[0].reshape(nb * rows_r, bw).astype(BF16), small_f32])
    late_shards = [a_w_out[0].astype(BF16), w_kv.astype(BF16), b_w_in[0].astype(BF16), b_w_out[0].astype(BF16)]
    g_a_w_in = _whole_from_columns(a_w_in_cols, name="a_w_in_whole")
    g_w_r = w_r_rows.reshape(N_DEV, nb, rows_r, bw).transpose(1, 0, 2, 3).reshape(nb, bw, bw)
    g_w_i = w_i_rows.reshape(N_DEV, nb, rows_r, bw).transpose(1, 0, 2, 3).reshape(nb, bw, bw)
    cw_rows = a_conv_w.shape[1] * a_conv_w.shape[2] // LANES
    conv_w_full = small_all[:, :cw_rows, :].reshape(N_DEV, CONV_W, a_conv_w.shape[2])
    conv_w_full = conv_w_full.transpose(1, 0, 2).reshape(CONV_W, C)
    bn_rows = b_norm.shape[1] // LANES
    b_norm_full = small_all[:, cw_rows:cw_rows + bn_rows, :].reshape(1, D)
    kv_norm2, final_norm2 = kv_norm.reshape(1, D), final_norm.reshape(1, D)

    (loss_part, grad_x, dw_a_in, dw_a_out, dw_kv, dw_b_in, dw_b_out, dw_r, dw_i, d_a_norm, d_conv_w,
     d_conv_b, d_b_r, d_b_i, d_lambda, d_kv_norm, d_b_norm, d_final_norm, sums,
     others) = _forward_backward(
         xs, target, a_norm, g_a_w_in, conv_w_full, a_conv_b, g_w_r, g_w_i, a_b_r, a_b_i, a_lambda,
         kv_norm2, b_norm_full, final_norm2, late_shards=late_shards, h_a=h_a)

    small_parts = [d_a_norm, d_conv_w, d_conv_b, d_b_r, d_b_i, d_lambda, d_kv_norm, d_b_norm, d_final_norm]
    small_sizes = [p.size // LANES for p in small_parts]
    small = jnp.concatenate([_rows(p) for p in small_parts], axis=0)
    small_everyone = _small_gather(small)

    def shard2d(w):
        return w.reshape(-1, w.shape[-1])

    names_big = [(a_w_in, m_a_w_in, v_a_w_in), (a_w_out, m_a_w_out, v_a_w_out), (w_kv, m_w_kv, v_w_kv),
                 (b_w_in, m_b_w_in, v_b_w_in), (b_w_out, m_b_w_out, v_b_w_out),
                 (a_w_r, m_a_w_r, v_a_w_r), (a_w_i, m_a_w_i, v_a_w_i)]
    upd_big = []
    for i, (w, m, v) in enumerate(names_big):
        res = _adamw(shard2d(w), shard2d(m), shard2d(v), [sums[i][0], others[i]], name=f"adamw_{i}")
        upd_big.append([r.reshape(w.shape) for r in res])

    soffs = [0]
    for s in small_sizes:
        soffs.append(soffs[-1] + s)

    def small_piece(i):
        return small_everyone[:, soffs[i]:soffs[i + 1], :]

    cw_cols = a_conv_w.shape[2]
    conv_piece = small_piece(1).reshape(N_DEV, CONV_W, C)
    conv_piece = lax.dynamic_slice_in_dim(conv_piece, me * cw_cols, cw_cols, axis=2)
    conv_piece = conv_piece.reshape(N_DEV, CONV_W * cw_cols // LANES, LANES)
    bn_piece = lax.dynamic_slice_in_dim(small_piece(7), me * bn_rows, bn_rows, axis=1)
    small_g = jnp.concatenate([small_piece(0), conv_piece, small_piece(2), small_piece(3), small_piece(4),
                               small_piece(5), small_piece(6), bn_piece, small_piece(8)], axis=1)
    small_w = [(a_norm, m_a_norm, v_a_norm), (a_conv_w, m_a_conv_w, v_a_conv_w),
               (a_conv_b, m_a_conv_b, v_a_conv_b), (a_b_r, m_a_b_r, v_a_b_r), (a_b_i, m_a_b_i, v_a_b_i),
               (a_lambda, m_a_lambda, v_a_lambda), (kv_norm, m_kv_norm, v_kv_norm),
               (b_norm, m_b_norm, v_b_norm), (final_norm, m_final_norm, v_final_norm)]
    pack = lambda idx: jnp.concatenate([_rows(t[idx]) for t in small_w], axis=0)
    res_small = _adamw(pack(0), pack(1), pack(2), [small_g], name="adamw_small")
    woffs = [0]
    for t in small_w:
        woffs.append(woffs[-1] + t[0].size // LANES)
    upd_small = [[r[woffs[i]:woffs[i + 1]].reshape(small_w[i][0].shape) for r in res_small]
                 for i in range(len(small_w))]

    order = [("s", 0), ("b", 0), ("s", 1), ("s", 2), ("b", 5), ("s", 3), ("b", 6), ("s", 4), ("s", 5),
             ("b", 1), ("s", 6), ("b", 2), ("s", 7), ("b", 3), ("b", 4), ("s", 8)]
    per_weight = [(upd_big if kind == "b" else upd_small)[i] for kind, i in order]
    loss = lax.psum(loss_part[0, 0], ("x", "y", "c"))
    result = [loss, grad_x[None]]
    for field in range(4):
        result += [u[field] for u in per_weight]
    return tuple(result)
```

```python
import math

import jax
import jax.numpy as jnp
from jax import lax
from jax.experimental import pallas as pl
from jax.experimental.pallas import tpu as pltpu

F32 = jnp.float32
BF16 = jnp.bfloat16
MESH = pl.DeviceIdType.MESH

EPS = 1e-6
LOG2E = 1.4426950408889634
WEIGHT_FLOOR_LOG2 = -200.0
LRU_C = 8.0
CONV_W = 4
HEAD_DIM = 128
ADAM_LR = 0.001
ADAM_B1 = 0.9
ADAM_B2 = 0.999
ADAM_EPS = 1e-08
ADAM_WD = 0.01
ADAM_STEP = 10

N_DEV = 8
LANES = 128
SUBLANES = 8
VMEM_LIMIT = 56 * 1024 * 1024

ATT_KEY_BLOCK = 256
ATT_QUERY_BLOCK = 256
ATT_STEP_BLOCKS = 4
SCAN_BLOCK = 256
ROW_BLOCK = 256
MM_TOKEN_BLOCK = 1024
MM_WIDE_K = 4096
MM_WEIGHT_TILE = 1280
MM_CONTRACT_TOKENS = 2048
ANY = pl.BlockSpec(memory_space=pl.ANY)


def _pcall(body, **kw):
    return pl.pallas_call(body, **kw)


def _params(*sem):
    return pltpu.CompilerParams(dimension_semantics=sem, vmem_limit_bytes=VMEM_LIMIT)


def _pick(n, cap):
    if n <= cap:
        return n
    best = None
    for t in range(LANES, cap + 1, LANES):
        if n % t == 0:
            best = t
    assert best is not None, (n, cap)
    return best


def _token_block(T, K):
    tm = MM_TOKEN_BLOCK if K <= MM_WIDE_K else MM_TOKEN_BLOCK // 2
    return tm if T % tm == 0 else T


def _sigmoid(x):
    return 1.0 / (1.0 + jnp.exp(-x))


def _dot(a, b, ca, cb):
    return lax.dot_general(a, b, (((ca,), (cb,)), ((), ())), preferred_element_type=F32)


def _mm_nn(a, w, *, name, out_dtype, col_off=0, cols=None, res=None):
    T, K = a.shape
    K2, N = w.shape
    assert K == K2
    cols = N if cols is None else cols
    tm = _token_block(T, K)
    tn = _pick(cols, MM_WEIGHT_TILE)
    assert col_off % tn == 0
    off = col_off // tn
    has_res = res is not None

    def body(a_ref, b_ref, *rest):
        o_ref = rest[-1]
        acc = jnp.dot(a_ref[...].astype(BF16), b_ref[...], preferred_element_type=F32)
        if has_res:
            acc = acc + rest[0][...]
        o_ref[...] = acc.astype(out_dtype)

    in_specs = [pl.BlockSpec((tm, K), lambda j, i: (i, 0)),
                pl.BlockSpec((K, tn), lambda j, i: (0, off + j))]
    args = [a, w]
    if has_res:
        in_specs.append(pl.BlockSpec((tm, tn), lambda j, i: (i, j)))
        args.append(res)
    return _pcall(
        body, name=name, grid=(cols // tn, T // tm), in_specs=in_specs,
        out_specs=pl.BlockSpec((tm, tn), lambda j, i: (i, j)),
        out_shape=jax.ShapeDtypeStruct((T, cols), out_dtype),
        compiler_params=_params("parallel", "parallel"))(*args)


def _mm_nt(a, w, *, name, out_dtype=F32, rider=None):
    parts = a.shape[0] if a.ndim == 3 else 1
    T, kp = a.shape[-2:]
    N, K = w.shape
    assert K == parts * kp
    tm = _token_block(T, K)
    tn = _pick(N, MM_WEIGHT_TILE)
    nj, ni = N // tn, T // tm
    rider = rider or _Rider([], [], [], None)
    nri, nro = len(rider.inputs), len(rider.out_shapes)

    def body(a_ref, b_ref, *refs):
        o_ref = refs[nri]
        j, i = pl.program_id(0), pl.program_id(1)
        if nro:
            start, finish = rider.bind(refs[:nri], refs[nri + 1:nri + 1 + nro], refs[nri + 1 + nro:])
            pl.when((j == 0) & (i == 0))(start)
        if a.ndim == 3:
            acc = None
            for p in range(parts):
                term = _dot(a_ref[p], b_ref[:, p * kp:(p + 1) * kp], 1, 1)
                acc = term if acc is None else acc + term
        else:
            acc = _dot(a_ref[...].astype(BF16), b_ref[...], 1, 1)
        o_ref[...] = acc.astype(out_dtype)
        if nro:
            pl.when((j == nj - 1) & (i == ni - 1))(finish)

    a_spec = (pl.BlockSpec((parts, tm, kp), lambda j, i: (0, i, 0)) if a.ndim == 3
              else pl.BlockSpec((tm, K), lambda j, i: (i, 0)))
    sem = ("arbitrary", "arbitrary") if nro else ("parallel", "parallel")
    out = _pcall(
        body, name=name, grid=(nj, ni),
        in_specs=[a_spec, pl.BlockSpec((tn, K), lambda j, i: (j, 0))] + [ANY] * nri,
        out_specs=[pl.BlockSpec((tm, tn), lambda j, i: (i, j))] + [ANY] * nro,
        out_shape=[jax.ShapeDtypeStruct((T, N), out_dtype)] + rider.out_shapes,
        scratch_shapes=rider.scratch,
        compiler_params=_params(*sem))(a, w, *rider.inputs)
    return out if nro else out[0]


def _mm_tn(a, b, *, name, shards=1):
    T, Ko = a.shape
    parts = b.shape[0] if b.ndim == 3 else 1
    T2, n_part = b.shape[-2:]
    N = parts * n_part
    assert T == T2
    n = N // shards
    tt = min(T, MM_CONTRACT_TOKENS)
    tko = _pick(Ko, 1024)
    span = 2 if shards > 1 and 2 * n <= MM_WEIGHT_TILE and n_part % (2 * n) == 0 else 1
    tn = span * n if span > 1 else _pick(n, 1024)
    per = max(n // tn, 1)
    assert n_part % tn == 0
    per_part = n_part // tn

    def body(a_ref, b_ref, o_ref):
        t = pl.program_id(2)
        p = _dot(a_ref[...].astype(BF16), b_ref[...].astype(BF16), 0, 0)
        pieces = [p] if span == 1 else [p[:, s * n:(s + 1) * n] for s in range(span)]

        @pl.when(t == 0)
        def _():
            for s, piece in enumerate(pieces):
                o_ref[(s,) if span > 1 else ...] = piece

        @pl.when(t > 0)
        def _():
            for s, piece in enumerate(pieces):
                o_ref[(s,) if span > 1 else ...] += piece

    if shards == 1:
        out_spec = pl.BlockSpec((tko, tn), lambda i, j, t: (i, j))
        out_shape = jax.ShapeDtypeStruct((Ko, N), F32)
    elif span > 1:
        out_spec = pl.BlockSpec((span, tko, n), lambda i, j, t: (j, i, 0))
        out_shape = jax.ShapeDtypeStruct((shards, Ko, n), F32)
    else:
        out_spec = pl.BlockSpec((None, tko, tn), lambda i, j, t: (j // per, i, j % per))
        out_shape = jax.ShapeDtypeStruct((shards, Ko, n), F32)
    b_spec = (pl.BlockSpec((None, tt, tn), lambda i, j, t: (j // per_part, t, j % per_part)) if b.ndim == 3
              else pl.BlockSpec((tt, tn), lambda i, j, t: (t, j)))
    return _pcall(
        body, name=name, grid=(Ko // tko, N // tn, T // tt),
        in_specs=[pl.BlockSpec((tt, tko), lambda i, j, t: (t, i)), b_spec],
        out_specs=out_spec, out_shape=out_shape,
        compiler_params=_params("parallel", "parallel", "arbitrary"))(a, b)


def _rms_fwd(x, gains, *, name, riders=()):
    T, D = x.shape
    tm = min(T, ROW_BLOCK)
    steps = T // tm
    n, nr = len(gains), len(riders)

    def body(x_ref, *refs):
        g_refs, rider_in = refs[:n], refs[n:n + nr]
        o_refs, rider_out = refs[n + nr:2 * n + nr], refs[2 * n + nr:2 * n + 2 * nr]
        if nr:
            gather = _Gather(rider_in, rider_out, *refs[2 * n + 2 * nr:])
            pl.when(pl.program_id(0) == 0)(gather.start)
        xv = x_ref[...]
        xh = xv * lax.rsqrt(jnp.mean(xv * xv, axis=-1, keepdims=True) + EPS)
        for g_ref, o_ref in zip(g_refs, o_refs):
            o_ref[...] = (xh * g_ref[...]).astype(BF16)
        if nr:
            pl.when(pl.program_id(0) == steps - 1)(gather.finish)

    row = pl.BlockSpec((tm, D), lambda i: (i, 0))
    vec = pl.BlockSpec((1, D), lambda i: (0, 0))
    return _pcall(
        body, name=name, grid=(steps,), in_specs=[row] + [vec] * n + [ANY] * nr,
        out_specs=[row] * n + [ANY] * nr,
        out_shape=[jax.ShapeDtypeStruct((T, D), BF16)] * n
                  + [jax.ShapeDtypeStruct((N_DEV,) + r.shape, r.dtype) for r in riders],
        scratch_shapes=_Gather.scratch(nr) if nr else [],
        compiler_params=_params("arbitrary" if nr else "parallel"))(x, *gains, *riders)


def _rms_bwd(x, dres, dhs, gains, *, name, rider=None):
    T, D = x.shape
    tm = min(T, ROW_BLOCK)
    steps = T // tm
    n = len(gains)
    rider = rider or _Rider([], [], [], None)
    nri, nro = len(rider.inputs), len(rider.out_shapes)

    def body(x_ref, dres_ref, *refs):
        dh_refs, g_refs = refs[:n], refs[n:2 * n]
        refs = refs[2 * n:]
        rider_in, refs = refs[:nri], refs[nri:]
        dx_ref, dg_refs = refs[0], refs[1:1 + n]
        rider_out, sems = refs[1 + n:1 + n + nro], refs[1 + n + nro:]
        i = pl.program_id(0)
        if nro:
            start, finish = rider.bind(rider_in, rider_out, sems)
            pl.when(i == 0)(start)
        xv = x_ref[...]
        r = lax.rsqrt(jnp.mean(xv * xv, axis=-1, keepdims=True) + EPS)
        xh = xv * r
        dxh = jnp.zeros_like(xv)
        for dh_ref, g_ref, dg_ref in zip(dh_refs, g_refs, dg_refs):
            dh = dh_ref[...]
            part = jnp.sum(dh * xh, axis=0, keepdims=True)

            @pl.when(i == 0)
            def _():
                dg_ref[...] = part

            @pl.when(i > 0)
            def _():
                dg_ref[...] += part

            dxh = dxh + dh * g_ref[...]
        dx_ref[...] = dres_ref[...] + r * (dxh - xh * jnp.mean(dxh * xh, axis=-1, keepdims=True))
        if nro:
            pl.when(i == steps - 1)(finish)

    row = pl.BlockSpec((tm, D), lambda i: (i, 0))
    vec = pl.BlockSpec((1, D), lambda i: (0, 0))
    return _pcall(
        body, name=name, grid=(steps,), in_specs=[row, row] + [row] * n + [vec] * n + [ANY] * nri,
        out_specs=[row] + [vec] * n + [ANY] * nro,
        out_shape=[jax.ShapeDtypeStruct((T, D), F32)] + [jax.ShapeDtypeStruct((1, D), F32)] * n
                  + rider.out_shapes,
        scratch_shapes=rider.scratch,
        compiler_params=_params("arbitrary"))(x, dres, *dhs, *gains, *rider.inputs)


def _final_loss(x, target, gain, *, name):
    T, D = x.shape
    tm = min(T, ROW_BLOCK)

    def body(x_ref, t_ref, g_ref, dx_ref, dg_ref, loss_ref):
        i = pl.program_id(0)
        xv = x_ref[...]
        g = g_ref[...]
        r = lax.rsqrt(jnp.mean(xv * xv, axis=-1, keepdims=True) + EPS)
        xh = xv * r
        err = xh * g - t_ref[...]
        part_loss = 0.5 * jnp.sum(jnp.mean(err * err, axis=-1, keepdims=True), axis=0, keepdims=True)
        dy = err * (1.0 / D)
        part_g = jnp.sum(dy * xh, axis=0, keepdims=True)

        @pl.when(i == 0)
        def _():
            dg_ref[...] = part_g
            loss_ref[...] = jnp.broadcast_to(part_loss, loss_ref.shape)

        @pl.when(i > 0)
        def _():
            dg_ref[...] += part_g
            loss_ref[...] += jnp.broadcast_to(part_loss, loss_ref.shape)

        dxh = dy * g
        dx_ref[...] = r * (dxh - xh * jnp.mean(dxh * xh, axis=-1, keepdims=True))

    row = pl.BlockSpec((tm, D), lambda i: (i, 0))
    vec = pl.BlockSpec((1, D), lambda i: (0, 0))
    return _pcall(
        body, name=name, grid=(T // tm,), in_specs=[row, row, vec],
        out_specs=[row, vec, pl.BlockSpec((1, LANES), lambda i: (0, 0))],
        out_shape=[jax.ShapeDtypeStruct((T, D), F32), jax.ShapeDtypeStruct((1, D), F32),
                   jax.ShapeDtypeStruct((1, LANES), F32)],
        compiler_params=_params("arbitrary"))(x, target, gain)


def _shift_down(x, prev_tail, j, row):
    tb = x.shape[0]
    prev = jnp.tile(prev_tail, (tb // SUBLANES, 1))
    return jnp.where(row >= j, pltpu.roll(x, j, 0), pltpu.roll(prev, j, 0))


def _shift_up(x, next_head, j, row):
    tb = x.shape[0]
    nxt = jnp.tile(next_head, (tb // SUBLANES, 1))
    return jnp.where(row < tb - j, pltpu.roll(x, tb - j, 0), pltpu.roll(nxt, tb - j, 0))


def _lru_gates(xb, wr, wi, br, bi, lam):
    xbb = xb.astype(BF16)
    r = _sigmoid(jnp.dot(xbb, wr, preferred_element_type=F32) + br)
    i = _sigmoid(jnp.dot(xbb, wi, preferred_element_type=F32) + bi)
    sp = jnp.maximum(-lam, 0.0) + jnp.log1p(jnp.exp(-jnp.abs(lam)))
    log_a = (-LRU_C) * r * sp
    a = jnp.exp(log_a)
    a2 = a * a
    mult = jnp.sqrt(jnp.maximum(-jnp.tanh(log_a) * (1.0 + a2), 0.0))
    return xbb, r, i, sp, a, a2, mult


def _scan_rows(coef, val, edge, reverse):
    tb, C = coef.shape
    a, b = coef, val
    row = lax.broadcasted_iota(jnp.int32, (tb, C), 0)
    s = 1
    while s < tb:
        m = (row < tb - s) if reverse else (row >= s)
        shift = tb - s if reverse else s
        b = jnp.where(m, a * pltpu.roll(b, shift, 0) + b, b)
        a = jnp.where(m, a * pltpu.roll(a, shift, 0), a)
        s *= 2
    return b + a * edge


def _acore_fwd(proj, conv_w, conv_b, w_r, w_i, b_r, b_i, lam, *, name, riders=()):
    T, C2 = proj.shape
    C = C2 // 2
    nb, bw, _ = w_r.shape
    tb = min(T, SCAN_BLOCK)
    nt = T // tb
    nr = len(riders)

    def body(xp_ref, gate_ref, cw_ref, cb_ref, wr_ref, wi_ref, br_ref, bi_ref, lam_ref, *refs):
        rider_in, refs = refs[:nr], refs[nr:]
        xb_ref, h_ref, yg_ref = refs[:3]
        rider_out, refs = refs[3:3 + nr], refs[3 + nr:]
        tail_ref, hlast_ref = refs[:2]
        t = pl.program_id(1)
        if nr:
            gather = _Gather(rider_in, rider_out, *refs[2:])
            pl.when((pl.program_id(0) == 0) & (t == 0))(gather.start)

        @pl.when(t == 0)
        def _():
            tail_ref[...] = jnp.zeros_like(tail_ref)
            hlast_ref[...] = jnp.zeros_like(hlast_ref)

        row = lax.broadcasted_iota(jnp.int32, (tb, bw), 0)
        xp = xp_ref[...]
        tail = tail_ref[...]
        xb = cb_ref[...] + cw_ref[CONV_W - 1:CONV_W, :] * xp
        for j in range(1, CONV_W):
            xb = xb + cw_ref[CONV_W - 1 - j:CONV_W - j, :] * _shift_down(xp, tail, j, row)
        tail_ref[...] = xp[tb - SUBLANES:, :]
        xb_ref[...] = xb

        _, r, i, sp, a, a2, mult = _lru_gates(xb, wr_ref[...], wi_ref[...], br_ref[...], bi_ref[...],
                                              lam_ref[...])
        h = _scan_rows(a, mult * (i * xb), hlast_ref[SUBLANES - 1:SUBLANES, :], False)
        hlast_ref[...] = h[tb - SUBLANES:, :]
        h_ref[...] = h
        gate = gate_ref[...]
        yg_ref[...] = (h * (gate * _sigmoid(gate))).astype(BF16)
        if nr:
            pl.when((pl.program_id(0) == nb - 1) & (t == nt - 1))(gather.finish)

    blk = lambda off: pl.BlockSpec((tb, bw), lambda n, t: (t, off + n))
    vec = pl.BlockSpec((1, bw), lambda n, t: (0, n))
    wspec = pl.BlockSpec((None, bw, bw), lambda n, t: (n, 0, 0))
    return _pcall(
        body, name=name, grid=(nb, nt),
        in_specs=[blk(0), blk(nb), pl.BlockSpec((CONV_W, bw), lambda n, t: (0, n)), vec, wspec, wspec,
                  vec, vec, vec] + [ANY] * nr,
        out_specs=[blk(0), blk(0), blk(0)] + [ANY] * nr,
        out_shape=[jax.ShapeDtypeStruct((T, C), F32), jax.ShapeDtypeStruct((T, C), F32),
                   jax.ShapeDtypeStruct((T, C), BF16)]
                  + [jax.ShapeDtypeStruct((N_DEV,) + r.shape, r.dtype) for r in riders],
        scratch_shapes=[pltpu.VMEM((SUBLANES, bw), F32), pltpu.VMEM((SUBLANES, bw), F32)]
                       + (_Gather.scratch(nr) if nr else []),
        compiler_params=_params("arbitrary" if nr else "parallel", "arbitrary"))(
            proj, proj, conv_w, conv_b, w_r, w_i, b_r, b_i, lam, *riders)


def _acore_bwd(dyg, proj, xb_all, h_all, conv_w, w_r, w_i, b_r, b_i, lam, *, name, rider=None):
    T, C2 = proj.shape
    C = C2 // 2
    nb, bw, _ = w_r.shape
    tb = min(T, SCAN_BLOCK)
    nt = T // tb
    per8 = tb // SUBLANES
    rider = rider or _Rider([], [], [], None)
    nri, nro = len(rider.inputs), len(rider.out_shapes)

    def body(dyg_ref, xp_ref, gate_ref, xb_ref, h_ref, xp_prev_ref, h_prev_ref, cw_ref,
             wr_ref, wi_ref, br_ref, bi_ref, lam_ref, *refs):
        rider_in, refs = refs[:nri], refs[nri:]
        dproj_ref, dcw_ref, dcb_ref, dbr_ref, dbi_ref, dlam_ref, dwr_ref, dwi_ref = refs[:8]
        rider_out, refs = refs[8:8 + nro], refs[8 + nro:]
        gh_next_ref, a_next_ref, dxb_next_ref = refs[:3]
        step = pl.program_id(1)
        first_block = step == nt - 1
        if nro:
            start, finish = rider.bind(rider_in, rider_out, refs[3:])
            pl.when((pl.program_id(0) == 0) & (step == 0))(start)

        @pl.when(step == 0)
        def _():
            gh_next_ref[...] = jnp.zeros_like(gh_next_ref)
            a_next_ref[...] = jnp.zeros_like(a_next_ref)
            dxb_next_ref[...] = jnp.zeros_like(dxb_next_ref)

        row = lax.broadcasted_iota(jnp.int32, (tb, bw), 0)
        keep = jnp.where(first_block, 0.0, 1.0)
        h_prev = h_prev_ref[...] * keep
        xp_prev = xp_prev_ref[...] * keep
        xp, gate, xb, h, dyg_v = xp_ref[...], gate_ref[...], xb_ref[...], h_ref[...], dyg_ref[...]
        lam_v = lam_ref[...]
        wr, wi = wr_ref[...], wi_ref[...]

        sg = _sigmoid(gate)
        dh = dyg_v * (gate * sg)
        dproj_ref[1] = (dyg_v * h * (sg * (1.0 + gate * (1.0 - sg)))).astype(BF16)

        xbb, r, i, sp, a, a2, mult = _lru_gates(xb, wr, wi, br_ref[...], bi_ref[...], lam_v)

        gh = _scan_rows(_shift_up(a, a_next_ref[...], 1, row), dh, gh_next_ref[0:1, :], True)
        gh_next_ref[...] = gh[0:SUBLANES, :]
        a_next_ref[...] = a[0:SUBLANES, :]

        da = gh * _shift_down(h, h_prev, 1, row)
        dmult = gh * (i * xb)
        di = gh * mult * xb
        dxb = gh * mult * i
        dla = da * a - dmult * jnp.where(mult > 0.0, a2 / mult, 0.0)
        dr = dla * ((-LRU_C) * sp)
        dsp = jnp.sum(dla * ((-LRU_C) * r), axis=0, keepdims=True)
        dlam_part = dsp * (-_sigmoid(-lam_v))
        dpr = dr * r * (1.0 - r)
        dpi = di * i * (1.0 - i)
        dbr_part = jnp.sum(dpr, axis=0, keepdims=True)
        dbi_part = jnp.sum(dpi, axis=0, keepdims=True)
        dprb, dpib = dpr.astype(BF16), dpi.astype(BF16)
        dwr_part = _dot(xbb, dprb, 0, 0)
        dwi_part = _dot(xbb, dpib, 0, 0)
        dxb = dxb + _dot(dprb, wr, 1, 1) + _dot(dpib, wi, 1, 1)

        dxb_next = dxb_next_ref[...]
        dxp = cw_ref[CONV_W - 1:CONV_W, :] * dxb
        for j in range(1, CONV_W):
            dxp = dxp + cw_ref[CONV_W - 1 - j:CONV_W - j, :] * _shift_up(dxb, dxb_next, j, row)
        dxb_next_ref[...] = dxb[0:SUBLANES, :]
        dproj_ref[0] = dxp.astype(BF16)
        dcb_part = jnp.sum(dxb, axis=0, keepdims=True)
        dcw_rows = []
        for k in range(CONV_W):
            j = CONV_W - 1 - k
            sh = xp if j == 0 else _shift_down(xp, xp_prev, j, row)
            dcw_rows.append(jnp.sum(dxb * sh, axis=0, keepdims=True))

        @pl.when(step == 0)
        def _():
            for k in range(CONV_W):
                dcw_ref[k:k + 1, :] = dcw_rows[k]
            dcb_ref[...] = dcb_part
            dbr_ref[...] = dbr_part
            dbi_ref[...] = dbi_part
            dlam_ref[...] = dlam_part
            dwr_ref[...] = dwr_part
            dwi_ref[...] = dwi_part

        @pl.when(step > 0)
        def _():
            for k in range(CONV_W):
                dcw_ref[k:k + 1, :] += dcw_rows[k]
            dcb_ref[...] += dcb_part
            dbr_ref[...] += dbr_part
            dbi_ref[...] += dbi_part
            dlam_ref[...] += dlam_part
            dwr_ref[...] += dwr_part
            dwi_ref[...] += dwi_part

        if nro:
            pl.when((pl.program_id(0) == nb - 1) & (step == nt - 1))(finish)

    rev = lambda s: nt - 1 - s
    blk = lambda off: pl.BlockSpec((tb, bw), lambda n, s: (rev(s), off + n))
    prev8 = lambda off: pl.BlockSpec(
        (SUBLANES, bw), lambda n, s: (jnp.maximum(rev(s) * per8 - 1, 0), off + n))
    vec = pl.BlockSpec((1, bw), lambda n, s: (0, n))
    wspec = pl.BlockSpec((None, bw, bw), lambda n, s: (n, 0, 0))
    cwspec = pl.BlockSpec((CONV_W, bw), lambda n, s: (0, n))
    vshape = jax.ShapeDtypeStruct((1, C), F32)
    wshape = jax.ShapeDtypeStruct((nb, bw, bw), F32)
    return _pcall(
        body, name=name, grid=(nb, nt),
        in_specs=[blk(0), blk(0), blk(nb), blk(0), blk(0), prev8(0), prev8(0), cwspec,
                  wspec, wspec, vec, vec, vec] + [ANY] * nri,
        out_specs=[pl.BlockSpec((2, tb, bw), lambda n, s: (0, rev(s), n)), cwspec, vec, vec, vec, vec,
                   wspec, wspec] + [ANY] * nro,
        out_shape=[jax.ShapeDtypeStruct((2, T, C), BF16),
                   jax.ShapeDtypeStruct((CONV_W, C), F32), vshape, vshape, vshape, vshape,
                   wshape, wshape] + rider.out_shapes,
        scratch_shapes=[pltpu.VMEM((SUBLANES, bw), F32)] * 3 + rider.scratch,
        compiler_params=_params("arbitrary" if nro else "parallel", "arbitrary"))(
            dyg, proj, proj, xb_all, h_all, proj, h_all, conv_w, w_r, w_i, b_r, b_i, lam, *rider.inputs)


def _later_sum(lk, tri):
    return jnp.dot(lk.astype(BF16), tri, preferred_element_type=F32)


def _log2_sigmoids(y):
    t = jnp.log(1.0 + jnp.exp2(-jnp.abs(y))) * LOG2E
    ls = jnp.minimum(y, 0.0) - t
    return ls, ls - y


def _attn_blocks(T):
    bk = min(T, ATT_KEY_BLOCK)
    bq = min(T, ATT_QUERY_BLOCK)
    return bq, bk, bq // bk


def _attn_step_blocks(T, bq):
    return ATT_STEP_BLOCKS if (T // bq) % ATT_STEP_BLOCKS == 0 else 1


def _attn_fwd(q, kv, gate, *, name):
    T, HD = q.shape
    H = HD // HEAD_DIM
    bq, bk, per = _attn_blocks(T)
    scale = 1.0 / math.sqrt(HEAD_DIM)

    nsub = _attn_step_blocks(T, bq)

    def body(q_ref, k_ref, v_ref, g_ref, o_ref, og_ref, lt_ref, w_ref):
        for sub in range(nsub):
            rows = pl.ds(sub * bq, bq)
            one_block(pl.program_id(1) * nsub + sub, q_ref.at[rows], k_ref, v_ref, g_ref.at[rows],
                      o_ref.at[rows], og_ref.at[rows], lt_ref.at[rows], w_ref)

    def one_block(i, q_ref, k_ref, v_ref, g_ref, o_ref, og_ref, lt_ref, w_ref):
        qv = q_ref[...]
        tr = lax.broadcasted_iota(jnp.int32, (bk, bk), 0)
        tc = lax.broadcasted_iota(jnp.int32, (bk, bk), 1)
        tri = (tr > tc).astype(BF16)
        ahead = (lax.broadcasted_iota(jnp.int32, (bq, bk), 0)
                 - lax.broadcasted_iota(jnp.int32, (bq, bk), 1))

        def starts_of(top):
            return [pl.multiple_of((top - d) * bk, bk) for d in range(per)]

        def scores(top):
            return [_dot(qv, k_ref[pl.ds(ks, bk), :], 1, 1) for ks in starts_of(top)]

        def weights(top, zs, c, mask):
            lss, sums, css, causals = [], [], [], []
            for ks, z in zip(starts_of(top), zs):
                ls, lk = _log2_sigmoids(z * (scale * LOG2E))
                if mask:
                    causals.append(ahead > ks - i * bq)
                    lk = jnp.where(causals[-1], lk, 0.0)
                lss.append(ls)
                sums.append(jnp.sum(lk, axis=1, keepdims=True))
                css.append(_later_sum(lk, tri))
            for d in range(per):
                w = jnp.exp2(lss[d] + (css[d] + c))
                if mask:
                    w = jnp.where(causals[d], w, 0.0)
                w_ref[d] = w.astype(BF16)
                c = c + sums[d]
            return c

        def values(top, acc):
            for d, ks in enumerate(starts_of(top)):
                acc = acc + jnp.dot(w_ref[d], v_ref[pl.ds(ks, bk), :], preferred_element_type=F32)
            return acc

        def more(state):
            gg, _, _, largest = state
            return (gg <= i) & (largest > WEIGHT_FLOOR_LOG2)

        def step(state):
            gg, acc, c, _ = state
            top = (i - gg) * per + per - 1
            zs = scores(top)
            acc = values(top + per, acc)
            c = weights(top, zs, c, False)
            return gg + 1, acc, c, jnp.max(c)

        diag_top = i * per + per - 1
        c = weights(diag_top, scores(diag_top), jnp.zeros((bq, 1), F32), True)
        gg, acc, c, _ = lax.while_loop(more, step, (1, jnp.zeros((bq, HEAD_DIM), F32), c, jnp.max(c)))
        acc = values((i - gg + 1) * per + per - 1, acc)
        o_ref[...] = acc
        g = g_ref[...]
        og_ref[...] = (acc * (g * _sigmoid(g))).astype(BF16)
        lane = lax.broadcasted_iota(jnp.int32, (bq, HEAD_DIM), 1)
        lt_ref[...] = jnp.where(lane == 1, (i - gg + 1).astype(F32), jnp.broadcast_to(c, (bq, HEAD_DIM)))

    qspec = pl.BlockSpec((nsub * bq, HEAD_DIM), lambda h, i: (i, h))
    return _pcall(
        body, name=name, grid=(H, T // (nsub * bq)),
        in_specs=[qspec, pl.BlockSpec((T, HEAD_DIM), lambda h, i: (0, h)),
                  pl.BlockSpec((T, HEAD_DIM), lambda h, i: (0, H + h)), qspec],
        out_specs=[qspec, qspec, qspec],
        out_shape=[jax.ShapeDtypeStruct((T, HD), F32), jax.ShapeDtypeStruct((T, HD), BF16),
                   jax.ShapeDtypeStruct((T, HD), F32)],
        scratch_shapes=[pltpu.VMEM((per, bq, bk), BF16)],
        compiler_params=_params("parallel", "arbitrary"))(q, kv, kv, gate)


def _attn_bwd(q, kv, gate, o, ltot, dog, *, name):
    T, HD = q.shape
    H = HD // HEAD_DIM
    bq, bk, per = _attn_blocks(T)
    nq = T // bq
    scale = 1.0 / math.sqrt(HEAD_DIM)

    nsub = _attn_step_blocks(T, bq)

    def body(q_ref, k_ref, v_ref, g_ref, o_ref, lt_ref, dog_ref,
             dqg_ref, dkv_ref, dk_acc, dv_acc, dz_ref, w_ref):
        for sub in range(nsub):
            rows = pl.ds(sub * bq, bq)
            one_block(pl.program_id(1) * nsub + sub, q_ref.at[rows], k_ref, v_ref, g_ref.at[rows],
                      o_ref.at[rows], lt_ref.at[rows], dog_ref.at[rows], dqg_ref.at[:, rows], dkv_ref,
                      dk_acc, dv_acc, dz_ref, w_ref)

    def one_block(i, q_ref, k_ref, v_ref, g_ref, o_ref, lt_ref, dog_ref,
                  dqg_ref, dkv_ref, dk_acc, dv_acc, dz_ref, w_ref):
        @pl.when(i == 0)
        def _():
            dk_acc[...] = jnp.zeros_like(dk_acc)
            dv_acc[...] = jnp.zeros_like(dv_acc)

        qv = q_ref[...]
        g, ov, dogv = g_ref[...], o_ref[...], dog_ref[...]
        sg = _sigmoid(g)
        do = dogv * (g * sg)
        dqg_ref[1] = (dogv * ov * (sg * (1.0 + g * (1.0 - sg)))).astype(BF16)
        dob = do.astype(BF16)
        ltot_v = lt_ref[:, 0:1]
        tr = lax.broadcasted_iota(jnp.int32, (bk, bk), 0)
        tc = lax.broadcasted_iota(jnp.int32, (bk, bk), 1)
        tri_later = (tr > tc).astype(BF16)
        tri_excl = (tr < tc).astype(BF16)
        ahead = (lax.broadcasted_iota(jnp.int32, (bq, bk), 0)
                 - lax.broadcasted_iota(jnp.int32, (bq, bk), 1))

        def starts_of(first):
            return [pl.multiple_of((first + d) * bk, bk) for d in range(per)]

        def scores(first):
            return ([_dot(qv, k_ref[pl.ds(ks, bk), :], 1, 1) for ks in starts_of(first)],
                    [_dot(dob, v_ref[pl.ds(ks, bk), :], 1, 1) for ks in starts_of(first)])

        def front(first, zs, dws, p_lk, p_g, mask):
            lss, css, causals = [], [], []
            for ks, z in zip(starts_of(first), zs):
                ls, lk = _log2_sigmoids(z * (scale * LOG2E))
                if mask:
                    causals.append(ahead > ks - i * bq)
                    lk = jnp.where(causals[-1], lk, 0.0)
                lss.append(ls)
                p_lk = p_lk + jnp.sum(lk, axis=1, keepdims=True)
                css.append((ltot_v - p_lk) + _later_sum(lk, tri_later))
            gms, befores = [], []
            for d in range(per):
                w = jnp.exp2(lss[d] + css[d])
                if mask:
                    w = jnp.where(causals[d], w, 0.0)
                gm = dws[d] * w
                gms.append(gm)
                w_ref[d] = w.astype(BF16)
                befores.append(jnp.dot(gm.astype(BF16), tri_excl, preferred_element_type=F32) + p_g)
                p_g = p_g + jnp.sum(gm, axis=1, keepdims=True)
            for d in range(per):
                dz = gms[d] - jnp.exp2(lss[d]) * (gms[d] + befores[d])
                if mask:
                    dz = jnp.where(causals[d], dz, 0.0)
                dz_ref[d] = (dz * scale).astype(BF16)
            return p_lk, p_g

        def back(first, dq):
            for d, ks in enumerate(starts_of(first)):
                dzb = dz_ref[d]
                dq = dq + jnp.dot(dzb, k_ref[pl.ds(ks, bk), :], preferred_element_type=F32)
                dk_acc[pl.ds(ks, bk), :] += _dot(dzb, qv, 0, 0)
                dv_acc[pl.ds(ks, bk), :] += _dot(w_ref[d], dob, 0, 0)
            return dq

        def step(mask):
            def trip(g, state):
                dq, p_lk, p_g = state
                zs, dws = scores(g * per)
                dq = back((g - 1) * per, dq)
                return (dq,) + front(g * per, zs, dws, p_lk, p_g, mask)
            return trip

        g0 = jnp.max(lt_ref[0:1, 1:2]).astype(jnp.int32)
        zero = jnp.zeros((bq, 1), F32)
        state = (jnp.zeros((bq, HEAD_DIM), F32),) + front(g0 * per, *scores(g0 * per), zero, zero, True)
        state = lax.fori_loop(g0 + 1, i, step(False), state)
        state = lax.fori_loop(jnp.maximum(i, g0 + 1), i + 1, step(True), state)
        dqg_ref[0] = back(i * per, state[0]).astype(BF16)

        @pl.when(i == nq - 1)
        def _():
            dkv_ref[0] = dk_acc[...].astype(BF16)
            dkv_ref[1] = dv_acc[...].astype(BF16)

    qspec = pl.BlockSpec((nsub * bq, HEAD_DIM), lambda h, i: (i, h))
    kspec = pl.BlockSpec((T, HEAD_DIM), lambda h, i: (0, h))
    return _pcall(
        body, name=name, grid=(H, nq // nsub),
        in_specs=[qspec, kspec, pl.BlockSpec((T, HEAD_DIM), lambda h, i: (0, H + h)),
                  qspec, qspec, qspec, qspec],
        out_specs=[pl.BlockSpec((2, nsub * bq, HEAD_DIM), lambda h, i: (0, i, h)),
                   pl.BlockSpec((2, T, HEAD_DIM), lambda h, i: (0, 0, h))],
        out_shape=[jax.ShapeDtypeStruct((2, T, HD), BF16)] * 2,
        scratch_shapes=[pltpu.VMEM((T, HEAD_DIM), F32)] * 2 + [pltpu.VMEM((per, bq, bk), BF16)] * 2,
        compiler_params=_params("parallel", "arbitrary"))(q, kv, kv, gate, o, ltot, dog)


def _position():
    return lax.axis_index("x"), lax.axis_index("y"), lax.axis_index("c")


def _chip_of(k, x, y):
    return (1 - x if k & 1 else x), (1 - y if k & 2 else y)


class _Gather:
    @staticmethod
    def scratch(n):
        return [pltpu.SemaphoreType.DMA((n, 7)), pltpu.SemaphoreType.DMA((n, 7)),
                pltpu.SemaphoreType.DMA((n,))]

    def __init__(self, ins, outs, send_sems, recv_sems, local_sems):
        self.ins, self.outs, self.n = ins, outs, len(ins)
        self.send_sems, self.recv_sems, self.local_sems = send_sems, recv_sems, local_sems
        x, y, c = _position()
        self.me, self.sibling = (x, y, c), (x, y, 1 - c)
        self.chips = [_chip_of(k, x, y) for k in (1, 2, 3)]

    def copy(self, a, k, block, to, src=None):
        slot = self.outs[a].at[4 * block[0] + 2 * block[1] + block[2]]
        return pltpu.make_async_remote_copy(
            src_ref=slot if src is None else src, dst_ref=slot,
            send_sem=self.send_sems.at[a, k], recv_sem=self.recv_sems.at[a, k],
            device_id=to, device_id_type=MESH)

    def own_copies(self):
        x, y, c = self.me
        mine = [pltpu.make_async_copy(self.ins[a], self.outs[a].at[4 * x + 2 * y + c], self.local_sems.at[a])
                for a in range(self.n)]
        first = []
        for a in range(self.n):
            first.append(self.copy(a, 0, self.me, self.sibling, src=self.ins[a]))
            first += [self.copy(a, 1 + j, self.me, (*chip, c), src=self.ins[a])
                      for j, chip in enumerate(self.chips)]
        return mine, first

    def start(self):
        mine, first = self.own_copies()
        for cp in mine + first:
            cp.start()

    def finish(self):
        c = self.me[2]
        mine, first = self.own_copies()
        passed = []
        for j, chip in enumerate(self.chips):
            for a in range(self.n):
                self.copy(a, 1 + j, (*chip, c), self.me).wait_recv()
                fwd = self.copy(a, 4 + j, (*chip, c), self.sibling)
                fwd.start()
                passed.append(fwd)
        for a in range(self.n):
            self.copy(a, 0, self.sibling, self.me).wait_recv()
            for j, chip in enumerate(self.chips):
                self.copy(a, 4 + j, (*chip, 1 - c), self.me).wait_recv()
        for cp in first + passed:
            cp.wait_send()
        for cp in mine:
            cp.wait()


class _Rider:
    def __init__(self, inputs, out_shapes, scratch, copies):
        self.inputs, self.out_shapes, self.scratch, self.copies = inputs, out_shapes, scratch, copies

    def bind(self, ins, outs, sems):
        def start():
            for cp in self.copies(ins, outs, sems):
                cp.start()

        def finish():
            cps = self.copies(ins, outs, sems)
            for cp in cps:
                cp.wait_send()
            for cp in cps:
                cp.wait_recv()

        return start, finish


def _sibling_rider(grads):
    n = len(grads)

    def copies(ins, outs, sems):
        x, y, c = _position()
        return [pltpu.make_async_remote_copy(
            src_ref=ins[a].at[2 * chip + (1 - c)], dst_ref=outs[a].at[chip],
            send_sem=sems[0].at[a, chip], recv_sem=sems[1].at[a, chip],
            device_id=(x, y, 1 - c), device_id_type=MESH) for a in range(n) for chip in range(4)]

    return _Rider(list(grads), [jax.ShapeDtypeStruct((4,) + g.shape[1:], g.dtype) for g in grads],
                  [pltpu.SemaphoreType.DMA((n, 4)), pltpu.SemaphoreType.DMA((n, 4))], copies)


def _chips_rider(parts):
    n = len(parts)

    def copies(ins, outs, sems):
        x, y, c = _position()
        cps = []
        for a in range(n):
            for k in range(3):
                cx, cy = _chip_of(k + 1, x, y)
                cps.append(pltpu.make_async_remote_copy(
                    src_ref=ins[a].at[2 * cx + cy], dst_ref=outs[a].at[k],
                    send_sem=sems[0].at[a, k], recv_sem=sems[1].at[a, k],
                    device_id=(cx, cy, c), device_id_type=MESH))
        return cps

    return _Rider(list(parts), [jax.ShapeDtypeStruct((3,) + p.shape[1:], p.dtype) for p in parts],
                  [pltpu.SemaphoreType.DMA((n, 3)), pltpu.SemaphoreType.DMA((n, 3))], copies)


def _small_gather(small):
    def body(small_ref, small_all, send_sems, recv_sems, local_sem):
        x, y, c = _position()
        me = 4 * x + 2 * y + c
        peers = [(x ^ (m >> 2), y ^ ((m >> 1) & 1), c ^ (m & 1)) for m in range(1, N_DEV)]
        sends = [pltpu.make_async_remote_copy(
            src_ref=small_ref, dst_ref=small_all.at[me], send_sem=send_sems.at[m], recv_sem=recv_sems.at[m],
            device_id=peer, device_id_type=MESH) for m, peer in enumerate(peers)]
        own = pltpu.make_async_copy(small_ref, small_all.at[me], local_sem)
        for cp in sends + [own]:
            cp.start()
        for cp in sends:
            cp.wait_send()
        for m, (px, py, pc) in enumerate(peers):
            pltpu.make_async_remote_copy(
                src_ref=small_ref, dst_ref=small_all.at[4 * px + 2 * py + pc],
                send_sem=send_sems.at[m], recv_sem=recv_sems.at[m],
                device_id=(px, py, pc), device_id_type=MESH).wait_recv()
        own.wait()

    return _pcall(
        body, name="small_gather", in_specs=[ANY], out_specs=ANY,
        out_shape=jax.ShapeDtypeStruct((N_DEV,) + small.shape, small.dtype),
        scratch_shapes=[pltpu.SemaphoreType.DMA((7,)), pltpu.SemaphoreType.DMA((7,)),
                        pltpu.SemaphoreType.DMA])(small)


def _pair_sum(grad, got, *, name):
    _, R, C = got.shape
    tr = _pick8(R, max(2 * SUBLANES, (1 << 17) // C))

    def body(g_ref, b_ref, own_ref, ob_ref):
        north = lax.axis_index("c") == 1
        x1, y1 = lax.axis_index("x") == 1, lax.axis_index("y") == 1
        sums = []
        for chip in range(4):
            sums.append(jnp.where(north, g_ref[chip, 1], g_ref[chip, 0]) + b_ref[chip])
            ob_ref[chip] = sums[-1].astype(BF16)
        own_ref[...] = jnp.where(x1, jnp.where(y1, sums[3], sums[2]), jnp.where(y1, sums[1], sums[0]))

    spec = pl.BlockSpec((4, tr, C), lambda i: (0, i, 0))
    return _pcall(
        body, name=name, grid=(R // tr,),
        in_specs=[pl.BlockSpec((4, 2, tr, C), lambda i: (0, 0, i, 0)), spec],
        out_specs=[pl.BlockSpec((tr, C), lambda i: (i, 0)), spec],
        out_shape=[jax.ShapeDtypeStruct((R, C), F32), jax.ShapeDtypeStruct((4, R, C), BF16)],
        compiler_params=_params("parallel"))(grad.reshape(4, 2, R, C), got)


def _pick8(n, cap):
    if n <= cap:
        return n
    best = None
    for t in range(SUBLANES, cap + 1, SUBLANES):
        if n % t == 0:
            best = t
    assert best is not None, (n, cap)
    return best


def _adamw(w, m, v, parts, *, name):
    R, C = w.shape
    tr = _pick8(R, max(SUBLANES, (1 << 17) // C))
    c1 = 1.0 - ADAM_B1 ** ADAM_STEP
    c2 = 1.0 - ADAM_B2 ** ADAM_STEP
    np_ = len(parts)

    def body(w_ref, m_ref, v_ref, *refs):
        p_refs = refs[:np_]
        g_ref, d_ref, nm_ref, nv_ref = refs[np_:]
        g = None
        for p_ref in p_refs:
            terms = [p_ref[...]] if len(p_ref.shape) == 2 else [p_ref[k] for k in range(p_ref.shape[0])]
            for t in terms:
                g = t.astype(F32) if g is None else g + t.astype(F32)
        mn = ADAM_B1 * m_ref[...] + (1.0 - ADAM_B1) * g
        vn = ADAM_B2 * v_ref[...] + (1.0 - ADAM_B2) * (g * g)
        d_ref[...] = -ADAM_LR * ((mn / c1) / (jnp.sqrt(vn / c2) + ADAM_EPS) + ADAM_WD * w_ref[...])
        g_ref[...] = g
        nm_ref[...] = mn
        nv_ref[...] = vn

    spec = pl.BlockSpec((tr, C), lambda i: (i, 0))
    pspecs = [spec if p.ndim == 2 else pl.BlockSpec((p.shape[0], tr, C), lambda i: (0, i, 0)) for p in parts]
    return _pcall(
        body, name=name, grid=(R // tr,), in_specs=[spec] * 3 + pspecs, out_specs=[spec] * 4,
        out_shape=[jax.ShapeDtypeStruct((R, C), F32)] * 4,
        compiler_params=_params("parallel"))(w, m, v, *parts)


def _rows(a):
    return a.reshape(-1, LANES)


def _whole_from_columns(shards, *, name):
    S, K, n = shards.shape
    tk = _pick8(K, 1024)

    def body(s_ref, o_ref):
        o_ref[...] = s_ref[...]

    return _pcall(
        body, name=name, grid=(K // tk, S),
        in_specs=[pl.BlockSpec((None, tk, n), lambda i, s: (s, i, 0))],
        out_specs=pl.BlockSpec((tk, n), lambda i, s: (i, s)),
        out_shape=jax.ShapeDtypeStruct((K, S * n), shards.dtype),
        compiler_params=_params("parallel", "parallel"))(shards)


def _late_weights(a_w_out_rows, w_kv_cols, b_w_in_cols, b_w_out_rows):
    whole_rows = lambda g: g.reshape(g.shape[0] * g.shape[1], g.shape[2])
    return (whole_rows(a_w_out_rows), _whole_from_columns(w_kv_cols, name="w_kv_whole"),
            _whole_from_columns(b_w_in_cols, name="b_w_in_whole"), whole_rows(b_w_out_rows))


def _forward_backward(xs, target, a_norm, g_a_w_in, conv_w, conv_b, g_w_r, g_w_i, b_r, b_i, lam,
                      kv_norm, b_norm, final_norm, *, late_weights=None, late_shards=None, h_a=None):
    if h_a is None:
        (h_a,) = _rms_fwd(xs, [a_norm], name="a_norm_fwd")
    proj_a = _mm_nn(h_a, g_a_w_in, name="a_in_proj", out_dtype=F32)
    xb, h_rec, yg, *gathered = _acore_fwd(proj_a, conv_w, conv_b, g_w_r, g_w_i, b_r, b_i, lam,
                                          name="a_core_fwd", riders=late_shards or ())
    g_a_w_out, g_w_kv, g_b_w_in, g_b_w_out = _late_weights(*gathered) if late_shards else late_weights
    x1 = _mm_nn(yg, g_a_w_out, name="a_out_proj", out_dtype=F32, res=xs)
    hk, hb = _rms_fwd(x1, [kv_norm, b_norm], name="kv_b_norm_fwd")
    kv = _mm_nn(hk, g_w_kv, name="kv_proj", out_dtype=BF16)
    hd = g_b_w_in.shape[1] // 2
    q = _mm_nn(hb, g_b_w_in, name="q_proj", out_dtype=BF16, col_off=0, cols=hd)
    gate_b = _mm_nn(hb, g_b_w_in, name="b_gate_proj", out_dtype=F32, col_off=hd, cols=hd)
    o, og, ltot = _attn_fwd(q, kv, gate_b, name="attn_fwd")
    x2 = _mm_nn(og, g_b_w_out, name="b_out_proj", out_dtype=F32, res=x1)
    dx2, d_final_norm, loss_part = _final_loss(x2, target, final_norm, name="final_norm_loss")

    dog = _mm_nt(dx2, g_b_w_out, name="b_out_proj_bwd")
    dw_b_out = _mm_tn(og, dx2, name="b_out_proj_wgrad")
    dproj_b, dkv = _attn_bwd(q, kv, gate_b, o, ltot, dog, name="attn_bwd")
    dhb = _mm_nt(dproj_b, g_b_w_in, name="b_in_proj_bwd")
    dw_b_in = _mm_tn(hb, dproj_b, name="b_in_proj_wgrad", shards=N_DEV)
    dhk = _mm_nt(dkv, g_w_kv, name="kv_proj_bwd")
    dw_kv = _mm_tn(hk, dkv, name="kv_proj_wgrad", shards=N_DEV)
    early = [dw_kv, dw_b_in, dw_b_out.reshape(N_DEV, -1, dw_b_out.shape[1])] if late_shards else []
    dx1, d_b_norm, d_kv_norm, *got = _rms_bwd(x1, dx2, [dhb, dhk], [b_norm, kv_norm], name="kv_b_norm_bwd",
                                              rider=_sibling_rider(early) if early else None)
    early_sums = [_pair_sum(f_, g_, name=f"pair_sum_early_{i}") for i, (f_, g_) in enumerate(zip(early, got))]
    dyg = _mm_nt(dx1, g_a_w_out, name="a_out_proj_bwd")
    dw_a_out = _mm_tn(yg, dx1, name="a_out_proj_wgrad")
    (dproj_a, d_conv_w, d_conv_b, d_b_r, d_b_i, d_lambda, dw_r, dw_i, *early_others) = _acore_bwd(
        dyg, proj_a, xb, h_rec, conv_w, g_w_r, g_w_i, b_r, b_i, lam, name="a_core_bwd",
        rider=_chips_rider([s[1] for s in early_sums]) if early else None)
    dw_a_in = _mm_tn(h_a, dproj_a, name="a_in_proj_wgrad", shards=N_DEV)
    rows = dw_r.shape[1] // N_DEV
    lru = lambda dw: dw.reshape(-1, N_DEV, rows, dw.shape[2]).transpose(1, 0, 2, 3).reshape(N_DEV, -1, dw.shape[2])
    late = [dw_a_in, dw_a_out.reshape(N_DEV, -1, dw_a_out.shape[1]), lru(dw_r), lru(dw_i)] if late_shards else []
    dh_a, *got = _mm_nt(dproj_a, g_a_w_in, name="a_in_proj_bwd", rider=_sibling_rider(late)) if late else (
        _mm_nt(dproj_a, g_a_w_in, name="a_in_proj_bwd"),)
    late_sums = [_pair_sum(f_, g_, name=f"pair_sum_late_{i}") for i, (f_, g_) in enumerate(zip(late, got))]
    grad_x, d_a_norm, *late_others = _rms_bwd(xs, dx1, [dh_a], [a_norm], name="a_norm_bwd",
                                              rider=_chips_rider([s[1] for s in late_sums]) if late else None)
    sums = late_sums[:2] + early_sums + late_sums[2:]
    others = late_others[:2] + early_others + late_others[2:]
    return (loss_part, grad_x, dw_a_in, dw_a_out, dw_kv, dw_b_in, dw_b_out, dw_r, dw_i, d_a_norm,
            d_conv_w, d_conv_b, d_b_r, d_b_i, d_lambda, d_kv_norm, d_b_norm, d_final_norm, sums, others)


def kernel(x, a_norm, a_w_in, a_conv_w, a_conv_b, a_w_r, a_b_r, a_w_i, a_b_i, a_lambda, a_w_out, kv_norm, w_kv, b_norm, b_w_in, b_w_out, final_norm, loss_target, m_a_norm, m_a_w_in, m_a_conv_w, m_a_conv_b, m_a_w_r, m_a_b_r, m_a_w_i, m_a_b_i, m_a_lambda, m_a_w_out, m_kv_norm, m_w_kv, m_b_norm, m_b_w_in, m_b_w_out, m_final_norm, v_a_norm, v_a_w_in, v_a_conv_w, v_a_conv_b, v_a_w_r, v_a_b_r, v_a_w_i, v_a_b_i, v_a_lambda, v_a_w_out, v_kv_norm, v_w_kv, v_b_norm, v_b_w_in, v_b_w_out, v_final_norm):
    T, D = x.shape[1], x.shape[2]
    nb, bw = a_w_r.shape[1], a_w_r.shape[3]
    C = nb * bw
    me = 4 * lax.axis_index("x") + 2 * lax.axis_index("y") + lax.axis_index("c")
    xs = x[0]
    target = loss_target[0]

    rows_r = a_w_r.shape[2]
    small_f32 = jnp.concatenate([_rows(a_conv_w[0]), _rows(b_norm[0])], axis=0)
    pad = (-small_f32.shape[0]) % SUBLANES
    small_f32 = jnp.pad(small_f32, ((0, pad), (0, 0)))
    h_a, a_w_in_cols, w_r_rows, w_i_rows, small_all = _rms_fwd(
        xs, [a_norm], name="a_norm_fwd",
        riders=[a_w_in[0].astype(BF16), a_w_r[0].reshape(nb * rows_r, bw).astype(BF16),
                a_w_i[0].reshape(nb * rows_r, bw).astype(BF16), small_f32])
    late_shards = [a_w_out[0].astype(BF16), w_kv.astype(BF16), b_w_in[0].astype(BF16), b_w_out[0].astype(BF16)]
    g_a_w_in = _whole_from_columns(a_w_in_cols, name="a_w_in_whole")
    g_w_r = w_r_rows.reshape(N_DEV, nb, rows_r, bw).transpose(1, 0, 2, 3).reshape(nb, bw, bw)
    g_w_i = w_i_rows.reshape(N_DEV, nb, rows_r, bw).transpose(1, 0, 2, 3).reshape(nb, bw, bw)
    cw_rows = a_conv_w.shape[1] * a_conv_w.shape[2] // LANES
    conv_w_full = small_all[:, :cw_rows, :].reshape(N_DEV, CONV_W, a_conv_w.shape[2])
    conv_w_full = conv_w_full.transpose(1, 0, 2).reshape(CONV_W, C)
    bn_rows = b_norm.shape[1] // LANES
    b_norm_full = small_all[:, cw_rows:cw_rows + bn_rows, :].reshape(1, D)
    kv_norm2, final_norm2 = kv_norm.reshape(1, D), final_norm.reshape(1, D)

    (loss_part, grad_x, dw_a_in, dw_a_out, dw_kv, dw_b_in, dw_b_out, dw_r, dw_i, d_a_norm, d_conv_w,
     d_conv_b, d_b_r, d_b_i, d_lambda, d_kv_norm, d_b_norm, d_final_norm, sums,
     others) = _forward_backward(
         xs, target, a_norm, g_a_w_in, conv_w_full, a_conv_b, g_w_r, g_w_i, a_b_r, a_b_i, a_lambda,
         kv_norm2, b_norm_full, final_norm2, late_shards=late_shards, h_a=h_a)

    small_parts = [d_a_norm, d_conv_w, d_conv_b, d_b_r, d_b_i, d_lambda, d_kv_norm, d_b_norm, d_final_norm]
    small_sizes = [p.size // LANES for p in small_parts]
    small = jnp.concatenate([_rows(p) for p in small_parts], axis=0)
    small_everyone = _small_gather(small)

    def shard2d(w):
        return w.reshape(-1, w.shape[-1])

    names_big = [(a_w_in, m_a_w_in, v_a_w_in), (a_w_out, m_a_w_out, v_a_w_out), (w_kv, m_w_kv, v_w_kv),
                 (b_w_in, m_b_w_in, v_b_w_in), (b_w_out, m_b_w_out, v_b_w_out),
                 (a_w_r, m_a_w_r, v_a_w_r), (a_w_i, m_a_w_i, v_a_w_i)]
    upd_big = []
    for i, (w, m, v) in enumerate(names_big):
        res = _adamw(shard2d(w), shard2d(m), shard2d(v), [sums[i][0], others[i]], name=f"adamw_{i}")
        upd_big.append([r.reshape(w.shape) for r in res])

    soffs = [0]
    for s in small_sizes:
        soffs.append(soffs[-1] + s)

    def small_piece(i):
        return small_everyone[:, soffs[i]:soffs[i + 1], :]

    cw_cols = a_conv_w.shape[2]
    conv_piece = small_piece(1).reshape(N_DEV, CONV_W, C)
    conv_piece = lax.dynamic_slice_in_dim(conv_piece, me * cw_cols, cw_cols, axis=2)
    conv_piece = conv_piece.reshape(N_DEV, CONV_W * cw_cols // LANES, LANES)
    bn_piece = lax.dynamic_slice_in_dim(small_piece(7), me * bn_rows, bn_rows, axis=1)
    small_g = jnp.concatenate([small_piece(0), conv_piece, small_piece(2), small_piece(3), small_piece(4),
                               small_piece(5), small_piece(6), bn_piece, small_piece(8)], axis=1)
    small_w = [(a_norm, m_a_norm, v_a_norm), (a_conv_w, m_a_conv_w, v_a_conv_w),
               (a_conv_b, m_a_conv_b, v_a_conv_b), (a_b_r, m_a_b_r, v_a_b_r), (a_b_i, m_a_b_i, v_a_b_i),
               (a_lambda, m_a_lambda, v_a_lambda), (kv_norm, m_kv_norm, v_kv_norm),
               (b_norm, m_b_norm, v_b_norm), (final_norm, m_final_norm, v_final_norm)]
    pack = lambda idx: jnp.concatenate([_rows(t[idx]) for t in small_w], axis=0)
    res_small = _adamw(pack(0), pack(1), pack(2), [small_g], name="adamw_small")
    woffs = [0]
    for t in small_w:
        woffs.append(woffs[-1] + t[0].size // LANES)
    upd_small = [[r[woffs[i]:woffs[i + 1]].reshape(small_w[i][0].shape) for r in res_small]
                 for i in range(len(small_w))]

    order = [("s", 0), ("b", 0), ("s", 1), ("s", 2), ("b", 5), ("s", 3), ("b", 6), ("s", 4), ("s", 5),
             ("b", 1), ("s", 6), ("b", 2), ("s", 7), ("b", 3), ("b", 4), ("s", 8)]
    per_weight = [(upd_big if kind == "b" else upd_small)[i] for kind, i in order]
    loss = lax.psum(loss_part[0, 0], ("x", "y", "c"))
    result = [loss, grad_x[None]]
    for field in range(4):
        result += [u[field] for u in per_weight]
    return tuple(result)
```

```python
import math

import jax
import jax.numpy as jnp
from jax import lax
from jax.experimental import pallas as pl
from jax.experimental.pallas import tpu as pltpu

F32 = jnp.float32
BF16 = jnp.bfloat16
MESH = pl.DeviceIdType.MESH

EPS = 1e-6
LOG2E = 1.4426950408889634
WEIGHT_FLOOR_LOG2 = -200.0
LRU_C = 8.0
CONV_W = 4
HEAD_DIM = 128
ADAM_LR = 0.001
ADAM_B1 = 0.9
ADAM_B2 = 0.999
ADAM_EPS = 1e-08
ADAM_WD = 0.01
ADAM_STEP = 10

N_DEV = 8
LANES = 128
SUBLANES = 8
VMEM_LIMIT = 56 * 1024 * 1024

ATT_KEY_BLOCK = 256
ATT_QUERY_BLOCK = 256
ATT_STEP_BLOCKS = 4
SCAN_BLOCK = 256
ROW_BLOCK = 256
MM_TOKEN_BLOCK = 1024
MM_WIDE_K = 2560
MM_WEIGHT_TILE = 1280
MM_CONTRACT_TOKENS = 2048
ANY = pl.BlockSpec(memory_space=pl.ANY)


def _pcall(body, **kw):
    return pl.pallas_call(body, **kw)


def _params(*sem):
    return pltpu.CompilerParams(dimension_semantics=sem, vmem_limit_bytes=VMEM_LIMIT)


def _pick(n, cap):
    if n <= cap:
        return n
    best = None
    for t in range(LANES, cap + 1, LANES):
        if n % t == 0:
            best = t
    assert best is not None, (n, cap)
    return best


def _token_block(T, K):
    tm = MM_TOKEN_BLOCK if K <= MM_WIDE_K else MM_TOKEN_BLOCK // 2
    return tm if T % tm == 0 else T


def _sigmoid(x):
    return 1.0 / (1.0 + jnp.exp(-x))


def _dot(a, b, ca, cb):
    return lax.dot_general(a, b, (((ca,), (cb,)), ((), ())), preferred_element_type=F32)


def _mm_nn(a, w, *, name, out_dtype, col_off=0, cols=None, res=None):
    T, K = a.shape
    K2, N = w.shape
    assert K == K2
    cols = N if cols is None else cols
    tm = _token_block(T, K)
    tn = _pick(cols, MM_WEIGHT_TILE)
    assert col_off % tn == 0
    off = col_off // tn
    has_res = res is not None

    def body(a_ref, b_ref, *rest):
        o_ref = rest[-1]
        acc = jnp.dot(a_ref[...].astype(BF16), b_ref[...], preferred_element_type=F32)
        if has_res:
            acc = acc + rest[0][...]
        o_ref[...] = acc.astype(out_dtype)

    in_specs = [pl.BlockSpec((tm, K), lambda j, i: (i, 0)),
                pl.BlockSpec((K, tn), lambda j, i: (0, off + j))]
    args = [a, w]
    if has_res:
        in_specs.append(pl.BlockSpec((tm, tn), lambda j, i: (i, j)))
        args.append(res)
    return _pcall(
        body, name=name, grid=(cols // tn, T // tm), in_specs=in_specs,
        out_specs=pl.BlockSpec((tm, tn), lambda j, i: (i, j)),
        out_shape=jax.ShapeDtypeStruct((T, cols), out_dtype),
        compiler_params=_params("parallel", "parallel"))(*args)


def _mm_nt(a, w, *, name, out_dtype=F32, rider=None):
    parts = a.shape[0] if a.ndim == 3 else 1
    T, kp = a.shape[-2:]
    N, K = w.shape
    assert K == parts * kp
    tm = _token_block(T, K)
    tn = _pick(N, MM_WEIGHT_TILE)
    nj, ni = N // tn, T // tm
    rider = rider or _Rider([], [], [], None)
    nri, nro = len(rider.inputs), len(rider.out_shapes)

    def body(a_ref, b_ref, *refs):
        o_ref = refs[nri]
        j, i = pl.program_id(0), pl.program_id(1)
        if nro:
            start, finish = rider.bind(refs[:nri], refs[nri + 1:nri + 1 + nro], refs[nri + 1 + nro:])
            pl.when((j == 0) & (i == 0))(start)
        if a.ndim == 3:
            acc = None
            for p in range(parts):
                term = _dot(a_ref[p], b_ref[:, p * kp:(p + 1) * kp], 1, 1)
                acc = term if acc is None else acc + term
        else:
            acc = _dot(a_ref[...].astype(BF16), b_ref[...], 1, 1)
        o_ref[...] = acc.astype(out_dtype)
        if nro:
            pl.when((j == nj - 1) & (i == ni - 1))(finish)

    a_spec = (pl.BlockSpec((parts, tm, kp), lambda j, i: (0, i, 0)) if a.ndim == 3
              else pl.BlockSpec((tm, K), lambda j, i: (i, 0)))
    sem = ("arbitrary", "arbitrary") if nro else ("parallel", "parallel")
    out = _pcall(
        body, name=name, grid=(nj, ni),
        in_specs=[a_spec, pl.BlockSpec((tn, K), lambda j, i: (j, 0))] + [ANY] * nri,
        out_specs=[pl.BlockSpec((tm, tn), lambda j, i: (i, j))] + [ANY] * nro,
        out_shape=[jax.ShapeDtypeStruct((T, N), out_dtype)] + rider.out_shapes,
        scratch_shapes=rider.scratch,
        compiler_params=_params(*sem))(a, w, *rider.inputs)
    return out if nro else out[0]


def _mm_tn(a, b, *, name, shards=1):
    T, Ko = a.shape
    parts = b.shape[0] if b.ndim == 3 else 1
    T2, n_part = b.shape[-2:]
    N = parts * n_part
    assert T == T2
    n = N // shards
    tt = min(T, MM_CONTRACT_TOKENS)
    tko = _pick(Ko, 1024)
    span = 2 if shards > 1 and 2 * n <= MM_WEIGHT_TILE and n_part % (2 * n) == 0 else 1
    tn = span * n if span > 1 else _pick(n, 1024)
    per = max(n // tn, 1)
    assert n_part % tn == 0
    per_part = n_part // tn

    def body(a_ref, b_ref, o_ref):
        t = pl.program_id(2)
        p = _dot(a_ref[...].astype(BF16), b_ref[...].astype(BF16), 0, 0)
        pieces = [p] if span == 1 else [p[:, s * n:(s + 1) * n] for s in range(span)]

        @pl.when(t == 0)
        def _():
            for s, piece in enumerate(pieces):
                o_ref[(s,) if span > 1 else ...] = piece

        @pl.when(t > 0)
        def _():
            for s, piece in enumerate(pieces):
                o_ref[(s,) if span > 1 else ...] += piece

    if shards == 1:
        out_spec = pl.BlockSpec((tko, tn), lambda i, j, t: (i, j))
        out_shape = jax.ShapeDtypeStruct((Ko, N), F32)
    elif span > 1:
        out_spec = pl.BlockSpec((span, tko, n), lambda i, j, t: (j, i, 0))
        out_shape = jax.ShapeDtypeStruct((shards, Ko, n), F32)
    else:
        out_spec = pl.BlockSpec((None, tko, tn), lambda i, j, t: (j // per, i, j % per))
        out_shape = jax.ShapeDtypeStruct((shards, Ko, n), F32)
    b_spec = (pl.BlockSpec((None, tt, tn), lambda i, j, t: (j // per_part, t, j % per_part)) if b.ndim == 3
              else pl.BlockSpec((tt, tn), lambda i, j, t: (t, j)))
    return _pcall(
        body, name=name, grid=(Ko // tko, N // tn, T // tt),
        in_specs=[pl.BlockSpec((tt, tko), lambda i, j, t: (t, i)), b_spec],
        out_specs=out_spec, out_shape=out_shape,
        compiler_params=_params("parallel", "parallel", "arbitrary"))(a, b)


def _rms_fwd(x, gains, *, name, riders=()):
    T, D = x.shape
    tm = min(T, ROW_BLOCK)
    steps = T // tm
    n, nr = len(gains), len(riders)

    def body(x_ref, *refs):
        g_refs, rider_in = refs[:n], refs[n:n + nr]
        o_refs, rider_out = refs[n + nr:2 * n + nr], refs[2 * n + nr:2 * n + 2 * nr]
        if nr:
            gather = _Gather(rider_in, rider_out, *refs[2 * n + 2 * nr:])
            pl.when(pl.program_id(0) == 0)(gather.start)
        xv = x_ref[...]
        xh = xv * lax.rsqrt(jnp.mean(xv * xv, axis=-1, keepdims=True) + EPS)
        for g_ref, o_ref in zip(g_refs, o_refs):
            o_ref[...] = (xh * g_ref[...]).astype(BF16)
        if nr:
            @pl.when(pl.program_id(0) == steps - 1)
            def _():
                gather.forward()
                gather.finish()

    row = pl.BlockSpec((tm, D), lambda i: (i, 0))
    vec = pl.BlockSpec((1, D), lambda i: (0, 0))
    return _pcall(
        body, name=name, grid=(steps,), in_specs=[row] + [vec] * n + [ANY] * nr,
        out_specs=[row] * n + [ANY] * nr,
        out_shape=[jax.ShapeDtypeStruct((T, D), BF16)] * n
                  + [jax.ShapeDtypeStruct((N_DEV,) + r.shape, r.dtype) for r in riders],
        scratch_shapes=_Gather.scratch(nr) if nr else [],
        compiler_params=_params("arbitrary" if nr else "parallel"))(x, *gains, *riders)


def _rms_bwd(x, dres, dhs, gains, *, name, rider=None):
    T, D = x.shape
    tm = min(T, ROW_BLOCK)
    steps = T // tm
    n = len(gains)
    rider = rider or _Rider([], [], [], None)
    nri, nro = len(rider.inputs), len(rider.out_shapes)

    def body(x_ref, dres_ref, *refs):
        dh_refs, g_refs = refs[:n], refs[n:2 * n]
        refs = refs[2 * n:]
        rider_in, refs = refs[:nri], refs[nri:]
        dx_ref, dg_refs = refs[0], refs[1:1 + n]
        rider_out, sems = refs[1 + n:1 + n + nro], refs[1 + n + nro:]
        i = pl.program_id(0)
        if nro:
            start, finish = rider.bind(rider_in, rider_out, sems)
            pl.when(i == 0)(start)
        xv = x_ref[...]
        r = lax.rsqrt(jnp.mean(xv * xv, axis=-1, keepdims=True) + EPS)
        xh = xv * r
        dxh = jnp.zeros_like(xv)
        for dh_ref, g_ref, dg_ref in zip(dh_refs, g_refs, dg_refs):
            dh = dh_ref[...]
            part = jnp.sum(dh * xh, axis=0, keepdims=True)

            @pl.when(i == 0)
            def _():
                dg_ref[...] = part

            @pl.when(i > 0)
            def _():
                dg_ref[...] += part

            dxh = dxh + dh * g_ref[...]
        dx_ref[...] = dres_ref[...] + r * (dxh - xh * jnp.mean(dxh * xh, axis=-1, keepdims=True))
        if nro:
            pl.when(i == steps - 1)(finish)

    row = pl.BlockSpec((tm, D), lambda i: (i, 0))
    vec = pl.BlockSpec((1, D), lambda i: (0, 0))
    return _pcall(
        body, name=name, grid=(steps,), in_specs=[row, row] + [row] * n + [vec] * n + [ANY] * nri,
        out_specs=[row] + [vec] * n + [ANY] * nro,
        out_shape=[jax.ShapeDtypeStruct((T, D), F32)] + [jax.ShapeDtypeStruct((1, D), F32)] * n
                  + rider.out_shapes,
        scratch_shapes=rider.scratch,
        compiler_params=_params("arbitrary"))(x, dres, *dhs, *gains, *rider.inputs)


def _final_loss(x, target, gain, *, name):
    T, D = x.shape
    tm = min(T, ROW_BLOCK)

    def body(x_ref, t_ref, g_ref, dx_ref, dg_ref, loss_ref):
        i = pl.program_id(0)
        xv = x_ref[...]
        g = g_ref[...]
        r = lax.rsqrt(jnp.mean(xv * xv, axis=-1, keepdims=True) + EPS)
        xh = xv * r
        err = xh * g - t_ref[...]
        part_loss = 0.5 * jnp.sum(jnp.mean(err * err, axis=-1, keepdims=True), axis=0, keepdims=True)
        dy = err * (1.0 / D)
        part_g = jnp.sum(dy * xh, axis=0, keepdims=True)

        @pl.when(i == 0)
        def _():
            dg_ref[...] = part_g
            loss_ref[...] = jnp.broadcast_to(part_loss, loss_ref.shape)

        @pl.when(i > 0)
        def _():
            dg_ref[...] += part_g
            loss_ref[...] += jnp.broadcast_to(part_loss, loss_ref.shape)

        dxh = dy * g
        dx_ref[...] = r * (dxh - xh * jnp.mean(dxh * xh, axis=-1, keepdims=True))

    row = pl.BlockSpec((tm, D), lambda i: (i, 0))
    vec = pl.BlockSpec((1, D), lambda i: (0, 0))
    return _pcall(
        body, name=name, grid=(T // tm,), in_specs=[row, row, vec],
        out_specs=[row, vec, pl.BlockSpec((1, LANES), lambda i: (0, 0))],
        out_shape=[jax.ShapeDtypeStruct((T, D), F32), jax.ShapeDtypeStruct((1, D), F32),
                   jax.ShapeDtypeStruct((1, LANES), F32)],
        compiler_params=_params("arbitrary"))(x, target, gain)


def _shift_down(x, prev_tail, j, row):
    tb = x.shape[0]
    prev = jnp.tile(prev_tail, (tb // SUBLANES, 1))
    return jnp.where(row >= j, pltpu.roll(x, j, 0), pltpu.roll(prev, j, 0))


def _shift_up(x, next_head, j, row):
    tb = x.shape[0]
    nxt = jnp.tile(next_head, (tb // SUBLANES, 1))
    return jnp.where(row < tb - j, pltpu.roll(x, tb - j, 0), pltpu.roll(nxt, tb - j, 0))


def _lru_gates(xb, wr, wi, br, bi, lam):
    xbb = xb.astype(BF16)
    r = _sigmoid(jnp.dot(xbb, wr, preferred_element_type=F32) + br)
    i = _sigmoid(jnp.dot(xbb, wi, preferred_element_type=F32) + bi)
    sp = jnp.maximum(-lam, 0.0) + jnp.log1p(jnp.exp(-jnp.abs(lam)))
    log_a = (-LRU_C) * r * sp
    a = jnp.exp(log_a)
    a2 = a * a
    mult = jnp.sqrt(jnp.maximum(-jnp.tanh(log_a) * (1.0 + a2), 0.0))
    return xbb, r, i, sp, a, a2, mult


def _scan_rows(coef, val, edge, reverse):
    tb, C = coef.shape
    a, b = coef, val
    row = lax.broadcasted_iota(jnp.int32, (tb, C), 0)
    s = 1
    while s < tb:
        m = (row < tb - s) if reverse else (row >= s)
        shift = tb - s if reverse else s
        b = jnp.where(m, a * pltpu.roll(b, shift, 0) + b, b)
        a = jnp.where(m, a * pltpu.roll(a, shift, 0), a)
        s *= 2
    return b + a * edge


def _acore_fwd(proj, conv_w, conv_b, w_r, w_i, b_r, b_i, lam, *, name, riders=()):
    T, C2 = proj.shape
    C = C2 // 2
    nb, bw, _ = w_r.shape
    tb = min(T, SCAN_BLOCK)
    nt = T // tb
    nr = len(riders)

    def body(xp_ref, gate_ref, cw_ref, cb_ref, wr_ref, wi_ref, br_ref, bi_ref, lam_ref, *refs):
        rider_in, refs = refs[:nr], refs[nr:]
        xb_ref, h_ref, yg_ref = refs[:3]
        rider_out, refs = refs[3:3 + nr], refs[3 + nr:]
        tail_ref, hlast_ref = refs[:2]
        t = pl.program_id(1)
        if nr:
            gather = _Gather(rider_in, rider_out, *refs[2:])
            pl.when((pl.program_id(0) == 0) & (t == 0))(gather.start)

        @pl.when(t == 0)
        def _():
            tail_ref[...] = jnp.zeros_like(tail_ref)
            hlast_ref[...] = jnp.zeros_like(hlast_ref)

        row = lax.broadcasted_iota(jnp.int32, (tb, bw), 0)
        xp = xp_ref[...]
        tail = tail_ref[...]
        xb = cb_ref[...] + cw_ref[CONV_W - 1:CONV_W, :] * xp
        for j in range(1, CONV_W):
            xb = xb + cw_ref[CONV_W - 1 - j:CONV_W - j, :] * _shift_down(xp, tail, j, row)
        tail_ref[...] = xp[tb - SUBLANES:, :]
        xb_ref[...] = xb

        _, r, i, sp, a, a2, mult = _lru_gates(xb, wr_ref[...], wi_ref[...], br_ref[...], bi_ref[...],
                                              lam_ref[...])
        h = _scan_rows(a, mult * (i * xb), hlast_ref[SUBLANES - 1:SUBLANES, :], False)
        hlast_ref[...] = h[tb - SUBLANES:, :]
        h_ref[...] = h
        gate = gate_ref[...]
        yg_ref[...] = (h * (gate * _sigmoid(gate))).astype(BF16)
        if nr:
            pl.when((pl.program_id(0) == nb - 1) & (t == 0))(gather.forward)
            pl.when((pl.program_id(0) == nb - 1) & (t == nt - 1))(gather.finish)

    blk = lambda off: pl.BlockSpec((tb, bw), lambda n, t: (t, off + n))
    vec = pl.BlockSpec((1, bw), lambda n, t: (0, n))
    wspec = pl.BlockSpec((None, bw, bw), lambda n, t: (n, 0, 0))
    return _pcall(
        body, name=name, grid=(nb, nt),
        in_specs=[blk(0), blk(nb), pl.BlockSpec((CONV_W, bw), lambda n, t: (0, n)), vec, wspec, wspec,
                  vec, vec, vec] + [ANY] * nr,
        out_specs=[blk(0), blk(0), blk(0)] + [ANY] * nr,
        out_shape=[jax.ShapeDtypeStruct((T, C), F32), jax.ShapeDtypeStruct((T, C), F32),
                   jax.ShapeDtypeStruct((T, C), BF16)]
                  + [jax.ShapeDtypeStruct((N_DEV,) + r.shape, r.dtype) for r in riders],
        scratch_shapes=[pltpu.VMEM((SUBLANES, bw), F32), pltpu.VMEM((SUBLANES, bw), F32)]
                       + (_Gather.scratch(nr) if nr else []),
        compiler_params=_params("arbitrary" if nr else "parallel", "arbitrary"))(
            proj, proj, conv_w, conv_b, w_r, w_i, b_r, b_i, lam, *riders)


def _acore_bwd(dyg, proj, xb_all, h_all, conv_w, w_r, w_i, b_r, b_i, lam, *, name, rider=None):
    T, C2 = proj.shape
    C = C2 // 2
    nb, bw, _ = w_r.shape
    tb = min(T, SCAN_BLOCK)
    nt = T // tb
    per8 = tb // SUBLANES
    rider = rider or _Rider([], [], [], None)
    nri, nro = len(rider.inputs), len(rider.out_shapes)

    def body(dyg_ref, xp_ref, gate_ref, xb_ref, h_ref, xp_prev_ref, h_prev_ref, cw_ref,
             wr_ref, wi_ref, br_ref, bi_ref, lam_ref, *refs):
        rider_in, refs = refs[:nri], refs[nri:]
        dproj_ref, dcw_ref, dcb_ref, dbr_ref, dbi_ref, dlam_ref, dwr_ref, dwi_ref = refs[:8]
        rider_out, refs = refs[8:8 + nro], refs[8 + nro:]
        gh_next_ref, a_next_ref, dxb_next_ref = refs[:3]
        step = pl.program_id(1)
        first_block = step == nt - 1
        if nro:
            start, finish = rider.bind(rider_in, rider_out, refs[3:])
            pl.when((pl.program_id(0) == 0) & (step == 0))(start)

        @pl.when(step == 0)
        def _():
            gh_next_ref[...] = jnp.zeros_like(gh_next_ref)
            a_next_ref[...] = jnp.zeros_like(a_next_ref)
            dxb_next_ref[...] = jnp.zeros_like(dxb_next_ref)

        row = lax.broadcasted_iota(jnp.int32, (tb, bw), 0)
        keep = jnp.where(first_block, 0.0, 1.0)
        h_prev = h_prev_ref[...] * keep
        xp_prev = xp_prev_ref[...] * keep
        xp, gate, xb, h, dyg_v = xp_ref[...], gate_ref[...], xb_ref[...], h_ref[...], dyg_ref[...]
        lam_v = lam_ref[...]
        wr, wi = wr_ref[...], wi_ref[...]

        sg = _sigmoid(gate)
        dh = dyg_v * (gate * sg)
        dproj_ref[1] = (dyg_v * h * (sg * (1.0 + gate * (1.0 - sg)))).astype(BF16)

        xbb, r, i, sp, a, a2, mult = _lru_gates(xb, wr, wi, br_ref[...], bi_ref[...], lam_v)

        gh = _scan_rows(_shift_up(a, a_next_ref[...], 1, row), dh, gh_next_ref[0:1, :], True)
        gh_next_ref[...] = gh[0:SUBLANES, :]
        a_next_ref[...] = a[0:SUBLANES, :]

        da = gh * _shift_down(h, h_prev, 1, row)
        dmult = gh * (i * xb)
        di = gh * mult * xb
        dxb = gh * mult * i
        dla = da * a - dmult * jnp.where(mult > 0.0, a2 / mult, 0.0)
        dr = dla * ((-LRU_C) * sp)
        dsp = jnp.sum(dla * ((-LRU_C) * r), axis=0, keepdims=True)
        dlam_part = dsp * (-_sigmoid(-lam_v))
        dpr = dr * r * (1.0 - r)
        dpi = di * i * (1.0 - i)
        dbr_part = jnp.sum(dpr, axis=0, keepdims=True)
        dbi_part = jnp.sum(dpi, axis=0, keepdims=True)
        dprb, dpib = dpr.astype(BF16), dpi.astype(BF16)
        dwr_part = _dot(xbb, dprb, 0, 0)
        dwi_part = _dot(xbb, dpib, 0, 0)
        dxb = dxb + _dot(dprb, wr, 1, 1) + _dot(dpib, wi, 1, 1)

        dxb_next = dxb_next_ref[...]
        dxp = cw_ref[CONV_W - 1:CONV_W, :] * dxb
        for j in range(1, CONV_W):
            dxp = dxp + cw_ref[CONV_W - 1 - j:CONV_W - j, :] * _shift_up(dxb, dxb_next, j, row)
        dxb_next_ref[...] = dxb[0:SUBLANES, :]
        dproj_ref[0] = dxp.astype(BF16)
        dcb_part = jnp.sum(dxb, axis=0, keepdims=True)
        dcw_rows = []
        for k in range(CONV_W):
            j = CONV_W - 1 - k
            sh = xp if j == 0 else _shift_down(xp, xp_prev, j, row)
            dcw_rows.append(jnp.sum(dxb * sh, axis=0, keepdims=True))

        @pl.when(step == 0)
        def _():
            for k in range(CONV_W):
                dcw_ref[k:k + 1, :] = dcw_rows[k]
            dcb_ref[...] = dcb_part
            dbr_ref[...] = dbr_part
            dbi_ref[...] = dbi_part
            dlam_ref[...] = dlam_part
            dwr_ref[...] = dwr_part
            dwi_ref[...] = dwi_part

        @pl.when(step > 0)
        def _():
            for k in range(CONV_W):
                dcw_ref[k:k + 1, :] += dcw_rows[k]
            dcb_ref[...] += dcb_part
            dbr_ref[...] += dbr_part
            dbi_ref[...] += dbi_part
            dlam_ref[...] += dlam_part
            dwr_ref[...] += dwr_part
            dwi_ref[...] += dwi_part

        if nro:
            pl.when((pl.program_id(0) == nb - 1) & (step == nt - 1))(finish)

    rev = lambda s: nt - 1 - s
    blk = lambda off: pl.BlockSpec((tb, bw), lambda n, s: (rev(s), off + n))
    prev8 = lambda off: pl.BlockSpec(
        (SUBLANES, bw), lambda n, s: (jnp.maximum(rev(s) * per8 - 1, 0), off + n))
    vec = pl.BlockSpec((1, bw), lambda n, s: (0, n))
    wspec = pl.BlockSpec((None, bw, bw), lambda n, s: (n, 0, 0))
    cwspec = pl.BlockSpec((CONV_W, bw), lambda n, s: (0, n))
    vshape = jax.ShapeDtypeStruct((1, C), F32)
    wshape = jax.ShapeDtypeStruct((nb, bw, bw), F32)
    return _pcall(
        body, name=name, grid=(nb, nt),
        in_specs=[blk(0), blk(0), blk(nb), blk(0), blk(0), prev8(0), prev8(0), cwspec,
                  wspec, wspec, vec, vec, vec] + [ANY] * nri,
        out_specs=[pl.BlockSpec((2, tb, bw), lambda n, s: (0, rev(s), n)), cwspec, vec, vec, vec, vec,
                   wspec, wspec] + [ANY] * nro,
        out_shape=[jax.ShapeDtypeStruct((2, T, C), BF16),
                   jax.ShapeDtypeStruct((CONV_W, C), F32), vshape, vshape, vshape, vshape,
                   wshape, wshape] + rider.out_shapes,
        scratch_shapes=[pltpu.VMEM((SUBLANES, bw), F32)] * 3 + rider.scratch,
        compiler_params=_params("arbitrary" if nro else "parallel", "arbitrary"))(
            dyg, proj, proj, xb_all, h_all, proj, h_all, conv_w, w_r, w_i, b_r, b_i, lam, *rider.inputs)


def _later_sum(lk, tri):
    return jnp.dot(lk.astype(BF16), tri, preferred_element_type=F32)


def _log2_sigmoids(y):
    t = jnp.log(1.0 + jnp.exp2(-jnp.abs(y))) * LOG2E
    ls = jnp.minimum(y, 0.0) - t
    return ls, ls - y


def _attn_blocks(T):
    bk = min(T, ATT_KEY_BLOCK)
    bq = min(T, ATT_QUERY_BLOCK)
    return bq, bk, bq // bk


def _attn_step_blocks(T, bq):
    return ATT_STEP_BLOCKS if (T // bq) % ATT_STEP_BLOCKS == 0 else 1


def _attn_fwd(q, kv, gate, *, name):
    T, HD = q.shape
    H = HD // HEAD_DIM
    bq, bk, per = _attn_blocks(T)
    scale = 1.0 / math.sqrt(HEAD_DIM)

    nsub = _attn_step_blocks(T, bq)

    def body(q_ref, k_ref, v_ref, g_ref, o_ref, og_ref, lt_ref, w_ref):
        for sub in range(nsub):
            rows = pl.ds(sub * bq, bq)
            one_block(pl.program_id(1) * nsub + sub, q_ref.at[rows], k_ref, v_ref, g_ref.at[rows],
                      o_ref.at[rows], og_ref.at[rows], lt_ref.at[rows], w_ref)

    def one_block(i, q_ref, k_ref, v_ref, g_ref, o_ref, og_ref, lt_ref, w_ref):
        qv = q_ref[...]
        tr = lax.broadcasted_iota(jnp.int32, (bk, bk), 0)
        tc = lax.broadcasted_iota(jnp.int32, (bk, bk), 1)
        tri = (tr > tc).astype(BF16)
        ahead = (lax.broadcasted_iota(jnp.int32, (bq, bk), 0)
                 - lax.broadcasted_iota(jnp.int32, (bq, bk), 1))

        def starts_of(top):
            return [pl.multiple_of((top - d) * bk, bk) for d in range(per)]

        def scores(top):
            return [_dot(qv, k_ref[pl.ds(ks, bk), :], 1, 1) for ks in starts_of(top)]

        def weights(top, zs, c, mask):
            lss, sums, css, causals = [], [], [], []
            for ks, z in zip(starts_of(top), zs):
                ls, lk = _log2_sigmoids(z * (scale * LOG2E))
                if mask:
                    causals.append(ahead > ks - i * bq)
                    lk = jnp.where(causals[-1], lk, 0.0)
                lss.append(ls)
                sums.append(jnp.sum(lk, axis=1, keepdims=True))
                css.append(_later_sum(lk, tri))
            for d in range(per):
                w = jnp.exp2(lss[d] + (css[d] + c))
                if mask:
                    w = jnp.where(causals[d], w, 0.0)
                w_ref[d] = w.astype(BF16)
                c = c + sums[d]
            return c

        def values(top, acc):
            for d, ks in enumerate(starts_of(top)):
                acc = acc + jnp.dot(w_ref[d], v_ref[pl.ds(ks, bk), :], preferred_element_type=F32)
            return acc

        def more(state):
            gg, _, _, largest = state
            return (gg <= i) & (largest > WEIGHT_FLOOR_LOG2)

        def step(state):
            gg, acc, c, _ = state
            top = (i - gg) * per + per - 1
            zs = scores(top)
            acc = values(top + per, acc)
            c = weights(top, zs, c, False)
            return gg + 1, acc, c, jnp.max(c)

        diag_top = i * per + per - 1
        c = weights(diag_top, scores(diag_top), jnp.zeros((bq, 1), F32), True)
        gg, acc, c, _ = lax.while_loop(more, step, (1, jnp.zeros((bq, HEAD_DIM), F32), c, jnp.max(c)))
        acc = values((i - gg + 1) * per + per - 1, acc)
        o_ref[...] = acc
        g = g_ref[...]
        og_ref[...] = (acc * (g * _sigmoid(g))).astype(BF16)
        lane = lax.broadcasted_iota(jnp.int32, (bq, HEAD_DIM), 1)
        lt_ref[...] = jnp.where(lane == 1, (i - gg + 1).astype(F32), jnp.broadcast_to(c, (bq, HEAD_DIM)))

    qspec = pl.BlockSpec((nsub * bq, HEAD_DIM), lambda h, i: (i, h))
    return _pcall(
        body, name=name, grid=(H, T // (nsub * bq)),
        in_specs=[qspec, pl.BlockSpec((T, HEAD_DIM), lambda h, i: (0, h)),
                  pl.BlockSpec((T, HEAD_DIM), lambda h, i: (0, H + h)), qspec],
        out_specs=[qspec, qspec, qspec],
        out_shape=[jax.ShapeDtypeStruct((T, HD), F32), jax.ShapeDtypeStruct((T, HD), BF16),
                   jax.ShapeDtypeStruct((T, HD), F32)],
        scratch_shapes=[pltpu.VMEM((per, bq, bk), BF16)],
        compiler_params=_params("parallel", "arbitrary"))(q, kv, kv, gate)


def _attn_bwd(q, kv, gate, o, ltot, dog, *, name):
    T, HD = q.shape
    H = HD // HEAD_DIM
    bq, bk, per = _attn_blocks(T)
    nq = T // bq
    scale = 1.0 / math.sqrt(HEAD_DIM)

    nsub = _attn_step_blocks(T, bq)

    def body(q_ref, k_ref, v_ref, g_ref, o_ref, lt_ref, dog_ref,
             dqg_ref, dkv_ref, dk_acc, dv_acc, dz_ref, w_ref):
        for sub in range(nsub):
            rows = pl.ds(sub * bq, bq)
            one_block(pl.program_id(1) * nsub + sub, q_ref.at[rows], k_ref, v_ref, g_ref.at[rows],
                      o_ref.at[rows], lt_ref.at[rows], dog_ref.at[rows], dqg_ref.at[:, rows], dkv_ref,
                      dk_acc, dv_acc, dz_ref, w_ref)

    def one_block(i, q_ref, k_ref, v_ref, g_ref, o_ref, lt_ref, dog_ref,
                  dqg_ref, dkv_ref, dk_acc, dv_acc, dz_ref, w_ref):
        @pl.when(i == 0)
        def _():
            dk_acc[...] = jnp.zeros_like(dk_acc)
            dv_acc[...] = jnp.zeros_like(dv_acc)

        qv = q_ref[...]
        g, ov, dogv = g_ref[...], o_ref[...], dog_ref[...]
        sg = _sigmoid(g)
        do = dogv * (g * sg)
        dqg_ref[1] = (dogv * ov * (sg * (1.0 + g * (1.0 - sg)))).astype(BF16)
        dob = do.astype(BF16)
        ltot_v = lt_ref[:, 0:1]
        tr = lax.broadcasted_iota(jnp.int32, (bk, bk), 0)
        tc = lax.broadcasted_iota(jnp.int32, (bk, bk), 1)
        tri_later = (tr > tc).astype(BF16)
        tri_excl = (tr < tc).astype(BF16)
        ahead = (lax.broadcasted_iota(jnp.int32, (bq, bk), 0)
                 - lax.broadcasted_iota(jnp.int32, (bq, bk), 1))

        def starts_of(first):
            return [pl.multiple_of((first + d) * bk, bk) for d in range(per)]

        def scores(first):
            return ([_dot(qv, k_ref[pl.ds(ks, bk), :], 1, 1) for ks in starts_of(first)],
                    [_dot(dob, v_ref[pl.ds(ks, bk), :], 1, 1) for ks in starts_of(first)])

        def front(first, zs, dws, p_lk, p_g, mask):
            lss, css, causals = [], [], []
            for ks, z in zip(starts_of(first), zs):
                ls, lk = _log2_sigmoids(z * (scale * LOG2E))
                if mask:
                    causals.append(ahead > ks - i * bq)
                    lk = jnp.where(causals[-1], lk, 0.0)
                lss.append(ls)
                p_lk = p_lk + jnp.sum(lk, axis=1, keepdims=True)
                css.append((ltot_v - p_lk) + _later_sum(lk, tri_later))
            gms, befores = [], []
            for d in range(per):
                w = jnp.exp2(lss[d] + css[d])
                if mask:
                    w = jnp.where(causals[d], w, 0.0)
                gm = dws[d] * w
                gms.append(gm)
                w_ref[d] = w.astype(BF16)
                befores.append(jnp.dot(gm.astype(BF16), tri_excl, preferred_element_type=F32) + p_g)
                p_g = p_g + jnp.sum(gm, axis=1, keepdims=True)
            for d in range(per):
                dz = gms[d] - jnp.exp2(lss[d]) * (gms[d] + befores[d])
                if mask:
                    dz = jnp.where(causals[d], dz, 0.0)
                dz_ref[d] = (dz * scale).astype(BF16)
            return p_lk, p_g

        def back(first, dq):
            for d, ks in enumerate(starts_of(first)):
                dzb = dz_ref[d]
                dq = dq + jnp.dot(dzb, k_ref[pl.ds(ks, bk), :], preferred_element_type=F32)
                dk_acc[pl.ds(ks, bk), :] += _dot(dzb, qv, 0, 0)
                dv_acc[pl.ds(ks, bk), :] += _dot(w_ref[d], dob, 0, 0)
            return dq

        def step(mask):
            def trip(g, state):
                dq, p_lk, p_g = state
                zs, dws = scores(g * per)
                dq = back((g - 1) * per, dq)
                return (dq,) + front(g * per, zs, dws, p_lk, p_g, mask)
            return trip

        g0 = jnp.max(lt_ref[0:1, 1:2]).astype(jnp.int32)
        zero = jnp.zeros((bq, 1), F32)
        state = (jnp.zeros((bq, HEAD_DIM), F32),) + front(g0 * per, *scores(g0 * per), zero, zero, True)
        state = lax.fori_loop(g0 + 1, i, step(False), state)
        state = lax.fori_loop(jnp.maximum(i, g0 + 1), i + 1, step(True), state)
        dqg_ref[0] = back(i * per, state[0]).astype(BF16)

        @pl.when(i == nq - 1)
        def _():
            dkv_ref[0] = dk_acc[...].astype(BF16)
            dkv_ref[1] = dv_acc[...].astype(BF16)

    qspec = pl.BlockSpec((nsub * bq, HEAD_DIM), lambda h, i: (i, h))
    kspec = pl.BlockSpec((T, HEAD_DIM), lambda h, i: (0, h))
    return _pcall(
        body, name=name, grid=(H, nq // nsub),
        in_specs=[qspec, kspec, pl.BlockSpec((T, HEAD_DIM), lambda h, i: (0, H + h)),
                  qspec, qspec, qspec, qspec],
        out_specs=[pl.BlockSpec((2, nsub * bq, HEAD_DIM), lambda h, i: (0, i, h)),
                   pl.BlockSpec((2, T, HEAD_DIM), lambda h, i: (0, 0, h))],
        out_shape=[jax.ShapeDtypeStruct((2, T, HD), BF16)] * 2,
        scratch_shapes=[pltpu.VMEM((T, HEAD_DIM), F32)] * 2 + [pltpu.VMEM((per, bq, bk), BF16)] * 2,
        compiler_params=_params("parallel", "arbitrary"))(q, kv, kv, gate, o, ltot, dog)


def _position():
    return lax.axis_index("x"), lax.axis_index("y"), lax.axis_index("c")


def _chip_of(k, x, y):
    return (1 - x if k & 1 else x), (1 - y if k & 2 else y)


class _Gather:
    @staticmethod
    def scratch(n):
        return [pltpu.SemaphoreType.DMA((n, 7)), pltpu.SemaphoreType.DMA((n, 7)),
                pltpu.SemaphoreType.DMA((n,))]

    def __init__(self, ins, outs, send_sems, recv_sems, local_sems):
        self.ins, self.outs, self.n = ins, outs, len(ins)
        self.send_sems, self.recv_sems, self.local_sems = send_sems, recv_sems, local_sems
        x, y, c = _position()
        self.me, self.sibling = (x, y, c), (x, y, 1 - c)
        self.chips = [_chip_of(k, x, y) for k in (1, 2, 3)]

    def copy(self, a, k, block, to, src=None):
        slot = self.outs[a].at[4 * block[0] + 2 * block[1] + block[2]]
        return pltpu.make_async_remote_copy(
            src_ref=slot if src is None else src, dst_ref=slot,
            send_sem=self.send_sems.at[a, k], recv_sem=self.recv_sems.at[a, k],
            device_id=to, device_id_type=MESH)

    def own_copies(self):
        x, y, c = self.me
        mine = [pltpu.make_async_copy(self.ins[a], self.outs[a].at[4 * x + 2 * y + c], self.local_sems.at[a])
                for a in range(self.n)]
        first = []
        for a in range(self.n):
            first.append(self.copy(a, 0, self.me, self.sibling, src=self.ins[a]))
            first += [self.copy(a, 1 + j, self.me, (*chip, c), src=self.ins[a])
                      for j, chip in enumerate(self.chips)]
        return mine, first

    def start(self):
        mine, first = self.own_copies()
        for cp in mine + first:
            cp.start()

    def passed_copies(self):
        c = self.me[2]
        return [self.copy(a, 4 + j, (*chip, c), self.sibling)
                for j, chip in enumerate(self.chips) for a in range(self.n)]

    def forward(self):
        c = self.me[2]
        passed = self.passed_copies()
        for j, chip in enumerate(self.chips):
            for a in range(self.n):
                self.copy(a, 1 + j, (*chip, c), self.me).wait_recv()
                passed[j * self.n + a].start()

    def finish(self):
        c = self.me[2]
        mine, first = self.own_copies()
        passed = self.passed_copies()
        for a in range(self.n):
            self.copy(a, 0, self.sibling, self.me).wait_recv()
            for j, chip in enumerate(self.chips):
                self.copy(a, 4 + j, (*chip, 1 - c), self.me).wait_recv()
        for cp in first + passed:
            cp.wait_send()
        for cp in mine:
            cp.wait()


class _Rider:
    def __init__(self, inputs, out_shapes, scratch, copies):
        self.inputs, self.out_shapes, self.scratch, self.copies = inputs, out_shapes, scratch, copies

    def bind(self, ins, outs, sems):
        def start():
            for cp in self.copies(ins, outs, sems):
                cp.start()

        def finish():
            cps = self.copies(ins, outs, sems)
            for cp in cps:
                cp.wait_send()
            for cp in cps:
                cp.wait_recv()

        return start, finish


def _sibling_rider(grads):
    n = len(grads)

    def copies(ins, outs, sems):
        x, y, c = _position()
        return [pltpu.make_async_remote_copy(
            src_ref=ins[a].at[2 * chip + (1 - c)], dst_ref=outs[a].at[chip],
            send_sem=sems[0].at[a, chip], recv_sem=sems[1].at[a, chip],
            device_id=(x, y, 1 - c), device_id_type=MESH) for a in range(n) for chip in range(4)]

    return _Rider(list(grads), [jax.ShapeDtypeStruct((4,) + g.shape[1:], g.dtype) for g in grads],
                  [pltpu.SemaphoreType.DMA((n, 4)), pltpu.SemaphoreType.DMA((n, 4))], copies)


def _chips_rider(parts):
    n = len(parts)

    def copies(ins, outs, sems):
        x, y, c = _position()
        cps = []
        for a in range(n):
            for k in range(3):
                cx, cy = _chip_of(k + 1, x, y)
                cps.append(pltpu.make_async_remote_copy(
                    src_ref=ins[a].at[2 * cx + cy], dst_ref=outs[a].at[k],
                    send_sem=sems[0].at[a, k], recv_sem=sems[1].at[a, k],
                    device_id=(cx, cy, c), device_id_type=MESH))
        return cps

    return _Rider(list(parts), [jax.ShapeDtypeStruct((3,) + p.shape[1:], p.dtype) for p in parts],
                  [pltpu.SemaphoreType.DMA((n, 3)), pltpu.SemaphoreType.DMA((n, 3))], copies)


def _small_gather(small):
    def body(small_ref, small_all, send_sems, recv_sems, local_sem):
        x, y, c = _position()
        me = 4 * x + 2 * y + c
        peers = [(x ^ (m >> 2), y ^ ((m >> 1) & 1), c ^ (m & 1)) for m in range(1, N_DEV)]
        sends = [pltpu.make_async_remote_copy(
            src_ref=small_ref, dst_ref=small_all.at[me], send_sem=send_sems.at[m], recv_sem=recv_sems.at[m],
            device_id=peer, device_id_type=MESH) for m, peer in enumerate(peers)]
        own = pltpu.make_async_copy(small_ref, small_all.at[me], local_sem)
        for cp in sends + [own]:
            cp.start()
        for cp in sends:
            cp.wait_send()
        for m, (px, py, pc) in enumerate(peers):
            pltpu.make_async_remote_copy(
                src_ref=small_ref, dst_ref=small_all.at[4 * px + 2 * py + pc],
                send_sem=send_sems.at[m], recv_sem=recv_sems.at[m],
                device_id=(px, py, pc), device_id_type=MESH).wait_recv()
        own.wait()

    return _pcall(
        body, name="small_gather", in_specs=[ANY], out_specs=ANY,
        out_shape=jax.ShapeDtypeStruct((N_DEV,) + small.shape, small.dtype),
        scratch_shapes=[pltpu.SemaphoreType.DMA((7,)), pltpu.SemaphoreType.DMA((7,)),
                        pltpu.SemaphoreType.DMA])(small)


def _pair_sum(grad, got, *, name):
    _, R, C = got.shape
    tr = _pick8(R, max(2 * SUBLANES, (1 << 17) // C))

    def body(g_ref, b_ref, own_ref, ob_ref):
        north = lax.axis_index("c") == 1
        x1, y1 = lax.axis_index("x") == 1, lax.axis_index("y") == 1
        sums = []
        for chip in range(4):
            sums.append(jnp.where(north, g_ref[chip, 1], g_ref[chip, 0]) + b_ref[chip])
            ob_ref[chip] = sums[-1].astype(BF16)
        own_ref[...] = jnp.where(x1, jnp.where(y1, sums[3], sums[2]), jnp.where(y1, sums[1], sums[0]))

    spec = pl.BlockSpec((4, tr, C), lambda i: (0, i, 0))
    return _pcall(
        body, name=name, grid=(R // tr,),
        in_specs=[pl.BlockSpec((4, 2, tr, C), lambda i: (0, 0, i, 0)), spec],
        out_specs=[pl.BlockSpec((tr, C), lambda i: (i, 0)), spec],
        out_shape=[jax.ShapeDtypeStruct((R, C), F32), jax.ShapeDtypeStruct((4, R, C), BF16)],
        compiler_params=_params("parallel"))(grad.reshape(4, 2, R, C), got)


def _pick8(n, cap):
    if n <= cap:
        return n
    best = None
    for t in range(SUBLANES, cap + 1, SUBLANES):
        if n % t == 0:
            best = t
    assert best is not None, (n, cap)
    return best


def _adamw(w, m, v, parts, *, name):
    R, C = w.shape
    tr = _pick8(R, max(SUBLANES, (1 << 17) // C))
    c1 = 1.0 - ADAM_B1 ** ADAM_STEP
    c2 = 1.0 - ADAM_B2 ** ADAM_STEP
    np_ = len(parts)

    def body(w_ref, m_ref, v_ref, *refs):
        p_refs = refs[:np_]
        g_ref, d_ref, nm_ref, nv_ref = refs[np_:]
        g = None
        for p_ref in p_refs:
            terms = [p_ref[...]] if len(p_ref.shape) == 2 else [p_ref[k] for k in range(p_ref.shape[0])]
            for t in terms:
                g = t.astype(F32) if g is None else g + t.astype(F32)
        mn = ADAM_B1 * m_ref[...] + (1.0 - ADAM_B1) * g
        vn = ADAM_B2 * v_ref[...] + (1.0 - ADAM_B2) * (g * g)
        d_ref[...] = -ADAM_LR * ((mn / c1) / (jnp.sqrt(vn / c2) + ADAM_EPS) + ADAM_WD * w_ref[...])
        g_ref[...] = g
        nm_ref[...] = mn
        nv_ref[...] = vn

    spec = pl.BlockSpec((tr, C), lambda i: (i, 0))
    pspecs = [spec if p.ndim == 2 else pl.BlockSpec((p.shape[0], tr, C), lambda i: (0, i, 0)) for p in parts]
    return _pcall(
        body, name=name, grid=(R // tr,), in_specs=[spec] * 3 + pspecs, out_specs=[spec] * 4,
        out_shape=[jax.ShapeDtypeStruct((R, C), F32)] * 4,
        compiler_params=_params("parallel"))(w, m, v, *parts)


def _rows(a):
    return a.reshape(-1, LANES)


def _whole_from_columns(shards, *, name):
    S, K, n = shards.shape
    tk = _pick8(K, 1024)

    def body(s_ref, o_ref):
        o_ref[...] = s_ref[...]

    return _pcall(
        body, name=name, grid=(K // tk, S),
        in_specs=[pl.BlockSpec((None, tk, n), lambda i, s: (s, i, 0))],
        out_specs=pl.BlockSpec((tk, n), lambda i, s: (i, s)),
        out_shape=jax.ShapeDtypeStruct((K, S * n), shards.dtype),
        compiler_params=_params("parallel", "parallel"))(shards)


def _late_weights(a_w_out_rows, w_kv_cols, b_w_in_cols, b_w_out_rows):
    whole_rows = lambda g: g.reshape(g.shape[0] * g.shape[1], g.shape[2])
    return (whole_rows(a_w_out_rows), _whole_from_columns(w_kv_cols, name="w_kv_whole"),
            _whole_from_columns(b_w_in_cols, name="b_w_in_whole"), whole_rows(b_w_out_rows))


def _forward_backward(xs, target, a_norm, g_a_w_in, conv_w, conv_b, g_w_r, g_w_i, b_r, b_i, lam,
                      kv_norm, b_norm, final_norm, *, late_weights=None, late_shards=None, h_a=None):
    if h_a is None:
        (h_a,) = _rms_fwd(xs, [a_norm], name="a_norm_fwd")
    proj_a = _mm_nn(h_a, g_a_w_in, name="a_in_proj", out_dtype=F32)
    xb, h_rec, yg, *gathered = _acore_fwd(proj_a, conv_w, conv_b, g_w_r, g_w_i, b_r, b_i, lam,
                                          name="a_core_fwd", riders=late_shards or ())
    g_a_w_out, g_w_kv, g_b_w_in, g_b_w_out = _late_weights(*gathered) if late_shards else late_weights
    x1 = _mm_nn(yg, g_a_w_out, name="a_out_proj", out_dtype=F32, res=xs)
    hk, hb = _rms_fwd(x1, [kv_norm, b_norm], name="kv_b_norm_fwd")
    kv = _mm_nn(hk, g_w_kv, name="kv_proj", out_dtype=BF16)
    hd = g_b_w_in.shape[1] // 2
    q = _mm_nn(hb, g_b_w_in, name="q_proj", out_dtype=BF16, col_off=0, cols=hd)
    gate_b = _mm_nn(hb, g_b_w_in, name="b_gate_proj", out_dtype=F32, col_off=hd, cols=hd)
    o, og, ltot = _attn_fwd(q, kv, gate_b, name="attn_fwd")
    x2 = _mm_nn(og, g_b_w_out, name="b_out_proj", out_dtype=F32, res=x1)
    dx2, d_final_norm, loss_part = _final_loss(x2, target, final_norm, name="final_norm_loss")

    dog = _mm_nt(dx2, g_b_w_out, name="b_out_proj_bwd")
    dw_b_out = _mm_tn(og, dx2, name="b_out_proj_wgrad")
    dproj_b, dkv = _attn_bwd(q, kv, gate_b, o, ltot, dog, name="attn_bwd")
    dhb = _mm_nt(dproj_b, g_b_w_in, name="b_in_proj_bwd")
    dw_b_in = _mm_tn(hb, dproj_b, name="b_in_proj_wgrad", shards=N_DEV)
    dhk = _mm_nt(dkv, g_w_kv, name="kv_proj_bwd")
    dw_kv = _mm_tn(hk, dkv, name="kv_proj_wgrad", shards=N_DEV)
    early = [dw_kv, dw_b_in, dw_b_out.reshape(N_DEV, -1, dw_b_out.shape[1])] if late_shards else []
    dx1, d_b_norm, d_kv_norm, *got = _rms_bwd(x1, dx2, [dhb, dhk], [b_norm, kv_norm], name="kv_b_norm_bwd",
                                              rider=_sibling_rider(early) if early else None)
    early_sums = [_pair_sum(f_, g_, name=f"pair_sum_early_{i}") for i, (f_, g_) in enumerate(zip(early, got))]
    dyg = _mm_nt(dx1, g_a_w_out, name="a_out_proj_bwd")
    dw_a_out = _mm_tn(yg, dx1, name="a_out_proj_wgrad")
    (dproj_a, d_conv_w, d_conv_b, d_b_r, d_b_i, d_lambda, dw_r, dw_i, *early_others) = _acore_bwd(
        dyg, proj_a, xb, h_rec, conv_w, g_w_r, g_w_i, b_r, b_i, lam, name="a_core_bwd",
        rider=_chips_rider([s[1] for s in early_sums]) if early else None)
    dw_a_in = _mm_tn(h_a, dproj_a, name="a_in_proj_wgrad", shards=N_DEV)
    rows = dw_r.shape[1] // N_DEV
    lru = lambda dw: dw.reshape(-1, N_DEV, rows, dw.shape[2]).transpose(1, 0, 2, 3).reshape(N_DEV, -1, dw.shape[2])
    late = [dw_a_in, dw_a_out.reshape(N_DEV, -1, dw_a_out.shape[1]), lru(dw_r), lru(dw_i)] if late_shards else []
    dh_a, *got = _mm_nt(dproj_a, g_a_w_in, name="a_in_proj_bwd", rider=_sibling_rider(late)) if late else (
        _mm_nt(dproj_a, g_a_w_in, name="a_in_proj_bwd"),)
    late_sums = [_pair_sum(f_, g_, name=f"pair_sum_late_{i}") for i, (f_, g_) in enumerate(zip(late, got))]
    grad_x, d_a_norm, *late_others = _rms_bwd(xs, dx1, [dh_a], [a_norm], name="a_norm_bwd",
                                              rider=_chips_rider([s[1] for s in late_sums]) if late else None)
    sums = late_sums[:2] + early_sums + late_sums[2:]
    others = late_others[:2] + early_others + late_others[2:]
    return (loss_part, grad_x, dw_a_in, dw_a_out, dw_kv, dw_b_in, dw_b_out, dw_r, dw_i, d_a_norm,
            d_conv_w, d_conv_b, d_b_r, d_b_i, d_lambda, d_kv_norm, d_b_norm, d_final_norm, sums, others)


def kernel(x, a_norm, a_w_in, a_conv_w, a_conv_b, a_w_r, a_b_r, a_w_i, a_b_i, a_lambda, a_w_out, kv_norm, w_kv, b_norm, b_w_in, b_w_out, final_norm, loss_target, m_a_norm, m_a_w_in, m_a_conv_w, m_a_conv_b, m_a_w_r, m_a_b_r, m_a_w_i, m_a_b_i, m_a_lambda, m_a_w_out, m_kv_norm, m_w_kv, m_b_norm, m_b_w_in, m_b_w_out, m_final_norm, v_a_norm, v_a_w_in, v_a_conv_w, v_a_conv_b, v_a_w_r, v_a_b_r, v_a_w_i, v_a_b_i, v_a_lambda, v_a_w_out, v_kv_norm, v_w_kv, v_b_norm, v_b_w_in, v_b_w_out, v_final_norm):
    T, D = x.shape[1], x.shape[2]
    nb, bw = a_w_r.shape[1], a_w_r.shape[3]
    C = nb * bw
    me = 4 * lax.axis_index("x") + 2 * lax.axis_index("y") + lax.axis_index("c")
    xs = x[0]
    target = loss_target[0]

    rows_r = a_w_r.shape[2]
    small_f32 = jnp.concatenate([_rows(a_conv_w[0]), _rows(b_norm[0])], axis=0)
    pad = (-small_f32.shape[0]) % SUBLANES
    small_f32 = jnp.pad(small_f32, ((0, pad), (0, 0)))
    h_a, a_w_in_cols, w_r_rows, w_i_rows, small_all = _rms_fwd(
        xs, [a_norm], name="a_norm_fwd",
        riders=[a_w_in[0].astype(BF16), a_w_r[0].reshape(nb * rows_r, bw).astype(BF16),
                a_w_i[0].reshape(nb * rows_r, bw).astype(BF16), small_f32])
    late_shards = [a_w_out[0].astype(BF16), w_kv.astype(BF16), b_w_in[0].astype(BF16), b_w_out[0].astype(BF16)]
    g_a_w_in = _whole_from_columns(a_w_in_cols, name="a_w_in_whole")
    g_w_r = w_r_rows.reshape(N_DEV, nb, rows_r, bw).transpose(1, 0, 2, 3).reshape(nb, bw, bw)
    g_w_i = w_i_rows.reshape(N_DEV, nb, rows_r, bw).transpose(1, 0, 2, 3).reshape(nb, bw, bw)
    cw_rows = a_conv_w.shape[1] * a_conv_w.shape[2] // LANES
    conv_w_full = small_all[:, :cw_rows, :].reshape(N_DEV, CONV_W, a_conv_w.shape[2])
    conv_w_full = conv_w_full.transpose(1, 0, 2).reshape(CONV_W, C)
    bn_rows = b_norm.shape[1] // LANES
    b_norm_full = small_all[:, cw_rows:cw_rows + bn_rows, :].reshape(1, D)
    kv_norm2, final_norm2 = kv_norm.reshape(1, D), final_norm.reshape(1, D)

    (loss_part, grad_x, dw_a_in, dw_a_out, dw_kv, dw_b_in, dw_b_out, dw_r, dw_i, d_a_norm, d_conv_w,
     d_conv_b, d_b_r, d_b_i, d_lambda, d_kv_norm, d_b_norm, d_final_norm, sums,
     others) = _forward_backward(
         xs, target, a_norm, g_a_w_in, conv_w_full, a_conv_b, g_w_r, g_w_i, a_b_r, a_b_i, a_lambda,
         kv_norm2, b_norm_full, final_norm2, late_shards=late_shards, h_a=h_a)

    small_parts = [d_a_norm, d_conv_w, d_conv_b, d_b_r, d_b_i, d_lambda, d_kv_norm, d_b_norm, d_final_norm]
    small_sizes = [p.size // LANES for p in small_parts]
    small = jnp.concatenate([_rows(p) for p in small_parts], axis=0)
    small_everyone = _small_gather(small)

    def shard2d(w):
        return w.reshape(-1, w.shape[-1])

    names_big = [(a_w_in, m_a_w_in, v_a_w_in), (a_w_out, m_a_w_out, v_a_w_out), (w_kv, m_w_kv, v_w_kv),
                 (b_w_in, m_b_w_in, v_b_w_in), (b_w_out, m_b_w_out, v_b_w_out),
                 (a_w_r, m_a_w_r, v_a_w_r), (a_w_i, m_a_w_i, v_a_w_i)]
    upd_big = []
    for i, (w, m, v) in enumerate(names_big):
        res = _adamw(shard2d(w), shard2d(m), shard2d(v), [sums[i][0], others[i]], name=f"adamw_{i}")
        upd_big.append([r.reshape(w.shape) for r in res])

    soffs = [0]
    for s in small_sizes:
        soffs.append(soffs[-1] + s)

    def small_piece(i):
        return small_everyone[:, soffs[i]:soffs[i + 1], :]

    cw_cols = a_conv_w.shape[2]
    conv_piece = small_piece(1).reshape(N_DEV, CONV_W, C)
    conv_piece = lax.dynamic_slice_in_dim(conv_piece, me * cw_cols, cw_cols, axis=2)
    conv_piece = conv_piece.reshape(N_DEV, CONV_W * cw_cols // LANES, LANES)
    bn_piece = lax.dynamic_slice_in_dim(small_piece(7), me * bn_rows, bn_rows, axis=1)
    small_g = jnp.concatenate([small_piece(0), conv_piece, small_piece(2), small_piece(3), small_piece(4),
                               small_piece(5), small_piece(6), bn_piece, small_piece(8)], axis=1)
    small_w = [(a_norm, m_a_norm, v_a_norm), (a_conv_w, m_a_conv_w, v_a_conv_w),
               (a_conv_b, m_a_conv_b, v_a_conv_b), (a_b_r, m_a_b_r, v_a_b_r), (a_b_i, m_a_b_i, v_a_b_i),
               (a_lambda, m_a_lambda, v_a_lambda), (kv_norm, m_kv_norm, v_kv_norm),
               (b_norm, m_b_norm, v_b_norm), (final_norm, m_final_norm, v_final_norm)]
    pack = lambda idx: jnp.concatenate([_rows(t[idx]) for t in small_w], axis=0)
    res_small = _adamw(pack(0), pack(1), pack(2), [small_g], name="adamw_small")
    woffs = [0]
    for t in small_w:
        woffs.append(woffs[-1] + t[0].size // LANES)
    upd_small = [[r[woffs[i]:woffs[i + 1]].reshape(small_w[i][0].shape) for r in res_small]
                 for i in range(len(small_w))]

    order = [("s", 0), ("b", 0), ("s", 1), ("s", 2), ("b", 5), ("s", 3), ("b", 6), ("s", 4), ("s", 5),
             ("b", 1), ("s", 6), ("b", 2), ("s", 7), ("b", 3), ("b", 4), ("s", 8)]
    per_weight = [(upd_big if kind == "b" else upd_small)[i] for kind, i in order]
    loss = lax.psum(loss_part[0, 0], ("x", "y", "c"))
    result = [loss, grad_x[None]]
    for field in range(4):
        result += [u[field] for u in per_weight]
    return tuple(result)
```

```python
import math

import jax
import jax.numpy as jnp
from jax import lax
from jax.experimental import pallas as pl
from jax.experimental.pallas import tpu as pltpu

F32 = jnp.float32
BF16 = jnp.bfloat16
MESH = pl.DeviceIdType.MESH

EPS = 1e-6
LOG2E = 1.4426950408889634
WEIGHT_FLOOR_LOG2 = -200.0
LRU_C = 8.0
CONV_W = 4
HEAD_DIM = 128
ADAM_LR = 0.001
ADAM_B1 = 0.9
ADAM_B2 = 0.999
ADAM_EPS = 1e-08
ADAM_WD = 0.01
ADAM_STEP = 10

N_DEV = 8
LANES = 128
SUBLANES = 8
VMEM_LIMIT = 56 * 1024 * 1024

ATT_KEY_BLOCK = 256
ATT_QUERY_BLOCK = 256
ATT_STEP_BLOCKS = 4
SCAN_BLOCK = 256
ROW_BLOCK = 256
MM_TOKEN_BLOCK = 1024
MM_WIDE_K = 2560
MM_WEIGHT_TILE = 1280
MM_CONTRACT_TOKENS = 2048
ANY = pl.BlockSpec(memory_space=pl.ANY)


def _pcall(body, **kw):
    return pl.pallas_call(body, **kw)


def _params(*sem):
    return pltpu.CompilerParams(dimension_semantics=sem, vmem_limit_bytes=VMEM_LIMIT)


def _pick(n, cap):
    if n <= cap:
        return n
    best = None
    for t in range(LANES, cap + 1, LANES):
        if n % t == 0:
            best = t
    assert best is not None, (n, cap)
    return best


def _token_block(T, K):
    tm = MM_TOKEN_BLOCK if K <= MM_WIDE_K else MM_TOKEN_BLOCK // 2
    return tm if T % tm == 0 else T


def _sigmoid(x):
    return 1.0 / (1.0 + jnp.exp(-x))


def _dot(a, b, ca, cb):
    return lax.dot_general(a, b, (((ca,), (cb,)), ((), ())), preferred_element_type=F32)


def _mm_nn(a, w, *, name, out_dtype, col_off=0, cols=None, res=None):
    T, K = a.shape
    K2, N = w.shape
    assert K == K2
    cols = N if cols is None else cols
    tm = _token_block(T, K)
    tn = _pick(cols, MM_WEIGHT_TILE)
    assert col_off % tn == 0
    off = col_off // tn
    has_res = res is not None

    def body(a_ref, b_ref, *rest):
        o_ref = rest[-1]
        acc = jnp.dot(a_ref[...].astype(BF16), b_ref[...], preferred_element_type=F32)
        if has_res:
            acc = acc + rest[0][...]
        o_ref[...] = acc.astype(out_dtype)

    in_specs = [pl.BlockSpec((tm, K), lambda j, i: (i, 0)),
                pl.BlockSpec((K, tn), lambda j, i: (0, off + j))]
    args = [a, w]
    if has_res:
        in_specs.append(pl.BlockSpec((tm, tn), lambda j, i: (i, j)))
        args.append(res)
    return _pcall(
        body, name=name, grid=(cols // tn, T // tm), in_specs=in_specs,
        out_specs=pl.BlockSpec((tm, tn), lambda j, i: (i, j)),
        out_shape=jax.ShapeDtypeStruct((T, cols), out_dtype),
        compiler_params=_params("parallel", "parallel"))(*args)


def _mm_nt(a, w, *, name, out_dtype=F32, rider=None):
    parts = a.shape[0] if a.ndim == 3 else 1
    T, kp = a.shape[-2:]
    N, K = w.shape
    assert K == parts * kp
    tm = _token_block(T, K)
    tn = _pick(N, MM_WEIGHT_TILE)
    nj, ni = N // tn, T // tm
    rider = rider or _Rider([], [], [], None)
    nri, nro = len(rider.inputs), len(rider.out_shapes)

    def body(a_ref, b_ref, *refs):
        o_ref = refs[nri]
        j, i = pl.program_id(0), pl.program_id(1)
        if nro:
            start, finish = rider.bind(refs[:nri], refs[nri + 1:nri + 1 + nro], refs[nri + 1 + nro:])
            pl.when((j == 0) & (i == 0))(start)
        if a.ndim == 3:
            acc = None
            for p in range(parts):
                term = _dot(a_ref[p], b_ref[:, p * kp:(p + 1) * kp], 1, 1)
                acc = term if acc is None else acc + term
        else:
            acc = _dot(a_ref[...].astype(BF16), b_ref[...], 1, 1)
        o_ref[...] = acc.astype(out_dtype)
        if nro:
            pl.when((j == nj - 1) & (i == ni - 1))(finish)

    a_spec = (pl.BlockSpec((parts, tm, kp), lambda j, i: (0, i, 0)) if a.ndim == 3
              else pl.BlockSpec((tm, K), lambda j, i: (i, 0)))
    sem = ("arbitrary", "arbitrary") if nro else ("parallel", "parallel")
    out = _pcall(
        body, name=name, grid=(nj, ni),
        in_specs=[a_spec, pl.BlockSpec((tn, K), lambda j, i: (j, 0))] + [ANY] * nri,
        out_specs=[pl.BlockSpec((tm, tn), lambda j, i: (i, j))] + [ANY] * nro,
        out_shape=[jax.ShapeDtypeStruct((T, N), out_dtype)] + rider.out_shapes,
        scratch_shapes=rider.scratch,
        compiler_params=_params(*sem))(a, w, *rider.inputs)
    return out if nro else out[0]


def _mm_tn(a, b, *, name, shards=1):
    T, Ko = a.shape
    parts = b.shape[0] if b.ndim == 3 else 1
    T2, n_part = b.shape[-2:]
    N = parts * n_part
    assert T == T2
    n = N // shards
    tt = min(T, MM_CONTRACT_TOKENS)
    tko = _pick(Ko, 1024)
    span = 2 if shards > 1 and 2 * n <= MM_WEIGHT_TILE and n_part % (2 * n) == 0 else 1
    tn = span * n if span > 1 else _pick(n, 1024)
    per = max(n // tn, 1)
    assert n_part % tn == 0
    per_part = n_part // tn

    def body(a_ref, b_ref, o_ref):
        t = pl.program_id(2)
        p = _dot(a_ref[...].astype(BF16), b_ref[...].astype(BF16), 0, 0)
        pieces = [p] if span == 1 else [p[:, s * n:(s + 1) * n] for s in range(span)]

        @pl.when(t == 0)
        def _():
            for s, piece in enumerate(pieces):
                o_ref[(s,) if span > 1 else ...] = piece

        @pl.when(t > 0)
        def _():
            for s, piece in enumerate(pieces):
                o_ref[(s,) if span > 1 else ...] += piece

    if shards == 1:
        out_spec = pl.BlockSpec((tko, tn), lambda i, j, t: (i, j))
        out_shape = jax.ShapeDtypeStruct((Ko, N), F32)
    elif span > 1:
        out_spec = pl.BlockSpec((span, tko, n), lambda i, j, t: (j, i, 0))
        out_shape = jax.ShapeDtypeStruct((shards, Ko, n), F32)
    else:
        out_spec = pl.BlockSpec((None, tko, tn), lambda i, j, t: (j // per, i, j % per))
        out_shape = jax.ShapeDtypeStruct((shards, Ko, n), F32)
    b_spec = (pl.BlockSpec((None, tt, tn), lambda i, j, t: (j // per_part, t, j % per_part)) if b.ndim == 3
              else pl.BlockSpec((tt, tn), lambda i, j, t: (t, j)))
    return _pcall(
        body, name=name, grid=(Ko // tko, N // tn, T // tt),
        in_specs=[pl.BlockSpec((tt, tko), lambda i, j, t: (t, i)), b_spec],
        out_specs=out_spec, out_shape=out_shape,
        compiler_params=_params("parallel", "parallel", "arbitrary"))(a, b)


def _rms_fwd(x, gains, *, name, riders=()):
    T, D = x.shape
    tm = min(T, ROW_BLOCK)
    steps = T // tm
    n, nr = len(gains), len(riders)

    def body(x_ref, *refs):
        g_refs, rider_in = refs[:n], refs[n:n + nr]
        o_refs, rider_out = refs[n + nr:2 * n + nr], refs[2 * n + nr:2 * n + 2 * nr]
        if nr:
            gather = _Gather(rider_in, rider_out, *refs[2 * n + 2 * nr:])
            pl.when(pl.program_id(0) == 0)(gather.start)
        xv = x_ref[...]
        xh = xv * lax.rsqrt(jnp.mean(xv * xv, axis=-1, keepdims=True) + EPS)
        for g_ref, o_ref in zip(g_refs, o_refs):
            o_ref[...] = (xh * g_ref[...]).astype(BF16)
        if nr:
            pl.when(pl.program_id(0) == steps - 1)(gather.finish)

    row = pl.BlockSpec((tm, D), lambda i: (i, 0))
    vec = pl.BlockSpec((1, D), lambda i: (0, 0))
    return _pcall(
        body, name=name, grid=(steps,), in_specs=[row] + [vec] * n + [ANY] * nr,
        out_specs=[row] * n + [ANY] * nr,
        out_shape=[jax.ShapeDtypeStruct((T, D), BF16)] * n
                  + [jax.ShapeDtypeStruct((N_DEV,) + r.shape, r.dtype) for r in riders],
        scratch_shapes=_Gather.scratch(nr) if nr else [],
        compiler_params=_params("arbitrary" if nr else "parallel"))(x, *gains, *riders)


def _rms_bwd(x, dres, dhs, gains, *, name, rider=None, bf16_copy=False):
    T, D = x.shape
    tm = min(T, ROW_BLOCK)
    steps = T // tm
    n = len(gains)
    rider = rider or _Rider([], [], [], None)
    nri, nro = len(rider.inputs), len(rider.out_shapes)
    ncopy = 1 if bf16_copy else 0

    def body(x_ref, dres_ref, *refs):
        dh_refs, g_refs = refs[:n], refs[n:2 * n]
        refs = refs[2 * n:]
        rider_in, refs = refs[:nri], refs[nri:]
        dx_ref, dg_refs = refs[0], refs[1:1 + n]
        rider_out, refs = refs[1 + n:1 + n + nro], refs[1 + n + nro:]
        copy_refs, sems = refs[:ncopy], refs[ncopy:]
        i = pl.program_id(0)
        if nro:
            start, finish = rider.bind(rider_in, rider_out, sems)
            pl.when(i == 0)(start)
        xv = x_ref[...]
        r = lax.rsqrt(jnp.mean(xv * xv, axis=-1, keepdims=True) + EPS)
        xh = xv * r
        dxh = jnp.zeros_like(xv)
        for dh_ref, g_ref, dg_ref in zip(dh_refs, g_refs, dg_refs):
            dh = dh_ref[...]
            part = jnp.sum(dh * xh, axis=0, keepdims=True)

            @pl.when(i == 0)
            def _():
                dg_ref[...] = part

            @pl.when(i > 0)
            def _():
                dg_ref[...] += part

            dxh = dxh + dh * g_ref[...]
        dx = dres_ref[...] + r * (dxh - xh * jnp.mean(dxh * xh, axis=-1, keepdims=True))
        dx_ref[...] = dx
        for copy_ref in copy_refs:
            copy_ref[...] = dx.astype(BF16)
        if nro:
            pl.when(i == steps - 1)(finish)

    row = pl.BlockSpec((tm, D), lambda i: (i, 0))
    vec = pl.BlockSpec((1, D), lambda i: (0, 0))
    return _pcall(
        body, name=name, grid=(steps,), in_specs=[row, row] + [row] * n + [vec] * n + [ANY] * nri,
        out_specs=[row] + [vec] * n + [ANY] * nro + [row] * ncopy,
        out_shape=[jax.ShapeDtypeStruct((T, D), F32)] + [jax.ShapeDtypeStruct((1, D), F32)] * n
                  + rider.out_shapes + [jax.ShapeDtypeStruct((T, D), BF16)] * ncopy,
        scratch_shapes=rider.scratch,
        compiler_params=_params("arbitrary"))(x, dres, *dhs, *gains, *rider.inputs)


def _final_loss(x, target, gain, *, name):
    T, D = x.shape
    tm = min(T, ROW_BLOCK)

    def body(x_ref, t_ref, g_ref, dx_ref, dg_ref, loss_ref, dxb_ref):
        i = pl.program_id(0)
        xv = x_ref[...]
        g = g_ref[...]
        r = lax.rsqrt(jnp.mean(xv * xv, axis=-1, keepdims=True) + EPS)
        xh = xv * r
        err = xh * g - t_ref[...]
        part_loss = 0.5 * jnp.sum(jnp.mean(err * err, axis=-1, keepdims=True), axis=0, keepdims=True)
        dy = err * (1.0 / D)
        part_g = jnp.sum(dy * xh, axis=0, keepdims=True)

        @pl.when(i == 0)
        def _():
            dg_ref[...] = part_g
            loss_ref[...] = jnp.broadcast_to(part_loss, loss_ref.shape)

        @pl.when(i > 0)
        def _():
            dg_ref[...] += part_g
            loss_ref[...] += jnp.broadcast_to(part_loss, loss_ref.shape)

        dxh = dy * g
        dx = r * (dxh - xh * jnp.mean(dxh * xh, axis=-1, keepdims=True))
        dx_ref[...] = dx
        dxb_ref[...] = dx.astype(BF16)

    row = pl.BlockSpec((tm, D), lambda i: (i, 0))
    vec = pl.BlockSpec((1, D), lambda i: (0, 0))
    return _pcall(
        body, name=name, grid=(T // tm,), in_specs=[row, row, vec],
        out_specs=[row, vec, pl.BlockSpec((1, LANES), lambda i: (0, 0)), row],
        out_shape=[jax.ShapeDtypeStruct((T, D), F32), jax.ShapeDtypeStruct((1, D), F32),
                   jax.ShapeDtypeStruct((1, LANES), F32), jax.ShapeDtypeStruct((T, D), BF16)],
        compiler_params=_params("arbitrary"))(x, target, gain)


def _shift_down(x, prev_tail, j, row):
    tb = x.shape[0]
    prev = jnp.tile(prev_tail, (tb // SUBLANES, 1))
    return jnp.where(row >= j, pltpu.roll(x, j, 0), pltpu.roll(prev, j, 0))


def _shift_up(x, next_head, j, row):
    tb = x.shape[0]
    nxt = jnp.tile(next_head, (tb // SUBLANES, 1))
    return jnp.where(row < tb - j, pltpu.roll(x, tb - j, 0), pltpu.roll(nxt, tb - j, 0))


def _lru_gates(xb, wr, wi, br, bi, lam):
    xbb = xb.astype(BF16)
    r = _sigmoid(jnp.dot(xbb, wr, preferred_element_type=F32) + br)
    i = _sigmoid(jnp.dot(xbb, wi, preferred_element_type=F32) + bi)
    sp = jnp.maximum(-lam, 0.0) + jnp.log1p(jnp.exp(-jnp.abs(lam)))
    log_a = (-LRU_C) * r * sp
    a = jnp.exp(log_a)
    a2 = a * a
    mult = jnp.sqrt(jnp.maximum(-jnp.tanh(log_a) * (1.0 + a2), 0.0))
    return xbb, r, i, sp, a, a2, mult


def _scan_rows(coef, val, edge, reverse):
    tb, C = coef.shape
    a, b = coef, val
    row = lax.broadcasted_iota(jnp.int32, (tb, C), 0)
    s = 1
    while s < tb:
        m = (row < tb - s) if reverse else (row >= s)
        shift = tb - s if reverse else s
        b = jnp.where(m, a * pltpu.roll(b, shift, 0) + b, b)
        a = jnp.where(m, a * pltpu.roll(a, shift, 0), a)
        s *= 2
    return b + a * edge


def _acore_fwd(proj, conv_w, conv_b, w_r, w_i, b_r, b_i, lam, *, name, riders=()):
    T, C2 = proj.shape
    C = C2 // 2
    nb, bw, _ = w_r.shape
    tb = min(T, SCAN_BLOCK)
    nt = T // tb
    nr = len(riders)

    def body(xp_ref, gate_ref, cw_ref, cb_ref, wr_ref, wi_ref, br_ref, bi_ref, lam_ref, *refs):
        rider_in, refs = refs[:nr], refs[nr:]
        xb_ref, h_ref, yg_ref = refs[:3]
        rider_out, refs = refs[3:3 + nr], refs[3 + nr:]
        tail_ref, hlast_ref = refs[:2]
        t = pl.program_id(1)
        if nr:
            gather = _Gather(rider_in, rider_out, *refs[2:])
            pl.when((pl.program_id(0) == 0) & (t == 0))(gather.start)

        @pl.when(t == 0)
        def _():
            tail_ref[...] = jnp.zeros_like(tail_ref)
            hlast_ref[...] = jnp.zeros_like(hlast_ref)

        row = lax.broadcasted_iota(jnp.int32, (tb, bw), 0)
        xp = xp_ref[...]
        tail = tail_ref[...]
        xb = cb_ref[...] + cw_ref[CONV_W - 1:CONV_W, :] * xp
        for j in range(1, CONV_W):
            xb = xb + cw_ref[CONV_W - 1 - j:CONV_W - j, :] * _shift_down(xp, tail, j, row)
        tail_ref[...] = xp[tb - SUBLANES:, :]
        xb_ref[...] = xb

        _, r, i, sp, a, a2, mult = _lru_gates(xb, wr_ref[...], wi_ref[...], br_ref[...], bi_ref[...],
                                              lam_ref[...])
        h = _scan_rows(a, mult * (i * xb), hlast_ref[SUBLANES - 1:SUBLANES, :], False)
        hlast_ref[...] = h[tb - SUBLANES:, :]
        h_ref[...] = h
        gate = gate_ref[...]
        yg_ref[...] = (h * (gate * _sigmoid(gate))).astype(BF16)
        if nr:
            pl.when((pl.program_id(0) == nb - 1) & (t == nt - 1))(gather.finish)

    blk = lambda off: pl.BlockSpec((tb, bw), lambda n, t: (t, off + n))
    vec = pl.BlockSpec((1, bw), lambda n, t: (0, n))
    wspec = pl.BlockSpec((None, bw, bw), lambda n, t: (n, 0, 0))
    return _pcall(
        body, name=name, grid=(nb, nt),
        in_specs=[blk(0), blk(nb), pl.BlockSpec((CONV_W, bw), lambda n, t: (0, n)), vec, wspec, wspec,
                  vec, vec, vec] + [ANY] * nr,
        out_specs=[blk(0), blk(0), blk(0)] + [ANY] * nr,
        out_shape=[jax.ShapeDtypeStruct((T, C), F32), jax.ShapeDtypeStruct((T, C), F32),
                   jax.ShapeDtypeStruct((T, C), BF16)]
                  + [jax.ShapeDtypeStruct((N_DEV,) + r.shape, r.dtype) for r in riders],
        scratch_shapes=[pltpu.VMEM((SUBLANES, bw), F32), pltpu.VMEM((SUBLANES, bw), F32)]
                       + (_Gather.scratch(nr) if nr else []),
        compiler_params=_params("arbitrary" if nr else "parallel", "arbitrary"))(
            proj, proj, conv_w, conv_b, w_r, w_i, b_r, b_i, lam, *riders)


def _acore_bwd(dyg, proj, xb_all, h_all, conv_w, w_r, w_i, b_r, b_i, lam, *, name, rider=None):
    T, C2 = proj.shape
    C = C2 // 2
    nb, bw, _ = w_r.shape
    tb = min(T, SCAN_BLOCK)
    nt = T // tb
    per8 = tb // SUBLANES
    rider = rider or _Rider([], [], [], None)
    nri, nro = len(rider.inputs), len(rider.out_shapes)

    def body(dyg_ref, xp_ref, gate_ref, xb_ref, h_ref, xp_prev_ref, h_prev_ref, cw_ref,
             wr_ref, wi_ref, br_ref, bi_ref, lam_ref, *refs):
        rider_in, refs = refs[:nri], refs[nri:]
        dproj_ref, dcw_ref, dcb_ref, dbr_ref, dbi_ref, dlam_ref, dwr_ref, dwi_ref = refs[:8]
        rider_out, refs = refs[8:8 + nro], refs[8 + nro:]
        gh_next_ref, a_next_ref, dxb_next_ref = refs[:3]
        step = pl.program_id(1)
        first_block = step == nt - 1
        if nro:
            start, finish = rider.bind(rider_in, rider_out, refs[3:])
            pl.when((pl.program_id(0) == 0) & (step == 0))(start)

        @pl.when(step == 0)
        def _():
            gh_next_ref[...] = jnp.zeros_like(gh_next_ref)
            a_next_ref[...] = jnp.zeros_like(a_next_ref)
            dxb_next_ref[...] = jnp.zeros_like(dxb_next_ref)

        row = lax.broadcasted_iota(jnp.int32, (tb, bw), 0)
        keep = jnp.where(first_block, 0.0, 1.0)
        h_prev = h_prev_ref[...] * keep
        xp_prev = xp_prev_ref[...] * keep
        xp, gate, xb, h, dyg_v = xp_ref[...], gate_ref[...], xb_ref[...], h_ref[...], dyg_ref[...]
        lam_v = lam_ref[...]
        wr, wi = wr_ref[...], wi_ref[...]

        sg = _sigmoid(gate)
        dh = dyg_v * (gate * sg)
        dproj_ref[1] = (dyg_v * h * (sg * (1.0 + gate * (1.0 - sg)))).astype(BF16)

        xbb, r, i, sp, a, a2, mult = _lru_gates(xb, wr, wi, br_ref[...], bi_ref[...], lam_v)

        gh = _scan_rows(_shift_up(a, a_next_ref[...], 1, row), dh, gh_next_ref[0:1, :], True)
        gh_next_ref[...] = gh[0:SUBLANES, :]
        a_next_ref[...] = a[0:SUBLANES, :]

        da = gh * _shift_down(h, h_prev, 1, row)
        dmult = gh * (i * xb)
        di = gh * mult * xb
        dxb = gh * mult * i
        dla = da * a - dmult * jnp.where(mult > 0.0, a2 / mult, 0.0)
        dr = dla * ((-LRU_C) * sp)
        dsp = jnp.sum(dla * ((-LRU_C) * r), axis=0, keepdims=True)
        dlam_part = dsp * (-_sigmoid(-lam_v))
        dpr = dr * r * (1.0 - r)
        dpi = di * i * (1.0 - i)
        dbr_part = jnp.sum(dpr, axis=0, keepdims=True)
        dbi_part = jnp.sum(dpi, axis=0, keepdims=True)
        dprb, dpib = dpr.astype(BF16), dpi.astype(BF16)
        dwr_part = _dot(xbb, dprb, 0, 0)
        dwi_part = _dot(xbb, dpib, 0, 0)
        dxb = dxb + _dot(dprb, wr, 1, 1) + _dot(dpib, wi, 1, 1)

        dxb_next = dxb_next_ref[...]
        dxp = cw_ref[CONV_W - 1:CONV_W, :] * dxb
        for j in range(1, CONV_W):
            dxp = dxp + cw_ref[CONV_W - 1 - j:CONV_W - j, :] * _shift_up(dxb, dxb_next, j, row)
        dxb_next_ref[...] = dxb[0:SUBLANES, :]
        dproj_ref[0] = dxp.astype(BF16)
        dcb_part = jnp.sum(dxb, axis=0, keepdims=True)
        dcw_rows = []
        for k in range(CONV_W):
            j = CONV_W - 1 - k
            sh = xp if j == 0 else _shift_down(xp, xp_prev, j, row)
            dcw_rows.append(jnp.sum(dxb * sh, axis=0, keepdims=True))

        @pl.when(step == 0)
        def _():
            for k in range(CONV_W):
                dcw_ref[k:k + 1, :] = dcw_rows[k]
            dcb_ref[...] = dcb_part
            dbr_ref[...] = dbr_part
            dbi_ref[...] = dbi_part
            dlam_ref[...] = dlam_part
            dwr_ref[...] = dwr_part
            dwi_ref[...] = dwi_part

        @pl.when(step > 0)
        def _():
            for k in range(CONV_W):
                dcw_ref[k:k + 1, :] += dcw_rows[k]
            dcb_ref[...] += dcb_part
            dbr_ref[...] += dbr_part
            dbi_ref[...] += dbi_part
            dlam_ref[...] += dlam_part
            dwr_ref[...] += dwr_part
            dwi_ref[...] += dwi_part

        if nro:
            pl.when((pl.program_id(0) == nb - 1) & (step == nt - 1))(finish)

    rev = lambda s: nt - 1 - s
    blk = lambda off: pl.BlockSpec((tb, bw), lambda n, s: (rev(s), off + n))
    prev8 = lambda off: pl.BlockSpec(
        (SUBLANES, bw), lambda n, s: (jnp.maximum(rev(s) * per8 - 1, 0), off + n))
    vec = pl.BlockSpec((1, bw), lambda n, s: (0, n))
    wspec = pl.BlockSpec((None, bw, bw), lambda n, s: (n, 0, 0))
    cwspec = pl.BlockSpec((CONV_W, bw), lambda n, s: (0, n))
    vshape = jax.ShapeDtypeStruct((1, C), F32)
    wshape = jax.ShapeDtypeStruct((nb, bw, bw), F32)
    return _pcall(
        body, name=name, grid=(nb, nt),
        in_specs=[blk(0), blk(0), blk(nb), blk(0), blk(0), prev8(0), prev8(0), cwspec,
                  wspec, wspec, vec, vec, vec] + [ANY] * nri,
        out_specs=[pl.BlockSpec((2, tb, bw), lambda n, s: (0, rev(s), n)), cwspec, vec, vec, vec, vec,
                   wspec, wspec] + [ANY] * nro,
        out_shape=[jax.ShapeDtypeStruct((2, T, C), BF16),
                   jax.ShapeDtypeStruct((CONV_W, C), F32), vshape, vshape, vshape, vshape,
                   wshape, wshape] + rider.out_shapes,
        scratch_shapes=[pltpu.VMEM((SUBLANES, bw), F32)] * 3 + rider.scratch,
        compiler_params=_params("arbitrary" if nro else "parallel", "arbitrary"))(
            dyg, proj, proj, xb_all, h_all, proj, h_all, conv_w, w_r, w_i, b_r, b_i, lam, *rider.inputs)


def _later_sum(lk, tri):
    return jnp.dot(lk.astype(BF16), tri, preferred_element_type=F32)


def _log2_sigmoids(y):
    t = jnp.log(1.0 + jnp.exp2(-jnp.abs(y))) * LOG2E
    ls = jnp.minimum(y, 0.0) - t
    return ls, ls - y


def _attn_blocks(T):
    bk = min(T, ATT_KEY_BLOCK)
    bq = min(T, ATT_QUERY_BLOCK)
    return bq, bk, bq // bk


def _attn_step_blocks(T, bq):
    return ATT_STEP_BLOCKS if (T // bq) % ATT_STEP_BLOCKS == 0 else 1


def _attn_fwd(q, kv, gate, *, name):
    T, HD = q.shape
    H = HD // HEAD_DIM
    bq, bk, per = _attn_blocks(T)
    scale = 1.0 / math.sqrt(HEAD_DIM)

    nsub = _attn_step_blocks(T, bq)

    def body(q_ref, k_ref, v_ref, g_ref, o_ref, og_ref, lt_ref, w_ref):
        for sub in range(nsub):
            rows = pl.ds(sub * bq, bq)
            one_block(pl.program_id(1) * nsub + sub, q_ref.at[rows], k_ref, v_ref, g_ref.at[rows],
                      o_ref.at[rows], og_ref.at[rows], lt_ref.at[rows], w_ref)

    def one_block(i, q_ref, k_ref, v_ref, g_ref, o_ref, og_ref, lt_ref, w_ref):
        qv = q_ref[...]
        tr = lax.broadcasted_iota(jnp.int32, (bk, bk), 0)
        tc = lax.broadcasted_iota(jnp.int32, (bk, bk), 1)
        tri = (tr > tc).astype(BF16)
        ahead = (lax.broadcasted_iota(jnp.int32, (bq, bk), 0)
                 - lax.broadcasted_iota(jnp.int32, (bq, bk), 1))

        def starts_of(top):
            return [pl.multiple_of((top - d) * bk, bk) for d in range(per)]

        def scores(top):
            return [_dot(qv, k_ref[pl.ds(ks, bk), :], 1, 1) for ks in starts_of(top)]

        def weights(top, zs, c, mask):
            lss, sums, css, causals = [], [], [], []
            for ks, z in zip(starts_of(top), zs):
                ls, lk = _log2_sigmoids(z * (scale * LOG2E))
                if mask:
                    causals.append(ahead > ks - i * bq)
                    lk = jnp.where(causals[-1], lk, 0.0)
                lss.append(ls)
                sums.append(jnp.sum(lk, axis=1, keepdims=True))
                css.append(_later_sum(lk, tri))
            for d in range(per):
                w = jnp.exp2(lss[d] + (css[d] + c))
                if mask:
                    w = jnp.where(causals[d], w, 0.0)
                w_ref[d] = w.astype(BF16)
                c = c + sums[d]
            return c

        def values(top, acc):
            for d, ks in enumerate(starts_of(top)):
                acc = acc + jnp.dot(w_ref[d], v_ref[pl.ds(ks, bk), :], preferred_element_type=F32)
            return acc

        def more(state):
            gg, _, _, largest = state
            return (gg <= i) & (largest > WEIGHT_FLOOR_LOG2)

        def step(state):
            gg, acc, c, _ = state
            top = (i - gg) * per + per - 1
            zs = scores(top)
            acc = values(top + per, acc)
            c = weights(top, zs, c, False)
            return gg + 1, acc, c, jnp.max(c)

        diag_top = i * per + per - 1
        c = weights(diag_top, scores(diag_top), jnp.zeros((bq, 1), F32), True)
        gg, acc, c, _ = lax.while_loop(more, step, (1, jnp.zeros((bq, HEAD_DIM), F32), c, jnp.max(c)))
        acc = values((i - gg + 1) * per + per - 1, acc)
        o_ref[...] = acc
        g = g_ref[...]
        og_ref[...] = (acc * (g * _sigmoid(g))).astype(BF16)
        lane = lax.broadcasted_iota(jnp.int32, (bq, HEAD_DIM), 1)
        lt_ref[...] = jnp.where(lane == 1, (i - gg + 1).astype(F32), jnp.broadcast_to(c, (bq, HEAD_DIM)))

    qspec = pl.BlockSpec((nsub * bq, HEAD_DIM), lambda h, i: (i, h))
    return _pcall(
        body, name=name, grid=(H, T // (nsub * bq)),
        in_specs=[qspec, pl.BlockSpec((T, HEAD_DIM), lambda h, i: (0, h)),
                  pl.BlockSpec((T, HEAD_DIM), lambda h, i: (0, H + h)), qspec],
        out_specs=[qspec, qspec, qspec],
        out_shape=[jax.ShapeDtypeStruct((T, HD), F32), jax.ShapeDtypeStruct((T, HD), BF16),
                   jax.ShapeDtypeStruct((T, HD), F32)],
        scratch_shapes=[pltpu.VMEM((per, bq, bk), BF16)],
        compiler_params=_params("parallel", "arbitrary"))(q, kv, kv, gate)


def _attn_bwd(q, kv, gate, o, ltot, dog, *, name):
    T, HD = q.shape
    H = HD // HEAD_DIM
    bq, bk, per = _attn_blocks(T)
    nq = T // bq
    scale = 1.0 / math.sqrt(HEAD_DIM)

    nsub = _attn_step_blocks(T, bq)

    def body(q_ref, k_ref, v_ref, g_ref, o_ref, lt_ref, dog_ref,
             dqg_ref, dkv_ref, dk_acc, dv_acc, dz_ref, w_ref):
        for sub in range(nsub):
            rows = pl.ds(sub * bq, bq)
            one_block(pl.program_id(1) * nsub + sub, q_ref.at[rows], k_ref, v_ref, g_ref.at[rows],
                      o_ref.at[rows], lt_ref.at[rows], dog_ref.at[rows], dqg_ref.at[:, rows], dkv_ref,
                      dk_acc, dv_acc, dz_ref, w_ref)

    def one_block(i, q_ref, k_ref, v_ref, g_ref, o_ref, lt_ref, dog_ref,
                  dqg_ref, dkv_ref, dk_acc, dv_acc, dz_ref, w_ref):
        @pl.when(i == 0)
        def _():
            dk_acc[...] = jnp.zeros_like(dk_acc)
            dv_acc[...] = jnp.zeros_like(dv_acc)

        qv = q_ref[...]
        g, ov, dogv = g_ref[...], o_ref[...], dog_ref[...]
        sg = _sigmoid(g)
        do = dogv * (g * sg)
        dqg_ref[1] = (dogv * ov * (sg * (1.0 + g * (1.0 - sg)))).astype(BF16)
        dob = do.astype(BF16)
        ltot_v = lt_ref[:, 0:1]
        tr = lax.broadcasted_iota(jnp.int32, (bk, bk), 0)
        tc = lax.broadcasted_iota(jnp.int32, (bk, bk), 1)
        tri_later = (tr > tc).astype(BF16)
        tri_excl = (tr < tc).astype(BF16)
        ahead = (lax.broadcasted_iota(jnp.int32, (bq, bk), 0)
                 - lax.broadcasted_iota(jnp.int32, (bq, bk), 1))

        def starts_of(first):
            return [pl.multiple_of((first + d) * bk, bk) for d in range(per)]

        def scores(first):
            return ([_dot(qv, k_ref[pl.ds(ks, bk), :], 1, 1) for ks in starts_of(first)],
                    [_dot(dob, v_ref[pl.ds(ks, bk), :], 1, 1) for ks in starts_of(first)])

        def front(first, zs, dws, p_lk, p_g, mask):
            lss, css, causals = [], [], []
            for ks, z in zip(starts_of(first), zs):
                ls, lk = _log2_sigmoids(z * (scale * LOG2E))
                if mask:
                    causals.append(ahead > ks - i * bq)
                    lk = jnp.where(causals[-1], lk, 0.0)
                lss.append(ls)
                p_lk = p_lk + jnp.sum(lk, axis=1, keepdims=True)
                css.append((ltot_v - p_lk) + _later_sum(lk, tri_later))
            gms, befores = [], []
            for d in range(per):
                w = jnp.exp2(lss[d] + css[d])
                if mask:
                    w = jnp.where(causals[d], w, 0.0)
                gm = dws[d] * w
                gms.append(gm)
                w_ref[d] = w.astype(BF16)
                befores.append(jnp.dot(gm.astype(BF16), tri_excl, preferred_element_type=F32) + p_g)
                p_g = p_g + jnp.sum(gm, axis=1, keepdims=True)
            for d in range(per):
                dz = gms[d] - jnp.exp2(lss[d]) * (gms[d] + befores[d])
                if mask:
                    dz = jnp.where(causals[d], dz, 0.0)
                dz_ref[d] = (dz * scale).astype(BF16)
            return p_lk, p_g

        def back(first, dq):
            for d, ks in enumerate(starts_of(first)):
                dzb = dz_ref[d]
                dq = dq + jnp.dot(dzb, k_ref[pl.ds(ks, bk), :], preferred_element_type=F32)
                dk_acc[pl.ds(ks, bk), :] += _dot(dzb, qv, 0, 0)
                dv_acc[pl.ds(ks, bk), :] += _dot(w_ref[d], dob, 0, 0)
            return dq

        def step(mask):
            def trip(g, state):
                dq, p_lk, p_g = state
                zs, dws = scores(g * per)
                dq = back((g - 1) * per, dq)
                return (dq,) + front(g * per, zs, dws, p_lk, p_g, mask)
            return trip

        g0 = jnp.max(lt_ref[0:1, 1:2]).astype(jnp.int32)
        zero = jnp.zeros((bq, 1), F32)
        state = (jnp.zeros((bq, HEAD_DIM), F32),) + front(g0 * per, *scores(g0 * per), zero, zero, True)
        state = lax.fori_loop(g0 + 1, i, step(False), state)
        state = lax.fori_loop(jnp.maximum(i, g0 + 1), i + 1, step(True), state)
        dqg_ref[0] = back(i * per, state[0]).astype(BF16)

        @pl.when(i == nq - 1)
        def _():
            dkv_ref[0] = dk_acc[...].astype(BF16)
            dkv_ref[1] = dv_acc[...].astype(BF16)

    qspec = pl.BlockSpec((nsub * bq, HEAD_DIM), lambda h, i: (i, h))
    kspec = pl.BlockSpec((T, HEAD_DIM), lambda h, i: (0, h))
    return _pcall(
        body, name=name, grid=(H, nq // nsub),
        in_specs=[qspec, kspec, pl.BlockSpec((T, HEAD_DIM), lambda h, i: (0, H + h)),
                  qspec, qspec, qspec, qspec],
        out_specs=[pl.BlockSpec((2, nsub * bq, HEAD_DIM), lambda h, i: (0, i, h)),
                   pl.BlockSpec((2, T, HEAD_DIM), lambda h, i: (0, 0, h))],
        out_shape=[jax.ShapeDtypeStruct((2, T, HD), BF16)] * 2,
        scratch_shapes=[pltpu.VMEM((T, HEAD_DIM), F32)] * 2 + [pltpu.VMEM((per, bq, bk), BF16)] * 2,
        compiler_params=_params("parallel", "arbitrary"))(q, kv, kv, gate, o, ltot, dog)


def _position():
    return lax.axis_index("x"), lax.axis_index("y"), lax.axis_index("c")


def _chip_of(k, x, y):
    return (1 - x if k & 1 else x), (1 - y if k & 2 else y)


class _Gather:
    @staticmethod
    def scratch(n):
        return [pltpu.SemaphoreType.DMA((n, 7)), pltpu.SemaphoreType.DMA((n, 7)),
                pltpu.SemaphoreType.DMA((n,))]

    def __init__(self, ins, outs, send_sems, recv_sems, local_sems):
        self.ins, self.outs, self.n = ins, outs, len(ins)
        self.send_sems, self.recv_sems, self.local_sems = send_sems, recv_sems, local_sems
        x, y, c = _position()
        self.me, self.sibling = (x, y, c), (x, y, 1 - c)
        self.chips = [_chip_of(k, x, y) for k in (1, 2, 3)]

    def copy(self, a, k, block, to, src=None):
        slot = self.outs[a].at[4 * block[0] + 2 * block[1] + block[2]]
        return pltpu.make_async_remote_copy(
            src_ref=slot if src is None else src, dst_ref=slot,
            send_sem=self.send_sems.at[a, k], recv_sem=self.recv_sems.at[a, k],
            device_id=to, device_id_type=MESH)

    def own_copies(self):
        x, y, c = self.me
        mine = [pltpu.make_async_copy(self.ins[a], self.outs[a].at[4 * x + 2 * y + c], self.local_sems.at[a])
                for a in range(self.n)]
        first = []
        for a in range(self.n):
            first.append(self.copy(a, 0, self.me, self.sibling, src=self.ins[a]))
            first += [self.copy(a, 1 + j, self.me, (*chip, c), src=self.ins[a])
                      for j, chip in enumerate(self.chips)]
        return mine, first

    def start(self):
        mine, first = self.own_copies()
        for cp in mine + first:
            cp.start()

    def finish(self):
        c = self.me[2]
        mine, first = self.own_copies()
        passed = []
        for j, chip in enumerate(self.chips):
            for a in range(self.n):
                self.copy(a, 1 + j, (*chip, c), self.me).wait_recv()
                fwd = self.copy(a, 4 + j, (*chip, c), self.sibling)
                fwd.start()
                passed.append(fwd)
        for a in range(self.n):
            self.copy(a, 0, self.sibling, self.me).wait_recv()
            for j, chip in enumerate(self.chips):
                self.copy(a, 4 + j, (*chip, 1 - c), self.me).wait_recv()
        for cp in first + passed:
            cp.wait_send()
        for cp in mine:
            cp.wait()


class _Rider:
    def __init__(self, inputs, out_shapes, scratch, copies):
        self.inputs, self.out_shapes, self.scratch, self.copies = inputs, out_shapes, scratch, copies

    def bind(self, ins, outs, sems):
        def start():
            for cp in self.copies(ins, outs, sems):
                cp.start()

        def finish():
            cps = self.copies(ins, outs, sems)
            for cp in cps:
                cp.wait_send()
            for cp in cps:
                cp.wait_recv()

        return start, finish


def _sibling_rider(grads):
    n = len(grads)

    def copies(ins, outs, sems):
        x, y, c = _position()
        return [pltpu.make_async_remote_copy(
            src_ref=ins[a].at[2 * chip + (1 - c)], dst_ref=outs[a].at[chip],
            send_sem=sems[0].at[a, chip], recv_sem=sems[1].at[a, chip],
            device_id=(x, y, 1 - c), device_id_type=MESH) for a in range(n) for chip in range(4)]

    return _Rider(list(grads), [jax.ShapeDtypeStruct((4,) + g.shape[1:], g.dtype) for g in grads],
                  [pltpu.SemaphoreType.DMA((n, 4)), pltpu.SemaphoreType.DMA((n, 4))], copies)


def _chips_rider(parts):
    n = len(parts)

    def copies(ins, outs, sems):
        x, y, c = _position()
        cps = []
        for a in range(n):
            for k in range(3):
                cx, cy = _chip_of(k + 1, x, y)
                cps.append(pltpu.make_async_remote_copy(
                    src_ref=ins[a].at[2 * cx + cy], dst_ref=outs[a].at[k],
                    send_sem=sems[0].at[a, k], recv_sem=sems[1].at[a, k],
                    device_id=(cx, cy, c), device_id_type=MESH))
        return cps

    return _Rider(list(parts), [jax.ShapeDtypeStruct((3,) + p.shape[1:], p.dtype) for p in parts],
                  [pltpu.SemaphoreType.DMA((n, 3)), pltpu.SemaphoreType.DMA((n, 3))], copies)


def _small_gather(small):
    def body(small_ref, small_all, send_sems, recv_sems, local_sem):
        x, y, c = _position()
        me = 4 * x + 2 * y + c
        peers = [(x ^ (m >> 2), y ^ ((m >> 1) & 1), c ^ (m & 1)) for m in range(1, N_DEV)]
        sends = [pltpu.make_async_remote_copy(
            src_ref=small_ref, dst_ref=small_all.at[me], send_sem=send_sems.at[m], recv_sem=recv_sems.at[m],
            device_id=peer, device_id_type=MESH) for m, peer in enumerate(peers)]
        own = pltpu.make_async_copy(small_ref, small_all.at[me], local_sem)
        for cp in sends + [own]:
            cp.start()
        for cp in sends:
            cp.wait_send()
        for m, (px, py, pc) in enumerate(peers):
            pltpu.make_async_remote_copy(
                src_ref=small_ref, dst_ref=small_all.at[4 * px + 2 * py + pc],
                send_sem=send_sems.at[m], recv_sem=recv_sems.at[m],
                device_id=(px, py, pc), device_id_type=MESH).wait_recv()
        own.wait()

    return _pcall(
        body, name="small_gather", in_specs=[ANY], out_specs=ANY,
        out_shape=jax.ShapeDtypeStruct((N_DEV,) + small.shape, small.dtype),
        scratch_shapes=[pltpu.SemaphoreType.DMA((7,)), pltpu.SemaphoreType.DMA((7,)),
                        pltpu.SemaphoreType.DMA])(small)


def _pair_sum(grad, got, *, name):
    _, R, C = got.shape
    tr = _pick8(R, max(2 * SUBLANES, (1 << 17) // C))

    def body(g_ref, b_ref, own_ref, ob_ref):
        north = lax.axis_index("c") == 1
        x1, y1 = lax.axis_index("x") == 1, lax.axis_index("y") == 1
        sums = []
        for chip in range(4):
            sums.append(jnp.where(north, g_ref[chip, 1], g_ref[chip, 0]) + b_ref[chip])
            ob_ref[chip] = sums[-1].astype(BF16)
        own_ref[...] = jnp.where(x1, jnp.where(y1, sums[3], sums[2]), jnp.where(y1, sums[1], sums[0]))

    spec = pl.BlockSpec((4, tr, C), lambda i: (0, i, 0))
    return _pcall(
        body, name=name, grid=(R // tr,),
        in_specs=[pl.BlockSpec((4, 2, tr, C), lambda i: (0, 0, i, 0)), spec],
        out_specs=[pl.BlockSpec((tr, C), lambda i: (i, 0)), spec],
        out_shape=[jax.ShapeDtypeStruct((R, C), F32), jax.ShapeDtypeStruct((4, R, C), BF16)],
        compiler_params=_params("parallel"))(grad.reshape(4, 2, R, C), got)


def _pick8(n, cap):
    if n <= cap:
        return n
    best = None
    for t in range(SUBLANES, cap + 1, SUBLANES):
        if n % t == 0:
            best = t
    assert best is not None, (n, cap)
    return best


def _adamw(w, m, v, parts, *, name):
    R, C = w.shape
    tr = _pick8(R, max(SUBLANES, (1 << 17) // C))
    c1 = 1.0 - ADAM_B1 ** ADAM_STEP
    c2 = 1.0 - ADAM_B2 ** ADAM_STEP
    np_ = len(parts)

    def body(w_ref, m_ref, v_ref, *refs):
        p_refs = refs[:np_]
        g_ref, d_ref, nm_ref, nv_ref = refs[np_:]
        g = None
        for p_ref in p_refs:
            terms = [p_ref[...]] if len(p_ref.shape) == 2 else [p_ref[k] for k in range(p_ref.shape[0])]
            for t in terms:
                g = t.astype(F32) if g is None else g + t.astype(F32)
        mn = ADAM_B1 * m_ref[...] + (1.0 - ADAM_B1) * g
        vn = ADAM_B2 * v_ref[...] + (1.0 - ADAM_B2) * (g * g)
        d_ref[...] = -ADAM_LR * ((mn / c1) / (jnp.sqrt(vn / c2) + ADAM_EPS) + ADAM_WD * w_ref[...])
        g_ref[...] = g
        nm_ref[...] = mn
        nv_ref[...] = vn

    spec = pl.BlockSpec((tr, C), lambda i: (i, 0))
    pspecs = [spec if p.ndim == 2 else pl.BlockSpec((p.shape[0], tr, C), lambda i: (0, i, 0)) for p in parts]
    return _pcall(
        body, name=name, grid=(R // tr,), in_specs=[spec] * 3 + pspecs, out_specs=[spec] * 4,
        out_shape=[jax.ShapeDtypeStruct((R, C), F32)] * 4,
        compiler_params=_params("parallel"))(w, m, v, *parts)


def _rows(a):
    return a.reshape(-1, LANES)


def _whole_from_columns(shards, *, name):
    S, K, n = shards.shape
    tk = _pick8(K, 1024)

    def body(s_ref, o_ref):
        o_ref[...] = s_ref[...]

    return _pcall(
        body, name=name, grid=(K // tk, S),
        in_specs=[pl.BlockSpec((None, tk, n), lambda i, s: (s, i, 0))],
        out_specs=pl.BlockSpec((tk, n), lambda i, s: (i, s)),
        out_shape=jax.ShapeDtypeStruct((K, S * n), shards.dtype),
        compiler_params=_params("parallel", "parallel"))(shards)


def _late_weights(a_w_out_rows, w_kv_cols, b_w_in_cols, b_w_out_rows):
    whole_rows = lambda g: g.reshape(g.shape[0] * g.shape[1], g.shape[2])
    return (whole_rows(a_w_out_rows), _whole_from_columns(w_kv_cols, name="w_kv_whole"),
            _whole_from_columns(b_w_in_cols, name="b_w_in_whole"), whole_rows(b_w_out_rows))


def _forward_backward(xs, target, a_norm, g_a_w_in, conv_w, conv_b, g_w_r, g_w_i, b_r, b_i, lam,
                      kv_norm, b_norm, final_norm, *, late_weights=None, late_shards=None, h_a=None):
    if h_a is None:
        (h_a,) = _rms_fwd(xs, [a_norm], name="a_norm_fwd")
    proj_a = _mm_nn(h_a, g_a_w_in, name="a_in_proj", out_dtype=F32)
    xb, h_rec, yg, *gathered = _acore_fwd(proj_a, conv_w, conv_b, g_w_r, g_w_i, b_r, b_i, lam,
                                          name="a_core_fwd", riders=late_shards or ())
    g_a_w_out, g_w_kv, g_b_w_in, g_b_w_out = _late_weights(*gathered) if late_shards else late_weights
    x1 = _mm_nn(yg, g_a_w_out, name="a_out_proj", out_dtype=F32, res=xs)
    hk, hb = _rms_fwd(x1, [kv_norm, b_norm], name="kv_b_norm_fwd")
    kv = _mm_nn(hk, g_w_kv, name="kv_proj", out_dtype=BF16)
    hd = g_b_w_in.shape[1] // 2
    q = _mm_nn(hb, g_b_w_in, name="q_proj", out_dtype=BF16, col_off=0, cols=hd)
    gate_b = _mm_nn(hb, g_b_w_in, name="b_gate_proj", out_dtype=F32, col_off=hd, cols=hd)
    o, og, ltot = _attn_fwd(q, kv, gate_b, name="attn_fwd")
    x2 = _mm_nn(og, g_b_w_out, name="b_out_proj", out_dtype=F32, res=x1)
    dx2, d_final_norm, loss_part, dx2_b = _final_loss(x2, target, final_norm, name="final_norm_loss")

    dog = _mm_nt(dx2_b, g_b_w_out, name="b_out_proj_bwd")
    dw_b_out = _mm_tn(og, dx2_b, name="b_out_proj_wgrad")
    dproj_b, dkv = _attn_bwd(q, kv, gate_b, o, ltot, dog, name="attn_bwd")
    dhb = _mm_nt(dproj_b, g_b_w_in, name="b_in_proj_bwd")
    dw_b_in = _mm_tn(hb, dproj_b, name="b_in_proj_wgrad", shards=N_DEV)
    dhk = _mm_nt(dkv, g_w_kv, name="kv_proj_bwd")
    dw_kv = _mm_tn(hk, dkv, name="kv_proj_wgrad", shards=N_DEV)
    early = [dw_kv, dw_b_in, dw_b_out.reshape(N_DEV, -1, dw_b_out.shape[1])] if late_shards else []
    dx1, d_b_norm, d_kv_norm, *got, dx1_b = _rms_bwd(
        x1, dx2, [dhb, dhk], [b_norm, kv_norm], name="kv_b_norm_bwd",
        rider=_sibling_rider(early) if early else None, bf16_copy=True)
    early_sums = [_pair_sum(f_, g_, name=f"pair_sum_early_{i}") for i, (f_, g_) in enumerate(zip(early, got))]
    dyg = _mm_nt(dx1_b, g_a_w_out, name="a_out_proj_bwd")
    dw_a_out = _mm_tn(yg, dx1_b, name="a_out_proj_wgrad")
    (dproj_a, d_conv_w, d_conv_b, d_b_r, d_b_i, d_lambda, dw_r, dw_i, *early_others) = _acore_bwd(
        dyg, proj_a, xb, h_rec, conv_w, g_w_r, g_w_i, b_r, b_i, lam, name="a_core_bwd",
        rider=_chips_rider([s[1] for s in early_sums]) if early else None)
    dw_a_in = _mm_tn(h_a, dproj_a, name="a_in_proj_wgrad", shards=N_DEV)
    rows = dw_r.shape[1] // N_DEV
    lru = lambda dw: dw.reshape(-1, N_DEV, rows, dw.shape[2]).transpose(1, 0, 2, 3).reshape(N_DEV, -1, dw.shape[2])
    late = [dw_a_in, dw_a_out.reshape(N_DEV, -1, dw_a_out.shape[1]), lru(dw_r), lru(dw_i)] if late_shards else []
    dh_a, *got = _mm_nt(dproj_a, g_a_w_in, name="a_in_proj_bwd", rider=_sibling_rider(late)) if late else (
        _mm_nt(dproj_a, g_a_w_in, name="a_in_proj_bwd"),)
    late_sums = [_pair_sum(f_, g_, name=f"pair_sum_late_{i}") for i, (f_, g_) in enumerate(zip(late, got))]
    grad_x, d_a_norm, *late_others = _rms_bwd(xs, dx1, [dh_a], [a_norm], name="a_norm_bwd",
                                              rider=_chips_rider([s[1] for s in late_sums]) if late else None)
    sums = late_sums[:2] + early_sums + late_sums[2:]
    others = late_others[:2] + early_others + late_others[2:]
    return (loss_part, grad_x, dw_a_in, dw_a_out, dw_kv, dw_b_in, dw_b_out, dw_r, dw_i, d_a_norm,
            d_conv_w, d_conv_b, d_b_r, d_b_i, d_lambda, d_kv_norm, d_b_norm, d_final_norm, sums, others)


def kernel(x, a_norm, a_w_in, a_conv_w, a_conv_b, a_w_r, a_b_r, a_w_i, a_b_i, a_lambda, a_w_out, kv_norm, w_kv, b_norm, b_w_in, b_w_out, final_norm, loss_target, m_a_norm, m_a_w_in, m_a_conv_w, m_a_conv_b, m_a_w_r, m_a_b_r, m_a_w_i, m_a_b_i, m_a_lambda, m_a_w_out, m_kv_norm, m_w_kv, m_b_norm, m_b_w_in, m_b_w_out, m_final_norm, v_a_norm, v_a_w_in, v_a_conv_w, v_a_conv_b, v_a_w_r, v_a_b_r, v_a_w_i, v_a_b_i, v_a_lambda, v_a_w_out, v_kv_norm, v_w_kv, v_b_norm, v_b_w_in, v_b_w_out, v_final_norm):
    T, D = x.shape[1], x.shape[2]
    nb, bw = a_w_r.shape[1], a_w_r.shape[3]
    C = nb * bw
    me = 4 * lax.axis_index("x") + 2 * lax.axis_index("y") + lax.axis_index("c")
    xs = x[0]
    target = loss_target[0]

    rows_r = a_w_r.shape[2]
    small_f32 = jnp.concatenate([_rows(a_conv_w[0]), _rows(b_norm[0])], axis=0)
    pad = (-small_f32.shape[0]) % SUBLANES
    small_f32 = jnp.pad(small_f32, ((0, pad), (0, 0)))
    h_a, a_w_in_cols, w_r_rows, w_i_rows, small_all = _rms_fwd(
        xs, [a_norm], name="a_norm_fwd",
        riders=[a_w_in[0].astype(BF16), a_w_r[0].reshape(nb * rows_r, bw).astype(BF16),
                a_w_i[0].reshape(nb * rows_r, bw).astype(BF16), small_f32])
    late_shards = [a_w_out[0].astype(BF16), w_kv.astype(BF16), b_w_in[0].astype(BF16), b_w_out[0].astype(BF16)]
    g_a_w_in = _whole_from_columns(a_w_in_cols, name="a_w_in_whole")
    g_w_r = w_r_rows.reshape(N_DEV, nb, rows_r, bw).transpose(1, 0, 2, 3).reshape(nb, bw, bw)
    g_w_i = w_i_rows.reshape(N_DEV, nb, rows_r, bw).transpose(1, 0, 2, 3).reshape(nb, bw, bw)
    cw_rows = a_conv_w.shape[1] * a_conv_w.shape[2] // LANES
    conv_w_full = small_all[:, :cw_rows, :].reshape(N_DEV, CONV_W, a_conv_w.shape[2])
    conv_w_full = conv_w_full.transpose(1, 0, 2).reshape(CONV_W, C)
    bn_rows = b_norm.shape[1] // LANES
    b_norm_full = small_all[:, cw_rows:cw_rows + bn_rows, :].reshape(1, D)
    kv_norm2, final_norm2 = kv_norm.reshape(1, D), final_norm.reshape(1, D)

    (loss_part, grad_x, dw_a_in, dw_a_out, dw_kv, dw_b_in, dw_b_out, dw_r, dw_i, d_a_norm, d_conv_w,
     d_conv_b, d_b_r, d_b_i, d_lambda, d_kv_norm, d_b_norm, d_final_norm, sums,
     others) = _forward_backward(
         xs, target, a_norm, g_a_w_in, conv_w_full, a_conv_b, g_w_r, g_w_i, a_b_r, a_b_i, a_lambda,
         kv_norm2, b_norm_full, final_norm2, late_shards=late_shards, h_a=h_a)

    small_parts = [d_a_norm, d_conv_w, d_conv_b, d_b_r, d_b_i, d_lambda, d_kv_norm, d_b_norm, d_final_norm]
    small_sizes = [p.size // LANES for p in small_parts]
    small = jnp.concatenate([_rows(p) for p in small_parts], axis=0)
    small_everyone = _small_gather(small)

    def shard2d(w):
        return w.reshape(-1, w.shape[-1])

    names_big = [(a_w_in, m_a_w_in, v_a_w_in), (a_w_out, m_a_w_out, v_a_w_out), (w_kv, m_w_kv, v_w_kv),
                 (b_w_in, m_b_w_in, v_b_w_in), (b_w_out, m_b_w_out, v_b_w_out),
                 (a_w_r, m_a_w_r, v_a_w_r), (a_w_i, m_a_w_i, v_a_w_i)]
    upd_big = []
    for i, (w, m, v) in enumerate(names_big):
        res = _adamw(shard2d(w), shard2d(m), shard2d(v), [sums[i][0], others[i]], name=f"adamw_{i}")
        upd_big.append([r.reshape(w.shape) for r in res])

    soffs = [0]
    for s in small_sizes:
        soffs.append(soffs[-1] + s)

    def small_piece(i):
        return small_everyone[:, soffs[i]:soffs[i + 1], :]

    cw_cols = a_conv_w.shape[2]
    conv_piece = small_piece(1).reshape(N_DEV, CONV_W, C)
    conv_piece = lax.dynamic_slice_in_dim(conv_piece, me * cw_cols, cw_cols, axis=2)
    conv_piece = conv_piece.reshape(N_DEV, CONV_W * cw_cols // LANES, LANES)
    bn_piece = lax.dynamic_slice_in_dim(small_piece(7), me * bn_rows, bn_rows, axis=1)
    small_g = jnp.concatenate([small_piece(0), conv_piece, small_piece(2), small_piece(3), small_piece(4),
                               small_piece(5), small_piece(6), bn_piece, small_piece(8)], axis=1)
    small_w = [(a_norm, m_a_norm, v_a_norm), (a_conv_w, m_a_conv_w, v_a_conv_w),
               (a_conv_b, m_a_conv_b, v_a_conv_b), (a_b_r, m_a_b_r, v_a_b_r), (a_b_i, m_a_b_i, v_a_b_i),
               (a_lambda, m_a_lambda, v_a_lambda), (kv_norm, m_kv_norm, v_kv_norm),
               (b_norm, m_b_norm, v_b_norm), (final_norm, m_final_norm, v_final_norm)]
    pack = lambda idx: jnp.concatenate([_rows(t[idx]) for t in small_w], axis=0)
    res_small = _adamw(pack(0), pack(1), pack(2), [small_g], name="adamw_small")
    woffs = [0]
    for t in small_w:
        woffs.append(woffs[-1] + t[0].size // LANES)
    upd_small = [[r[woffs[i]:woffs[i + 1]].reshape(small_w[i][0].shape) for r in res_small]
                 for i in range(len(small_w))]

    order = [("s", 0), ("b", 0), ("s", 1), ("s", 2), ("b", 5), ("s", 3), ("b", 6), ("s", 4), ("s", 5),
             ("b", 1), ("s", 6), ("b", 2), ("s", 7), ("b", 3), ("b", 4), ("s", 8)]
    per_weight = [(upd_big if kind == "b" else upd_small)[i] for kind, i in order]
    loss = lax.psum(loss_part[0, 0], ("x", "y", "c"))
    result = [loss, grad_x[None]]
    for field in range(4):
        result += [u[field] for u in per_weight]
    return tuple(result)
```
